```python
import jax, jax.numpy as jnp
from jax import lax
import numpy as np

D_MODEL = 1024
BATCH = 8
SEQ = 16384
DEPTH = 1

HEAD_DIM = 64
SB_HEADS = 8
DSA_GROUPS = ((128, 1), (512, 4), (2048, 16))
DSA_HEADS_PER_GROUP = 4
DSA_HEADS = DSA_HEADS_PER_GROUP * len(DSA_GROUPS)
MEM_HEADS = 4
MEM_LEN = 256
D_FF = 2816
ROPE_THETA = 10000.0
NORM_EPS = 1e-6
Q_BLOCK = 128
N_BRANCH = 3
SB_W = SB_HEADS * HEAD_DIM
DSA_W = DSA_HEADS * HEAD_DIM
DSA_OUT_W = DSA_HEADS_PER_GROUP * HEAD_DIM
MEM_W = MEM_HEADS * HEAD_DIM
IN_COLS = 3 * SB_W + 3 * DSA_W + MEM_W
MAX_DIL = max(r for _, r in DSA_GROUPS)

kernel_name = "hybrid_stickbreak_dilated_memory_block"

F32 = jnp.float32


def rms_norm(x, g):
    xf = x.astype(F32)
    y = xf * lax.rsqrt(jnp.mean(xf * xf, axis=-1, keepdims=True) + NORM_EPS)
    return (y * g.astype(F32)).astype(x.dtype)


def swiglu(x, w1, w3, w2):
    return (jax.nn.silu(x @ w1) * (x @ w3)) @ w2


def split_heads(t, n):
    b, s, _ = t.shape
    return t.reshape(b, s, n, HEAD_DIM).transpose(0, 2, 1, 3)


def merge_heads(t):
    b, n, s, hd = t.shape
    return t.transpose(0, 2, 1, 3).reshape(b, s, n * hd)


def rope(x, positions):
    half = HEAD_DIM // 2
    inv_freq = jnp.power(ROPE_THETA, -jnp.arange(half, dtype=F32) / half)
    ang = positions.astype(F32)[:, None] * inv_freq[None, :]
    cos, sin = jnp.cos(ang), jnp.sin(ang)
    xf = x.astype(F32)
    x1, x2 = xf[..., :half], xf[..., half:]
    return jnp.concatenate([x1 * cos - x2 * sin, x2 * cos + x1 * sin], axis=-1).astype(x.dtype)


def stick_breaking_attention(q, k, v):
    b, h, s, hd = q.shape
    nb = s // Q_BLOCK
    scale = hd ** -0.5
    qb = q.reshape(b, h, nb, Q_BLOCK, hd).transpose(2, 0, 1, 3, 4)
    key_pos = jnp.arange(s)
    vf = v.astype(F32)

    def block(args):
        qi, bi = args
        z = jnp.einsum('bhqd,bhkd->bhqk', qi, k).astype(F32) * scale
        q_pos = bi * Q_BLOCK + jnp.arange(Q_BLOCK)
        before = key_pos[None, :] < q_pos[:, None]
        log_fail = jnp.where(before, jax.nn.log_sigmoid(-z), 0.0)
        later = lax.cumsum(log_fail, axis=3, reverse=True) - log_fail
        w = jnp.where(before, jnp.exp(jax.nn.log_sigmoid(z) + later), 0.0)
        return jnp.einsum('bhqk,bhkd->bhqd', w, vf)

    out = lax.map(block, (qb, jnp.arange(nb)))
    return out.transpose(1, 2, 0, 3, 4).reshape(b, h, s, hd).astype(q.dtype)


def banded_window_attention(q, k, v, n_back):
    *lead, n, hd = q.shape
    nb = n // Q_BLOCK
    scale = hd ** -0.5
    qb = q.reshape(*lead, nb, Q_BLOCK, hd)

    def with_prev(t):
        tp = jnp.concatenate([jnp.zeros_like(t[..., :Q_BLOCK, :]), t], axis=-2)
        tp = tp.reshape(*lead, nb + 1, Q_BLOCK, hd)
        return jnp.concatenate([tp[..., :-1, :, :], tp[..., 1:, :, :]], axis=-2)

    kb, vb = with_prev(k), with_prev(v)
    sc = jnp.einsum('...qd,...kd->...qk', qb, kb).astype(F32) * scale
    qi = jnp.arange(Q_BLOCK)[:, None]
    kj = jnp.arange(2 * Q_BLOCK)[None, :]
    dist = Q_BLOCK + qi - kj
    blk = jnp.arange(nb)[:, None, None]
    valid = (dist >= 0) & (dist <= n_back) & ((blk > 0) | (kj >= Q_BLOCK))
    sc = jnp.where(valid, sc, -jnp.inf)
    m = jnp.max(sc, axis=-1, keepdims=True)
    p = jnp.exp(sc - m)
    den = jnp.sum(p, axis=-1, keepdims=True)
    out = jnp.einsum('...qk,...kd->...qd', p, vb.astype(F32)) / den
    lse = (m + jnp.log(den))[..., 0]
    return out.reshape(*lead, n, hd), lse.reshape(*lead, n)


def dilated_mixture_attention(q, k, v):
    b, _, s, hd = q.shape
    unit = Q_BLOCK * MAX_DIL
    sp = ((s + unit - 1) // unit) * unit
    pad = ((0, 0), (0, 0), (0, sp - s), (0, 0))
    q, k, v = jnp.pad(q, pad), jnp.pad(k, pad), jnp.pad(v, pad)
    outs, lses = [], []
    for g, (window, dil) in enumerate(DSA_GROUPS):
        sl = slice(g * DSA_HEADS_PER_GROUP, (g + 1) * DSA_HEADS_PER_GROUP)

        def stride_gather(t):
            return t[:, sl].reshape(b, DSA_HEADS_PER_GROUP, sp // dil, dil, hd).swapaxes(2, 3)

        o, l = banded_window_attention(stride_gather(q), stride_gather(k), stride_gather(v), window // dil)
        outs.append(o.swapaxes(2, 3).reshape(b, DSA_HEADS_PER_GROUP, sp, hd))
        lses.append(l.swapaxes(2, 3).reshape(b, DSA_HEADS_PER_GROUP, sp))
    alpha = jax.nn.softmax(jnp.stack(lses, axis=0), axis=0)
    o = jnp.sum(alpha[..., None] * jnp.stack(outs, axis=0), axis=0)
    return o[:, :, :s].astype(q.dtype)


def memory_cross_attention(q, mem_h, w_mem_kv, qn, kn):
    kv = mem_h @ w_mem_kv
    km, vm = jnp.split(kv, 2, axis=-1)
    km = rms_norm(split_heads(km, MEM_HEADS), kn)
    vm = split_heads(vm, MEM_HEADS)
    q = rms_norm(q, qn)
    sc = jnp.einsum('bhqd,bhkd->bhqk', q, km).astype(F32) * (HEAD_DIM ** -0.5)
    p = jax.nn.softmax(sc, axis=-1)
    return jnp.einsum('bhqk,bhkd->bhqd', p, vm.astype(F32)).astype(q.dtype)


def _fwd_setup_inputs(seed: int = 0) -> dict:
    key = jax.random.key(seed)
    ks = iter(jax.random.split(key, 32))

    def w(shape, fan_in):
        return jax.random.normal(next(ks), (DEPTH,) + shape, F32) * (fan_in ** -0.5)

    def gain(shape):
        return 1.0 + 0.02 * jax.random.normal(next(ks), (DEPTH,) + shape, F32)

    return {
        "x": jax.random.normal(next(ks), (BATCH, SEQ, D_MODEL), F32),
        "mem": jax.random.normal(next(ks), (BATCH, MEM_LEN, D_MODEL), F32),
        "ffn1_norm": gain((D_MODEL,)),
        "ffn1_w1": w((D_MODEL, D_FF), D_MODEL),
        "ffn1_w3": w((D_MODEL, D_FF), D_MODEL),
        "ffn1_w2": w((D_FF, D_MODEL), D_FF),
        "mix_norm": gain((D_MODEL,)),
        "mem_norm": gain((D_MODEL,)),
        "w_in": w((D_MODEL, IN_COLS), D_MODEL),
        "w_mem_kv": w((D_MODEL, 2 * MEM_W), D_MODEL),
        "qn_dsa": gain((HEAD_DIM,)),
        "kn_dsa": gain((HEAD_DIM,)),
        "qn_mem": gain((HEAD_DIM,)),
        "kn_mem": gain((HEAD_DIM,)),
        "w_branch_sb": w((SB_W, D_MODEL), SB_W),
        "w_branch_dsa": w((DSA_OUT_W, D_MODEL), DSA_OUT_W),
        "w_branch_mem": w((MEM_W, D_MODEL), MEM_W),
        "w_gate": w((D_MODEL, N_BRANCH * D_MODEL), D_MODEL),
        "b_gate": 0.01 * jax.random.normal(next(ks), (DEPTH, N_BRANCH * D_MODEL), F32),
        "w_out": w((D_MODEL, D_MODEL), D_MODEL),
        "ffn2_norm": gain((D_MODEL,)),
        "ffn2_w1": w((D_MODEL, D_FF), D_MODEL),
        "ffn2_w3": w((D_MODEL, D_FF), D_MODEL),
        "ffn2_w2": w((D_FF, D_MODEL), D_FF),
    }


def _fwd_reference(x, mem, ffn1_norm, ffn1_w1, ffn1_w3, ffn1_w2, mix_norm, mem_norm, w_in, w_mem_kv,
              qn_dsa, kn_dsa, qn_mem, kn_mem, w_branch_sb, w_branch_dsa, w_branch_mem,
              w_gate, b_gate, w_out, ffn2_norm, ffn2_w1, ffn2_w3, ffn2_w2):
    b, s, d = x.shape
    positions = jnp.arange(s)
    cuts = np.cumsum([SB_W, SB_W, SB_W, DSA_W, DSA_W, DSA_W])
    for l in range(DEPTH):
        x = x + 0.5 * swiglu(rms_norm(x, ffn1_norm[l]), ffn1_w1[l], ffn1_w3[l], ffn1_w2[l])

        h = rms_norm(x, mix_norm[l])
        qa, ka, va, qb, kb, vb, qc = jnp.split(h @ w_in[l], cuts, axis=-1)

        ya = stick_breaking_attention(split_heads(qa, SB_HEADS), split_heads(ka, SB_HEADS),
                                      split_heads(va, SB_HEADS))
        ya = merge_heads(ya) @ w_branch_sb[l]

        qb_h = rope(rms_norm(split_heads(qb, DSA_HEADS), qn_dsa[l]), positions)
        kb_h = rope(rms_norm(split_heads(kb, DSA_HEADS), kn_dsa[l]), positions)
        yb = dilated_mixture_attention(qb_h, kb_h, split_heads(vb, DSA_HEADS))
        yb = merge_heads(yb) @ w_branch_dsa[l]

        yc = memory_cross_attention(split_heads(qc, MEM_HEADS), rms_norm(mem, mem_norm[l]),
                                    w_mem_kv[l], qn_mem[l], kn_mem[l])
        yc = merge_heads(yc) @ w_branch_mem[l]

        gates = jax.nn.sigmoid(h @ w_gate[l] + b_gate[l]).reshape(b, s, N_BRANCH, d)
        merged = gates[:, :, 0] * ya + gates[:, :, 1] * yb + gates[:, :, 2] * yc
        x = x + merged @ w_out[l]

        x = x + 0.5 * swiglu(rms_norm(x, ffn2_norm[l]), ffn2_w1[l], ffn2_w3[l], ffn2_w2[l])
    return x


import jax as _jax
import jax.numpy as _jnp

TWIN_FORMAT = 'train_step'
FWD_PARAMS = ['x', 'mem', 'ffn1_norm', 'ffn1_w1', 'ffn1_w3', 'ffn1_w2', 'mix_norm', 'mem_norm', 'w_in', 'w_mem_kv', 'qn_dsa', 'kn_dsa', 'qn_mem', 'kn_mem', 'w_branch_sb', 'w_branch_dsa', 'w_branch_mem', 'w_gate', 'b_gate', 'w_out', 'ffn2_norm', 'ffn2_w1', 'ffn2_w3', 'ffn2_w2']
TWIN_WEIGHTS = ['ffn1_norm', 'ffn1_w1', 'ffn1_w3', 'ffn1_w2', 'mix_norm', 'mem_norm', 'w_in', 'w_mem_kv', 'qn_dsa', 'kn_dsa', 'qn_mem', 'kn_mem', 'w_branch_sb', 'w_branch_dsa', 'w_branch_mem', 'w_gate', 'b_gate', 'w_out', 'ffn2_norm', 'ffn2_w1', 'ffn2_w3', 'ffn2_w2']
TWIN_DIFF_INPUT = 'x'
TWIN_INPUTS = ['x', 'mem', 'ffn1_norm', 'ffn1_w1', 'ffn1_w3', 'ffn1_w2', 'mix_norm', 'mem_norm', 'w_in', 'w_mem_kv', 'qn_dsa', 'kn_dsa', 'qn_mem', 'kn_mem', 'w_branch_sb', 'w_branch_dsa', 'w_branch_mem', 'w_gate', 'b_gate', 'w_out', 'ffn2_norm', 'ffn2_w1', 'ffn2_w3', 'ffn2_w2', 'loss_target', 'm_ffn1_norm', 'm_ffn1_w1', 'm_ffn1_w3', 'm_ffn1_w2', 'm_mix_norm', 'm_mem_norm', 'm_w_in', 'm_w_mem_kv', 'm_qn_dsa', 'm_kn_dsa', 'm_qn_mem', 'm_kn_mem', 'm_w_branch_sb', 'm_w_branch_dsa', 'm_w_branch_mem', 'm_w_gate', 'm_b_gate', 'm_w_out', 'm_ffn2_norm', 'm_ffn2_w1', 'm_ffn2_w3', 'm_ffn2_w2', 'v_ffn1_norm', 'v_ffn1_w1', 'v_ffn1_w3', 'v_ffn1_w2', 'v_mix_norm', 'v_mem_norm', 'v_w_in', 'v_w_mem_kv', 'v_qn_dsa', 'v_kn_dsa', 'v_qn_mem', 'v_kn_mem', 'v_w_branch_sb', 'v_w_branch_dsa', 'v_w_branch_mem', 'v_w_gate', 'v_b_gate', 'v_w_out', 'v_ffn2_norm', 'v_ffn2_w1', 'v_ffn2_w3', 'v_ffn2_w2']
TWIN_OUTPUTS = ['loss', 'grad_x', 'grad_ffn1_norm', 'grad_ffn1_w1', 'grad_ffn1_w3', 'grad_ffn1_w2', 'grad_mix_norm', 'grad_mem_norm', 'grad_w_in', 'grad_w_mem_kv', 'grad_qn_dsa', 'grad_kn_dsa', 'grad_qn_mem', 'grad_kn_mem', 'grad_w_branch_sb', 'grad_w_branch_dsa', 'grad_w_branch_mem', 'grad_w_gate', 'grad_b_gate', 'grad_w_out', 'grad_ffn2_norm', 'grad_ffn2_w1', 'grad_ffn2_w3', 'grad_ffn2_w2', 'delta_ffn1_norm', 'delta_ffn1_w1', 'delta_ffn1_w3', 'delta_ffn1_w2', 'delta_mix_norm', 'delta_mem_norm', 'delta_w_in', 'delta_w_mem_kv', 'delta_qn_dsa', 'delta_kn_dsa', 'delta_qn_mem', 'delta_kn_mem', 'delta_w_branch_sb', 'delta_w_branch_dsa', 'delta_w_branch_mem', 'delta_w_gate', 'delta_b_gate', 'delta_w_out', 'delta_ffn2_norm', 'delta_ffn2_w1', 'delta_ffn2_w3', 'delta_ffn2_w2', 'new_m_ffn1_norm', 'new_m_ffn1_w1', 'new_m_ffn1_w3', 'new_m_ffn1_w2', 'new_m_mix_norm', 'new_m_mem_norm', 'new_m_w_in', 'new_m_w_mem_kv', 'new_m_qn_dsa', 'new_m_kn_dsa', 'new_m_qn_mem', 'new_m_kn_mem', 'new_m_w_branch_sb', 'new_m_w_branch_dsa', 'new_m_w_branch_mem', 'new_m_w_gate', 'new_m_b_gate', 'new_m_w_out', 'new_m_ffn2_norm', 'new_m_ffn2_w1', 'new_m_ffn2_w3', 'new_m_ffn2_w2', 'new_v_ffn1_norm', 'new_v_ffn1_w1', 'new_v_ffn1_w3', 'new_v_ffn1_w2', 'new_v_mix_norm', 'new_v_mem_norm', 'new_v_w_in', 'new_v_w_mem_kv', 'new_v_qn_dsa', 'new_v_kn_dsa', 'new_v_qn_mem', 'new_v_kn_mem', 'new_v_w_branch_sb', 'new_v_w_branch_dsa', 'new_v_w_branch_mem', 'new_v_w_gate', 'new_v_b_gate', 'new_v_w_out', 'new_v_ffn2_norm', 'new_v_ffn2_w1', 'new_v_ffn2_w3', 'new_v_ffn2_w2']
TWIN_LEAF_KINDS = {'loss': 'loss', 'grad_x': 'grad_x', 'grad_ffn1_norm': 'grad_w', 'grad_ffn1_w1': 'grad_w', 'grad_ffn1_w3': 'grad_w', 'grad_ffn1_w2': 'grad_w', 'grad_mix_norm': 'grad_w', 'grad_mem_norm': 'grad_w', 'grad_w_in': 'grad_w', 'grad_w_mem_kv': 'grad_w', 'grad_qn_dsa': 'grad_w', 'grad_kn_dsa': 'grad_w', 'grad_qn_mem': 'grad_w', 'grad_kn_mem': 'grad_w', 'grad_w_branch_sb': 'grad_w', 'grad_w_branch_dsa': 'grad_w', 'grad_w_branch_mem': 'grad_w', 'grad_w_gate': 'grad_w', 'grad_b_gate': 'grad_w', 'grad_w_out': 'grad_w', 'grad_ffn2_norm': 'grad_w', 'grad_ffn2_w1': 'grad_w', 'grad_ffn2_w3': 'grad_w', 'grad_ffn2_w2': 'grad_w', 'delta_ffn1_norm': 'delta_w', 'delta_ffn1_w1': 'delta_w', 'delta_ffn1_w3': 'delta_w', 'delta_ffn1_w2': 'delta_w', 'delta_mix_norm': 'delta_w', 'delta_mem_norm': 'delta_w', 'delta_w_in': 'delta_w', 'delta_w_mem_kv': 'delta_w', 'delta_qn_dsa': 'delta_w', 'delta_kn_dsa': 'delta_w', 'delta_qn_mem': 'delta_w', 'delta_kn_mem': 'delta_w', 'delta_w_branch_sb': 'delta_w', 'delta_w_branch_dsa': 'delta_w', 'delta_w_branch_mem': 'delta_w', 'delta_w_gate': 'delta_w', 'delta_b_gate': 'delta_w', 'delta_w_out': 'delta_w', 'delta_ffn2_norm': 'delta_w', 'delta_ffn2_w1': 'delta_w', 'delta_ffn2_w3': 'delta_w', 'delta_ffn2_w2': 'delta_w', 'new_m_ffn1_norm': 'new_m', 'new_m_ffn1_w1': 'new_m', 'new_m_ffn1_w3': 'new_m', 'new_m_ffn1_w2': 'new_m', 'new_m_mix_norm': 'new_m', 'new_m_mem_norm': 'new_m', 'new_m_w_in': 'new_m', 'new_m_w_mem_kv': 'new_m', 'new_m_qn_dsa': 'new_m', 'new_m_kn_dsa': 'new_m', 'new_m_qn_mem': 'new_m', 'new_m_kn_mem': 'new_m', 'new_m_w_branch_sb': 'new_m', 'new_m_w_branch_dsa': 'new_m', 'new_m_w_branch_mem': 'new_m', 'new_m_w_gate': 'new_m', 'new_m_b_gate': 'new_m', 'new_m_w_out': 'new_m', 'new_m_ffn2_norm': 'new_m', 'new_m_ffn2_w1': 'new_m', 'new_m_ffn2_w3': 'new_m', 'new_m_ffn2_w2': 'new_m', 'new_v_ffn1_norm': 'new_v', 'new_v_ffn1_w1': 'new_v', 'new_v_ffn1_w3': 'new_v', 'new_v_ffn1_w2': 'new_v', 'new_v_mix_norm': 'new_v', 'new_v_mem_norm': 'new_v', 'new_v_w_in': 'new_v', 'new_v_w_mem_kv': 'new_v', 'new_v_qn_dsa': 'new_v', 'new_v_kn_dsa': 'new_v', 'new_v_qn_mem': 'new_v', 'new_v_kn_mem': 'new_v', 'new_v_w_branch_sb': 'new_v', 'new_v_w_branch_dsa': 'new_v', 'new_v_w_branch_mem': 'new_v', 'new_v_w_gate': 'new_v', 'new_v_b_gate': 'new_v', 'new_v_w_out': 'new_v', 'new_v_ffn2_norm': 'new_v', 'new_v_ffn2_w1': 'new_v', 'new_v_ffn2_w3': 'new_v', 'new_v_ffn2_w2': 'new_v'}


def _forward(args):
    return _fwd_reference(*[args[k] for k in FWD_PARAMS])


def _output_shape():
    def fwd():
        inp = _fwd_setup_inputs(0)
        return _fwd_reference(*[inp[k] for k in FWD_PARAMS])
    out = _jax.eval_shape(fwd)
    return out.shape, out.dtype

N_MICROBATCH = 1
ADAM_LR = 0.001
ADAM_B1 = 0.9
ADAM_B2 = 0.999
ADAM_EPS = 1e-08
ADAM_WD = 0.01
ADAM_STEP = 10
PER_EXAMPLE_BATCH_AXIS = {'x': 0, 'mem': 0, 'loss_target': 0}
SHARED_INPUTS = []
_WEIGHT_DTYPES = {'ffn1_norm': _jnp.float32, 'ffn1_w1': _jnp.float32, 'ffn1_w3': _jnp.float32, 'ffn1_w2': _jnp.float32, 'mix_norm': _jnp.float32, 'mem_norm': _jnp.float32, 'w_in': _jnp.float32, 'w_mem_kv': _jnp.float32, 'qn_dsa': _jnp.float32, 'kn_dsa': _jnp.float32, 'qn_mem': _jnp.float32, 'kn_mem': _jnp.float32, 'w_branch_sb': _jnp.float32, 'w_branch_dsa': _jnp.float32, 'w_branch_mem': _jnp.float32, 'w_gate': _jnp.float32, 'b_gate': _jnp.float32, 'w_out': _jnp.float32, 'ffn2_norm': _jnp.float32, 'ffn2_w1': _jnp.float32, 'ffn2_w3': _jnp.float32, 'ffn2_w2': _jnp.float32}
MOMENT_SCALE = {'ffn1_norm': 2.416389e+01, 'ffn1_w1': 1.399564e-01, 'ffn1_w3': 1.749849e-01, 'ffn1_w2': 2.870262e-01, 'mix_norm': 2.130608e+01, 'mem_norm': 3.271624e-01, 'w_in': 2.461417e-01, 'w_mem_kv': 1.411130e-01, 'qn_dsa': 1.908911e+00, 'kn_dsa': 1.906515e+00, 'qn_mem': 5.183471e+00, 'kn_mem': 5.186107e+00, 'w_branch_sb': 6.205468e-01, 'w_branch_dsa': 6.573110e-02, 'w_branch_mem': 6.600359e-02, 'w_gate': 5.146501e-02, 'b_gate': 3.175718e+00, 'w_out': 4.770192e-01, 'ffn2_norm': 2.470990e+01, 'ffn2_w1': 1.251535e-01, 'ffn2_w3': 1.784950e-01, 'ffn2_w2': 2.930143e-01}


def _to_microbatches(a, axis):
    t = _jnp.moveaxis(a, axis, 0)
    t = t.reshape((N_MICROBATCH, t.shape[0] // N_MICROBATCH) + t.shape[1:])
    return _jnp.moveaxis(t, 1, axis + 1)


def setup_inputs(seed: int = 0) -> dict:
    inp = _fwd_setup_inputs(seed)
    key = _jax.random.fold_in(_jax.random.key(seed), 7919)
    shape, _ = _output_shape()
    out = dict(inp)
    out["loss_target"] = _jax.random.normal(_jax.random.fold_in(key, 0), shape, _jnp.float32)
    for i, name in enumerate(TWIN_WEIGHTS):
        w = inp[name].astype(_jnp.float32)
        if MOMENT_SCALE is None:
            s = _jnp.sqrt(_jnp.mean(_jnp.square(w)) + 1e-30)
        else:
            s = MOMENT_SCALE[name]
        km, kv = _jax.random.split(_jax.random.fold_in(key, i + 1))
        out[name] = w
        out["m_" + name] = s * _jax.random.normal(km, w.shape, _jnp.float32)
        out["v_" + name] = (s * s) * _jax.random.uniform(kv, w.shape, _jnp.float32, 0.5, 1.5)
    if N_MICROBATCH > 1:
        for name, axis in PER_EXAMPLE_BATCH_AXIS.items():
            out[name] = _to_microbatches(out[name], axis)
    return {'x': out['x'], 'mem': out['mem'], 'ffn1_norm': out['ffn1_norm'], 'ffn1_w1': out['ffn1_w1'], 'ffn1_w3': out['ffn1_w3'], 'ffn1_w2': out['ffn1_w2'], 'mix_norm': out['mix_norm'], 'mem_norm': out['mem_norm'], 'w_in': out['w_in'], 'w_mem_kv': out['w_mem_kv'], 'qn_dsa': out['qn_dsa'], 'kn_dsa': out['kn_dsa'], 'qn_mem': out['qn_mem'], 'kn_mem': out['kn_mem'], 'w_branch_sb': out['w_branch_sb'], 'w_branch_dsa': out['w_branch_dsa'], 'w_branch_mem': out['w_branch_mem'], 'w_gate': out['w_gate'], 'b_gate': out['b_gate'], 'w_out': out['w_out'], 'ffn2_norm': out['ffn2_norm'], 'ffn2_w1': out['ffn2_w1'], 'ffn2_w3': out['ffn2_w3'], 'ffn2_w2': out['ffn2_w2'], 'loss_target': out['loss_target'], 'm_ffn1_norm': out['m_ffn1_norm'], 'm_ffn1_w1': out['m_ffn1_w1'], 'm_ffn1_w3': out['m_ffn1_w3'], 'm_ffn1_w2': out['m_ffn1_w2'], 'm_mix_norm': out['m_mix_norm'], 'm_mem_norm': out['m_mem_norm'], 'm_w_in': out['m_w_in'], 'm_w_mem_kv': out['m_w_mem_kv'], 'm_qn_dsa': out['m_qn_dsa'], 'm_kn_dsa': out['m_kn_dsa'], 'm_qn_mem': out['m_qn_mem'], 'm_kn_mem': out['m_kn_mem'], 'm_w_branch_sb': out['m_w_branch_sb'], 'm_w_branch_dsa': out['m_w_branch_dsa'], 'm_w_branch_mem': out['m_w_branch_mem'], 'm_w_gate': out['m_w_gate'], 'm_b_gate': out['m_b_gate'], 'm_w_out': out['m_w_out'], 'm_ffn2_norm': out['m_ffn2_norm'], 'm_ffn2_w1': out['m_ffn2_w1'], 'm_ffn2_w3': out['m_ffn2_w3'], 'm_ffn2_w2': out['m_ffn2_w2'], 'v_ffn1_norm': out['v_ffn1_norm'], 'v_ffn1_w1': out['v_ffn1_w1'], 'v_ffn1_w3': out['v_ffn1_w3'], 'v_ffn1_w2': out['v_ffn1_w2'], 'v_mix_norm': out['v_mix_norm'], 'v_mem_norm': out['v_mem_norm'], 'v_w_in': out['v_w_in'], 'v_w_mem_kv': out['v_w_mem_kv'], 'v_qn_dsa': out['v_qn_dsa'], 'v_kn_dsa': out['v_kn_dsa'], 'v_qn_mem': out['v_qn_mem'], 'v_kn_mem': out['v_kn_mem'], 'v_w_branch_sb': out['v_w_branch_sb'], 'v_w_branch_dsa': out['v_w_branch_dsa'], 'v_w_branch_mem': out['v_w_branch_mem'], 'v_w_gate': out['v_w_gate'], 'v_b_gate': out['v_b_gate'], 'v_w_out': out['v_w_out'], 'v_ffn2_norm': out['v_ffn2_norm'], 'v_ffn2_w1': out['v_ffn2_w1'], 'v_ffn2_w3': out['v_ffn2_w3'], 'v_ffn2_w2': out['v_ffn2_w2']}


def _loss(weights, diff, rest, loss_target):
    with _jax.named_scope("forward"):
        args = {**rest, TWIN_DIFF_INPUT: diff, **{k: w.astype(_WEIGHT_DTYPES[k]) for k, w in weights.items()}}
        y = _forward(args)
    with _jax.named_scope("loss_head"):
        err = _jnp.square(y.astype(_jnp.float32) - loss_target)
        return 0.5 * _jnp.sum(_jnp.mean(err, axis=-1)) if err.ndim else 0.5 * err


def _adamw(w, g, m, v):
    m = ADAM_B1 * m + (1.0 - ADAM_B1) * g
    v = ADAM_B2 * v + (1.0 - ADAM_B2) * _jnp.square(g)
    m_hat = m / (1.0 - ADAM_B1 ** ADAM_STEP)
    v_hat = v / (1.0 - ADAM_B2 ** ADAM_STEP)
    delta = -ADAM_LR * (m_hat / (_jnp.sqrt(v_hat) + ADAM_EPS) + ADAM_WD * w)
    return delta, m, v


def reference(x, mem, ffn1_norm, ffn1_w1, ffn1_w3, ffn1_w2, mix_norm, mem_norm, w_in, w_mem_kv, qn_dsa, kn_dsa, qn_mem, kn_mem, w_branch_sb, w_branch_dsa, w_branch_mem, w_gate, b_gate, w_out, ffn2_norm, ffn2_w1, ffn2_w3, ffn2_w2, loss_target, m_ffn1_norm, m_ffn1_w1, m_ffn1_w3, m_ffn1_w2, m_mix_norm, m_mem_norm, m_w_in, m_w_mem_kv, m_qn_dsa, m_kn_dsa, m_qn_mem, m_kn_mem, m_w_branch_sb, m_w_branch_dsa, m_w_branch_mem, m_w_gate, m_b_gate, m_w_out, m_ffn2_norm, m_ffn2_w1, m_ffn2_w3, m_ffn2_w2, v_ffn1_norm, v_ffn1_w1, v_ffn1_w3, v_ffn1_w2, v_mix_norm, v_mem_norm, v_w_in, v_w_mem_kv, v_qn_dsa, v_kn_dsa, v_qn_mem, v_kn_mem, v_w_branch_sb, v_w_branch_dsa, v_w_branch_mem, v_w_gate, v_b_gate, v_w_out, v_ffn2_norm, v_ffn2_w1, v_ffn2_w3, v_ffn2_w2):
    given = dict(x=x, mem=mem, ffn1_norm=ffn1_norm, ffn1_w1=ffn1_w1, ffn1_w3=ffn1_w3, ffn1_w2=ffn1_w2, mix_norm=mix_norm, mem_norm=mem_norm, w_in=w_in, w_mem_kv=w_mem_kv, qn_dsa=qn_dsa, kn_dsa=kn_dsa, qn_mem=qn_mem, kn_mem=kn_mem, w_branch_sb=w_branch_sb, w_branch_dsa=w_branch_dsa, w_branch_mem=w_branch_mem, w_gate=w_gate, b_gate=b_gate, w_out=w_out, ffn2_norm=ffn2_norm, ffn2_w1=ffn2_w1, ffn2_w3=ffn2_w3, ffn2_w2=ffn2_w2, loss_target=loss_target, m_ffn1_norm=m_ffn1_norm, m_ffn1_w1=m_ffn1_w1, m_ffn1_w3=m_ffn1_w3, m_ffn1_w2=m_ffn1_w2, m_mix_norm=m_mix_norm, m_mem_norm=m_mem_norm, m_w_in=m_w_in, m_w_mem_kv=m_w_mem_kv, m_qn_dsa=m_qn_dsa, m_kn_dsa=m_kn_dsa, m_qn_mem=m_qn_mem, m_kn_mem=m_kn_mem, m_w_branch_sb=m_w_branch_sb, m_w_branch_dsa=m_w_branch_dsa, m_w_branch_mem=m_w_branch_mem, m_w_gate=m_w_gate, m_b_gate=m_b_gate, m_w_out=m_w_out, m_ffn2_norm=m_ffn2_norm, m_ffn2_w1=m_ffn2_w1, m_ffn2_w3=m_ffn2_w3, m_ffn2_w2=m_ffn2_w2, v_ffn1_norm=v_ffn1_norm, v_ffn1_w1=v_ffn1_w1, v_ffn1_w3=v_ffn1_w3, v_ffn1_w2=v_ffn1_w2, v_mix_norm=v_mix_norm, v_mem_norm=v_mem_norm, v_w_in=v_w_in, v_w_mem_kv=v_w_mem_kv, v_qn_dsa=v_qn_dsa, v_kn_dsa=v_kn_dsa, v_qn_mem=v_qn_mem, v_kn_mem=v_kn_mem, v_w_branch_sb=v_w_branch_sb, v_w_branch_dsa=v_w_branch_dsa, v_w_branch_mem=v_w_branch_mem, v_w_gate=v_w_gate, v_b_gate=v_b_gate, v_w_out=v_w_out, v_ffn2_norm=v_ffn2_norm, v_ffn2_w1=v_ffn2_w1, v_ffn2_w3=v_ffn2_w3, v_ffn2_w2=v_ffn2_w2)
    weights = {n: given[n] for n in TWIN_WEIGHTS}
    shared = {n: given[n] for n in SHARED_INPUTS}
    per_example = {n: given[n] for n in ['x', 'mem']}
    grad_fn = _jax.value_and_grad(_loss, argnums=(0, 1))

    def one_microbatch(ex, loss_target):
        ex = dict(ex)
        diff = ex.pop(TWIN_DIFF_INPUT)
        return grad_fn(weights, diff, {**shared, **ex}, loss_target)

    if N_MICROBATCH == 1:
        loss, (grad_w, grad_x) = one_microbatch(per_example, given["loss_target"])
    else:
        def body(carry, xs):
            loss_sum, grad_sum = carry
            l_k, (gw_k, gx_k) = one_microbatch(xs[0], xs[1])
            with _jax.named_scope("update"):
                return (loss_sum + l_k, _jax.tree.map(_jnp.add, grad_sum, gw_k)), gx_k

        init = (_jnp.zeros((), _jnp.float32), _jax.tree.map(_jnp.zeros_like, weights))
        (loss, grad_w), grad_x = _jax.lax.scan(body, init, (per_example, given["loss_target"]))
    with _jax.named_scope("update"):
        delta_w, new_m, new_v = {}, {}, {}
        for n in TWIN_WEIGHTS:
            delta_w[n], new_m[n], new_v[n] = _adamw(weights[n], grad_w[n], given["m_" + n], given["v_" + n])
    return (loss, grad_x, *[grad_w[n] for n in TWIN_WEIGHTS], *[delta_w[n] for n in TWIN_WEIGHTS],
            *[new_m[n] for n in TWIN_WEIGHTS], *[new_v[n] for n in TWIN_WEIGHTS])
```

```python
import functools

import numpy as np
import jax
import jax.numpy as jnp
from jax import lax
from jax.experimental import pallas as pl
from jax.experimental.pallas import tpu as pltpu

F32, BF16 = jnp.float32, jnp.bfloat16
SDS = jax.ShapeDtypeStruct
MESH = pl.DeviceIdType.MESH

D = 1024
HD = 64
QB = 128
D_FF = 2816
SB_W, DSA_W, DSA_OUT_W, MEM_W = 512, 768, 256, 256
DSA_DILS = (1, 4, 16)
MEM_LEN = 256
N_CHIPS = 4
EPS = 1e-6
SCALE = HD ** -0.5
EXHAUSTED = -104.0
NEG = -1e30
VMEM_LIMIT = 56 * 1024 * 1024

ADAM_LR, ADAM_B1, ADAM_B2, ADAM_EPS, ADAM_WD, ADAM_STEP = 0.001, 0.9, 0.999, 1e-08, 0.01, 10

NN = (((1,), (0,)), ((), ()))
NT = (((1,), (1,)), ((), ()))
TN = (((0,), (0,)), ((), ()))

SHARDED = (
    ("ffn1_w1", (D, D_FF), 1), ("ffn1_w3", (D, D_FF), 1), ("ffn1_w2", (D_FF, D), 0),
    ("w_in", (D, 4096), 1), ("w_mem_kv", (D, 512), 0),
    ("w_branch_sb", (SB_W, D), 1), ("w_branch_dsa", (DSA_OUT_W, D), 1), ("w_branch_mem", (MEM_W, D), 1),
    ("w_gate", (D, 3 * D), 1), ("w_out", (D, D), 0),
    ("ffn2_w1", (D, D_FF), 1), ("ffn2_w3", (D, D_FF), 1), ("ffn2_w2", (D_FF, D), 0),
)
SMALL = (("ffn1_norm", D), ("mix_norm", D), ("mem_norm", D), ("ffn2_norm", D), ("b_gate", 3 * D),
         ("qn_dsa", HD), ("kn_dsa", HD), ("qn_mem", HD), ("kn_mem", HD))
WEIGHTS = ("ffn1_norm", "ffn1_w1", "ffn1_w3", "ffn1_w2", "mix_norm", "mem_norm", "w_in", "w_mem_kv", "qn_dsa", "kn_dsa",
           "qn_mem", "kn_mem", "w_branch_sb", "w_branch_dsa", "w_branch_mem", "w_gate", "b_gate", "w_out", "ffn2_norm",
           "ffn2_w1", "ffn2_w3", "ffn2_w2")
SMALL_ROWS = 8


def _dot(a, b, dn=NN):
    return lax.dot_general(a, b, dn, preferred_element_type=F32)


def _dot01(x, m01):
    hi = x.astype(BF16)
    r1 = x - hi.astype(F32)
    mid = r1.astype(BF16)
    lo = (r1 - mid.astype(F32)).astype(BF16)
    return _dot(hi, m01) + _dot(mid, m01) + _dot(lo, m01)


def _pick(n, cands):
    for c in cands:
        if n % c == 0:
            return c
    raise ValueError(f"no tile for {n}")


def _tokmap(name, fn, tok_ins, consts, tok_outs, acc_outs=(), tile=512):
    n = tok_ins[0].shape[0]
    tile = min(tile, n)
    assert n % tile == 0, (name, n, tile)
    n_in, n_tok, n_acc = len(tok_ins) + len(consts), len(tok_outs), len(acc_outs)

    def body(*refs):
        outs = fn(*[r[...] for r in refs[:n_in]])
        outs = outs if isinstance(outs, (tuple, list)) else (outs,)
        assert len(outs) == n_tok + n_acc, (name, len(outs))
        orefs = refs[n_in:]
        for r, v in zip(orefs[:n_tok], outs[:n_tok]):
            r[...] = v.astype(r.dtype)
        if n_acc:
            @pl.when(pl.program_id(0) == 0)
            def _():
                for r in orefs[n_tok:]:
                    r[...] = jnp.zeros(r.shape, r.dtype)
            for r, v in zip(orefs[n_tok:], outs[n_tok:]):
                r[...] += v.astype(r.dtype)

    in_specs = [pl.BlockSpec((tile, a.shape[1]), lambda i: (i, 0)) for a in tok_ins]
    in_specs += [pl.BlockSpec(c.shape, lambda i: (0, 0)) for c in consts]
    out_specs = [pl.BlockSpec((tile, w), lambda i: (i, 0)) for w, _ in tok_outs]
    out_specs += [pl.BlockSpec(s, lambda i: (0, 0)) for s in acc_outs]
    out_shape = [SDS((n, w), dt) for w, dt in tok_outs] + [SDS(s, F32) for s in acc_outs]
    res = pl.pallas_call(
        body, name=name, grid=(n // tile,), in_specs=in_specs, out_specs=out_specs, out_shape=out_shape,
        compiler_params=pltpu.CompilerParams(dimension_semantics=("arbitrary",), vmem_limit_bytes=VMEM_LIMIT),
    )(*tok_ins, *consts)
    return res


def _matmul(name, a, b, dn, out_dtype, epi=None, tiles=(), rows=()):
    if dn == NN:
        (m, k), n = a.shape, b.shape[1]
    elif dn == NT:
        (m, k), n = a.shape, b.shape[0]
    else:
        (k, m), n = a.shape, b.shape[1]
    tm, tn, tk = _pick(m, (1024, 512, 256, 128)), _pick(n, (512, 256, 128)), _pick(k, (1024, 512, 256, 128))
    nk = k // tk
    n_t, n_r = len(tiles), len(rows)

    def body(a_ref, b_ref, *rest):
        o_ref, acc = rest[n_t + n_r], rest[n_t + n_r + 1]
        kk = pl.program_id(2)

        @pl.when(kk == 0)
        def _():
            acc[...] = jnp.zeros(acc.shape, F32)

        acc[...] += _dot(a_ref[...].astype(BF16), b_ref[...].astype(BF16), dn)

        @pl.when(kk == nk - 1)
        def _():
            r = acc[...]
            if epi is not None:
                r = epi(r, *[e[...] for e in rest[:n_t + n_r]])
            o_ref[...] = r.astype(o_ref.dtype)

    a_spec = pl.BlockSpec((tk, tm), lambda i, j, kk: (kk, i)) if dn == TN else pl.BlockSpec((tm, tk), lambda i, j, kk: (i, kk))
    b_spec = pl.BlockSpec((tn, tk), lambda i, j, kk: (j, kk)) if dn == NT else pl.BlockSpec((tk, tn), lambda i, j, kk: (kk, j))
    in_specs = [a_spec, b_spec] + [pl.BlockSpec((tm, tn), lambda i, j, kk: (i, j)) for _ in tiles]
    in_specs += [pl.BlockSpec((1, tn), lambda i, j, kk: (0, j)) for _ in rows]
    return pl.pallas_call(
        body, name=name, grid=(m // tm, n // tn, nk), in_specs=in_specs,
        out_specs=pl.BlockSpec((tm, tn), lambda i, j, kk: (i, j)), out_shape=SDS((m, n), out_dtype),
        scratch_shapes=[pltpu.VMEM((tm, tn), F32)],
        compiler_params=pltpu.CompilerParams(dimension_semantics=("parallel", "parallel", "arbitrary"),
                                             vmem_limit_bytes=VMEM_LIMIT),
    )(a, b, *tiles, *rows)


def _mean_all(v):
    return jnp.mean(v, axis=-1, keepdims=True)


def _mean_heads(bd):
    return lambda v: _dot01(v, bd) * (1.0 / HD)


def _rms_fwd(x, g, mean):
    return x * lax.rsqrt(mean(x * x) + EPS) * g


def _rms_bwd(x, g, dy, mean):
    r = lax.rsqrt(mean(x * x) + EPS)
    dn = dy * g
    dx = r * dn - x * (r * r * r) * mean(dn * x)
    return dx, jnp.sum(dy * x * r, axis=0, keepdims=True)


def _swap_halves(x):
    w = x.shape[1]
    lane = lax.broadcasted_iota(jnp.int32, x.shape, 1)
    return jnp.where(lane % HD < HD // 2, pltpu.roll(x, w - HD // 2, 1), pltpu.roll(x, HD // 2, 1))


def _lanes(t, w):
    return jnp.tile(t, (1, w // t.shape[1]))


def _rope_fwd(x, cos, sin_signed):
    return x * _lanes(cos, x.shape[1]) + _swap_halves(x) * _lanes(sin_signed, x.shape[1])


def _rope_bwd(dy, cos, sin_signed):
    return dy * _lanes(cos, dy.shape[1]) + _swap_halves(dy * _lanes(sin_signed, dy.shape[1]))


def _bcast_heads(cols):
    return jnp.concatenate([jnp.broadcast_to(c, (c.shape[0], HD)) for c in cols], axis=1)


def _softplus(z):
    return jnp.maximum(z, 0.0) + jnp.log1p(jnp.exp(-jnp.abs(z)))


def _block_diag(w):
    h = np.arange(w) // HD
    return jnp.asarray(h[:, None] == h[None, :], BF16)


def _sb_fwd(qkv):
    s = qkv.shape[0]
    nq = s // QB
    npairs = SB_W // 128

    def body(q_ref, k_ref, v_ref, o_ref, tot_ref, nb_ref):
        p, i = pl.program_id(0), pl.program_id(1)
        row = lax.broadcasted_iota(jnp.int32, (QB, QB), 0)
        col = lax.broadcasted_iota(jnp.int32, (QB, QB), 1)
        later_of = (row > col).astype(BF16)
        q = q_ref[...]

        def step(c):
            kb, _, tots, outs = c
            off = pl.multiple_of(kb * QB, QB)
            kblk, vblk = k_ref[pl.ds(off, QB), :], v_ref[pl.ds(off, QB), :]
            mask = (col < row) | (kb < i)
            new_t, new_o = [], []
            for hh in range(2):
                sl = slice(HD * hh, HD * hh + HD)
                z = _dot(q[:, sl], kblk[:, sl], NT) * SCALE
                sp = _softplus(z)
                lf = jnp.where(mask, -sp, 0.0)
                later = tots[hh] + _dot01(lf, later_of)
                w = jnp.where(mask, jnp.exp(z - sp + later), 0.0)
                new_o.append(outs[hh] + _dot(w.astype(BF16), vblk[:, sl]))
                new_t.append(tots[hh] + jnp.sum(lf, axis=1, keepdims=True))
            alive = jnp.maximum(jnp.max(new_t[0]), jnp.max(new_t[1]))
            return kb - 1, alive, tuple(new_t), tuple(new_o)

        zt, zo = jnp.zeros((QB, 1), F32), jnp.zeros((QB, HD), F32)
        kb, _, tots, outs = lax.while_loop(lambda c: (c[0] >= 0) & (c[1] > EXHAUSTED), step,
                                           (i, jnp.float32(0.0), (zt, zt), (zo, zo)))
        o_ref[...] = jnp.concatenate(outs, axis=1).astype(o_ref.dtype)
        tot_ref[...] = _bcast_heads(tots)
        nb_ref[p, i] = i - kb

    whole = lambda off: pl.BlockSpec((s, 128), lambda p, i: (0, off + p), pipeline_mode=pl.Buffered(1))
    tile = pl.BlockSpec((QB, 128), lambda p, i: (i, p))
    return pl.pallas_call(
        body, name="sb_fwd", grid=(npairs, nq),
        in_specs=[tile, whole(npairs), whole(2 * npairs)],
        out_specs=[tile, tile, pl.BlockSpec(memory_space=pltpu.SMEM)],
        out_shape=[SDS((s, SB_W), BF16), SDS((s, SB_W), F32), SDS((npairs, nq), jnp.int32)],
        compiler_params=pltpu.CompilerParams(dimension_semantics=("arbitrary", "arbitrary"), vmem_limit_bytes=VMEM_LIMIT),
    )(qkv, qkv, qkv)


def _sb_bwd(qkv, do, tot, nblk):
    s = qkv.shape[0]
    nq = s // QB
    npairs = SB_W // 128

    def body(nb_ref, q_ref, k_ref, v_ref, do_ref, tot_ref, dq_ref, dk_ref, dv_ref):
        p, i = pl.program_id(0), pl.program_id(1)

        @pl.when(i == 0)
        def _():
            dk_ref[...] = jnp.zeros(dk_ref.shape, F32)
            dv_ref[...] = jnp.zeros(dv_ref.shape, F32)

        row = lax.broadcasted_iota(jnp.int32, (QB, QB), 0)
        col = lax.broadcasted_iota(jnp.int32, (QB, QB), 1)
        upto = (row <= col).astype(BF16)
        before = (row < col).astype(BF16)
        q, dout, tt = q_ref[...], do_ref[...], tot_ref[...]
        n = nb_ref[p, i]
        first = i - n + 1

        def step(t, c):
            pres, gpres, dqs = c
            kb = first + t
            off = pl.multiple_of(kb * QB, QB)
            kblk, vblk = k_ref[pl.ds(off, QB), :], v_ref[pl.ds(off, QB), :]
            mask = (col < row) | (kb < i)
            new_p, new_g, new_dq, dks, dvs = [], [], [], [], []
            for hh in range(2):
                sl = slice(HD * hh, HD * hh + HD)
                z = _dot(q[:, sl], kblk[:, sl], NT) * SCALE
                sp = _softplus(z)
                lf = jnp.where(mask, -sp, 0.0)
                later = tt[:, HD * hh:HD * hh + 1] - (pres[hh] + _dot01(lf, upto))
                w = jnp.where(mask, jnp.exp(z - sp + later), 0.0)
                beta = jnp.exp(z - sp)
                g = _dot(dout[:, sl], vblk[:, sl], NT) * w
                g_far = gpres[hh] + _dot(g.astype(BF16), before)
                dz = (jnp.where(mask, g * (1.0 - beta) - beta * g_far, 0.0) * SCALE).astype(BF16)
                new_dq.append(dqs[hh] + _dot(dz, kblk[:, sl]))
                dks.append(_dot(dz, q[:, sl], TN))
                dvs.append(_dot(w.astype(BF16), dout[:, sl], TN))
                new_p.append(pres[hh] + jnp.sum(lf, axis=1, keepdims=True))
                new_g.append(gpres[hh] + jnp.sum(g, axis=1, keepdims=True))
            dk_ref[pl.ds(off, QB), :] += jnp.concatenate(dks, axis=1)
            dv_ref[pl.ds(off, QB), :] += jnp.concatenate(dvs, axis=1)
            return tuple(new_p), tuple(new_g), tuple(new_dq)

        zt, zo = jnp.zeros((QB, 1), F32), jnp.zeros((QB, HD), F32)
        _, _, dqs = lax.fori_loop(0, n, step, ((zt, zt), (zt, zt), (zo, zo)))
        dq_ref[...] = jnp.concatenate(dqs, axis=1)

    whole_in = lambda off: pl.BlockSpec((s, 128), lambda p, i: (0, off + p), pipeline_mode=pl.Buffered(1))
    whole_out = pl.BlockSpec((s, 128), lambda p, i: (0, p), pipeline_mode=pl.Buffered(1))
    tile = pl.BlockSpec((QB, 128), lambda p, i: (i, p))
    return pl.pallas_call(
        body, name="sb_bwd", grid=(npairs, nq),
        in_specs=[pl.BlockSpec(memory_space=pltpu.SMEM), tile, whole_in(npairs), whole_in(2 * npairs), tile, tile],
        out_specs=[tile, whole_out, whole_out],
        out_shape=[SDS((s, SB_W), F32)] * 3,
        compiler_params=pltpu.CompilerParams(dimension_semantics=("arbitrary", "arbitrary"), vmem_limit_bytes=VMEM_LIMIT),
    )(nblk, qkv, qkv, qkv, do, tot)


def _win_masks(has_prev):
    row = lax.broadcasted_iota(jnp.int32, (QB, QB), 0)
    col = lax.broadcasted_iota(jnp.int32, (QB, QB), 1)
    return col <= row, (col >= row) & has_prev


def _dsa_fwd(q, k, v, dil):
    s = q.shape[0]
    n = s // dil
    nb = n // QB
    q, k, v = (t.reshape(n, dil * DSA_OUT_W) for t in (q, k, v))

    def body(q_ref, kc_ref, kp_ref, vc_ref, vp_ref, o_ref, lse_ref):
        m_cur, m_prev = _win_masks(pl.program_id(1) > 0)
        outs, lses = [], []
        for hh in range(2):
            sl = slice(HD * hh, HD * hh + HD)
            qh = q_ref[:, sl]
            sc = jnp.where(m_cur, _dot(qh, kc_ref[:, sl], NT) * SCALE, NEG)
            sp = jnp.where(m_prev, _dot(qh, kp_ref[:, sl], NT) * SCALE, NEG)
            m = jnp.maximum(jnp.max(sc, axis=1, keepdims=True), jnp.max(sp, axis=1, keepdims=True))
            pc, pp = jnp.exp(sc - m), jnp.exp(sp - m)
            den = jnp.sum(pc, axis=1, keepdims=True) + jnp.sum(pp, axis=1, keepdims=True)
            outs.append((_dot(pc.astype(BF16), vc_ref[:, sl]) + _dot(pp.astype(BF16), vp_ref[:, sl])) / den)
            lses.append(m + jnp.log(den))
        o_ref[...] = jnp.concatenate(outs, axis=1)
        lse_ref[...] = _bcast_heads(lses)

    cur = pl.BlockSpec((QB, 128), lambda c, i, p: (i, 2 * c + p))
    prev = pl.BlockSpec((QB, 128), lambda c, i, p: (jnp.maximum(i - 1, 0), 2 * c + p))
    o, lse = pl.pallas_call(
        body, name=f"dsa_fwd_d{dil}", grid=(dil, nb, 2), in_specs=[cur, cur, prev, cur, prev], out_specs=[cur, cur],
        out_shape=[SDS((n, dil * DSA_OUT_W), F32)] * 2,
        compiler_params=pltpu.CompilerParams(dimension_semantics=("parallel", "parallel", "parallel")),
    )(q, k, k, v, v)
    return o.reshape(s, DSA_OUT_W), lse.reshape(s, DSA_OUT_W)


def _dsa_bwd(q, k, v, do, cc, lse, dil):
    s = q.shape[0]
    n = s // dil
    nb = n // QB
    q, k, v, do, cc, lse = (t.reshape(n, dil * DSA_OUT_W) for t in (q, k, v, do, cc, lse))

    def body(qj_ref, qn_ref, kp_ref, kj_ref, vp_ref, vj_ref, doj_ref, don_ref, cj_ref, cn_ref, lj_ref, ln_ref,
             dq_ref, dk_ref, dv_ref):
        j = pl.program_id(1)
        m_cur, m_prev = _win_masks(j > 0)
        _, m_next = _win_masks(j + 1 < nb)
        dqs, dks, dvs = [], [], []
        for hh in range(2):
            sl = slice(HD * hh, HD * hh + HD)
            one = slice(HD * hh, HD * hh + 1)
            qj, qn, kp, kj, vp, vj = (r[:, sl] for r in (qj_ref, qn_ref, kp_ref, kj_ref, vp_ref, vj_ref))
            doj, don = doj_ref[:, sl], don_ref[:, sl]

            def dscore(qq, kk, vv, dd, c_ref, l_ref, mask):
                prob = jnp.where(mask, jnp.exp(_dot(qq, kk, NT) * SCALE - l_ref[:, one]), 0.0)
                return prob, (prob * (_dot(dd, vv, NT) + c_ref[:, one]) * SCALE).astype(BF16)

            _, ds_a = dscore(qj, kp, vp, doj, cj_ref, lj_ref, m_prev)
            p_b, ds_b = dscore(qj, kj, vj, doj, cj_ref, lj_ref, m_cur)
            p_c, ds_c = dscore(qn, kj, vj, don, cn_ref, ln_ref, m_next)
            dqs.append(_dot(ds_a, kp) + _dot(ds_b, kj))
            dks.append(_dot(ds_b, qj, TN) + _dot(ds_c, qn, TN))
            dvs.append(_dot(p_b.astype(BF16), doj, TN) + _dot(p_c.astype(BF16), don, TN))
        dq_ref[...] = jnp.concatenate(dqs, axis=1)
        dk_ref[...] = jnp.concatenate(dks, axis=1)
        dv_ref[...] = jnp.concatenate(dvs, axis=1)

    cur = pl.BlockSpec((QB, 128), lambda c, j, p: (j, 2 * c + p))
    prev = pl.BlockSpec((QB, 128), lambda c, j, p: (jnp.maximum(j - 1, 0), 2 * c + p))
    nxt = pl.BlockSpec((QB, 128), lambda c, j, p: (jnp.minimum(j + 1, nb - 1), 2 * c + p))
    dq, dk, dv = pl.pallas_call(
        body, name=f"dsa_bwd_d{dil}", grid=(dil, nb, 2),
        in_specs=[cur, nxt, prev, cur, prev, cur, cur, nxt, cur, nxt, cur, nxt], out_specs=[cur, cur, cur],
        out_shape=[SDS((n, dil * DSA_OUT_W), F32)] * 3,
        compiler_params=pltpu.CompilerParams(dimension_semantics=("parallel", "parallel", "parallel")),
    )(q, q, k, k, v, v, do, do, cc, cc, lse, lse)
    return tuple(t.reshape(s, DSA_OUT_W) for t in (dq, dk, dv))


def _ffn_fwd(tag, x, gain, w13, w2):
    n = _tokmap(f"{tag}_norm", lambda xv, g: _rms_fwd(xv, g, _mean_all), [x], [gain], [(D, BF16)])[0]
    ab = _matmul(f"{tag}_up", n, w13, NN, BF16)

    def gate(abv):
        a, b = abv[:, :D_FF].astype(F32), abv[:, D_FF:].astype(F32)
        return a * jax.nn.sigmoid(a) * b

    h = _tokmap(f"{tag}_gate", gate, [ab], [], [(D_FF, BF16)], tile=256)[0]
    y = _matmul(f"{tag}_down", h, w2, NN, F32, epi=lambda acc, res: res + 0.5 * acc, tiles=[x])
    return y, (n, ab)


def _ffn_bwd(tag, x, gain, w13, w2, saved, dy):
    n, ab = saved
    dh = _matmul(f"{tag}_bwd_dh", dy, w2, NT, BF16, epi=lambda acc: 0.5 * acc)

    def gate_bwd(abv, dhv):
        a, b, dhf = abv[:, :D_FF].astype(F32), abv[:, D_FF:].astype(F32), dhv.astype(F32)
        sg = jax.nn.sigmoid(a)
        silu = a * sg
        da = dhf * b * (sg * (1.0 + a * (1.0 - sg)))
        return jnp.concatenate([da, dhf * silu], axis=1), silu * b

    dab, h = _tokmap(f"{tag}_bwd_gate", gate_bwd, [ab, dh], [], [(2 * D_FF, BF16), (D_FF, BF16)], tile=256)
    dw2 = _matmul(f"{tag}_bwd_dw2", h, dy, TN, F32, epi=lambda acc: 0.5 * acc)
    dw13 = _matmul(f"{tag}_bwd_dw13", n, dab, TN, F32)
    dn = _matmul(f"{tag}_bwd_dn", dab, w13, NT, F32)

    def norm_bwd(xv, dnv, dyv, g):
        dx, dg = _rms_bwd(xv, g, dnv, _mean_all)
        return dx + dyv, dg

    dx, dgain = _tokmap(f"{tag}_bwd_norm", norm_bwd, [x, dn, dy], [gain], [(D, F32)], [(1, D)])
    return dx, dgain, dw13, dw2


def _rope_tables(s):
    half = HD // 2
    inv_freq = jnp.power(10000.0, -jnp.arange(half, dtype=F32) / half)
    ang = jnp.arange(s).astype(F32)[:, None] * inv_freq[None, :]
    cos, sin = jnp.cos(ang), jnp.sin(ang)
    return jnp.tile(jnp.concatenate([cos, cos], axis=1), (1, 2)), jnp.tile(jnp.concatenate([-sin, sin], axis=1), (1, 2))


def _local_step(x, mem, tgt, w, sm):
    s = x.shape[0]
    assert s % (QB * max(DSA_DILS)) == 0
    w13_1 = jnp.concatenate([w["ffn1_w1"], w["ffn1_w3"]], axis=1)
    w13_2 = jnp.concatenate([w["ffn2_w1"], w["ffn2_w3"]], axis=1)
    w_all = jnp.concatenate([w["w_in"], w["w_gate"]], axis=1)
    c_sb, c_dsa, c_qm = 3 * SB_W, 3 * SB_W + 3 * DSA_W, 4096
    cos, sin = _rope_tables(s)
    bd768, bd256 = _block_diag(DSA_W), _block_diag(MEM_W)
    gq_dsa, gk_dsa = jnp.tile(sm["qn_dsa"], (1, DSA_W // HD)), jnp.tile(sm["kn_dsa"], (1, DSA_W // HD))
    gq_mem, gk_mem = jnp.tile(sm["qn_mem"], (1, MEM_W // HD)), jnp.tile(sm["kn_mem"], (1, MEM_W // HD))
    wb_sb, wb_dsa, wb_mem = w["w_branch_sb"], w["w_branch_dsa"], w["w_branch_mem"]

    x1, ffn1_saved = _ffn_fwd("ffn1", x, sm["ffn1_norm"], w13_1, w["ffn1_w2"])
    hmix = _tokmap("mix_norm", lambda xv, g: _rms_fwd(xv, g, _mean_all), [x1], [sm["mix_norm"]], [(D, BF16)])[0]
    qkv_sb = _matmul("proj_sb", hmix, w_all[:, :c_sb], NN, BF16)
    qkv_dsa = _matmul("proj_dsa", hmix, w_all[:, c_sb:c_dsa], NN, BF16)
    q_mem = _matmul("proj_qmem", hmix, w_all[:, c_dsa:c_qm], NN, BF16)
    gpre = _matmul("proj_gate", hmix, w_all[:, c_qm:], NN, BF16, epi=lambda acc, b: acc + b, rows=[sm["b_gate"]])

    o_sb, sb_tot, sb_nblk = _sb_fwd(qkv_sb)

    def dsa_prep(qkv, cs, sn, gq, gk, bd):
        mean = _mean_heads(bd)
        qn = _rope_fwd(_rms_fwd(qkv[:, :DSA_W].astype(F32), gq, mean), cs, sn)
        kn = _rope_fwd(_rms_fwd(qkv[:, DSA_W:2 * DSA_W].astype(F32), gk, mean), cs, sn)
        v = qkv[:, 2 * DSA_W:]
        outs = []
        for t in (qn, kn, v):
            outs += [t[:, DSA_OUT_W * g:DSA_OUT_W * (g + 1)] for g in range(3)]
        return outs

    dsa_in = _tokmap("dsa_prep", dsa_prep, [qkv_dsa, cos, sin], [gq_dsa, gk_dsa, bd768], [(DSA_OUT_W, BF16)] * 9, tile=256)
    dsa_q, dsa_k, dsa_v = dsa_in[0:3], dsa_in[3:6], dsa_in[6:9]
    dsa_o, dsa_lse = zip(*[_dsa_fwd(dsa_q[g], dsa_k[g], dsa_v[g], DSA_DILS[g]) for g in range(3)])

    def alphas(l0, l1, l2):
        m = jnp.maximum(jnp.maximum(l0, l1), l2)
        e = [jnp.exp(l - m) for l in (l0, l1, l2)]
        tot = e[0] + e[1] + e[2]
        return [t / tot for t in e]

    def dsa_mix(o0, o1, o2, l0, l1, l2):
        a = alphas(l0, l1, l2)
        return a[0] * o0 + a[1] * o1 + a[2] * o2

    o_dsa = _tokmap("dsa_mix", dsa_mix, [*dsa_o, *dsa_lse], [], [(DSA_OUT_W, BF16)])[0]

    def mem_kv(memv, g, wkv, gk, bd):
        kv = _dot(_rms_fwd(memv, g, _mean_all).astype(BF16), wkv)
        return _rms_fwd(kv[:, :MEM_W], gk, _mean_heads(bd)), kv[:, MEM_W:]

    km, vm = _tokmap("mem_kv", mem_kv, [mem], [sm["mem_norm"], w["w_mem_kv"], gk_mem, bd256], [(MEM_W, BF16)] * 2)

    def mem_probs(qv, kmv, gq, bd):
        qn = _rms_fwd(qv.astype(F32), gq, _mean_heads(bd)).astype(BF16)
        ps = []
        for h in range(MEM_W // HD):
            sl = slice(HD * h, HD * h + HD)
            sc = _dot(qn[:, sl], kmv[:, sl], NT) * SCALE
            e = jnp.exp(sc - jnp.max(sc, axis=1, keepdims=True))
            ps.append(e / jnp.sum(e, axis=1, keepdims=True))
        return qn, ps

    def mem_attn(qv, kmv, vmv, gq, bd):
        _, ps = mem_probs(qv, kmv, gq, bd)
        return jnp.concatenate([_dot(p.astype(BF16), vmv[:, HD * h:HD * h + HD]) for h, p in enumerate(ps)], axis=1)

    o_mem = _tokmap("mem_attn", mem_attn, [q_mem], [km, vm, gq_mem, bd256], [(MEM_W, BF16)])[0]

    def merge(osb, odsa, omem, gp, w_sb, w_dsa, w_mem):
        gates = jax.nn.sigmoid(gp.astype(F32))
        ys = (_dot(osb, w_sb), _dot(odsa, w_dsa), _dot(omem, w_mem))
        return gates, ys, gates[:, :D] * ys[0] + gates[:, D:2 * D] * ys[1] + gates[:, 2 * D:] * ys[2]

    merged = _tokmap("merge", lambda *a: merge(*a)[2], [o_sb, o_dsa, o_mem, gpre], [wb_sb, wb_dsa, wb_mem], [(D, BF16)],
                     tile=256)[0]
    x2 = _matmul("out_proj", merged, w["w_out"], NN, F32, epi=lambda acc, res: res + acc, tiles=[x1])
    y, ffn2_saved = _ffn_fwd("ffn2", x2, sm["ffn2_norm"], w13_2, w["ffn2_w2"])

    def loss_fn(yv, tv):
        e = yv - tv
        part = 0.5 * jnp.sum(jnp.mean(e * e, axis=1, keepdims=True), axis=0, keepdims=True)
        return e * (1.0 / D), jnp.broadcast_to(part, (1, 128))

    dy, loss = _tokmap("loss", loss_fn, [y, tgt], [], [(D, F32)], [(1, 128)])

    gw, gs = {}, {}
    dx2, gs["ffn2_norm"], dw13, gw["ffn2_w2"] = _ffn_bwd("ffn2", x2, sm["ffn2_norm"], w13_2, w["ffn2_w2"], ffn2_saved, dy)
    gw["ffn2_w1"], gw["ffn2_w3"] = dw13[:, :D_FF], dw13[:, D_FF:]
    dmerged = _matmul("out_proj_bwd_dx", dx2, w["w_out"], NT, BF16)
    gw["w_out"] = _matmul("out_proj_bwd_dw", merged, dx2, TN, F32)

    def merge_bwd(osb, odsa, omem, gp, dm, w_sb, w_dsa, w_mem):
        gates, ys, _ = merge(osb, odsa, omem, gp, w_sb, w_dsa, w_mem)
        dmf = dm.astype(F32)
        dgp, dos, dws = [], [], []
        for b, (ov, wv) in enumerate(((osb, w_sb), (odsa, w_dsa), (omem, w_mem))):
            gb = gates[:, D * b:D * (b + 1)]
            dgp.append(dmf * ys[b] * gb * (1.0 - gb))
            dyb = (dmf * gb).astype(BF16)
            dos.append(_dot(dyb, wv, NT))
            dws.append(_dot(ov, dyb, TN))
        dgp = jnp.concatenate(dgp, axis=1)
        return dos[0], dos[1], dos[2], dgp, dws[0], dws[1], dws[2], jnp.sum(dgp, axis=0, keepdims=True)

    do_sb, do_dsa, do_mem, dgpre, gw["w_branch_sb"], gw["w_branch_dsa"], gw["w_branch_mem"], gs["b_gate"] = _tokmap(
        "merge_bwd", merge_bwd, [o_sb, o_dsa, o_mem, gpre, dmerged], [wb_sb, wb_dsa, wb_mem],
        [(SB_W, BF16), (DSA_OUT_W, F32), (MEM_W, BF16), (3 * D, BF16)],
        [(SB_W, D), (DSA_OUT_W, D), (MEM_W, D), (1, 3 * D)], tile=256)

    dq_sb, dk_sb, dv_sb = _sb_bwd(qkv_sb, do_sb, sb_tot, sb_nblk)

    def dsa_mix_bwd(o0, o1, o2, l0, l1, l2, dov, bd):
        a = alphas(l0, l1, l2)
        omix = a[0] * o0 + a[1] * o1 + a[2] * o2
        dot_o = _dot01(dov * omix, bd)
        return [dov * t for t in a] + [-t * dot_o for t in a]

    mixb = _tokmap("dsa_mix_bwd", dsa_mix_bwd, [*dsa_o, *dsa_lse, do_dsa], [bd256],
                   [(DSA_OUT_W, BF16)] * 3 + [(DSA_OUT_W, F32)] * 3)
    dsa_d = [_dsa_bwd(dsa_q[g], dsa_k[g], dsa_v[g], mixb[g], mixb[3 + g], dsa_lse[g], DSA_DILS[g]) for g in range(3)]

    def dsa_prep_bwd(qkv, cs, sn, *rest):
        dqs, dks, dvs, (gq, gk, bd) = rest[0:3], rest[3:6], rest[6:9], rest[9:]
        mean = _mean_heads(bd)
        dq, dgq = _rms_bwd(qkv[:, :DSA_W].astype(F32), gq, _rope_bwd(jnp.concatenate(dqs, axis=1), cs, sn), mean)
        dk, dgk = _rms_bwd(qkv[:, DSA_W:2 * DSA_W].astype(F32), gk, _rope_bwd(jnp.concatenate(dks, axis=1), cs, sn), mean)
        return jnp.concatenate([dq, dk] + list(dvs), axis=1), dgq, dgk

    dqkv_dsa, dgq_dsa, dgk_dsa = _tokmap(
        "dsa_prep_bwd", dsa_prep_bwd,
        [qkv_dsa, cos, sin] + [dsa_d[g][t] for t in range(3) for g in range(3)], [gq_dsa, gk_dsa, bd768],
        [(3 * DSA_W, BF16)], [(1, DSA_W), (1, DSA_W)], tile=256)
    gs["qn_dsa"] = dgq_dsa.reshape(DSA_W // HD, HD).sum(axis=0, keepdims=True)
    gs["kn_dsa"] = dgk_dsa.reshape(DSA_W // HD, HD).sum(axis=0, keepdims=True)

    def mem_attn_bwd(qv, dov, kmv, vmv, gq, bd):
        qn, ps = mem_probs(qv, kmv, gq, bd)
        dqn, dkm, dvm = [], [], []
        for h, p in enumerate(ps):
            sl = slice(HD * h, HD * h + HD)
            dp = _dot(dov[:, sl], vmv[:, sl], NT)
            ds = (p * (dp - jnp.sum(p * dp, axis=1, keepdims=True)) * SCALE).astype(BF16)
            dqn.append(_dot(ds, kmv[:, sl]))
            dkm.append(_dot(ds, qn[:, sl], TN))
            dvm.append(_dot(p.astype(BF16), dov[:, sl], TN))
        dq, dgq = _rms_bwd(qv.astype(F32), gq, jnp.concatenate(dqn, axis=1), _mean_heads(bd))
        return dq, jnp.concatenate(dkm, axis=1), jnp.concatenate(dvm, axis=1), dgq

    dq_mem, dkm, dvm, dgq_mem = _tokmap("mem_attn_bwd", mem_attn_bwd, [q_mem, do_mem], [km, vm, gq_mem, bd256],
                                        [(MEM_W, BF16)], [(MEM_LEN, MEM_W), (MEM_LEN, MEM_W), (1, MEM_W)])
    gs["qn_mem"] = dgq_mem.reshape(MEM_W // HD, HD).sum(axis=0, keepdims=True)

    def mem_kv_bwd(memv, dkmv, dvmv, g, wkv, gk, bd):
        memn = _rms_fwd(memv, g, _mean_all).astype(BF16)
        kv = _dot(memn, wkv)
        dk, dgk = _rms_bwd(kv[:, :MEM_W], gk, dkmv, _mean_heads(bd))
        dkv = jnp.concatenate([dk, dvmv], axis=1).astype(BF16)
        _, dg = _rms_bwd(memv, g, _dot(dkv, wkv, NT), _mean_all)
        return _dot(memn, dkv, TN), dg, dgk

    gw["w_mem_kv"], gs["mem_norm"], dgk_mem = _tokmap(
        "mem_kv_bwd", mem_kv_bwd, [mem, dkm, dvm], [sm["mem_norm"], w["w_mem_kv"], gk_mem, bd256], [],
        [(D, 2 * MEM_W), (1, D), (1, MEM_W)])
    gs["kn_mem"] = dgk_mem.reshape(MEM_W // HD, HD).sum(axis=0, keepdims=True)

    dall = jnp.concatenate([dq_sb.astype(BF16), dk_sb.astype(BF16), dv_sb.astype(BF16), dqkv_dsa, dq_mem, dgpre], axis=1)
    dhmix = _matmul("proj_bwd_dx", dall, w_all, NT, F32)
    dw_all = _matmul("proj_bwd_dw", hmix, dall, TN, F32)
    gw["w_in"], gw["w_gate"] = dw_all[:, :c_qm], dw_all[:, c_qm:]

    def mix_norm_bwd(xv, dnv, dyv, g):
        dx, dg = _rms_bwd(xv, g, dnv, _mean_all)
        return dx + dyv, dg

    dx1, gs["mix_norm"] = _tokmap("mix_norm_bwd", mix_norm_bwd, [x1, dhmix, dx2], [sm["mix_norm"]], [(D, F32)], [(1, D)])
    gx, gs["ffn1_norm"], dw13, gw["ffn1_w2"] = _ffn_bwd("ffn1", x, sm["ffn1_norm"], w13_1, w["ffn1_w2"], ffn1_saved, dx1)
    gw["ffn1_w1"], gw["ffn1_w3"] = dw13[:, :D_FF], dw13[:, D_FF:]
    return loss, gx, gw, gs


def _shard_shape(shape, axis):
    return (shape[0] // N_CHIPS, shape[1]) if axis == 0 else (shape[0], shape[1] // N_CHIPS)


PACK_ROWS = sum(sh[0] * sh[1] // N_CHIPS // D for _, sh, _ in SHARDED)


def _pack(shards):
    return jnp.concatenate([shards[n].reshape(-1, D) for n, _, _ in SHARDED], axis=0)


def _unpack(pack):
    out, r = {}, 0
    for n, sh, ax in SHARDED:
        ss = _shard_shape(sh, ax)
        rows = ss[0] * ss[1] // D
        out[n] = pack[r:r + rows].reshape(ss)
        r += rows
    return out


def _full_from_packs(packs):
    per_chip = [_unpack(packs[c]) for c in range(N_CHIPS)]
    return {n: jnp.concatenate([per_chip[c][n] for c in range(N_CHIPS)], axis=ax) for n, _, ax in SHARDED}


def _packs_from_full(full):
    packs = []
    for c in range(N_CHIPS):
        shards = {}
        for n, sh, ax in SHARDED:
            ss = _shard_shape(sh, ax)
            shards[n] = lax.slice_in_dim(full[n], c * ss[ax], (c + 1) * ss[ax], axis=ax)
        packs.append(_pack(shards))
    return jnp.stack(packs)


SMALL_USED = sum(n for _, n in SMALL)


def _pack_small(d, loss=None):
    parts = [d[n].reshape(-1) for n, _ in SMALL]
    parts.append(jnp.zeros((1,), F32) if loss is None else loss.reshape(1))
    parts.append(jnp.zeros((SMALL_ROWS * D - SMALL_USED - 1,), F32))
    return jnp.concatenate(parts).reshape(SMALL_ROWS, D)


def _unpack_small(v):
    flat, out, r = v.reshape(-1), {}, 0
    for n, k in SMALL:
        out[n] = flat[r:r + k]
        r += k
    return out, flat[r]


def _place():
    return lax.axis_index("x"), lax.axis_index("y"), lax.axis_index("c")


def _other_chips(x, y):
    return [(1 - x, y), (x, 1 - y), (1 - x, 1 - y)]


HBM_SPEC = pl.BlockSpec(memory_space=pl.ANY)


def _all_gather_chips(pack):
    def body(src, out, send_sems, recv_sems, local_sem):
        x, y, c = _place()
        me = 2 * x + y
        mine = pltpu.make_async_copy(src, out.at[me], local_sem)
        mine.start()
        copies = [pltpu.make_async_remote_copy(src_ref=src, dst_ref=out.at[me], send_sem=send_sems.at[k], recv_sem=recv_sems.at[k],
                                               device_id=(px, py, c), device_id_type=MESH)
                  for k, (px, py) in enumerate(_other_chips(x, y))]
        for cp in copies:
            cp.start()
        for cp in copies:
            cp.wait()
        mine.wait()

    return pl.pallas_call(
        body, name="weights_all_gather", in_specs=[HBM_SPEC], out_specs=HBM_SPEC,
        out_shape=SDS((N_CHIPS,) + pack.shape, pack.dtype),
        scratch_shapes=[pltpu.SemaphoreType.DMA((3,)), pltpu.SemaphoreType.DMA((3,)), pltpu.SemaphoreType.DMA],
    )(pack)


def _scatter_to_chips(packs):
    def body(src, out, send_sems, recv_sems):
        x, y, c = _place()
        copies = [pltpu.make_async_remote_copy(src_ref=src.at[2 * px + py], dst_ref=out.at[k], send_sem=send_sems.at[k],
                                               recv_sem=recv_sems.at[k], device_id=(px, py, c), device_id_type=MESH)
                  for k, (px, py) in enumerate(_other_chips(x, y))]
        for cp in copies:
            cp.start()
        for cp in copies:
            cp.wait()

    return pl.pallas_call(
        body, name="grads_scatter", in_specs=[HBM_SPEC], out_specs=HBM_SPEC,
        out_shape=SDS((3,) + packs.shape[1:], packs.dtype),
        scratch_shapes=[pltpu.SemaphoreType.DMA((3,)), pltpu.SemaphoreType.DMA((3,))],
    )(packs)


def _swap_with_sibling(v):
    def body(src, out, send_sem, recv_sem):
        x, y, c = _place()
        cp = pltpu.make_async_remote_copy(src_ref=src, dst_ref=out, send_sem=send_sem, recv_sem=recv_sem,
                                          device_id=(x, y, 1 - c), device_id_type=MESH)
        cp.start()
        cp.wait()

    return pl.pallas_call(
        body, name="grads_swap_cores", in_specs=[HBM_SPEC], out_specs=HBM_SPEC, out_shape=SDS(v.shape, v.dtype),
        scratch_shapes=[pltpu.SemaphoreType.DMA, pltpu.SemaphoreType.DMA],
    )(v)


def _all_reduce_small(v):
    n_dev = 8

    def body(v_ref, out_ref, land, send_sems, recv_sems):
        x, y, c = _place()
        me = 4 * x + 2 * y + c
        land[me] = v_ref[...]
        copies = []
        for k in range(1, n_dev):
            peer = (x ^ (k >> 2), y ^ ((k >> 1) & 1), c ^ (k & 1))
            copies.append(pltpu.make_async_remote_copy(src_ref=v_ref, dst_ref=land.at[me], send_sem=send_sems.at[k - 1],
                                                       recv_sem=recv_sems.at[k - 1], device_id=peer, device_id_type=MESH))
        for cp in copies:
            cp.start()
        for cp in copies:
            cp.wait()
        acc = land[0]
        for d in range(1, n_dev):
            acc = acc + land[d]
        out_ref[...] = acc

    return pl.pallas_call(
        body, name="small_all_reduce", in_specs=[pl.BlockSpec(memory_space=pltpu.VMEM)],
        out_specs=pl.BlockSpec(memory_space=pltpu.VMEM), out_shape=SDS(v.shape, v.dtype),
        scratch_shapes=[pltpu.VMEM((n_dev,) + v.shape, v.dtype), pltpu.SemaphoreType.DMA((n_dev - 1,)),
                        pltpu.SemaphoreType.DMA((n_dev - 1,))],
    )(v)


def _adamw(g, wv, m, v):
    m = ADAM_B1 * m + (1.0 - ADAM_B1) * g
    v = ADAM_B2 * v + (1.0 - ADAM_B2) * (g * g)
    m_hat = m / (1.0 - ADAM_B1 ** ADAM_STEP)
    v_hat = v / (1.0 - ADAM_B2 ** ADAM_STEP)
    delta = -ADAM_LR * (m_hat / (jnp.sqrt(v_hat) + ADAM_EPS) + ADAM_WD * wv)
    return delta, m, v


def kernel(x, mem, ffn1_norm, ffn1_w1, ffn1_w3, ffn1_w2, mix_norm, mem_norm, w_in, w_mem_kv, qn_dsa, kn_dsa, qn_mem, kn_mem, w_branch_sb, w_branch_dsa, w_branch_mem, w_gate, b_gate, w_out, ffn2_norm, ffn2_w1, ffn2_w3, ffn2_w2, loss_target, m_ffn1_norm, m_ffn1_w1, m_ffn1_w3, m_ffn1_w2, m_mix_norm, m_mem_norm, m_w_in, m_w_mem_kv, m_qn_dsa, m_kn_dsa, m_qn_mem, m_kn_mem, m_w_branch_sb, m_w_branch_dsa, m_w_branch_mem, m_w_gate, m_b_gate, m_w_out, m_ffn2_norm, m_ffn2_w1, m_ffn2_w3, m_ffn2_w2, v_ffn1_norm, v_ffn1_w1, v_ffn1_w3, v_ffn1_w2, v_mix_norm, v_mem_norm, v_w_in, v_w_mem_kv, v_qn_dsa, v_kn_dsa, v_qn_mem, v_kn_mem, v_w_branch_sb, v_w_branch_dsa, v_w_branch_mem, v_w_gate, v_b_gate, v_w_out, v_ffn2_norm, v_ffn2_w1, v_ffn2_w3, v_ffn2_w2):
    given = dict(locals())
    wts = {n: given[n][0] for n in WEIGHTS}
    moms = {n: given["m_" + n][0] for n in WEIGHTS}
    vars_ = {n: given["v_" + n][0] for n in WEIGHTS}

    w_pack = _pack({n: wts[n] for n, _, _ in SHARDED})
    full = _full_from_packs(_all_gather_chips(w_pack.astype(BF16)))
    small = {n: wts[n].reshape(1, -1) for n, _ in SMALL}

    loss, gx, gw, gs = _local_step(x[0], mem[0], loss_target[0], full, small)

    g_packs = _packs_from_full(gw)
    x_i, y_i, _ = _place()
    landed = _scatter_to_chips(g_packs)
    own = lax.dynamic_index_in_dim(g_packs, 2 * x_i + y_i, 0, keepdims=False)
    half = _tokmap("grads_sum_chips", lambda a, b0, b1, b2: ((a + b0) + b1) + b2, [own, landed[0], landed[1], landed[2]], [],
                   [(D, F32)])[0]
    other = _swap_with_sibling(half)

    def update(hv, ov, wv, mv, vv):
        g = hv + ov
        return (g,) + _adamw(g, wv, mv, vv)

    g_sh, d_sh, m_sh, v_sh = _tokmap(
        "adamw", update, [half, other, w_pack, _pack({n: moms[n] for n, _, _ in SHARDED}), _pack({n: vars_[n] for n, _, _ in SHARDED})],
        [], [(D, F32)] * 4)

    s_red = _all_reduce_small(_pack_small(gs, loss[0, 0]))
    g_sm, d_sm, m_sm, v_sm = _tokmap(
        "adamw_small", lambda g, wv, mv, vv: (g,) + _adamw(g, wv, mv, vv),
        [s_red, _pack_small(small), _pack_small({n: moms[n] for n, _ in SMALL}), _pack_small({n: vars_[n] for n, _ in SMALL})],
        [], [(D, F32)] * 4)

    outs = []
    for packed, packed_small in ((g_sh, g_sm), (d_sh, d_sm), (m_sh, m_sm), (v_sh, v_sm)):
        d = _unpack(packed)
        ds, _ = _unpack_small(packed_small)
        d.update(ds)
        outs += [d[n][None] for n in WEIGHTS]
    _, total_loss = _unpack_small(s_red)
    return (total_loss, gx[None], *outs)
```

```python
import functools

import numpy as np
import jax
import jax.numpy as jnp
from jax import lax
from jax.experimental import pallas as pl
from jax.experimental.pallas import tpu as pltpu

F32, BF16 = jnp.float32, jnp.bfloat16
SDS = jax.ShapeDtypeStruct
MESH = pl.DeviceIdType.MESH

D = 1024
HD = 64
QB = 128
D_FF = 2816
SB_W, DSA_W, DSA_OUT_W, MEM_W = 512, 768, 256, 256
DSA_DILS = (1, 4, 16)
MEM_LEN = 256
N_CHIPS = 4
EPS = 1e-6
SCALE = HD ** -0.5
EXHAUSTED = -104.0
SB_NB = 4
NEG = -1e30
VMEM_LIMIT = 56 * 1024 * 1024

ADAM_LR, ADAM_B1, ADAM_B2, ADAM_EPS, ADAM_WD, ADAM_STEP = 0.001, 0.9, 0.999, 1e-08, 0.01, 10

NN = (((1,), (0,)), ((), ()))
NT = (((1,), (1,)), ((), ()))
TN = (((0,), (0,)), ((), ()))

SHARDED = (
    ("ffn1_w1", (D, D_FF), 1), ("ffn1_w3", (D, D_FF), 1), ("ffn1_w2", (D_FF, D), 0),
    ("w_in", (D, 4096), 1), ("w_mem_kv", (D, 512), 0),
    ("w_branch_sb", (SB_W, D), 1), ("w_branch_dsa", (DSA_OUT_W, D), 1), ("w_branch_mem", (MEM_W, D), 1),
    ("w_gate", (D, 3 * D), 1), ("w_out", (D, D), 0),
    ("ffn2_w1", (D, D_FF), 1), ("ffn2_w3", (D, D_FF), 1), ("ffn2_w2", (D_FF, D), 0),
)
SMALL = (("ffn1_norm", D), ("mix_norm", D), ("mem_norm", D), ("ffn2_norm", D), ("b_gate", 3 * D),
         ("qn_dsa", HD), ("kn_dsa", HD), ("qn_mem", HD), ("kn_mem", HD))
WEIGHTS = ("ffn1_norm", "ffn1_w1", "ffn1_w3", "ffn1_w2", "mix_norm", "mem_norm", "w_in", "w_mem_kv", "qn_dsa", "kn_dsa",
           "qn_mem", "kn_mem", "w_branch_sb", "w_branch_dsa", "w_branch_mem", "w_gate", "b_gate", "w_out", "ffn2_norm",
           "ffn2_w1", "ffn2_w3", "ffn2_w2")
SMALL_ROWS = 8


def _dot(a, b, dn=NN):
    return lax.dot_general(a, b, dn, preferred_element_type=F32)


def _dot01(x, m01):
    hi = x.astype(BF16)
    r1 = x - hi.astype(F32)
    mid = r1.astype(BF16)
    lo = (r1 - mid.astype(F32)).astype(BF16)
    return _dot(hi, m01) + _dot(mid, m01) + _dot(lo, m01)


def _pick(n, cands):
    for c in cands:
        if n % c == 0:
            return c
    raise ValueError(f"no tile for {n}")


def _tokmap(name, fn, tok_ins, consts, tok_outs, acc_outs=(), tile=512):
    n = tok_ins[0].shape[0]
    tile = min(tile, n)
    assert n % tile == 0, (name, n, tile)
    n_in, n_tok, n_acc = len(tok_ins) + len(consts), len(tok_outs), len(acc_outs)

    def body(*refs):
        outs = fn(*[r[...] for r in refs[:n_in]])
        outs = outs if isinstance(outs, (tuple, list)) else (outs,)
        assert len(outs) == n_tok + n_acc, (name, len(outs))
        orefs = refs[n_in:]
        for r, v in zip(orefs[:n_tok], outs[:n_tok]):
            r[...] = v.astype(r.dtype)
        if n_acc:
            @pl.when(pl.program_id(0) == 0)
            def _():
                for r in orefs[n_tok:]:
                    r[...] = jnp.zeros(r.shape, r.dtype)
            for r, v in zip(orefs[n_tok:], outs[n_tok:]):
                r[...] += v.astype(r.dtype)

    in_specs = [pl.BlockSpec((tile, a.shape[1]), lambda i: (i, 0)) for a in tok_ins]
    in_specs += [pl.BlockSpec(c.shape, lambda i: (0, 0)) for c in consts]
    out_specs = [pl.BlockSpec((tile, w), lambda i: (i, 0)) for w, _ in tok_outs]
    out_specs += [pl.BlockSpec(s, lambda i: (0, 0)) for s in acc_outs]
    out_shape = [SDS((n, w), dt) for w, dt in tok_outs] + [SDS(s, F32) for s in acc_outs]
    res = pl.pallas_call(
        body, name=name, grid=(n // tile,), in_specs=in_specs, out_specs=out_specs, out_shape=out_shape,
        compiler_params=pltpu.CompilerParams(dimension_semantics=("arbitrary",), vmem_limit_bytes=VMEM_LIMIT),
    )(*tok_ins, *consts)
    return res


MATMUL_VMEM_BUDGET = 40 * 1024 * 1024


def _matmul_tiles(m, n, k, a_bytes, b_bytes, o_bytes, extra_bytes):
    best = None
    for tk in [c for c in (3584, 2816, 2048, 1408, 1024, 512, 256, 128) if k % c == 0]:
        for tm in [c for c in (1408, 1024, 768, 512, 256, 128) if m % c == 0]:
            for tn in [c for c in (1408, 1024, 768, 512, 256, 128) if n % c == 0]:
                need = 2 * tk * (tm * a_bytes + tn * b_bytes) + tm * tn * (2 * o_bytes + 2 * extra_bytes + 8)
                if need > MATMUL_VMEM_BUDGET:
                    continue
                score = (min(tm, 512) * min(tn, 512), tk, tm * tn)
                if best is None or score > best[0]:
                    best = (score, (tm, tn, tk))
    return best[1]


def _matmul(name, a, b, dn, out_dtype, epi=None, tiles=(), rows=()):
    if dn == NN:
        (m, k), n = a.shape, b.shape[1]
    elif dn == NT:
        (m, k), n = a.shape, b.shape[0]
    else:
        (k, m), n = a.shape, b.shape[1]
    n_t, n_r = len(tiles), len(rows)
    tm, tn, tk = _matmul_tiles(m, n, k, a.dtype.itemsize, b.dtype.itemsize, jnp.dtype(out_dtype).itemsize,
                               sum(t.dtype.itemsize for t in tiles))
    nk = k // tk

    def body(a_ref, b_ref, *rest):
        o_ref = rest[n_t + n_r]
        part = _dot(a_ref[...].astype(BF16), b_ref[...].astype(BF16), dn)

        def finish(r):
            if epi is not None:
                r = epi(r, *[e[...] for e in rest[:n_t + n_r]])
            o_ref[...] = r.astype(o_ref.dtype)

        if nk == 1:
            finish(part)
            return
        acc = rest[n_t + n_r + 1]
        kk = pl.program_id(2)

        @pl.when(kk == 0)
        def _():
            acc[...] = part

        @pl.when(kk > 0)
        def _():
            acc[...] += part

        @pl.when(kk == nk - 1)
        def _():
            finish(acc[...])

    a_spec = pl.BlockSpec((tk, tm), lambda i, j, kk: (kk, i)) if dn == TN else pl.BlockSpec((tm, tk), lambda i, j, kk: (i, kk))
    b_spec = pl.BlockSpec((tn, tk), lambda i, j, kk: (j, kk)) if dn == NT else pl.BlockSpec((tk, tn), lambda i, j, kk: (kk, j))
    in_specs = [a_spec, b_spec] + [pl.BlockSpec((tm, tn), lambda i, j, kk: (i, j)) for _ in tiles]
    in_specs += [pl.BlockSpec((1, tn), lambda i, j, kk: (0, j)) for _ in rows]
    return pl.pallas_call(
        body, name=name, grid=(m // tm, n // tn, nk), in_specs=in_specs,
        out_specs=pl.BlockSpec((tm, tn), lambda i, j, kk: (i, j)), out_shape=SDS((m, n), out_dtype),
        scratch_shapes=[pltpu.VMEM((tm, tn), F32)] if nk > 1 else [],
        compiler_params=pltpu.CompilerParams(dimension_semantics=("parallel", "parallel", "arbitrary"),
                                             vmem_limit_bytes=VMEM_LIMIT),
    )(a, b, *tiles, *rows)


def _mean_all(v):
    return jnp.mean(v, axis=-1, keepdims=True)


def _mean_heads(bd):
    return lambda v: _dot01(v, bd) * (1.0 / HD)


def _rms_fwd(x, g, mean):
    return x * lax.rsqrt(mean(x * x) + EPS) * g


def _rms_bwd(x, g, dy, mean):
    r = lax.rsqrt(mean(x * x) + EPS)
    dn = dy * g
    dx = r * dn - x * (r * r * r) * mean(dn * x)
    return dx, jnp.sum(dy * x * r, axis=0, keepdims=True)


def _swap_halves(x):
    w = x.shape[1]
    lane = lax.broadcasted_iota(jnp.int32, x.shape, 1)
    return jnp.where(lane % HD < HD // 2, pltpu.roll(x, w - HD // 2, 1), pltpu.roll(x, HD // 2, 1))


def _lanes(t, w):
    return jnp.tile(t, (1, w // t.shape[1]))


def _rope_fwd(x, cos, sin_signed):
    return x * _lanes(cos, x.shape[1]) + _swap_halves(x) * _lanes(sin_signed, x.shape[1])


def _rope_bwd(dy, cos, sin_signed):
    return dy * _lanes(cos, dy.shape[1]) + _swap_halves(dy * _lanes(sin_signed, dy.shape[1]))


def _bcast_heads(cols):
    return jnp.concatenate([jnp.broadcast_to(c, (c.shape[0], HD)) for c in cols], axis=1)


def _softplus(z):
    return jnp.maximum(z, 0.0) + jnp.log1p(jnp.exp(-jnp.abs(z)))


def _block_diag(w):
    h = np.arange(w) // HD
    return jnp.asarray(h[:, None] == h[None, :], BF16)


def _sb_fwd(qkv):
    s = qkv.shape[0]
    nq = s // QB
    npairs = SB_W // 128

    def body(q_ref, k_ref, v_ref, o_ref, tot_ref, nb_ref):
        p, i = pl.program_id(0), pl.program_id(1)
        row = lax.broadcasted_iota(jnp.int32, (QB, QB), 0)
        col = lax.broadcasted_iota(jnp.int32, (QB, QB), 1)
        later_of = (row > col).astype(BF16)
        q = q_ref[...]

        def step(c):
            kb_hi, _, tots, outs = c
            tots, outs = list(tots), list(outs)
            for cc in range(SB_NB):
                kb = kb_hi - cc
                off = pl.multiple_of(jnp.maximum(kb, 0) * QB, QB)
                kblk, vblk = k_ref[pl.ds(off, QB), :], v_ref[pl.ds(off, QB), :]
                mask = ((col < row) | (kb < i)) & (kb >= 0)
                for hh in range(2):
                    sl = slice(HD * hh, HD * hh + HD)
                    z = _dot(q[:, sl], kblk[:, sl], NT) * SCALE
                    sp = _softplus(z)
                    lf = jnp.where(mask, -sp, 0.0)
                    later = tots[hh] + _dot01(lf, later_of)
                    w = jnp.where(mask, jnp.exp(z - sp + later), 0.0)
                    outs[hh] = outs[hh] + _dot(w.astype(BF16), vblk[:, sl])
                    tots[hh] = tots[hh] + jnp.sum(lf, axis=1, keepdims=True)
            alive = jnp.maximum(jnp.max(tots[0]), jnp.max(tots[1]))
            return kb_hi - SB_NB, alive, tuple(tots), tuple(outs)

        zt, zo = jnp.zeros((QB, 1), F32), jnp.zeros((QB, HD), F32)
        kb_hi, _, tots, outs = lax.while_loop(lambda c: (c[0] >= 0) & (c[1] > EXHAUSTED), step,
                                              (i, jnp.float32(0.0), (zt, zt), (zo, zo)))
        o_ref[...] = jnp.concatenate(outs, axis=1).astype(o_ref.dtype)
        tot_ref[...] = _bcast_heads(tots)
        nb_ref[p, i] = (i - kb_hi) // SB_NB

    whole = lambda off: pl.BlockSpec((s, 128), lambda p, i: (0, off + p), pipeline_mode=pl.Buffered(1))
    tile = pl.BlockSpec((QB, 128), lambda p, i: (i, p))
    return pl.pallas_call(
        body, name="sb_fwd", grid=(npairs, nq),
        in_specs=[tile, whole(npairs), whole(2 * npairs)],
        out_specs=[tile, tile, pl.BlockSpec(memory_space=pltpu.SMEM)],
        out_shape=[SDS((s, SB_W), BF16), SDS((s, SB_W), F32), SDS((npairs, nq), jnp.int32)],
        compiler_params=pltpu.CompilerParams(dimension_semantics=("arbitrary", "arbitrary"), vmem_limit_bytes=VMEM_LIMIT),
    )(qkv, qkv, qkv)


def _sb_bwd(qkv, do, tot, nblk):
    s = qkv.shape[0]
    nq = s // QB
    npairs = SB_W // 128

    def body(nb_ref, q_ref, k_ref, v_ref, do_ref, tot_ref, dq_ref, dk_ref, dv_ref):
        p, i = pl.program_id(0), pl.program_id(1)

        @pl.when(i == 0)
        def _():
            dk_ref[...] = jnp.zeros(dk_ref.shape, F32)
            dv_ref[...] = jnp.zeros(dv_ref.shape, F32)

        row = lax.broadcasted_iota(jnp.int32, (QB, QB), 0)
        col = lax.broadcasted_iota(jnp.int32, (QB, QB), 1)
        upto = (row <= col).astype(BF16)
        before = (row < col).astype(BF16)
        q, dout, tt = q_ref[...], do_ref[...], tot_ref[...]
        n = nb_ref[p, i]

        def step(t, c):
            pres, gpres, dqs = (list(v) for v in c)
            kb_hi = i - (n - 1 - t) * SB_NB
            for cc in range(SB_NB - 1, -1, -1):
                kb = kb_hi - cc
                off = pl.multiple_of(jnp.maximum(kb, 0) * QB, QB)
                kblk, vblk = k_ref[pl.ds(off, QB), :], v_ref[pl.ds(off, QB), :]
                mask = ((col < row) | (kb < i)) & (kb >= 0)
                dks, dvs = [], []
                for hh in range(2):
                    sl = slice(HD * hh, HD * hh + HD)
                    z = _dot(q[:, sl], kblk[:, sl], NT) * SCALE
                    sp = _softplus(z)
                    lf = jnp.where(mask, -sp, 0.0)
                    later = tt[:, HD * hh:HD * hh + 1] - (pres[hh] + _dot01(lf, upto))
                    w = jnp.where(mask, jnp.exp(z - sp + later), 0.0)
                    beta = jnp.exp(z - sp)
                    g = _dot(dout[:, sl], vblk[:, sl], NT) * w
                    g_far = gpres[hh] + _dot(g.astype(BF16), before)
                    dz = (jnp.where(mask, g * (1.0 - beta) - beta * g_far, 0.0) * SCALE).astype(BF16)
                    dqs[hh] = dqs[hh] + _dot(dz, kblk[:, sl])
                    dks.append(_dot(dz, q[:, sl], TN))
                    dvs.append(_dot(w.astype(BF16), dout[:, sl], TN))
                    pres[hh] = pres[hh] + jnp.sum(lf, axis=1, keepdims=True)
                    gpres[hh] = gpres[hh] + jnp.sum(g, axis=1, keepdims=True)
                dk_ref[pl.ds(off, QB), :] += jnp.concatenate(dks, axis=1)
                dv_ref[pl.ds(off, QB), :] += jnp.concatenate(dvs, axis=1)
            return tuple(pres), tuple(gpres), tuple(dqs)

        zt, zo = jnp.zeros((QB, 1), F32), jnp.zeros((QB, HD), F32)
        _, _, dqs = lax.fori_loop(0, n, step, ((zt, zt), (zt, zt), (zo, zo)))
        dq_ref[...] = jnp.concatenate(dqs, axis=1)

    whole_in = lambda off: pl.BlockSpec((s, 128), lambda p, i: (0, off + p), pipeline_mode=pl.Buffered(1))
    whole_out = pl.BlockSpec((s, 128), lambda p, i: (0, p), pipeline_mode=pl.Buffered(1))
    tile = pl.BlockSpec((QB, 128), lambda p, i: (i, p))
    return pl.pallas_call(
        body, name="sb_bwd", grid=(npairs, nq),
        in_specs=[pl.BlockSpec(memory_space=pltpu.SMEM), tile, whole_in(npairs), whole_in(2 * npairs), tile, tile],
        out_specs=[tile, whole_out, whole_out],
        out_shape=[SDS((s, SB_W), F32)] * 3,
        compiler_params=pltpu.CompilerParams(dimension_semantics=("arbitrary", "arbitrary"), vmem_limit_bytes=VMEM_LIMIT),
    )(nblk, qkv, qkv, qkv, do, tot)


def _win_masks(has_prev):
    row = lax.broadcasted_iota(jnp.int32, (QB, QB), 0)
    col = lax.broadcasted_iota(jnp.int32, (QB, QB), 1)
    return col <= row, (col >= row) & has_prev


def _dsa_fwd(q, k, v, dil):
    s = q.shape[0]
    n = s // dil
    nb = n // QB
    q, k, v = (t.reshape(n, dil * DSA_OUT_W) for t in (q, k, v))

    def body(q_ref, kc_ref, kp_ref, vc_ref, vp_ref, o_ref, lse_ref):
        m_cur, m_prev = _win_masks(pl.program_id(1) > 0)
        outs, lses = [], []
        for hh in range(2):
            sl = slice(HD * hh, HD * hh + HD)
            qh = q_ref[:, sl]
            sc = jnp.where(m_cur, _dot(qh, kc_ref[:, sl], NT) * SCALE, NEG)
            sp = jnp.where(m_prev, _dot(qh, kp_ref[:, sl], NT) * SCALE, NEG)
            m = jnp.maximum(jnp.max(sc, axis=1, keepdims=True), jnp.max(sp, axis=1, keepdims=True))
            pc, pp = jnp.exp(sc - m), jnp.exp(sp - m)
            den = jnp.sum(pc, axis=1, keepdims=True) + jnp.sum(pp, axis=1, keepdims=True)
            outs.append((_dot(pc.astype(BF16), vc_ref[:, sl]) + _dot(pp.astype(BF16), vp_ref[:, sl])) / den)
            lses.append(m + jnp.log(den))
        o_ref[...] = jnp.concatenate(outs, axis=1)
        lse_ref[...] = _bcast_heads(lses)

    cur = pl.BlockSpec((QB, 128), lambda c, i, p: (i, 2 * c + p))
    prev = pl.BlockSpec((QB, 128), lambda c, i, p: (jnp.maximum(i - 1, 0), 2 * c + p))
    o, lse = pl.pallas_call(
        body, name=f"dsa_fwd_d{dil}", grid=(dil, nb, 2), in_specs=[cur, cur, prev, cur, prev], out_specs=[cur, cur],
        out_shape=[SDS((n, dil * DSA_OUT_W), F32)] * 2,
        compiler_params=pltpu.CompilerParams(dimension_semantics=("parallel", "parallel", "parallel")),
    )(q, k, k, v, v)
    return o.reshape(s, DSA_OUT_W), lse.reshape(s, DSA_OUT_W)


def _dsa_bwd(q, k, v, do, cc, lse, dil):
    s = q.shape[0]
    n = s // dil
    nb = n // QB
    q, k, v, do, cc, lse = (t.reshape(n, dil * DSA_OUT_W) for t in (q, k, v, do, cc, lse))

    def body(qj_ref, qn_ref, kp_ref, kj_ref, vp_ref, vj_ref, doj_ref, don_ref, cj_ref, cn_ref, lj_ref, ln_ref,
             dq_ref, dk_ref, dv_ref):
        j = pl.program_id(1)
        m_cur, m_prev = _win_masks(j > 0)
        _, m_next = _win_masks(j + 1 < nb)
        dqs, dks, dvs = [], [], []
        for hh in range(2):
            sl = slice(HD * hh, HD * hh + HD)
            one = slice(HD * hh, HD * hh + 1)
            qj, qn, kp, kj, vp, vj = (r[:, sl] for r in (qj_ref, qn_ref, kp_ref, kj_ref, vp_ref, vj_ref))
            doj, don = doj_ref[:, sl], don_ref[:, sl]

            def dscore(qq, kk, vv, dd, c_ref, l_ref, mask):
                prob = jnp.where(mask, jnp.exp(_dot(qq, kk, NT) * SCALE - l_ref[:, one]), 0.0)
                return prob, (prob * (_dot(dd, vv, NT) + c_ref[:, one]) * SCALE).astype(BF16)

            _, ds_a = dscore(qj, kp, vp, doj, cj_ref, lj_ref, m_prev)
            p_b, ds_b = dscore(qj, kj, vj, doj, cj_ref, lj_ref, m_cur)
            p_c, ds_c = dscore(qn, kj, vj, don, cn_ref, ln_ref, m_next)
            dqs.append(_dot(ds_a, kp) + _dot(ds_b, kj))
            dks.append(_dot(ds_b, qj, TN) + _dot(ds_c, qn, TN))
            dvs.append(_dot(p_b.astype(BF16), doj, TN) + _dot(p_c.astype(BF16), don, TN))
        dq_ref[...] = jnp.concatenate(dqs, axis=1)
        dk_ref[...] = jnp.concatenate(dks, axis=1)
        dv_ref[...] = jnp.concatenate(dvs, axis=1)

    cur = pl.BlockSpec((QB, 128), lambda c, j, p: (j, 2 * c + p))
    prev = pl.BlockSpec((QB, 128), lambda c, j, p: (jnp.maximum(j - 1, 0), 2 * c + p))
    nxt = pl.BlockSpec((QB, 128), lambda c, j, p: (jnp.minimum(j + 1, nb - 1), 2 * c + p))
    dq, dk, dv = pl.pallas_call(
        body, name=f"dsa_bwd_d{dil}", grid=(dil, nb, 2),
        in_specs=[cur, nxt, prev, cur, prev, cur, cur, nxt, cur, nxt, cur, nxt], out_specs=[cur, cur, cur],
        out_shape=[SDS((n, dil * DSA_OUT_W), F32)] * 3,
        compiler_params=pltpu.CompilerParams(dimension_semantics=("parallel", "parallel", "parallel")),
    )(q, q, k, k, v, v, do, do, cc, cc, lse, lse)
    return tuple(t.reshape(s, DSA_OUT_W) for t in (dq, dk, dv))


def _ffn_fwd(tag, x, gain, w13, w2):
    n = _tokmap(f"{tag}_norm", lambda xv, g: _rms_fwd(xv, g, _mean_all), [x], [gain], [(D, BF16)])[0]
    ab = _matmul(f"{tag}_up", n, w13, NN, BF16)

    def gate(abv):
        a, b = abv[:, :D_FF].astype(F32), abv[:, D_FF:].astype(F32)
        return a * jax.nn.sigmoid(a) * b

    h = _tokmap(f"{tag}_gate", gate, [ab], [], [(D_FF, BF16)], tile=256)[0]
    y = _matmul(f"{tag}_down", h, w2, NN, F32, epi=lambda acc, res: res + 0.5 * acc, tiles=[x])
    return y, (n, ab)


def _ffn_bwd(tag, x, gain, w13, w2, saved, dy):
    n, ab = saved
    dh = _matmul(f"{tag}_bwd_dh", dy, w2, NT, BF16, epi=lambda acc: 0.5 * acc)

    def gate_bwd(abv, dhv):
        a, b, dhf = abv[:, :D_FF].astype(F32), abv[:, D_FF:].astype(F32), dhv.astype(F32)
        sg = jax.nn.sigmoid(a)
        silu = a * sg
        da = dhf * b * (sg * (1.0 + a * (1.0 - sg)))
        return jnp.concatenate([da, dhf * silu], axis=1), silu * b

    dab, h = _tokmap(f"{tag}_bwd_gate", gate_bwd, [ab, dh], [], [(2 * D_FF, BF16), (D_FF, BF16)], tile=256)
    dw2 = _matmul(f"{tag}_bwd_dw2", h, dy, TN, F32, epi=lambda acc: 0.5 * acc)
    dw13 = _matmul(f"{tag}_bwd_dw13", n, dab, TN, F32)
    dn = _matmul(f"{tag}_bwd_dn", dab, w13, NT, F32)

    def norm_bwd(xv, dnv, dyv, g):
        dx, dg = _rms_bwd(xv, g, dnv, _mean_all)
        return dx + dyv, dg

    dx, dgain = _tokmap(f"{tag}_bwd_norm", norm_bwd, [x, dn, dy], [gain], [(D, F32)], [(1, D)])
    return dx, dgain, dw13, dw2


def _rope_tables(s):
    half = HD // 2
    inv_freq = jnp.power(10000.0, -jnp.arange(half, dtype=F32) / half)
    ang = jnp.arange(s).astype(F32)[:, None] * inv_freq[None, :]
    cos, sin = jnp.cos(ang), jnp.sin(ang)
    return jnp.tile(jnp.concatenate([cos, cos], axis=1), (1, 2)), jnp.tile(jnp.concatenate([-sin, sin], axis=1), (1, 2))


def _local_step(x, mem, tgt, w, sm):
    s = x.shape[0]
    assert s % (QB * max(DSA_DILS)) == 0
    w13_1 = jnp.concatenate([w["ffn1_w1"], w["ffn1_w3"]], axis=1)
    w13_2 = jnp.concatenate([w["ffn2_w1"], w["ffn2_w3"]], axis=1)
    w_all = jnp.concatenate([w["w_in"], w["w_gate"]], axis=1)
    c_sb, c_dsa, c_qm = 3 * SB_W, 3 * SB_W + 3 * DSA_W, 4096
    cos, sin = _rope_tables(s)
    bd768, bd256 = _block_diag(DSA_W), _block_diag(MEM_W)
    gq_dsa, gk_dsa = jnp.tile(sm["qn_dsa"], (1, DSA_W // HD)), jnp.tile(sm["kn_dsa"], (1, DSA_W // HD))
    gq_mem, gk_mem = jnp.tile(sm["qn_mem"], (1, MEM_W // HD)), jnp.tile(sm["kn_mem"], (1, MEM_W // HD))
    wb_sb, wb_dsa, wb_mem = w["w_branch_sb"], w["w_branch_dsa"], w["w_branch_mem"]

    x1, ffn1_saved = _ffn_fwd("ffn1", x, sm["ffn1_norm"], w13_1, w["ffn1_w2"])
    hmix = _tokmap("mix_norm", lambda xv, g: _rms_fwd(xv, g, _mean_all), [x1], [sm["mix_norm"]], [(D, BF16)])[0]
    qkv_sb = _matmul("proj_sb", hmix, w_all[:, :c_sb], NN, BF16)
    qkv_dsa = _matmul("proj_dsa", hmix, w_all[:, c_sb:c_dsa], NN, BF16)
    q_mem = _matmul("proj_qmem", hmix, w_all[:, c_dsa:c_qm], NN, BF16)
    gpre = _matmul("proj_gate", hmix, w_all[:, c_qm:], NN, BF16, epi=lambda acc, b: acc + b, rows=[sm["b_gate"]])

    o_sb, sb_tot, sb_nblk = _sb_fwd(qkv_sb)

    def dsa_prep(qkv, cs, sn, gq, gk, bd):
        mean = _mean_heads(bd)
        qn = _rope_fwd(_rms_fwd(qkv[:, :DSA_W].astype(F32), gq, mean), cs, sn)
        kn = _rope_fwd(_rms_fwd(qkv[:, DSA_W:2 * DSA_W].astype(F32), gk, mean), cs, sn)
        v = qkv[:, 2 * DSA_W:]
        outs = []
        for t in (qn, kn, v):
            outs += [t[:, DSA_OUT_W * g:DSA_OUT_W * (g + 1)] for g in range(3)]
        return outs

    dsa_in = _tokmap("dsa_prep", dsa_prep, [qkv_dsa, cos, sin], [gq_dsa, gk_dsa, bd768], [(DSA_OUT_W, BF16)] * 9, tile=256)
    dsa_q, dsa_k, dsa_v = dsa_in[0:3], dsa_in[3:6], dsa_in[6:9]
    dsa_o, dsa_lse = zip(*[_dsa_fwd(dsa_q[g], dsa_k[g], dsa_v[g], DSA_DILS[g]) for g in range(3)])

    def alphas(l0, l1, l2):
        m = jnp.maximum(jnp.maximum(l0, l1), l2)
        e = [jnp.exp(l - m) for l in (l0, l1, l2)]
        tot = e[0] + e[1] + e[2]
        return [t / tot for t in e]

    def dsa_mix(o0, o1, o2, l0, l1, l2):
        a = alphas(l0, l1, l2)
        return a[0] * o0 + a[1] * o1 + a[2] * o2

    o_dsa = _tokmap("dsa_mix", dsa_mix, [*dsa_o, *dsa_lse], [], [(DSA_OUT_W, BF16)])[0]

    def mem_kv(memv, g, wkv, gk, bd):
        kv = _dot(_rms_fwd(memv, g, _mean_all).astype(BF16), wkv)
        return _rms_fwd(kv[:, :MEM_W], gk, _mean_heads(bd)), kv[:, MEM_W:]

    km, vm = _tokmap("mem_kv", mem_kv, [mem], [sm["mem_norm"], w["w_mem_kv"], gk_mem, bd256], [(MEM_W, BF16)] * 2)

    def mem_probs(qv, kmv, gq, bd):
        qn = _rms_fwd(qv.astype(F32), gq, _mean_heads(bd)).astype(BF16)
        ps = []
        for h in range(MEM_W // HD):
            sl = slice(HD * h, HD * h + HD)
            sc = _dot(qn[:, sl], kmv[:, sl], NT) * SCALE
            e = jnp.exp(sc - jnp.max(sc, axis=1, keepdims=True))
            ps.append(e / jnp.sum(e, axis=1, keepdims=True))
        return qn, ps

    def mem_attn(qv, kmv, vmv, gq, bd):
        _, ps = mem_probs(qv, kmv, gq, bd)
        return jnp.concatenate([_dot(p.astype(BF16), vmv[:, HD * h:HD * h + HD]) for h, p in enumerate(ps)], axis=1)

    o_mem = _tokmap("mem_attn", mem_attn, [q_mem], [km, vm, gq_mem, bd256], [(MEM_W, BF16)])[0]

    def merge(osb, odsa, omem, gp, w_sb, w_dsa, w_mem):
        gates = jax.nn.sigmoid(gp.astype(F32))
        ys = (_dot(osb, w_sb), _dot(odsa, w_dsa), _dot(omem, w_mem))
        return gates, ys, gates[:, :D] * ys[0] + gates[:, D:2 * D] * ys[1] + gates[:, 2 * D:] * ys[2]

    merged = _tokmap("merge", lambda *a: merge(*a)[2], [o_sb, o_dsa, o_mem, gpre], [wb_sb, wb_dsa, wb_mem], [(D, BF16)],
                     tile=256)[0]
    x2 = _matmul("out_proj", merged, w["w_out"], NN, F32, epi=lambda acc, res: res + acc, tiles=[x1])
    y, ffn2_saved = _ffn_fwd("ffn2", x2, sm["ffn2_norm"], w13_2, w["ffn2_w2"])

    def loss_fn(yv, tv):
        e = yv - tv
        part = 0.5 * jnp.sum(jnp.mean(e * e, axis=1, keepdims=True), axis=0, keepdims=True)
        return e * (1.0 / D), jnp.broadcast_to(part, (1, 128))

    dy, loss = _tokmap("loss", loss_fn, [y, tgt], [], [(D, F32)], [(1, 128)])

    gw, gs = {}, {}
    dx2, gs["ffn2_norm"], dw13, gw["ffn2_w2"] = _ffn_bwd("ffn2", x2, sm["ffn2_norm"], w13_2, w["ffn2_w2"], ffn2_saved, dy)
    gw["ffn2_w1"], gw["ffn2_w3"] = dw13[:, :D_FF], dw13[:, D_FF:]
    dmerged = _matmul("out_proj_bwd_dx", dx2, w["w_out"], NT, BF16)
    gw["w_out"] = _matmul("out_proj_bwd_dw", merged, dx2, TN, F32)

    def merge_bwd(osb, odsa, omem, gp, dm, w_sb, w_dsa, w_mem):
        gates, ys, _ = merge(osb, odsa, omem, gp, w_sb, w_dsa, w_mem)
        dmf = dm.astype(F32)
        dgp, dos, dws = [], [], []
        for b, (ov, wv) in enumerate(((osb, w_sb), (odsa, w_dsa), (omem, w_mem))):
            gb = gates[:, D * b:D * (b + 1)]
            dgp.append(dmf * ys[b] * gb * (1.0 - gb))
            dyb = (dmf * gb).astype(BF16)
            dos.append(_dot(dyb, wv, NT))
            dws.append(_dot(ov, dyb, TN))
        dgp = jnp.concatenate(dgp, axis=1)
        return dos[0], dos[1], dos[2], dgp, dws[0], dws[1], dws[2], jnp.sum(dgp, axis=0, keepdims=True)

    do_sb, do_dsa, do_mem, dgpre, gw["w_branch_sb"], gw["w_branch_dsa"], gw["w_branch_mem"], gs["b_gate"] = _tokmap(
        "merge_bwd", merge_bwd, [o_sb, o_dsa, o_mem, gpre, dmerged], [wb_sb, wb_dsa, wb_mem],
        [(SB_W, BF16), (DSA_OUT_W, F32), (MEM_W, BF16), (3 * D, BF16)],
        [(SB_W, D), (DSA_OUT_W, D), (MEM_W, D), (1, 3 * D)], tile=256)

    dq_sb, dk_sb, dv_sb = _sb_bwd(qkv_sb, do_sb, sb_tot, sb_nblk)

    def dsa_mix_bwd(o0, o1, o2, l0, l1, l2, dov, bd):
        a = alphas(l0, l1, l2)
        omix = a[0] * o0 + a[1] * o1 + a[2] * o2
        dot_o = _dot01(dov * omix, bd)
        return [dov * t for t in a] + [-t * dot_o for t in a]

    mixb = _tokmap("dsa_mix_bwd", dsa_mix_bwd, [*dsa_o, *dsa_lse, do_dsa], [bd256],
                   [(DSA_OUT_W, BF16)] * 3 + [(DSA_OUT_W, F32)] * 3)
    dsa_d = [_dsa_bwd(dsa_q[g], dsa_k[g], dsa_v[g], mixb[g], mixb[3 + g], dsa_lse[g], DSA_DILS[g]) for g in range(3)]

    def dsa_prep_bwd(qkv, cs, sn, *rest):
        dqs, dks, dvs, (gq, gk, bd) = rest[0:3], rest[3:6], rest[6:9], rest[9:]
        mean = _mean_heads(bd)
        dq, dgq = _rms_bwd(qkv[:, :DSA_W].astype(F32), gq, _rope_bwd(jnp.concatenate(dqs, axis=1), cs, sn), mean)
        dk, dgk = _rms_bwd(qkv[:, DSA_W:2 * DSA_W].astype(F32), gk, _rope_bwd(jnp.concatenate(dks, axis=1), cs, sn), mean)
        return jnp.concatenate([dq, dk] + list(dvs), axis=1), dgq, dgk

    dqkv_dsa, dgq_dsa, dgk_dsa = _tokmap(
        "dsa_prep_bwd", dsa_prep_bwd,
        [qkv_dsa, cos, sin] + [dsa_d[g][t] for t in range(3) for g in range(3)], [gq_dsa, gk_dsa, bd768],
        [(3 * DSA_W, BF16)], [(1, DSA_W), (1, DSA_W)], tile=256)
    gs["qn_dsa"] = dgq_dsa.reshape(DSA_W // HD, HD).sum(axis=0, keepdims=True)
    gs["kn_dsa"] = dgk_dsa.reshape(DSA_W // HD, HD).sum(axis=0, keepdims=True)

    def mem_attn_bwd(qv, dov, kmv, vmv, gq, bd):
        qn, ps = mem_probs(qv, kmv, gq, bd)
        dqn, dkm, dvm = [], [], []
        for h, p in enumerate(ps):
            sl = slice(HD * h, HD * h + HD)
            dp = _dot(dov[:, sl], vmv[:, sl], NT)
            ds = (p * (dp - jnp.sum(p * dp, axis=1, keepdims=True)) * SCALE).astype(BF16)
            dqn.append(_dot(ds, kmv[:, sl]))
            dkm.append(_dot(ds, qn[:, sl], TN))
            dvm.append(_dot(p.astype(BF16), dov[:, sl], TN))
        dq, dgq = _rms_bwd(qv.astype(F32), gq, jnp.concatenate(dqn, axis=1), _mean_heads(bd))
        return dq, jnp.concatenate(dkm, axis=1), jnp.concatenate(dvm, axis=1), dgq

    dq_mem, dkm, dvm, dgq_mem = _tokmap("mem_attn_bwd", mem_attn_bwd, [q_mem, do_mem], [km, vm, gq_mem, bd256],
                                        [(MEM_W, BF16)], [(MEM_LEN, MEM_W), (MEM_LEN, MEM_W), (1, MEM_W)])
    gs["qn_mem"] = dgq_mem.reshape(MEM_W // HD, HD).sum(axis=0, keepdims=True)

    def mem_kv_bwd(memv, dkmv, dvmv, g, wkv, gk, bd):
        memn = _rms_fwd(memv, g, _mean_all).astype(BF16)
        kv = _dot(memn, wkv)
        dk, dgk = _rms_bwd(kv[:, :MEM_W], gk, dkmv, _mean_heads(bd))
        dkv = jnp.concatenate([dk, dvmv], axis=1).astype(BF16)
        _, dg = _rms_bwd(memv, g, _dot(dkv, wkv, NT), _mean_all)
        return _dot(memn, dkv, TN), dg, dgk

    gw["w_mem_kv"], gs["mem_norm"], dgk_mem = _tokmap(
        "mem_kv_bwd", mem_kv_bwd, [mem, dkm, dvm], [sm["mem_norm"], w["w_mem_kv"], gk_mem, bd256], [],
        [(D, 2 * MEM_W), (1, D), (1, MEM_W)])
    gs["kn_mem"] = dgk_mem.reshape(MEM_W // HD, HD).sum(axis=0, keepdims=True)

    dall = jnp.concatenate([dq_sb.astype(BF16), dk_sb.astype(BF16), dv_sb.astype(BF16), dqkv_dsa, dq_mem, dgpre], axis=1)
    dhmix = _matmul("proj_bwd_dx", dall, w_all, NT, F32)
    dw_all = _matmul("proj_bwd_dw", hmix, dall, TN, F32)
    gw["w_in"], gw["w_gate"] = dw_all[:, :c_qm], dw_all[:, c_qm:]

    def mix_norm_bwd(xv, dnv, dyv, g):
        dx, dg = _rms_bwd(xv, g, dnv, _mean_all)
        return dx + dyv, dg

    dx1, gs["mix_norm"] = _tokmap("mix_norm_bwd", mix_norm_bwd, [x1, dhmix, dx2], [sm["mix_norm"]], [(D, F32)], [(1, D)])
    gx, gs["ffn1_norm"], dw13, gw["ffn1_w2"] = _ffn_bwd("ffn1", x, sm["ffn1_norm"], w13_1, w["ffn1_w2"], ffn1_saved, dx1)
    gw["ffn1_w1"], gw["ffn1_w3"] = dw13[:, :D_FF], dw13[:, D_FF:]
    return loss, gx, gw, gs


def _shard_shape(shape, axis):
    return (shape[0] // N_CHIPS, shape[1]) if axis == 0 else (shape[0], shape[1] // N_CHIPS)


PACK_ROWS = sum(sh[0] * sh[1] // N_CHIPS // D for _, sh, _ in SHARDED)


def _pack(shards):
    return jnp.concatenate([shards[n].reshape(-1, D) for n, _, _ in SHARDED], axis=0)


def _unpack(pack):
    out, r = {}, 0
    for n, sh, ax in SHARDED:
        ss = _shard_shape(sh, ax)
        rows = ss[0] * ss[1] // D
        out[n] = pack[r:r + rows].reshape(ss)
        r += rows
    return out


def _full_from_packs(packs):
    per_chip = [_unpack(packs[c]) for c in range(N_CHIPS)]
    return {n: jnp.concatenate([per_chip[c][n] for c in range(N_CHIPS)], axis=ax) for n, _, ax in SHARDED}


def _packs_from_full(full):
    packs = []
    for c in range(N_CHIPS):
        shards = {}
        for n, sh, ax in SHARDED:
            ss = _shard_shape(sh, ax)
            shards[n] = lax.slice_in_dim(full[n], c * ss[ax], (c + 1) * ss[ax], axis=ax)
        packs.append(_pack(shards))
    return jnp.stack(packs)


SMALL_USED = sum(n for _, n in SMALL)


def _pack_small(d, loss=None):
    parts = [d[n].reshape(-1) for n, _ in SMALL]
    parts.append(jnp.zeros((1,), F32) if loss is None else loss.reshape(1))
    parts.append(jnp.zeros((SMALL_ROWS * D - SMALL_USED - 1,), F32))
    return jnp.concatenate(parts).reshape(SMALL_ROWS, D)


def _unpack_small(v):
    flat, out, r = v.reshape(-1), {}, 0
    for n, k in SMALL:
        out[n] = flat[r:r + k]
        r += k
    return out, flat[r]


def _place():
    return lax.axis_index("x"), lax.axis_index("y"), lax.axis_index("c")


def _other_chips(x, y):
    return [(1 - x, y), (x, 1 - y), (1 - x, 1 - y)]


HBM_SPEC = pl.BlockSpec(memory_space=pl.ANY)


def _all_gather_chips(pack):
    def body(src, out, send_sems, recv_sems, local_sem):
        x, y, c = _place()
        me = 2 * x + y
        mine = pltpu.make_async_copy(src, out.at[me], local_sem)
        mine.start()
        copies = [pltpu.make_async_remote_copy(src_ref=src, dst_ref=out.at[me], send_sem=send_sems.at[k], recv_sem=recv_sems.at[k],
                                               device_id=(px, py, c), device_id_type=MESH)
                  for k, (px, py) in enumerate(_other_chips(x, y))]
        for cp in copies:
            cp.start()
        for cp in copies:
            cp.wait()
        mine.wait()

    return pl.pallas_call(
        body, name="weights_all_gather", in_specs=[HBM_SPEC], out_specs=HBM_SPEC,
        out_shape=SDS((N_CHIPS,) + pack.shape, pack.dtype),
        scratch_shapes=[pltpu.SemaphoreType.DMA((3,)), pltpu.SemaphoreType.DMA((3,)), pltpu.SemaphoreType.DMA],
    )(pack)


def _scatter_to_chips(packs):
    def body(src, out, send_sems, recv_sems):
        x, y, c = _place()
        copies = [pltpu.make_async_remote_copy(src_ref=src.at[2 * px + py], dst_ref=out.at[k], send_sem=send_sems.at[k],
                                               recv_sem=recv_sems.at[k], device_id=(px, py, c), device_id_type=MESH)
                  for k, (px, py) in enumerate(_other_chips(x, y))]
        for cp in copies:
            cp.start()
        for cp in copies:
            cp.wait()

    return pl.pallas_call(
        body, name="grads_scatter", in_specs=[HBM_SPEC], out_specs=HBM_SPEC,
        out_shape=SDS((3,) + packs.shape[1:], packs.dtype),
        scratch_shapes=[pltpu.SemaphoreType.DMA((3,)), pltpu.SemaphoreType.DMA((3,))],
    )(packs)


def _swap_with_sibling(v):
    def body(src, out, send_sem, recv_sem):
        x, y, c = _place()
        cp = pltpu.make_async_remote_copy(src_ref=src, dst_ref=out, send_sem=send_sem, recv_sem=recv_sem,
                                          device_id=(x, y, 1 - c), device_id_type=MESH)
        cp.start()
        cp.wait()

    return pl.pallas_call(
        body, name="grads_swap_cores", in_specs=[HBM_SPEC], out_specs=HBM_SPEC, out_shape=SDS(v.shape, v.dtype),
        scratch_shapes=[pltpu.SemaphoreType.DMA, pltpu.SemaphoreType.DMA],
    )(v)


def _all_reduce_small(v):
    n_dev = 8

    def body(v_ref, out_ref, land, send_sems, recv_sems):
        x, y, c = _place()
        me = 4 * x + 2 * y + c
        land[me] = v_ref[...]
        copies = []
        for k in range(1, n_dev):
            peer = (x ^ (k >> 2), y ^ ((k >> 1) & 1), c ^ (k & 1))
            copies.append(pltpu.make_async_remote_copy(src_ref=v_ref, dst_ref=land.at[me], send_sem=send_sems.at[k - 1],
                                                       recv_sem=recv_sems.at[k - 1], device_id=peer, device_id_type=MESH))
        for cp in copies:
            cp.start()
        for cp in copies:
            cp.wait()
        acc = land[0]
        for d in range(1, n_dev):
            acc = acc + land[d]
        out_ref[...] = acc

    return pl.pallas_call(
        body, name="small_all_reduce", in_specs=[pl.BlockSpec(memory_space=pltpu.VMEM)],
        out_specs=pl.BlockSpec(memory_space=pltpu.VMEM), out_shape=SDS(v.shape, v.dtype),
        scratch_shapes=[pltpu.VMEM((n_dev,) + v.shape, v.dtype), pltpu.SemaphoreType.DMA((n_dev - 1,)),
                        pltpu.SemaphoreType.DMA((n_dev - 1,))],
    )(v)


def _adamw(g, wv, m, v):
    m = ADAM_B1 * m + (1.0 - ADAM_B1) * g
    v = ADAM_B2 * v + (1.0 - ADAM_B2) * (g * g)
    m_hat = m / (1.0 - ADAM_B1 ** ADAM_STEP)
    v_hat = v / (1.0 - ADAM_B2 ** ADAM_STEP)
    delta = -ADAM_LR * (m_hat / (jnp.sqrt(v_hat) + ADAM_EPS) + ADAM_WD * wv)
    return delta, m, v


def kernel(x, mem, ffn1_norm, ffn1_w1, ffn1_w3, ffn1_w2, mix_norm, mem_norm, w_in, w_mem_kv, qn_dsa, kn_dsa, qn_mem, kn_mem, w_branch_sb, w_branch_dsa, w_branch_mem, w_gate, b_gate, w_out, ffn2_norm, ffn2_w1, ffn2_w3, ffn2_w2, loss_target, m_ffn1_norm, m_ffn1_w1, m_ffn1_w3, m_ffn1_w2, m_mix_norm, m_mem_norm, m_w_in, m_w_mem_kv, m_qn_dsa, m_kn_dsa, m_qn_mem, m_kn_mem, m_w_branch_sb, m_w_branch_dsa, m_w_branch_mem, m_w_gate, m_b_gate, m_w_out, m_ffn2_norm, m_ffn2_w1, m_ffn2_w3, m_ffn2_w2, v_ffn1_norm, v_ffn1_w1, v_ffn1_w3, v_ffn1_w2, v_mix_norm, v_mem_norm, v_w_in, v_w_mem_kv, v_qn_dsa, v_kn_dsa, v_qn_mem, v_kn_mem, v_w_branch_sb, v_w_branch_dsa, v_w_branch_mem, v_w_gate, v_b_gate, v_w_out, v_ffn2_norm, v_ffn2_w1, v_ffn2_w3, v_ffn2_w2):
    given = dict(locals())
    wts = {n: given[n][0] for n in WEIGHTS}
    moms = {n: given["m_" + n][0] for n in WEIGHTS}
    vars_ = {n: given["v_" + n][0] for n in WEIGHTS}

    w_pack = _pack({n: wts[n] for n, _, _ in SHARDED})
    full = _full_from_packs(_all_gather_chips(w_pack.astype(BF16)))
    small = {n: wts[n].reshape(1, -1) for n, _ in SMALL}

    loss, gx, gw, gs = _local_step(x[0], mem[0], loss_target[0], full, small)

    g_packs = _packs_from_full(gw)
    x_i, y_i, _ = _place()
    landed = _scatter_to_chips(g_packs.astype(BF16))
    own = lax.dynamic_index_in_dim(g_packs, 2 * x_i + y_i, 0, keepdims=False)
    half = _tokmap("grads_sum_chips", lambda a, b0, b1, b2: ((a + b0.astype(F32)) + b1.astype(F32)) + b2.astype(F32),
                   [own, landed[0], landed[1], landed[2]], [], [(D, F32)])[0]
    other = _swap_with_sibling(half)

    def update(hv, ov, wv, mv, vv):
        g = hv + ov
        return (g,) + _adamw(g, wv, mv, vv)

    g_sh, d_sh, m_sh, v_sh = _tokmap(
        "adamw", update, [half, other, w_pack, _pack({n: moms[n] for n, _, _ in SHARDED}), _pack({n: vars_[n] for n, _, _ in SHARDED})],
        [], [(D, F32)] * 4)

    s_red = _all_reduce_small(_pack_small(gs, loss[0, 0]))
    g_sm, d_sm, m_sm, v_sm = _tokmap(
        "adamw_small", lambda g, wv, mv, vv: (g,) + _adamw(g, wv, mv, vv),
        [s_red, _pack_small(small), _pack_small({n: moms[n] for n, _ in SMALL}), _pack_small({n: vars_[n] for n, _ in SMALL})],
        [], [(D, F32)] * 4)

    outs = []
    for packed, packed_small in ((g_sh, g_sm), (d_sh, d_sm), (m_sh, m_sm), (v_sh, v_sm)):
        d = _unpack(packed)
        ds, _ = _unpack_small(packed_small)
        d.update(ds)
        outs += [d[n][None] for n in WEIGHTS]
    _, total_loss = _unpack_small(s_red)
    return (total_loss, gx[None], *outs)
```

```python
import functools

import numpy as np
import jax
import jax.numpy as jnp
from jax import lax
from jax.experimental import pallas as pl
from jax.experimental.pallas import tpu as pltpu

F32, BF16 = jnp.float32, jnp.bfloat16
SDS = jax.ShapeDtypeStruct
MESH = pl.DeviceIdType.MESH

D = 1024
HD = 64
QB = 128
D_FF = 2816
SB_W, DSA_W, DSA_OUT_W, MEM_W = 512, 768, 256, 256
DSA_DILS = (1, 4, 16)
MEM_LEN = 256
N_CHIPS = 4
EPS = 1e-6
SCALE = HD ** -0.5
EXHAUSTED = -104.0
SB_WIN = 384
NEG = -1e30
VMEM_LIMIT = 56 * 1024 * 1024

ADAM_LR, ADAM_B1, ADAM_B2, ADAM_EPS, ADAM_WD, ADAM_STEP = 0.001, 0.9, 0.999, 1e-08, 0.01, 10

NN = (((1,), (0,)), ((), ()))
NT = (((1,), (1,)), ((), ()))
TN = (((0,), (0,)), ((), ()))

SHARDED = (
    ("ffn1_w1", (D, D_FF), 1), ("ffn1_w3", (D, D_FF), 1), ("ffn1_w2", (D_FF, D), 0),
    ("w_in", (D, 4096), 1), ("w_mem_kv", (D, 512), 0),
    ("w_branch_sb", (SB_W, D), 1), ("w_branch_dsa", (DSA_OUT_W, D), 1), ("w_branch_mem", (MEM_W, D), 1),
    ("w_gate", (D, 3 * D), 1), ("w_out", (D, D), 0),
    ("ffn2_w1", (D, D_FF), 1), ("ffn2_w3", (D, D_FF), 1), ("ffn2_w2", (D_FF, D), 0),
)
SHARDED_BY_NAME = {n: (sh, ax) for n, sh, ax in SHARDED}
GROUPS = {
    "ffn2": ("ffn2_w1", "ffn2_w3", "ffn2_w2"),
    "mid": ("w_in", "w_mem_kv", "w_branch_sb", "w_branch_dsa", "w_branch_mem", "w_gate", "w_out"),
    "ffn1": ("ffn1_w1", "ffn1_w3", "ffn1_w2"),
}
WEIGHT_PIECES = (
    (None, GROUPS["ffn1"]), ("ffn1_up", GROUPS["mid"]), ("ffn1_down", ("ffn2_w1", "ffn2_w3")), ("proj_gate", ("ffn2_w2",)),
)
GRAD_HOSTS = {"ffn2": "ffn2_bwd_dn", "mid": "ffn1_bwd_dw13", "ffn1": "ffn1_bwd_dn"}
SMALL = (("ffn1_norm", D), ("mix_norm", D), ("mem_norm", D), ("ffn2_norm", D), ("b_gate", 3 * D),
         ("qn_dsa", HD), ("kn_dsa", HD), ("qn_mem", HD), ("kn_mem", HD))
WEIGHTS = ("ffn1_norm", "ffn1_w1", "ffn1_w3", "ffn1_w2", "mix_norm", "mem_norm", "w_in", "w_mem_kv", "qn_dsa", "kn_dsa",
           "qn_mem", "kn_mem", "w_branch_sb", "w_branch_dsa", "w_branch_mem", "w_gate", "b_gate", "w_out", "ffn2_norm",
           "ffn2_w1", "ffn2_w3", "ffn2_w2")
SMALL_ROWS = 8


def _dot(a, b, dn=NN):
    return lax.dot_general(a, b, dn, preferred_element_type=F32)


def _dot01(x, m01):
    hi = x.astype(BF16)
    r1 = x - hi.astype(F32)
    mid = r1.astype(BF16)
    lo = (r1 - mid.astype(F32)).astype(BF16)
    return _dot(hi, m01) + _dot(mid, m01) + _dot(lo, m01)


def _pick(n, cands):
    for c in cands:
        if n % c == 0:
            return c
    raise ValueError(f"no tile for {n}")


def _tokmap(name, fn, tok_ins, consts, tok_outs, acc_outs=(), tile=512):
    n = tok_ins[0].shape[0]
    tile = _pick(n, [t for t in (512, 256, 128, 64, 32, 16, 8) if t <= tile])
    n_in, n_tok, n_acc = len(tok_ins) + len(consts), len(tok_outs), len(acc_outs)

    def body(*refs):
        outs = fn(*[r[...] for r in refs[:n_in]])
        outs = outs if isinstance(outs, (tuple, list)) else (outs,)
        assert len(outs) == n_tok + n_acc, (name, len(outs))
        orefs = refs[n_in:]
        for r, v in zip(orefs[:n_tok], outs[:n_tok]):
            r[...] = v.astype(r.dtype)
        if n_acc:
            @pl.when(pl.program_id(0) == 0)
            def _():
                for r in orefs[n_tok:]:
                    r[...] = jnp.zeros(r.shape, r.dtype)
            for r, v in zip(orefs[n_tok:], outs[n_tok:]):
                r[...] += v.astype(r.dtype)

    in_specs = [pl.BlockSpec((tile, a.shape[1]), lambda i: (i, 0)) for a in tok_ins]
    in_specs += [pl.BlockSpec(c.shape, lambda i: (0, 0)) for c in consts]
    out_specs = [pl.BlockSpec((tile, w), lambda i: (i, 0)) for w, _ in tok_outs]
    out_specs += [pl.BlockSpec(s, lambda i: (0, 0)) for s in acc_outs]
    out_shape = [SDS((n, w), dt) for w, dt in tok_outs] + [SDS(s, F32) for s in acc_outs]
    res = pl.pallas_call(
        body, name=name, grid=(n // tile,), in_specs=in_specs, out_specs=out_specs, out_shape=out_shape,
        compiler_params=pltpu.CompilerParams(dimension_semantics=("arbitrary",), vmem_limit_bytes=VMEM_LIMIT),
    )(*tok_ins, *consts)
    return res


MATMUL_VMEM_BUDGET = 40 * 1024 * 1024


def _matmul_tiles(m, n, k, a_bytes, b_bytes, o_bytes, extra_bytes):
    best = None
    for tk in [c for c in (3584, 2816, 2048, 1408, 1024, 512, 256, 128) if k % c == 0]:
        for tm in [c for c in (1408, 1024, 768, 512, 256, 128) if m % c == 0]:
            for tn in [c for c in (1408, 1024, 768, 512, 256, 128) if n % c == 0]:
                need = 2 * tk * (tm * a_bytes + tn * b_bytes) + tm * tn * (2 * o_bytes + 2 * extra_bytes + 8)
                if need > MATMUL_VMEM_BUDGET:
                    continue
                score = (min(tm, 512) * min(tn, 512), tk, tm * tn)
                if best is None or score > best[0]:
                    best = (score, (tm, tn, tk))
    return best[1]


class _Carry:
    def __init__(self, ins, outs, sems, copies, then):
        self.ins, self.outs, self.sems, self.copies, self.then = ins, outs, sems, copies, then


class _Plan:
    def __init__(self):
        self.pending = {}

    def put(self, host, carry):
        assert host not in self.pending, host
        self.pending[host] = carry

    def take(self, host):
        return self.pending.pop(host, None)


def _matmul(name, a, b, dn, out_dtype, epi=None, tiles=(), rows=(), plan=None):
    if dn == NN:
        (m, k), n = a.shape, b.shape[1]
    elif dn == NT:
        (m, k), n = a.shape, b.shape[0]
    else:
        (k, m), n = a.shape, b.shape[1]
    n_t, n_r = len(tiles), len(rows)
    tm, tn, tk = _matmul_tiles(m, n, k, a.dtype.itemsize, b.dtype.itemsize, jnp.dtype(out_dtype).itemsize,
                               sum(t.dtype.itemsize for t in tiles))
    nk = k // tk
    grid = (m // tm, n // tn, nk)
    carry = plan.take(name) if plan is not None else None
    n_ci, n_co = (len(carry.ins), len(carry.outs)) if carry else (0, 0)

    def body(a_ref, b_ref, *rest):
        extras, rest = rest[:n_t + n_r], rest[n_t + n_r:]
        c_in, o_ref, c_out, scratch = rest[:n_ci], rest[n_ci], rest[n_ci + 1:n_ci + 1 + n_co], rest[n_ci + 1 + n_co:]
        ids = [pl.program_id(d) for d in range(3)]
        if carry:
            sems = scratch[1:] if nk > 1 else scratch

            @pl.when((ids[0] == 0) & (ids[1] == 0) & (ids[2] == 0))
            def _():
                for cp in carry.copies(c_in, c_out, *sems):
                    cp.start()

        part = _dot(a_ref[...].astype(BF16), b_ref[...].astype(BF16), dn)

        def finish(r):
            if epi is not None:
                r = epi(r, *[e[...] for e in extras])
            o_ref[...] = r.astype(o_ref.dtype)

        if nk == 1:
            finish(part)
        else:
            acc = scratch[0]

            @pl.when(ids[2] == 0)
            def _():
                acc[...] = part

            @pl.when(ids[2] > 0)
            def _():
                acc[...] += part

            @pl.when(ids[2] == nk - 1)
            def _():
                finish(acc[...])

        if carry:
            @pl.when((ids[0] == grid[0] - 1) & (ids[1] == grid[1] - 1) & (ids[2] == nk - 1))
            def _():
                for cp in carry.copies(c_in, c_out, *sems):
                    cp.wait()

    a_spec = pl.BlockSpec((tk, tm), lambda i, j, kk: (kk, i)) if dn == TN else pl.BlockSpec((tm, tk), lambda i, j, kk: (i, kk))
    b_spec = pl.BlockSpec((tn, tk), lambda i, j, kk: (j, kk)) if dn == NT else pl.BlockSpec((tk, tn), lambda i, j, kk: (kk, j))
    in_specs = [a_spec, b_spec] + [pl.BlockSpec((tm, tn), lambda i, j, kk: (i, j)) for _ in tiles]
    in_specs += [pl.BlockSpec((1, tn), lambda i, j, kk: (0, j)) for _ in rows] + [HBM_SPEC] * n_ci
    res = pl.pallas_call(
        body, name=name, grid=grid, in_specs=in_specs,
        out_specs=[pl.BlockSpec((tm, tn), lambda i, j, kk: (i, j))] + [HBM_SPEC] * n_co,
        out_shape=[SDS((m, n), out_dtype)] + (list(carry.outs) if carry else []),
        scratch_shapes=([pltpu.VMEM((tm, tn), F32)] if nk > 1 else []) + (list(carry.sems) if carry else []),
        compiler_params=pltpu.CompilerParams(
            dimension_semantics=("arbitrary",) * 3 if carry else ("parallel", "parallel", "arbitrary"),
            vmem_limit_bytes=VMEM_LIMIT),
    )(a, b, *tiles, *rows, *(carry.ins if carry else []))
    if carry:
        carry.then(res[1:])
    return res[0]


def _mean_all(v):
    return jnp.mean(v, axis=-1, keepdims=True)


def _mean_heads(bd):
    return lambda v: _dot01(v, bd) * (1.0 / HD)


def _rms_fwd(x, g, mean):
    return x * lax.rsqrt(mean(x * x) + EPS) * g


def _rms_bwd(x, g, dy, mean):
    r = lax.rsqrt(mean(x * x) + EPS)
    dn = dy * g
    dx = r * dn - x * (r * r * r) * mean(dn * x)
    return dx, jnp.sum(dy * x * r, axis=0, keepdims=True)


def _swap_halves(x):
    w = x.shape[1]
    lane = lax.broadcasted_iota(jnp.int32, x.shape, 1)
    return jnp.where(lane % HD < HD // 2, pltpu.roll(x, w - HD // 2, 1), pltpu.roll(x, HD // 2, 1))


def _lanes(t, w):
    return jnp.tile(t, (1, w // t.shape[1]))


def _rope_fwd(x, cos, sin_signed):
    return x * _lanes(cos, x.shape[1]) + _swap_halves(x) * _lanes(sin_signed, x.shape[1])


def _rope_bwd(dy, cos, sin_signed):
    return dy * _lanes(cos, dy.shape[1]) + _swap_halves(dy * _lanes(sin_signed, dy.shape[1]))


def _bcast_heads(cols):
    return jnp.concatenate([jnp.broadcast_to(c, (c.shape[0], HD)) for c in cols], axis=1)


def _softplus(z):
    return jnp.maximum(z, 0.0) + jnp.log1p(jnp.exp(-jnp.abs(z)))


def _block_diag(w):
    h = np.arange(w) // HD
    return jnp.asarray(h[:, None] == h[None, :], BF16)


def _sb_window(i, t):
    hi = (i + 1) * QB - t * SB_WIN
    lo = hi - SB_WIN
    ws = pl.multiple_of(jnp.maximum(lo, 0), QB)
    kpos = ws + lax.broadcasted_iota(jnp.int32, (QB, SB_WIN), 1)
    qpos = i * QB + lax.broadcasted_iota(jnp.int32, (QB, SB_WIN), 0)
    return (kpos < qpos) & (kpos >= lo) & (kpos < hi), ws


def _sb_fwd(qkv):
    s = qkv.shape[0]
    assert s >= SB_WIN
    nq = s // QB
    npairs = SB_W // 128

    def body(q_ref, k_ref, v_ref, later_ref, o_ref, tot_ref, nb_ref):
        p, i = pl.program_id(0), pl.program_id(1)
        q = q_ref[...]
        later_of = later_ref[...]

        def step(c):
            t, _, tots, outs = c
            mask, ws = _sb_window(i, t)
            kw, vw = k_ref[pl.ds(ws, SB_WIN), :], v_ref[pl.ds(ws, SB_WIN), :]
            new_t, new_o = [], []
            for hh in range(2):
                sl = slice(HD * hh, HD * hh + HD)
                z = _dot(q[:, sl], kw[:, sl], NT) * SCALE
                sp = _softplus(z)
                lf = jnp.where(mask, -sp, 0.0)
                later = tots[hh] + _dot01(lf, later_of)
                w = jnp.where(mask, jnp.exp(z - sp + later), 0.0)
                new_o.append(outs[hh] + _dot(w.astype(BF16), vw[:, sl]))
                new_t.append(tots[hh] + jnp.sum(lf, axis=1, keepdims=True))
            alive = jnp.maximum(jnp.max(new_t[0]), jnp.max(new_t[1]))
            return t + 1, alive, tuple(new_t), tuple(new_o)

        zt, zo = jnp.zeros((QB, 1), F32), jnp.zeros((QB, HD), F32)
        t, _, tots, outs = lax.while_loop(lambda c: ((i + 1) * QB - c[0] * SB_WIN > 0) & (c[1] > EXHAUSTED), step,
                                          (jnp.int32(0), jnp.float32(0.0), (zt, zt), (zo, zo)))
        o_ref[...] = jnp.concatenate(outs, axis=1).astype(o_ref.dtype)
        tot_ref[...] = _bcast_heads(tots)
        nb_ref[p, i] = t

    whole = lambda off: pl.BlockSpec((s, 128), lambda p, i: (0, off + p), pipeline_mode=pl.Buffered(1))
    tile = pl.BlockSpec((QB, 128), lambda p, i: (i, p))
    tri = pl.BlockSpec((SB_WIN, SB_WIN), lambda p, i: (0, 0), pipeline_mode=pl.Buffered(1))
    idx = np.arange(SB_WIN)
    return pl.pallas_call(
        body, name="sb_fwd", grid=(npairs, nq),
        in_specs=[tile, whole(npairs), whole(2 * npairs), tri],
        out_specs=[tile, tile, pl.BlockSpec(memory_space=pltpu.SMEM)],
        out_shape=[SDS((s, SB_W), BF16), SDS((s, SB_W), F32), SDS((npairs, nq), jnp.int32)],
        compiler_params=pltpu.CompilerParams(dimension_semantics=("arbitrary", "arbitrary"), vmem_limit_bytes=VMEM_LIMIT),
    )(qkv, qkv, qkv, jnp.asarray(idx[:, None] > idx[None, :], BF16))


def _sb_bwd(qkv, do, tot, nblk):
    s = qkv.shape[0]
    nq = s // QB
    npairs = SB_W // 128

    def body(nb_ref, q_ref, k_ref, v_ref, do_ref, tot_ref, upto_ref, before_ref, dq_ref, dk_ref, dv_ref):
        p, i = pl.program_id(0), pl.program_id(1)

        @pl.when(i == 0)
        def _():
            dk_ref[...] = jnp.zeros(dk_ref.shape, F32)
            dv_ref[...] = jnp.zeros(dv_ref.shape, F32)

        upto = upto_ref[...]
        before = before_ref[...]
        q, dout, tt = q_ref[...], do_ref[...], tot_ref[...]
        n = nb_ref[p, i]

        def step(it, c):
            pres, gpres, dqs = c
            mask, ws = _sb_window(i, n - 1 - it)
            kw, vw = k_ref[pl.ds(ws, SB_WIN), :], v_ref[pl.ds(ws, SB_WIN), :]
            new_p, new_g, new_dq, dks, dvs = [], [], [], [], []
            for hh in range(2):
                sl = slice(HD * hh, HD * hh + HD)
                z = _dot(q[:, sl], kw[:, sl], NT) * SCALE
                sp = _softplus(z)
                lf = jnp.where(mask, -sp, 0.0)
                later = tt[:, HD * hh:HD * hh + 1] - (pres[hh] + _dot01(lf, upto))
                w = jnp.where(mask, jnp.exp(z - sp + later), 0.0)
                beta = jnp.exp(z - sp)
                g = _dot(dout[:, sl], vw[:, sl], NT) * w
                g_far = gpres[hh] + _dot(g.astype(BF16), before)
                dz = (jnp.where(mask, g * (1.0 - beta) - beta * g_far, 0.0) * SCALE).astype(BF16)
                new_dq.append(dqs[hh] + _dot(dz, kw[:, sl]))
                dks.append(_dot(dz, q[:, sl], TN))
                dvs.append(_dot(w.astype(BF16), dout[:, sl], TN))
                new_p.append(pres[hh] + jnp.sum(lf, axis=1, keepdims=True))
                new_g.append(gpres[hh] + jnp.sum(g, axis=1, keepdims=True))
            dk_ref[pl.ds(ws, SB_WIN), :] += jnp.concatenate(dks, axis=1)
            dv_ref[pl.ds(ws, SB_WIN), :] += jnp.concatenate(dvs, axis=1)
            return tuple(new_p), tuple(new_g), tuple(new_dq)

        zt, zo = jnp.zeros((QB, 1), F32), jnp.zeros((QB, HD), F32)
        _, _, dqs = lax.fori_loop(0, n, step, ((zt, zt), (zt, zt), (zo, zo)))
        dq_ref[...] = jnp.concatenate(dqs, axis=1)

    whole_in = lambda off: pl.BlockSpec((s, 128), lambda p, i: (0, off + p), pipeline_mode=pl.Buffered(1))
    whole_out = pl.BlockSpec((s, 128), lambda p, i: (0, p), pipeline_mode=pl.Buffered(1))
    tile = pl.BlockSpec((QB, 128), lambda p, i: (i, p))
    tri = pl.BlockSpec((SB_WIN, SB_WIN), lambda p, i: (0, 0), pipeline_mode=pl.Buffered(1))
    idx = np.arange(SB_WIN)
    return pl.pallas_call(
        body, name="sb_bwd", grid=(npairs, nq),
        in_specs=[pl.BlockSpec(memory_space=pltpu.SMEM), tile, whole_in(npairs), whole_in(2 * npairs), tile, tile, tri, tri],
        out_specs=[tile, whole_out, whole_out],
        out_shape=[SDS((s, SB_W), F32)] * 3,
        compiler_params=pltpu.CompilerParams(dimension_semantics=("arbitrary", "arbitrary"), vmem_limit_bytes=VMEM_LIMIT),
    )(nblk, qkv, qkv, qkv, do, tot, jnp.asarray(idx[:, None] <= idx[None, :], BF16), jnp.asarray(idx[:, None] < idx[None, :], BF16))


def _win_masks(has_prev):
    row = lax.broadcasted_iota(jnp.int32, (QB, QB), 0)
    col = lax.broadcasted_iota(jnp.int32, (QB, QB), 1)
    return col <= row, (col >= row) & has_prev


def _dsa_fwd(q, k, v, dil):
    s = q.shape[0]
    n = s // dil
    nb = n // QB
    q, k, v = (t.reshape(n, dil * DSA_OUT_W) for t in (q, k, v))

    def body(q_ref, kc_ref, kp_ref, vc_ref, vp_ref, o_ref, lse_ref):
        m_cur, m_prev = _win_masks(pl.program_id(1) > 0)
        outs, lses = [], []
        for hh in range(2):
            sl = slice(HD * hh, HD * hh + HD)
            qh = q_ref[:, sl]
            sc = jnp.where(m_cur, _dot(qh, kc_ref[:, sl], NT) * SCALE, NEG)
            sp = jnp.where(m_prev, _dot(qh, kp_ref[:, sl], NT) * SCALE, NEG)
            m = jnp.maximum(jnp.max(sc, axis=1, keepdims=True), jnp.max(sp, axis=1, keepdims=True))
            pc, pp = jnp.exp(sc - m), jnp.exp(sp - m)
            den = jnp.sum(pc, axis=1, keepdims=True) + jnp.sum(pp, axis=1, keepdims=True)
            outs.append((_dot(pc.astype(BF16), vc_ref[:, sl]) + _dot(pp.astype(BF16), vp_ref[:, sl])) / den)
            lses.append(m + jnp.log(den))
        o_ref[...] = jnp.concatenate(outs, axis=1)
        lse_ref[...] = _bcast_heads(lses)

    cur = pl.BlockSpec((QB, 128), lambda c, i, p: (i, 2 * c + p))
    prev = pl.BlockSpec((QB, 128), lambda c, i, p: (jnp.maximum(i - 1, 0), 2 * c + p))
    o, lse = pl.pallas_call(
        body, name=f"dsa_fwd_d{dil}", grid=(dil, nb, 2), in_specs=[cur, cur, prev, cur, prev], out_specs=[cur, cur],
        out_shape=[SDS((n, dil * DSA_OUT_W), F32)] * 2,
        compiler_params=pltpu.CompilerParams(dimension_semantics=("parallel", "parallel", "parallel")),
    )(q, k, k, v, v)
    return o.reshape(s, DSA_OUT_W), lse.reshape(s, DSA_OUT_W)


def _dsa_bwd(q, k, v, do, cc, lse, dil):
    s = q.shape[0]
    n = s // dil
    nb = n // QB
    q, k, v, do, cc, lse = (t.reshape(n, dil * DSA_OUT_W) for t in (q, k, v, do, cc, lse))

    def body(qj_ref, qn_ref, kp_ref, kj_ref, vp_ref, vj_ref, doj_ref, don_ref, cj_ref, cn_ref, lj_ref, ln_ref,
             dq_ref, dk_ref, dv_ref):
        j = pl.program_id(1)
        m_cur, m_prev = _win_masks(j > 0)
        _, m_next = _win_masks(j + 1 < nb)
        dqs, dks, dvs = [], [], []
        for hh in range(2):
            sl = slice(HD * hh, HD * hh + HD)
            one = slice(HD * hh, HD * hh + 1)
            qj, qn, kp, kj, vp, vj = (r[:, sl] for r in (qj_ref, qn_ref, kp_ref, kj_ref, vp_ref, vj_ref))
            doj, don = doj_ref[:, sl], don_ref[:, sl]

            def dscore(qq, kk, vv, dd, c_ref, l_ref, mask):
                prob = jnp.where(mask, jnp.exp(_dot(qq, kk, NT) * SCALE - l_ref[:, one]), 0.0)
                return prob, (prob * (_dot(dd, vv, NT) + c_ref[:, one]) * SCALE).astype(BF16)

            _, ds_a = dscore(qj, kp, vp, doj, cj_ref, lj_ref, m_prev)
            p_b, ds_b = dscore(qj, kj, vj, doj, cj_ref, lj_ref, m_cur)
            p_c, ds_c = dscore(qn, kj, vj, don, cn_ref, ln_ref, m_next)
            dqs.append(_dot(ds_a, kp) + _dot(ds_b, kj))
            dks.append(_dot(ds_b, qj, TN) + _dot(ds_c, qn, TN))
            dvs.append(_dot(p_b.astype(BF16), doj, TN) + _dot(p_c.astype(BF16), don, TN))
        dq_ref[...] = jnp.concatenate(dqs, axis=1)
        dk_ref[...] = jnp.concatenate(dks, axis=1)
        dv_ref[...] = jnp.concatenate(dvs, axis=1)

    cur = pl.BlockSpec((QB, 128), lambda c, j, p: (j, 2 * c + p))
    prev = pl.BlockSpec((QB, 128), lambda c, j, p: (jnp.maximum(j - 1, 0), 2 * c + p))
    nxt = pl.BlockSpec((QB, 128), lambda c, j, p: (jnp.minimum(j + 1, nb - 1), 2 * c + p))
    dq, dk, dv = pl.pallas_call(
        body, name=f"dsa_bwd_d{dil}", grid=(dil, nb, 2),
        in_specs=[cur, nxt, prev, cur, prev, cur, cur, nxt, cur, nxt, cur, nxt], out_specs=[cur, cur, cur],
        out_shape=[SDS((n, dil * DSA_OUT_W), F32)] * 3,
        compiler_params=pltpu.CompilerParams(dimension_semantics=("parallel", "parallel", "parallel")),
    )(q, q, k, k, v, v, do, do, cc, cc, lse, lse)
    return tuple(t.reshape(s, DSA_OUT_W) for t in (dq, dk, dv))


def _ffn_fwd(tag, x, gain, w13, w2, plan=None):
    n = _tokmap(f"{tag}_norm", lambda xv, g: _rms_fwd(xv, g, _mean_all), [x], [gain], [(D, BF16)])[0]
    ab = _matmul(f"{tag}_up", n, w13, NN, BF16, plan=plan)

    def gate(abv):
        a, b = abv[:, :D_FF].astype(F32), abv[:, D_FF:].astype(F32)
        return a * jax.nn.sigmoid(a) * b

    h = _tokmap(f"{tag}_gate", gate, [ab], [], [(D_FF, BF16)], tile=256)[0]
    y = _matmul(f"{tag}_down", h, w2, NN, F32, epi=lambda acc, res: res + 0.5 * acc, tiles=[x], plan=plan)
    return y, (n, ab)


def _ffn_bwd(tag, x, gain, w13, w2, saved, dy, plan=None, on_dw=None):
    n, ab = saved
    dh = _matmul(f"{tag}_bwd_dh", dy, w2, NT, BF16, epi=lambda acc: 0.5 * acc)

    def gate_bwd(abv, dhv):
        a, b, dhf = abv[:, :D_FF].astype(F32), abv[:, D_FF:].astype(F32), dhv.astype(F32)
        sg = jax.nn.sigmoid(a)
        silu = a * sg
        da = dhf * b * (sg * (1.0 + a * (1.0 - sg)))
        return jnp.concatenate([da, dhf * silu], axis=1), silu * b

    dab, h = _tokmap(f"{tag}_bwd_gate", gate_bwd, [ab, dh], [], [(2 * D_FF, BF16), (D_FF, BF16)], tile=256)
    dw2 = _matmul(f"{tag}_bwd_dw2", h, dy, TN, F32, epi=lambda acc: 0.5 * acc)
    dw13 = _matmul(f"{tag}_bwd_dw13", n, dab, TN, F32, plan=plan)
    if on_dw is not None:
        on_dw(dw13, dw2)
    dn = _matmul(f"{tag}_bwd_dn", dab, w13, NT, F32, plan=plan)

    def norm_bwd(xv, dnv, dyv, g):
        dx, dg = _rms_bwd(xv, g, dnv, _mean_all)
        return dx + dyv, dg

    dx, dgain = _tokmap(f"{tag}_bwd_norm", norm_bwd, [x, dn, dy], [gain], [(D, F32)], [(1, D)])
    return dx, dgain, dw13, dw2


def _rope_tables(s):
    half = HD // 2
    inv_freq = jnp.power(10000.0, -jnp.arange(half, dtype=F32) / half)
    ang = jnp.arange(s).astype(F32)[:, None] * inv_freq[None, :]
    cos, sin = jnp.cos(ang), jnp.sin(ang)
    return jnp.tile(jnp.concatenate([cos, cos], axis=1), (1, 2)), jnp.tile(jnp.concatenate([-sin, sin], axis=1), (1, 2))


def _local_step(x, mem, tgt, w, sm, plan=None, on_grads=None):
    s = x.shape[0]
    assert s % (QB * max(DSA_DILS)) == 0
    on_grads = on_grads or (lambda group, grads: None)
    c_sb, c_dsa, c_qm = 3 * SB_W, 3 * SB_W + 3 * DSA_W, 4096
    cos, sin = _rope_tables(s)
    bd768, bd256 = _block_diag(DSA_W), _block_diag(MEM_W)
    gq_dsa, gk_dsa = jnp.tile(sm["qn_dsa"], (1, DSA_W // HD)), jnp.tile(sm["kn_dsa"], (1, DSA_W // HD))
    gq_mem, gk_mem = jnp.tile(sm["qn_mem"], (1, MEM_W // HD)), jnp.tile(sm["kn_mem"], (1, MEM_W // HD))

    w13_1 = jnp.concatenate([w["ffn1_w1"], w["ffn1_w3"]], axis=1)
    x1, ffn1_saved = _ffn_fwd("ffn1", x, sm["ffn1_norm"], w13_1, w["ffn1_w2"], plan)
    w_all = jnp.concatenate([w["w_in"], w["w_gate"]], axis=1)
    wb_sb, wb_dsa, wb_mem = w["w_branch_sb"], w["w_branch_dsa"], w["w_branch_mem"]
    hmix = _tokmap("mix_norm", lambda xv, g: _rms_fwd(xv, g, _mean_all), [x1], [sm["mix_norm"]], [(D, BF16)])[0]
    qkv_sb = _matmul("proj_sb", hmix, w_all[:, :c_sb], NN, BF16)
    qkv_dsa = _matmul("proj_dsa", hmix, w_all[:, c_sb:c_dsa], NN, BF16)
    q_mem = _matmul("proj_qmem", hmix, w_all[:, c_dsa:c_qm], NN, BF16)
    gpre = _matmul("proj_gate", hmix, w_all[:, c_qm:], NN, BF16, epi=lambda acc, b: acc + b, rows=[sm["b_gate"]], plan=plan)

    o_sb, sb_tot, sb_nblk = _sb_fwd(qkv_sb)

    def dsa_prep(qkv, cs, sn, gq, gk, bd):
        mean = _mean_heads(bd)
        qn = _rope_fwd(_rms_fwd(qkv[:, :DSA_W].astype(F32), gq, mean), cs, sn)
        kn = _rope_fwd(_rms_fwd(qkv[:, DSA_W:2 * DSA_W].astype(F32), gk, mean), cs, sn)
        v = qkv[:, 2 * DSA_W:]
        outs = []
        for t in (qn, kn, v):
            outs += [t[:, DSA_OUT_W * g:DSA_OUT_W * (g + 1)] for g in range(3)]
        return outs

    dsa_in = _tokmap("dsa_prep", dsa_prep, [qkv_dsa, cos, sin], [gq_dsa, gk_dsa, bd768], [(DSA_OUT_W, BF16)] * 9, tile=256)
    dsa_q, dsa_k, dsa_v = dsa_in[0:3], dsa_in[3:6], dsa_in[6:9]
    dsa_o, dsa_lse = zip(*[_dsa_fwd(dsa_q[g], dsa_k[g], dsa_v[g], DSA_DILS[g]) for g in range(3)])

    def alphas(l0, l1, l2):
        m = jnp.maximum(jnp.maximum(l0, l1), l2)
        e = [jnp.exp(l - m) for l in (l0, l1, l2)]
        tot = e[0] + e[1] + e[2]
        return [t / tot for t in e]

    def dsa_mix(o0, o1, o2, l0, l1, l2):
        a = alphas(l0, l1, l2)
        return a[0] * o0 + a[1] * o1 + a[2] * o2

    o_dsa = _tokmap("dsa_mix", dsa_mix, [*dsa_o, *dsa_lse], [], [(DSA_OUT_W, BF16)])[0]

    def mem_kv(memv, g, wkv, gk, bd):
        kv = _dot(_rms_fwd(memv, g, _mean_all).astype(BF16), wkv)
        return _rms_fwd(kv[:, :MEM_W], gk, _mean_heads(bd)), kv[:, MEM_W:]

    km, vm = _tokmap("mem_kv", mem_kv, [mem], [sm["mem_norm"], w["w_mem_kv"], gk_mem, bd256], [(MEM_W, BF16)] * 2)

    def mem_probs(qv, kmv, gq, bd):
        qn = _rms_fwd(qv.astype(F32), gq, _mean_heads(bd)).astype(BF16)
        ps = []
        for h in range(MEM_W // HD):
            sl = slice(HD * h, HD * h + HD)
            sc = _dot(qn[:, sl], kmv[:, sl], NT) * SCALE
            e = jnp.exp(sc - jnp.max(sc, axis=1, keepdims=True))
            ps.append(e / jnp.sum(e, axis=1, keepdims=True))
        return qn, ps

    def mem_attn(qv, kmv, vmv, gq, bd):
        _, ps = mem_probs(qv, kmv, gq, bd)
        return jnp.concatenate([_dot(p.astype(BF16), vmv[:, HD * h:HD * h + HD]) for h, p in enumerate(ps)], axis=1)

    o_mem = _tokmap("mem_attn", mem_attn, [q_mem], [km, vm, gq_mem, bd256], [(MEM_W, BF16)])[0]

    def merge(osb, odsa, omem, gp, w_sb, w_dsa, w_mem):
        gates = jax.nn.sigmoid(gp.astype(F32))
        ys = (_dot(osb, w_sb), _dot(odsa, w_dsa), _dot(omem, w_mem))
        return gates, ys, gates[:, :D] * ys[0] + gates[:, D:2 * D] * ys[1] + gates[:, 2 * D:] * ys[2]

    merged = _tokmap("merge", lambda *a: merge(*a)[2], [o_sb, o_dsa, o_mem, gpre], [wb_sb, wb_dsa, wb_mem], [(D, BF16)],
                     tile=256)[0]
    x2 = _matmul("out_proj", merged, w["w_out"], NN, F32, epi=lambda acc, res: res + acc, tiles=[x1])
    w13_2 = jnp.concatenate([w["ffn2_w1"], w["ffn2_w3"]], axis=1)
    y, ffn2_saved = _ffn_fwd("ffn2", x2, sm["ffn2_norm"], w13_2, w["ffn2_w2"])

    def loss_fn(yv, tv):
        e = yv - tv
        part = 0.5 * jnp.sum(jnp.mean(e * e, axis=1, keepdims=True), axis=0, keepdims=True)
        return e * (1.0 / D), jnp.broadcast_to(part, (1, 128))

    dy, loss = _tokmap("loss", loss_fn, [y, tgt], [], [(D, F32)], [(1, 128)])

    gw, gs = {}, {}
    def ffn_grads(tag):
        def on_dw(dw13, dw2):
            gw[f"{tag}_w1"], gw[f"{tag}_w3"], gw[f"{tag}_w2"] = dw13[:, :D_FF], dw13[:, D_FF:], dw2
            on_grads(tag, {n: gw[n] for n in (f"{tag}_w1", f"{tag}_w3", f"{tag}_w2")})
        return on_dw

    dx2, gs["ffn2_norm"], _, _ = _ffn_bwd("ffn2", x2, sm["ffn2_norm"], w13_2, w["ffn2_w2"], ffn2_saved, dy, plan,
                                          ffn_grads("ffn2"))
    dmerged = _matmul("out_proj_bwd_dx", dx2, w["w_out"], NT, BF16)
    gw["w_out"] = _matmul("out_proj_bwd_dw", merged, dx2, TN, F32)

    def merge_bwd(osb, odsa, omem, gp, dm, w_sb, w_dsa, w_mem):
        gates, ys, _ = merge(osb, odsa, omem, gp, w_sb, w_dsa, w_mem)
        dmf = dm.astype(F32)
        dgp, dos, dws = [], [], []
        for b, (ov, wv) in enumerate(((osb, w_sb), (odsa, w_dsa), (omem, w_mem))):
            gb = gates[:, D * b:D * (b + 1)]
            dgp.append(dmf * ys[b] * gb * (1.0 - gb))
            dyb = (dmf * gb).astype(BF16)
            dos.append(_dot(dyb, wv, NT))
            dws.append(_dot(ov, dyb, TN))
        dgp = jnp.concatenate(dgp, axis=1)
        return dos[0], dos[1], dos[2], dgp, dws[0], dws[1], dws[2], jnp.sum(dgp, axis=0, keepdims=True)

    do_sb, do_dsa, do_mem, dgpre, gw["w_branch_sb"], gw["w_branch_dsa"], gw["w_branch_mem"], gs["b_gate"] = _tokmap(
        "merge_bwd", merge_bwd, [o_sb, o_dsa, o_mem, gpre, dmerged], [wb_sb, wb_dsa, wb_mem],
        [(SB_W, BF16), (DSA_OUT_W, F32), (MEM_W, BF16), (3 * D, BF16)],
        [(SB_W, D), (DSA_OUT_W, D), (MEM_W, D), (1, 3 * D)], tile=256)

    dq_sb, dk_sb, dv_sb = _sb_bwd(qkv_sb, do_sb, sb_tot, sb_nblk)

    def dsa_mix_bwd(o0, o1, o2, l0, l1, l2, dov, bd):
        a = alphas(l0, l1, l2)
        omix = a[0] * o0 + a[1] * o1 + a[2] * o2
        dot_o = _dot01(dov * omix, bd)
        return [dov * t for t in a] + [-t * dot_o for t in a]

    mixb = _tokmap("dsa_mix_bwd", dsa_mix_bwd, [*dsa_o, *dsa_lse, do_dsa], [bd256],
                   [(DSA_OUT_W, BF16)] * 3 + [(DSA_OUT_W, F32)] * 3)
    dsa_d = [_dsa_bwd(dsa_q[g], dsa_k[g], dsa_v[g], mixb[g], mixb[3 + g], dsa_lse[g], DSA_DILS[g]) for g in range(3)]

    def dsa_prep_bwd(qkv, cs, sn, *rest):
        dqs, dks, dvs, (gq, gk, bd) = rest[0:3], rest[3:6], rest[6:9], rest[9:]
        mean = _mean_heads(bd)
        dq, dgq = _rms_bwd(qkv[:, :DSA_W].astype(F32), gq, _rope_bwd(jnp.concatenate(dqs, axis=1), cs, sn), mean)
        dk, dgk = _rms_bwd(qkv[:, DSA_W:2 * DSA_W].astype(F32), gk, _rope_bwd(jnp.concatenate(dks, axis=1), cs, sn), mean)
        return jnp.concatenate([dq, dk] + list(dvs), axis=1), dgq, dgk

    dqkv_dsa, dgq_dsa, dgk_dsa = _tokmap(
        "dsa_prep_bwd", dsa_prep_bwd,
        [qkv_dsa, cos, sin] + [dsa_d[g][t] for t in range(3) for g in range(3)], [gq_dsa, gk_dsa, bd768],
        [(3 * DSA_W, BF16)], [(1, DSA_W), (1, DSA_W)], tile=256)
    gs["qn_dsa"] = dgq_dsa.reshape(DSA_W // HD, HD).sum(axis=0, keepdims=True)
    gs["kn_dsa"] = dgk_dsa.reshape(DSA_W // HD, HD).sum(axis=0, keepdims=True)

    def mem_attn_bwd(qv, dov, kmv, vmv, gq, bd):
        qn, ps = mem_probs(qv, kmv, gq, bd)
        dqn, dkm, dvm = [], [], []
        for h, p in enumerate(ps):
            sl = slice(HD * h, HD * h + HD)
            dp = _dot(dov[:, sl], vmv[:, sl], NT)
            ds = (p * (dp - jnp.sum(p * dp, axis=1, keepdims=True)) * SCALE).astype(BF16)
            dqn.append(_dot(ds, kmv[:, sl]))
            dkm.append(_dot(ds, qn[:, sl], TN))
            dvm.append(_dot(p.astype(BF16), dov[:, sl], TN))
        dq, dgq = _rms_bwd(qv.astype(F32), gq, jnp.concatenate(dqn, axis=1), _mean_heads(bd))
        return dq, jnp.concatenate(dkm, axis=1), jnp.concatenate(dvm, axis=1), dgq

    dq_mem, dkm, dvm, dgq_mem = _tokmap("mem_attn_bwd", mem_attn_bwd, [q_mem, do_mem], [km, vm, gq_mem, bd256],
                                        [(MEM_W, BF16)], [(MEM_LEN, MEM_W), (MEM_LEN, MEM_W), (1, MEM_W)])
    gs["qn_mem"] = dgq_mem.reshape(MEM_W // HD, HD).sum(axis=0, keepdims=True)

    def mem_kv_bwd(memv, dkmv, dvmv, g, wkv, gk, bd):
        memn = _rms_fwd(memv, g, _mean_all).astype(BF16)
        kv = _dot(memn, wkv)
        dk, dgk = _rms_bwd(kv[:, :MEM_W], gk, dkmv, _mean_heads(bd))
        dkv = jnp.concatenate([dk, dvmv], axis=1).astype(BF16)
        _, dg = _rms_bwd(memv, g, _dot(dkv, wkv, NT), _mean_all)
        return _dot(memn, dkv, TN), dg, dgk

    gw["w_mem_kv"], gs["mem_norm"], dgk_mem = _tokmap(
        "mem_kv_bwd", mem_kv_bwd, [mem, dkm, dvm], [sm["mem_norm"], w["w_mem_kv"], gk_mem, bd256], [],
        [(D, 2 * MEM_W), (1, D), (1, MEM_W)])
    gs["kn_mem"] = dgk_mem.reshape(MEM_W // HD, HD).sum(axis=0, keepdims=True)

    dall = jnp.concatenate([dq_sb.astype(BF16), dk_sb.astype(BF16), dv_sb.astype(BF16), dqkv_dsa, dq_mem, dgpre], axis=1)
    dhmix = _matmul("proj_bwd_dx", dall, w_all, NT, F32)
    dw_all = _matmul("proj_bwd_dw", hmix, dall, TN, F32)
    gw["w_in"], gw["w_gate"] = dw_all[:, :c_qm], dw_all[:, c_qm:]
    on_grads("mid", {n: gw[n] for n in GROUPS["mid"]})

    def mix_norm_bwd(xv, dnv, dyv, g):
        dx, dg = _rms_bwd(xv, g, dnv, _mean_all)
        return dx + dyv, dg

    dx1, gs["mix_norm"] = _tokmap("mix_norm_bwd", mix_norm_bwd, [x1, dhmix, dx2], [sm["mix_norm"]], [(D, F32)], [(1, D)])
    gx, gs["ffn1_norm"], _, _ = _ffn_bwd("ffn1", x, sm["ffn1_norm"], w13_1, w["ffn1_w2"], ffn1_saved, dx1, plan,
                                         ffn_grads("ffn1"))
    return loss, gx, gw, gs


def _shard_shape(shape, axis):
    return (shape[0] // N_CHIPS, shape[1]) if axis == 0 else (shape[0], shape[1] // N_CHIPS)


def _pack(shards, names):
    return jnp.concatenate([shards[n].reshape(-1, D) for n in names], axis=0)


def _unpack(pack, names):
    out, r = {}, 0
    for n in names:
        ss = _shard_shape(*SHARDED_BY_NAME[n])
        rows = ss[0] * ss[1] // D
        out[n] = pack[r:r + rows].reshape(ss)
        r += rows
    return out


def _full_from_packs(packs, names):
    per_chip = [_unpack(packs[c], names) for c in range(N_CHIPS)]
    return {n: jnp.concatenate([per_chip[c][n] for c in range(N_CHIPS)], axis=SHARDED_BY_NAME[n][1]) for n in names}


def _packs_from_full(full, names, dtype):
    packs = []
    for c in range(N_CHIPS):
        shards = {}
        for n in names:
            sh, ax = SHARDED_BY_NAME[n]
            ss = _shard_shape(sh, ax)
            shards[n] = lax.slice_in_dim(full[n], c * ss[ax], (c + 1) * ss[ax], axis=ax).astype(dtype)
        packs.append(_pack(shards, names))
    return jnp.stack(packs)


def _own_pack(full, names, chip):
    shards = {}
    for n in names:
        sh, ax = SHARDED_BY_NAME[n]
        ss = _shard_shape(sh, ax)
        shards[n] = lax.dynamic_slice_in_dim(full[n], chip * ss[ax], ss[ax], axis=ax)
    return _pack(shards, names)


SMALL_USED = sum(n for _, n in SMALL)


def _pack_small(d, loss=None):
    parts = [d[n].reshape(-1) for n, _ in SMALL]
    parts.append(jnp.zeros((1,), F32) if loss is None else loss.reshape(1))
    parts.append(jnp.zeros((SMALL_ROWS * D - SMALL_USED - 1,), F32))
    return jnp.concatenate(parts).reshape(SMALL_ROWS, D)


def _unpack_small(v):
    flat, out, r = v.reshape(-1), {}, 0
    for n, k in SMALL:
        out[n] = flat[r:r + k]
        r += k
    return out, flat[r]


def _place():
    return lax.axis_index("x"), lax.axis_index("y"), lax.axis_index("c")


def _other_chips(x, y):
    return [(1 - x, y), (x, 1 - y), (1 - x, 1 - y)]


HBM_SPEC = pl.BlockSpec(memory_space=pl.ANY)


CHIP_SEMS = (pltpu.SemaphoreType.DMA((3,)), pltpu.SemaphoreType.DMA((3,)), pltpu.SemaphoreType.DMA)


def _gather_copies(ins, outs, send_sems, recv_sems, local_sem):
    (src,), (out,) = ins, outs
    x, y, c = _place()
    me = 2 * x + y
    copies = [pltpu.make_async_copy(src, out.at[me], local_sem)]
    copies += [pltpu.make_async_remote_copy(src_ref=src, dst_ref=out.at[me], send_sem=send_sems.at[k], recv_sem=recv_sems.at[k],
                                            device_id=(px, py, c), device_id_type=MESH)
               for k, (px, py) in enumerate(_other_chips(x, y))]
    return copies


def _scatter_copies(ins, outs, send_sems, recv_sems, local_sem):
    (src,), (out,) = ins, outs
    x, y, c = _place()
    return [pltpu.make_async_remote_copy(src_ref=src.at[2 * px + py], dst_ref=out.at[k], send_sem=send_sems.at[k],
                                         recv_sem=recv_sems.at[k], device_id=(px, py, c), device_id_type=MESH)
            for k, (px, py) in enumerate(_other_chips(x, y))]


def _all_gather_chips(pack):
    def body(src, out, *sems):
        copies = _gather_copies((src,), (out,), *sems)
        for cp in copies:
            cp.start()
        for cp in copies:
            cp.wait()

    return pl.pallas_call(
        body, name="weights_all_gather", in_specs=[HBM_SPEC], out_specs=HBM_SPEC,
        out_shape=SDS((N_CHIPS,) + pack.shape, pack.dtype), scratch_shapes=list(CHIP_SEMS),
    )(pack)


def _swap_with_sibling(name, v):
    def body(src, out, send_sem, recv_sem):
        x, y, c = _place()
        cp = pltpu.make_async_remote_copy(src_ref=src, dst_ref=out, send_sem=send_sem, recv_sem=recv_sem,
                                          device_id=(x, y, 1 - c), device_id_type=MESH)
        cp.start()
        cp.wait()

    return pl.pallas_call(
        body, name=name, in_specs=[HBM_SPEC], out_specs=HBM_SPEC, out_shape=SDS(v.shape, v.dtype),
        scratch_shapes=[pltpu.SemaphoreType.DMA, pltpu.SemaphoreType.DMA],
    )(v)


def _all_reduce_small(v):
    n_dev = 8

    def body(v_ref, out_ref, land, send_sems, recv_sems):
        x, y, c = _place()
        me = 4 * x + 2 * y + c
        land[me] = v_ref[...]
        copies = []
        for k in range(1, n_dev):
            peer = (x ^ (k >> 2), y ^ ((k >> 1) & 1), c ^ (k & 1))
            copies.append(pltpu.make_async_remote_copy(src_ref=v_ref, dst_ref=land.at[me], send_sem=send_sems.at[k - 1],
                                                       recv_sem=recv_sems.at[k - 1], device_id=peer, device_id_type=MESH))
        for cp in copies:
            cp.start()
        for cp in copies:
            cp.wait()
        acc = land[0]
        for d in range(1, n_dev):
            acc = acc + land[d]
        out_ref[...] = acc

    return pl.pallas_call(
        body, name="small_all_reduce", in_specs=[pl.BlockSpec(memory_space=pltpu.VMEM)],
        out_specs=pl.BlockSpec(memory_space=pltpu.VMEM), out_shape=SDS(v.shape, v.dtype),
        scratch_shapes=[pltpu.VMEM((n_dev,) + v.shape, v.dtype), pltpu.SemaphoreType.DMA((n_dev - 1,)),
                        pltpu.SemaphoreType.DMA((n_dev - 1,))],
    )(v)


def _adamw(g, wv, m, v):
    m = ADAM_B1 * m + (1.0 - ADAM_B1) * g
    v = ADAM_B2 * v + (1.0 - ADAM_B2) * (g * g)
    m_hat = m / (1.0 - ADAM_B1 ** ADAM_STEP)
    v_hat = v / (1.0 - ADAM_B2 ** ADAM_STEP)
    delta = -ADAM_LR * (m_hat / (jnp.sqrt(v_hat) + ADAM_EPS) + ADAM_WD * wv)
    return delta, m, v


def kernel(x, mem, ffn1_norm, ffn1_w1, ffn1_w3, ffn1_w2, mix_norm, mem_norm, w_in, w_mem_kv, qn_dsa, kn_dsa, qn_mem, kn_mem, w_branch_sb, w_branch_dsa, w_branch_mem, w_gate, b_gate, w_out, ffn2_norm, ffn2_w1, ffn2_w3, ffn2_w2, loss_target, m_ffn1_norm, m_ffn1_w1, m_ffn1_w3, m_ffn1_w2, m_mix_norm, m_mem_norm, m_w_in, m_w_mem_kv, m_qn_dsa, m_kn_dsa, m_qn_mem, m_kn_mem, m_w_branch_sb, m_w_branch_dsa, m_w_branch_mem, m_w_gate, m_b_gate, m_w_out, m_ffn2_norm, m_ffn2_w1, m_ffn2_w3, m_ffn2_w2, v_ffn1_norm, v_ffn1_w1, v_ffn1_w3, v_ffn1_w2, v_mix_norm, v_mem_norm, v_w_in, v_w_mem_kv, v_qn_dsa, v_kn_dsa, v_qn_mem, v_kn_mem, v_w_branch_sb, v_w_branch_dsa, v_w_branch_mem, v_w_gate, v_b_gate, v_w_out, v_ffn2_norm, v_ffn2_w1, v_ffn2_w3, v_ffn2_w2):
    given = dict(locals())
    wts = {n: given[n][0] for n in WEIGHTS}
    moms = {n: given["m_" + n][0] for n in WEIGHTS}
    vars_ = {n: given["v_" + n][0] for n in WEIGHTS}

    plan = _Plan()
    x_i, y_i, _ = _place()
    my_chip = 2 * x_i + y_i

    full = {}
    for host, names in WEIGHT_PIECES:
        pack = _pack(wts, names).astype(BF16)
        if host is None:
            full.update(_full_from_packs(_all_gather_chips(pack), names))
        else:
            plan.put(host, _Carry([pack], [SDS((N_CHIPS,) + pack.shape, BF16)], CHIP_SEMS, _gather_copies,
                                  lambda res, names=names: full.update(_full_from_packs(res[0], names))))
    small = {n: wts[n].reshape(1, -1) for n, _ in SMALL}

    landed = {}

    def on_grads(group, grads):
        names = GROUPS[group]
        packs = _packs_from_full(grads, names, BF16)
        own = _own_pack(grads, names, my_chip)
        plan.put(GRAD_HOSTS[group], _Carry([packs], [SDS((3,) + packs.shape[1:], BF16)], CHIP_SEMS, _scatter_copies,
                                           lambda res: landed.update({group: (own, res[0])})))

    loss, gx, _, gs = _local_step(x[0], mem[0], loss_target[0], full, small, plan, on_grads)
    assert not plan.pending, list(plan.pending)

    def update(hv, ov, wv, mv, vv):
        g = hv + ov
        return (g,) + _adamw(g, wv, mv, vv)

    outs = [{}, {}, {}, {}]
    for group, names in GROUPS.items():
        own, got = landed[group]
        half = _tokmap(f"grads_sum_chips_{group}",
                       lambda a, b0, b1, b2: ((a + b0.astype(F32)) + b1.astype(F32)) + b2.astype(F32),
                       [own, got[0], got[1], got[2]], [], [(D, F32)])[0]
        other = _swap_with_sibling(f"grads_swap_cores_{group}", half)
        res = _tokmap(f"adamw_{group}", update, [half, other, _pack(wts, names), _pack(moms, names), _pack(vars_, names)],
                      [], [(D, F32)] * 4)
        for d, packed in zip(outs, res):
            d.update(_unpack(packed, names))

    s_red = _all_reduce_small(_pack_small(gs, loss[0, 0]))
    res = _tokmap(
        "adamw_small", lambda g, wv, mv, vv: (g,) + _adamw(g, wv, mv, vv),
        [s_red, _pack_small(small), _pack_small({n: moms[n] for n, _ in SMALL}), _pack_small({n: vars_[n] for n, _ in SMALL})],
        [], [(D, F32)] * 4)
    for d, packed in zip(outs, res):
        d.update(_unpack_small(packed)[0])
    _, total_loss = _unpack_small(s_red)
    return (total_loss, gx[None], *[d[n][None] for d in outs for n in WEIGHTS])
```

```python
import functools

import numpy as np
import jax
import jax.numpy as jnp
from jax import lax
from jax.experimental import pallas as pl
from jax.experimental.pallas import tpu as pltpu

F32, BF16 = jnp.float32, jnp.bfloat16
SDS = jax.ShapeDtypeStruct
MESH = pl.DeviceIdType.MESH

D = 1024
HD = 64
QB = 128
D_FF = 2816
SB_W, DSA_W, DSA_OUT_W, MEM_W = 512, 768, 256, 256
DSA_DILS = (1, 4, 16)
MEM_LEN = 256
N_CHIPS = 4
EPS = 1e-6
SCALE = HD ** -0.5
EXHAUSTED = -104.0
SB_WIN = 384
NEG = -1e30
VMEM_LIMIT = 56 * 1024 * 1024

ADAM_LR, ADAM_B1, ADAM_B2, ADAM_EPS, ADAM_WD, ADAM_STEP = 0.001, 0.9, 0.999, 1e-08, 0.01, 10

NN = (((1,), (0,)), ((), ()))
NT = (((1,), (1,)), ((), ()))
TN = (((0,), (0,)), ((), ()))

SHARDED = (
    ("ffn1_w1", (D, D_FF), 1), ("ffn1_w3", (D, D_FF), 1), ("ffn1_w2", (D_FF, D), 0),
    ("w_in", (D, 4096), 1), ("w_mem_kv", (D, 512), 0),
    ("w_branch_sb", (SB_W, D), 1), ("w_branch_dsa", (DSA_OUT_W, D), 1), ("w_branch_mem", (MEM_W, D), 1),
    ("w_gate", (D, 3 * D), 1), ("w_out", (D, D), 0),
    ("ffn2_w1", (D, D_FF), 1), ("ffn2_w3", (D, D_FF), 1), ("ffn2_w2", (D_FF, D), 0),
)
SHARDED_BY_NAME = {n: (sh, ax) for n, sh, ax in SHARDED}
GROUPS = {
    "ffn2": ("ffn2_w1", "ffn2_w3", "ffn2_w2"),
    "mid": ("w_in", "w_mem_kv", "w_branch_sb", "w_branch_dsa", "w_branch_mem", "w_gate", "w_out"),
    "ffn1": ("ffn1_w1", "ffn1_w3", "ffn1_w2"),
}
WEIGHT_PIECES = (
    (None, ("ffn1_w1", "ffn1_w3")),
    ("ffn1_up", ("ffn1_w2", "w_in")),
    ("ffn1_down", ("w_gate", "w_mem_kv", "w_branch_sb", "w_branch_dsa", "w_branch_mem", "w_out")),
    ("proj_dsa", ("ffn2_w2",)),
    ("proj_gate", ("ffn2_w1", "ffn2_w3")),
)
GRAD_HOSTS = {"ffn2": "ffn2_bwd_dn", "mid": "ffn1_bwd_dw13", "ffn1": "ffn1_bwd_dn"}
SMALL = (("ffn1_norm", D), ("mix_norm", D), ("mem_norm", D), ("ffn2_norm", D), ("b_gate", 3 * D),
         ("qn_dsa", HD), ("kn_dsa", HD), ("qn_mem", HD), ("kn_mem", HD))
WEIGHTS = ("ffn1_norm", "ffn1_w1", "ffn1_w3", "ffn1_w2", "mix_norm", "mem_norm", "w_in", "w_mem_kv", "qn_dsa", "kn_dsa",
           "qn_mem", "kn_mem", "w_branch_sb", "w_branch_dsa", "w_branch_mem", "w_gate", "b_gate", "w_out", "ffn2_norm",
           "ffn2_w1", "ffn2_w3", "ffn2_w2")
SMALL_ROWS = 8


def _dot(a, b, dn=NN):
    return lax.dot_general(a, b, dn, preferred_element_type=F32)


def _dot01(x, m01):
    hi = x.astype(BF16)
    r1 = x - hi.astype(F32)
    mid = r1.astype(BF16)
    lo = (r1 - mid.astype(F32)).astype(BF16)
    return _dot(hi, m01) + _dot(mid, m01) + _dot(lo, m01)


def _pick(n, cands):
    for c in cands:
        if n % c == 0:
            return c
    raise ValueError(f"no tile for {n}")


def _tokmap(name, fn, tok_ins, consts, tok_outs, acc_outs=(), tile=512):
    n = tok_ins[0].shape[0]
    tile = _pick(n, [t for t in (512, 256, 128, 64, 32, 16, 8) if t <= tile])
    n_in, n_tok, n_acc = len(tok_ins) + len(consts), len(tok_outs), len(acc_outs)

    def body(*refs):
        outs = fn(*[r[...] for r in refs[:n_in]])
        outs = outs if isinstance(outs, (tuple, list)) else (outs,)
        assert len(outs) == n_tok + n_acc, (name, len(outs))
        orefs = refs[n_in:]
        for r, v in zip(orefs[:n_tok], outs[:n_tok]):
            r[...] = v.astype(r.dtype)
        if n_acc:
            @pl.when(pl.program_id(0) == 0)
            def _():
                for r in orefs[n_tok:]:
                    r[...] = jnp.zeros(r.shape, r.dtype)
            for r, v in zip(orefs[n_tok:], outs[n_tok:]):
                r[...] += v.astype(r.dtype)

    in_specs = [pl.BlockSpec((tile, a.shape[1]), lambda i: (i, 0)) for a in tok_ins]
    in_specs += [pl.BlockSpec(c.shape, lambda i: (0, 0)) for c in consts]
    out_specs = [pl.BlockSpec((tile, w), lambda i: (i, 0)) for w, _ in tok_outs]
    out_specs += [pl.BlockSpec(s, lambda i: (0, 0)) for s in acc_outs]
    out_shape = [SDS((n, w), dt) for w, dt in tok_outs] + [SDS(s, F32) for s in acc_outs]
    res = pl.pallas_call(
        body, name=name, grid=(n // tile,), in_specs=in_specs, out_specs=out_specs, out_shape=out_shape,
        compiler_params=pltpu.CompilerParams(dimension_semantics=("arbitrary",), vmem_limit_bytes=VMEM_LIMIT),
    )(*tok_ins, *consts)
    return res


MATMUL_VMEM_BUDGET = 40 * 1024 * 1024


def _matmul_tiles(m, n, k, a_bytes, b_bytes, o_bytes, extra_bytes):
    best = None
    for tk in [c for c in (3584, 2816, 2048, 1408, 1024, 512, 256, 128) if k % c == 0]:
        for tm in [c for c in (1408, 1024, 768, 512, 256, 128) if m % c == 0]:
            for tn in [c for c in (1408, 1024, 768, 512, 256, 128) if n % c == 0]:
                need = 2 * tk * (tm * a_bytes + tn * b_bytes) + tm * tn * (2 * o_bytes + 2 * extra_bytes + 8)
                if need > MATMUL_VMEM_BUDGET:
                    continue
                score = (min(tm, 512) * min(tn, 512), tk, tm * tn)
                if best is None or score > best[0]:
                    best = (score, (tm, tn, tk))
    return best[1]


class _Carry:
    def __init__(self, ins, outs, sems, copies, then):
        self.ins, self.outs, self.sems, self.copies, self.then = ins, outs, sems, copies, then


class _Plan:
    def __init__(self):
        self.pending = {}

    def put(self, host, carry):
        assert host not in self.pending, host
        self.pending[host] = carry

    def take(self, host):
        return self.pending.pop(host, None)


def _matmul(name, a, b, dn, out_dtype, epi=None, tiles=(), rows=(), plan=None):
    if dn == NN:
        (m, k), n = a.shape, b.shape[1]
    elif dn == NT:
        (m, k), n = a.shape, b.shape[0]
    else:
        (k, m), n = a.shape, b.shape[1]
    n_t, n_r = len(tiles), len(rows)
    tm, tn, tk = _matmul_tiles(m, n, k, a.dtype.itemsize, b.dtype.itemsize, jnp.dtype(out_dtype).itemsize,
                               sum(t.dtype.itemsize for t in tiles))
    nk = k // tk
    grid = (m // tm, n // tn, nk)
    carry = plan.take(name) if plan is not None else None
    n_ci, n_co = (len(carry.ins), len(carry.outs)) if carry else (0, 0)

    def body(a_ref, b_ref, *rest):
        extras, rest = rest[:n_t + n_r], rest[n_t + n_r:]
        c_in, o_ref, c_out, scratch = rest[:n_ci], rest[n_ci], rest[n_ci + 1:n_ci + 1 + n_co], rest[n_ci + 1 + n_co:]
        ids = [pl.program_id(d) for d in range(3)]
        if carry:
            sems = scratch[1:] if nk > 1 else scratch

            @pl.when((ids[0] == 0) & (ids[1] == 0) & (ids[2] == 0))
            def _():
                for cp in carry.copies(c_in, c_out, *sems):
                    cp.start()

        part = _dot(a_ref[...].astype(BF16), b_ref[...].astype(BF16), dn)

        def finish(r):
            if epi is not None:
                r = epi(r, *[e[...] for e in extras])
            o_ref[...] = r.astype(o_ref.dtype)

        if nk == 1:
            finish(part)
        else:
            acc = scratch[0]

            @pl.when(ids[2] == 0)
            def _():
                acc[...] = part

            @pl.when(ids[2] > 0)
            def _():
                acc[...] += part

            @pl.when(ids[2] == nk - 1)
            def _():
                finish(acc[...])

        if carry:
            @pl.when((ids[0] == grid[0] - 1) & (ids[1] == grid[1] - 1) & (ids[2] == nk - 1))
            def _():
                for cp in carry.copies(c_in, c_out, *sems):
                    cp.wait()

    a_spec = pl.BlockSpec((tk, tm), lambda i, j, kk: (kk, i)) if dn == TN else pl.BlockSpec((tm, tk), lambda i, j, kk: (i, kk))
    b_spec = pl.BlockSpec((tn, tk), lambda i, j, kk: (j, kk)) if dn == NT else pl.BlockSpec((tk, tn), lambda i, j, kk: (kk, j))
    in_specs = [a_spec, b_spec] + [pl.BlockSpec((tm, tn), lambda i, j, kk: (i, j)) for _ in tiles]
    in_specs += [pl.BlockSpec((1, tn), lambda i, j, kk: (0, j)) for _ in rows] + [HBM_SPEC] * n_ci
    res = pl.pallas_call(
        body, name=name, grid=grid, in_specs=in_specs,
        out_specs=[pl.BlockSpec((tm, tn), lambda i, j, kk: (i, j))] + [HBM_SPEC] * n_co,
        out_shape=[SDS((m, n), out_dtype)] + (list(carry.outs) if carry else []),
        scratch_shapes=([pltpu.VMEM((tm, tn), F32)] if nk > 1 else []) + (list(carry.sems) if carry else []),
        compiler_params=pltpu.CompilerParams(
            dimension_semantics=("arbitrary",) * 3 if carry else ("parallel", "parallel", "arbitrary"),
            vmem_limit_bytes=VMEM_LIMIT),
    )(a, b, *tiles, *rows, *(carry.ins if carry else []))
    if carry:
        carry.then(res[1:])
    return res[0]


def _mean_all(v):
    return jnp.mean(v, axis=-1, keepdims=True)


def _head_sums(v, bd):
    w = bd.shape[0]
    return jnp.concatenate([_dot01(v[:, j:j + w], bd) for j in range(0, v.shape[1], w)], axis=1)


def _mean_heads(bd):
    return lambda v: _head_sums(v, bd) * (1.0 / HD)


def _rms_fwd(x, g, mean):
    return x * lax.rsqrt(mean(x * x) + EPS) * g


def _rms_bwd(x, g, dy, mean):
    r = lax.rsqrt(mean(x * x) + EPS)
    dn = dy * g
    dx = r * dn - x * (r * r * r) * mean(dn * x)
    return dx, jnp.sum(dy * x * r, axis=0, keepdims=True)


def _swap_halves(x):
    w = x.shape[1]
    lane = lax.broadcasted_iota(jnp.int32, x.shape, 1)
    return jnp.where(lane % HD < HD // 2, pltpu.roll(x, w - HD // 2, 1), pltpu.roll(x, HD // 2, 1))


def _lanes(t, w):
    return jnp.tile(t, (1, w // t.shape[1]))


def _rope_fwd(x, cos, sin_signed):
    return x * _lanes(cos, x.shape[1]) + _swap_halves(x) * _lanes(sin_signed, x.shape[1])


def _rope_bwd(dy, cos, sin_signed):
    return dy * _lanes(cos, dy.shape[1]) + _swap_halves(dy * _lanes(sin_signed, dy.shape[1]))


def _bcast_heads(cols):
    return jnp.concatenate([jnp.broadcast_to(c, (c.shape[0], HD)) for c in cols], axis=1)


def _softplus(z):
    return jnp.maximum(z, 0.0) + jnp.log1p(jnp.exp(-jnp.abs(z)))


def _block_diag(w):
    h = np.arange(w) // HD
    return jnp.asarray(h[:, None] == h[None, :], BF16)


def _sb_window(i, t):
    hi = (i + 1) * QB - t * SB_WIN
    lo = hi - SB_WIN
    ws = pl.multiple_of(jnp.maximum(lo, 0), QB)
    kpos = ws + lax.broadcasted_iota(jnp.int32, (QB, SB_WIN), 1)
    qpos = i * QB + lax.broadcasted_iota(jnp.int32, (QB, SB_WIN), 0)
    return (kpos < qpos) & (kpos >= lo) & (kpos < hi), ws


def _sb_fwd(qkv):
    s = qkv.shape[0]
    assert s >= SB_WIN
    nq = s // QB
    npairs = SB_W // 128

    def body(q_ref, k_ref, v_ref, later_ref, o_ref, tot_ref, nb_ref):
        p, i = pl.program_id(0), pl.program_id(1)
        q = q_ref[...]
        later_of = later_ref[...]

        def step(c):
            t, _, tots, outs = c
            mask, ws = _sb_window(i, t)
            kw, vw = k_ref[pl.ds(ws, SB_WIN), :], v_ref[pl.ds(ws, SB_WIN), :]
            new_t, new_o = [], []
            for hh in range(2):
                sl = slice(HD * hh, HD * hh + HD)
                z = _dot(q[:, sl], kw[:, sl], NT) * SCALE
                sp = _softplus(z)
                lf = jnp.where(mask, -sp, 0.0)
                later = tots[hh] + _dot01(lf, later_of)
                w = jnp.where(mask, jnp.exp(z - sp + later), 0.0)
                new_o.append(outs[hh] + _dot(w.astype(BF16), vw[:, sl]))
                new_t.append(tots[hh] + jnp.sum(lf, axis=1, keepdims=True))
            alive = jnp.maximum(jnp.max(new_t[0]), jnp.max(new_t[1]))
            return t + 1, alive, tuple(new_t), tuple(new_o)

        zt, zo = jnp.zeros((QB, 1), F32), jnp.zeros((QB, HD), F32)
        t, _, tots, outs = lax.while_loop(lambda c: ((i + 1) * QB - c[0] * SB_WIN > 0) & (c[1] > EXHAUSTED), step,
                                          (jnp.int32(0), jnp.float32(0.0), (zt, zt), (zo, zo)))
        o_ref[...] = jnp.concatenate(outs, axis=1).astype(o_ref.dtype)
        tot_ref[...] = _bcast_heads(tots)
        nb_ref[p, i] = t

    whole = lambda off: pl.BlockSpec((s, 128), lambda p, i: (0, off + p), pipeline_mode=pl.Buffered(1))
    tile = pl.BlockSpec((QB, 128), lambda p, i: (i, p))
    tri = pl.BlockSpec((SB_WIN, SB_WIN), lambda p, i: (0, 0), pipeline_mode=pl.Buffered(1))
    idx = np.arange(SB_WIN)
    return pl.pallas_call(
        body, name="sb_fwd", grid=(npairs, nq),
        in_specs=[tile, whole(npairs), whole(2 * npairs), tri],
        out_specs=[tile, tile, pl.BlockSpec(memory_space=pltpu.SMEM)],
        out_shape=[SDS((s, SB_W), BF16), SDS((s, SB_W), F32), SDS((npairs, nq), jnp.int32)],
        compiler_params=pltpu.CompilerParams(dimension_semantics=("arbitrary", "arbitrary"), vmem_limit_bytes=VMEM_LIMIT),
    )(qkv, qkv, qkv, jnp.asarray(idx[:, None] > idx[None, :], BF16))


def _sb_bwd(qkv, do, tot, nblk):
    s = qkv.shape[0]
    nq = s // QB
    npairs = SB_W // 128

    def body(nb_ref, q_ref, k_ref, v_ref, do_ref, tot_ref, upto_ref, before_ref, dq_ref, dk_ref, dv_ref):
        p, i = pl.program_id(0), pl.program_id(1)

        @pl.when(i == 0)
        def _():
            dk_ref[...] = jnp.zeros(dk_ref.shape, F32)
            dv_ref[...] = jnp.zeros(dv_ref.shape, F32)

        upto = upto_ref[...]
        before = before_ref[...]
        q, dout, tt = q_ref[...], do_ref[...], tot_ref[...]
        n = nb_ref[p, i]

        def step(it, c):
            pres, gpres, dqs = c
            mask, ws = _sb_window(i, n - 1 - it)
            kw, vw = k_ref[pl.ds(ws, SB_WIN), :], v_ref[pl.ds(ws, SB_WIN), :]
            new_p, new_g, new_dq, dks, dvs = [], [], [], [], []
            for hh in range(2):
                sl = slice(HD * hh, HD * hh + HD)
                z = _dot(q[:, sl], kw[:, sl], NT) * SCALE
                sp = _softplus(z)
                lf = jnp.where(mask, -sp, 0.0)
                later = tt[:, HD * hh:HD * hh + 1] - (pres[hh] + _dot01(lf, upto))
                w = jnp.where(mask, jnp.exp(z - sp + later), 0.0)
                beta = jnp.exp(z - sp)
                g = _dot(dout[:, sl], vw[:, sl], NT) * w
                g_far = gpres[hh] + _dot(g.astype(BF16), before)
                dz = (jnp.where(mask, g * (1.0 - beta) - beta * g_far, 0.0) * SCALE).astype(BF16)
                new_dq.append(dqs[hh] + _dot(dz, kw[:, sl]))
                dks.append(_dot(dz, q[:, sl], TN))
                dvs.append(_dot(w.astype(BF16), dout[:, sl], TN))
                new_p.append(pres[hh] + jnp.sum(lf, axis=1, keepdims=True))
                new_g.append(gpres[hh] + jnp.sum(g, axis=1, keepdims=True))
            dk_ref[pl.ds(ws, SB_WIN), :] += jnp.concatenate(dks, axis=1)
            dv_ref[pl.ds(ws, SB_WIN), :] += jnp.concatenate(dvs, axis=1)
            return tuple(new_p), tuple(new_g), tuple(new_dq)

        zt, zo = jnp.zeros((QB, 1), F32), jnp.zeros((QB, HD), F32)
        _, _, dqs = lax.fori_loop(0, n, step, ((zt, zt), (zt, zt), (zo, zo)))
        dq_ref[...] = jnp.concatenate(dqs, axis=1)

    whole_in = lambda off: pl.BlockSpec((s, 128), lambda p, i: (0, off + p), pipeline_mode=pl.Buffered(1))
    whole_out = pl.BlockSpec((s, 128), lambda p, i: (0, p), pipeline_mode=pl.Buffered(1))
    tile = pl.BlockSpec((QB, 128), lambda p, i: (i, p))
    tri = pl.BlockSpec((SB_WIN, SB_WIN), lambda p, i: (0, 0), pipeline_mode=pl.Buffered(1))
    idx = np.arange(SB_WIN)
    return pl.pallas_call(
        body, name="sb_bwd", grid=(npairs, nq),
        in_specs=[pl.BlockSpec(memory_space=pltpu.SMEM), tile, whole_in(npairs), whole_in(2 * npairs), tile, tile, tri, tri],
        out_specs=[tile, whole_out, whole_out],
        out_shape=[SDS((s, SB_W), F32)] * 3,
        compiler_params=pltpu.CompilerParams(dimension_semantics=("arbitrary", "arbitrary"), vmem_limit_bytes=VMEM_LIMIT),
    )(nblk, qkv, qkv, qkv, do, tot, jnp.asarray(idx[:, None] <= idx[None, :], BF16), jnp.asarray(idx[:, None] < idx[None, :], BF16))


def _win_masks(has_prev):
    row = lax.broadcasted_iota(jnp.int32, (QB, QB), 0)
    col = lax.broadcasted_iota(jnp.int32, (QB, QB), 1)
    return col <= row, (col >= row) & has_prev


def _dsa_fwd(q, k, v, dil):
    s = q.shape[0]
    n = s // dil
    nb = n // QB
    q, k, v = (t.reshape(n, dil * DSA_OUT_W) for t in (q, k, v))

    def body(q_ref, kc_ref, kp_ref, vc_ref, vp_ref, o_ref, lse_ref):
        m_cur, m_prev = _win_masks(pl.program_id(1) > 0)
        outs, lses = [], []
        for hh in range(DSA_OUT_W // HD):
            sl = slice(HD * hh, HD * hh + HD)
            qh = q_ref[:, sl]
            sc = jnp.where(m_cur, _dot(qh, kc_ref[:, sl], NT) * SCALE, NEG)
            sp = jnp.where(m_prev, _dot(qh, kp_ref[:, sl], NT) * SCALE, NEG)
            m = jnp.maximum(jnp.max(sc, axis=1, keepdims=True), jnp.max(sp, axis=1, keepdims=True))
            pc, pp = jnp.exp(sc - m), jnp.exp(sp - m)
            den = jnp.sum(pc, axis=1, keepdims=True) + jnp.sum(pp, axis=1, keepdims=True)
            outs.append((_dot(pc.astype(BF16), vc_ref[:, sl]) + _dot(pp.astype(BF16), vp_ref[:, sl])) / den)
            lses.append(m + jnp.log(den))
        o_ref[...] = jnp.concatenate(outs, axis=1)
        lse_ref[...] = _bcast_heads(lses)

    cur = pl.BlockSpec((QB, DSA_OUT_W), lambda c, i: (i, c))
    prev = pl.BlockSpec((QB, DSA_OUT_W), lambda c, i: (jnp.maximum(i - 1, 0), c))
    o, lse = pl.pallas_call(
        body, name=f"dsa_fwd_d{dil}", grid=(dil, nb), in_specs=[cur, cur, prev, cur, prev], out_specs=[cur, cur],
        out_shape=[SDS((n, dil * DSA_OUT_W), F32)] * 2,
        compiler_params=pltpu.CompilerParams(dimension_semantics=("parallel", "parallel")),
    )(q, k, k, v, v)
    return o.reshape(s, DSA_OUT_W), lse.reshape(s, DSA_OUT_W)


def _dsa_bwd(q, k, v, do, cc, lse, dil):
    s = q.shape[0]
    n = s // dil
    nb = n // QB
    q, k, v, do, cc, lse = (t.reshape(n, dil * DSA_OUT_W) for t in (q, k, v, do, cc, lse))

    def body(qj_ref, qn_ref, kp_ref, kj_ref, vp_ref, vj_ref, doj_ref, don_ref, cj_ref, cn_ref, lj_ref, ln_ref,
             dq_ref, dk_ref, dv_ref):
        j = pl.program_id(1)
        m_cur, m_prev = _win_masks(j > 0)
        _, m_next = _win_masks(j + 1 < nb)
        dqs, dks, dvs = [], [], []
        for hh in range(DSA_OUT_W // HD):
            sl = slice(HD * hh, HD * hh + HD)
            one = slice(HD * hh, HD * hh + 1)
            qj, qn, kp, kj, vp, vj = (r[:, sl] for r in (qj_ref, qn_ref, kp_ref, kj_ref, vp_ref, vj_ref))
            doj, don = doj_ref[:, sl], don_ref[:, sl]

            def dscore(qq, kk, vv, dd, c_ref, l_ref, mask):
                prob = jnp.where(mask, jnp.exp(_dot(qq, kk, NT) * SCALE - l_ref[:, one]), 0.0)
                return prob, (prob * (_dot(dd, vv, NT) + c_ref[:, one]) * SCALE).astype(BF16)

            _, ds_a = dscore(qj, kp, vp, doj, cj_ref, lj_ref, m_prev)
            p_b, ds_b = dscore(qj, kj, vj, doj, cj_ref, lj_ref, m_cur)
            p_c, ds_c = dscore(qn, kj, vj, don, cn_ref, ln_ref, m_next)
            dqs.append(_dot(ds_a, kp) + _dot(ds_b, kj))
            dks.append(_dot(ds_b, qj, TN) + _dot(ds_c, qn, TN))
            dvs.append(_dot(p_b.astype(BF16), doj, TN) + _dot(p_c.astype(BF16), don, TN))
        dq_ref[...] = jnp.concatenate(dqs, axis=1)
        dk_ref[...] = jnp.concatenate(dks, axis=1)
        dv_ref[...] = jnp.concatenate(dvs, axis=1)

    cur = pl.BlockSpec((QB, DSA_OUT_W), lambda c, j: (j, c))
    prev = pl.BlockSpec((QB, DSA_OUT_W), lambda c, j: (jnp.maximum(j - 1, 0), c))
    nxt = pl.BlockSpec((QB, DSA_OUT_W), lambda c, j: (jnp.minimum(j + 1, nb - 1), c))
    dq, dk, dv = pl.pallas_call(
        body, name=f"dsa_bwd_d{dil}", grid=(dil, nb),
        in_specs=[cur, nxt, prev, cur, prev, cur, cur, nxt, cur, nxt, cur, nxt], out_specs=[cur, cur, cur],
        out_shape=[SDS((n, dil * DSA_OUT_W), F32)] * 3,
        compiler_params=pltpu.CompilerParams(dimension_semantics=("parallel", "parallel")),
    )(q, q, k, k, v, v, do, do, cc, cc, lse, lse)
    return tuple(t.reshape(s, DSA_OUT_W) for t in (dq, dk, dv))


def _ffn_fwd(tag, x, gain, w13, w2, plan=None):
    n = _tokmap(f"{tag}_norm", lambda xv, g: _rms_fwd(xv, g, _mean_all), [x], [gain], [(D, BF16)])[0]
    ab = _matmul(f"{tag}_up", n, w13, NN, BF16, plan=plan)

    def gate(abv):
        a, b = abv[:, :D_FF].astype(F32), abv[:, D_FF:].astype(F32)
        return a * jax.nn.sigmoid(a) * b

    h = _tokmap(f"{tag}_gate", gate, [ab], [], [(D_FF, BF16)], tile=256)[0]
    y = _matmul(f"{tag}_down", h, w2(), NN, F32, epi=lambda acc, res: res + 0.5 * acc, tiles=[x], plan=plan)
    return y, (n, ab)


def _ffn_bwd(tag, x, gain, w13, w2, saved, dy, plan=None, on_dw=None):
    n, ab = saved
    dh = _matmul(f"{tag}_bwd_dh", dy, w2, NT, BF16, epi=lambda acc: 0.5 * acc)

    def gate_bwd(abv, dhv):
        a, b, dhf = abv[:, :D_FF].astype(F32), abv[:, D_FF:].astype(F32), dhv.astype(F32)
        sg = jax.nn.sigmoid(a)
        silu = a * sg
        da = dhf * b * (sg * (1.0 + a * (1.0 - sg)))
        return jnp.concatenate([da, dhf * silu], axis=1), silu * b

    dab, h = _tokmap(f"{tag}_bwd_gate", gate_bwd, [ab, dh], [], [(2 * D_FF, BF16), (D_FF, BF16)], tile=256)
    dw2 = _matmul(f"{tag}_bwd_dw2", h, dy, TN, F32, epi=lambda acc: 0.5 * acc)
    dw13 = _matmul(f"{tag}_bwd_dw13", n, dab, TN, F32, plan=plan)
    if on_dw is not None:
        on_dw(dw13, dw2)
    dn = _matmul(f"{tag}_bwd_dn", dab, w13, NT, F32, plan=plan)

    def norm_bwd(xv, dnv, dyv, g):
        dx, dg = _rms_bwd(xv, g, dnv, _mean_all)
        return dx + dyv, dg

    dx, dgain = _tokmap(f"{tag}_bwd_norm", norm_bwd, [x, dn, dy], [gain], [(D, F32)], [(1, D)])
    return dx, dgain, dw13, dw2


def _rope_tables(s):
    half = HD // 2
    inv_freq = jnp.power(10000.0, -jnp.arange(half, dtype=F32) / half)
    ang = jnp.arange(s).astype(F32)[:, None] * inv_freq[None, :]
    cos, sin = jnp.cos(ang), jnp.sin(ang)
    return jnp.tile(jnp.concatenate([cos, cos], axis=1), (1, 2)), jnp.tile(jnp.concatenate([-sin, sin], axis=1), (1, 2))


def _local_step(x, mem, tgt, w, sm, plan=None, on_grads=None):
    s = x.shape[0]
    assert s % (QB * max(DSA_DILS)) == 0
    on_grads = on_grads or (lambda group, grads: None)
    c_sb, c_dsa, c_qm = 3 * SB_W, 3 * SB_W + 3 * DSA_W, 4096
    cos, sin = _rope_tables(s)
    bd768 = bd256 = _block_diag(128)
    gq_dsa, gk_dsa = jnp.tile(sm["qn_dsa"], (1, DSA_W // HD)), jnp.tile(sm["kn_dsa"], (1, DSA_W // HD))
    gq_mem, gk_mem = jnp.tile(sm["qn_mem"], (1, MEM_W // HD)), jnp.tile(sm["kn_mem"], (1, MEM_W // HD))

    w13_1 = jnp.concatenate([w["ffn1_w1"], w["ffn1_w3"]], axis=1)
    x1, ffn1_saved = _ffn_fwd("ffn1", x, sm["ffn1_norm"], w13_1, lambda: w["ffn1_w2"], plan)
    w_all = jnp.concatenate([w["w_in"], w["w_gate"]], axis=1)
    wb_sb, wb_dsa, wb_mem = w["w_branch_sb"], w["w_branch_dsa"], w["w_branch_mem"]
    hmix = _tokmap("mix_norm", lambda xv, g: _rms_fwd(xv, g, _mean_all), [x1], [sm["mix_norm"]], [(D, BF16)])[0]
    qkv_sb = _matmul("proj_sb", hmix, w_all[:, :c_sb], NN, BF16)
    qkv_dsa = _matmul("proj_dsa", hmix, w_all[:, c_sb:c_dsa], NN, BF16, plan=plan)
    q_mem = _matmul("proj_qmem", hmix, w_all[:, c_dsa:c_qm], NN, BF16)
    gpre = _matmul("proj_gate", hmix, w_all[:, c_qm:], NN, BF16, epi=lambda acc, b: acc + b, rows=[sm["b_gate"]], plan=plan)

    o_sb, sb_tot, sb_nblk = _sb_fwd(qkv_sb)

    def dsa_prep(qkv, cs, sn, gq, gk, bd):
        mean = _mean_heads(bd)
        qn = _rope_fwd(_rms_fwd(qkv[:, :DSA_W].astype(F32), gq, mean), cs, sn)
        kn = _rope_fwd(_rms_fwd(qkv[:, DSA_W:2 * DSA_W].astype(F32), gk, mean), cs, sn)
        v = qkv[:, 2 * DSA_W:]
        outs = []
        for t in (qn, kn, v):
            outs += [t[:, DSA_OUT_W * g:DSA_OUT_W * (g + 1)] for g in range(3)]
        return outs

    dsa_in = _tokmap("dsa_prep", dsa_prep, [qkv_dsa, cos, sin], [gq_dsa, gk_dsa, bd768], [(DSA_OUT_W, BF16)] * 9, tile=256)
    dsa_q, dsa_k, dsa_v = dsa_in[0:3], dsa_in[3:6], dsa_in[6:9]
    dsa_o, dsa_lse = zip(*[_dsa_fwd(dsa_q[g], dsa_k[g], dsa_v[g], DSA_DILS[g]) for g in range(3)])

    def alphas(l0, l1, l2):
        m = jnp.maximum(jnp.maximum(l0, l1), l2)
        e = [jnp.exp(l - m) for l in (l0, l1, l2)]
        tot = e[0] + e[1] + e[2]
        return [t / tot for t in e]

    def dsa_mix(o0, o1, o2, l0, l1, l2):
        a = alphas(l0, l1, l2)
        return a[0] * o0 + a[1] * o1 + a[2] * o2

    o_dsa = _tokmap("dsa_mix", dsa_mix, [*dsa_o, *dsa_lse], [], [(DSA_OUT_W, BF16)])[0]

    def mem_kv(memv, g, wkv, gk, bd):
        kv = _dot(_rms_fwd(memv, g, _mean_all).astype(BF16), wkv)
        return _rms_fwd(kv[:, :MEM_W], gk, _mean_heads(bd)), kv[:, MEM_W:]

    km, vm = _tokmap("mem_kv", mem_kv, [mem], [sm["mem_norm"], w["w_mem_kv"], gk_mem, bd256], [(MEM_W, BF16)] * 2)

    def mem_probs(qv, kmv, gq, bd):
        qn = _rms_fwd(qv.astype(F32), gq, _mean_heads(bd)).astype(BF16)
        ps = []
        for h in range(MEM_W // HD):
            sl = slice(HD * h, HD * h + HD)
            sc = _dot(qn[:, sl], kmv[:, sl], NT) * SCALE
            e = jnp.exp(sc - jnp.max(sc, axis=1, keepdims=True))
            ps.append(e / jnp.sum(e, axis=1, keepdims=True))
        return qn, ps

    def mem_attn(qv, kmv, vmv, gq, bd):
        _, ps = mem_probs(qv, kmv, gq, bd)
        return jnp.concatenate([_dot(p.astype(BF16), vmv[:, HD * h:HD * h + HD]) for h, p in enumerate(ps)], axis=1)

    o_mem = _tokmap("mem_attn", mem_attn, [q_mem], [km, vm, gq_mem, bd256], [(MEM_W, BF16)])[0]

    def merge(osb, odsa, omem, gp, w_sb, w_dsa, w_mem):
        gates = jax.nn.sigmoid(gp.astype(F32))
        ys = (_dot(osb, w_sb), _dot(odsa, w_dsa), _dot(omem, w_mem))
        return gates, ys, gates[:, :D] * ys[0] + gates[:, D:2 * D] * ys[1] + gates[:, 2 * D:] * ys[2]

    merged = _tokmap("merge", lambda *a: merge(*a)[2], [o_sb, o_dsa, o_mem, gpre], [wb_sb, wb_dsa, wb_mem], [(D, BF16)],
                     tile=256)[0]
    x2 = _matmul("out_proj", merged, w["w_out"], NN, F32, epi=lambda acc, res: res + acc, tiles=[x1])
    w13_2 = jnp.concatenate([w["ffn2_w1"], w["ffn2_w3"]], axis=1)
    y, ffn2_saved = _ffn_fwd("ffn2", x2, sm["ffn2_norm"], w13_2, lambda: w["ffn2_w2"])

    def loss_fn(yv, tv):
        e = yv - tv
        part = 0.5 * jnp.sum(jnp.mean(e * e, axis=1, keepdims=True), axis=0, keepdims=True)
        return e * (1.0 / D), jnp.broadcast_to(part, (1, 128))

    dy, loss = _tokmap("loss", loss_fn, [y, tgt], [], [(D, F32)], [(1, 128)])

    gw, gs = {}, {}
    def ffn_grads(tag):
        def on_dw(dw13, dw2):
            gw[f"{tag}_w1"], gw[f"{tag}_w3"], gw[f"{tag}_w2"] = dw13[:, :D_FF], dw13[:, D_FF:], dw2
            on_grads(tag, {n: gw[n] for n in (f"{tag}_w1", f"{tag}_w3", f"{tag}_w2")})
        return on_dw

    dx2, gs["ffn2_norm"], _, _ = _ffn_bwd("ffn2", x2, sm["ffn2_norm"], w13_2, w["ffn2_w2"], ffn2_saved, dy, plan,
                                          ffn_grads("ffn2"))
    dmerged = _matmul("out_proj_bwd_dx", dx2, w["w_out"], NT, BF16)
    gw["w_out"] = _matmul("out_proj_bwd_dw", merged, dx2, TN, F32)

    def merge_bwd(osb, odsa, omem, gp, dm, w_sb, w_dsa, w_mem):
        gates, ys, _ = merge(osb, odsa, omem, gp, w_sb, w_dsa, w_mem)
        dmf = dm.astype(F32)
        dgp, dos, dws = [], [], []
        for b, (ov, wv) in enumerate(((osb, w_sb), (odsa, w_dsa), (omem, w_mem))):
            gb = gates[:, D * b:D * (b + 1)]
            dgp.append(dmf * ys[b] * gb * (1.0 - gb))
            dyb = (dmf * gb).astype(BF16)
            dos.append(_dot(dyb, wv, NT))
            dws.append(_dot(ov, dyb, TN))
        dgp = jnp.concatenate(dgp, axis=1)
        return dos[0], dos[1], dos[2], dgp, dws[0], dws[1], dws[2], jnp.sum(dgp, axis=0, keepdims=True)

    do_sb, do_dsa, do_mem, dgpre, gw["w_branch_sb"], gw["w_branch_dsa"], gw["w_branch_mem"], gs["b_gate"] = _tokmap(
        "merge_bwd", merge_bwd, [o_sb, o_dsa, o_mem, gpre, dmerged], [wb_sb, wb_dsa, wb_mem],
        [(SB_W, BF16), (DSA_OUT_W, F32), (MEM_W, BF16), (3 * D, BF16)],
        [(SB_W, D), (DSA_OUT_W, D), (MEM_W, D), (1, 3 * D)], tile=256)

    dq_sb, dk_sb, dv_sb = _sb_bwd(qkv_sb, do_sb, sb_tot, sb_nblk)

    def dsa_mix_bwd(o0, o1, o2, l0, l1, l2, dov, bd):
        a = alphas(l0, l1, l2)
        omix = a[0] * o0 + a[1] * o1 + a[2] * o2
        dot_o = _head_sums(dov * omix, bd)
        return [dov * t for t in a] + [-t * dot_o for t in a]

    mixb = _tokmap("dsa_mix_bwd", dsa_mix_bwd, [*dsa_o, *dsa_lse, do_dsa], [bd256],
                   [(DSA_OUT_W, BF16)] * 3 + [(DSA_OUT_W, F32)] * 3)
    dsa_d = [_dsa_bwd(dsa_q[g], dsa_k[g], dsa_v[g], mixb[g], mixb[3 + g], dsa_lse[g], DSA_DILS[g]) for g in range(3)]

    def dsa_prep_bwd(qkv, cs, sn, *rest):
        dqs, dks, dvs, (gq, gk, bd) = rest[0:3], rest[3:6], rest[6:9], rest[9:]
        mean = _mean_heads(bd)
        dq, dgq = _rms_bwd(qkv[:, :DSA_W].astype(F32), gq, _rope_bwd(jnp.concatenate(dqs, axis=1), cs, sn), mean)
        dk, dgk = _rms_bwd(qkv[:, DSA_W:2 * DSA_W].astype(F32), gk, _rope_bwd(jnp.concatenate(dks, axis=1), cs, sn), mean)
        return jnp.concatenate([dq, dk] + list(dvs), axis=1), dgq, dgk

    dqkv_dsa, dgq_dsa, dgk_dsa = _tokmap(
        "dsa_prep_bwd", dsa_prep_bwd,
        [qkv_dsa, cos, sin] + [dsa_d[g][t] for t in range(3) for g in range(3)], [gq_dsa, gk_dsa, bd768],
        [(3 * DSA_W, BF16)], [(1, DSA_W), (1, DSA_W)], tile=256)
    gs["qn_dsa"] = dgq_dsa.reshape(DSA_W // HD, HD).sum(axis=0, keepdims=True)
    gs["kn_dsa"] = dgk_dsa.reshape(DSA_W // HD, HD).sum(axis=0, keepdims=True)

    def mem_attn_bwd(qv, dov, kmv, vmv, gq, bd):
        qn, ps = mem_probs(qv, kmv, gq, bd)
        dqn, dkm, dvm = [], [], []
        for h, p in enumerate(ps):
            sl = slice(HD * h, HD * h + HD)
            dp = _dot(dov[:, sl], vmv[:, sl], NT)
            ds = (p * (dp - jnp.sum(p * dp, axis=1, keepdims=True)) * SCALE).astype(BF16)
            dqn.append(_dot(ds, kmv[:, sl]))
            dkm.append(_dot(ds, qn[:, sl], TN))
            dvm.append(_dot(p.astype(BF16), dov[:, sl], TN))
        dq, dgq = _rms_bwd(qv.astype(F32), gq, jnp.concatenate(dqn, axis=1), _mean_heads(bd))
        return dq, jnp.concatenate(dkm, axis=1), jnp.concatenate(dvm, axis=1), dgq

    dq_mem, dkm, dvm, dgq_mem = _tokmap("mem_attn_bwd", mem_attn_bwd, [q_mem, do_mem], [km, vm, gq_mem, bd256],
                                        [(MEM_W, BF16)], [(MEM_LEN, MEM_W), (MEM_LEN, MEM_W), (1, MEM_W)])
    gs["qn_mem"] = dgq_mem.reshape(MEM_W // HD, HD).sum(axis=0, keepdims=True)

    def mem_kv_bwd(memv, dkmv, dvmv, g, wkv, gk, bd):
        memn = _rms_fwd(memv, g, _mean_all).astype(BF16)
        kv = _dot(memn, wkv)
        dk, dgk = _rms_bwd(kv[:, :MEM_W], gk, dkmv, _mean_heads(bd))
        dkv = jnp.concatenate([dk, dvmv], axis=1).astype(BF16)
        _, dg = _rms_bwd(memv, g, _dot(dkv, wkv, NT), _mean_all)
        return _dot(memn, dkv, TN), dg, dgk

    gw["w_mem_kv"], gs["mem_norm"], dgk_mem = _tokmap(
        "mem_kv_bwd", mem_kv_bwd, [mem, dkm, dvm], [sm["mem_norm"], w["w_mem_kv"], gk_mem, bd256], [],
        [(D, 2 * MEM_W), (1, D), (1, MEM_W)])
    gs["kn_mem"] = dgk_mem.reshape(MEM_W // HD, HD).sum(axis=0, keepdims=True)

    dall = jnp.concatenate([dq_sb.astype(BF16), dk_sb.astype(BF16), dv_sb.astype(BF16), dqkv_dsa, dq_mem, dgpre], axis=1)
    dhmix = _matmul("proj_bwd_dx", dall, w_all, NT, F32)
    dw_all = _matmul("proj_bwd_dw", hmix, dall, TN, F32)
    gw["w_in"], gw["w_gate"] = dw_all[:, :c_qm], dw_all[:, c_qm:]
    on_grads("mid", {n: gw[n] for n in GROUPS["mid"]})

    def mix_norm_bwd(xv, dnv, dyv, g):
        dx, dg = _rms_bwd(xv, g, dnv, _mean_all)
        return dx + dyv, dg

    dx1, gs["mix_norm"] = _tokmap("mix_norm_bwd", mix_norm_bwd, [x1, dhmix, dx2], [sm["mix_norm"]], [(D, F32)], [(1, D)])
    gx, gs["ffn1_norm"], _, _ = _ffn_bwd("ffn1", x, sm["ffn1_norm"], w13_1, w["ffn1_w2"], ffn1_saved, dx1, plan,
                                         ffn_grads("ffn1"))
    return loss, gx, gw, gs


def _shard_shape(shape, axis):
    return (shape[0] // N_CHIPS, shape[1]) if axis == 0 else (shape[0], shape[1] // N_CHIPS)


def _pack(shards, names):
    return jnp.concatenate([shards[n].reshape(-1, D) for n in names], axis=0)


def _unpack(pack, names):
    out, r = {}, 0
    for n in names:
        ss = _shard_shape(*SHARDED_BY_NAME[n])
        rows = ss[0] * ss[1] // D
        out[n] = pack[r:r + rows].reshape(ss)
        r += rows
    return out


def _full_from_packs(packs, names):
    per_chip = [_unpack(packs[c], names) for c in range(N_CHIPS)]
    return {n: jnp.concatenate([per_chip[c][n] for c in range(N_CHIPS)], axis=SHARDED_BY_NAME[n][1]) for n in names}


def _packs_from_full(full, names, dtype):
    packs = []
    for c in range(N_CHIPS):
        shards = {}
        for n in names:
            sh, ax = SHARDED_BY_NAME[n]
            ss = _shard_shape(sh, ax)
            shards[n] = lax.slice_in_dim(full[n], c * ss[ax], (c + 1) * ss[ax], axis=ax).astype(dtype)
        packs.append(_pack(shards, names))
    return jnp.stack(packs)


def _own_pack(full, names, chip):
    shards = {}
    for n in names:
        sh, ax = SHARDED_BY_NAME[n]
        ss = _shard_shape(sh, ax)
        shards[n] = lax.dynamic_slice_in_dim(full[n], chip * ss[ax], ss[ax], axis=ax)
    return _pack(shards, names)


SMALL_USED = sum(n for _, n in SMALL)


def _pack_small(d, loss=None):
    parts = [d[n].reshape(-1) for n, _ in SMALL]
    parts.append(jnp.zeros((1,), F32) if loss is None else loss.reshape(1))
    parts.append(jnp.zeros((SMALL_ROWS * D - SMALL_USED - 1,), F32))
    return jnp.concatenate(parts).reshape(SMALL_ROWS, D)


def _unpack_small(v):
    flat, out, r = v.reshape(-1), {}, 0
    for n, k in SMALL:
        out[n] = flat[r:r + k]
        r += k
    return out, flat[r]


def _place():
    return lax.axis_index("x"), lax.axis_index("y"), lax.axis_index("c")


def _other_chips(x, y):
    return [(1 - x, y), (x, 1 - y), (1 - x, 1 - y)]


HBM_SPEC = pl.BlockSpec(memory_space=pl.ANY)


CHIP_SEMS = (pltpu.SemaphoreType.DMA((3,)), pltpu.SemaphoreType.DMA((3,)), pltpu.SemaphoreType.DMA)


def _gather_copies(ins, outs, send_sems, recv_sems, local_sem):
    (src,), (out,) = ins, outs
    x, y, c = _place()
    me = 2 * x + y
    copies = [pltpu.make_async_copy(src, out.at[me], local_sem)]
    copies += [pltpu.make_async_remote_copy(src_ref=src, dst_ref=out.at[me], send_sem=send_sems.at[k], recv_sem=recv_sems.at[k],
                                            device_id=(px, py, c), device_id_type=MESH)
               for k, (px, py) in enumerate(_other_chips(x, y))]
    return copies


def _scatter_copies(ins, outs, send_sems, recv_sems, local_sem):
    (src,), (out,) = ins, outs
    x, y, c = _place()
    return [pltpu.make_async_remote_copy(src_ref=src.at[2 * px + py], dst_ref=out.at[k], send_sem=send_sems.at[k],
                                         recv_sem=recv_sems.at[k], device_id=(px, py, c), device_id_type=MESH)
            for k, (px, py) in enumerate(_other_chips(x, y))]


def _all_gather_chips(pack):
    def body(src, out, *sems):
        copies = _gather_copies((src,), (out,), *sems)
        for cp in copies:
            cp.start()
        for cp in copies:
            cp.wait()

    return pl.pallas_call(
        body, name="weights_all_gather", in_specs=[HBM_SPEC], out_specs=HBM_SPEC,
        out_shape=SDS((N_CHIPS,) + pack.shape, pack.dtype), scratch_shapes=list(CHIP_SEMS),
    )(pack)


def _swap_with_sibling(name, v):
    def body(src, out, send_sem, recv_sem):
        x, y, c = _place()
        cp = pltpu.make_async_remote_copy(src_ref=src, dst_ref=out, send_sem=send_sem, recv_sem=recv_sem,
                                          device_id=(x, y, 1 - c), device_id_type=MESH)
        cp.start()
        cp.wait()

    return pl.pallas_call(
        body, name=name, in_specs=[HBM_SPEC], out_specs=HBM_SPEC, out_shape=SDS(v.shape, v.dtype),
        scratch_shapes=[pltpu.SemaphoreType.DMA, pltpu.SemaphoreType.DMA],
    )(v)


def _all_reduce_small(v):
    n_dev = 8

    def body(v_ref, out_ref, land, send_sems, recv_sems):
        x, y, c = _place()
        me = 4 * x + 2 * y + c
        land[me] = v_ref[...]
        copies = []
        for k in range(1, n_dev):
            peer = (x ^ (k >> 2), y ^ ((k >> 1) & 1), c ^ (k & 1))
            copies.append(pltpu.make_async_remote_copy(src_ref=v_ref, dst_ref=land.at[me], send_sem=send_sems.at[k - 1],
                                                       recv_sem=recv_sems.at[k - 1], device_id=peer, device_id_type=MESH))
        for cp in copies:
            cp.start()
        for cp in copies:
            cp.wait()
        acc = land[0]
        for d in range(1, n_dev):
            acc = acc + land[d]
        out_ref[...] = acc

    return pl.pallas_call(
        body, name="small_all_reduce", in_specs=[pl.BlockSpec(memory_space=pltpu.VMEM)],
        out_specs=pl.BlockSpec(memory_space=pltpu.VMEM), out_shape=SDS(v.shape, v.dtype),
        scratch_shapes=[pltpu.VMEM((n_dev,) + v.shape, v.dtype), pltpu.SemaphoreType.DMA((n_dev - 1,)),
                        pltpu.SemaphoreType.DMA((n_dev - 1,))],
    )(v)


def _adamw(g, wv, m, v):
    m = ADAM_B1 * m + (1.0 - ADAM_B1) * g
    v = ADAM_B2 * v + (1.0 - ADAM_B2) * (g * g)
    m_hat = m / (1.0 - ADAM_B1 ** ADAM_STEP)
    v_hat = v / (1.0 - ADAM_B2 ** ADAM_STEP)
    delta = -ADAM_LR * (m_hat / (jnp.sqrt(v_hat) + ADAM_EPS) + ADAM_WD * wv)
    return delta, m, v


def kernel(x, mem, ffn1_norm, ffn1_w1, ffn1_w3, ffn1_w2, mix_norm, mem_norm, w_in, w_mem_kv, qn_dsa, kn_dsa, qn_mem, kn_mem, w_branch_sb, w_branch_dsa, w_branch_mem, w_gate, b_gate, w_out, ffn2_norm, ffn2_w1, ffn2_w3, ffn2_w2, loss_target, m_ffn1_norm, m_ffn1_w1, m_ffn1_w3, m_ffn1_w2, m_mix_norm, m_mem_norm, m_w_in, m_w_mem_kv, m_qn_dsa, m_kn_dsa, m_qn_mem, m_kn_mem, m_w_branch_sb, m_w_branch_dsa, m_w_branch_mem, m_w_gate, m_b_gate, m_w_out, m_ffn2_norm, m_ffn2_w1, m_ffn2_w3, m_ffn2_w2, v_ffn1_norm, v_ffn1_w1, v_ffn1_w3, v_ffn1_w2, v_mix_norm, v_mem_norm, v_w_in, v_w_mem_kv, v_qn_dsa, v_kn_dsa, v_qn_mem, v_kn_mem, v_w_branch_sb, v_w_branch_dsa, v_w_branch_mem, v_w_gate, v_b_gate, v_w_out, v_ffn2_norm, v_ffn2_w1, v_ffn2_w3, v_ffn2_w2):
    given = dict(locals())
    wts = {n: given[n][0] for n in WEIGHTS}
    moms = {n: given["m_" + n][0] for n in WEIGHTS}
    vars_ = {n: given["v_" + n][0] for n in WEIGHTS}

    plan = _Plan()
    x_i, y_i, _ = _place()
    my_chip = 2 * x_i + y_i

    full = {}
    for host, names in WEIGHT_PIECES:
        pack = _pack(wts, names).astype(BF16)
        if host is None:
            full.update(_full_from_packs(_all_gather_chips(pack), names))
        else:
            plan.put(host, _Carry([pack], [SDS((N_CHIPS,) + pack.shape, BF16)], CHIP_SEMS, _gather_copies,
                                  lambda res, names=names: full.update(_full_from_packs(res[0], names))))
    small = {n: wts[n].reshape(1, -1) for n, _ in SMALL}

    landed = {}

    def on_grads(group, grads):
        names = GROUPS[group]
        packs = _packs_from_full(grads, names, BF16)
        own = _own_pack(grads, names, my_chip)
        plan.put(GRAD_HOSTS[group], _Carry([packs], [SDS((3,) + packs.shape[1:], BF16)], CHIP_SEMS, _scatter_copies,
                                           lambda res: landed.update({group: (own, res[0])})))

    loss, gx, _, gs = _local_step(x[0], mem[0], loss_target[0], full, small, plan, on_grads)
    assert not plan.pending, list(plan.pending)

    def update(hv, ov, wv, mv, vv):
        g = hv + ov
        return (g,) + _adamw(g, wv, mv, vv)

    outs = [{}, {}, {}, {}]
    for group, names in GROUPS.items():
        own, got = landed[group]
        half = _tokmap(f"grads_sum_chips_{group}",
                       lambda a, b0, b1, b2: ((a + b0.astype(F32)) + b1.astype(F32)) + b2.astype(F32),
                       [own, got[0], got[1], got[2]], [], [(D, F32)])[0]
        other = _swap_with_sibling(f"grads_swap_cores_{group}", half)
        res = _tokmap(f"adamw_{group}", update, [half, other, _pack(wts, names), _pack(moms, names), _pack(vars_, names)],
                      [], [(D, F32)] * 4)
        for d, packed in zip(outs, res):
            d.update(_unpack(packed, names))

    s_red = _all_reduce_small(_pack_small(gs, loss[0, 0]))
    res = _tokmap(
        "adamw_small", lambda g, wv, mv, vv: (g,) + _adamw(g, wv, mv, vv),
        [s_red, _pack_small(small), _pack_small({n: moms[n] for n, _ in SMALL}), _pack_small({n: vars_[n] for n, _ in SMALL})],
        [], [(D, F32)] * 4)
    for d, packed in zip(outs, res):
        d.update(_unpack_small(packed)[0])
    _, total_loss = _unpack_small(s_red)
    return (total_loss, gx[None], *[d[n][None] for d in outs for n in WEIGHTS])
```

```python
import functools

import numpy as np
import jax
import jax.numpy as jnp
from jax import lax
from jax.experimental import pallas as pl
from jax.experimental.pallas import tpu as pltpu

F32, BF16 = jnp.float32, jnp.bfloat16
SDS = jax.ShapeDtypeStruct
MESH = pl.DeviceIdType.MESH

D = 1024
HD = 64
QB = 128
D_FF = 2816
SB_W, DSA_W, DSA_OUT_W, MEM_W = 512, 768, 256, 256
DSA_DILS = (1, 4, 16)
MEM_LEN = 256
N_CHIPS = 4
EPS = 1e-6
SCALE = HD ** -0.5
EXHAUSTED = -104.0
SB_WIN = 384
NEG = -1e30
VMEM_LIMIT = 56 * 1024 * 1024

ADAM_LR, ADAM_B1, ADAM_B2, ADAM_EPS, ADAM_WD, ADAM_STEP = 0.001, 0.9, 0.999, 1e-08, 0.01, 10

NN = (((1,), (0,)), ((), ()))
NT = (((1,), (1,)), ((), ()))
TN = (((0,), (0,)), ((), ()))

SHARDED = (
    ("ffn1_w1", (D, D_FF), 1), ("ffn1_w3", (D, D_FF), 1), ("ffn1_w2", (D_FF, D), 0),
    ("w_in", (D, 4096), 1), ("w_mem_kv", (D, 512), 0),
    ("w_branch_sb", (SB_W, D), 1), ("w_branch_dsa", (DSA_OUT_W, D), 1), ("w_branch_mem", (MEM_W, D), 1),
    ("w_gate", (D, 3 * D), 1), ("w_out", (D, D), 0),
    ("ffn2_w1", (D, D_FF), 1), ("ffn2_w3", (D, D_FF), 1), ("ffn2_w2", (D_FF, D), 0),
)
SHARDED_BY_NAME = {n: (sh, ax) for n, sh, ax in SHARDED}
GROUPS = {
    "ffn2": ("ffn2_w1", "ffn2_w3", "ffn2_w2"),
    "mid": ("w_in", "w_mem_kv", "w_branch_sb", "w_branch_dsa", "w_branch_mem", "w_gate", "w_out"),
    "ffn1": ("ffn1_w1", "ffn1_w3", "ffn1_w2"),
}
WEIGHT_PIECES = (
    (None, ("ffn1_w1", "ffn1_w3")),
    ("ffn1_up", ("ffn1_w2", "w_in")),
    ("ffn1_down", ("w_gate", "w_mem_kv", "w_branch_sb", "w_branch_dsa", "w_branch_mem", "w_out")),
    ("proj_dsa", ("ffn2_w2",)),
    ("proj_gate", ("ffn2_w1", "ffn2_w3")),
)
GRAD_HOSTS = {"ffn2": "ffn2_bwd_dn", "mid": "ffn1_bwd_dw13", "ffn1": "ffn1_bwd_dn"}
SMALL = (("ffn1_norm", D), ("mix_norm", D), ("mem_norm", D), ("ffn2_norm", D), ("b_gate", 3 * D),
         ("qn_dsa", HD), ("kn_dsa", HD), ("qn_mem", HD), ("kn_mem", HD))
WEIGHTS = ("ffn1_norm", "ffn1_w1", "ffn1_w3", "ffn1_w2", "mix_norm", "mem_norm", "w_in", "w_mem_kv", "qn_dsa", "kn_dsa",
           "qn_mem", "kn_mem", "w_branch_sb", "w_branch_dsa", "w_branch_mem", "w_gate", "b_gate", "w_out", "ffn2_norm",
           "ffn2_w1", "ffn2_w3", "ffn2_w2")
SMALL_ROWS = 8


def _dot(a, b, dn=NN):
    return lax.dot_general(a, b, dn, preferred_element_type=F32)


def _dot01(x, m01):
    hi = x.astype(BF16)
    r1 = x - hi.astype(F32)
    mid = r1.astype(BF16)
    lo = (r1 - mid.astype(F32)).astype(BF16)
    return _dot(hi, m01) + _dot(mid, m01) + _dot(lo, m01)


def _pick(n, cands):
    for c in cands:
        if n % c == 0:
            return c
    raise ValueError(f"no tile for {n}")


def _from_dilated(v, d, scr):
    w = v.shape[1] // d
    v = v.astype(F32)
    for c in range(d):
        for p, buf in enumerate(scr[:w // 128]):
            buf[pl.ds(c, v.shape[0], stride=d), :] = v[:, c * w + 128 * p:c * w + 128 * (p + 1)]
    return jnp.concatenate([buf[...] for buf in scr[:w // 128]], axis=1)


def _to_dilated(v, d, scr):
    w = v.shape[1]
    for p, buf in enumerate(scr[:w // 128]):
        buf[...] = v[:, 128 * p:128 * (p + 1)].astype(F32)
    return jnp.concatenate([buf[pl.ds(c, v.shape[0] // d, stride=d), :] for c in range(d) for buf in scr[:w // 128]], axis=1)


def _tokmap(name, fn, tok_ins, consts, tok_outs, acc_outs=(), tile=512, dil_ins=None, dil_outs=None):
    dil_ins, dil_outs = dil_ins or {}, dil_outs or {}
    n = tok_ins[0].shape[0] * dil_ins.get(0, 1)
    tile = _pick(n, [t for t in (512, 256, 128, 64, 32, 16, 8) if t <= tile])
    n_tin, n_in, n_tok, n_acc = len(tok_ins), len(tok_ins) + len(consts), len(tok_outs), len(acc_outs)
    n_scr = max([tok_ins[j].shape[1] // d // 128 for j, d in dil_ins.items() if d > 1]
                + [tok_outs[j][0] // 128 for j, d in dil_outs.items() if d > 1] + [0])

    def body(*refs):
        scr = refs[len(refs) - n_scr:]
        vals = [r[...] for r in refs[:n_in]]
        for j, d in dil_ins.items():
            if d > 1:
                vals[j] = _from_dilated(vals[j], d, scr)
        outs = fn(*vals)
        outs = list(outs) if isinstance(outs, (tuple, list)) else [outs]
        assert len(outs) == n_tok + n_acc, (name, len(outs))
        for j, d in dil_outs.items():
            if d > 1:
                outs[j] = _to_dilated(outs[j], d, scr)
        orefs = refs[n_in:]
        for r, v in zip(orefs[:n_tok], outs[:n_tok]):
            r[...] = v.astype(r.dtype)
        if n_acc:
            @pl.when(pl.program_id(0) == 0)
            def _():
                for r in orefs[n_tok:n_tok + n_acc]:
                    r[...] = jnp.zeros(r.shape, r.dtype)
            for r, v in zip(orefs[n_tok:n_tok + n_acc], outs[n_tok:]):
                r[...] += v.astype(r.dtype)

    def tok_spec(width, d):
        return pl.BlockSpec((tile // d, d * width), lambda i: (i, 0))

    in_specs = [tok_spec(a.shape[1] // dil_ins.get(j, 1), dil_ins.get(j, 1)) for j, a in enumerate(tok_ins)]
    in_specs += [pl.BlockSpec(c.shape, lambda i: (0, 0)) for c in consts]
    out_specs = [tok_spec(w, dil_outs.get(j, 1)) for j, (w, _) in enumerate(tok_outs)]
    out_specs += [pl.BlockSpec(s, lambda i: (0, 0)) for s in acc_outs]
    out_shape = [SDS((n // dil_outs.get(j, 1), w * dil_outs.get(j, 1)), dt) for j, (w, dt) in enumerate(tok_outs)]
    out_shape += [SDS(s, F32) for s in acc_outs]
    res = pl.pallas_call(
        body, name=name, grid=(n // tile,), in_specs=in_specs, out_specs=out_specs, out_shape=out_shape,
        scratch_shapes=[pltpu.VMEM((tile, 128), F32)] * n_scr,
        compiler_params=pltpu.CompilerParams(dimension_semantics=("arbitrary",), vmem_limit_bytes=VMEM_LIMIT),
    )(*tok_ins, *consts)
    return res


MATMUL_VMEM_BUDGET = 40 * 1024 * 1024


def _matmul_tiles(m, n, k, a_bytes, b_bytes, o_bytes, extra_bytes):
    best = None
    for tk in [c for c in (3584, 2816, 2048, 1408, 1024, 512, 256, 128) if k % c == 0]:
        for tm in [c for c in (1408, 1024, 768, 512, 256, 128) if m % c == 0]:
            for tn in [c for c in (1408, 1024, 768, 512, 256, 128) if n % c == 0]:
                need = 2 * tk * (tm * a_bytes + tn * b_bytes) + tm * tn * (2 * o_bytes + 2 * extra_bytes + 8)
                if need > MATMUL_VMEM_BUDGET:
                    continue
                score = (min(tm, 512) * min(tn, 512), tk, tm * tn)
                if best is None or score > best[0]:
                    best = (score, (tm, tn, tk))
    return best[1]


class _Carry:
    def __init__(self, ins, outs, sems, copies, then):
        self.ins, self.outs, self.sems, self.copies, self.then = ins, outs, sems, copies, then


class _Plan:
    def __init__(self):
        self.pending = {}

    def put(self, host, carry):
        assert host not in self.pending, host
        self.pending[host] = carry

    def take(self, host):
        return self.pending.pop(host, None)


def _matmul(name, a, b, dn, out_dtype, epi=None, tiles=(), rows=(), plan=None):
    if dn == NN:
        (m, k), n = a.shape, b.shape[1]
    elif dn == NT:
        (m, k), n = a.shape, b.shape[0]
    else:
        (k, m), n = a.shape, b.shape[1]
    n_t, n_r = len(tiles), len(rows)
    tm, tn, tk = _matmul_tiles(m, n, k, a.dtype.itemsize, b.dtype.itemsize, jnp.dtype(out_dtype).itemsize,
                               sum(t.dtype.itemsize for t in tiles))
    nk = k // tk
    grid = (m // tm, n // tn, nk)
    carry = plan.take(name) if plan is not None else None
    n_ci, n_co = (len(carry.ins), len(carry.outs)) if carry else (0, 0)

    def body(a_ref, b_ref, *rest):
        extras, rest = rest[:n_t + n_r], rest[n_t + n_r:]
        c_in, o_ref, c_out, scratch = rest[:n_ci], rest[n_ci], rest[n_ci + 1:n_ci + 1 + n_co], rest[n_ci + 1 + n_co:]
        ids = [pl.program_id(d) for d in range(3)]
        if carry:
            sems = scratch[1:] if nk > 1 else scratch

            @pl.when((ids[0] == 0) & (ids[1] == 0) & (ids[2] == 0))
            def _():
                for cp in carry.copies(c_in, c_out, *sems):
                    cp.start()

        part = _dot(a_ref[...].astype(BF16), b_ref[...].astype(BF16), dn)

        def finish(r):
            if epi is not None:
                r = epi(r, *[e[...] for e in extras])
            o_ref[...] = r.astype(o_ref.dtype)

        if nk == 1:
            finish(part)
        else:
            acc = scratch[0]

            @pl.when(ids[2] == 0)
            def _():
                acc[...] = part

            @pl.when(ids[2] > 0)
            def _():
                acc[...] += part

            @pl.when(ids[2] == nk - 1)
            def _():
                finish(acc[...])

        if carry:
            @pl.when((ids[0] == grid[0] - 1) & (ids[1] == grid[1] - 1) & (ids[2] == nk - 1))
            def _():
                for cp in carry.copies(c_in, c_out, *sems):
                    cp.wait()

    a_spec = pl.BlockSpec((tk, tm), lambda i, j, kk: (kk, i)) if dn == TN else pl.BlockSpec((tm, tk), lambda i, j, kk: (i, kk))
    b_spec = pl.BlockSpec((tn, tk), lambda i, j, kk: (j, kk)) if dn == NT else pl.BlockSpec((tk, tn), lambda i, j, kk: (kk, j))
    in_specs = [a_spec, b_spec] + [pl.BlockSpec((tm, tn), lambda i, j, kk: (i, j)) for _ in tiles]
    in_specs += [pl.BlockSpec((1, tn), lambda i, j, kk: (0, j)) for _ in rows] + [HBM_SPEC] * n_ci
    res = pl.pallas_call(
        body, name=name, grid=grid, in_specs=in_specs,
        out_specs=[pl.BlockSpec((tm, tn), lambda i, j, kk: (i, j))] + [HBM_SPEC] * n_co,
        out_shape=[SDS((m, n), out_dtype)] + (list(carry.outs) if carry else []),
        scratch_shapes=([pltpu.VMEM((tm, tn), F32)] if nk > 1 else []) + (list(carry.sems) if carry else []),
        compiler_params=pltpu.CompilerParams(
            dimension_semantics=("arbitrary",) * 3 if carry else ("parallel", "parallel", "arbitrary"),
            vmem_limit_bytes=VMEM_LIMIT),
    )(a, b, *tiles, *rows, *(carry.ins if carry else []))
    if carry:
        carry.then(res[1:])
    return res[0]


def _mean_all(v):
    return jnp.mean(v, axis=-1, keepdims=True)


def _head_sums(v, bd):
    w = bd.shape[0]
    return jnp.concatenate([_dot01(v[:, j:j + w], bd) for j in range(0, v.shape[1], w)], axis=1)


def _mean_heads(bd):
    return lambda v: _head_sums(v, bd) * (1.0 / HD)


def _rms_fwd(x, g, mean):
    return x * lax.rsqrt(mean(x * x) + EPS) * g


def _rms_bwd(x, g, dy, mean):
    r = lax.rsqrt(mean(x * x) + EPS)
    dn = dy * g
    dx = r * dn - x * (r * r * r) * mean(dn * x)
    return dx, jnp.sum(dy * x * r, axis=0, keepdims=True)


def _swap_halves(x):
    w = x.shape[1]
    lane = lax.broadcasted_iota(jnp.int32, x.shape, 1)
    return jnp.where(lane % HD < HD // 2, pltpu.roll(x, w - HD // 2, 1), pltpu.roll(x, HD // 2, 1))


def _lanes(t, w):
    return jnp.tile(t, (1, w // t.shape[1]))


def _rope_fwd(x, cos, sin_signed):
    return x * _lanes(cos, x.shape[1]) + _swap_halves(x) * _lanes(sin_signed, x.shape[1])


def _rope_bwd(dy, cos, sin_signed):
    return dy * _lanes(cos, dy.shape[1]) + _swap_halves(dy * _lanes(sin_signed, dy.shape[1]))


def _bcast_heads(cols):
    return jnp.concatenate([jnp.broadcast_to(c, (c.shape[0], HD)) for c in cols], axis=1)


def _softplus(z):
    return jnp.maximum(z, 0.0) + jnp.log1p(jnp.exp(-jnp.abs(z)))


def _block_diag(w):
    h = np.arange(w) // HD
    return jnp.asarray(h[:, None] == h[None, :], BF16)


def _sb_window(i, t):
    hi = (i + 1) * QB - t * SB_WIN
    lo = hi - SB_WIN
    ws = pl.multiple_of(jnp.maximum(lo, 0), QB)
    kpos = ws + lax.broadcasted_iota(jnp.int32, (QB, SB_WIN), 1)
    qpos = i * QB + lax.broadcasted_iota(jnp.int32, (QB, SB_WIN), 0)
    return (kpos < qpos) & (kpos >= lo) & (kpos < hi), ws


def _sb_fwd(qkv):
    s = qkv.shape[0]
    assert s >= SB_WIN
    nq = s // QB
    npairs = SB_W // 128

    def body(q_ref, k_ref, v_ref, later_ref, o_ref, tot_ref, nb_ref):
        p, i = pl.program_id(0), pl.program_id(1)
        q = q_ref[...]
        later_of = later_ref[...]

        def step(c):
            t, _, tots, outs = c
            mask, ws = _sb_window(i, t)
            kw, vw = k_ref[pl.ds(ws, SB_WIN), :], v_ref[pl.ds(ws, SB_WIN), :]
            new_t, new_o = [], []
            for hh in range(2):
                sl = slice(HD * hh, HD * hh + HD)
                z = _dot(q[:, sl], kw[:, sl], NT) * SCALE
                sp = _softplus(z)
                lf = jnp.where(mask, -sp, 0.0)
                later = tots[hh] + _dot01(lf, later_of)
                w = jnp.where(mask, jnp.exp(z - sp + later), 0.0)
                new_o.append(outs[hh] + _dot(w.astype(BF16), vw[:, sl]))
                new_t.append(tots[hh] + jnp.sum(lf, axis=1, keepdims=True))
            alive = jnp.maximum(jnp.max(new_t[0]), jnp.max(new_t[1]))
            return t + 1, alive, tuple(new_t), tuple(new_o)

        zt, zo = jnp.zeros((QB, 1), F32), jnp.zeros((QB, HD), F32)
        t, _, tots, outs = lax.while_loop(lambda c: ((i + 1) * QB - c[0] * SB_WIN > 0) & (c[1] > EXHAUSTED), step,
                                          (jnp.int32(0), jnp.float32(0.0), (zt, zt), (zo, zo)))
        o_ref[...] = jnp.concatenate(outs, axis=1).astype(o_ref.dtype)
        tot_ref[...] = _bcast_heads(tots)
        nb_ref[p, i] = t

    whole = lambda off: pl.BlockSpec((s, 128), lambda p, i: (0, off + p), pipeline_mode=pl.Buffered(1))
    tile = pl.BlockSpec((QB, 128), lambda p, i: (i, p))
    tri = pl.BlockSpec((SB_WIN, SB_WIN), lambda p, i: (0, 0), pipeline_mode=pl.Buffered(1))
    idx = np.arange(SB_WIN)
    return pl.pallas_call(
        body, name="sb_fwd", grid=(npairs, nq),
        in_specs=[tile, whole(npairs), whole(2 * npairs), tri],
        out_specs=[tile, tile, pl.BlockSpec(memory_space=pltpu.SMEM)],
        out_shape=[SDS((s, SB_W), BF16), SDS((s, SB_W), F32), SDS((npairs, nq), jnp.int32)],
        compiler_params=pltpu.CompilerParams(dimension_semantics=("arbitrary", "arbitrary"), vmem_limit_bytes=VMEM_LIMIT),
    )(qkv, qkv, qkv, jnp.asarray(idx[:, None] > idx[None, :], BF16))


def _sb_bwd(qkv, do, tot, nblk):
    s = qkv.shape[0]
    nq = s // QB
    npairs = SB_W // 128

    def body(nb_ref, q_ref, k_ref, v_ref, do_ref, tot_ref, upto_ref, before_ref, dq_ref, dk_ref, dv_ref):
        p, i = pl.program_id(0), pl.program_id(1)

        @pl.when(i == 0)
        def _():
            dk_ref[...] = jnp.zeros(dk_ref.shape, F32)
            dv_ref[...] = jnp.zeros(dv_ref.shape, F32)

        upto = upto_ref[...]
        before = before_ref[...]
        q, dout, tt = q_ref[...], do_ref[...], tot_ref[...]
        n = nb_ref[p, i]

        def step(it, c):
            pres, gpres, dqs = c
            mask, ws = _sb_window(i, n - 1 - it)
            kw, vw = k_ref[pl.ds(ws, SB_WIN), :], v_ref[pl.ds(ws, SB_WIN), :]
            new_p, new_g, new_dq, dks, dvs = [], [], [], [], []
            for hh in range(2):
                sl = slice(HD * hh, HD * hh + HD)
                z = _dot(q[:, sl], kw[:, sl], NT) * SCALE
                sp = _softplus(z)
                lf = jnp.where(mask, -sp, 0.0)
                later = tt[:, HD * hh:HD * hh + 1] - (pres[hh] + _dot01(lf, upto))
                w = jnp.where(mask, jnp.exp(z - sp + later), 0.0)
                beta = jnp.exp(z - sp)
                g = _dot(dout[:, sl], vw[:, sl], NT) * w
                g_far = gpres[hh] + _dot(g.astype(BF16), before)
                dz = (jnp.where(mask, g * (1.0 - beta) - beta * g_far, 0.0) * SCALE).astype(BF16)
                new_dq.append(dqs[hh] + _dot(dz, kw[:, sl]))
                dks.append(_dot(dz, q[:, sl], TN))
                dvs.append(_dot(w.astype(BF16), dout[:, sl], TN))
                new_p.append(pres[hh] + jnp.sum(lf, axis=1, keepdims=True))
                new_g.append(gpres[hh] + jnp.sum(g, axis=1, keepdims=True))
            dk_ref[pl.ds(ws, SB_WIN), :] += jnp.concatenate(dks, axis=1)
            dv_ref[pl.ds(ws, SB_WIN), :] += jnp.concatenate(dvs, axis=1)
            return tuple(new_p), tuple(new_g), tuple(new_dq)

        zt, zo = jnp.zeros((QB, 1), F32), jnp.zeros((QB, HD), F32)
        _, _, dqs = lax.fori_loop(0, n, step, ((zt, zt), (zt, zt), (zo, zo)))
        dq_ref[...] = jnp.concatenate(dqs, axis=1)

    whole_in = lambda off: pl.BlockSpec((s, 128), lambda p, i: (0, off + p), pipeline_mode=pl.Buffered(1))
    whole_out = pl.BlockSpec((s, 128), lambda p, i: (0, p), pipeline_mode=pl.Buffered(1))
    tile = pl.BlockSpec((QB, 128), lambda p, i: (i, p))
    tri = pl.BlockSpec((SB_WIN, SB_WIN), lambda p, i: (0, 0), pipeline_mode=pl.Buffered(1))
    idx = np.arange(SB_WIN)
    return pl.pallas_call(
        body, name="sb_bwd", grid=(npairs, nq),
        in_specs=[pl.BlockSpec(memory_space=pltpu.SMEM), tile, whole_in(npairs), whole_in(2 * npairs), tile, tile, tri, tri],
        out_specs=[tile, whole_out, whole_out],
        out_shape=[SDS((s, SB_W), F32)] * 3,
        compiler_params=pltpu.CompilerParams(dimension_semantics=("arbitrary", "arbitrary"), vmem_limit_bytes=VMEM_LIMIT),
    )(nblk, qkv, qkv, qkv, do, tot, jnp.asarray(idx[:, None] <= idx[None, :], BF16), jnp.asarray(idx[:, None] < idx[None, :], BF16))


def _win_masks(has_prev):
    row = lax.broadcasted_iota(jnp.int32, (QB, QB), 0)
    col = lax.broadcasted_iota(jnp.int32, (QB, QB), 1)
    return col <= row, (col >= row) & has_prev


def _dsa_fwd(q, k, v, dil):
    n = q.shape[0]
    nb = n // QB

    def body(q_ref, kc_ref, kp_ref, vc_ref, vp_ref, o_ref, lse_ref):
        m_cur, m_prev = _win_masks(pl.program_id(1) > 0)
        outs, lses = [], []
        for hh in range(DSA_OUT_W // HD):
            sl = slice(HD * hh, HD * hh + HD)
            qh = q_ref[:, sl]
            sc = jnp.where(m_cur, _dot(qh, kc_ref[:, sl], NT) * SCALE, NEG)
            sp = jnp.where(m_prev, _dot(qh, kp_ref[:, sl], NT) * SCALE, NEG)
            m = jnp.maximum(jnp.max(sc, axis=1, keepdims=True), jnp.max(sp, axis=1, keepdims=True))
            pc, pp = jnp.exp(sc - m), jnp.exp(sp - m)
            den = jnp.sum(pc, axis=1, keepdims=True) + jnp.sum(pp, axis=1, keepdims=True)
            outs.append((_dot(pc.astype(BF16), vc_ref[:, sl]) + _dot(pp.astype(BF16), vp_ref[:, sl])) / den)
            lses.append(m + jnp.log(den))
        o_ref[...] = jnp.concatenate(outs, axis=1)
        lse_ref[...] = _bcast_heads(lses)

    cur = pl.BlockSpec((QB, DSA_OUT_W), lambda c, i: (i, c))
    prev = pl.BlockSpec((QB, DSA_OUT_W), lambda c, i: (jnp.maximum(i - 1, 0), c))
    o, lse = pl.pallas_call(
        body, name=f"dsa_fwd_d{dil}", grid=(dil, nb), in_specs=[cur, cur, prev, cur, prev], out_specs=[cur, cur],
        out_shape=[SDS((n, dil * DSA_OUT_W), F32)] * 2,
        compiler_params=pltpu.CompilerParams(dimension_semantics=("parallel", "parallel")),
    )(q, k, k, v, v)
    return o, lse


def _dsa_bwd(q, k, v, do, cc, lse, dil):
    n = q.shape[0]
    nb = n // QB

    def body(qj_ref, qn_ref, kp_ref, kj_ref, vp_ref, vj_ref, doj_ref, don_ref, cj_ref, cn_ref, lj_ref, ln_ref,
             dq_ref, dk_ref, dv_ref):
        j = pl.program_id(1)
        m_cur, m_prev = _win_masks(j > 0)
        _, m_next = _win_masks(j + 1 < nb)
        dqs, dks, dvs = [], [], []
        for hh in range(DSA_OUT_W // HD):
            sl = slice(HD * hh, HD * hh + HD)
            one = slice(HD * hh, HD * hh + 1)
            qj, qn, kp, kj, vp, vj = (r[:, sl] for r in (qj_ref, qn_ref, kp_ref, kj_ref, vp_ref, vj_ref))
            doj, don = doj_ref[:, sl], don_ref[:, sl]

            def dscore(qq, kk, vv, dd, c_ref, l_ref, mask):
                prob = jnp.where(mask, jnp.exp(_dot(qq, kk, NT) * SCALE - l_ref[:, one]), 0.0)
                return prob, (prob * (_dot(dd, vv, NT) + c_ref[:, one]) * SCALE).astype(BF16)

            _, ds_a = dscore(qj, kp, vp, doj, cj_ref, lj_ref, m_prev)
            p_b, ds_b = dscore(qj, kj, vj, doj, cj_ref, lj_ref, m_cur)
            p_c, ds_c = dscore(qn, kj, vj, don, cn_ref, ln_ref, m_next)
            dqs.append(_dot(ds_a, kp) + _dot(ds_b, kj))
            dks.append(_dot(ds_b, qj, TN) + _dot(ds_c, qn, TN))
            dvs.append(_dot(p_b.astype(BF16), doj, TN) + _dot(p_c.astype(BF16), don, TN))
        dq_ref[...] = jnp.concatenate(dqs, axis=1)
        dk_ref[...] = jnp.concatenate(dks, axis=1)
        dv_ref[...] = jnp.concatenate(dvs, axis=1)

    cur = pl.BlockSpec((QB, DSA_OUT_W), lambda c, j: (j, c))
    prev = pl.BlockSpec((QB, DSA_OUT_W), lambda c, j: (jnp.maximum(j - 1, 0), c))
    nxt = pl.BlockSpec((QB, DSA_OUT_W), lambda c, j: (jnp.minimum(j + 1, nb - 1), c))
    dq, dk, dv = pl.pallas_call(
        body, name=f"dsa_bwd_d{dil}", grid=(dil, nb),
        in_specs=[cur, nxt, prev, cur, prev, cur, cur, nxt, cur, nxt, cur, nxt], out_specs=[cur, cur, cur],
        out_shape=[SDS((n, dil * DSA_OUT_W), F32)] * 3,
        compiler_params=pltpu.CompilerParams(dimension_semantics=("parallel", "parallel")),
    )(q, q, k, k, v, v, do, do, cc, cc, lse, lse)
    return dq, dk, dv


def _ffn_fwd(tag, x, gain, w13, w2, plan=None):
    n = _tokmap(f"{tag}_norm", lambda xv, g: _rms_fwd(xv, g, _mean_all), [x], [gain], [(D, BF16)])[0]
    ab = _matmul(f"{tag}_up", n, w13, NN, BF16, plan=plan)

    def gate(abv):
        a, b = abv[:, :D_FF].astype(F32), abv[:, D_FF:].astype(F32)
        return a * jax.nn.sigmoid(a) * b

    h = _tokmap(f"{tag}_gate", gate, [ab], [], [(D_FF, BF16)], tile=256)[0]
    y = _matmul(f"{tag}_down", h, w2(), NN, F32, epi=lambda acc, res: res + 0.5 * acc, tiles=[x], plan=plan)
    return y, (n, ab, h)


def _ffn_bwd(tag, x, gain, w13, w2, saved, dy, plan=None, on_dw=None):
    n, ab, h = saved
    dh = _matmul(f"{tag}_bwd_dh", dy, w2, NT, BF16, epi=lambda acc: 0.5 * acc)

    def gate_bwd(abv, dhv):
        a, b, dhf = abv[:, :D_FF].astype(F32), abv[:, D_FF:].astype(F32), dhv.astype(F32)
        sg = jax.nn.sigmoid(a)
        da = dhf * b * (sg * (1.0 + a * (1.0 - sg)))
        return jnp.concatenate([da, dhf * (a * sg)], axis=1)

    dab = _tokmap(f"{tag}_bwd_gate", gate_bwd, [ab, dh], [], [(2 * D_FF, BF16)], tile=256)[0]
    dw2 = _matmul(f"{tag}_bwd_dw2", h, dy, TN, F32, epi=lambda acc: 0.5 * acc)
    dw13 = _matmul(f"{tag}_bwd_dw13", n, dab, TN, F32, plan=plan)
    if on_dw is not None:
        on_dw(dw13, dw2)
    dn = _matmul(f"{tag}_bwd_dn", dab, w13, NT, F32, plan=plan)

    def norm_bwd(xv, dnv, dyv, g):
        dx, dg = _rms_bwd(xv, g, dnv, _mean_all)
        return dx + dyv, dg

    dx, dgain = _tokmap(f"{tag}_bwd_norm", norm_bwd, [x, dn, dy], [gain], [(D, F32)], [(1, D)])
    return dx, dgain, dw13, dw2


def _rope_tables(s):
    half = HD // 2
    inv_freq = jnp.power(10000.0, -jnp.arange(half, dtype=F32) / half)
    ang = jnp.arange(s).astype(F32)[:, None] * inv_freq[None, :]
    cos, sin = jnp.cos(ang), jnp.sin(ang)
    return jnp.tile(jnp.concatenate([cos, cos], axis=1), (1, 2)), jnp.tile(jnp.concatenate([-sin, sin], axis=1), (1, 2))


def _local_step(x, mem, tgt, w, sm, plan=None, on_grads=None):
    s = x.shape[0]
    assert s % (QB * max(DSA_DILS)) == 0
    on_grads = on_grads or (lambda group, grads: None)
    c_sb, c_dsa, c_qm = 3 * SB_W, 3 * SB_W + 3 * DSA_W, 4096
    cos, sin = _rope_tables(s)
    bd768 = bd256 = _block_diag(128)
    gq_dsa, gk_dsa = jnp.tile(sm["qn_dsa"], (1, DSA_W // HD)), jnp.tile(sm["kn_dsa"], (1, DSA_W // HD))
    gq_mem, gk_mem = jnp.tile(sm["qn_mem"], (1, MEM_W // HD)), jnp.tile(sm["kn_mem"], (1, MEM_W // HD))

    w13_1 = jnp.concatenate([w["ffn1_w1"], w["ffn1_w3"]], axis=1)
    x1, ffn1_saved = _ffn_fwd("ffn1", x, sm["ffn1_norm"], w13_1, lambda: w["ffn1_w2"], plan)
    w_all = jnp.concatenate([w["w_in"], w["w_gate"]], axis=1)
    wb_sb, wb_dsa, wb_mem = w["w_branch_sb"], w["w_branch_dsa"], w["w_branch_mem"]
    hmix = _tokmap("mix_norm", lambda xv, g: _rms_fwd(xv, g, _mean_all), [x1], [sm["mix_norm"]], [(D, BF16)])[0]
    qkv_sb = _matmul("proj_sb", hmix, w_all[:, :c_sb], NN, BF16)
    qkv_dsa = _matmul("proj_dsa", hmix, w_all[:, c_sb:c_dsa], NN, BF16, plan=plan)
    q_mem = _matmul("proj_qmem", hmix, w_all[:, c_dsa:c_qm], NN, BF16)
    gpre = _matmul("proj_gate", hmix, w_all[:, c_qm:], NN, BF16, epi=lambda acc, b: acc + b, rows=[sm["b_gate"]], plan=plan)

    o_sb, sb_tot, sb_nblk = _sb_fwd(qkv_sb)

    def dsa_prep(qkv, cs, sn, gq, gk, bd):
        mean = _mean_heads(bd)
        qn = _rope_fwd(_rms_fwd(qkv[:, :DSA_W].astype(F32), gq, mean), cs, sn)
        kn = _rope_fwd(_rms_fwd(qkv[:, DSA_W:2 * DSA_W].astype(F32), gk, mean), cs, sn)
        v = qkv[:, 2 * DSA_W:]
        outs = []
        for t in (qn, kn, v):
            outs += [t[:, DSA_OUT_W * g:DSA_OUT_W * (g + 1)] for g in range(3)]
        return outs

    dsa_in = _tokmap("dsa_prep", dsa_prep, [qkv_dsa, cos, sin], [gq_dsa, gk_dsa, bd768], [(DSA_OUT_W, BF16)] * 9, tile=256,
                     dil_outs={j: DSA_DILS[j % 3] for j in range(9)})
    dsa_q, dsa_k, dsa_v = dsa_in[0:3], dsa_in[3:6], dsa_in[6:9]
    dsa_o, dsa_lse = zip(*[_dsa_fwd(dsa_q[g], dsa_k[g], dsa_v[g], DSA_DILS[g]) for g in range(3)])

    def alphas(l0, l1, l2):
        m = jnp.maximum(jnp.maximum(l0, l1), l2)
        e = [jnp.exp(l - m) for l in (l0, l1, l2)]
        tot = e[0] + e[1] + e[2]
        return [t / tot for t in e]

    def dsa_mix(o0, o1, o2, l0, l1, l2):
        a = alphas(l0, l1, l2)
        return a[0] * o0 + a[1] * o1 + a[2] * o2

    o_dsa = _tokmap("dsa_mix", dsa_mix, [*dsa_o, *dsa_lse], [], [(DSA_OUT_W, BF16)], tile=256,
                    dil_ins={j: DSA_DILS[j % 3] for j in range(6)})[0]

    def mem_kv(memv, g, wkv, gk, bd):
        kv = _dot(_rms_fwd(memv, g, _mean_all).astype(BF16), wkv)
        return _rms_fwd(kv[:, :MEM_W], gk, _mean_heads(bd)), kv[:, MEM_W:]

    km, vm = _tokmap("mem_kv", mem_kv, [mem], [sm["mem_norm"], w["w_mem_kv"], gk_mem, bd256], [(MEM_W, BF16)] * 2)

    def mem_probs(qv, kmv, gq, bd):
        qn = _rms_fwd(qv.astype(F32), gq, _mean_heads(bd)).astype(BF16)
        ps = []
        for h in range(MEM_W // HD):
            sl = slice(HD * h, HD * h + HD)
            sc = _dot(qn[:, sl], kmv[:, sl], NT) * SCALE
            e = jnp.exp(sc - jnp.max(sc, axis=1, keepdims=True))
            ps.append(e / jnp.sum(e, axis=1, keepdims=True))
        return qn, ps

    def mem_attn(qv, kmv, vmv, gq, bd):
        _, ps = mem_probs(qv, kmv, gq, bd)
        return jnp.concatenate([_dot(p.astype(BF16), vmv[:, HD * h:HD * h + HD]) for h, p in enumerate(ps)], axis=1)

    o_mem = _tokmap("mem_attn", mem_attn, [q_mem], [km, vm, gq_mem, bd256], [(MEM_W, BF16)])[0]

    def merge(osb, odsa, omem, gp, w_sb, w_dsa, w_mem):
        gates = jax.nn.sigmoid(gp.astype(F32))
        ys = (_dot(osb, w_sb), _dot(odsa, w_dsa), _dot(omem, w_mem))
        return gates, ys, gates[:, :D] * ys[0] + gates[:, D:2 * D] * ys[1] + gates[:, 2 * D:] * ys[2]

    merged = _tokmap("merge", lambda *a: merge(*a)[2], [o_sb, o_dsa, o_mem, gpre], [wb_sb, wb_dsa, wb_mem], [(D, BF16)],
                     tile=256)[0]
    x2 = _matmul("out_proj", merged, w["w_out"], NN, F32, epi=lambda acc, res: res + acc, tiles=[x1])
    w13_2 = jnp.concatenate([w["ffn2_w1"], w["ffn2_w3"]], axis=1)
    y, ffn2_saved = _ffn_fwd("ffn2", x2, sm["ffn2_norm"], w13_2, lambda: w["ffn2_w2"])

    def loss_fn(yv, tv):
        e = yv - tv
        part = 0.5 * jnp.sum(jnp.mean(e * e, axis=1, keepdims=True), axis=0, keepdims=True)
        return e * (1.0 / D), jnp.broadcast_to(part, (1, 128))

    dy, loss = _tokmap("loss", loss_fn, [y, tgt], [], [(D, F32)], [(1, 128)])

    gw, gs = {}, {}
    def ffn_grads(tag):
        def on_dw(dw13, dw2):
            gw[f"{tag}_w1"], gw[f"{tag}_w3"], gw[f"{tag}_w2"] = dw13[:, :D_FF], dw13[:, D_FF:], dw2
            on_grads(tag, {n: gw[n] for n in (f"{tag}_w1", f"{tag}_w3", f"{tag}_w2")})
        return on_dw

    dx2, gs["ffn2_norm"], _, _ = _ffn_bwd("ffn2", x2, sm["ffn2_norm"], w13_2, w["ffn2_w2"], ffn2_saved, dy, plan,
                                          ffn_grads("ffn2"))
    dmerged = _matmul("out_proj_bwd_dx", dx2, w["w_out"], NT, BF16)
    gw["w_out"] = _matmul("out_proj_bwd_dw", merged, dx2, TN, F32)

    def merge_bwd(osb, odsa, omem, gp, dm, w_sb, w_dsa, w_mem):
        gates, ys, _ = merge(osb, odsa, omem, gp, w_sb, w_dsa, w_mem)
        dmf = dm.astype(F32)
        dgp, dos, dws = [], [], []
        for b, (ov, wv) in enumerate(((osb, w_sb), (odsa, w_dsa), (omem, w_mem))):
            gb = gates[:, D * b:D * (b + 1)]
            dgp.append(dmf * ys[b] * gb * (1.0 - gb))
            dyb = (dmf * gb).astype(BF16)
            dos.append(_dot(dyb, wv, NT))
            dws.append(_dot(ov, dyb, TN))
        dgp = jnp.concatenate(dgp, axis=1)
        return dos[0], dos[1], dos[2], dgp, dws[0], dws[1], dws[2], jnp.sum(dgp, axis=0, keepdims=True)

    do_sb, do_dsa, do_mem, dgpre, gw["w_branch_sb"], gw["w_branch_dsa"], gw["w_branch_mem"], gs["b_gate"] = _tokmap(
        "merge_bwd", merge_bwd, [o_sb, o_dsa, o_mem, gpre, dmerged], [wb_sb, wb_dsa, wb_mem],
        [(SB_W, BF16), (DSA_OUT_W, F32), (MEM_W, BF16), (3 * D, BF16)],
        [(SB_W, D), (DSA_OUT_W, D), (MEM_W, D), (1, 3 * D)], tile=256)

    dq_sb, dk_sb, dv_sb = _sb_bwd(qkv_sb, do_sb, sb_tot, sb_nblk)

    def dsa_mix_bwd(o0, o1, o2, l0, l1, l2, dov, bd):
        a = alphas(l0, l1, l2)
        omix = a[0] * o0 + a[1] * o1 + a[2] * o2
        dot_o = _head_sums(dov * omix, bd)
        return [dov * t for t in a] + [-t * dot_o for t in a]

    mixb = _tokmap("dsa_mix_bwd", dsa_mix_bwd, [*dsa_o, *dsa_lse, do_dsa], [bd256],
                   [(DSA_OUT_W, BF16)] * 3 + [(DSA_OUT_W, F32)] * 3, tile=256,
                   dil_ins={j: DSA_DILS[j % 3] for j in range(6)}, dil_outs={j: DSA_DILS[j % 3] for j in range(6)})
    dsa_d = [_dsa_bwd(dsa_q[g], dsa_k[g], dsa_v[g], mixb[g], mixb[3 + g], dsa_lse[g], DSA_DILS[g]) for g in range(3)]

    def dsa_prep_bwd(qkv, cs, sn, *rest):
        dqs, dks, dvs, (gq, gk, bd) = rest[0:3], rest[3:6], rest[6:9], rest[9:]
        mean = _mean_heads(bd)
        dq, dgq = _rms_bwd(qkv[:, :DSA_W].astype(F32), gq, _rope_bwd(jnp.concatenate(dqs, axis=1), cs, sn), mean)
        dk, dgk = _rms_bwd(qkv[:, DSA_W:2 * DSA_W].astype(F32), gk, _rope_bwd(jnp.concatenate(dks, axis=1), cs, sn), mean)
        return jnp.concatenate([dq, dk] + list(dvs), axis=1), dgq, dgk

    dqkv_dsa, dgq_dsa, dgk_dsa = _tokmap(
        "dsa_prep_bwd", dsa_prep_bwd,
        [qkv_dsa, cos, sin] + [dsa_d[g][t] for t in range(3) for g in range(3)], [gq_dsa, gk_dsa, bd768],
        [(3 * DSA_W, BF16)], [(1, DSA_W), (1, DSA_W)], tile=256, dil_ins={3 + j: DSA_DILS[j % 3] for j in range(9)})
    gs["qn_dsa"] = dgq_dsa.reshape(DSA_W // HD, HD).sum(axis=0, keepdims=True)
    gs["kn_dsa"] = dgk_dsa.reshape(DSA_W // HD, HD).sum(axis=0, keepdims=True)

    def mem_attn_bwd(qv, dov, kmv, vmv, gq, bd):
        qn, ps = mem_probs(qv, kmv, gq, bd)
        dqn, dkm, dvm = [], [], []
        for h, p in enumerate(ps):
            sl = slice(HD * h, HD * h + HD)
            dp = _dot(dov[:, sl], vmv[:, sl], NT)
            ds = (p * (dp - jnp.sum(p * dp, axis=1, keepdims=True)) * SCALE).astype(BF16)
            dqn.append(_dot(ds, kmv[:, sl]))
            dkm.append(_dot(ds, qn[:, sl], TN))
            dvm.append(_dot(p.astype(BF16), dov[:, sl], TN))
        dq, dgq = _rms_bwd(qv.astype(F32), gq, jnp.concatenate(dqn, axis=1), _mean_heads(bd))
        return dq, jnp.concatenate(dkm, axis=1), jnp.concatenate(dvm, axis=1), dgq

    dq_mem, dkm, dvm, dgq_mem = _tokmap("mem_attn_bwd", mem_attn_bwd, [q_mem, do_mem], [km, vm, gq_mem, bd256],
                                        [(MEM_W, BF16)], [(MEM_LEN, MEM_W), (MEM_LEN, MEM_W), (1, MEM_W)])
    gs["qn_mem"] = dgq_mem.reshape(MEM_W // HD, HD).sum(axis=0, keepdims=True)

    def mem_kv_bwd(memv, dkmv, dvmv, g, wkv, gk, bd):
        memn = _rms_fwd(memv, g, _mean_all).astype(BF16)
        kv = _dot(memn, wkv)
        dk, dgk = _rms_bwd(kv[:, :MEM_W], gk, dkmv, _mean_heads(bd))
        dkv = jnp.concatenate([dk, dvmv], axis=1).astype(BF16)
        _, dg = _rms_bwd(memv, g, _dot(dkv, wkv, NT), _mean_all)
        return _dot(memn, dkv, TN), dg, dgk

    gw["w_mem_kv"], gs["mem_norm"], dgk_mem = _tokmap(
        "mem_kv_bwd", mem_kv_bwd, [mem, dkm, dvm], [sm["mem_norm"], w["w_mem_kv"], gk_mem, bd256], [],
        [(D, 2 * MEM_W), (1, D), (1, MEM_W)])
    gs["kn_mem"] = dgk_mem.reshape(MEM_W // HD, HD).sum(axis=0, keepdims=True)

    dall = jnp.concatenate([dq_sb.astype(BF16), dk_sb.astype(BF16), dv_sb.astype(BF16), dqkv_dsa, dq_mem, dgpre], axis=1)
    dhmix = _matmul("proj_bwd_dx", dall, w_all, NT, F32)
    dw_all = _matmul("proj_bwd_dw", hmix, dall, TN, F32)
    gw["w_in"], gw["w_gate"] = dw_all[:, :c_qm], dw_all[:, c_qm:]
    on_grads("mid", {n: gw[n] for n in GROUPS["mid"]})

    def mix_norm_bwd(xv, dnv, dyv, g):
        dx, dg = _rms_bwd(xv, g, dnv, _mean_all)
        return dx + dyv, dg

    dx1, gs["mix_norm"] = _tokmap("mix_norm_bwd", mix_norm_bwd, [x1, dhmix, dx2], [sm["mix_norm"]], [(D, F32)], [(1, D)])
    gx, gs["ffn1_norm"], _, _ = _ffn_bwd("ffn1", x, sm["ffn1_norm"], w13_1, w["ffn1_w2"], ffn1_saved, dx1, plan,
                                         ffn_grads("ffn1"))
    return loss, gx, gw, gs


def _shard_shape(shape, axis):
    return (shape[0] // N_CHIPS, shape[1]) if axis == 0 else (shape[0], shape[1] // N_CHIPS)


def _pack(shards, names):
    return jnp.concatenate([shards[n].reshape(-1, D) for n in names], axis=0)


def _unpack(pack, names):
    out, r = {}, 0
    for n in names:
        ss = _shard_shape(*SHARDED_BY_NAME[n])
        rows = ss[0] * ss[1] // D
        out[n] = pack[r:r + rows].reshape(ss)
        r += rows
    return out


def _full_from_packs(packs, names):
    per_chip = [_unpack(packs[c], names) for c in range(N_CHIPS)]
    return {n: jnp.concatenate([per_chip[c][n] for c in range(N_CHIPS)], axis=SHARDED_BY_NAME[n][1]) for n in names}


def _packs_from_full(full, names, dtype):
    packs = []
    for c in range(N_CHIPS):
        shards = {}
        for n in names:
            sh, ax = SHARDED_BY_NAME[n]
            ss = _shard_shape(sh, ax)
            shards[n] = lax.slice_in_dim(full[n], c * ss[ax], (c + 1) * ss[ax], axis=ax).astype(dtype)
        packs.append(_pack(shards, names))
    return jnp.stack(packs)


def _own_pack(full, names, chip):
    shards = {}
    for n in names:
        sh, ax = SHARDED_BY_NAME[n]
        ss = _shard_shape(sh, ax)
        shards[n] = lax.dynamic_slice_in_dim(full[n], chip * ss[ax], ss[ax], axis=ax)
    return _pack(shards, names)


SMALL_USED = sum(n for _, n in SMALL)


def _pack_small(d, loss=None):
    parts = [d[n].reshape(-1) for n, _ in SMALL]
    parts.append(jnp.zeros((1,), F32) if loss is None else loss.reshape(1))
    parts.append(jnp.zeros((SMALL_ROWS * D - SMALL_USED - 1,), F32))
    return jnp.concatenate(parts).reshape(SMALL_ROWS, D)


def _unpack_small(v):
    flat, out, r = v.reshape(-1), {}, 0
    for n, k in SMALL:
        out[n] = flat[r:r + k]
        r += k
    return out, flat[r]


def _place():
    return lax.axis_index("x"), lax.axis_index("y"), lax.axis_index("c")


def _other_chips(x, y):
    return [(1 - x, y), (x, 1 - y), (1 - x, 1 - y)]


HBM_SPEC = pl.BlockSpec(memory_space=pl.ANY)


CHIP_SEMS = (pltpu.SemaphoreType.DMA((3,)), pltpu.SemaphoreType.DMA((3,)), pltpu.SemaphoreType.DMA)


def _gather_copies(ins, outs, send_sems, recv_sems, local_sem):
    (src,), (out,) = ins, outs
    x, y, c = _place()
    me = 2 * x + y
    copies = [pltpu.make_async_copy(src, out.at[me], local_sem)]
    copies += [pltpu.make_async_remote_copy(src_ref=src, dst_ref=out.at[me], send_sem=send_sems.at[k], recv_sem=recv_sems.at[k],
                                            device_id=(px, py, c), device_id_type=MESH)
               for k, (px, py) in enumerate(_other_chips(x, y))]
    return copies


def _scatter_copies(ins, outs, send_sems, recv_sems, local_sem):
    (src,), (out,) = ins, outs
    x, y, c = _place()
    return [pltpu.make_async_remote_copy(src_ref=src.at[2 * px + py], dst_ref=out.at[k], send_sem=send_sems.at[k],
                                         recv_sem=recv_sems.at[k], device_id=(px, py, c), device_id_type=MESH)
            for k, (px, py) in enumerate(_other_chips(x, y))]


def _all_gather_chips(pack):
    def body(src, out, *sems):
        copies = _gather_copies((src,), (out,), *sems)
        for cp in copies:
            cp.start()
        for cp in copies:
            cp.wait()

    return pl.pallas_call(
        body, name="weights_all_gather", in_specs=[HBM_SPEC], out_specs=HBM_SPEC,
        out_shape=SDS((N_CHIPS,) + pack.shape, pack.dtype), scratch_shapes=list(CHIP_SEMS),
    )(pack)


def _swap_with_sibling(name, v):
    def body(src, out, send_sem, recv_sem):
        x, y, c = _place()
        cp = pltpu.make_async_remote_copy(src_ref=src, dst_ref=out, send_sem=send_sem, recv_sem=recv_sem,
                                          device_id=(x, y, 1 - c), device_id_type=MESH)
        cp.start()
        cp.wait()

    return pl.pallas_call(
        body, name=name, in_specs=[HBM_SPEC], out_specs=HBM_SPEC, out_shape=SDS(v.shape, v.dtype),
        scratch_shapes=[pltpu.SemaphoreType.DMA, pltpu.SemaphoreType.DMA],
    )(v)


def _all_reduce_small(v):
    n_dev = 8

    def body(v_ref, out_ref, land, send_sems, recv_sems):
        x, y, c = _place()
        me = 4 * x + 2 * y + c
        land[me] = v_ref[...]
        copies = []
        for k in range(1, n_dev):
            peer = (x ^ (k >> 2), y ^ ((k >> 1) & 1), c ^ (k & 1))
            copies.append(pltpu.make_async_remote_copy(src_ref=v_ref, dst_ref=land.at[me], send_sem=send_sems.at[k - 1],
                                                       recv_sem=recv_sems.at[k - 1], device_id=peer, device_id_type=MESH))
        for cp in copies:
            cp.start()
        for cp in copies:
            cp.wait()
        acc = land[0]
        for d in range(1, n_dev):
            acc = acc + land[d]
        out_ref[...] = acc

    return pl.pallas_call(
        body, name="small_all_reduce", in_specs=[pl.BlockSpec(memory_space=pltpu.VMEM)],
        out_specs=pl.BlockSpec(memory_space=pltpu.VMEM), out_shape=SDS(v.shape, v.dtype),
        scratch_shapes=[pltpu.VMEM((n_dev,) + v.shape, v.dtype), pltpu.SemaphoreType.DMA((n_dev - 1,)),
                        pltpu.SemaphoreType.DMA((n_dev - 1,))],
    )(v)


def _adamw(g, wv, m, v):
    m = ADAM_B1 * m + (1.0 - ADAM_B1) * g
    v = ADAM_B2 * v + (1.0 - ADAM_B2) * (g * g)
    m_hat = m / (1.0 - ADAM_B1 ** ADAM_STEP)
    v_hat = v / (1.0 - ADAM_B2 ** ADAM_STEP)
    delta = -ADAM_LR * (m_hat / (jnp.sqrt(v_hat) + ADAM_EPS) + ADAM_WD * wv)
    return delta, m, v


def kernel(x, mem, ffn1_norm, ffn1_w1, ffn1_w3, ffn1_w2, mix_norm, mem_norm, w_in, w_mem_kv, qn_dsa, kn_dsa, qn_mem, kn_mem, w_branch_sb, w_branch_dsa, w_branch_mem, w_gate, b_gate, w_out, ffn2_norm, ffn2_w1, ffn2_w3, ffn2_w2, loss_target, m_ffn1_norm, m_ffn1_w1, m_ffn1_w3, m_ffn1_w2, m_mix_norm, m_mem_norm, m_w_in, m_w_mem_kv, m_qn_dsa, m_kn_dsa, m_qn_mem, m_kn_mem, m_w_branch_sb, m_w_branch_dsa, m_w_branch_mem, m_w_gate, m_b_gate, m_w_out, m_ffn2_norm, m_ffn2_w1, m_ffn2_w3, m_ffn2_w2, v_ffn1_norm, v_ffn1_w1, v_ffn1_w3, v_ffn1_w2, v_mix_norm, v_mem_norm, v_w_in, v_w_mem_kv, v_qn_dsa, v_kn_dsa, v_qn_mem, v_kn_mem, v_w_branch_sb, v_w_branch_dsa, v_w_branch_mem, v_w_gate, v_b_gate, v_w_out, v_ffn2_norm, v_ffn2_w1, v_ffn2_w3, v_ffn2_w2):
    given = dict(locals())
    wts = {n: given[n][0] for n in WEIGHTS}
    moms = {n: given["m_" + n][0] for n in WEIGHTS}
    vars_ = {n: given["v_" + n][0] for n in WEIGHTS}

    plan = _Plan()
    x_i, y_i, _ = _place()
    my_chip = 2 * x_i + y_i

    full = {}
    for host, names in WEIGHT_PIECES:
        pack = _pack(wts, names).astype(BF16)
        if host is None:
            full.update(_full_from_packs(_all_gather_chips(pack), names))
        else:
            plan.put(host, _Carry([pack], [SDS((N_CHIPS,) + pack.shape, BF16)], CHIP_SEMS, _gather_copies,
                                  lambda res, names=names: full.update(_full_from_packs(res[0], names))))
    small = {n: wts[n].reshape(1, -1) for n, _ in SMALL}

    landed = {}

    def on_grads(group, grads):
        names = GROUPS[group]
        packs = _packs_from_full(grads, names, BF16)
        own = _own_pack(grads, names, my_chip)
        plan.put(GRAD_HOSTS[group], _Carry([packs], [SDS((3,) + packs.shape[1:], BF16)], CHIP_SEMS, _scatter_copies,
                                           lambda res: landed.update({group: (own, res[0])})))

    loss, gx, _, gs = _local_step(x[0], mem[0], loss_target[0], full, small, plan, on_grads)
    assert not plan.pending, list(plan.pending)

    def update(hv, ov, wv, mv, vv):
        g = hv + ov
        return (g,) + _adamw(g, wv, mv, vv)

    outs = [{}, {}, {}, {}]
    for group, names in GROUPS.items():
        own, got = landed[group]
        half = _tokmap(f"grads_sum_chips_{group}",
                       lambda a, b0, b1, b2: ((a + b0.astype(F32)) + b1.astype(F32)) + b2.astype(F32),
                       [own, got[0], got[1], got[2]], [], [(D, F32)])[0]
        other = _swap_with_sibling(f"grads_swap_cores_{group}", half)
        res = _tokmap(f"adamw_{group}", update, [half, other, _pack(wts, names), _pack(moms, names), _pack(vars_, names)],
                      [], [(D, F32)] * 4)
        for d, packed in zip(outs, res):
            d.update(_unpack(packed, names))

    s_red = _all_reduce_small(_pack_small(gs, loss[0, 0]))
    res = _tokmap(
        "adamw_small", lambda g, wv, mv, vv: (g,) + _adamw(g, wv, mv, vv),
        [s_red, _pack_small(small), _pack_small({n: moms[n] for n, _ in SMALL}), _pack_small({n: vars_[n] for n, _ in SMALL})],
        [], [(D, F32)] * 4)
    for d, packed in zip(outs, res):
        d.update(_unpack_small(packed)[0])
    _, total_loss = _unpack_small(s_red)
    return (total_loss, gx[None], *[d[n][None] for d in outs for n in WEIGHTS])
```

```python
import functools

import numpy as np
import jax
import jax.numpy as jnp
from jax import lax
from jax.experimental import pallas as pl
from jax.experimental.pallas import tpu as pltpu

F32, BF16 = jnp.float32, jnp.bfloat16
SDS = jax.ShapeDtypeStruct
MESH = pl.DeviceIdType.MESH

D = 1024
HD = 64
QB = 128
D_FF = 2816
SB_W, DSA_W, DSA_OUT_W, MEM_W = 512, 768, 256, 256
DSA_DILS = (1, 4, 16)
MEM_LEN = 256
N_CHIPS = 4
EPS = 1e-6
SCALE = HD ** -0.5
EXHAUSTED = -104.0
SB_FWD_HEADS = 4
SB_WIN = 384
NEG = -1e30
VMEM_LIMIT = 56 * 1024 * 1024

ADAM_LR, ADAM_B1, ADAM_B2, ADAM_EPS, ADAM_WD, ADAM_STEP = 0.001, 0.9, 0.999, 1e-08, 0.01, 10

NN = (((1,), (0,)), ((), ()))
NT = (((1,), (1,)), ((), ()))
TN = (((0,), (0,)), ((), ()))

SHARDED = (
    ("ffn1_w1", (D, D_FF), 1), ("ffn1_w3", (D, D_FF), 1), ("ffn1_w2", (D_FF, D), 0),
    ("w_in", (D, 4096), 1), ("w_mem_kv", (D, 512), 0),
    ("w_branch_sb", (SB_W, D), 1), ("w_branch_dsa", (DSA_OUT_W, D), 1), ("w_branch_mem", (MEM_W, D), 1),
    ("w_gate", (D, 3 * D), 1), ("w_out", (D, D), 0),
    ("ffn2_w1", (D, D_FF), 1), ("ffn2_w3", (D, D_FF), 1), ("ffn2_w2", (D_FF, D), 0),
)
SHARDED_BY_NAME = {n: (sh, ax) for n, sh, ax in SHARDED}
GROUPS = {
    "ffn2": ("ffn2_w1", "ffn2_w3", "ffn2_w2"),
    "mid": ("w_in", "w_mem_kv", "w_branch_sb", "w_branch_dsa", "w_branch_mem", "w_gate", "w_out"),
    "ffn1": ("ffn1_w1", "ffn1_w3", "ffn1_w2"),
}
WEIGHT_PIECES = (
    (None, ("ffn1_w1", "ffn1_w3")),
    ("ffn1_up", ("ffn1_w2", "w_in")),
    ("ffn1_down", ("w_gate", "w_mem_kv", "w_branch_sb", "w_branch_dsa", "w_branch_mem", "w_out")),
    ("proj_dsa", ("ffn2_w2",)),
    ("proj_gate", ("ffn2_w1", "ffn2_w3")),
)
GRAD_HOSTS = {"ffn2": "ffn2_bwd_dn", "mid": "ffn1_bwd_dw13", "ffn1": "ffn1_bwd_dn"}
SMALL = (("ffn1_norm", D), ("mix_norm", D), ("mem_norm", D), ("ffn2_norm", D), ("b_gate", 3 * D),
         ("qn_dsa", HD), ("kn_dsa", HD), ("qn_mem", HD), ("kn_mem", HD))
WEIGHTS = ("ffn1_norm", "ffn1_w1", "ffn1_w3", "ffn1_w2", "mix_norm", "mem_norm", "w_in", "w_mem_kv", "qn_dsa", "kn_dsa",
           "qn_mem", "kn_mem", "w_branch_sb", "w_branch_dsa", "w_branch_mem", "w_gate", "b_gate", "w_out", "ffn2_norm",
           "ffn2_w1", "ffn2_w3", "ffn2_w2")
SMALL_ROWS = 8


def _dot(a, b, dn=NN):
    return lax.dot_general(a, b, dn, preferred_element_type=F32)


def _dot01(x, m01):
    hi = x.astype(BF16)
    r1 = x - hi.astype(F32)
    mid = r1.astype(BF16)
    lo = (r1 - mid.astype(F32)).astype(BF16)
    return _dot(hi, m01) + _dot(mid, m01) + _dot(lo, m01)


def _pick(n, cands):
    for c in cands:
        if n % c == 0:
            return c
    raise ValueError(f"no tile for {n}")


def _from_dilated(v, d, scr):
    w = v.shape[1] // d
    v = v.astype(F32)
    for c in range(d):
        for p, buf in enumerate(scr[:w // 128]):
            buf[pl.ds(c, v.shape[0], stride=d), :] = v[:, c * w + 128 * p:c * w + 128 * (p + 1)]
    return jnp.concatenate([buf[...] for buf in scr[:w // 128]], axis=1)


def _to_dilated(v, d, scr):
    w = v.shape[1]
    for p, buf in enumerate(scr[:w // 128]):
        buf[...] = v[:, 128 * p:128 * (p + 1)].astype(F32)
    return jnp.concatenate([buf[pl.ds(c, v.shape[0] // d, stride=d), :] for c in range(d) for buf in scr[:w // 128]], axis=1)


def _tokmap(name, fn, tok_ins, consts, tok_outs, acc_outs=(), tile=512, dil_ins=None, dil_outs=None):
    dil_ins, dil_outs = dil_ins or {}, dil_outs or {}
    n = tok_ins[0].shape[0] * dil_ins.get(0, 1)
    tile = _pick(n, [t for t in (512, 256, 128, 64, 32, 16, 8) if t <= tile])
    n_tin, n_in, n_tok, n_acc = len(tok_ins), len(tok_ins) + len(consts), len(tok_outs), len(acc_outs)
    n_scr = max([tok_ins[j].shape[1] // d // 128 for j, d in dil_ins.items() if d > 1]
                + [tok_outs[j][0] // 128 for j, d in dil_outs.items() if d > 1] + [0])

    def body(*refs):
        scr = refs[len(refs) - n_scr:]
        vals = [r[...] for r in refs[:n_in]]
        for j, d in dil_ins.items():
            if d > 1:
                vals[j] = _from_dilated(vals[j], d, scr)
        outs = fn(*vals)
        outs = list(outs) if isinstance(outs, (tuple, list)) else [outs]
        assert len(outs) == n_tok + n_acc, (name, len(outs))
        for j, d in dil_outs.items():
            if d > 1:
                outs[j] = _to_dilated(outs[j], d, scr)
        orefs = refs[n_in:]
        for r, v in zip(orefs[:n_tok], outs[:n_tok]):
            r[...] = v.astype(r.dtype)
        if n_acc:
            @pl.when(pl.program_id(0) == 0)
            def _():
                for r in orefs[n_tok:n_tok + n_acc]:
                    r[...] = jnp.zeros(r.shape, r.dtype)
            for r, v in zip(orefs[n_tok:n_tok + n_acc], outs[n_tok:]):
                r[...] += v.astype(r.dtype)

    def tok_spec(width, d):
        return pl.BlockSpec((tile // d, d * width), lambda i: (i, 0))

    in_specs = [tok_spec(a.shape[1] // dil_ins.get(j, 1), dil_ins.get(j, 1)) for j, a in enumerate(tok_ins)]
    in_specs += [pl.BlockSpec(c.shape, lambda i: (0, 0)) for c in consts]
    out_specs = [tok_spec(w, dil_outs.get(j, 1)) for j, (w, _) in enumerate(tok_outs)]
    out_specs += [pl.BlockSpec(s, lambda i: (0, 0)) for s in acc_outs]
    out_shape = [SDS((n // dil_outs.get(j, 1), w * dil_outs.get(j, 1)), dt) for j, (w, dt) in enumerate(tok_outs)]
    out_shape += [SDS(s, F32) for s in acc_outs]
    res = pl.pallas_call(
        body, name=name, grid=(n // tile,), in_specs=in_specs, out_specs=out_specs, out_shape=out_shape,
        scratch_shapes=[pltpu.VMEM((tile, 128), F32)] * n_scr,
        compiler_params=pltpu.CompilerParams(dimension_semantics=("arbitrary",), vmem_limit_bytes=VMEM_LIMIT),
    )(*tok_ins, *consts)
    return res


MATMUL_VMEM_BUDGET = 40 * 1024 * 1024


def _matmul_tiles(m, n, k, a_bytes, b_bytes, o_bytes, extra_bytes):
    best = None
    for tk in [c for c in (3584, 2816, 2048, 1408, 1024, 512, 256, 128) if k % c == 0]:
        for tm in [c for c in (1408, 1024, 768, 512, 256, 128) if m % c == 0]:
            for tn in [c for c in (1408, 1024, 768, 512, 256, 128) if n % c == 0]:
                need = 2 * tk * (tm * a_bytes + tn * b_bytes) + tm * tn * (2 * o_bytes + 2 * extra_bytes + 8)
                if need > MATMUL_VMEM_BUDGET:
                    continue
                score = (min(tm, 512) * min(tn, 512), tk, tm * tn)
                if best is None or score > best[0]:
                    best = (score, (tm, tn, tk))
    return best[1]


class _Carry:
    def __init__(self, ins, outs, sems, copies, then):
        self.ins, self.outs, self.sems, self.copies, self.then = ins, outs, sems, copies, then


class _Plan:
    def __init__(self):
        self.pending = {}

    def put(self, host, carry):
        assert host not in self.pending, host
        self.pending[host] = carry

    def take(self, host):
        return self.pending.pop(host, None)


def _matmul(name, a, b, dn, out_dtype, epi=None, tiles=(), rows=(), plan=None):
    if dn == NN:
        (m, k), n = a.shape, b.shape[1]
    elif dn == NT:
        (m, k), n = a.shape, b.shape[0]
    else:
        (k, m), n = a.shape, b.shape[1]
    n_t, n_r = len(tiles), len(rows)
    tm, tn, tk = _matmul_tiles(m, n, k, a.dtype.itemsize, b.dtype.itemsize, jnp.dtype(out_dtype).itemsize,
                               sum(t.dtype.itemsize for t in tiles))
    nk = k // tk
    grid = (m // tm, n // tn, nk)
    carry = plan.take(name) if plan is not None else None
    n_ci, n_co = (len(carry.ins), len(carry.outs)) if carry else (0, 0)

    def body(a_ref, b_ref, *rest):
        extras, rest = rest[:n_t + n_r], rest[n_t + n_r:]
        c_in, o_ref, c_out, scratch = rest[:n_ci], rest[n_ci], rest[n_ci + 1:n_ci + 1 + n_co], rest[n_ci + 1 + n_co:]
        ids = [pl.program_id(d) for d in range(3)]
        if carry:
            sems = scratch[1:] if nk > 1 else scratch

            @pl.when((ids[0] == 0) & (ids[1] == 0) & (ids[2] == 0))
            def _():
                for cp in carry.copies(c_in, c_out, *sems):
                    cp.start()

        part = _dot(a_ref[...].astype(BF16), b_ref[...].astype(BF16), dn)

        def finish(r):
            if epi is not None:
                r = epi(r, *[e[...] for e in extras])
            o_ref[...] = r.astype(o_ref.dtype)

        if nk == 1:
            finish(part)
        else:
            acc = scratch[0]

            @pl.when(ids[2] == 0)
            def _():
                acc[...] = part

            @pl.when(ids[2] > 0)
            def _():
                acc[...] += part

            @pl.when(ids[2] == nk - 1)
            def _():
                finish(acc[...])

        if carry:
            @pl.when((ids[0] == grid[0] - 1) & (ids[1] == grid[1] - 1) & (ids[2] == nk - 1))
            def _():
                for cp in carry.copies(c_in, c_out, *sems):
                    cp.wait()

    a_spec = pl.BlockSpec((tk, tm), lambda i, j, kk: (kk, i)) if dn == TN else pl.BlockSpec((tm, tk), lambda i, j, kk: (i, kk))
    b_spec = pl.BlockSpec((tn, tk), lambda i, j, kk: (j, kk)) if dn == NT else pl.BlockSpec((tk, tn), lambda i, j, kk: (kk, j))
    in_specs = [a_spec, b_spec] + [pl.BlockSpec((tm, tn), lambda i, j, kk: (i, j)) for _ in tiles]
    in_specs += [pl.BlockSpec((1, tn), lambda i, j, kk: (0, j)) for _ in rows] + [HBM_SPEC] * n_ci
    res = pl.pallas_call(
        body, name=name, grid=grid, in_specs=in_specs,
        out_specs=[pl.BlockSpec((tm, tn), lambda i, j, kk: (i, j))] + [HBM_SPEC] * n_co,
        out_shape=[SDS((m, n), out_dtype)] + (list(carry.outs) if carry else []),
        scratch_shapes=([pltpu.VMEM((tm, tn), F32)] if nk > 1 else []) + (list(carry.sems) if carry else []),
        compiler_params=pltpu.CompilerParams(
            dimension_semantics=("arbitrary",) * 3 if carry else ("parallel", "parallel", "arbitrary"),
            vmem_limit_bytes=VMEM_LIMIT),
    )(a, b, *tiles, *rows, *(carry.ins if carry else []))
    if carry:
        carry.then(res[1:])
    return res[0]


def _mean_all(v):
    return jnp.mean(v, axis=-1, keepdims=True)


def _head_sums(v, bd):
    w = bd.shape[0]
    return jnp.concatenate([_dot01(v[:, j:j + w], bd) for j in range(0, v.shape[1], w)], axis=1)


def _mean_heads(bd):
    return lambda v: _head_sums(v, bd) * (1.0 / HD)


def _rms_fwd(x, g, mean):
    return x * lax.rsqrt(mean(x * x) + EPS) * g


def _rms_bwd(x, g, dy, mean):
    r = lax.rsqrt(mean(x * x) + EPS)
    dn = dy * g
    dx = r * dn - x * (r * r * r) * mean(dn * x)
    return dx, jnp.sum(dy * x * r, axis=0, keepdims=True)


def _swap_halves(x):
    w = x.shape[1]
    lane = lax.broadcasted_iota(jnp.int32, x.shape, 1)
    return jnp.where(lane % HD < HD // 2, pltpu.roll(x, w - HD // 2, 1), pltpu.roll(x, HD // 2, 1))


def _lanes(t, w):
    return jnp.tile(t, (1, w // t.shape[1]))


def _rope_fwd(x, cos, sin_signed):
    return x * _lanes(cos, x.shape[1]) + _swap_halves(x) * _lanes(sin_signed, x.shape[1])


def _rope_bwd(dy, cos, sin_signed):
    return dy * _lanes(cos, dy.shape[1]) + _swap_halves(dy * _lanes(sin_signed, dy.shape[1]))


def _bcast_heads(cols):
    return jnp.concatenate([jnp.broadcast_to(c, (c.shape[0], HD)) for c in cols], axis=1)


def _softplus(z):
    return jnp.maximum(z, 0.0) + jnp.log1p(jnp.exp(-jnp.abs(z)))


def _block_diag(w):
    h = np.arange(w) // HD
    return jnp.asarray(h[:, None] == h[None, :], BF16)


def _sb_window(i, t):
    hi = (i + 1) * QB - t * SB_WIN
    lo = hi - SB_WIN
    ws = pl.multiple_of(jnp.maximum(lo, 0), QB)
    kpos = ws + lax.broadcasted_iota(jnp.int32, (QB, SB_WIN), 1)
    qpos = i * QB + lax.broadcasted_iota(jnp.int32, (QB, SB_WIN), 0)
    return (kpos < qpos) & (kpos >= lo) & (kpos < hi), ws


def _sb_fwd(qkv):
    s = qkv.shape[0]
    assert s >= SB_WIN
    nq = s // QB
    nh = SB_FWD_HEADS
    bw = HD * nh
    ngroups = SB_W // bw

    def body(q_ref, k_ref, v_ref, later_ref, o_ref, tot_ref, nb_ref):
        p, i = pl.program_id(0), pl.program_id(1)
        q = q_ref[...]
        later_of = later_ref[...]

        def step(c):
            t, _, tots, outs = c
            mask, ws = _sb_window(i, t)
            kw, vw = k_ref[pl.ds(ws, SB_WIN), :], v_ref[pl.ds(ws, SB_WIN), :]
            new_t, new_o = [], []
            for hh in range(nh):
                sl = slice(HD * hh, HD * hh + HD)
                z = _dot(q[:, sl], kw[:, sl], NT) * SCALE
                sp = _softplus(z)
                lf = jnp.where(mask, -sp, 0.0)
                later = tots[hh] + _dot01(lf, later_of)
                w = jnp.where(mask, jnp.exp(z - sp + later), 0.0)
                new_o.append(outs[hh] + _dot(w.astype(BF16), vw[:, sl]))
                new_t.append(tots[hh] + jnp.sum(lf, axis=1, keepdims=True))
            alive = functools.reduce(jnp.maximum, [jnp.max(v) for v in new_t])
            return t + 1, alive, tuple(new_t), tuple(new_o)

        zt, zo = jnp.zeros((QB, 1), F32), jnp.zeros((QB, HD), F32)
        t, _, tots, outs = lax.while_loop(lambda c: ((i + 1) * QB - c[0] * SB_WIN > 0) & (c[1] > EXHAUSTED), step,
                                          (jnp.int32(0), jnp.float32(0.0), (zt,) * nh, (zo,) * nh))
        o_ref[...] = jnp.concatenate(outs, axis=1).astype(o_ref.dtype)
        tot_ref[...] = _bcast_heads(tots)
        nb_ref[p, i] = t

    whole = lambda off: pl.BlockSpec((s, bw), lambda p, i: (0, off + p), pipeline_mode=pl.Buffered(1))
    tile = pl.BlockSpec((QB, bw), lambda p, i: (i, p))
    tri = pl.BlockSpec((SB_WIN, SB_WIN), lambda p, i: (0, 0), pipeline_mode=pl.Buffered(1))
    idx = np.arange(SB_WIN)
    return pl.pallas_call(
        body, name="sb_fwd", grid=(ngroups, nq),
        in_specs=[tile, whole(ngroups), whole(2 * ngroups), tri],
        out_specs=[tile, tile, pl.BlockSpec(memory_space=pltpu.SMEM)],
        out_shape=[SDS((s, SB_W), BF16), SDS((s, SB_W), F32), SDS((ngroups, nq), jnp.int32)],
        compiler_params=pltpu.CompilerParams(dimension_semantics=("arbitrary", "arbitrary"), vmem_limit_bytes=VMEM_LIMIT),
    )(qkv, qkv, qkv, jnp.asarray(idx[:, None] > idx[None, :], BF16))


def _sb_bwd(qkv, do, tot, nblk):
    s = qkv.shape[0]
    nq = s // QB
    npairs = SB_W // 128

    def body(nb_ref, q_ref, k_ref, v_ref, do_ref, tot_ref, upto_ref, before_ref, dq_ref, dk_ref, dv_ref):
        p, i = pl.program_id(0), pl.program_id(1)

        @pl.when(i == 0)
        def _():
            dk_ref[...] = jnp.zeros(dk_ref.shape, F32)
            dv_ref[...] = jnp.zeros(dv_ref.shape, F32)

        upto = upto_ref[...]
        before = before_ref[...]
        q, dout, tt = q_ref[...], do_ref[...], tot_ref[...]
        n = nb_ref[p * 2 // SB_FWD_HEADS, i]

        def step(it, c):
            pres, gpres, dqs = c
            mask, ws = _sb_window(i, n - 1 - it)
            kw, vw = k_ref[pl.ds(ws, SB_WIN), :], v_ref[pl.ds(ws, SB_WIN), :]
            new_p, new_g, new_dq, dks, dvs = [], [], [], [], []
            for hh in range(2):
                sl = slice(HD * hh, HD * hh + HD)
                z = _dot(q[:, sl], kw[:, sl], NT) * SCALE
                sp = _softplus(z)
                lf = jnp.where(mask, -sp, 0.0)
                later = tt[:, HD * hh:HD * hh + 1] - (pres[hh] + _dot01(lf, upto))
                w = jnp.where(mask, jnp.exp(z - sp + later), 0.0)
                beta = jnp.exp(z - sp)
                g = _dot(dout[:, sl], vw[:, sl], NT) * w
                g_far = gpres[hh] + _dot(g.astype(BF16), before)
                dz = (jnp.where(mask, g * (1.0 - beta) - beta * g_far, 0.0) * SCALE).astype(BF16)
                new_dq.append(dqs[hh] + _dot(dz, kw[:, sl]))
                dks.append(_dot(dz, q[:, sl], TN))
                dvs.append(_dot(w.astype(BF16), dout[:, sl], TN))
                new_p.append(pres[hh] + jnp.sum(lf, axis=1, keepdims=True))
                new_g.append(gpres[hh] + jnp.sum(g, axis=1, keepdims=True))
            dk_ref[pl.ds(ws, SB_WIN), :] += jnp.concatenate(dks, axis=1)
            dv_ref[pl.ds(ws, SB_WIN), :] += jnp.concatenate(dvs, axis=1)
            return tuple(new_p), tuple(new_g), tuple(new_dq)

        zt, zo = jnp.zeros((QB, 1), F32), jnp.zeros((QB, HD), F32)
        _, _, dqs = lax.fori_loop(0, n, step, ((zt, zt), (zt, zt), (zo, zo)))
        dq_ref[...] = jnp.concatenate(dqs, axis=1)

    whole_in = lambda off: pl.BlockSpec((s, 128), lambda p, i: (0, off + p), pipeline_mode=pl.Buffered(1))
    whole_out = pl.BlockSpec((s, 128), lambda p, i: (0, p), pipeline_mode=pl.Buffered(1))
    tile = pl.BlockSpec((QB, 128), lambda p, i: (i, p))
    tri = pl.BlockSpec((SB_WIN, SB_WIN), lambda p, i: (0, 0), pipeline_mode=pl.Buffered(1))
    idx = np.arange(SB_WIN)
    return pl.pallas_call(
        body, name="sb_bwd", grid=(npairs, nq),
        in_specs=[pl.BlockSpec(memory_space=pltpu.SMEM), tile, whole_in(npairs), whole_in(2 * npairs), tile, tile, tri, tri],
        out_specs=[tile, whole_out, whole_out],
        out_shape=[SDS((s, SB_W), F32)] * 3,
        compiler_params=pltpu.CompilerParams(dimension_semantics=("arbitrary", "arbitrary"), vmem_limit_bytes=VMEM_LIMIT),
    )(nblk, qkv, qkv, qkv, do, tot, jnp.asarray(idx[:, None] <= idx[None, :], BF16), jnp.asarray(idx[:, None] < idx[None, :], BF16))


def _win_masks(has_prev):
    row = lax.broadcasted_iota(jnp.int32, (QB, QB), 0)
    col = lax.broadcasted_iota(jnp.int32, (QB, QB), 1)
    return col <= row, (col >= row) & has_prev


def _dsa_fwd(q, k, v, dil):
    n = q.shape[0]
    nb = n // QB

    def body(q_ref, kc_ref, kp_ref, vc_ref, vp_ref, o_ref, lse_ref):
        m_cur, m_prev = _win_masks(pl.program_id(1) > 0)
        outs, lses = [], []
        for hh in range(DSA_OUT_W // HD):
            sl = slice(HD * hh, HD * hh + HD)
            qh = q_ref[:, sl]
            sc = jnp.where(m_cur, _dot(qh, kc_ref[:, sl], NT) * SCALE, NEG)
            sp = jnp.where(m_prev, _dot(qh, kp_ref[:, sl], NT) * SCALE, NEG)
            m = jnp.maximum(jnp.max(sc, axis=1, keepdims=True), jnp.max(sp, axis=1, keepdims=True))
            pc, pp = jnp.exp(sc - m), jnp.exp(sp - m)
            den = jnp.sum(pc, axis=1, keepdims=True) + jnp.sum(pp, axis=1, keepdims=True)
            outs.append((_dot(pc.astype(BF16), vc_ref[:, sl]) + _dot(pp.astype(BF16), vp_ref[:, sl])) / den)
            lses.append(m + jnp.log(den))
        o_ref[...] = jnp.concatenate(outs, axis=1)
        lse_ref[...] = _bcast_heads(lses)

    cur = pl.BlockSpec((QB, DSA_OUT_W), lambda c, i: (i, c))
    prev = pl.BlockSpec((QB, DSA_OUT_W), lambda c, i: (jnp.maximum(i - 1, 0), c))
    o, lse = pl.pallas_call(
        body, name=f"dsa_fwd_d{dil}", grid=(dil, nb), in_specs=[cur, cur, prev, cur, prev], out_specs=[cur, cur],
        out_shape=[SDS((n, dil * DSA_OUT_W), F32)] * 2,
        compiler_params=pltpu.CompilerParams(dimension_semantics=("parallel", "parallel")),
    )(q, k, k, v, v)
    return o, lse


def _dsa_bwd(q, k, v, do, cc, lse, dil):
    n = q.shape[0]
    nb = n // QB

    def body(qj_ref, qn_ref, kp_ref, kj_ref, vp_ref, vj_ref, doj_ref, don_ref, cj_ref, cn_ref, lj_ref, ln_ref,
             dq_ref, dk_ref, dv_ref):
        j = pl.program_id(1)
        m_cur, m_prev = _win_masks(j > 0)
        _, m_next = _win_masks(j + 1 < nb)
        dqs, dks, dvs = [], [], []
        for hh in range(DSA_OUT_W // HD):
            sl = slice(HD * hh, HD * hh + HD)
            one = slice(HD * hh, HD * hh + 1)
            qj, qn, kp, kj, vp, vj = (r[:, sl] for r in (qj_ref, qn_ref, kp_ref, kj_ref, vp_ref, vj_ref))
            doj, don = doj_ref[:, sl], don_ref[:, sl]

            def dscore(qq, kk, vv, dd, c_ref, l_ref, mask):
                prob = jnp.where(mask, jnp.exp(_dot(qq, kk, NT) * SCALE - l_ref[:, one]), 0.0)
                return prob, (prob * (_dot(dd, vv, NT) + c_ref[:, one]) * SCALE).astype(BF16)

            _, ds_a = dscore(qj, kp, vp, doj, cj_ref, lj_ref, m_prev)
            p_b, ds_b = dscore(qj, kj, vj, doj, cj_ref, lj_ref, m_cur)
            p_c, ds_c = dscore(qn, kj, vj, don, cn_ref, ln_ref, m_next)
            dqs.append(_dot(ds_a, kp) + _dot(ds_b, kj))
            dks.append(_dot(ds_b, qj, TN) + _dot(ds_c, qn, TN))
            dvs.append(_dot(p_b.astype(BF16), doj, TN) + _dot(p_c.astype(BF16), don, TN))
        dq_ref[...] = jnp.concatenate(dqs, axis=1)
        dk_ref[...] = jnp.concatenate(dks, axis=1)
        dv_ref[...] = jnp.concatenate(dvs, axis=1)

    cur = pl.BlockSpec((QB, DSA_OUT_W), lambda c, j: (j, c))
    prev = pl.BlockSpec((QB, DSA_OUT_W), lambda c, j: (jnp.maximum(j - 1, 0), c))
    nxt = pl.BlockSpec((QB, DSA_OUT_W), lambda c, j: (jnp.minimum(j + 1, nb - 1), c))
    dq, dk, dv = pl.pallas_call(
        body, name=f"dsa_bwd_d{dil}", grid=(dil, nb),
        in_specs=[cur, nxt, prev, cur, prev, cur, cur, nxt, cur, nxt, cur, nxt], out_specs=[cur, cur, cur],
        out_shape=[SDS((n, dil * DSA_OUT_W), F32)] * 3,
        compiler_params=pltpu.CompilerParams(dimension_semantics=("parallel", "parallel")),
    )(q, q, k, k, v, v, do, do, cc, cc, lse, lse)
    return dq, dk, dv


def _ffn_fwd(tag, x, gain, w13, w2, plan=None):
    n = _tokmap(f"{tag}_norm", lambda xv, g: _rms_fwd(xv, g, _mean_all), [x], [gain], [(D, BF16)])[0]
    ab = _matmul(f"{tag}_up", n, w13, NN, BF16, plan=plan)

    def gate(abv):
        a, b = abv[:, :D_FF].astype(F32), abv[:, D_FF:].astype(F32)
        return a * jax.nn.sigmoid(a) * b

    h = _tokmap(f"{tag}_gate", gate, [ab], [], [(D_FF, BF16)], tile=256)[0]
    y = _matmul(f"{tag}_down", h, w2(), NN, F32, epi=lambda acc, res: res + 0.5 * acc, tiles=[x], plan=plan)
    return y, (n, ab, h)


def _ffn_bwd(tag, x, gain, w13, w2, saved, dy, plan=None, on_dw=None):
    n, ab, h = saved
    dh = _matmul(f"{tag}_bwd_dh", dy, w2, NT, BF16, epi=lambda acc: 0.5 * acc)

    def gate_bwd(abv, dhv):
        a, b, dhf = abv[:, :D_FF].astype(F32), abv[:, D_FF:].astype(F32), dhv.astype(F32)
        sg = jax.nn.sigmoid(a)
        da = dhf * b * (sg * (1.0 + a * (1.0 - sg)))
        return jnp.concatenate([da, dhf * (a * sg)], axis=1)

    dab = _tokmap(f"{tag}_bwd_gate", gate_bwd, [ab, dh], [], [(2 * D_FF, BF16)], tile=256)[0]
    dw2 = _matmul(f"{tag}_bwd_dw2", h, dy, TN, F32, epi=lambda acc: 0.5 * acc)
    dw13 = _matmul(f"{tag}_bwd_dw13", n, dab, TN, F32, plan=plan)
    if on_dw is not None:
        on_dw(dw13, dw2)
    dn = _matmul(f"{tag}_bwd_dn", dab, w13, NT, F32, plan=plan)

    def norm_bwd(xv, dnv, dyv, g):
        dx, dg = _rms_bwd(xv, g, dnv, _mean_all)
        return dx + dyv, dg

    dx, dgain = _tokmap(f"{tag}_bwd_norm", norm_bwd, [x, dn, dy], [gain], [(D, F32)], [(1, D)])
    return dx, dgain, dw13, dw2


def _rope_tables(s):
    half = HD // 2
    inv_freq = jnp.power(10000.0, -jnp.arange(half, dtype=F32) / half)
    ang = jnp.arange(s).astype(F32)[:, None] * inv_freq[None, :]
    cos, sin = jnp.cos(ang), jnp.sin(ang)
    return jnp.tile(jnp.concatenate([cos, cos], axis=1), (1, 2)), jnp.tile(jnp.concatenate([-sin, sin], axis=1), (1, 2))


def _local_step(x, mem, tgt, w, sm, plan=None, on_grads=None):
    s = x.shape[0]
    assert s % (QB * max(DSA_DILS)) == 0
    on_grads = on_grads or (lambda group, grads: None)
    c_sb, c_dsa, c_qm = 3 * SB_W, 3 * SB_W + 3 * DSA_W, 4096
    cos, sin = _rope_tables(s)
    bd768 = bd256 = _block_diag(128)
    gq_dsa, gk_dsa = jnp.tile(sm["qn_dsa"], (1, DSA_W // HD)), jnp.tile(sm["kn_dsa"], (1, DSA_W // HD))
    gq_mem, gk_mem = jnp.tile(sm["qn_mem"], (1, MEM_W // HD)), jnp.tile(sm["kn_mem"], (1, MEM_W // HD))

    w13_1 = jnp.concatenate([w["ffn1_w1"], w["ffn1_w3"]], axis=1)
    x1, ffn1_saved = _ffn_fwd("ffn1", x, sm["ffn1_norm"], w13_1, lambda: w["ffn1_w2"], plan)
    w_all = jnp.concatenate([w["w_in"], w["w_gate"]], axis=1)
    wb_sb, wb_dsa, wb_mem = w["w_branch_sb"], w["w_branch_dsa"], w["w_branch_mem"]
    hmix = _tokmap("mix_norm", lambda xv, g: _rms_fwd(xv, g, _mean_all), [x1], [sm["mix_norm"]], [(D, BF16)])[0]
    qkv_sb = _matmul("proj_sb", hmix, w_all[:, :c_sb], NN, BF16)
    qkv_dsa = _matmul("proj_dsa", hmix, w_all[:, c_sb:c_dsa], NN, BF16, plan=plan)
    q_mem = _matmul("proj_qmem", hmix, w_all[:, c_dsa:c_qm], NN, BF16)
    gpre = _matmul("proj_gate", hmix, w_all[:, c_qm:], NN, BF16, epi=lambda acc, b: acc + b, rows=[sm["b_gate"]], plan=plan)

    o_sb, sb_tot, sb_nblk = _sb_fwd(qkv_sb)

    def dsa_prep(qkv, cs, sn, gq, gk, bd):
        mean = _mean_heads(bd)
        qn = _rope_fwd(_rms_fwd(qkv[:, :DSA_W].astype(F32), gq, mean), cs, sn)
        kn = _rope_fwd(_rms_fwd(qkv[:, DSA_W:2 * DSA_W].astype(F32), gk, mean), cs, sn)
        v = qkv[:, 2 * DSA_W:]
        outs = []
        for t in (qn, kn, v):
            outs += [t[:, DSA_OUT_W * g:DSA_OUT_W * (g + 1)] for g in range(3)]
        return outs

    dsa_in = _tokmap("dsa_prep", dsa_prep, [qkv_dsa, cos, sin], [gq_dsa, gk_dsa, bd768], [(DSA_OUT_W, BF16)] * 9, tile=256,
                     dil_outs={j: DSA_DILS[j % 3] for j in range(9)})
    dsa_q, dsa_k, dsa_v = dsa_in[0:3], dsa_in[3:6], dsa_in[6:9]
    dsa_o, dsa_lse = zip(*[_dsa_fwd(dsa_q[g], dsa_k[g], dsa_v[g], DSA_DILS[g]) for g in range(3)])

    def alphas(l0, l1, l2):
        m = jnp.maximum(jnp.maximum(l0, l1), l2)
        e = [jnp.exp(l - m) for l in (l0, l1, l2)]
        tot = e[0] + e[1] + e[2]
        return [t / tot for t in e]

    def dsa_mix(o0, o1, o2, l0, l1, l2):
        a = alphas(l0, l1, l2)
        return a[0] * o0 + a[1] * o1 + a[2] * o2

    o_dsa = _tokmap("dsa_mix", dsa_mix, [*dsa_o, *dsa_lse], [], [(DSA_OUT_W, BF16)], tile=256,
                    dil_ins={j: DSA_DILS[j % 3] for j in range(6)})[0]

    def mem_kv(memv, g, wkv, gk, bd):
        kv = _dot(_rms_fwd(memv, g, _mean_all).astype(BF16), wkv)
        return _rms_fwd(kv[:, :MEM_W], gk, _mean_heads(bd)), kv[:, MEM_W:]

    km, vm = _tokmap("mem_kv", mem_kv, [mem], [sm["mem_norm"], w["w_mem_kv"], gk_mem, bd256], [(MEM_W, BF16)] * 2)

    def mem_probs(qv, kmv, gq, bd):
        qn = _rms_fwd(qv.astype(F32), gq, _mean_heads(bd)).astype(BF16)
        ps = []
        for h in range(MEM_W // HD):
            sl = slice(HD * h, HD * h + HD)
            sc = _dot(qn[:, sl], kmv[:, sl], NT) * SCALE
            e = jnp.exp(sc - jnp.max(sc, axis=1, keepdims=True))
            ps.append(e / jnp.sum(e, axis=1, keepdims=True))
        return qn, ps

    def mem_attn(qv, kmv, vmv, gq, bd):
        _, ps = mem_probs(qv, kmv, gq, bd)
        return jnp.concatenate([_dot(p.astype(BF16), vmv[:, HD * h:HD * h + HD]) for h, p in enumerate(ps)], axis=1)

    o_mem = _tokmap("mem_attn", mem_attn, [q_mem], [km, vm, gq_mem, bd256], [(MEM_W, BF16)])[0]

    def merge(osb, odsa, omem, gp, w_sb, w_dsa, w_mem):
        gates = jax.nn.sigmoid(gp.astype(F32))
        ys = (_dot(osb, w_sb), _dot(odsa, w_dsa), _dot(omem, w_mem))
        return gates, ys, gates[:, :D] * ys[0] + gates[:, D:2 * D] * ys[1] + gates[:, 2 * D:] * ys[2]

    merged = _tokmap("merge", lambda *a: merge(*a)[2], [o_sb, o_dsa, o_mem, gpre], [wb_sb, wb_dsa, wb_mem], [(D, BF16)],
                     tile=256)[0]
    x2 = _matmul("out_proj", merged, w["w_out"], NN, F32, epi=lambda acc, res: res + acc, tiles=[x1])
    w13_2 = jnp.concatenate([w["ffn2_w1"], w["ffn2_w3"]], axis=1)
    y, ffn2_saved = _ffn_fwd("ffn2", x2, sm["ffn2_norm"], w13_2, lambda: w["ffn2_w2"])

    def loss_fn(yv, tv):
        e = yv - tv
        part = 0.5 * jnp.sum(jnp.mean(e * e, axis=1, keepdims=True), axis=0, keepdims=True)
        return e * (1.0 / D), jnp.broadcast_to(part, (1, 128))

    dy, loss = _tokmap("loss", loss_fn, [y, tgt], [], [(D, F32)], [(1, 128)])

    gw, gs = {}, {}
    def ffn_grads(tag):
        def on_dw(dw13, dw2):
            gw[f"{tag}_w1"], gw[f"{tag}_w3"], gw[f"{tag}_w2"] = dw13[:, :D_FF], dw13[:, D_FF:], dw2
            on_grads(tag, {n: gw[n] for n in (f"{tag}_w1", f"{tag}_w3", f"{tag}_w2")})
        return on_dw

    dx2, gs["ffn2_norm"], _, _ = _ffn_bwd("ffn2", x2, sm["ffn2_norm"], w13_2, w["ffn2_w2"], ffn2_saved, dy, plan,
                                          ffn_grads("ffn2"))
    dmerged = _matmul("out_proj_bwd_dx", dx2, w["w_out"], NT, BF16)
    gw["w_out"] = _matmul("out_proj_bwd_dw", merged, dx2, TN, F32)

    def merge_bwd(osb, odsa, omem, gp, dm, w_sb, w_dsa, w_mem):
        gates, ys, _ = merge(osb, odsa, omem, gp, w_sb, w_dsa, w_mem)
        dmf = dm.astype(F32)
        dgp, dos, dws = [], [], []
        for b, (ov, wv) in enumerate(((osb, w_sb), (odsa, w_dsa), (omem, w_mem))):
            gb = gates[:, D * b:D * (b + 1)]
            dgp.append(dmf * ys[b] * gb * (1.0 - gb))
            dyb = (dmf * gb).astype(BF16)
            dos.append(_dot(dyb, wv, NT))
            dws.append(_dot(ov, dyb, TN))
        dgp = jnp.concatenate(dgp, axis=1)
        return dos[0], dos[1], dos[2], dgp, dws[0], dws[1], dws[2], jnp.sum(dgp, axis=0, keepdims=True)

    do_sb, do_dsa, do_mem, dgpre, gw["w_branch_sb"], gw["w_branch_dsa"], gw["w_branch_mem"], gs["b_gate"] = _tokmap(
        "merge_bwd", merge_bwd, [o_sb, o_dsa, o_mem, gpre, dmerged], [wb_sb, wb_dsa, wb_mem],
        [(SB_W, BF16), (DSA_OUT_W, F32), (MEM_W, BF16), (3 * D, BF16)],
        [(SB_W, D), (DSA_OUT_W, D), (MEM_W, D), (1, 3 * D)], tile=256)

    dq_sb, dk_sb, dv_sb = _sb_bwd(qkv_sb, do_sb, sb_tot, sb_nblk)

    def dsa_mix_bwd(o0, o1, o2, l0, l1, l2, dov, bd):
        a = alphas(l0, l1, l2)
        omix = a[0] * o0 + a[1] * o1 + a[2] * o2
        dot_o = _head_sums(dov * omix, bd)
        return [dov * t for t in a] + [-t * dot_o for t in a]

    mixb = _tokmap("dsa_mix_bwd", dsa_mix_bwd, [*dsa_o, *dsa_lse, do_dsa], [bd256],
                   [(DSA_OUT_W, BF16)] * 3 + [(DSA_OUT_W, F32)] * 3, tile=256,
                   dil_ins={j: DSA_DILS[j % 3] for j in range(6)}, dil_outs={j: DSA_DILS[j % 3] for j in range(6)})
    dsa_d = [_dsa_bwd(dsa_q[g], dsa_k[g], dsa_v[g], mixb[g], mixb[3 + g], dsa_lse[g], DSA_DILS[g]) for g in range(3)]

    def dsa_prep_bwd(qkv, cs, sn, *rest):
        dqs, dks, dvs, (gq, gk, bd) = rest[0:3], rest[3:6], rest[6:9], rest[9:]
        mean = _mean_heads(bd)
        dq, dgq = _rms_bwd(qkv[:, :DSA_W].astype(F32), gq, _rope_bwd(jnp.concatenate(dqs, axis=1), cs, sn), mean)
        dk, dgk = _rms_bwd(qkv[:, DSA_W:2 * DSA_W].astype(F32), gk, _rope_bwd(jnp.concatenate(dks, axis=1), cs, sn), mean)
        return jnp.concatenate([dq, dk] + list(dvs), axis=1), dgq, dgk

    dqkv_dsa, dgq_dsa, dgk_dsa = _tokmap(
        "dsa_prep_bwd", dsa_prep_bwd,
        [qkv_dsa, cos, sin] + [dsa_d[g][t] for t in range(3) for g in range(3)], [gq_dsa, gk_dsa, bd768],
        [(3 * DSA_W, BF16)], [(1, DSA_W), (1, DSA_W)], tile=256, dil_ins={3 + j: DSA_DILS[j % 3] for j in range(9)})
    gs["qn_dsa"] = dgq_dsa.reshape(DSA_W // HD, HD).sum(axis=0, keepdims=True)
    gs["kn_dsa"] = dgk_dsa.reshape(DSA_W // HD, HD).sum(axis=0, keepdims=True)

    def mem_attn_bwd(qv, dov, kmv, vmv, gq, bd):
        qn, ps = mem_probs(qv, kmv, gq, bd)
        dqn, dkm, dvm = [], [], []
        for h, p in enumerate(ps):
            sl = slice(HD * h, HD * h + HD)
            dp = _dot(dov[:, sl], vmv[:, sl], NT)
            ds = (p * (dp - jnp.sum(p * dp, axis=1, keepdims=True)) * SCALE).astype(BF16)
            dqn.append(_dot(ds, kmv[:, sl]))
            dkm.append(_dot(ds, qn[:, sl], TN))
            dvm.append(_dot(p.astype(BF16), dov[:, sl], TN))
        dq, dgq = _rms_bwd(qv.astype(F32), gq, jnp.concatenate(dqn, axis=1), _mean_heads(bd))
        return dq, jnp.concatenate(dkm, axis=1), jnp.concatenate(dvm, axis=1), dgq

    dq_mem, dkm, dvm, dgq_mem = _tokmap("mem_attn_bwd", mem_attn_bwd, [q_mem, do_mem], [km, vm, gq_mem, bd256],
                                        [(MEM_W, BF16)], [(MEM_LEN, MEM_W), (MEM_LEN, MEM_W), (1, MEM_W)])
    gs["qn_mem"] = dgq_mem.reshape(MEM_W // HD, HD).sum(axis=0, keepdims=True)

    def mem_kv_bwd(memv, dkmv, dvmv, g, wkv, gk, bd):
        memn = _rms_fwd(memv, g, _mean_all).astype(BF16)
        kv = _dot(memn, wkv)
        dk, dgk = _rms_bwd(kv[:, :MEM_W], gk, dkmv, _mean_heads(bd))
        dkv = jnp.concatenate([dk, dvmv], axis=1).astype(BF16)
        _, dg = _rms_bwd(memv, g, _dot(dkv, wkv, NT), _mean_all)
        return _dot(memn, dkv, TN), dg, dgk

    gw["w_mem_kv"], gs["mem_norm"], dgk_mem = _tokmap(
        "mem_kv_bwd", mem_kv_bwd, [mem, dkm, dvm], [sm["mem_norm"], w["w_mem_kv"], gk_mem, bd256], [],
        [(D, 2 * MEM_W), (1, D), (1, MEM_W)])
    gs["kn_mem"] = dgk_mem.reshape(MEM_W // HD, HD).sum(axis=0, keepdims=True)

    dall = jnp.concatenate([dq_sb.astype(BF16), dk_sb.astype(BF16), dv_sb.astype(BF16), dqkv_dsa, dq_mem, dgpre], axis=1)
    dhmix = _matmul("proj_bwd_dx", dall, w_all, NT, F32)
    dw_all = _matmul("proj_bwd_dw", hmix, dall, TN, F32)
    gw["w_in"], gw["w_gate"] = dw_all[:, :c_qm], dw_all[:, c_qm:]
    on_grads("mid", {n: gw[n] for n in GROUPS["mid"]})

    def mix_norm_bwd(xv, dnv, dyv, g):
        dx, dg = _rms_bwd(xv, g, dnv, _mean_all)
        return dx + dyv, dg

    dx1, gs["mix_norm"] = _tokmap("mix_norm_bwd", mix_norm_bwd, [x1, dhmix, dx2], [sm["mix_norm"]], [(D, F32)], [(1, D)])
    gx, gs["ffn1_norm"], _, _ = _ffn_bwd("ffn1", x, sm["ffn1_norm"], w13_1, w["ffn1_w2"], ffn1_saved, dx1, plan,
                                         ffn_grads("ffn1"))
    return loss, gx, gw, gs


def _shard_shape(name):
    shape, axis = SHARDED_BY_NAME[name]
    return (shape[0] // N_CHIPS, shape[1]) if axis == 0 else (shape[0], shape[1] // N_CHIPS)


def _full_from_shards(name, shards):
    axis = SHARDED_BY_NAME[name][1]
    return shards.reshape(SHARDED_BY_NAME[name][0]) if axis == 0 else jnp.concatenate(list(shards), axis=1)


def _shards_from_full(name, full, dtype):
    axis, n = SHARDED_BY_NAME[name][1], _shard_shape(name)
    return jnp.stack([lax.slice_in_dim(full, c * n[axis], (c + 1) * n[axis], axis=axis).astype(dtype) for c in range(N_CHIPS)])


def _own_shard(name, full, chip):
    axis, n = SHARDED_BY_NAME[name][1], _shard_shape(name)
    return lax.dynamic_slice_in_dim(full, chip * n[axis], n[axis], axis=axis)


SMALL_USED = sum(n for _, n in SMALL)


def _pack_small(d, loss=None):
    parts = [d[n].reshape(-1) for n, _ in SMALL]
    parts.append(jnp.zeros((1,), F32) if loss is None else loss.reshape(1))
    parts.append(jnp.zeros((SMALL_ROWS * D - SMALL_USED - 1,), F32))
    return jnp.concatenate(parts).reshape(SMALL_ROWS, D)


def _unpack_small(v):
    flat, out, r = v.reshape(-1), {}, 0
    for n, k in SMALL:
        out[n] = flat[r:r + k]
        r += k
    return out, flat[r]


def _place():
    return lax.axis_index("x"), lax.axis_index("y"), lax.axis_index("c")


def _other_chips(x, y):
    return [(1 - x, y), (x, 1 - y), (1 - x, 1 - y)]


HBM_SPEC = pl.BlockSpec(memory_space=pl.ANY)


def _chip_sems(n):
    return (pltpu.SemaphoreType.DMA((3 * n,)), pltpu.SemaphoreType.DMA((3 * n,)), pltpu.SemaphoreType.DMA((n,)))


def _gather_copies(ins, outs, send_sems, recv_sems, local_sems):
    x, y, c = _place()
    me = 2 * x + y
    copies = []
    for a, (src, out) in enumerate(zip(ins, outs)):
        copies.append(pltpu.make_async_copy(src, out.at[me], local_sems.at[a]))
        copies += [pltpu.make_async_remote_copy(src_ref=src, dst_ref=out.at[me], send_sem=send_sems.at[3 * a + k],
                                                recv_sem=recv_sems.at[3 * a + k], device_id=(px, py, c), device_id_type=MESH)
                   for k, (px, py) in enumerate(_other_chips(x, y))]
    return copies


def _scatter_copies(ins, outs, send_sems, recv_sems, local_sems):
    x, y, c = _place()
    return [pltpu.make_async_remote_copy(src_ref=src.at[2 * px + py], dst_ref=out.at[k], send_sem=send_sems.at[3 * a + k],
                                         recv_sem=recv_sems.at[3 * a + k], device_id=(px, py, c), device_id_type=MESH)
            for a, (src, out) in enumerate(zip(ins, outs)) for k, (px, py) in enumerate(_other_chips(x, y))]


def _all_gather_chips(arrays):
    n = len(arrays)

    def body(*refs):
        copies = _gather_copies(refs[:n], refs[n:2 * n], *refs[2 * n:])
        for cp in copies:
            cp.start()
        for cp in copies:
            cp.wait()

    return pl.pallas_call(
        body, name="weights_all_gather", in_specs=[HBM_SPEC] * n, out_specs=[HBM_SPEC] * n,
        out_shape=[SDS((N_CHIPS,) + a.shape, a.dtype) for a in arrays], scratch_shapes=list(_chip_sems(n)),
    )(*arrays)


def _swap_with_sibling(name, arrays):
    n = len(arrays)

    def body(*refs):
        x, y, c = _place()
        send_sems, recv_sems = refs[2 * n:]
        copies = [pltpu.make_async_remote_copy(src_ref=refs[a], dst_ref=refs[n + a], send_sem=send_sems.at[a],
                                               recv_sem=recv_sems.at[a], device_id=(x, y, 1 - c), device_id_type=MESH)
                  for a in range(n)]
        for cp in copies:
            cp.start()
        for cp in copies:
            cp.wait()

    return pl.pallas_call(
        body, name=name, in_specs=[HBM_SPEC] * n, out_specs=[HBM_SPEC] * n, out_shape=[SDS(a.shape, a.dtype) for a in arrays],
        scratch_shapes=[pltpu.SemaphoreType.DMA((n,)), pltpu.SemaphoreType.DMA((n,))],
    )(*arrays)


def _all_reduce_small(v):
    n_dev = 8

    def body(v_ref, out_ref, land, send_sems, recv_sems):
        x, y, c = _place()
        me = 4 * x + 2 * y + c
        land[me] = v_ref[...]
        copies = []
        for k in range(1, n_dev):
            peer = (x ^ (k >> 2), y ^ ((k >> 1) & 1), c ^ (k & 1))
            copies.append(pltpu.make_async_remote_copy(src_ref=v_ref, dst_ref=land.at[me], send_sem=send_sems.at[k - 1],
                                                       recv_sem=recv_sems.at[k - 1], device_id=peer, device_id_type=MESH))
        for cp in copies:
            cp.start()
        for cp in copies:
            cp.wait()
        acc = land[0]
        for d in range(1, n_dev):
            acc = acc + land[d]
        out_ref[...] = acc

    return pl.pallas_call(
        body, name="small_all_reduce", in_specs=[pl.BlockSpec(memory_space=pltpu.VMEM)],
        out_specs=pl.BlockSpec(memory_space=pltpu.VMEM), out_shape=SDS(v.shape, v.dtype),
        scratch_shapes=[pltpu.VMEM((n_dev,) + v.shape, v.dtype), pltpu.SemaphoreType.DMA((n_dev - 1,)),
                        pltpu.SemaphoreType.DMA((n_dev - 1,))],
    )(v)


def _adamw(g, wv, m, v):
    m = ADAM_B1 * m + (1.0 - ADAM_B1) * g
    v = ADAM_B2 * v + (1.0 - ADAM_B2) * (g * g)
    m_hat = m / (1.0 - ADAM_B1 ** ADAM_STEP)
    v_hat = v / (1.0 - ADAM_B2 ** ADAM_STEP)
    delta = -ADAM_LR * (m_hat / (jnp.sqrt(v_hat) + ADAM_EPS) + ADAM_WD * wv)
    return delta, m, v


def kernel(x, mem, ffn1_norm, ffn1_w1, ffn1_w3, ffn1_w2, mix_norm, mem_norm, w_in, w_mem_kv, qn_dsa, kn_dsa, qn_mem, kn_mem, w_branch_sb, w_branch_dsa, w_branch_mem, w_gate, b_gate, w_out, ffn2_norm, ffn2_w1, ffn2_w3, ffn2_w2, loss_target, m_ffn1_norm, m_ffn1_w1, m_ffn1_w3, m_ffn1_w2, m_mix_norm, m_mem_norm, m_w_in, m_w_mem_kv, m_qn_dsa, m_kn_dsa, m_qn_mem, m_kn_mem, m_w_branch_sb, m_w_branch_dsa, m_w_branch_mem, m_w_gate, m_b_gate, m_w_out, m_ffn2_norm, m_ffn2_w1, m_ffn2_w3, m_ffn2_w2, v_ffn1_norm, v_ffn1_w1, v_ffn1_w3, v_ffn1_w2, v_mix_norm, v_mem_norm, v_w_in, v_w_mem_kv, v_qn_dsa, v_kn_dsa, v_qn_mem, v_kn_mem, v_w_branch_sb, v_w_branch_dsa, v_w_branch_mem, v_w_gate, v_b_gate, v_w_out, v_ffn2_norm, v_ffn2_w1, v_ffn2_w3, v_ffn2_w2):
    given = dict(locals())
    wts = {n: given[n][0] for n in WEIGHTS}
    moms = {n: given["m_" + n][0] for n in WEIGHTS}
    vars_ = {n: given["v_" + n][0] for n in WEIGHTS}

    plan = _Plan()
    x_i, y_i, _ = _place()
    my_chip = 2 * x_i + y_i

    full = {}

    def gathered(names):
        return lambda res: full.update({n: _full_from_shards(n, g) for n, g in zip(names, res)})

    for host, names in WEIGHT_PIECES:
        shards = [wts[n].astype(BF16) for n in names]
        if host is None:
            gathered(names)(_all_gather_chips(shards))
        else:
            plan.put(host, _Carry(shards, [SDS((N_CHIPS,) + a.shape, BF16) for a in shards], _chip_sems(len(names)),
                                  _gather_copies, gathered(names)))
    small = {n: wts[n].reshape(1, -1) for n, _ in SMALL}

    landed = {}

    def on_grads(group, grads):
        names = GROUPS[group]
        slices = [_shards_from_full(n, grads[n], BF16) for n in names]
        own = [_own_shard(n, grads[n], my_chip) for n in names]
        plan.put(GRAD_HOSTS[group], _Carry(slices, [SDS((3,) + a.shape[1:], BF16) for a in slices], _chip_sems(len(names)),
                                           _scatter_copies, lambda res: landed.update({group: (own, res)})))

    loss, gx, _, gs = _local_step(x[0], mem[0], loss_target[0], full, small, plan, on_grads)
    assert not plan.pending, list(plan.pending)

    def update(hv, ov, wv, mv, vv):
        g = hv + ov
        return (g,) + _adamw(g, wv, mv, vv)

    outs = [{}, {}, {}, {}]
    for group, names in GROUPS.items():
        own, got = landed[group]
        halves = [_tokmap(f"grads_sum_chips_{n}",
                          lambda a, b0, b1, b2: ((a + b0.astype(F32)) + b1.astype(F32)) + b2.astype(F32),
                          [o, g[0], g[1], g[2]], [], [(o.shape[1], F32)])[0] for n, o, g in zip(names, own, got)]
        others = _swap_with_sibling(f"grads_swap_cores_{group}", halves)
        for n, half, other in zip(names, halves, others):
            res = _tokmap(f"adamw_{n}", update, [half, other, wts[n], moms[n], vars_[n]], [], [(half.shape[1], F32)] * 4)
            for d, r in zip(outs, res):
                d[n] = r

    s_red = _all_reduce_small(_pack_small(gs, loss[0, 0]))
    res = _tokmap(
        "adamw_small", lambda g, wv, mv, vv: (g,) + _adamw(g, wv, mv, vv),
        [s_red, _pack_small(small), _pack_small({n: moms[n] for n, _ in SMALL}), _pack_small({n: vars_[n] for n, _ in SMALL})],
        [], [(D, F32)] * 4)
    for d, packed in zip(outs, res):
        d.update(_unpack_small(packed)[0])
    _, total_loss = _unpack_small(s_red)
    return (total_loss, gx[None], *[d[n][None] for d in outs for n in WEIGHTS])
```

```python
import functools

import numpy as np
import jax
import jax.numpy as jnp
from jax import lax
from jax.experimental import pallas as pl
from jax.experimental.pallas import tpu as pltpu

F32, BF16 = jnp.float32, jnp.bfloat16
SDS = jax.ShapeDtypeStruct
MESH = pl.DeviceIdType.MESH

D = 1024
HD = 64
QB = 128
DSA_T = 256
D_FF = 2816
SB_W, DSA_W, DSA_OUT_W, MEM_W = 512, 768, 256, 256
DSA_DILS = (1, 4, 16)
MEM_LEN = 256
N_CHIPS = 4
EPS = 1e-6
SCALE = HD ** -0.5
EXHAUSTED = -104.0
SB_FWD_HEADS = 4
SB_QB = 256
SB_WIN = 512
NEG = -1e30
VMEM_LIMIT = 56 * 1024 * 1024

ADAM_LR, ADAM_B1, ADAM_B2, ADAM_EPS, ADAM_WD, ADAM_STEP = 0.001, 0.9, 0.999, 1e-08, 0.01, 10

NN = (((1,), (0,)), ((), ()))
NT = (((1,), (1,)), ((), ()))
TN = (((0,), (0,)), ((), ()))

SHARDED = (
    ("ffn1_w1", (D, D_FF), 1), ("ffn1_w3", (D, D_FF), 1), ("ffn1_w2", (D_FF, D), 0),
    ("w_in", (D, 4096), 1), ("w_mem_kv", (D, 512), 0),
    ("w_branch_sb", (SB_W, D), 1), ("w_branch_dsa", (DSA_OUT_W, D), 1), ("w_branch_mem", (MEM_W, D), 1),
    ("w_gate", (D, 3 * D), 1), ("w_out", (D, D), 0),
    ("ffn2_w1", (D, D_FF), 1), ("ffn2_w3", (D, D_FF), 1), ("ffn2_w2", (D_FF, D), 0),
)
SHARDED_BY_NAME = {n: (sh, ax) for n, sh, ax in SHARDED}
GROUPS = {
    "ffn2": ("ffn2_w1", "ffn2_w3", "ffn2_w2"),
    "mid": ("w_in", "w_mem_kv", "w_branch_sb", "w_branch_dsa", "w_branch_mem", "w_gate", "w_out"),
    "ffn1": ("ffn1_w1", "ffn1_w3", "ffn1_w2"),
}
WEIGHT_PIECES = (
    (None, ("ffn1_w1", "ffn1_w3")),
    ("ffn1_up", ("ffn1_w2", "w_in")),
    ("ffn1_down", ("w_gate", "w_mem_kv", "w_branch_sb", "w_branch_dsa", "w_branch_mem", "w_out")),
    ("proj_dsa", ("ffn2_w2",)),
    ("proj_gate", ("ffn2_w1", "ffn2_w3")),
)
GRAD_HOSTS = {"ffn2": "ffn2_bwd_dn", "mid": "ffn1_bwd_dw13", "ffn1": "ffn1_bwd_dn"}
SMALL = (("ffn1_norm", D), ("mix_norm", D), ("mem_norm", D), ("ffn2_norm", D), ("b_gate", 3 * D),
         ("qn_dsa", HD), ("kn_dsa", HD), ("qn_mem", HD), ("kn_mem", HD))
WEIGHTS = ("ffn1_norm", "ffn1_w1", "ffn1_w3", "ffn1_w2", "mix_norm", "mem_norm", "w_in", "w_mem_kv", "qn_dsa", "kn_dsa",
           "qn_mem", "kn_mem", "w_branch_sb", "w_branch_dsa", "w_branch_mem", "w_gate", "b_gate", "w_out", "ffn2_norm",
           "ffn2_w1", "ffn2_w3", "ffn2_w2")
SMALL_ROWS = 8


def _dot(a, b, dn=NN):
    return lax.dot_general(a, b, dn, preferred_element_type=F32)


def _dot01(x, m01):
    hi = x.astype(BF16)
    r1 = x - hi.astype(F32)
    mid = r1.astype(BF16)
    lo = (r1 - mid.astype(F32)).astype(BF16)
    return _dot(hi, m01) + _dot(mid, m01) + _dot(lo, m01)


def _pick(n, cands):
    for c in cands:
        if n % c == 0:
            return c
    raise ValueError(f"no tile for {n}")


def _from_dilated(v, d, scr):
    w = v.shape[1] // d
    v = v.astype(F32)
    for c in range(d):
        for p, buf in enumerate(scr[:w // 128]):
            buf[pl.ds(c, v.shape[0], stride=d), :] = v[:, c * w + 128 * p:c * w + 128 * (p + 1)]
    return jnp.concatenate([buf[...] for buf in scr[:w // 128]], axis=1)


def _to_dilated(v, d, scr):
    w = v.shape[1]
    for p, buf in enumerate(scr[:w // 128]):
        buf[...] = v[:, 128 * p:128 * (p + 1)].astype(F32)
    return jnp.concatenate([buf[pl.ds(c, v.shape[0] // d, stride=d), :] for c in range(d) for buf in scr[:w // 128]], axis=1)


def _tokmap(name, fn, tok_ins, consts, tok_outs, acc_outs=(), tile=512, dil_ins=None, dil_outs=None):
    dil_ins, dil_outs = dil_ins or {}, dil_outs or {}
    n = tok_ins[0].shape[0] * dil_ins.get(0, 1)
    tile = _pick(n, [t for t in (512, 256, 128, 64, 32, 16, 8) if t <= tile])
    n_tin, n_in, n_tok, n_acc = len(tok_ins), len(tok_ins) + len(consts), len(tok_outs), len(acc_outs)
    n_scr = max([tok_ins[j].shape[1] // d // 128 for j, d in dil_ins.items() if d > 1]
                + [tok_outs[j][0] // 128 for j, d in dil_outs.items() if d > 1] + [0])

    def body(*refs):
        scr = refs[len(refs) - n_scr:]
        vals = [r[...] for r in refs[:n_in]]
        for j, d in dil_ins.items():
            if d > 1:
                vals[j] = _from_dilated(vals[j], d, scr)
        outs = fn(*vals)
        outs = list(outs) if isinstance(outs, (tuple, list)) else [outs]
        assert len(outs) == n_tok + n_acc, (name, len(outs))
        for j, d in dil_outs.items():
            if d > 1:
                outs[j] = _to_dilated(outs[j], d, scr)
        orefs = refs[n_in:]
        for r, v in zip(orefs[:n_tok], outs[:n_tok]):
            r[...] = v.astype(r.dtype)
        if n_acc:
            @pl.when(pl.program_id(0) == 0)
            def _():
                for r in orefs[n_tok:n_tok + n_acc]:
                    r[...] = jnp.zeros(r.shape, r.dtype)
            for r, v in zip(orefs[n_tok:n_tok + n_acc], outs[n_tok:]):
                r[...] += v.astype(r.dtype)

    def tok_spec(width, d):
        return pl.BlockSpec((tile // d, d * width), lambda i: (i, 0))

    in_specs = [tok_spec(a.shape[1] // dil_ins.get(j, 1), dil_ins.get(j, 1)) for j, a in enumerate(tok_ins)]
    in_specs += [pl.BlockSpec(c.shape, lambda i: (0, 0)) for c in consts]
    out_specs = [tok_spec(w, dil_outs.get(j, 1)) for j, (w, _) in enumerate(tok_outs)]
    out_specs += [pl.BlockSpec(s, lambda i: (0, 0)) for s in acc_outs]
    out_shape = [SDS((n // dil_outs.get(j, 1), w * dil_outs.get(j, 1)), dt) for j, (w, dt) in enumerate(tok_outs)]
    out_shape += [SDS(s, F32) for s in acc_outs]
    res = pl.pallas_call(
        body, name=name, grid=(n // tile,), in_specs=in_specs, out_specs=out_specs, out_shape=out_shape,
        scratch_shapes=[pltpu.VMEM((tile, 128), F32)] * n_scr,
        compiler_params=pltpu.CompilerParams(dimension_semantics=("arbitrary",), vmem_limit_bytes=VMEM_LIMIT),
    )(*tok_ins, *consts)
    return res


MATMUL_VMEM_BUDGET = 40 * 1024 * 1024


def _matmul_tiles(m, n, k, a_bytes, b_bytes, o_bytes, extra_bytes):
    best = None
    for tk in [c for c in (3584, 2816, 2048, 1408, 1024, 512, 256, 128) if k % c == 0]:
        for tm in [c for c in (1408, 1024, 768, 512, 256, 128) if m % c == 0]:
            for tn in [c for c in (1408, 1024, 768, 512, 256, 128) if n % c == 0]:
                need = 2 * tk * (tm * a_bytes + tn * b_bytes) + tm * tn * (2 * o_bytes + 2 * extra_bytes + 8)
                if need > MATMUL_VMEM_BUDGET:
                    continue
                score = (min(tm, 512) * min(tn, 512), tk, tm * tn)
                if best is None or score > best[0]:
                    best = (score, (tm, tn, tk))
    return best[1]


class _Carry:
    def __init__(self, ins, outs, sems, copies, then):
        self.ins, self.outs, self.sems, self.copies, self.then = ins, outs, sems, copies, then


class _Plan:
    def __init__(self):
        self.pending = {}

    def put(self, host, carry):
        assert host not in self.pending, host
        self.pending[host] = carry

    def take(self, host):
        return self.pending.pop(host, None)


def _matmul(name, a, b, dn, out_dtype, epi=None, tiles=(), rows=(), plan=None):
    if dn == NN:
        (m, k), n = a.shape, b.shape[1]
    elif dn == NT:
        (m, k), n = a.shape, b.shape[0]
    else:
        (k, m), n = a.shape, b.shape[1]
    n_t, n_r = len(tiles), len(rows)
    tm, tn, tk = _matmul_tiles(m, n, k, a.dtype.itemsize, b.dtype.itemsize, jnp.dtype(out_dtype).itemsize,
                               sum(t.dtype.itemsize for t in tiles))
    nk = k // tk
    grid = (m // tm, n // tn, nk)
    carry = plan.take(name) if plan is not None else None
    n_ci, n_co = (len(carry.ins), len(carry.outs)) if carry else (0, 0)

    def body(a_ref, b_ref, *rest):
        extras, rest = rest[:n_t + n_r], rest[n_t + n_r:]
        c_in, o_ref, c_out, scratch = rest[:n_ci], rest[n_ci], rest[n_ci + 1:n_ci + 1 + n_co], rest[n_ci + 1 + n_co:]
        ids = [pl.program_id(d) for d in range(3)]
        if carry:
            sems = scratch[1:] if nk > 1 else scratch

            @pl.when((ids[0] == 0) & (ids[1] == 0) & (ids[2] == 0))
            def _():
                for cp in carry.copies(c_in, c_out, *sems):
                    cp.start()

        part = _dot(a_ref[...].astype(BF16), b_ref[...].astype(BF16), dn)

        def finish(r):
            if epi is not None:
                r = epi(r, *[e[...] for e in extras])
            o_ref[...] = r.astype(o_ref.dtype)

        if nk == 1:
            finish(part)
        else:
            acc = scratch[0]

            @pl.when(ids[2] == 0)
            def _():
                acc[...] = part

            @pl.when(ids[2] > 0)
            def _():
                acc[...] += part

            @pl.when(ids[2] == nk - 1)
            def _():
                finish(acc[...])

        if carry:
            @pl.when((ids[0] == grid[0] - 1) & (ids[1] == grid[1] - 1) & (ids[2] == nk - 1))
            def _():
                for cp in carry.copies(c_in, c_out, *sems):
                    cp.wait()

    a_spec = pl.BlockSpec((tk, tm), lambda i, j, kk: (kk, i)) if dn == TN else pl.BlockSpec((tm, tk), lambda i, j, kk: (i, kk))
    b_spec = pl.BlockSpec((tn, tk), lambda i, j, kk: (j, kk)) if dn == NT else pl.BlockSpec((tk, tn), lambda i, j, kk: (kk, j))
    in_specs = [a_spec, b_spec] + [pl.BlockSpec((tm, tn), lambda i, j, kk: (i, j)) for _ in tiles]
    in_specs += [pl.BlockSpec((1, tn), lambda i, j, kk: (0, j)) for _ in rows] + [HBM_SPEC] * n_ci
    res = pl.pallas_call(
        body, name=name, grid=grid, in_specs=in_specs,
        out_specs=[pl.BlockSpec((tm, tn), lambda i, j, kk: (i, j))] + [HBM_SPEC] * n_co,
        out_shape=[SDS((m, n), out_dtype)] + (list(carry.outs) if carry else []),
        scratch_shapes=([pltpu.VMEM((tm, tn), F32)] if nk > 1 else []) + (list(carry.sems) if carry else []),
        compiler_params=pltpu.CompilerParams(
            dimension_semantics=("arbitrary",) * 3 if carry else ("parallel", "parallel", "arbitrary"),
            vmem_limit_bytes=VMEM_LIMIT),
    )(a, b, *tiles, *rows, *(carry.ins if carry else []))
    if carry:
        carry.then(res[1:])
    return res[0]


def _mean_all(v):
    return jnp.mean(v, axis=-1, keepdims=True)


def _head_sums(v, bd):
    w = bd.shape[0]
    return jnp.concatenate([_dot01(v[:, j:j + w], bd) for j in range(0, v.shape[1], w)], axis=1)


def _mean_heads(bd):
    return lambda v: _head_sums(v, bd) * (1.0 / HD)


def _rms_fwd(x, g, mean):
    return x * lax.rsqrt(mean(x * x) + EPS) * g


def _rms_bwd(x, g, dy, mean):
    r = lax.rsqrt(mean(x * x) + EPS)
    dn = dy * g
    dx = r * dn - x * (r * r * r) * mean(dn * x)
    return dx, jnp.sum(dy * x * r, axis=0, keepdims=True)


def _swap_halves(x):
    w = x.shape[1]
    lane = lax.broadcasted_iota(jnp.int32, x.shape, 1)
    return jnp.where(lane % HD < HD // 2, pltpu.roll(x, w - HD // 2, 1), pltpu.roll(x, HD // 2, 1))


def _lanes(t, w):
    return jnp.tile(t, (1, w // t.shape[1]))


def _rope_fwd(x, cos, sin_signed):
    return x * _lanes(cos, x.shape[1]) + _swap_halves(x) * _lanes(sin_signed, x.shape[1])


def _rope_bwd(dy, cos, sin_signed):
    return dy * _lanes(cos, dy.shape[1]) + _swap_halves(dy * _lanes(sin_signed, dy.shape[1]))


def _bcast_heads(cols):
    return jnp.concatenate([jnp.broadcast_to(c, (c.shape[0], HD)) for c in cols], axis=1)


def _softplus(z):
    return jnp.maximum(z, 0.0) + jnp.log1p(jnp.exp(-jnp.abs(z)))


def _block_diag(w):
    h = np.arange(w) // HD
    return jnp.asarray(h[:, None] == h[None, :], BF16)


def _sb_window(i, t):
    hi = (i + 1) * SB_QB - t * SB_WIN
    lo = hi - SB_WIN
    ws = pl.multiple_of(jnp.maximum(lo, 0), SB_QB)
    kpos = ws + lax.broadcasted_iota(jnp.int32, (SB_QB, SB_WIN), 1)
    qpos = i * SB_QB + lax.broadcasted_iota(jnp.int32, (SB_QB, SB_WIN), 0)
    return (kpos < qpos) & (kpos >= lo) & (kpos < hi), ws


def _sb_fwd(qkv):
    s = qkv.shape[0]
    assert s >= SB_WIN
    nq = s // SB_QB
    nh = SB_FWD_HEADS
    bw = HD * nh
    ngroups = SB_W // bw

    def body(q_ref, k_ref, v_ref, later_ref, o_ref, tot_ref, nb_ref):
        p, i = pl.program_id(0), pl.program_id(1)
        q = q_ref[...]
        later_of = later_ref[...]

        def step(c):
            t, _, tots, outs = c
            mask, ws = _sb_window(i, t)
            kw, vw = k_ref[pl.ds(ws, SB_WIN), :], v_ref[pl.ds(ws, SB_WIN), :]
            new_t, new_o = [], []
            for hh in range(nh):
                sl = slice(HD * hh, HD * hh + HD)
                z = _dot(q[:, sl], kw[:, sl], NT) * SCALE
                sp = _softplus(z)
                lf = jnp.where(mask, -sp, 0.0)
                later = tots[hh] + _dot01(lf, later_of)
                w = jnp.where(mask, jnp.exp(z - sp + later), 0.0)
                new_o.append(outs[hh] + _dot(w.astype(BF16), vw[:, sl]))
                new_t.append(tots[hh] + jnp.sum(lf, axis=1, keepdims=True))
            alive = functools.reduce(jnp.maximum, [jnp.max(v) for v in new_t])
            return t + 1, alive, tuple(new_t), tuple(new_o)

        zt, zo = jnp.zeros((SB_QB, 1), F32), jnp.zeros((SB_QB, HD), F32)
        t, _, tots, outs = lax.while_loop(lambda c: ((i + 1) * SB_QB - c[0] * SB_WIN > 0) & (c[1] > EXHAUSTED), step,
                                          (jnp.int32(0), jnp.float32(0.0), (zt,) * nh, (zo,) * nh))
        o_ref[...] = jnp.concatenate(outs, axis=1).astype(o_ref.dtype)
        tot_ref[...] = _bcast_heads(tots)
        nb_ref[p, i] = t

    whole = lambda off: pl.BlockSpec((s, bw), lambda p, i: (0, off + p), pipeline_mode=pl.Buffered(1))
    tile = pl.BlockSpec((SB_QB, bw), lambda p, i: (i, p))
    tri = pl.BlockSpec((SB_WIN, SB_WIN), lambda p, i: (0, 0), pipeline_mode=pl.Buffered(1))
    idx = np.arange(SB_WIN)
    return pl.pallas_call(
        body, name="sb_fwd", grid=(ngroups, nq),
        in_specs=[tile, whole(ngroups), whole(2 * ngroups), tri],
        out_specs=[tile, tile, pl.BlockSpec(memory_space=pltpu.SMEM)],
        out_shape=[SDS((s, SB_W), BF16), SDS((s, SB_W), F32), SDS((ngroups, nq), jnp.int32)],
        compiler_params=pltpu.CompilerParams(dimension_semantics=("arbitrary", "arbitrary"), vmem_limit_bytes=VMEM_LIMIT),
    )(qkv, qkv, qkv, jnp.asarray(idx[:, None] > idx[None, :], BF16))


def _sb_bwd(qkv, do, tot, nblk):
    s = qkv.shape[0]
    nq = s // SB_QB
    npairs = SB_W // 128

    def body(nb_ref, q_ref, k_ref, v_ref, do_ref, tot_ref, upto_ref, before_ref, dq_ref, dk_ref, dv_ref):
        p, i = pl.program_id(0), pl.program_id(1)

        @pl.when(i == 0)
        def _():
            dk_ref[...] = jnp.zeros(dk_ref.shape, F32)
            dv_ref[...] = jnp.zeros(dv_ref.shape, F32)

        upto = upto_ref[...]
        before = before_ref[...]
        q, dout, tt = q_ref[...], do_ref[...], tot_ref[...]
        n = nb_ref[p * 2 // SB_FWD_HEADS, i]

        def step(it, c):
            pres, gpres, dqs = c
            mask, ws = _sb_window(i, n - 1 - it)
            kw, vw = k_ref[pl.ds(ws, SB_WIN), :], v_ref[pl.ds(ws, SB_WIN), :]
            new_p, new_g, new_dq, dks, dvs = [], [], [], [], []
            for hh in range(2):
                sl = slice(HD * hh, HD * hh + HD)
                z = _dot(q[:, sl], kw[:, sl], NT) * SCALE
                sp = _softplus(z)
                lf = jnp.where(mask, -sp, 0.0)
                later = tt[:, HD * hh:HD * hh + 1] - (pres[hh] + _dot01(lf, upto))
                w = jnp.where(mask, jnp.exp(z - sp + later), 0.0)
                beta = jnp.exp(z - sp)
                g = _dot(dout[:, sl], vw[:, sl], NT) * w
                g_far = gpres[hh] + _dot(g.astype(BF16), before)
                dz = (jnp.where(mask, g * (1.0 - beta) - beta * g_far, 0.0) * SCALE).astype(BF16)
                new_dq.append(dqs[hh] + _dot(dz, kw[:, sl]))
                dks.append(_dot(dz, q[:, sl], TN))
                dvs.append(_dot(w.astype(BF16), dout[:, sl], TN))
                new_p.append(pres[hh] + jnp.sum(lf, axis=1, keepdims=True))
                new_g.append(gpres[hh] + jnp.sum(g, axis=1, keepdims=True))
            dk_ref[pl.ds(ws, SB_WIN), :] += jnp.concatenate(dks, axis=1)
            dv_ref[pl.ds(ws, SB_WIN), :] += jnp.concatenate(dvs, axis=1)
            return tuple(new_p), tuple(new_g), tuple(new_dq)

        zt, zo = jnp.zeros((SB_QB, 1), F32), jnp.zeros((SB_QB, HD), F32)
        _, _, dqs = lax.fori_loop(0, n, step, ((zt, zt), (zt, zt), (zo, zo)))
        dq_ref[...] = jnp.concatenate(dqs, axis=1)

    whole_in = lambda off: pl.BlockSpec((s, 128), lambda p, i: (0, off + p), pipeline_mode=pl.Buffered(1))
    whole_out = pl.BlockSpec((s, 128), lambda p, i: (0, p), pipeline_mode=pl.Buffered(1))
    tile = pl.BlockSpec((SB_QB, 128), lambda p, i: (i, p))
    tri = pl.BlockSpec((SB_WIN, SB_WIN), lambda p, i: (0, 0), pipeline_mode=pl.Buffered(1))
    idx = np.arange(SB_WIN)
    return pl.pallas_call(
        body, name="sb_bwd", grid=(npairs, nq),
        in_specs=[pl.BlockSpec(memory_space=pltpu.SMEM), tile, whole_in(npairs), whole_in(2 * npairs), tile, tile, tri, tri],
        out_specs=[tile, whole_out, whole_out],
        out_shape=[SDS((s, SB_W), F32)] * 3,
        compiler_params=pltpu.CompilerParams(dimension_semantics=("arbitrary", "arbitrary"), vmem_limit_bytes=VMEM_LIMIT),
    )(nblk, qkv, qkv, qkv, do, tot, jnp.asarray(idx[:, None] <= idx[None, :], BF16), jnp.asarray(idx[:, None] < idx[None, :], BF16))


def _dsa_mask(has_prev):
    r = lax.broadcasted_iota(jnp.int32, (DSA_T, QB + DSA_T), 0)
    j = lax.broadcasted_iota(jnp.int32, (DSA_T, QB + DSA_T), 1) - QB
    return (j <= r) & (j >= r - QB) & ((j >= 0) | has_prev)


def _dsa_fwd(q, k, v, dil):
    n = q.shape[0]
    nt = n // DSA_T

    def body(q_ref, kc_ref, kp_ref, vc_ref, vp_ref, o_ref, lse_ref):
        mask = _dsa_mask(pl.program_id(1) > 0)
        outs, lses = [], []
        for hh in range(DSA_OUT_W // HD):
            sl = slice(HD * hh, HD * hh + HD)
            kcat = jnp.concatenate([kp_ref[:, sl], kc_ref[:, sl]], axis=0)
            vcat = jnp.concatenate([vp_ref[:, sl], vc_ref[:, sl]], axis=0)
            sc = jnp.where(mask, _dot(q_ref[:, sl], kcat, NT) * SCALE, NEG)
            m = jnp.max(sc, axis=1, keepdims=True)
            p = jnp.exp(sc - m)
            den = jnp.sum(p, axis=1, keepdims=True)
            outs.append(_dot(p.astype(BF16), vcat) / den)
            lses.append(m + jnp.log(den))
        o_ref[...] = jnp.concatenate(outs, axis=1)
        lse_ref[...] = _bcast_heads(lses)

    cur = pl.BlockSpec((DSA_T, DSA_OUT_W), lambda c, i: (i, c))
    prev = pl.BlockSpec((QB, DSA_OUT_W), lambda c, i: (jnp.maximum(i * (DSA_T // QB) - 1, 0), c))
    o, lse = pl.pallas_call(
        body, name=f"dsa_fwd_d{dil}", grid=(dil, nt), in_specs=[cur, cur, prev, cur, prev], out_specs=[cur, cur],
        out_shape=[SDS((n, dil * DSA_OUT_W), F32)] * 2,
        compiler_params=pltpu.CompilerParams(dimension_semantics=("parallel", "parallel")),
    )(q, k, k, v, v)
    return o, lse


def _dsa_bwd(q, k, v, do, cc, lse, dil):
    n = q.shape[0]
    nt = n // DSA_T
    per = DSA_T // QB

    def body(qj_ref, qn_ref, kp_ref, kj_ref, vp_ref, vj_ref, doj_ref, don_ref, cj_ref, cn_ref, lj_ref, ln_ref,
             dq_ref, dk_ref, dv_ref):
        j = pl.program_id(1)
        mask = _dsa_mask(j > 0)
        r = lax.broadcasted_iota(jnp.int32, (QB, DSA_T), 0)
        kk = lax.broadcasted_iota(jnp.int32, (QB, DSA_T), 1)
        m_next = (kk >= r + QB) & (j + 1 < nt)
        dqs, dks, dvs = [], [], []
        for hh in range(DSA_OUT_W // HD):
            sl = slice(HD * hh, HD * hh + HD)
            one = slice(HD * hh, HD * hh + 1)
            qj, qn, kj, vj, doj, don = (t[:, sl] for t in (qj_ref, qn_ref, kj_ref, vj_ref, doj_ref, don_ref))
            kcat = jnp.concatenate([kp_ref[:, sl], kj], axis=0)
            vcat = jnp.concatenate([vp_ref[:, sl], vj], axis=0)
            p1 = jnp.where(mask, jnp.exp(_dot(qj, kcat, NT) * SCALE - lj_ref[:, one]), 0.0)
            ds1 = (p1 * (_dot(doj, vcat, NT) + cj_ref[:, one]) * SCALE).astype(BF16)
            p2 = jnp.where(m_next, jnp.exp(_dot(qn, kj, NT) * SCALE - ln_ref[:, one]), 0.0)
            ds2 = (p2 * (_dot(don, vj, NT) + cn_ref[:, one]) * SCALE).astype(BF16)
            dqs.append(_dot(ds1, kcat))
            dks.append(_dot(ds1[:, QB:], qj, TN) + _dot(ds2, qn, TN))
            dvs.append(_dot(p1[:, QB:].astype(BF16), doj, TN) + _dot(p2.astype(BF16), don, TN))
        dq_ref[...] = jnp.concatenate(dqs, axis=1)
        dk_ref[...] = jnp.concatenate(dks, axis=1)
        dv_ref[...] = jnp.concatenate(dvs, axis=1)

    cur = pl.BlockSpec((DSA_T, DSA_OUT_W), lambda c, j: (j, c))
    prev = pl.BlockSpec((QB, DSA_OUT_W), lambda c, j: (jnp.maximum(j * per - 1, 0), c))
    nxt = pl.BlockSpec((QB, DSA_OUT_W), lambda c, j: (jnp.minimum((j + 1) * per, n // QB - 1), c))
    dq, dk, dv = pl.pallas_call(
        body, name=f"dsa_bwd_d{dil}", grid=(dil, nt),
        in_specs=[cur, nxt, prev, cur, prev, cur, cur, nxt, cur, nxt, cur, nxt], out_specs=[cur, cur, cur],
        out_shape=[SDS((n, dil * DSA_OUT_W), F32)] * 3,
        compiler_params=pltpu.CompilerParams(dimension_semantics=("parallel", "parallel")),
    )(q, q, k, k, v, v, do, do, cc, cc, lse, lse)
    return dq, dk, dv


def _ffn_fwd(tag, x, gain, w13, w2, plan=None):
    n = _tokmap(f"{tag}_norm", lambda xv, g: _rms_fwd(xv, g, _mean_all), [x], [gain], [(D, BF16)])[0]
    ab = _matmul(f"{tag}_up", n, w13, NN, BF16, plan=plan)

    def gate(abv):
        a, b = abv[:, :D_FF].astype(F32), abv[:, D_FF:].astype(F32)
        return a * jax.nn.sigmoid(a) * b

    h = _tokmap(f"{tag}_gate", gate, [ab], [], [(D_FF, BF16)], tile=256)[0]
    y = _matmul(f"{tag}_down", h, w2(), NN, F32, epi=lambda acc, res: res + 0.5 * acc, tiles=[x], plan=plan)
    return y, (n, ab, h)


def _ffn_bwd(tag, x, gain, w13, w2, saved, dy, plan=None, on_dw=None):
    n, ab, h = saved
    dh = _matmul(f"{tag}_bwd_dh", dy, w2, NT, BF16, epi=lambda acc: 0.5 * acc)

    def gate_bwd(abv, dhv):
        a, b, dhf = abv[:, :D_FF].astype(F32), abv[:, D_FF:].astype(F32), dhv.astype(F32)
        sg = jax.nn.sigmoid(a)
        da = dhf * b * (sg * (1.0 + a * (1.0 - sg)))
        return jnp.concatenate([da, dhf * (a * sg)], axis=1)

    dab = _tokmap(f"{tag}_bwd_gate", gate_bwd, [ab, dh], [], [(2 * D_FF, BF16)], tile=256)[0]
    dw2 = _matmul(f"{tag}_bwd_dw2", h, dy, TN, F32, epi=lambda acc: 0.5 * acc)
    dw13 = _matmul(f"{tag}_bwd_dw13", n, dab, TN, F32, plan=plan)
    if on_dw is not None:
        on_dw(dw13, dw2)
    dn = _matmul(f"{tag}_bwd_dn", dab, w13, NT, F32, plan=plan)

    def norm_bwd(xv, dnv, dyv, g):
        dx, dg = _rms_bwd(xv, g, dnv, _mean_all)
        return dx + dyv, dg

    dx, dgain = _tokmap(f"{tag}_bwd_norm", norm_bwd, [x, dn, dy], [gain], [(D, F32)], [(1, D)])
    return dx, dgain, dw13, dw2


def _rope_tables(s):
    half = HD // 2
    inv_freq = jnp.power(10000.0, -jnp.arange(half, dtype=F32) / half)
    ang = jnp.arange(s).astype(F32)[:, None] * inv_freq[None, :]
    cos, sin = jnp.cos(ang), jnp.sin(ang)
    return jnp.tile(jnp.concatenate([cos, cos], axis=1), (1, 2)), jnp.tile(jnp.concatenate([-sin, sin], axis=1), (1, 2))


def _local_step(x, mem, tgt, w, sm, plan=None, on_grads=None):
    s = x.shape[0]
    assert s % (DSA_T * max(DSA_DILS)) == 0
    on_grads = on_grads or (lambda group, grads: None)
    c_sb, c_dsa, c_qm = 3 * SB_W, 3 * SB_W + 3 * DSA_W, 4096
    cos, sin = _rope_tables(s)
    bd768 = bd256 = _block_diag(128)
    gq_dsa, gk_dsa = jnp.tile(sm["qn_dsa"], (1, DSA_W // HD)), jnp.tile(sm["kn_dsa"], (1, DSA_W // HD))
    gq_mem, gk_mem = jnp.tile(sm["qn_mem"], (1, MEM_W // HD)), jnp.tile(sm["kn_mem"], (1, MEM_W // HD))

    w13_1 = jnp.concatenate([w["ffn1_w1"], w["ffn1_w3"]], axis=1)
    x1, ffn1_saved = _ffn_fwd("ffn1", x, sm["ffn1_norm"], w13_1, lambda: w["ffn1_w2"], plan)
    w_all = jnp.concatenate([w["w_in"], w["w_gate"]], axis=1)
    wb_sb, wb_dsa, wb_mem = w["w_branch_sb"], w["w_branch_dsa"], w["w_branch_mem"]
    hmix = _tokmap("mix_norm", lambda xv, g: _rms_fwd(xv, g, _mean_all), [x1], [sm["mix_norm"]], [(D, BF16)])[0]
    qkv_sb = _matmul("proj_sb", hmix, w_all[:, :c_sb], NN, BF16)
    qkv_dsa = _matmul("proj_dsa", hmix, w_all[:, c_sb:c_dsa], NN, BF16, plan=plan)
    q_mem = _matmul("proj_qmem", hmix, w_all[:, c_dsa:c_qm], NN, BF16)
    gpre = _matmul("proj_gate", hmix, w_all[:, c_qm:], NN, BF16, epi=lambda acc, b: acc + b, rows=[sm["b_gate"]], plan=plan)

    o_sb, sb_tot, sb_nblk = _sb_fwd(qkv_sb)

    def dsa_prep(qkv, cs, sn, gq, gk, bd):
        mean = _mean_heads(bd)
        qn = _rope_fwd(_rms_fwd(qkv[:, :DSA_W].astype(F32), gq, mean), cs, sn)
        kn = _rope_fwd(_rms_fwd(qkv[:, DSA_W:2 * DSA_W].astype(F32), gk, mean), cs, sn)
        v = qkv[:, 2 * DSA_W:]
        outs = []
        for t in (qn, kn, v):
            outs += [t[:, DSA_OUT_W * g:DSA_OUT_W * (g + 1)] for g in range(3)]
        return outs

    dsa_in = _tokmap("dsa_prep", dsa_prep, [qkv_dsa, cos, sin], [gq_dsa, gk_dsa, bd768], [(DSA_OUT_W, BF16)] * 9, tile=256,
                     dil_outs={j: DSA_DILS[j % 3] for j in range(9)})
    dsa_q, dsa_k, dsa_v = dsa_in[0:3], dsa_in[3:6], dsa_in[6:9]
    dsa_o, dsa_lse = zip(*[_dsa_fwd(dsa_q[g], dsa_k[g], dsa_v[g], DSA_DILS[g]) for g in range(3)])

    def alphas(l0, l1, l2):
        m = jnp.maximum(jnp.maximum(l0, l1), l2)
        e = [jnp.exp(l - m) for l in (l0, l1, l2)]
        tot = e[0] + e[1] + e[2]
        return [t / tot for t in e]

    def dsa_mix(o0, o1, o2, l0, l1, l2):
        a = alphas(l0, l1, l2)
        return a[0] * o0 + a[1] * o1 + a[2] * o2

    o_dsa = _tokmap("dsa_mix", dsa_mix, [*dsa_o, *dsa_lse], [], [(DSA_OUT_W, BF16)], tile=256,
                    dil_ins={j: DSA_DILS[j % 3] for j in range(6)})[0]

    def mem_kv(memv, g, wkv, gk, bd):
        kv = _dot(_rms_fwd(memv, g, _mean_all).astype(BF16), wkv)
        return _rms_fwd(kv[:, :MEM_W], gk, _mean_heads(bd)), kv[:, MEM_W:]

    km, vm = _tokmap("mem_kv", mem_kv, [mem], [sm["mem_norm"], w["w_mem_kv"], gk_mem, bd256], [(MEM_W, BF16)] * 2)

    def mem_probs(qv, kmv, gq, bd):
        qn = _rms_fwd(qv.astype(F32), gq, _mean_heads(bd)).astype(BF16)
        ps = []
        for h in range(MEM_W // HD):
            sl = slice(HD * h, HD * h + HD)
            sc = _dot(qn[:, sl], kmv[:, sl], NT) * SCALE
            e = jnp.exp(sc - jnp.max(sc, axis=1, keepdims=True))
            ps.append(e / jnp.sum(e, axis=1, keepdims=True))
        return qn, ps

    def mem_attn(qv, kmv, vmv, gq, bd):
        _, ps = mem_probs(qv, kmv, gq, bd)
        return jnp.concatenate([_dot(p.astype(BF16), vmv[:, HD * h:HD * h + HD]) for h, p in enumerate(ps)], axis=1)

    o_mem = _tokmap("mem_attn", mem_attn, [q_mem], [km, vm, gq_mem, bd256], [(MEM_W, BF16)])[0]

    def merge(osb, odsa, omem, gp, w_sb, w_dsa, w_mem):
        gates = jax.nn.sigmoid(gp.astype(F32))
        ys = (_dot(osb, w_sb), _dot(odsa, w_dsa), _dot(omem, w_mem))
        return gates, ys, gates[:, :D] * ys[0] + gates[:, D:2 * D] * ys[1] + gates[:, 2 * D:] * ys[2]

    merged = _tokmap("merge", lambda *a: merge(*a)[2], [o_sb, o_dsa, o_mem, gpre], [wb_sb, wb_dsa, wb_mem], [(D, BF16)],
                     tile=256)[0]
    x2 = _matmul("out_proj", merged, w["w_out"], NN, F32, epi=lambda acc, res: res + acc, tiles=[x1])
    w13_2 = jnp.concatenate([w["ffn2_w1"], w["ffn2_w3"]], axis=1)
    y, ffn2_saved = _ffn_fwd("ffn2", x2, sm["ffn2_norm"], w13_2, lambda: w["ffn2_w2"])

    def loss_fn(yv, tv):
        e = yv - tv
        part = 0.5 * jnp.sum(jnp.mean(e * e, axis=1, keepdims=True), axis=0, keepdims=True)
        return e * (1.0 / D), jnp.broadcast_to(part, (1, 128))

    dy, loss = _tokmap("loss", loss_fn, [y, tgt], [], [(D, F32)], [(1, 128)])

    gw, gs = {}, {}
    def ffn_grads(tag):
        def on_dw(dw13, dw2):
            gw[f"{tag}_w1"], gw[f"{tag}_w3"], gw[f"{tag}_w2"] = dw13[:, :D_FF], dw13[:, D_FF:], dw2
            on_grads(tag, {n: gw[n] for n in (f"{tag}_w1", f"{tag}_w3", f"{tag}_w2")})
        return on_dw

    dx2, gs["ffn2_norm"], _, _ = _ffn_bwd("ffn2", x2, sm["ffn2_norm"], w13_2, w["ffn2_w2"], ffn2_saved, dy, plan,
                                          ffn_grads("ffn2"))
    dmerged = _matmul("out_proj_bwd_dx", dx2, w["w_out"], NT, BF16)
    gw["w_out"] = _matmul("out_proj_bwd_dw", merged, dx2, TN, F32)

    def merge_bwd(osb, odsa, omem, gp, dm, w_sb, w_dsa, w_mem):
        gates, ys, _ = merge(osb, odsa, omem, gp, w_sb, w_dsa, w_mem)
        dmf = dm.astype(F32)
        dgp, dos, dws = [], [], []
        for b, (ov, wv) in enumerate(((osb, w_sb), (odsa, w_dsa), (omem, w_mem))):
            gb = gates[:, D * b:D * (b + 1)]
            dgp.append(dmf * ys[b] * gb * (1.0 - gb))
            dyb = (dmf * gb).astype(BF16)
            dos.append(_dot(dyb, wv, NT))
            dws.append(_dot(ov, dyb, TN))
        dgp = jnp.concatenate(dgp, axis=1)
        return dos[0], dos[1], dos[2], dgp, dws[0], dws[1], dws[2], jnp.sum(dgp, axis=0, keepdims=True)

    do_sb, do_dsa, do_mem, dgpre, gw["w_branch_sb"], gw["w_branch_dsa"], gw["w_branch_mem"], gs["b_gate"] = _tokmap(
        "merge_bwd", merge_bwd, [o_sb, o_dsa, o_mem, gpre, dmerged], [wb_sb, wb_dsa, wb_mem],
        [(SB_W, BF16), (DSA_OUT_W, F32), (MEM_W, BF16), (3 * D, BF16)],
        [(SB_W, D), (DSA_OUT_W, D), (MEM_W, D), (1, 3 * D)], tile=256)

    dq_sb, dk_sb, dv_sb = _sb_bwd(qkv_sb, do_sb, sb_tot, sb_nblk)

    def dsa_mix_bwd(o0, o1, o2, l0, l1, l2, dov, bd):
        a = alphas(l0, l1, l2)
        omix = a[0] * o0 + a[1] * o1 + a[2] * o2
        dot_o = _head_sums(dov * omix, bd)
        return [dov * t for t in a] + [-t * dot_o for t in a]

    mixb = _tokmap("dsa_mix_bwd", dsa_mix_bwd, [*dsa_o, *dsa_lse, do_dsa], [bd256],
                   [(DSA_OUT_W, BF16)] * 3 + [(DSA_OUT_W, F32)] * 3, tile=256,
                   dil_ins={j: DSA_DILS[j % 3] for j in range(6)}, dil_outs={j: DSA_DILS[j % 3] for j in range(6)})
    dsa_d = [_dsa_bwd(dsa_q[g], dsa_k[g], dsa_v[g], mixb[g], mixb[3 + g], dsa_lse[g], DSA_DILS[g]) for g in range(3)]

    def dsa_prep_bwd(qkv, cs, sn, *rest):
        dqs, dks, dvs, (gq, gk, bd) = rest[0:3], rest[3:6], rest[6:9], rest[9:]
        mean = _mean_heads(bd)
        dq, dgq = _rms_bwd(qkv[:, :DSA_W].astype(F32), gq, _rope_bwd(jnp.concatenate(dqs, axis=1), cs, sn), mean)
        dk, dgk = _rms_bwd(qkv[:, DSA_W:2 * DSA_W].astype(F32), gk, _rope_bwd(jnp.concatenate(dks, axis=1), cs, sn), mean)
        return jnp.concatenate([dq, dk] + list(dvs), axis=1), dgq, dgk

    dqkv_dsa, dgq_dsa, dgk_dsa = _tokmap(
        "dsa_prep_bwd", dsa_prep_bwd,
        [qkv_dsa, cos, sin] + [dsa_d[g][t] for t in range(3) for g in range(3)], [gq_dsa, gk_dsa, bd768],
        [(3 * DSA_W, BF16)], [(1, DSA_W), (1, DSA_W)], tile=256, dil_ins={3 + j: DSA_DILS[j % 3] for j in range(9)})
    gs["qn_dsa"] = dgq_dsa.reshape(DSA_W // HD, HD).sum(axis=0, keepdims=True)
    gs["kn_dsa"] = dgk_dsa.reshape(DSA_W // HD, HD).sum(axis=0, keepdims=True)

    def mem_attn_bwd(qv, dov, kmv, vmv, gq, bd):
        qn, ps = mem_probs(qv, kmv, gq, bd)
        dqn, dkm, dvm = [], [], []
        for h, p in enumerate(ps):
            sl = slice(HD * h, HD * h + HD)
            dp = _dot(dov[:, sl], vmv[:, sl], NT)
            ds = (p * (dp - jnp.sum(p * dp, axis=1, keepdims=True)) * SCALE).astype(BF16)
            dqn.append(_dot(ds, kmv[:, sl]))
            dkm.append(_dot(ds, qn[:, sl], TN))
            dvm.append(_dot(p.astype(BF16), dov[:, sl], TN))
        dq, dgq = _rms_bwd(qv.astype(F32), gq, jnp.concatenate(dqn, axis=1), _mean_heads(bd))
        return dq, jnp.concatenate(dkm, axis=1), jnp.concatenate(dvm, axis=1), dgq

    dq_mem, dkm, dvm, dgq_mem = _tokmap("mem_attn_bwd", mem_attn_bwd, [q_mem, do_mem], [km, vm, gq_mem, bd256],
                                        [(MEM_W, BF16)], [(MEM_LEN, MEM_W), (MEM_LEN, MEM_W), (1, MEM_W)])
    gs["qn_mem"] = dgq_mem.reshape(MEM_W // HD, HD).sum(axis=0, keepdims=True)

    def mem_kv_bwd(memv, dkmv, dvmv, g, wkv, gk, bd):
        memn = _rms_fwd(memv, g, _mean_all).astype(BF16)
        kv = _dot(memn, wkv)
        dk, dgk = _rms_bwd(kv[:, :MEM_W], gk, dkmv, _mean_heads(bd))
        dkv = jnp.concatenate([dk, dvmv], axis=1).astype(BF16)
        _, dg = _rms_bwd(memv, g, _dot(dkv, wkv, NT), _mean_all)
        return _dot(memn, dkv, TN), dg, dgk

    gw["w_mem_kv"], gs["mem_norm"], dgk_mem = _tokmap(
        "mem_kv_bwd", mem_kv_bwd, [mem, dkm, dvm], [sm["mem_norm"], w["w_mem_kv"], gk_mem, bd256], [],
        [(D, 2 * MEM_W), (1, D), (1, MEM_W)])
    gs["kn_mem"] = dgk_mem.reshape(MEM_W // HD, HD).sum(axis=0, keepdims=True)

    dall = jnp.concatenate([dq_sb.astype(BF16), dk_sb.astype(BF16), dv_sb.astype(BF16), dqkv_dsa, dq_mem, dgpre], axis=1)
    dhmix = _matmul("proj_bwd_dx", dall, w_all, NT, F32)
    dw_all = _matmul("proj_bwd_dw", hmix, dall, TN, F32)
    gw["w_in"], gw["w_gate"] = dw_all[:, :c_qm], dw_all[:, c_qm:]
    on_grads("mid", {n: gw[n] for n in GROUPS["mid"]})

    def mix_norm_bwd(xv, dnv, dyv, g):
        dx, dg = _rms_bwd(xv, g, dnv, _mean_all)
        return dx + dyv, dg

    dx1, gs["mix_norm"] = _tokmap("mix_norm_bwd", mix_norm_bwd, [x1, dhmix, dx2], [sm["mix_norm"]], [(D, F32)], [(1, D)])
    gx, gs["ffn1_norm"], _, _ = _ffn_bwd("ffn1", x, sm["ffn1_norm"], w13_1, w["ffn1_w2"], ffn1_saved, dx1, plan,
                                         ffn_grads("ffn1"))
    return loss, gx, gw, gs


def _shard_shape(name):
    shape, axis = SHARDED_BY_NAME[name]
    return (shape[0] // N_CHIPS, shape[1]) if axis == 0 else (shape[0], shape[1] // N_CHIPS)


def _full_from_shards(name, shards):
    axis = SHARDED_BY_NAME[name][1]
    return shards.reshape(SHARDED_BY_NAME[name][0]) if axis == 0 else jnp.concatenate(list(shards), axis=1)


def _shards_from_full(name, full, dtype):
    axis, n = SHARDED_BY_NAME[name][1], _shard_shape(name)
    return jnp.stack([lax.slice_in_dim(full, c * n[axis], (c + 1) * n[axis], axis=axis).astype(dtype) for c in range(N_CHIPS)])


def _own_shard(name, full, chip):
    axis, n = SHARDED_BY_NAME[name][1], _shard_shape(name)
    return lax.dynamic_slice_in_dim(full, chip * n[axis], n[axis], axis=axis)


SMALL_USED = sum(n for _, n in SMALL)


def _pack_small(d, loss=None):
    parts = [d[n].reshape(-1) for n, _ in SMALL]
    parts.append(jnp.zeros((1,), F32) if loss is None else loss.reshape(1))
    parts.append(jnp.zeros((SMALL_ROWS * D - SMALL_USED - 1,), F32))
    return jnp.concatenate(parts).reshape(SMALL_ROWS, D)


def _unpack_small(v):
    flat, out, r = v.reshape(-1), {}, 0
    for n, k in SMALL:
        out[n] = flat[r:r + k]
        r += k
    return out, flat[r]


def _place():
    return lax.axis_index("x"), lax.axis_index("y"), lax.axis_index("c")


def _other_chips(x, y):
    return [(1 - x, y), (x, 1 - y), (1 - x, 1 - y)]


HBM_SPEC = pl.BlockSpec(memory_space=pl.ANY)


def _chip_sems(n):
    return (pltpu.SemaphoreType.DMA((3 * n,)), pltpu.SemaphoreType.DMA((3 * n,)), pltpu.SemaphoreType.DMA((n,)))


def _gather_copies(ins, outs, send_sems, recv_sems, local_sems):
    x, y, c = _place()
    me = 2 * x + y
    copies = []
    for a, (src, out) in enumerate(zip(ins, outs)):
        copies.append(pltpu.make_async_copy(src, out.at[me], local_sems.at[a]))
        copies += [pltpu.make_async_remote_copy(src_ref=src, dst_ref=out.at[me], send_sem=send_sems.at[3 * a + k],
                                                recv_sem=recv_sems.at[3 * a + k], device_id=(px, py, c), device_id_type=MESH)
                   for k, (px, py) in enumerate(_other_chips(x, y))]
    return copies


def _scatter_copies(ins, outs, send_sems, recv_sems, local_sems):
    x, y, c = _place()
    return [pltpu.make_async_remote_copy(src_ref=src.at[2 * px + py], dst_ref=out.at[k], send_sem=send_sems.at[3 * a + k],
                                         recv_sem=recv_sems.at[3 * a + k], device_id=(px, py, c), device_id_type=MESH)
            for a, (src, out) in enumerate(zip(ins, outs)) for k, (px, py) in enumerate(_other_chips(x, y))]


def _all_gather_chips(arrays):
    n = len(arrays)

    def body(*refs):
        copies = _gather_copies(refs[:n], refs[n:2 * n], *refs[2 * n:])
        for cp in copies:
            cp.start()
        for cp in copies:
            cp.wait()

    return pl.pallas_call(
        body, name="weights_all_gather", in_specs=[HBM_SPEC] * n, out_specs=[HBM_SPEC] * n,
        out_shape=[SDS((N_CHIPS,) + a.shape, a.dtype) for a in arrays], scratch_shapes=list(_chip_sems(n)),
    )(*arrays)


def _swap_with_sibling(name, arrays):
    n = len(arrays)

    def body(*refs):
        x, y, c = _place()
        send_sems, recv_sems = refs[2 * n:]
        copies = [pltpu.make_async_remote_copy(src_ref=refs[a], dst_ref=refs[n + a], send_sem=send_sems.at[a],
                                               recv_sem=recv_sems.at[a], device_id=(x, y, 1 - c), device_id_type=MESH)
                  for a in range(n)]
        for cp in copies:
            cp.start()
        for cp in copies:
            cp.wait()

    return pl.pallas_call(
        body, name=name, in_specs=[HBM_SPEC] * n, out_specs=[HBM_SPEC] * n, out_shape=[SDS(a.shape, a.dtype) for a in arrays],
        scratch_shapes=[pltpu.SemaphoreType.DMA((n,)), pltpu.SemaphoreType.DMA((n,))],
    )(*arrays)


def _all_reduce_small(v):
    n_dev = 8

    def body(v_ref, out_ref, land, send_sems, recv_sems):
        x, y, c = _place()
        me = 4 * x + 2 * y + c
        land[me] = v_ref[...]
        copies = []
        for k in range(1, n_dev):
            peer = (x ^ (k >> 2), y ^ ((k >> 1) & 1), c ^ (k & 1))
            copies.append(pltpu.make_async_remote_copy(src_ref=v_ref, dst_ref=land.at[me], send_sem=send_sems.at[k - 1],
                                                       recv_sem=recv_sems.at[k - 1], device_id=peer, device_id_type=MESH))
        for cp in copies:
            cp.start()
        for cp in copies:
            cp.wait()
        acc = land[0]
        for d in range(1, n_dev):
            acc = acc + land[d]
        out_ref[...] = acc

    return pl.pallas_call(
        body, name="small_all_reduce", in_specs=[pl.BlockSpec(memory_space=pltpu.VMEM)],
        out_specs=pl.BlockSpec(memory_space=pltpu.VMEM), out_shape=SDS(v.shape, v.dtype),
        scratch_shapes=[pltpu.VMEM((n_dev,) + v.shape, v.dtype), pltpu.SemaphoreType.DMA((n_dev - 1,)),
                        pltpu.SemaphoreType.DMA((n_dev - 1,))],
    )(v)


def _adamw(g, wv, m, v):
    m = ADAM_B1 * m + (1.0 - ADAM_B1) * g
    v = ADAM_B2 * v + (1.0 - ADAM_B2) * (g * g)
    m_hat = m / (1.0 - ADAM_B1 ** ADAM_STEP)
    v_hat = v / (1.0 - ADAM_B2 ** ADAM_STEP)
    delta = -ADAM_LR * (m_hat / (jnp.sqrt(v_hat) + ADAM_EPS) + ADAM_WD * wv)
    return delta, m, v


def kernel(x, mem, ffn1_norm, ffn1_w1, ffn1_w3, ffn1_w2, mix_norm, mem_norm, w_in, w_mem_kv, qn_dsa, kn_dsa, qn_mem, kn_mem, w_branch_sb, w_branch_dsa, w_branch_mem, w_gate, b_gate, w_out, ffn2_norm, ffn2_w1, ffn2_w3, ffn2_w2, loss_target, m_ffn1_norm, m_ffn1_w1, m_ffn1_w3, m_ffn1_w2, m_mix_norm, m_mem_norm, m_w_in, m_w_mem_kv, m_qn_dsa, m_kn_dsa, m_qn_mem, m_kn_mem, m_w_branch_sb, m_w_branch_dsa, m_w_branch_mem, m_w_gate, m_b_gate, m_w_out, m_ffn2_norm, m_ffn2_w1, m_ffn2_w3, m_ffn2_w2, v_ffn1_norm, v_ffn1_w1, v_ffn1_w3, v_ffn1_w2, v_mix_norm, v_mem_norm, v_w_in, v_w_mem_kv, v_qn_dsa, v_kn_dsa, v_qn_mem, v_kn_mem, v_w_branch_sb, v_w_branch_dsa, v_w_branch_mem, v_w_gate, v_b_gate, v_w_out, v_ffn2_norm, v_ffn2_w1, v_ffn2_w3, v_ffn2_w2):
    given = dict(locals())
    wts = {n: given[n][0] for n in WEIGHTS}
    moms = {n: given["m_" + n][0] for n in WEIGHTS}
    vars_ = {n: given["v_" + n][0] for n in WEIGHTS}

    plan = _Plan()
    x_i, y_i, _ = _place()
    my_chip = 2 * x_i + y_i

    full = {}

    def gathered(names):
        return lambda res: full.update({n: _full_from_shards(n, g) for n, g in zip(names, res)})

    for host, names in WEIGHT_PIECES:
        shards = [wts[n].astype(BF16) for n in names]
        if host is None:
            gathered(names)(_all_gather_chips(shards))
        else:
            plan.put(host, _Carry(shards, [SDS((N_CHIPS,) + a.shape, BF16) for a in shards], _chip_sems(len(names)),
                                  _gather_copies, gathered(names)))
    small = {n: wts[n].reshape(1, -1) for n, _ in SMALL}

    landed = {}

    def on_grads(group, grads):
        names = GROUPS[group]
        slices = [_shards_from_full(n, grads[n], BF16) for n in names]
        own = [_own_shard(n, grads[n], my_chip) for n in names]
        plan.put(GRAD_HOSTS[group], _Carry(slices, [SDS((3,) + a.shape[1:], BF16) for a in slices], _chip_sems(len(names)),
                                           _scatter_copies, lambda res: landed.update({group: (own, res)})))

    loss, gx, _, gs = _local_step(x[0], mem[0], loss_target[0], full, small, plan, on_grads)
    assert not plan.pending, list(plan.pending)

    def update(hv, ov, wv, mv, vv):
        g = hv + ov
        return (g,) + _adamw(g, wv, mv, vv)

    outs = [{}, {}, {}, {}]
    for group, names in GROUPS.items():
        own, got = landed[group]
        halves = [_tokmap(f"grads_sum_chips_{n}",
                          lambda a, b0, b1, b2: ((a + b0.astype(F32)) + b1.astype(F32)) + b2.astype(F32),
                          [o, g[0], g[1], g[2]], [], [(o.shape[1], F32)])[0] for n, o, g in zip(names, own, got)]
        others = _swap_with_sibling(f"grads_swap_cores_{group}", halves)
        for n, half, other in zip(names, halves, others):
            res = _tokmap(f"adamw_{n}", update, [half, other, wts[n], moms[n], vars_[n]], [], [(half.shape[1], F32)] * 4)
            for d, r in zip(outs, res):
                d[n] = r

    s_red = _all_reduce_small(_pack_small(gs, loss[0, 0]))
    res = _tokmap(
        "adamw_small", lambda g, wv, mv, vv: (g,) + _adamw(g, wv, mv, vv),
        [s_red, _pack_small(small), _pack_small({n: moms[n] for n, _ in SMALL}), _pack_small({n: vars_[n] for n, _ in SMALL})],
        [], [(D, F32)] * 4)
    for d, packed in zip(outs, res):
        d.update(_unpack_small(packed)[0])
    _, total_loss = _unpack_small(s_red)
    return (total_loss, gx[None], *[d[n][None] for d in outs for n in WEIGHTS])
```

```python
import functools

import numpy as np
import jax
import jax.numpy as jnp
from jax import lax
from jax.experimental import pallas as pl
from jax.experimental.pallas import tpu as pltpu

F32, BF16 = jnp.float32, jnp.bfloat16
SDS = jax.ShapeDtypeStruct
MESH = pl.DeviceIdType.MESH

D = 1024
HD = 64
QB = 128
DSA_T_FWD, DSA_T_BWD = 512, 256
D_FF = 2816
SB_W, DSA_W, DSA_OUT_W, MEM_W = 512, 768, 256, 256
DSA_DILS = (1, 4, 16)
MEM_LEN = 256
N_CHIPS = 4
EPS = 1e-6
SCALE = HD ** -0.5
EXHAUSTED = -104.0
SB_FWD_HEADS = 4
SB_QB = 256
SB_WIN = 512
NEG = -1e30
VMEM_LIMIT = 56 * 1024 * 1024

ADAM_LR, ADAM_B1, ADAM_B2, ADAM_EPS, ADAM_WD, ADAM_STEP = 0.001, 0.9, 0.999, 1e-08, 0.01, 10

NN = (((1,), (0,)), ((), ()))
NT = (((1,), (1,)), ((), ()))
TN = (((0,), (0,)), ((), ()))

SHARDED = (
    ("ffn1_w1", (D, D_FF), 1), ("ffn1_w3", (D, D_FF), 1), ("ffn1_w2", (D_FF, D), 0),
    ("w_in", (D, 4096), 1), ("w_mem_kv", (D, 512), 0),
    ("w_branch_sb", (SB_W, D), 1), ("w_branch_dsa", (DSA_OUT_W, D), 1), ("w_branch_mem", (MEM_W, D), 1),
    ("w_gate", (D, 3 * D), 1), ("w_out", (D, D), 0),
    ("ffn2_w1", (D, D_FF), 1), ("ffn2_w3", (D, D_FF), 1), ("ffn2_w2", (D_FF, D), 0),
)
SHARDED_BY_NAME = {n: (sh, ax) for n, sh, ax in SHARDED}
GROUPS = {
    "ffn2": ("ffn2_w1", "ffn2_w3", "ffn2_w2"),
    "mid": ("w_in", "w_mem_kv", "w_branch_sb", "w_branch_dsa", "w_branch_mem", "w_gate", "w_out"),
    "ffn1": ("ffn1_w1", "ffn1_w3", "ffn1_w2"),
}
WEIGHT_PIECES = (
    (None, ("ffn1_w1", "ffn1_w3")),
    ("ffn1_up", ("ffn1_w2", "w_in")),
    ("ffn1_down", ("w_gate", "w_mem_kv", "w_branch_sb", "w_branch_dsa", "w_branch_mem", "w_out")),
    ("proj_dsa", ("ffn2_w2",)),
    ("proj_gate", ("ffn2_w1", "ffn2_w3")),
)
GRAD_HOSTS = {"ffn2": "ffn2_bwd_dn", "mid": "ffn1_bwd_dw13", "ffn1": "ffn1_bwd_dn"}
SMALL = (("ffn1_norm", D), ("mix_norm", D), ("mem_norm", D), ("ffn2_norm", D), ("b_gate", 3 * D),
         ("qn_dsa", HD), ("kn_dsa", HD), ("qn_mem", HD), ("kn_mem", HD))
WEIGHTS = ("ffn1_norm", "ffn1_w1", "ffn1_w3", "ffn1_w2", "mix_norm", "mem_norm", "w_in", "w_mem_kv", "qn_dsa", "kn_dsa",
           "qn_mem", "kn_mem", "w_branch_sb", "w_branch_dsa", "w_branch_mem", "w_gate", "b_gate", "w_out", "ffn2_norm",
           "ffn2_w1", "ffn2_w3", "ffn2_w2")
SMALL_ROWS = 8


def _dot(a, b, dn=NN):
    return lax.dot_general(a, b, dn, preferred_element_type=F32)


def _dot01(x, m01):
    hi = x.astype(BF16)
    r1 = x - hi.astype(F32)
    mid = r1.astype(BF16)
    lo = (r1 - mid.astype(F32)).astype(BF16)
    return _dot(hi, m01) + _dot(mid, m01) + _dot(lo, m01)


def _pick(n, cands):
    for c in cands:
        if n % c == 0:
            return c
    raise ValueError(f"no tile for {n}")


def _from_dilated(v, d, scr):
    w = v.shape[1] // d
    v = v.astype(F32)
    for c in range(d):
        for p, buf in enumerate(scr[:w // 128]):
            buf[pl.ds(c, v.shape[0], stride=d), :] = v[:, c * w + 128 * p:c * w + 128 * (p + 1)]
    return jnp.concatenate([buf[...] for buf in scr[:w // 128]], axis=1)


def _to_dilated(v, d, scr):
    w = v.shape[1]
    for p, buf in enumerate(scr[:w // 128]):
        buf[...] = v[:, 128 * p:128 * (p + 1)].astype(F32)
    return jnp.concatenate([buf[pl.ds(c, v.shape[0] // d, stride=d), :] for c in range(d) for buf in scr[:w // 128]], axis=1)


def _tokmap(name, fn, tok_ins, consts, tok_outs, acc_outs=(), tile=512, dil_ins=None, dil_outs=None, place=None):
    dil_ins, dil_outs, place = dil_ins or {}, dil_outs or {}, place or {}
    bufs = [(j, buf) for j, (_, _, buf) in place.items() if buf is not None]
    n_buf = len(bufs)
    n = tok_ins[0].shape[0] * dil_ins.get(0, 1)
    tile = _pick(n, [t for t in (512, 256, 128, 64, 32, 16, 8) if t <= tile])
    n_tin, n_in, n_tok, n_acc = len(tok_ins), len(tok_ins) + len(consts), len(tok_outs), len(acc_outs)
    n_scr = max([tok_ins[j].shape[1] // d // 128 for j, d in dil_ins.items() if d > 1]
                + [tok_outs[j][0] // 128 for j, d in dil_outs.items() if d > 1] + [0])

    def body(*refs):
        scr = refs[len(refs) - n_scr:]
        vals = [r[...] for r in refs[:n_in]]
        for j, d in dil_ins.items():
            if d > 1:
                vals[j] = _from_dilated(vals[j], d, scr)
        outs = fn(*vals)
        outs = list(outs) if isinstance(outs, (tuple, list)) else [outs]
        assert len(outs) == n_tok + n_acc, (name, len(outs))
        for j, d in dil_outs.items():
            if d > 1:
                outs[j] = _to_dilated(outs[j], d, scr)
        orefs = refs[n_in + n_buf:]
        for r, v in zip(orefs[:n_tok], outs[:n_tok]):
            r[...] = v.astype(r.dtype)
        if n_acc:
            @pl.when(pl.program_id(0) == 0)
            def _():
                for r in orefs[n_tok:n_tok + n_acc]:
                    r[...] = jnp.zeros(r.shape, r.dtype)
            for r, v in zip(orefs[n_tok:n_tok + n_acc], outs[n_tok:]):
                r[...] += v.astype(r.dtype)

    def tok_spec(width, d):
        return pl.BlockSpec((tile // d, d * width), lambda i: (i, 0))

    in_specs = [tok_spec(a.shape[1] // dil_ins.get(j, 1), dil_ins.get(j, 1)) for j, a in enumerate(tok_ins)]
    in_specs += [pl.BlockSpec(c.shape, lambda i: (0, 0)) for c in consts]
    in_specs += [HBM_SPEC] * n_buf
    out_specs = [tok_spec(w, dil_outs.get(j, 1)) for j, (w, _) in enumerate(tok_outs)]
    out_shape = [SDS((n // dil_outs.get(j, 1), w * dil_outs.get(j, 1)), dt) for j, (w, dt) in enumerate(tok_outs)]
    for j, (total, col_block, _) in place.items():
        out_specs[j] = pl.BlockSpec((tile, tok_outs[j][0]), lambda i, cb=col_block: (i, cb))
        out_shape[j] = SDS((n, total), tok_outs[j][1])
    out_specs += [pl.BlockSpec(s, lambda i: (0, 0)) for s in acc_outs]
    out_shape += [SDS(s, F32) for s in acc_outs]
    res = pl.pallas_call(
        body, name=name, grid=(n // tile,), in_specs=in_specs, out_specs=out_specs, out_shape=out_shape,
        scratch_shapes=[pltpu.VMEM((tile, 128), F32)] * n_scr,
        input_output_aliases={n_in + b: j for b, (j, _) in enumerate(bufs)},
        compiler_params=pltpu.CompilerParams(dimension_semantics=("arbitrary",), vmem_limit_bytes=VMEM_LIMIT),
    )(*tok_ins, *consts, *[buf for _, buf in bufs])
    return res


MATMUL_VMEM_BUDGET = 40 * 1024 * 1024


def _matmul_tiles(m, n, k, a_bytes, b_bytes, o_bytes, extra_bytes):
    best = None
    for tk in [c for c in (3584, 2816, 2048, 1408, 1024, 512, 256, 128) if k % c == 0]:
        for tm in [c for c in (1408, 1024, 768, 512, 256, 128) if m % c == 0]:
            for tn in [c for c in (1408, 1024, 768, 512, 256, 128) if n % c == 0]:
                need = 2 * tk * (tm * a_bytes + tn * b_bytes) + tm * tn * (2 * o_bytes + 2 * extra_bytes + 8)
                if need > MATMUL_VMEM_BUDGET:
                    continue
                score = (min(tm, 512) * min(tn, 512), tk, tm * tn, tn)
                if best is None or score > best[0]:
                    best = (score, (tm, tn, tk))
    return best[1]


class _Carry:
    def __init__(self, ins, outs, sems, copies, then):
        self.ins, self.outs, self.sems, self.copies, self.then = ins, outs, sems, copies, then


class _Plan:
    def __init__(self):
        self.pending = {}

    def put(self, host, carry):
        assert host not in self.pending, host
        self.pending[host] = carry

    def take(self, host):
        return self.pending.pop(host, None)


def _matmul(name, a, b, dn, out_dtype, epi=None, tiles=(), rows=(), plan=None):
    if dn == NN:
        (m, k), n = a.shape, b.shape[1]
    elif dn == NT:
        (m, k), n = a.shape, b.shape[0]
    else:
        (k, m), n = a.shape, b.shape[1]
    n_t, n_r = len(tiles), len(rows)
    tm, tn, tk = _matmul_tiles(m, n, k, a.dtype.itemsize, b.dtype.itemsize, jnp.dtype(out_dtype).itemsize,
                               sum(t.dtype.itemsize for t in tiles))
    nk = k // tk
    grid = (m // tm, n // tn, nk)
    carry = plan.take(name) if plan is not None else None
    n_ci, n_co = (len(carry.ins), len(carry.outs)) if carry else (0, 0)

    def body(a_ref, b_ref, *rest):
        extras, rest = rest[:n_t + n_r], rest[n_t + n_r:]
        c_in, o_ref, c_out, scratch = rest[:n_ci], rest[n_ci], rest[n_ci + 1:n_ci + 1 + n_co], rest[n_ci + 1 + n_co:]
        ids = [pl.program_id(d) for d in range(3)]
        if carry:
            sems = scratch[1:] if nk > 1 else scratch

            @pl.when((ids[0] == 0) & (ids[1] == 0) & (ids[2] == 0))
            def _():
                for cp in carry.copies(c_in, c_out, *sems):
                    cp.start()

        part = _dot(a_ref[...].astype(BF16), b_ref[...].astype(BF16), dn)

        def finish(r):
            if epi is not None:
                r = epi(r, *[e[...] for e in extras])
            o_ref[...] = r.astype(o_ref.dtype)

        if nk == 1:
            finish(part)
        else:
            acc = scratch[0]

            @pl.when(ids[2] == 0)
            def _():
                acc[...] = part

            @pl.when(ids[2] > 0)
            def _():
                acc[...] += part

            @pl.when(ids[2] == nk - 1)
            def _():
                finish(acc[...])

        if carry:
            @pl.when((ids[0] == grid[0] - 1) & (ids[1] == grid[1] - 1) & (ids[2] == nk - 1))
            def _():
                for cp in carry.copies(c_in, c_out, *sems):
                    cp.wait()

    a_spec = pl.BlockSpec((tk, tm), lambda i, j, kk: (kk, i)) if dn == TN else pl.BlockSpec((tm, tk), lambda i, j, kk: (i, kk))
    b_spec = pl.BlockSpec((tn, tk), lambda i, j, kk: (j, kk)) if dn == NT else pl.BlockSpec((tk, tn), lambda i, j, kk: (kk, j))
    in_specs = [a_spec, b_spec] + [pl.BlockSpec((tm, tn), lambda i, j, kk: (i, j)) for _ in tiles]
    in_specs += [pl.BlockSpec((1, tn), lambda i, j, kk: (0, j)) for _ in rows] + [HBM_SPEC] * n_ci
    res = pl.pallas_call(
        body, name=name, grid=grid, in_specs=in_specs,
        out_specs=[pl.BlockSpec((tm, tn), lambda i, j, kk: (i, j))] + [HBM_SPEC] * n_co,
        out_shape=[SDS((m, n), out_dtype)] + (list(carry.outs) if carry else []),
        scratch_shapes=([pltpu.VMEM((tm, tn), F32)] if nk > 1 else []) + (list(carry.sems) if carry else []),
        compiler_params=pltpu.CompilerParams(
            dimension_semantics=("arbitrary",) * 3 if carry else ("parallel", "parallel", "arbitrary"),
            vmem_limit_bytes=VMEM_LIMIT),
    )(a, b, *tiles, *rows, *(carry.ins if carry else []))
    if carry:
        carry.then(res[1:])
    return res[0]


def _mean_all(v):
    return jnp.mean(v, axis=-1, keepdims=True)


def _head_sums(v, bd):
    w = bd.shape[0]
    return jnp.concatenate([_dot01(v[:, j:j + w], bd) for j in range(0, v.shape[1], w)], axis=1)


def _mean_heads(bd):
    return lambda v: _head_sums(v, bd) * (1.0 / HD)


def _rms_fwd(x, g, mean):
    return x * lax.rsqrt(mean(x * x) + EPS) * g


def _rms_bwd(x, g, dy, mean):
    r = lax.rsqrt(mean(x * x) + EPS)
    dn = dy * g
    dx = r * dn - x * (r * r * r) * mean(dn * x)
    return dx, jnp.sum(dy * x * r, axis=0, keepdims=True)


def _swap_halves(x):
    w = x.shape[1]
    lane = lax.broadcasted_iota(jnp.int32, x.shape, 1)
    return jnp.where(lane % HD < HD // 2, pltpu.roll(x, w - HD // 2, 1), pltpu.roll(x, HD // 2, 1))


def _lanes(t, w):
    return jnp.tile(t, (1, w // t.shape[1]))


def _rope_fwd(x, cos, sin_signed):
    return x * _lanes(cos, x.shape[1]) + _swap_halves(x) * _lanes(sin_signed, x.shape[1])


def _rope_bwd(dy, cos, sin_signed):
    return dy * _lanes(cos, dy.shape[1]) + _swap_halves(dy * _lanes(sin_signed, dy.shape[1]))


def _bcast_heads(cols):
    return jnp.concatenate([jnp.broadcast_to(c, (c.shape[0], HD)) for c in cols], axis=1)


def _softplus(z):
    return jnp.maximum(z, 0.0) + jnp.log1p(jnp.exp(-jnp.abs(z)))


def _block_diag(w):
    h = np.arange(w) // HD
    return jnp.asarray(h[:, None] == h[None, :], BF16)


def _sb_window(i, t):
    hi = (i + 1) * SB_QB - t * SB_WIN
    lo = hi - SB_WIN
    ws = pl.multiple_of(jnp.maximum(lo, 0), SB_QB)
    kpos = ws + lax.broadcasted_iota(jnp.int32, (SB_QB, SB_WIN), 1)
    qpos = i * SB_QB + lax.broadcasted_iota(jnp.int32, (SB_QB, SB_WIN), 0)
    return (kpos < qpos) & (kpos >= lo) & (kpos < hi), ws


def _sb_fwd(qkv):
    s = qkv.shape[0]
    assert s >= SB_WIN
    nq = s // SB_QB
    nh = SB_FWD_HEADS
    bw = HD * nh
    ngroups = SB_W // bw

    def body(q_ref, k_ref, v_ref, later_ref, o_ref, tot_ref, nb_ref):
        p, i = pl.program_id(0), pl.program_id(1)
        q = q_ref[...]
        later_of = later_ref[...]

        def step(c):
            t, _, tots, outs = c
            mask, ws = _sb_window(i, t)
            kw, vw = k_ref[pl.ds(ws, SB_WIN), :], v_ref[pl.ds(ws, SB_WIN), :]
            new_t, new_o = [], []
            for hh in range(nh):
                sl = slice(HD * hh, HD * hh + HD)
                z = _dot(q[:, sl], kw[:, sl], NT) * SCALE
                sp = _softplus(z)
                lf = jnp.where(mask, -sp, 0.0)
                later = tots[hh] + _dot01(lf, later_of)
                w = jnp.where(mask, jnp.exp(z - sp + later), 0.0)
                new_o.append(outs[hh] + _dot(w.astype(BF16), vw[:, sl]))
                new_t.append(tots[hh] + jnp.sum(lf, axis=1, keepdims=True))
            alive = functools.reduce(jnp.maximum, [jnp.max(v) for v in new_t])
            return t + 1, alive, tuple(new_t), tuple(new_o)

        zt, zo = jnp.zeros((SB_QB, 1), F32), jnp.zeros((SB_QB, HD), F32)
        t, _, tots, outs = lax.while_loop(lambda c: ((i + 1) * SB_QB - c[0] * SB_WIN > 0) & (c[1] > EXHAUSTED), step,
                                          (jnp.int32(0), jnp.float32(0.0), (zt,) * nh, (zo,) * nh))
        o_ref[...] = jnp.concatenate(outs, axis=1).astype(o_ref.dtype)
        tot_ref[...] = _bcast_heads(tots)
        nb_ref[p, i] = t

    whole = lambda off: pl.BlockSpec((s, bw), lambda p, i: (0, off + p), pipeline_mode=pl.Buffered(1))
    tile = pl.BlockSpec((SB_QB, bw), lambda p, i: (i, p))
    tri = pl.BlockSpec((SB_WIN, SB_WIN), lambda p, i: (0, 0), pipeline_mode=pl.Buffered(1))
    idx = np.arange(SB_WIN)
    return pl.pallas_call(
        body, name="sb_fwd", grid=(ngroups, nq),
        in_specs=[tile, whole(ngroups), whole(2 * ngroups), tri],
        out_specs=[tile, tile, pl.BlockSpec(memory_space=pltpu.SMEM)],
        out_shape=[SDS((s, SB_W), BF16), SDS((s, SB_W), F32), SDS((ngroups, nq), jnp.int32)],
        compiler_params=pltpu.CompilerParams(dimension_semantics=("arbitrary", "arbitrary"), vmem_limit_bytes=VMEM_LIMIT),
    )(qkv, qkv, qkv, jnp.asarray(idx[:, None] > idx[None, :], BF16))


def _sb_bwd(qkv, do, tot, nblk, buf, col):
    s = qkv.shape[0]
    nq = s // SB_QB
    npairs = SB_W // 128

    def body(nb_ref, q_ref, k_ref, v_ref, do_ref, tot_ref, upto_ref, before_ref, buf_ref, dq_ref, dk_ref, dv_ref):
        p, i = pl.program_id(0), pl.program_id(1)

        @pl.when(i == 0)
        def _():
            dk_ref[...] = jnp.zeros(dk_ref.shape, F32)
            dv_ref[...] = jnp.zeros(dv_ref.shape, F32)

        upto = upto_ref[...]
        before = before_ref[...]
        q, dout, tt = q_ref[...], do_ref[...], tot_ref[...]
        n = nb_ref[p * 2 // SB_FWD_HEADS, i]

        def step(it, c):
            pres, gpres, dqs = c
            mask, ws = _sb_window(i, n - 1 - it)
            kw, vw = k_ref[pl.ds(ws, SB_WIN), :], v_ref[pl.ds(ws, SB_WIN), :]
            new_p, new_g, new_dq, dks, dvs = [], [], [], [], []
            for hh in range(2):
                sl = slice(HD * hh, HD * hh + HD)
                z = _dot(q[:, sl], kw[:, sl], NT) * SCALE
                sp = _softplus(z)
                lf = jnp.where(mask, -sp, 0.0)
                later = tt[:, HD * hh:HD * hh + 1] - (pres[hh] + _dot01(lf, upto))
                w = jnp.where(mask, jnp.exp(z - sp + later), 0.0)
                beta = jnp.exp(z - sp)
                g = _dot(dout[:, sl], vw[:, sl], NT) * w
                g_far = gpres[hh] + _dot(g.astype(BF16), before)
                dz = (jnp.where(mask, g * (1.0 - beta) - beta * g_far, 0.0) * SCALE).astype(BF16)
                new_dq.append(dqs[hh] + _dot(dz, kw[:, sl]))
                dks.append(_dot(dz, q[:, sl], TN))
                dvs.append(_dot(w.astype(BF16), dout[:, sl], TN))
                new_p.append(pres[hh] + jnp.sum(lf, axis=1, keepdims=True))
                new_g.append(gpres[hh] + jnp.sum(g, axis=1, keepdims=True))
            dk_ref[pl.ds(ws, SB_WIN), :] += jnp.concatenate(dks, axis=1)
            dv_ref[pl.ds(ws, SB_WIN), :] += jnp.concatenate(dvs, axis=1)
            return tuple(new_p), tuple(new_g), tuple(new_dq)

        zt, zo = jnp.zeros((SB_QB, 1), F32), jnp.zeros((SB_QB, HD), F32)
        _, _, dqs = lax.fori_loop(0, n, step, ((zt, zt), (zt, zt), (zo, zo)))
        dq_ref[...] = jnp.concatenate(dqs, axis=1).astype(dq_ref.dtype)

    whole_in = lambda off: pl.BlockSpec((s, 128), lambda p, i: (0, off + p), pipeline_mode=pl.Buffered(1))
    whole_out = pl.BlockSpec((s, 128), lambda p, i: (0, p), pipeline_mode=pl.Buffered(1))
    tile = pl.BlockSpec((SB_QB, 128), lambda p, i: (i, p))
    dq_tile = pl.BlockSpec((SB_QB, 128), lambda p, i: (i, col // 128 + p))
    tri = pl.BlockSpec((SB_WIN, SB_WIN), lambda p, i: (0, 0), pipeline_mode=pl.Buffered(1))
    idx = np.arange(SB_WIN)
    return pl.pallas_call(
        body, name="sb_bwd", grid=(npairs, nq),
        in_specs=[pl.BlockSpec(memory_space=pltpu.SMEM), tile, whole_in(npairs), whole_in(2 * npairs), tile, tile, tri, tri,
                  HBM_SPEC],
        out_specs=[dq_tile, whole_out, whole_out],
        out_shape=[SDS(buf.shape, buf.dtype)] + [SDS((s, SB_W), F32)] * 2,
        input_output_aliases={8: 0},
        compiler_params=pltpu.CompilerParams(dimension_semantics=("arbitrary", "arbitrary"), vmem_limit_bytes=VMEM_LIMIT),
    )(nblk, qkv, qkv, qkv, do, tot, jnp.asarray(idx[:, None] <= idx[None, :], BF16), jnp.asarray(idx[:, None] < idx[None, :], BF16),
      buf)


def _dsa_mask(DSA_T, has_prev):
    r = lax.broadcasted_iota(jnp.int32, (DSA_T, QB + DSA_T), 0)
    j = lax.broadcasted_iota(jnp.int32, (DSA_T, QB + DSA_T), 1) - QB
    return (j <= r) & (j >= r - QB) & ((j >= 0) | has_prev)


def _dsa_fwd(q, k, v, dil):
    n = q.shape[0]
    DSA_T = DSA_T_FWD
    nt = n // DSA_T

    def body(q_ref, kc_ref, kp_ref, vc_ref, vp_ref, o_ref, lse_ref):
        mask = _dsa_mask(DSA_T, pl.program_id(1) > 0)
        outs, lses = [], []
        for hh in range(DSA_OUT_W // HD):
            sl = slice(HD * hh, HD * hh + HD)
            kcat = jnp.concatenate([kp_ref[:, sl], kc_ref[:, sl]], axis=0)
            vcat = jnp.concatenate([vp_ref[:, sl], vc_ref[:, sl]], axis=0)
            sc = jnp.where(mask, _dot(q_ref[:, sl], kcat, NT) * SCALE, NEG)
            m = jnp.max(sc, axis=1, keepdims=True)
            p = jnp.exp(sc - m)
            den = jnp.sum(p, axis=1, keepdims=True)
            outs.append(_dot(p.astype(BF16), vcat) / den)
            lses.append(m + jnp.log(den))
        o_ref[...] = jnp.concatenate(outs, axis=1)
        lse_ref[...] = _bcast_heads(lses)

    cur = pl.BlockSpec((DSA_T, DSA_OUT_W), lambda c, i: (i, c))
    prev = pl.BlockSpec((QB, DSA_OUT_W), lambda c, i: (jnp.maximum(i * (DSA_T // QB) - 1, 0), c))
    o, lse = pl.pallas_call(
        body, name=f"dsa_fwd_d{dil}", grid=(dil, nt), in_specs=[cur, cur, prev, cur, prev], out_specs=[cur, cur],
        out_shape=[SDS((n, dil * DSA_OUT_W), F32)] * 2,
        compiler_params=pltpu.CompilerParams(dimension_semantics=("parallel", "parallel")),
    )(q, k, k, v, v)
    return o, lse


def _dsa_bwd(q, k, v, do, cc, lse, dil):
    n = q.shape[0]
    DSA_T = DSA_T_BWD
    nt = n // DSA_T
    per = DSA_T // QB

    def body(qj_ref, qn_ref, kp_ref, kj_ref, vp_ref, vj_ref, doj_ref, don_ref, cj_ref, cn_ref, lj_ref, ln_ref,
             dq_ref, dk_ref, dv_ref):
        j = pl.program_id(1)
        mask = _dsa_mask(DSA_T, j > 0)
        r = lax.broadcasted_iota(jnp.int32, (QB, DSA_T), 0)
        kk = lax.broadcasted_iota(jnp.int32, (QB, DSA_T), 1)
        m_next = (kk >= r + QB) & (j + 1 < nt)
        dqs, dks, dvs = [], [], []
        for hh in range(DSA_OUT_W // HD):
            sl = slice(HD * hh, HD * hh + HD)
            one = slice(HD * hh, HD * hh + 1)
            qj, qn, kj, vj, doj, don = (t[:, sl] for t in (qj_ref, qn_ref, kj_ref, vj_ref, doj_ref, don_ref))
            kcat = jnp.concatenate([kp_ref[:, sl], kj], axis=0)
            vcat = jnp.concatenate([vp_ref[:, sl], vj], axis=0)
            p1 = jnp.where(mask, jnp.exp(_dot(qj, kcat, NT) * SCALE - lj_ref[:, one]), 0.0)
            ds1 = (p1 * (_dot(doj, vcat, NT) + cj_ref[:, one]) * SCALE).astype(BF16)
            p2 = jnp.where(m_next, jnp.exp(_dot(qn, kj, NT) * SCALE - ln_ref[:, one]), 0.0)
            ds2 = (p2 * (_dot(don, vj, NT) + cn_ref[:, one]) * SCALE).astype(BF16)
            dqs.append(_dot(ds1, kcat))
            dks.append(_dot(ds1[:, QB:], qj, TN) + _dot(ds2, qn, TN))
            dvs.append(_dot(p1[:, QB:].astype(BF16), doj, TN) + _dot(p2.astype(BF16), don, TN))
        dq_ref[...] = jnp.concatenate(dqs, axis=1)
        dk_ref[...] = jnp.concatenate(dks, axis=1)
        dv_ref[...] = jnp.concatenate(dvs, axis=1)

    cur = pl.BlockSpec((DSA_T, DSA_OUT_W), lambda c, j: (j, c))
    prev = pl.BlockSpec((QB, DSA_OUT_W), lambda c, j: (jnp.maximum(j * per - 1, 0), c))
    nxt = pl.BlockSpec((QB, DSA_OUT_W), lambda c, j: (jnp.minimum((j + 1) * per, n // QB - 1), c))
    dq, dk, dv = pl.pallas_call(
        body, name=f"dsa_bwd_d{dil}", grid=(dil, nt),
        in_specs=[cur, nxt, prev, cur, prev, cur, cur, nxt, cur, nxt, cur, nxt], out_specs=[cur, cur, cur],
        out_shape=[SDS((n, dil * DSA_OUT_W), F32)] * 3,
        compiler_params=pltpu.CompilerParams(dimension_semantics=("parallel", "parallel")),
    )(q, q, k, k, v, v, do, do, cc, cc, lse, lse)
    return dq, dk, dv


def _ffn_fwd(tag, x, gain, w13, w2, plan=None):
    n = _tokmap(f"{tag}_norm", lambda xv, g: _rms_fwd(xv, g, _mean_all), [x], [gain], [(D, BF16)])[0]
    ab = _matmul(f"{tag}_up", n, w13, NN, BF16, plan=plan)

    def gate(abv):
        a, b = abv[:, :D_FF].astype(F32), abv[:, D_FF:].astype(F32)
        return a * jax.nn.sigmoid(a) * b

    h = _tokmap(f"{tag}_gate", gate, [ab], [], [(D_FF, BF16)], tile=256)[0]
    y = _matmul(f"{tag}_down", h, w2(), NN, F32, epi=lambda acc, res: res + 0.5 * acc, tiles=[x], plan=plan)
    return y, (n, ab, h)


def _ffn_bwd(tag, x, gain, w13, w2, saved, dy, plan=None, on_dw=None):
    n, ab, h = saved
    dh = _matmul(f"{tag}_bwd_dh", dy, w2, NT, BF16, epi=lambda acc: 0.5 * acc)

    def gate_bwd(abv, dhv):
        a, b, dhf = abv[:, :D_FF].astype(F32), abv[:, D_FF:].astype(F32), dhv.astype(F32)
        sg = jax.nn.sigmoid(a)
        da = dhf * b * (sg * (1.0 + a * (1.0 - sg)))
        return jnp.concatenate([da, dhf * (a * sg)], axis=1)

    dab = _tokmap(f"{tag}_bwd_gate", gate_bwd, [ab, dh], [], [(2 * D_FF, BF16)], tile=256)[0]
    dw2 = _matmul(f"{tag}_bwd_dw2", h, dy, TN, F32, epi=lambda acc: 0.5 * acc)
    dw13 = _matmul(f"{tag}_bwd_dw13", n, dab, TN, F32, plan=plan)
    if on_dw is not None:
        on_dw(dw13, dw2)
    dn = _matmul(f"{tag}_bwd_dn", dab, w13, NT, F32, plan=plan)

    def norm_bwd(xv, dnv, dyv, g):
        dx, dg = _rms_bwd(xv, g, dnv, _mean_all)
        return dx + dyv, dg

    dx, dgain = _tokmap(f"{tag}_bwd_norm", norm_bwd, [x, dn, dy], [gain], [(D, F32)], [(1, D)])
    return dx, dgain, dw13, dw2


def _rope_tables(s):
    half = HD // 2
    inv_freq = jnp.power(10000.0, -jnp.arange(half, dtype=F32) / half)
    ang = jnp.arange(s).astype(F32)[:, None] * inv_freq[None, :]
    cos, sin = jnp.cos(ang), jnp.sin(ang)
    return jnp.tile(jnp.concatenate([cos, cos], axis=1), (1, 2)), jnp.tile(jnp.concatenate([-sin, sin], axis=1), (1, 2))


def _local_step(x, mem, tgt, w, sm, plan=None, on_grads=None):
    s = x.shape[0]
    assert s % (max(DSA_T_FWD, DSA_T_BWD) * max(DSA_DILS)) == 0
    on_grads = on_grads or (lambda group, grads: None)
    c_sb, c_dsa, c_qm, c_all = 3 * D, 3 * D + 3 * SB_W, 3 * D + 3 * SB_W + 3 * DSA_W, 3 * D + 4096
    cos, sin = _rope_tables(s)
    bd768 = bd256 = _block_diag(128)
    gq_dsa, gk_dsa = jnp.tile(sm["qn_dsa"], (1, DSA_W // HD)), jnp.tile(sm["kn_dsa"], (1, DSA_W // HD))
    gq_mem, gk_mem = jnp.tile(sm["qn_mem"], (1, MEM_W // HD)), jnp.tile(sm["kn_mem"], (1, MEM_W // HD))

    w13_1 = jnp.concatenate([w["ffn1_w1"], w["ffn1_w3"]], axis=1)
    x1, ffn1_saved = _ffn_fwd("ffn1", x, sm["ffn1_norm"], w13_1, lambda: w["ffn1_w2"], plan)
    w_all = jnp.concatenate([w["w_gate"], w["w_in"]], axis=1)
    wb_sb, wb_dsa, wb_mem = w["w_branch_sb"], w["w_branch_dsa"], w["w_branch_mem"]
    hmix = _tokmap("mix_norm", lambda xv, g: _rms_fwd(xv, g, _mean_all), [x1], [sm["mix_norm"]], [(D, BF16)])[0]
    qkv_sb = _matmul("proj_sb", hmix, w_all[:, c_sb:c_dsa], NN, BF16)
    qkv_dsa = _matmul("proj_dsa", hmix, w_all[:, c_dsa:c_qm], NN, BF16, plan=plan)
    q_mem = _matmul("proj_qmem", hmix, w_all[:, c_qm:], NN, BF16)
    gpre = _matmul("proj_gate", hmix, w_all[:, :c_sb], NN, BF16, epi=lambda acc, b: acc + b, rows=[sm["b_gate"]], plan=plan)

    o_sb, sb_tot, sb_nblk = _sb_fwd(qkv_sb)

    def dsa_prep(qkv, cs, sn, gq, gk, bd):
        mean = _mean_heads(bd)
        qn = _rope_fwd(_rms_fwd(qkv[:, :DSA_W].astype(F32), gq, mean), cs, sn)
        kn = _rope_fwd(_rms_fwd(qkv[:, DSA_W:2 * DSA_W].astype(F32), gk, mean), cs, sn)
        v = qkv[:, 2 * DSA_W:]
        outs = []
        for t in (qn, kn, v):
            outs += [t[:, DSA_OUT_W * g:DSA_OUT_W * (g + 1)] for g in range(3)]
        return outs

    dsa_in = _tokmap("dsa_prep", dsa_prep, [qkv_dsa, cos, sin], [gq_dsa, gk_dsa, bd768], [(DSA_OUT_W, BF16)] * 9, tile=256,
                     dil_outs={j: DSA_DILS[j % 3] for j in range(9)})
    dsa_q, dsa_k, dsa_v = dsa_in[0:3], dsa_in[3:6], dsa_in[6:9]
    dsa_o, dsa_lse = zip(*[_dsa_fwd(dsa_q[g], dsa_k[g], dsa_v[g], DSA_DILS[g]) for g in range(3)])

    def alphas(l0, l1, l2):
        m = jnp.maximum(jnp.maximum(l0, l1), l2)
        e = [jnp.exp(l - m) for l in (l0, l1, l2)]
        tot = e[0] + e[1] + e[2]
        return [t / tot for t in e]

    def dsa_mix(o0, o1, o2, l0, l1, l2):
        a = alphas(l0, l1, l2)
        return a[0] * o0 + a[1] * o1 + a[2] * o2

    o_dsa = _tokmap("dsa_mix", dsa_mix, [*dsa_o, *dsa_lse], [], [(DSA_OUT_W, BF16)], tile=256,
                    dil_ins={j: DSA_DILS[j % 3] for j in range(6)})[0]

    def mem_kv(memv, g, wkv, gk, bd):
        kv = _dot(_rms_fwd(memv, g, _mean_all).astype(BF16), wkv)
        return _rms_fwd(kv[:, :MEM_W], gk, _mean_heads(bd)), kv[:, MEM_W:]

    km, vm = _tokmap("mem_kv", mem_kv, [mem], [sm["mem_norm"], w["w_mem_kv"], gk_mem, bd256], [(MEM_W, BF16)] * 2)

    def mem_probs(qv, kmv, gq, bd):
        qn = _rms_fwd(qv.astype(F32), gq, _mean_heads(bd)).astype(BF16)
        ps = []
        for h in range(MEM_W // HD):
            sl = slice(HD * h, HD * h + HD)
            sc = _dot(qn[:, sl], kmv[:, sl], NT) * SCALE
            e = jnp.exp(sc - jnp.max(sc, axis=1, keepdims=True))
            ps.append(e / jnp.sum(e, axis=1, keepdims=True))
        return qn, ps

    def mem_attn(qv, kmv, vmv, gq, bd):
        _, ps = mem_probs(qv, kmv, gq, bd)
        return jnp.concatenate([_dot(p.astype(BF16), vmv[:, HD * h:HD * h + HD]) for h, p in enumerate(ps)], axis=1)

    o_mem = _tokmap("mem_attn", mem_attn, [q_mem], [km, vm, gq_mem, bd256], [(MEM_W, BF16)])[0]

    def merge(osb, odsa, omem, gp, w_sb, w_dsa, w_mem):
        gates = jax.nn.sigmoid(gp.astype(F32))
        ys = (_dot(osb, w_sb), _dot(odsa, w_dsa), _dot(omem, w_mem))
        return gates, ys, gates[:, :D] * ys[0] + gates[:, D:2 * D] * ys[1] + gates[:, 2 * D:] * ys[2]

    merged = _tokmap("merge", lambda *a: merge(*a)[2], [o_sb, o_dsa, o_mem, gpre], [wb_sb, wb_dsa, wb_mem], [(D, BF16)],
                     tile=256)[0]
    x2 = _matmul("out_proj", merged, w["w_out"], NN, F32, epi=lambda acc, res: res + acc, tiles=[x1])
    w13_2 = jnp.concatenate([w["ffn2_w1"], w["ffn2_w3"]], axis=1)
    y, ffn2_saved = _ffn_fwd("ffn2", x2, sm["ffn2_norm"], w13_2, lambda: w["ffn2_w2"])

    def loss_fn(yv, tv):
        e = yv - tv
        part = 0.5 * jnp.sum(jnp.mean(e * e, axis=1, keepdims=True), axis=0, keepdims=True)
        return e * (1.0 / D), jnp.broadcast_to(part, (1, 128))

    dy, loss = _tokmap("loss", loss_fn, [y, tgt], [], [(D, F32)], [(1, 128)])

    gw, gs = {}, {}
    def ffn_grads(tag):
        def on_dw(dw13, dw2):
            gw[f"{tag}_w1"], gw[f"{tag}_w3"], gw[f"{tag}_w2"] = dw13[:, :D_FF], dw13[:, D_FF:], dw2
            on_grads(tag, {n: gw[n] for n in (f"{tag}_w1", f"{tag}_w3", f"{tag}_w2")})
        return on_dw

    dx2, gs["ffn2_norm"], _, _ = _ffn_bwd("ffn2", x2, sm["ffn2_norm"], w13_2, w["ffn2_w2"], ffn2_saved, dy, plan,
                                          ffn_grads("ffn2"))
    dmerged = _matmul("out_proj_bwd_dx", dx2, w["w_out"], NT, BF16)
    gw["w_out"] = _matmul("out_proj_bwd_dw", merged, dx2, TN, F32)

    def merge_bwd(osb, odsa, omem, gp, dm, w_sb, w_dsa, w_mem):
        gates, ys, _ = merge(osb, odsa, omem, gp, w_sb, w_dsa, w_mem)
        dmf = dm.astype(F32)
        dgp, dos, dws = [], [], []
        for b, (ov, wv) in enumerate(((osb, w_sb), (odsa, w_dsa), (omem, w_mem))):
            gb = gates[:, D * b:D * (b + 1)]
            dgp.append(dmf * ys[b] * gb * (1.0 - gb))
            dyb = (dmf * gb).astype(BF16)
            dos.append(_dot(dyb, wv, NT))
            dws.append(_dot(ov, dyb, TN))
        dgp = jnp.concatenate(dgp, axis=1)
        return dos[0], dos[1], dos[2], dgp, dws[0], dws[1], dws[2], jnp.sum(dgp, axis=0, keepdims=True)

    do_sb, do_dsa, do_mem, dgpre, gw["w_branch_sb"], gw["w_branch_dsa"], gw["w_branch_mem"], gs["b_gate"] = _tokmap(
        "merge_bwd", merge_bwd, [o_sb, o_dsa, o_mem, gpre, dmerged], [wb_sb, wb_dsa, wb_mem],
        [(SB_W, BF16), (DSA_OUT_W, F32), (MEM_W, BF16), (3 * D, BF16)],
        [(SB_W, D), (DSA_OUT_W, D), (MEM_W, D), (1, 3 * D)], tile=256, place={3: (c_all, 0, None)})

    dall, dk_sb, dv_sb = _sb_bwd(qkv_sb, do_sb, sb_tot, sb_nblk, dgpre, c_sb)
    dall = lax.dynamic_update_slice(dall, dk_sb.astype(BF16), (0, c_sb + SB_W))
    dall = lax.dynamic_update_slice(dall, dv_sb.astype(BF16), (0, c_sb + 2 * SB_W))

    def dsa_mix_bwd(o0, o1, o2, l0, l1, l2, dov, bd):
        a = alphas(l0, l1, l2)
        omix = a[0] * o0 + a[1] * o1 + a[2] * o2
        dot_o = _head_sums(dov * omix, bd)
        return [dov * t for t in a] + [-t * dot_o for t in a]

    mixb = _tokmap("dsa_mix_bwd", dsa_mix_bwd, [*dsa_o, *dsa_lse, do_dsa], [bd256],
                   [(DSA_OUT_W, BF16)] * 3 + [(DSA_OUT_W, F32)] * 3, tile=256,
                   dil_ins={j: DSA_DILS[j % 3] for j in range(6)}, dil_outs={j: DSA_DILS[j % 3] for j in range(6)})
    dsa_d = [_dsa_bwd(dsa_q[g], dsa_k[g], dsa_v[g], mixb[g], mixb[3 + g], dsa_lse[g], DSA_DILS[g]) for g in range(3)]

    def dsa_prep_bwd(qkv, cs, sn, *rest):
        dqs, dks, dvs, (gq, gk, bd) = rest[0:3], rest[3:6], rest[6:9], rest[9:]
        mean = _mean_heads(bd)
        dq, dgq = _rms_bwd(qkv[:, :DSA_W].astype(F32), gq, _rope_bwd(jnp.concatenate(dqs, axis=1), cs, sn), mean)
        dk, dgk = _rms_bwd(qkv[:, DSA_W:2 * DSA_W].astype(F32), gk, _rope_bwd(jnp.concatenate(dks, axis=1), cs, sn), mean)
        return jnp.concatenate([dq, dk] + list(dvs), axis=1), dgq, dgk

    dall, dgq_dsa, dgk_dsa = _tokmap(
        "dsa_prep_bwd", dsa_prep_bwd,
        [qkv_dsa, cos, sin] + [dsa_d[g][t] for t in range(3) for g in range(3)], [gq_dsa, gk_dsa, bd768],
        [(3 * DSA_W, BF16)], [(1, DSA_W), (1, DSA_W)], tile=256, dil_ins={3 + j: DSA_DILS[j % 3] for j in range(9)},
        place={0: (c_all, c_dsa // (3 * DSA_W), dall)})
    gs["qn_dsa"] = dgq_dsa.reshape(DSA_W // HD, HD).sum(axis=0, keepdims=True)
    gs["kn_dsa"] = dgk_dsa.reshape(DSA_W // HD, HD).sum(axis=0, keepdims=True)

    def mem_attn_bwd(qv, dov, kmv, vmv, gq, bd):
        qn, ps = mem_probs(qv, kmv, gq, bd)
        dqn, dkm, dvm = [], [], []
        for h, p in enumerate(ps):
            sl = slice(HD * h, HD * h + HD)
            dp = _dot(dov[:, sl], vmv[:, sl], NT)
            ds = (p * (dp - jnp.sum(p * dp, axis=1, keepdims=True)) * SCALE).astype(BF16)
            dqn.append(_dot(ds, kmv[:, sl]))
            dkm.append(_dot(ds, qn[:, sl], TN))
            dvm.append(_dot(p.astype(BF16), dov[:, sl], TN))
        dq, dgq = _rms_bwd(qv.astype(F32), gq, jnp.concatenate(dqn, axis=1), _mean_heads(bd))
        return dq, jnp.concatenate(dkm, axis=1), jnp.concatenate(dvm, axis=1), dgq

    dall, dkm, dvm, dgq_mem = _tokmap("mem_attn_bwd", mem_attn_bwd, [q_mem, do_mem], [km, vm, gq_mem, bd256],
                                      [(MEM_W, BF16)], [(MEM_LEN, MEM_W), (MEM_LEN, MEM_W), (1, MEM_W)],
                                      place={0: (c_all, c_qm // MEM_W, dall)})
    gs["qn_mem"] = dgq_mem.reshape(MEM_W // HD, HD).sum(axis=0, keepdims=True)

    def mem_kv_bwd(memv, dkmv, dvmv, g, wkv, gk, bd):
        memn = _rms_fwd(memv, g, _mean_all).astype(BF16)
        kv = _dot(memn, wkv)
        dk, dgk = _rms_bwd(kv[:, :MEM_W], gk, dkmv, _mean_heads(bd))
        dkv = jnp.concatenate([dk, dvmv], axis=1).astype(BF16)
        _, dg = _rms_bwd(memv, g, _dot(dkv, wkv, NT), _mean_all)
        return _dot(memn, dkv, TN), dg, dgk

    gw["w_mem_kv"], gs["mem_norm"], dgk_mem = _tokmap(
        "mem_kv_bwd", mem_kv_bwd, [mem, dkm, dvm], [sm["mem_norm"], w["w_mem_kv"], gk_mem, bd256], [],
        [(D, 2 * MEM_W), (1, D), (1, MEM_W)])
    gs["kn_mem"] = dgk_mem.reshape(MEM_W // HD, HD).sum(axis=0, keepdims=True)

    dhmix = _matmul("proj_bwd_dx", dall, w_all, NT, F32)
    dw_all = _matmul("proj_bwd_dw", hmix, dall, TN, F32)
    gw["w_gate"], gw["w_in"] = dw_all[:, :c_sb], dw_all[:, c_sb:]
    on_grads("mid", {n: gw[n] for n in GROUPS["mid"]})

    def mix_norm_bwd(xv, dnv, dyv, g):
        dx, dg = _rms_bwd(xv, g, dnv, _mean_all)
        return dx + dyv, dg

    dx1, gs["mix_norm"] = _tokmap("mix_norm_bwd", mix_norm_bwd, [x1, dhmix, dx2], [sm["mix_norm"]], [(D, F32)], [(1, D)])
    gx, gs["ffn1_norm"], _, _ = _ffn_bwd("ffn1", x, sm["ffn1_norm"], w13_1, w["ffn1_w2"], ffn1_saved, dx1, plan,
                                         ffn_grads("ffn1"))
    return loss, gx, gw, gs


def _shard_shape(name):
    shape, axis = SHARDED_BY_NAME[name]
    return (shape[0] // N_CHIPS, shape[1]) if axis == 0 else (shape[0], shape[1] // N_CHIPS)


def _full_from_shards(name, shards):
    axis = SHARDED_BY_NAME[name][1]
    return shards.reshape(SHARDED_BY_NAME[name][0]) if axis == 0 else jnp.concatenate(list(shards), axis=1)


def _shards_from_full(name, full, dtype):
    axis, n = SHARDED_BY_NAME[name][1], _shard_shape(name)
    return jnp.stack([lax.slice_in_dim(full, c * n[axis], (c + 1) * n[axis], axis=axis).astype(dtype) for c in range(N_CHIPS)])


def _own_shard(name, full, chip):
    axis, n = SHARDED_BY_NAME[name][1], _shard_shape(name)
    return lax.dynamic_slice_in_dim(full, chip * n[axis], n[axis], axis=axis)


SMALL_USED = sum(n for _, n in SMALL)


def _pack_small(d, loss=None):
    parts = [d[n].reshape(-1) for n, _ in SMALL]
    parts.append(jnp.zeros((1,), F32) if loss is None else loss.reshape(1))
    parts.append(jnp.zeros((SMALL_ROWS * D - SMALL_USED - 1,), F32))
    return jnp.concatenate(parts).reshape(SMALL_ROWS, D)


def _unpack_small(v):
    flat, out, r = v.reshape(-1), {}, 0
    for n, k in SMALL:
        out[n] = flat[r:r + k]
        r += k
    return out, flat[r]


def _place():
    return lax.axis_index("x"), lax.axis_index("y"), lax.axis_index("c")


def _other_chips(x, y):
    return [(1 - x, y), (x, 1 - y), (1 - x, 1 - y)]


HBM_SPEC = pl.BlockSpec(memory_space=pl.ANY)


def _chip_sems(n):
    return (pltpu.SemaphoreType.DMA((3 * n,)), pltpu.SemaphoreType.DMA((3 * n,)), pltpu.SemaphoreType.DMA((n,)))


def _gather_copies(ins, outs, send_sems, recv_sems, local_sems):
    x, y, c = _place()
    me = 2 * x + y
    copies = []
    for a, (src, out) in enumerate(zip(ins, outs)):
        copies.append(pltpu.make_async_copy(src, out.at[me], local_sems.at[a]))
        copies += [pltpu.make_async_remote_copy(src_ref=src, dst_ref=out.at[me], send_sem=send_sems.at[3 * a + k],
                                                recv_sem=recv_sems.at[3 * a + k], device_id=(px, py, c), device_id_type=MESH)
                   for k, (px, py) in enumerate(_other_chips(x, y))]
    return copies


def _scatter_copies(ins, outs, send_sems, recv_sems, local_sems):
    x, y, c = _place()
    return [pltpu.make_async_remote_copy(src_ref=src.at[2 * px + py], dst_ref=out.at[k], send_sem=send_sems.at[3 * a + k],
                                         recv_sem=recv_sems.at[3 * a + k], device_id=(px, py, c), device_id_type=MESH)
            for a, (src, out) in enumerate(zip(ins, outs)) for k, (px, py) in enumerate(_other_chips(x, y))]


def _all_gather_chips(arrays):
    n = len(arrays)

    def body(*refs):
        copies = _gather_copies(refs[:n], refs[n:2 * n], *refs[2 * n:])
        for cp in copies:
            cp.start()
        for cp in copies:
            cp.wait()

    return pl.pallas_call(
        body, name="weights_all_gather", in_specs=[HBM_SPEC] * n, out_specs=[HBM_SPEC] * n,
        out_shape=[SDS((N_CHIPS,) + a.shape, a.dtype) for a in arrays], scratch_shapes=list(_chip_sems(n)),
    )(*arrays)


def _swap_with_sibling(name, arrays):
    n = len(arrays)

    def body(*refs):
        x, y, c = _place()
        send_sems, recv_sems = refs[2 * n:]
        copies = [pltpu.make_async_remote_copy(src_ref=refs[a], dst_ref=refs[n + a], send_sem=send_sems.at[a],
                                               recv_sem=recv_sems.at[a], device_id=(x, y, 1 - c), device_id_type=MESH)
                  for a in range(n)]
        for cp in copies:
            cp.start()
        for cp in copies:
            cp.wait()

    return pl.pallas_call(
        body, name=name, in_specs=[HBM_SPEC] * n, out_specs=[HBM_SPEC] * n, out_shape=[SDS(a.shape, a.dtype) for a in arrays],
        scratch_shapes=[pltpu.SemaphoreType.DMA((n,)), pltpu.SemaphoreType.DMA((n,))],
    )(*arrays)


def _all_reduce_small(v):
    n_dev = 8

    def body(v_ref, out_ref, land, send_sems, recv_sems):
        x, y, c = _place()
        me = 4 * x + 2 * y + c
        land[me] = v_ref[...]
        copies = []
        for k in range(1, n_dev):
            peer = (x ^ (k >> 2), y ^ ((k >> 1) & 1), c ^ (k & 1))
            copies.append(pltpu.make_async_remote_copy(src_ref=v_ref, dst_ref=land.at[me], send_sem=send_sems.at[k - 1],
                                                       recv_sem=recv_sems.at[k - 1], device_id=peer, device_id_type=MESH))
        for cp in copies:
            cp.start()
        for cp in copies:
            cp.wait()
        acc = land[0]
        for d in range(1, n_dev):
            acc = acc + land[d]
        out_ref[...] = acc

    return pl.pallas_call(
        body, name="small_all_reduce", in_specs=[pl.BlockSpec(memory_space=pltpu.VMEM)],
        out_specs=pl.BlockSpec(memory_space=pltpu.VMEM), out_shape=SDS(v.shape, v.dtype),
        scratch_shapes=[pltpu.VMEM((n_dev,) + v.shape, v.dtype), pltpu.SemaphoreType.DMA((n_dev - 1,)),
                        pltpu.SemaphoreType.DMA((n_dev - 1,))],
    )(v)


def _adamw(g, wv, m, v):
    m = ADAM_B1 * m + (1.0 - ADAM_B1) * g
    v = ADAM_B2 * v + (1.0 - ADAM_B2) * (g * g)
    m_hat = m / (1.0 - ADAM_B1 ** ADAM_STEP)
    v_hat = v / (1.0 - ADAM_B2 ** ADAM_STEP)
    delta = -ADAM_LR * (m_hat / (jnp.sqrt(v_hat) + ADAM_EPS) + ADAM_WD * wv)
    return delta, m, v


def kernel(x, mem, ffn1_norm, ffn1_w1, ffn1_w3, ffn1_w2, mix_norm, mem_norm, w_in, w_mem_kv, qn_dsa, kn_dsa, qn_mem, kn_mem, w_branch_sb, w_branch_dsa, w_branch_mem, w_gate, b_gate, w_out, ffn2_norm, ffn2_w1, ffn2_w3, ffn2_w2, loss_target, m_ffn1_norm, m_ffn1_w1, m_ffn1_w3, m_ffn1_w2, m_mix_norm, m_mem_norm, m_w_in, m_w_mem_kv, m_qn_dsa, m_kn_dsa, m_qn_mem, m_kn_mem, m_w_branch_sb, m_w_branch_dsa, m_w_branch_mem, m_w_gate, m_b_gate, m_w_out, m_ffn2_norm, m_ffn2_w1, m_ffn2_w3, m_ffn2_w2, v_ffn1_norm, v_ffn1_w1, v_ffn1_w3, v_ffn1_w2, v_mix_norm, v_mem_norm, v_w_in, v_w_mem_kv, v_qn_dsa, v_kn_dsa, v_qn_mem, v_kn_mem, v_w_branch_sb, v_w_branch_dsa, v_w_branch_mem, v_w_gate, v_b_gate, v_w_out, v_ffn2_norm, v_ffn2_w1, v_ffn2_w3, v_ffn2_w2):
    given = dict(locals())
    wts = {n: given[n][0] for n in WEIGHTS}
    moms = {n: given["m_" + n][0] for n in WEIGHTS}
    vars_ = {n: given["v_" + n][0] for n in WEIGHTS}

    plan = _Plan()
    x_i, y_i, _ = _place()
    my_chip = 2 * x_i + y_i

    full = {}

    def gathered(names):
        return lambda res: full.update({n: _full_from_shards(n, g) for n, g in zip(names, res)})

    for host, names in WEIGHT_PIECES:
        shards = [wts[n].astype(BF16) for n in names]
        if host is None:
            gathered(names)(_all_gather_chips(shards))
        else:
            plan.put(host, _Carry(shards, [SDS((N_CHIPS,) + a.shape, BF16) for a in shards], _chip_sems(len(names)),
                                  _gather_copies, gathered(names)))
    small = {n: wts[n].reshape(1, -1) for n, _ in SMALL}

    landed = {}

    def on_grads(group, grads):
        names = GROUPS[group]
        slices = [_shards_from_full(n, grads[n], BF16) for n in names]
        own = [_own_shard(n, grads[n], my_chip) for n in names]
        plan.put(GRAD_HOSTS[group], _Carry(slices, [SDS((3,) + a.shape[1:], BF16) for a in slices], _chip_sems(len(names)),
                                           _scatter_copies, lambda res: landed.update({group: (own, res)})))

    loss, gx, _, gs = _local_step(x[0], mem[0], loss_target[0], full, small, plan, on_grads)
    assert not plan.pending, list(plan.pending)

    def update(hv, ov, wv, mv, vv):
        g = hv + ov
        return (g,) + _adamw(g, wv, mv, vv)

    outs = [{}, {}, {}, {}]
    for group, names in GROUPS.items():
        own, got = landed[group]
        halves = [_tokmap(f"grads_sum_chips_{n}",
                          lambda a, b0, b1, b2: ((a + b0.astype(F32)) + b1.astype(F32)) + b2.astype(F32),
                          [o, g[0], g[1], g[2]], [], [(o.shape[1], F32)])[0] for n, o, g in zip(names, own, got)]
        others = _swap_with_sibling(f"grads_swap_cores_{group}", halves)
        for n, half, other in zip(names, halves, others):
            res = _tokmap(f"adamw_{n}", update, [half, other, wts[n], moms[n], vars_[n]], [], [(half.shape[1], F32)] * 4)
            for d, r in zip(outs, res):
                d[n] = r

    s_red = _all_reduce_small(_pack_small(gs, loss[0, 0]))
    res = _tokmap(
        "adamw_small", lambda g, wv, mv, vv: (g,) + _adamw(g, wv, mv, vv),
        [s_red, _pack_small(small), _pack_small({n: moms[n] for n, _ in SMALL}), _pack_small({n: vars_[n] for n, _ in SMALL})],
        [], [(D, F32)] * 4)
    for d, packed in zip(outs, res):
        d.update(_unpack_small(packed)[0])
    _, total_loss = _unpack_small(s_red)
    return (total_loss, gx[None], *[d[n][None] for d in outs for n in WEIGHTS])
```

```python
import functools

import numpy as np
import jax
import jax.numpy as jnp
from jax import lax
from jax.experimental import pallas as pl
from jax.experimental.pallas import tpu as pltpu

F32, BF16 = jnp.float32, jnp.bfloat16
SDS = jax.ShapeDtypeStruct
MESH = pl.DeviceIdType.MESH

D = 1024
HD = 64
QB = 128
DSA_T_FWD, DSA_T_BWD = 512, 256
D_FF = 2816
SB_W, DSA_W, DSA_OUT_W, MEM_W = 512, 768, 256, 256
DSA_DILS = (1, 4, 16)
MEM_LEN = 256
N_CHIPS = 4
EPS = 1e-6
SCALE = HD ** -0.5
EXHAUSTED = -104.0
SB_FWD_HEADS = 4
SB_QB = 256
SB_WIN = 512
NEG = -1e30
VMEM_LIMIT = 56 * 1024 * 1024

ADAM_LR, ADAM_B1, ADAM_B2, ADAM_EPS, ADAM_WD, ADAM_STEP = 0.001, 0.9, 0.999, 1e-08, 0.01, 10

NN = (((1,), (0,)), ((), ()))
NT = (((1,), (1,)), ((), ()))
TN = (((0,), (0,)), ((), ()))

SHARDED = (
    ("ffn1_w1", (D, D_FF), 1), ("ffn1_w3", (D, D_FF), 1), ("ffn1_w2", (D_FF, D), 0),
    ("w_in", (D, 4096), 1), ("w_mem_kv", (D, 512), 0),
    ("w_branch_sb", (SB_W, D), 1), ("w_branch_dsa", (DSA_OUT_W, D), 1), ("w_branch_mem", (MEM_W, D), 1),
    ("w_gate", (D, 3 * D), 1), ("w_out", (D, D), 0),
    ("ffn2_w1", (D, D_FF), 1), ("ffn2_w3", (D, D_FF), 1), ("ffn2_w2", (D_FF, D), 0),
)
SHARDED_BY_NAME = {n: (sh, ax) for n, sh, ax in SHARDED}
GROUPS = {
    "ffn2": ("ffn2_w1", "ffn2_w3", "ffn2_w2"),
    "mid": ("w_in", "w_mem_kv", "w_branch_sb", "w_branch_dsa", "w_branch_mem", "w_gate", "w_out"),
    "ffn1": ("ffn1_w1", "ffn1_w3", "ffn1_w2"),
}
WEIGHT_PIECES = (
    (None, ("ffn1_w1", "ffn1_w3")),
    ("ffn1_up", ("ffn1_w2", "w_in")),
    ("ffn1_down", ("w_gate", "w_mem_kv", "w_branch_sb", "w_branch_dsa", "w_branch_mem", "w_out")),
    ("proj_dsa", ("ffn2_w2",)),
    ("proj_gate", ("ffn2_w1", "ffn2_w3")),
)
GRAD_HOSTS = {"ffn2": "ffn2_bwd_dn", "mid": "ffn1_bwd_dw13", "ffn1": "ffn1_bwd_dn"}
SMALL = (("ffn1_norm", D), ("mix_norm", D), ("mem_norm", D), ("ffn2_norm", D), ("b_gate", 3 * D),
         ("qn_dsa", HD), ("kn_dsa", HD), ("qn_mem", HD), ("kn_mem", HD))
WEIGHTS = ("ffn1_norm", "ffn1_w1", "ffn1_w3", "ffn1_w2", "mix_norm", "mem_norm", "w_in", "w_mem_kv", "qn_dsa", "kn_dsa",
           "qn_mem", "kn_mem", "w_branch_sb", "w_branch_dsa", "w_branch_mem", "w_gate", "b_gate", "w_out", "ffn2_norm",
           "ffn2_w1", "ffn2_w3", "ffn2_w2")
SMALL_ROWS = 8


def _dot(a, b, dn=NN):
    return lax.dot_general(a, b, dn, preferred_element_type=F32)


def _dot01(x, m01):
    hi = x.astype(BF16)
    r1 = x - hi.astype(F32)
    mid = r1.astype(BF16)
    lo = (r1 - mid.astype(F32)).astype(BF16)
    return _dot(hi, m01) + _dot(mid, m01) + _dot(lo, m01)


def _pick(n, cands):
    for c in cands:
        if n % c == 0:
            return c
    raise ValueError(f"no tile for {n}")


def _from_dilated(v, d, scr):
    w = v.shape[1] // d
    v = v.astype(F32)
    for c in range(d):
        for p, buf in enumerate(scr[:w // 128]):
            buf[pl.ds(c, v.shape[0], stride=d), :] = v[:, c * w + 128 * p:c * w + 128 * (p + 1)]
    return jnp.concatenate([buf[...] for buf in scr[:w // 128]], axis=1)


def _to_dilated(v, d, scr):
    w = v.shape[1]
    for p, buf in enumerate(scr[:w // 128]):
        buf[...] = v[:, 128 * p:128 * (p + 1)].astype(F32)
    return jnp.concatenate([buf[pl.ds(c, v.shape[0] // d, stride=d), :] for c in range(d) for buf in scr[:w // 128]], axis=1)


def _tokmap(name, fn, tok_ins, consts, tok_outs, acc_outs=(), tile=512, dil_ins=None, dil_outs=None, place=None):
    dil_ins, dil_outs, place = dil_ins or {}, dil_outs or {}, place or {}
    bufs = [(j, buf) for j, (_, _, buf) in place.items() if buf is not None]
    n_buf = len(bufs)
    n = tok_ins[0].shape[0] * dil_ins.get(0, 1)
    tile = _pick(n, [t for t in (512, 256, 128, 64, 32, 16, 8) if t <= tile])
    n_tin, n_in, n_tok, n_acc = len(tok_ins), len(tok_ins) + len(consts), len(tok_outs), len(acc_outs)
    n_scr = max([tok_ins[j].shape[1] // d // 128 for j, d in dil_ins.items() if d > 1]
                + [tok_outs[j][0] // 128 for j, d in dil_outs.items() if d > 1] + [0])

    def body(*refs):
        scr = refs[len(refs) - n_scr:]
        vals = [r[...] for r in refs[:n_in]]
        for j, d in dil_ins.items():
            if d > 1:
                vals[j] = _from_dilated(vals[j], d, scr)
        outs = fn(*vals)
        outs = list(outs) if isinstance(outs, (tuple, list)) else [outs]
        assert len(outs) == n_tok + n_acc, (name, len(outs))
        for j, d in dil_outs.items():
            if d > 1:
                outs[j] = _to_dilated(outs[j], d, scr)
        orefs = refs[n_in + n_buf:]
        for r, v in zip(orefs[:n_tok], outs[:n_tok]):
            r[...] = v.astype(r.dtype)
        if n_acc:
            @pl.when(pl.program_id(0) == 0)
            def _():
                for r in orefs[n_tok:n_tok + n_acc]:
                    r[...] = jnp.zeros(r.shape, r.dtype)
            for r, v in zip(orefs[n_tok:n_tok + n_acc], outs[n_tok:]):
                r[...] += v.astype(r.dtype)

    def tok_spec(width, d):
        return pl.BlockSpec((tile // d, d * width), lambda i: (i, 0))

    in_specs = [tok_spec(a.shape[1] // dil_ins.get(j, 1), dil_ins.get(j, 1)) for j, a in enumerate(tok_ins)]
    in_specs += [pl.BlockSpec(c.shape, lambda i: (0, 0)) for c in consts]
    in_specs += [HBM_SPEC] * n_buf
    out_specs = [tok_spec(w, dil_outs.get(j, 1)) for j, (w, _) in enumerate(tok_outs)]
    out_shape = [SDS((n // dil_outs.get(j, 1), w * dil_outs.get(j, 1)), dt) for j, (w, dt) in enumerate(tok_outs)]
    for j, (total, col_block, _) in place.items():
        out_specs[j] = pl.BlockSpec((tile, tok_outs[j][0]), lambda i, cb=col_block: (i, cb))
        out_shape[j] = SDS((n, total), tok_outs[j][1])
    out_specs += [pl.BlockSpec(s, lambda i: (0, 0)) for s in acc_outs]
    out_shape += [SDS(s, F32) for s in acc_outs]
    res = pl.pallas_call(
        body, name=name, grid=(n // tile,), in_specs=in_specs, out_specs=out_specs, out_shape=out_shape,
        scratch_shapes=[pltpu.VMEM((tile, 128), F32)] * n_scr,
        input_output_aliases={n_in + b: j for b, (j, _) in enumerate(bufs)},
        compiler_params=pltpu.CompilerParams(dimension_semantics=("arbitrary",), vmem_limit_bytes=VMEM_LIMIT),
    )(*tok_ins, *consts, *[buf for _, buf in bufs])
    return res


MATMUL_VMEM_BUDGET = 40 * 1024 * 1024


def _matmul_tiles(m, n, k, a_bytes, b_bytes, o_bytes, extra_bytes):
    best = None
    for tk in [c for c in (3584, 2816, 2048, 1408, 1024, 512, 256, 128) if k % c == 0]:
        for tm in [c for c in (1408, 1024, 768, 512, 256, 128) if m % c == 0]:
            for tn in [c for c in (1408, 1024, 768, 512, 256, 128) if n % c == 0]:
                need = 2 * tk * (tm * a_bytes + tn * b_bytes) + tm * tn * (2 * o_bytes + 2 * extra_bytes + 8)
                if need > MATMUL_VMEM_BUDGET:
                    continue
                score = (min(tm, 512) * min(tn, 512), tk, tm * tn, tn)
                if best is None or score > best[0]:
                    best = (score, (tm, tn, tk))
    return best[1]


class _Carry:
    def __init__(self, ins, outs, sems, copies, then):
        self.ins, self.outs, self.sems, self.copies, self.then = ins, outs, sems, copies, then


class _Plan:
    def __init__(self):
        self.pending = {}

    def put(self, host, carry):
        assert host not in self.pending, host
        self.pending[host] = carry

    def take(self, host):
        return self.pending.pop(host, None)


def _matmul(name, a, b, dn, out_dtype, epi=None, tiles=(), rows=(), plan=None):
    if dn == NN:
        (m, k), n = a.shape, b.shape[1]
    elif dn == NT:
        (m, k), n = a.shape, b.shape[0]
    else:
        (k, m), n = a.shape, b.shape[1]
    n_t, n_r = len(tiles), len(rows)
    tm, tn, tk = _matmul_tiles(m, n, k, a.dtype.itemsize, b.dtype.itemsize, jnp.dtype(out_dtype).itemsize,
                               sum(t.dtype.itemsize for t in tiles))
    nk = k // tk
    grid = (m // tm, n // tn, nk)
    carry = plan.take(name) if plan is not None else None
    n_ci, n_co = (len(carry.ins), len(carry.outs)) if carry else (0, 0)

    def body(a_ref, b_ref, *rest):
        extras, rest = rest[:n_t + n_r], rest[n_t + n_r:]
        c_in, o_ref, c_out, scratch = rest[:n_ci], rest[n_ci], rest[n_ci + 1:n_ci + 1 + n_co], rest[n_ci + 1 + n_co:]
        ids = [pl.program_id(d) for d in range(3)]
        if carry:
            sems = scratch[1:] if nk > 1 else scratch

            @pl.when((ids[0] == 0) & (ids[1] == 0) & (ids[2] == 0))
            def _():
                for cp in carry.copies(c_in, c_out, *sems):
                    cp.start()

        part = _dot(a_ref[...].astype(BF16), b_ref[...].astype(BF16), dn)

        def finish(r):
            if epi is not None:
                r = epi(r, *[e[...] for e in extras])
            o_ref[...] = r.astype(o_ref.dtype)

        if nk == 1:
            finish(part)
        else:
            acc = scratch[0]

            @pl.when(ids[2] == 0)
            def _():
                acc[...] = part

            @pl.when(ids[2] > 0)
            def _():
                acc[...] += part

            @pl.when(ids[2] == nk - 1)
            def _():
                finish(acc[...])

        if carry:
            @pl.when((ids[0] == grid[0] - 1) & (ids[1] == grid[1] - 1) & (ids[2] == nk - 1))
            def _():
                for cp in carry.copies(c_in, c_out, *sems):
                    cp.wait()

    a_spec = pl.BlockSpec((tk, tm), lambda i, j, kk: (kk, i)) if dn == TN else pl.BlockSpec((tm, tk), lambda i, j, kk: (i, kk))
    b_spec = pl.BlockSpec((tn, tk), lambda i, j, kk: (j, kk)) if dn == NT else pl.BlockSpec((tk, tn), lambda i, j, kk: (kk, j))
    in_specs = [a_spec, b_spec] + [pl.BlockSpec((tm, tn), lambda i, j, kk: (i, j)) for _ in tiles]
    in_specs += [pl.BlockSpec((1, tn), lambda i, j, kk: (0, j)) for _ in rows] + [HBM_SPEC] * n_ci
    res = pl.pallas_call(
        body, name=name, grid=grid, in_specs=in_specs,
        out_specs=[pl.BlockSpec((tm, tn), lambda i, j, kk: (i, j))] + [HBM_SPEC] * n_co,
        out_shape=[SDS((m, n), out_dtype)] + (list(carry.outs) if carry else []),
        scratch_shapes=([pltpu.VMEM((tm, tn), F32)] if nk > 1 else []) + (list(carry.sems) if carry else []),
        compiler_params=pltpu.CompilerParams(
            dimension_semantics=("arbitrary",) * 3 if carry else ("parallel", "parallel", "arbitrary"),
            vmem_limit_bytes=VMEM_LIMIT),
    )(a, b, *tiles, *rows, *(carry.ins if carry else []))
    if carry:
        carry.then(res[1:])
    return res[0]


def _mean_all(v):
    return jnp.mean(v, axis=-1, keepdims=True)


def _head_sums(v, bd):
    w = bd.shape[0]
    return jnp.concatenate([_dot01(v[:, j:j + w], bd) for j in range(0, v.shape[1], w)], axis=1)


def _mean_heads(bd):
    return lambda v: _head_sums(v, bd) * (1.0 / HD)


def _rms_fwd(x, g, mean):
    return x * lax.rsqrt(mean(x * x) + EPS) * g


def _rms_bwd(x, g, dy, mean):
    r = lax.rsqrt(mean(x * x) + EPS)
    dn = dy * g
    dx = r * dn - x * (r * r * r) * mean(dn * x)
    return dx, jnp.sum(dy * x * r, axis=0, keepdims=True)


def _swap_halves(x):
    w = x.shape[1]
    lane = lax.broadcasted_iota(jnp.int32, x.shape, 1)
    return jnp.where(lane % HD < HD // 2, pltpu.roll(x, w - HD // 2, 1), pltpu.roll(x, HD // 2, 1))


def _lanes(t, w):
    return jnp.tile(t, (1, w // t.shape[1]))


def _rope_fwd(x, cos, sin_signed):
    return x * _lanes(cos, x.shape[1]) + _swap_halves(x) * _lanes(sin_signed, x.shape[1])


def _rope_bwd(dy, cos, sin_signed):
    return dy * _lanes(cos, dy.shape[1]) + _swap_halves(dy * _lanes(sin_signed, dy.shape[1]))


def _bcast_heads(cols):
    return jnp.concatenate([jnp.broadcast_to(c, (c.shape[0], HD)) for c in cols], axis=1)


def _softplus(z):
    return jnp.maximum(z, 0.0) + jnp.log1p(jnp.exp(-jnp.abs(z)))


def _block_diag(w):
    h = np.arange(w) // HD
    return jnp.asarray(h[:, None] == h[None, :], BF16)


def _sb_window(i, t):
    hi = (i + 1) * SB_QB - t * SB_WIN
    lo = hi - SB_WIN
    ws = pl.multiple_of(jnp.maximum(lo, 0), SB_QB)
    kpos = ws + lax.broadcasted_iota(jnp.int32, (SB_QB, SB_WIN), 1)
    qpos = i * SB_QB + lax.broadcasted_iota(jnp.int32, (SB_QB, SB_WIN), 0)
    return (kpos < qpos) & (kpos >= lo) & (kpos < hi), ws


def _sb_fwd(qkv):
    s = qkv.shape[0]
    assert s >= SB_WIN
    nq = s // SB_QB
    nh = SB_FWD_HEADS
    bw = HD * nh
    ngroups = SB_W // bw

    def body(q_ref, k_ref, v_ref, later_ref, o_ref, tot_ref, w0_ref, b0_ref, nb_ref):
        p, i = pl.program_id(0), pl.program_id(1)
        q = q_ref[...]
        later_of = later_ref[...]

        def window(t, tots, outs, keep):
            mask, ws = _sb_window(i, t)
            kw, vw = k_ref[pl.ds(ws, SB_WIN), :], v_ref[pl.ds(ws, SB_WIN), :]
            new_t, new_o, w_all, b_all = [], [], [], []
            for hh in range(nh):
                sl = slice(HD * hh, HD * hh + HD)
                z = _dot(q[:, sl], kw[:, sl], NT) * SCALE
                sp = _softplus(z)
                lf = jnp.where(mask, -sp, 0.0)
                later = tots[hh] + _dot01(lf, later_of)
                w = jnp.where(mask, jnp.exp(z - sp + later), 0.0).astype(BF16)
                new_o.append(outs[hh] + _dot(w, vw[:, sl]))
                new_t.append(tots[hh] + jnp.sum(lf, axis=1, keepdims=True))
                if keep:
                    w_all.append(w)
                    b_all.append(jnp.where(mask, jnp.exp(z - sp), 0.0).astype(BF16))
            if keep:
                w0_ref[...] = jnp.concatenate(w_all, axis=1)
                b0_ref[...] = jnp.concatenate(b_all, axis=1)
            alive = functools.reduce(jnp.maximum, [jnp.max(v) for v in new_t])
            return t + 1, alive, tuple(new_t), tuple(new_o)

        zt, zo = jnp.zeros((SB_QB, 1), F32), jnp.zeros((SB_QB, HD), F32)
        first = window(jnp.int32(0), (zt,) * nh, (zo,) * nh, True)
        t, _, tots, outs = lax.while_loop(lambda c: ((i + 1) * SB_QB - c[0] * SB_WIN > 0) & (c[1] > EXHAUSTED),
                                          lambda c: window(c[0], c[2], c[3], False), first)
        o_ref[...] = jnp.concatenate(outs, axis=1).astype(o_ref.dtype)
        tot_ref[...] = _bcast_heads(tots)
        nb_ref[p, i] = t

    whole = lambda off: pl.BlockSpec((s, bw), lambda p, i: (0, off + p), pipeline_mode=pl.Buffered(1))
    tile = pl.BlockSpec((SB_QB, bw), lambda p, i: (i, p))
    tri = pl.BlockSpec((SB_WIN, SB_WIN), lambda p, i: (0, 0), pipeline_mode=pl.Buffered(1))
    near = pl.BlockSpec((SB_QB, nh * SB_WIN), lambda p, i: (i, p))
    n_heads = SB_W // HD
    idx = np.arange(SB_WIN)
    return pl.pallas_call(
        body, name="sb_fwd", grid=(ngroups, nq),
        in_specs=[tile, whole(ngroups), whole(2 * ngroups), tri],
        out_specs=[tile, tile, near, near, pl.BlockSpec(memory_space=pltpu.SMEM)],
        out_shape=[SDS((s, SB_W), BF16), SDS((s, SB_W), F32), SDS((s, n_heads * SB_WIN), BF16), SDS((s, n_heads * SB_WIN), BF16),
                   SDS((ngroups, nq), jnp.int32)],
        compiler_params=pltpu.CompilerParams(dimension_semantics=("arbitrary", "arbitrary"), vmem_limit_bytes=VMEM_LIMIT),
    )(qkv, qkv, qkv, jnp.asarray(idx[:, None] > idx[None, :], BF16))


def _sb_bwd(qkv, do, tot, nblk, w0, b0, buf, col):
    s = qkv.shape[0]
    nq = s // SB_QB
    npairs = SB_W // 128

    def body(nb_ref, q_ref, k_ref, v_ref, do_ref, tot_ref, upto_ref, before_ref, w0_ref, b0_ref, buf_ref,
             dq_ref, dk_ref, dv_ref):
        p, i = pl.program_id(0), pl.program_id(1)

        @pl.when(i == 0)
        def _():
            dk_ref[...] = jnp.zeros(dk_ref.shape, F32)
            dv_ref[...] = jnp.zeros(dv_ref.shape, F32)

        upto = upto_ref[...]
        before = before_ref[...]
        q, dout, tt = q_ref[...], do_ref[...], tot_ref[...]
        n = nb_ref[p * 2 // SB_FWD_HEADS, i]

        def step(it, c):
            pres, gpres, dqs = c
            mask, ws = _sb_window(i, n - 1 - it)
            kw, vw = k_ref[pl.ds(ws, SB_WIN), :], v_ref[pl.ds(ws, SB_WIN), :]
            new_p, new_g, new_dq, dks, dvs = [], [], [], [], []
            for hh in range(2):
                sl = slice(HD * hh, HD * hh + HD)
                z = _dot(q[:, sl], kw[:, sl], NT) * SCALE
                sp = _softplus(z)
                lf = jnp.where(mask, -sp, 0.0)
                later = tt[:, HD * hh:HD * hh + 1] - (pres[hh] + _dot01(lf, upto))
                w = jnp.where(mask, jnp.exp(z - sp + later), 0.0)
                beta = jnp.exp(z - sp)
                g = _dot(dout[:, sl], vw[:, sl], NT) * w
                g_far = gpres[hh] + _dot(g.astype(BF16), before)
                dz = (jnp.where(mask, g * (1.0 - beta) - beta * g_far, 0.0) * SCALE).astype(BF16)
                new_dq.append(dqs[hh] + _dot(dz, kw[:, sl]))
                dks.append(_dot(dz, q[:, sl], TN))
                dvs.append(_dot(w.astype(BF16), dout[:, sl], TN))
                new_p.append(pres[hh] + jnp.sum(lf, axis=1, keepdims=True))
                new_g.append(gpres[hh] + jnp.sum(g, axis=1, keepdims=True))
            dk_ref[pl.ds(ws, SB_WIN), :] += jnp.concatenate(dks, axis=1)
            dv_ref[pl.ds(ws, SB_WIN), :] += jnp.concatenate(dvs, axis=1)
            return tuple(new_p), tuple(new_g), tuple(new_dq)

        zt, zo = jnp.zeros((SB_QB, 1), F32), jnp.zeros((SB_QB, HD), F32)
        _, gpres, dqs = lax.fori_loop(0, n - 1, step, ((zt, zt), (zt, zt), (zo, zo)))
        _, ws = _sb_window(i, 0)
        kw, vw = k_ref[pl.ds(ws, SB_WIN), :], v_ref[pl.ds(ws, SB_WIN), :]
        dqs, dks, dvs = list(dqs), [], []
        for hh in range(2):
            sl = slice(HD * hh, HD * hh + HD)
            w = w0_ref[:, SB_WIN * hh:SB_WIN * (hh + 1)]
            beta = b0_ref[:, SB_WIN * hh:SB_WIN * (hh + 1)].astype(F32)
            g = _dot(dout[:, sl], vw[:, sl], NT) * w.astype(F32)
            g_far = gpres[hh] + _dot(g.astype(BF16), before)
            dz = ((g * (1.0 - beta) - beta * g_far) * SCALE).astype(BF16)
            dqs[hh] = dqs[hh] + _dot(dz, kw[:, sl])
            dks.append(_dot(dz, q[:, sl], TN))
            dvs.append(_dot(w, dout[:, sl], TN))
        dk_ref[pl.ds(ws, SB_WIN), :] += jnp.concatenate(dks, axis=1)
        dv_ref[pl.ds(ws, SB_WIN), :] += jnp.concatenate(dvs, axis=1)
        dq_ref[...] = jnp.concatenate(dqs, axis=1).astype(dq_ref.dtype)

    whole_in = lambda off: pl.BlockSpec((s, 128), lambda p, i: (0, off + p), pipeline_mode=pl.Buffered(1))
    whole_out = pl.BlockSpec((s, 128), lambda p, i: (0, p), pipeline_mode=pl.Buffered(1))
    tile = pl.BlockSpec((SB_QB, 128), lambda p, i: (i, p))
    near = pl.BlockSpec((SB_QB, 2 * SB_WIN), lambda p, i: (i, p))
    dq_tile = pl.BlockSpec((SB_QB, 128), lambda p, i: (i, col // 128 + p))
    tri = pl.BlockSpec((SB_WIN, SB_WIN), lambda p, i: (0, 0), pipeline_mode=pl.Buffered(1))
    idx = np.arange(SB_WIN)
    return pl.pallas_call(
        body, name="sb_bwd", grid=(npairs, nq),
        in_specs=[pl.BlockSpec(memory_space=pltpu.SMEM), tile, whole_in(npairs), whole_in(2 * npairs), tile, tile, tri, tri,
                  near, near, HBM_SPEC],
        out_specs=[dq_tile, whole_out, whole_out],
        out_shape=[SDS(buf.shape, buf.dtype)] + [SDS((s, SB_W), F32)] * 2,
        input_output_aliases={10: 0},
        compiler_params=pltpu.CompilerParams(dimension_semantics=("arbitrary", "arbitrary"), vmem_limit_bytes=VMEM_LIMIT),
    )(nblk, qkv, qkv, qkv, do, tot, jnp.asarray(idx[:, None] <= idx[None, :], BF16), jnp.asarray(idx[:, None] < idx[None, :], BF16),
      w0, b0, buf)


def _dsa_mask(DSA_T, has_prev):
    r = lax.broadcasted_iota(jnp.int32, (DSA_T, QB + DSA_T), 0)
    j = lax.broadcasted_iota(jnp.int32, (DSA_T, QB + DSA_T), 1) - QB
    return (j <= r) & (j >= r - QB) & ((j >= 0) | has_prev)


def _dsa_fwd(q, k, v, dil):
    n = q.shape[0]
    DSA_T = DSA_T_FWD
    nt = n // DSA_T

    def body(q_ref, kc_ref, kp_ref, vc_ref, vp_ref, o_ref, lse_ref):
        mask = _dsa_mask(DSA_T, pl.program_id(1) > 0)
        outs, lses = [], []
        for hh in range(DSA_OUT_W // HD):
            sl = slice(HD * hh, HD * hh + HD)
            kcat = jnp.concatenate([kp_ref[:, sl], kc_ref[:, sl]], axis=0)
            vcat = jnp.concatenate([vp_ref[:, sl], vc_ref[:, sl]], axis=0)
            sc = jnp.where(mask, _dot(q_ref[:, sl], kcat, NT) * SCALE, NEG)
            m = jnp.max(sc, axis=1, keepdims=True)
            p = jnp.exp(sc - m)
            den = jnp.sum(p, axis=1, keepdims=True)
            outs.append(_dot(p.astype(BF16), vcat) / den)
            lses.append(m + jnp.log(den))
        o_ref[...] = jnp.concatenate(outs, axis=1)
        lse_ref[...] = _bcast_heads(lses)

    cur = pl.BlockSpec((DSA_T, DSA_OUT_W), lambda c, i: (i, c))
    prev = pl.BlockSpec((QB, DSA_OUT_W), lambda c, i: (jnp.maximum(i * (DSA_T // QB) - 1, 0), c))
    o, lse = pl.pallas_call(
        body, name=f"dsa_fwd_d{dil}", grid=(dil, nt), in_specs=[cur, cur, prev, cur, prev], out_specs=[cur, cur],
        out_shape=[SDS((n, dil * DSA_OUT_W), F32)] * 2,
        compiler_params=pltpu.CompilerParams(dimension_semantics=("parallel", "parallel")),
    )(q, k, k, v, v)
    return o, lse


def _dsa_bwd(q, k, v, do, cc, lse, dil):
    n = q.shape[0]
    DSA_T = DSA_T_BWD
    nt = n // DSA_T
    per = DSA_T // QB

    def body(qj_ref, qn_ref, kp_ref, kj_ref, vp_ref, vj_ref, doj_ref, don_ref, cj_ref, cn_ref, lj_ref, ln_ref,
             dq_ref, dk_ref, dv_ref):
        j = pl.program_id(1)
        mask = _dsa_mask(DSA_T, j > 0)
        r = lax.broadcasted_iota(jnp.int32, (QB, DSA_T), 0)
        kk = lax.broadcasted_iota(jnp.int32, (QB, DSA_T), 1)
        m_next = (kk >= r + QB) & (j + 1 < nt)
        dqs, dks, dvs = [], [], []
        for hh in range(DSA_OUT_W // HD):
            sl = slice(HD * hh, HD * hh + HD)
            one = slice(HD * hh, HD * hh + 1)
            qj, qn, kj, vj, doj, don = (t[:, sl] for t in (qj_ref, qn_ref, kj_ref, vj_ref, doj_ref, don_ref))
            kcat = jnp.concatenate([kp_ref[:, sl], kj], axis=0)
            vcat = jnp.concatenate([vp_ref[:, sl], vj], axis=0)
            p1 = jnp.where(mask, jnp.exp(_dot(qj, kcat, NT) * SCALE - lj_ref[:, one]), 0.0)
            ds1 = (p1 * (_dot(doj, vcat, NT) + cj_ref[:, one]) * SCALE).astype(BF16)
            p2 = jnp.where(m_next, jnp.exp(_dot(qn, kj, NT) * SCALE - ln_ref[:, one]), 0.0)
            ds2 = (p2 * (_dot(don, vj, NT) + cn_ref[:, one]) * SCALE).astype(BF16)
            dqs.append(_dot(ds1, kcat))
            dks.append(_dot(ds1[:, QB:], qj, TN) + _dot(ds2, qn, TN))
            dvs.append(_dot(p1[:, QB:].astype(BF16), doj, TN) + _dot(p2.astype(BF16), don, TN))
        dq_ref[...] = jnp.concatenate(dqs, axis=1)
        dk_ref[...] = jnp.concatenate(dks, axis=1)
        dv_ref[...] = jnp.concatenate(dvs, axis=1)

    cur = pl.BlockSpec((DSA_T, DSA_OUT_W), lambda c, j: (j, c))
    prev = pl.BlockSpec((QB, DSA_OUT_W), lambda c, j: (jnp.maximum(j * per - 1, 0), c))
    nxt = pl.BlockSpec((QB, DSA_OUT_W), lambda c, j: (jnp.minimum((j + 1) * per, n // QB - 1), c))
    dq, dk, dv = pl.pallas_call(
        body, name=f"dsa_bwd_d{dil}", grid=(dil, nt),
        in_specs=[cur, nxt, prev, cur, prev, cur, cur, nxt, cur, nxt, cur, nxt], out_specs=[cur, cur, cur],
        out_shape=[SDS((n, dil * DSA_OUT_W), F32)] * 3,
        compiler_params=pltpu.CompilerParams(dimension_semantics=("parallel", "parallel")),
    )(q, q, k, k, v, v, do, do, cc, cc, lse, lse)
    return dq, dk, dv


def _ffn_fwd(tag, x, gain, w13, w2, plan=None):
    n = _tokmap(f"{tag}_norm", lambda xv, g: _rms_fwd(xv, g, _mean_all), [x], [gain], [(D, BF16)])[0]
    ab = _matmul(f"{tag}_up", n, w13, NN, BF16, plan=plan)

    def gate(abv):
        a, b = abv[:, :D_FF].astype(F32), abv[:, D_FF:].astype(F32)
        return a * jax.nn.sigmoid(a) * b

    h = _tokmap(f"{tag}_gate", gate, [ab], [], [(D_FF, BF16)], tile=256)[0]
    y = _matmul(f"{tag}_down", h, w2(), NN, F32, epi=lambda acc, res: res + 0.5 * acc, tiles=[x], plan=plan)
    return y, (n, ab, h)


def _ffn_bwd(tag, x, gain, w13, w2, saved, dy, plan=None, on_dw=None):
    n, ab, h = saved
    dh = _matmul(f"{tag}_bwd_dh", dy, w2, NT, BF16, epi=lambda acc: 0.5 * acc)

    def gate_bwd(abv, dhv):
        a, b, dhf = abv[:, :D_FF].astype(F32), abv[:, D_FF:].astype(F32), dhv.astype(F32)
        sg = jax.nn.sigmoid(a)
        da = dhf * b * (sg * (1.0 + a * (1.0 - sg)))
        return jnp.concatenate([da, dhf * (a * sg)], axis=1)

    dab = _tokmap(f"{tag}_bwd_gate", gate_bwd, [ab, dh], [], [(2 * D_FF, BF16)], tile=256)[0]
    dw2 = _matmul(f"{tag}_bwd_dw2", h, dy, TN, F32, epi=lambda acc: 0.5 * acc)
    dw13 = _matmul(f"{tag}_bwd_dw13", n, dab, TN, F32, plan=plan)
    if on_dw is not None:
        on_dw(dw13, dw2)
    dn = _matmul(f"{tag}_bwd_dn", dab, w13, NT, F32, plan=plan)

    def norm_bwd(xv, dnv, dyv, g):
        dx, dg = _rms_bwd(xv, g, dnv, _mean_all)
        return dx + dyv, dg

    dx, dgain = _tokmap(f"{tag}_bwd_norm", norm_bwd, [x, dn, dy], [gain], [(D, F32)], [(1, D)])
    return dx, dgain, dw13, dw2


def _rope_tables(s):
    half = HD // 2
    inv_freq = jnp.power(10000.0, -jnp.arange(half, dtype=F32) / half)
    ang = jnp.arange(s).astype(F32)[:, None] * inv_freq[None, :]
    cos, sin = jnp.cos(ang), jnp.sin(ang)
    return jnp.tile(jnp.concatenate([cos, cos], axis=1), (1, 2)), jnp.tile(jnp.concatenate([-sin, sin], axis=1), (1, 2))


def _local_step(x, mem, tgt, w, sm, plan=None, on_grads=None):
    s = x.shape[0]
    assert s % (max(DSA_T_FWD, DSA_T_BWD) * max(DSA_DILS)) == 0
    on_grads = on_grads or (lambda group, grads: None)
    c_sb, c_dsa, c_qm, c_all = 3 * D, 3 * D + 3 * SB_W, 3 * D + 3 * SB_W + 3 * DSA_W, 3 * D + 4096
    cos, sin = _rope_tables(s)
    bd768 = bd256 = _block_diag(128)
    gq_dsa, gk_dsa = jnp.tile(sm["qn_dsa"], (1, DSA_W // HD)), jnp.tile(sm["kn_dsa"], (1, DSA_W // HD))
    gq_mem, gk_mem = jnp.tile(sm["qn_mem"], (1, MEM_W // HD)), jnp.tile(sm["kn_mem"], (1, MEM_W // HD))

    w13_1 = jnp.concatenate([w["ffn1_w1"], w["ffn1_w3"]], axis=1)
    x1, ffn1_saved = _ffn_fwd("ffn1", x, sm["ffn1_norm"], w13_1, lambda: w["ffn1_w2"], plan)
    w_all = jnp.concatenate([w["w_gate"], w["w_in"]], axis=1)
    wb_sb, wb_dsa, wb_mem = w["w_branch_sb"], w["w_branch_dsa"], w["w_branch_mem"]
    hmix = _tokmap("mix_norm", lambda xv, g: _rms_fwd(xv, g, _mean_all), [x1], [sm["mix_norm"]], [(D, BF16)])[0]
    qkv_sb = _matmul("proj_sb", hmix, w_all[:, c_sb:c_dsa], NN, BF16)
    qkv_dsa = _matmul("proj_dsa", hmix, w_all[:, c_dsa:c_qm], NN, BF16, plan=plan)
    q_mem = _matmul("proj_qmem", hmix, w_all[:, c_qm:], NN, BF16)
    gpre = _matmul("proj_gate", hmix, w_all[:, :c_sb], NN, BF16, epi=lambda acc, b: acc + b, rows=[sm["b_gate"]], plan=plan)

    o_sb, sb_tot, sb_w0, sb_b0, sb_nblk = _sb_fwd(qkv_sb)

    def dsa_prep(qkv, cs, sn, gq, gk, bd):
        mean = _mean_heads(bd)
        qn = _rope_fwd(_rms_fwd(qkv[:, :DSA_W].astype(F32), gq, mean), cs, sn)
        kn = _rope_fwd(_rms_fwd(qkv[:, DSA_W:2 * DSA_W].astype(F32), gk, mean), cs, sn)
        v = qkv[:, 2 * DSA_W:]
        outs = []
        for t in (qn, kn, v):
            outs += [t[:, DSA_OUT_W * g:DSA_OUT_W * (g + 1)] for g in range(3)]
        return outs

    dsa_in = _tokmap("dsa_prep", dsa_prep, [qkv_dsa, cos, sin], [gq_dsa, gk_dsa, bd768], [(DSA_OUT_W, BF16)] * 9, tile=256,
                     dil_outs={j: DSA_DILS[j % 3] for j in range(9)})
    dsa_q, dsa_k, dsa_v = dsa_in[0:3], dsa_in[3:6], dsa_in[6:9]
    dsa_o, dsa_lse = zip(*[_dsa_fwd(dsa_q[g], dsa_k[g], dsa_v[g], DSA_DILS[g]) for g in range(3)])

    def alphas(l0, l1, l2):
        m = jnp.maximum(jnp.maximum(l0, l1), l2)
        e = [jnp.exp(l - m) for l in (l0, l1, l2)]
        tot = e[0] + e[1] + e[2]
        return [t / tot for t in e]

    def dsa_mix(o0, o1, o2, l0, l1, l2):
        a = alphas(l0, l1, l2)
        return a[0] * o0 + a[1] * o1 + a[2] * o2

    o_dsa = _tokmap("dsa_mix", dsa_mix, [*dsa_o, *dsa_lse], [], [(DSA_OUT_W, BF16)], tile=256,
                    dil_ins={j: DSA_DILS[j % 3] for j in range(6)})[0]

    def mem_kv(memv, g, wkv, gk, bd):
        kv = _dot(_rms_fwd(memv, g, _mean_all).astype(BF16), wkv)
        return _rms_fwd(kv[:, :MEM_W], gk, _mean_heads(bd)), kv[:, MEM_W:]

    km, vm = _tokmap("mem_kv", mem_kv, [mem], [sm["mem_norm"], w["w_mem_kv"], gk_mem, bd256], [(MEM_W, BF16)] * 2)

    def mem_probs(qv, kmv, gq, bd):
        qn = _rms_fwd(qv.astype(F32), gq, _mean_heads(bd)).astype(BF16)
        ps = []
        for h in range(MEM_W // HD):
            sl = slice(HD * h, HD * h + HD)
            sc = _dot(qn[:, sl], kmv[:, sl], NT) * SCALE
            e = jnp.exp(sc - jnp.max(sc, axis=1, keepdims=True))
            ps.append(e / jnp.sum(e, axis=1, keepdims=True))
        return qn, ps

    def mem_attn(qv, kmv, vmv, gq, bd):
        _, ps = mem_probs(qv, kmv, gq, bd)
        return jnp.concatenate([_dot(p.astype(BF16), vmv[:, HD * h:HD * h + HD]) for h, p in enumerate(ps)], axis=1)

    o_mem = _tokmap("mem_attn", mem_attn, [q_mem], [km, vm, gq_mem, bd256], [(MEM_W, BF16)])[0]

    def merge(osb, odsa, omem, gp, w_sb, w_dsa, w_mem):
        gates = jax.nn.sigmoid(gp.astype(F32))
        ys = (_dot(osb, w_sb), _dot(odsa, w_dsa), _dot(omem, w_mem))
        return gates, ys, gates[:, :D] * ys[0] + gates[:, D:2 * D] * ys[1] + gates[:, 2 * D:] * ys[2]

    merged = _tokmap("merge", lambda *a: merge(*a)[2], [o_sb, o_dsa, o_mem, gpre], [wb_sb, wb_dsa, wb_mem], [(D, BF16)],
                     tile=256)[0]
    x2 = _matmul("out_proj", merged, w["w_out"], NN, F32, epi=lambda acc, res: res + acc, tiles=[x1])
    w13_2 = jnp.concatenate([w["ffn2_w1"], w["ffn2_w3"]], axis=1)
    y, ffn2_saved = _ffn_fwd("ffn2", x2, sm["ffn2_norm"], w13_2, lambda: w["ffn2_w2"])

    def loss_fn(yv, tv):
        e = yv - tv
        part = 0.5 * jnp.sum(jnp.mean(e * e, axis=1, keepdims=True), axis=0, keepdims=True)
        return e * (1.0 / D), jnp.broadcast_to(part, (1, 128))

    dy, loss = _tokmap("loss", loss_fn, [y, tgt], [], [(D, F32)], [(1, 128)])

    gw, gs = {}, {}
    def ffn_grads(tag):
        def on_dw(dw13, dw2):
            gw[f"{tag}_w1"], gw[f"{tag}_w3"], gw[f"{tag}_w2"] = dw13[:, :D_FF], dw13[:, D_FF:], dw2
            on_grads(tag, {n: gw[n] for n in (f"{tag}_w1", f"{tag}_w3", f"{tag}_w2")})
        return on_dw

    dx2, gs["ffn2_norm"], _, _ = _ffn_bwd("ffn2", x2, sm["ffn2_norm"], w13_2, w["ffn2_w2"], ffn2_saved, dy, plan,
                                          ffn_grads("ffn2"))
    dmerged = _matmul("out_proj_bwd_dx", dx2, w["w_out"], NT, BF16)
    gw["w_out"] = _matmul("out_proj_bwd_dw", merged, dx2, TN, F32)

    def merge_bwd(osb, odsa, omem, gp, dm, w_sb, w_dsa, w_mem):
        gates, ys, _ = merge(osb, odsa, omem, gp, w_sb, w_dsa, w_mem)
        dmf = dm.astype(F32)
        dgp, dos, dws = [], [], []
        for b, (ov, wv) in enumerate(((osb, w_sb), (odsa, w_dsa), (omem, w_mem))):
            gb = gates[:, D * b:D * (b + 1)]
            dgp.append(dmf * ys[b] * gb * (1.0 - gb))
            dyb = (dmf * gb).astype(BF16)
            dos.append(_dot(dyb, wv, NT))
            dws.append(_dot(ov, dyb, TN))
        dgp = jnp.concatenate(dgp, axis=1)
        return dos[0], dos[1], dos[2], dgp, dws[0], dws[1], dws[2], jnp.sum(dgp, axis=0, keepdims=True)

    do_sb, do_dsa, do_mem, dgpre, gw["w_branch_sb"], gw["w_branch_dsa"], gw["w_branch_mem"], gs["b_gate"] = _tokmap(
        "merge_bwd", merge_bwd, [o_sb, o_dsa, o_mem, gpre, dmerged], [wb_sb, wb_dsa, wb_mem],
        [(SB_W, BF16), (DSA_OUT_W, F32), (MEM_W, BF16), (3 * D, BF16)],
        [(SB_W, D), (DSA_OUT_W, D), (MEM_W, D), (1, 3 * D)], tile=256, place={3: (c_all, 0, None)})

    dall, dk_sb, dv_sb = _sb_bwd(qkv_sb, do_sb, sb_tot, sb_nblk, sb_w0, sb_b0, dgpre, c_sb)
    dall = lax.dynamic_update_slice(dall, dk_sb.astype(BF16), (0, c_sb + SB_W))
    dall = lax.dynamic_update_slice(dall, dv_sb.astype(BF16), (0, c_sb + 2 * SB_W))

    def dsa_mix_bwd(o0, o1, o2, l0, l1, l2, dov, bd):
        a = alphas(l0, l1, l2)
        omix = a[0] * o0 + a[1] * o1 + a[2] * o2
        dot_o = _head_sums(dov * omix, bd)
        return [dov * t for t in a] + [-t * dot_o for t in a]

    mixb = _tokmap("dsa_mix_bwd", dsa_mix_bwd, [*dsa_o, *dsa_lse, do_dsa], [bd256],
                   [(DSA_OUT_W, BF16)] * 3 + [(DSA_OUT_W, F32)] * 3, tile=256,
                   dil_ins={j: DSA_DILS[j % 3] for j in range(6)}, dil_outs={j: DSA_DILS[j % 3] for j in range(6)})
    dsa_d = [_dsa_bwd(dsa_q[g], dsa_k[g], dsa_v[g], mixb[g], mixb[3 + g], dsa_lse[g], DSA_DILS[g]) for g in range(3)]

    def dsa_prep_bwd(qkv, cs, sn, *rest):
        dqs, dks, dvs, (gq, gk, bd) = rest[0:3], rest[3:6], rest[6:9], rest[9:]
        mean = _mean_heads(bd)
        dq, dgq = _rms_bwd(qkv[:, :DSA_W].astype(F32), gq, _rope_bwd(jnp.concatenate(dqs, axis=1), cs, sn), mean)
        dk, dgk = _rms_bwd(qkv[:, DSA_W:2 * DSA_W].astype(F32), gk, _rope_bwd(jnp.concatenate(dks, axis=1), cs, sn), mean)
        return jnp.concatenate([dq, dk] + list(dvs), axis=1), dgq, dgk

    dall, dgq_dsa, dgk_dsa = _tokmap(
        "dsa_prep_bwd", dsa_prep_bwd,
        [qkv_dsa, cos, sin] + [dsa_d[g][t] for t in range(3) for g in range(3)], [gq_dsa, gk_dsa, bd768],
        [(3 * DSA_W, BF16)], [(1, DSA_W), (1, DSA_W)], tile=256, dil_ins={3 + j: DSA_DILS[j % 3] for j in range(9)},
        place={0: (c_all, c_dsa // (3 * DSA_W), dall)})
    gs["qn_dsa"] = dgq_dsa.reshape(DSA_W // HD, HD).sum(axis=0, keepdims=True)
    gs["kn_dsa"] = dgk_dsa.reshape(DSA_W // HD, HD).sum(axis=0, keepdims=True)

    def mem_attn_bwd(qv, dov, kmv, vmv, gq, bd):
        qn, ps = mem_probs(qv, kmv, gq, bd)
        dqn, dkm, dvm = [], [], []
        for h, p in enumerate(ps):
            sl = slice(HD * h, HD * h + HD)
            dp = _dot(dov[:, sl], vmv[:, sl], NT)
            ds = (p * (dp - jnp.sum(p * dp, axis=1, keepdims=True)) * SCALE).astype(BF16)
            dqn.append(_dot(ds, kmv[:, sl]))
            dkm.append(_dot(ds, qn[:, sl], TN))
            dvm.append(_dot(p.astype(BF16), dov[:, sl], TN))
        dq, dgq = _rms_bwd(qv.astype(F32), gq, jnp.concatenate(dqn, axis=1), _mean_heads(bd))
        return dq, jnp.concatenate(dkm, axis=1), jnp.concatenate(dvm, axis=1), dgq

    dall, dkm, dvm, dgq_mem = _tokmap("mem_attn_bwd", mem_attn_bwd, [q_mem, do_mem], [km, vm, gq_mem, bd256],
                                      [(MEM_W, BF16)], [(MEM_LEN, MEM_W), (MEM_LEN, MEM_W), (1, MEM_W)],
                                      place={0: (c_all, c_qm // MEM_W, dall)})
    gs["qn_mem"] = dgq_mem.reshape(MEM_W // HD, HD).sum(axis=0, keepdims=True)

    def mem_kv_bwd(memv, dkmv, dvmv, g, wkv, gk, bd):
        memn = _rms_fwd(memv, g, _mean_all).astype(BF16)
        kv = _dot(memn, wkv)
        dk, dgk = _rms_bwd(kv[:, :MEM_W], gk, dkmv, _mean_heads(bd))
        dkv = jnp.concatenate([dk, dvmv], axis=1).astype(BF16)
        _, dg = _rms_bwd(memv, g, _dot(dkv, wkv, NT), _mean_all)
        return _dot(memn, dkv, TN), dg, dgk

    gw["w_mem_kv"], gs["mem_norm"], dgk_mem = _tokmap(
        "mem_kv_bwd", mem_kv_bwd, [mem, dkm, dvm], [sm["mem_norm"], w["w_mem_kv"], gk_mem, bd256], [],
        [(D, 2 * MEM_W), (1, D), (1, MEM_W)])
    gs["kn_mem"] = dgk_mem.reshape(MEM_W // HD, HD).sum(axis=0, keepdims=True)

    dhmix = _matmul("proj_bwd_dx", dall, w_all, NT, F32)
    dw_all = _matmul("proj_bwd_dw", hmix, dall, TN, F32)
    gw["w_gate"], gw["w_in"] = dw_all[:, :c_sb], dw_all[:, c_sb:]
    on_grads("mid", {n: gw[n] for n in GROUPS["mid"]})

    def mix_norm_bwd(xv, dnv, dyv, g):
        dx, dg = _rms_bwd(xv, g, dnv, _mean_all)
        return dx + dyv, dg

    dx1, gs["mix_norm"] = _tokmap("mix_norm_bwd", mix_norm_bwd, [x1, dhmix, dx2], [sm["mix_norm"]], [(D, F32)], [(1, D)])
    gx, gs["ffn1_norm"], _, _ = _ffn_bwd("ffn1", x, sm["ffn1_norm"], w13_1, w["ffn1_w2"], ffn1_saved, dx1, plan,
                                         ffn_grads("ffn1"))
    return loss, gx, gw, gs


def _shard_shape(name):
    shape, axis = SHARDED_BY_NAME[name]
    return (shape[0] // N_CHIPS, shape[1]) if axis == 0 else (shape[0], shape[1] // N_CHIPS)


def _full_from_shards(name, shards):
    axis = SHARDED_BY_NAME[name][1]
    return shards.reshape(SHARDED_BY_NAME[name][0]) if axis == 0 else jnp.concatenate(list(shards), axis=1)


def _shards_from_full(name, full, dtype):
    axis, n = SHARDED_BY_NAME[name][1], _shard_shape(name)
    return jnp.stack([lax.slice_in_dim(full, c * n[axis], (c + 1) * n[axis], axis=axis).astype(dtype) for c in range(N_CHIPS)])


def _own_shard(name, full, chip):
    axis, n = SHARDED_BY_NAME[name][1], _shard_shape(name)
    return lax.dynamic_slice_in_dim(full, chip * n[axis], n[axis], axis=axis)


SMALL_USED = sum(n for _, n in SMALL)


def _pack_small(d, loss=None):
    parts = [d[n].reshape(-1) for n, _ in SMALL]
    parts.append(jnp.zeros((1,), F32) if loss is None else loss.reshape(1))
    parts.append(jnp.zeros((SMALL_ROWS * D - SMALL_USED - 1,), F32))
    return jnp.concatenate(parts).reshape(SMALL_ROWS, D)


def _unpack_small(v):
    flat, out, r = v.reshape(-1), {}, 0
    for n, k in SMALL:
        out[n] = flat[r:r + k]
        r += k
    return out, flat[r]


def _place():
    return lax.axis_index("x"), lax.axis_index("y"), lax.axis_index("c")


def _other_chips(x, y):
    return [(1 - x, y), (x, 1 - y), (1 - x, 1 - y)]


HBM_SPEC = pl.BlockSpec(memory_space=pl.ANY)


def _chip_sems(n):
    return (pltpu.SemaphoreType.DMA((3 * n,)), pltpu.SemaphoreType.DMA((3 * n,)), pltpu.SemaphoreType.DMA((n,)))


def _gather_copies(ins, outs, send_sems, recv_sems, local_sems):
    x, y, c = _place()
    me = 2 * x + y
    copies = []
    for a, (src, out) in enumerate(zip(ins, outs)):
        copies.append(pltpu.make_async_copy(src, out.at[me], local_sems.at[a]))
        copies += [pltpu.make_async_remote_copy(src_ref=src, dst_ref=out.at[me], send_sem=send_sems.at[3 * a + k],
                                                recv_sem=recv_sems.at[3 * a + k], device_id=(px, py, c), device_id_type=MESH)
                   for k, (px, py) in enumerate(_other_chips(x, y))]
    return copies


def _scatter_copies(ins, outs, send_sems, recv_sems, local_sems):
    x, y, c = _place()
    return [pltpu.make_async_remote_copy(src_ref=src.at[2 * px + py], dst_ref=out.at[k], send_sem=send_sems.at[3 * a + k],
                                         recv_sem=recv_sems.at[3 * a + k], device_id=(px, py, c), device_id_type=MESH)
            for a, (src, out) in enumerate(zip(ins, outs)) for k, (px, py) in enumerate(_other_chips(x, y))]


def _all_gather_chips(arrays):
    n = len(arrays)

    def body(*refs):
        copies = _gather_copies(refs[:n], refs[n:2 * n], *refs[2 * n:])
        for cp in copies:
            cp.start()
        for cp in copies:
            cp.wait()

    return pl.pallas_call(
        body, name="weights_all_gather", in_specs=[HBM_SPEC] * n, out_specs=[HBM_SPEC] * n,
        out_shape=[SDS((N_CHIPS,) + a.shape, a.dtype) for a in arrays], scratch_shapes=list(_chip_sems(n)),
    )(*arrays)


def _swap_with_sibling(name, arrays):
    n = len(arrays)

    def body(*refs):
        x, y, c = _place()
        send_sems, recv_sems = refs[2 * n:]
        copies = [pltpu.make_async_remote_copy(src_ref=refs[a], dst_ref=refs[n + a], send_sem=send_sems.at[a],
                                               recv_sem=recv_sems.at[a], device_id=(x, y, 1 - c), device_id_type=MESH)
                  for a in range(n)]
        for cp in copies:
            cp.start()
        for cp in copies:
            cp.wait()

    return pl.pallas_call(
        body, name=name, in_specs=[HBM_SPEC] * n, out_specs=[HBM_SPEC] * n, out_shape=[SDS(a.shape, a.dtype) for a in arrays],
        scratch_shapes=[pltpu.SemaphoreType.DMA((n,)), pltpu.SemaphoreType.DMA((n,))],
    )(*arrays)


def _all_reduce_small(v):
    n_dev = 8

    def body(v_ref, out_ref, land, send_sems, recv_sems):
        x, y, c = _place()
        me = 4 * x + 2 * y + c
        land[me] = v_ref[...]
        copies = []
        for k in range(1, n_dev):
            peer = (x ^ (k >> 2), y ^ ((k >> 1) & 1), c ^ (k & 1))
            copies.append(pltpu.make_async_remote_copy(src_ref=v_ref, dst_ref=land.at[me], send_sem=send_sems.at[k - 1],
                                                       recv_sem=recv_sems.at[k - 1], device_id=peer, device_id_type=MESH))
        for cp in copies:
            cp.start()
        for cp in copies:
            cp.wait()
        acc = land[0]
        for d in range(1, n_dev):
            acc = acc + land[d]
        out_ref[...] = acc

    return pl.pallas_call(
        body, name="small_all_reduce", in_specs=[pl.BlockSpec(memory_space=pltpu.VMEM)],
        out_specs=pl.BlockSpec(memory_space=pltpu.VMEM), out_shape=SDS(v.shape, v.dtype),
        scratch_shapes=[pltpu.VMEM((n_dev,) + v.shape, v.dtype), pltpu.SemaphoreType.DMA((n_dev - 1,)),
                        pltpu.SemaphoreType.DMA((n_dev - 1,))],
    )(v)


def _adamw(g, wv, m, v):
    m = ADAM_B1 * m + (1.0 - ADAM_B1) * g
    v = ADAM_B2 * v + (1.0 - ADAM_B2) * (g * g)
    m_hat = m / (1.0 - ADAM_B1 ** ADAM_STEP)
    v_hat = v / (1.0 - ADAM_B2 ** ADAM_STEP)
    delta = -ADAM_LR * (m_hat / (jnp.sqrt(v_hat) + ADAM_EPS) + ADAM_WD * wv)
    return delta, m, v


def kernel(x, mem, ffn1_norm, ffn1_w1, ffn1_w3, ffn1_w2, mix_norm, mem_norm, w_in, w_mem_kv, qn_dsa, kn_dsa, qn_mem, kn_mem, w_branch_sb, w_branch_dsa, w_branch_mem, w_gate, b_gate, w_out, ffn2_norm, ffn2_w1, ffn2_w3, ffn2_w2, loss_target, m_ffn1_norm, m_ffn1_w1, m_ffn1_w3, m_ffn1_w2, m_mix_norm, m_mem_norm, m_w_in, m_w_mem_kv, m_qn_dsa, m_kn_dsa, m_qn_mem, m_kn_mem, m_w_branch_sb, m_w_branch_dsa, m_w_branch_mem, m_w_gate, m_b_gate, m_w_out, m_ffn2_norm, m_ffn2_w1, m_ffn2_w3, m_ffn2_w2, v_ffn1_norm, v_ffn1_w1, v_ffn1_w3, v_ffn1_w2, v_mix_norm, v_mem_norm, v_w_in, v_w_mem_kv, v_qn_dsa, v_kn_dsa, v_qn_mem, v_kn_mem, v_w_branch_sb, v_w_branch_dsa, v_w_branch_mem, v_w_gate, v_b_gate, v_w_out, v_ffn2_norm, v_ffn2_w1, v_ffn2_w3, v_ffn2_w2):
    given = dict(locals())
    wts = {n: given[n][0] for n in WEIGHTS}
    moms = {n: given["m_" + n][0] for n in WEIGHTS}
    vars_ = {n: given["v_" + n][0] for n in WEIGHTS}

    plan = _Plan()
    x_i, y_i, _ = _place()
    my_chip = 2 * x_i + y_i

    full = {}

    def gathered(names):
        return lambda res: full.update({n: _full_from_shards(n, g) for n, g in zip(names, res)})

    for host, names in WEIGHT_PIECES:
        shards = [wts[n].astype(BF16) for n in names]
        if host is None:
            gathered(names)(_all_gather_chips(shards))
        else:
            plan.put(host, _Carry(shards, [SDS((N_CHIPS,) + a.shape, BF16) for a in shards], _chip_sems(len(names)),
                                  _gather_copies, gathered(names)))
    small = {n: wts[n].reshape(1, -1) for n, _ in SMALL}

    landed = {}

    def on_grads(group, grads):
        names = GROUPS[group]
        slices = [_shards_from_full(n, grads[n], BF16) for n in names]
        own = [_own_shard(n, grads[n], my_chip) for n in names]
        plan.put(GRAD_HOSTS[group], _Carry(slices, [SDS((3,) + a.shape[1:], BF16) for a in slices], _chip_sems(len(names)),
                                           _scatter_copies, lambda res: landed.update({group: (own, res)})))

    loss, gx, _, gs = _local_step(x[0], mem[0], loss_target[0], full, small, plan, on_grads)
    assert not plan.pending, list(plan.pending)

    def update(hv, ov, wv, mv, vv):
        g = hv + ov
        return (g,) + _adamw(g, wv, mv, vv)

    outs = [{}, {}, {}, {}]
    for group, names in GROUPS.items():
        own, got = landed[group]
        halves = [_tokmap(f"grads_sum_chips_{n}",
                          lambda a, b0, b1, b2: ((a + b0.astype(F32)) + b1.astype(F32)) + b2.astype(F32),
                          [o, g[0], g[1], g[2]], [], [(o.shape[1], F32)])[0] for n, o, g in zip(names, own, got)]
        others = _swap_with_sibling(f"grads_swap_cores_{group}", halves)
        for n, half, other in zip(names, halves, others):
            res = _tokmap(f"adamw_{n}", update, [half, other, wts[n], moms[n], vars_[n]], [], [(half.shape[1], F32)] * 4)
            for d, r in zip(outs, res):
                d[n] = r

    s_red = _all_reduce_small(_pack_small(gs, loss[0, 0]))
    res = _tokmap(
        "adamw_small", lambda g, wv, mv, vv: (g,) + _adamw(g, wv, mv, vv),
        [s_red, _pack_small(small), _pack_small({n: moms[n] for n, _ in SMALL}), _pack_small({n: vars_[n] for n, _ in SMALL})],
        [], [(D, F32)] * 4)
    for d, packed in zip(outs, res):
        d.update(_unpack_small(packed)[0])
    _, total_loss = _unpack_small(s_red)
    return (total_loss, gx[None], *[d[n][None] for d in outs for n in WEIGHTS])
```

```python
import functools

import numpy as np
import jax
import jax.numpy as jnp
from jax import lax
from jax.experimental import pallas as pl
from jax.experimental.pallas import tpu as pltpu

F32, BF16 = jnp.float32, jnp.bfloat16
SDS = jax.ShapeDtypeStruct
MESH = pl.DeviceIdType.MESH

D = 1024
HD = 64
QB = 128
DSA_T_FWD, DSA_T_BWD = 512, 256
D_FF = 2816
SB_W, DSA_W, DSA_OUT_W, MEM_W = 512, 768, 256, 256
DSA_DILS = (1, 4, 16)
MEM_LEN = 256
N_CHIPS = 4
EPS = 1e-6
SCALE = HD ** -0.5
EXHAUSTED = -104.0
SB_FWD_HEADS = 4
SB_QB = 256
SB_WIN = 512
NEG = -1e30
VMEM_LIMIT = 56 * 1024 * 1024

ADAM_LR, ADAM_B1, ADAM_B2, ADAM_EPS, ADAM_WD, ADAM_STEP = 0.001, 0.9, 0.999, 1e-08, 0.01, 10

NN = (((1,), (0,)), ((), ()))
NT = (((1,), (1,)), ((), ()))
TN = (((0,), (0,)), ((), ()))

SHARDED = (
    ("ffn1_w1", (D, D_FF), 1), ("ffn1_w3", (D, D_FF), 1), ("ffn1_w2", (D_FF, D), 0),
    ("w_in", (D, 4096), 1), ("w_mem_kv", (D, 512), 0),
    ("w_branch_sb", (SB_W, D), 1), ("w_branch_dsa", (DSA_OUT_W, D), 1), ("w_branch_mem", (MEM_W, D), 1),
    ("w_gate", (D, 3 * D), 1), ("w_out", (D, D), 0),
    ("ffn2_w1", (D, D_FF), 1), ("ffn2_w3", (D, D_FF), 1), ("ffn2_w2", (D_FF, D), 0),
)
SHARDED_BY_NAME = {n: (sh, ax) for n, sh, ax in SHARDED}
GROUPS = {
    "ffn2": ("ffn2_w1", "ffn2_w3", "ffn2_w2"),
    "mid": ("w_in", "w_mem_kv", "w_branch_sb", "w_branch_dsa", "w_branch_mem", "w_gate", "w_out"),
    "ffn1": ("ffn1_w1", "ffn1_w3", "ffn1_w2"),
}
WEIGHT_PIECES = (
    (None, ("ffn1_w1", "ffn1_w3")),
    ("ffn1_up", ("ffn1_w2", "w_in")),
    ("ffn1_down", ("w_gate", "w_mem_kv", "w_branch_sb", "w_branch_dsa", "w_branch_mem", "w_out")),
    ("proj_dsa", ("ffn2_w2",)),
    ("proj_gate", ("ffn2_w1", "ffn2_w3")),
)
GRAD_HOSTS = {"ffn2": "ffn2_bwd_dn", "mid": "ffn1_bwd_dw13", "ffn1": "ffn1_bwd_dn"}
SMALL = (("ffn1_norm", D), ("mix_norm", D), ("mem_norm", D), ("ffn2_norm", D), ("b_gate", 3 * D),
         ("qn_dsa", HD), ("kn_dsa", HD), ("qn_mem", HD), ("kn_mem", HD))
WEIGHTS = ("ffn1_norm", "ffn1_w1", "ffn1_w3", "ffn1_w2", "mix_norm", "mem_norm", "w_in", "w_mem_kv", "qn_dsa", "kn_dsa",
           "qn_mem", "kn_mem", "w_branch_sb", "w_branch_dsa", "w_branch_mem", "w_gate", "b_gate", "w_out", "ffn2_norm",
           "ffn2_w1", "ffn2_w3", "ffn2_w2")
SMALL_ROWS = 8


def _dot(a, b, dn=NN):
    return lax.dot_general(a, b, dn, preferred_element_type=F32)


def _dot01(x, m01, pieces=3):
    hi = x.astype(BF16)
    r1 = x - hi.astype(F32)
    mid = r1.astype(BF16)
    if pieces == 2:
        return _dot(hi, m01) + _dot(mid, m01)
    lo = (r1 - mid.astype(F32)).astype(BF16)
    return _dot(hi, m01) + _dot(mid, m01) + _dot(lo, m01)


def _pick(n, cands):
    for c in cands:
        if n % c == 0:
            return c
    raise ValueError(f"no tile for {n}")


def _from_dilated(v, d, scr):
    w = v.shape[1] // d
    v = v.astype(F32)
    for c in range(d):
        for p, buf in enumerate(scr[:w // 128]):
            buf[pl.ds(c, v.shape[0], stride=d), :] = v[:, c * w + 128 * p:c * w + 128 * (p + 1)]
    return jnp.concatenate([buf[...] for buf in scr[:w // 128]], axis=1)


def _to_dilated(v, d, scr):
    w = v.shape[1]
    for p, buf in enumerate(scr[:w // 128]):
        buf[...] = v[:, 128 * p:128 * (p + 1)].astype(F32)
    return jnp.concatenate([buf[pl.ds(c, v.shape[0] // d, stride=d), :] for c in range(d) for buf in scr[:w // 128]], axis=1)


def _tokmap(name, fn, tok_ins, consts, tok_outs, acc_outs=(), tile=512, dil_ins=None, dil_outs=None, place=None):
    dil_ins, dil_outs, place = dil_ins or {}, dil_outs or {}, place or {}
    bufs = [(j, buf) for j, (_, _, buf) in place.items() if buf is not None]
    n_buf = len(bufs)
    n = tok_ins[0].shape[0] * dil_ins.get(0, 1)
    tile = _pick(n, [t for t in (512, 256, 128, 64, 32, 16, 8) if t <= tile])
    n_tin, n_in, n_tok, n_acc = len(tok_ins), len(tok_ins) + len(consts), len(tok_outs), len(acc_outs)
    n_scr = max([tok_ins[j].shape[1] // d // 128 for j, d in dil_ins.items() if d > 1]
                + [tok_outs[j][0] // 128 for j, d in dil_outs.items() if d > 1] + [0])

    def body(*refs):
        scr = refs[len(refs) - n_scr:]
        vals = [r[...] for r in refs[:n_in]]
        for j, d in dil_ins.items():
            if d > 1:
                vals[j] = _from_dilated(vals[j], d, scr)
        outs = fn(*vals)
        outs = list(outs) if isinstance(outs, (tuple, list)) else [outs]
        assert len(outs) == n_tok + n_acc, (name, len(outs))
        for j, d in dil_outs.items():
            if d > 1:
                outs[j] = _to_dilated(outs[j], d, scr)
        orefs = refs[n_in + n_buf:]
        for r, v in zip(orefs[:n_tok], outs[:n_tok]):
            r[...] = v.astype(r.dtype)
        if n_acc:
            @pl.when(pl.program_id(0) == 0)
            def _():
                for r in orefs[n_tok:n_tok + n_acc]:
                    r[...] = jnp.zeros(r.shape, r.dtype)
            for r, v in zip(orefs[n_tok:n_tok + n_acc], outs[n_tok:]):
                r[...] += v.astype(r.dtype)

    def tok_spec(width, d):
        return pl.BlockSpec((tile // d, d * width), lambda i: (i, 0))

    in_specs = [tok_spec(a.shape[1] // dil_ins.get(j, 1), dil_ins.get(j, 1)) for j, a in enumerate(tok_ins)]
    in_specs += [pl.BlockSpec(c.shape, lambda i: (0, 0)) for c in consts]
    in_specs += [HBM_SPEC] * n_buf
    out_specs = [tok_spec(w, dil_outs.get(j, 1)) for j, (w, _) in enumerate(tok_outs)]
    out_shape = [SDS((n // dil_outs.get(j, 1), w * dil_outs.get(j, 1)), dt) for j, (w, dt) in enumerate(tok_outs)]
    for j, (total, col_block, _) in place.items():
        out_specs[j] = pl.BlockSpec((tile, tok_outs[j][0]), lambda i, cb=col_block: (i, cb))
        out_shape[j] = SDS((n, total), tok_outs[j][1])
    out_specs += [pl.BlockSpec(s, lambda i: (0, 0)) for s in acc_outs]
    out_shape += [SDS(s, F32) for s in acc_outs]
    res = pl.pallas_call(
        body, name=name, grid=(n // tile,), in_specs=in_specs, out_specs=out_specs, out_shape=out_shape,
        scratch_shapes=[pltpu.VMEM((tile, 128), F32)] * n_scr,
        input_output_aliases={n_in + b: j for b, (j, _) in enumerate(bufs)},
        compiler_params=pltpu.CompilerParams(dimension_semantics=("arbitrary",), vmem_limit_bytes=VMEM_LIMIT),
    )(*tok_ins, *consts, *[buf for _, buf in bufs])
    return res


MATMUL_VMEM_BUDGET = 40 * 1024 * 1024


def _matmul_tiles(m, n, k, a_bytes, b_bytes, o_bytes, extra_bytes):
    best = None
    for tk in [c for c in (3584, 2816, 2048, 1408, 1024, 512, 256, 128) if k % c == 0]:
        for tm in [c for c in (1408, 1024, 768, 512, 256, 128) if m % c == 0]:
            for tn in [c for c in (1408, 1024, 768, 512, 256, 128) if n % c == 0]:
                need = 2 * tk * (tm * a_bytes + tn * b_bytes) + tm * tn * (2 * o_bytes + 2 * extra_bytes + 8)
                if need > MATMUL_VMEM_BUDGET:
                    continue
                score = (min(tm, 512) * min(tn, 512), tk, tm * tn, tn)
                if best is None or score > best[0]:
                    best = (score, (tm, tn, tk))
    return best[1]


class _Carry:
    def __init__(self, ins, outs, sems, copies, then):
        self.ins, self.outs, self.sems, self.copies, self.then = ins, outs, sems, copies, then


class _Plan:
    def __init__(self):
        self.pending = {}

    def put(self, host, carry):
        assert host not in self.pending, host
        self.pending[host] = carry

    def take(self, host):
        return self.pending.pop(host, None)


def _matmul(name, a, b, dn, out_dtype, epi=None, tiles=(), rows=(), plan=None):
    if dn == NN:
        (m, k), n = a.shape, b.shape[1]
    elif dn == NT:
        (m, k), n = a.shape, b.shape[0]
    else:
        (k, m), n = a.shape, b.shape[1]
    n_t, n_r = len(tiles), len(rows)
    tm, tn, tk = _matmul_tiles(m, n, k, a.dtype.itemsize, b.dtype.itemsize, jnp.dtype(out_dtype).itemsize,
                               sum(t.dtype.itemsize for t in tiles))
    nk = k // tk
    grid = (m // tm, n // tn, nk)
    carry = plan.take(name) if plan is not None else None
    n_ci, n_co = (len(carry.ins), len(carry.outs)) if carry else (0, 0)

    def body(a_ref, b_ref, *rest):
        extras, rest = rest[:n_t + n_r], rest[n_t + n_r:]
        c_in, o_ref, c_out, scratch = rest[:n_ci], rest[n_ci], rest[n_ci + 1:n_ci + 1 + n_co], rest[n_ci + 1 + n_co:]
        ids = [pl.program_id(d) for d in range(3)]
        if carry:
            sems = scratch[1:] if nk > 1 else scratch

            @pl.when((ids[0] == 0) & (ids[1] == 0) & (ids[2] == 0))
            def _():
                for cp in carry.copies(c_in, c_out, *sems):
                    cp.start()

        part = _dot(a_ref[...].astype(BF16), b_ref[...].astype(BF16), dn)

        def finish(r):
            if epi is not None:
                r = epi(r, *[e[...] for e in extras])
            o_ref[...] = r.astype(o_ref.dtype)

        if nk == 1:
            finish(part)
        else:
            acc = scratch[0]

            @pl.when(ids[2] == 0)
            def _():
                acc[...] = part

            @pl.when(ids[2] > 0)
            def _():
                acc[...] += part

            @pl.when(ids[2] == nk - 1)
            def _():
                finish(acc[...])

        if carry:
            @pl.when((ids[0] == grid[0] - 1) & (ids[1] == grid[1] - 1) & (ids[2] == nk - 1))
            def _():
                for cp in carry.copies(c_in, c_out, *sems):
                    cp.wait()

    a_spec = pl.BlockSpec((tk, tm), lambda i, j, kk: (kk, i)) if dn == TN else pl.BlockSpec((tm, tk), lambda i, j, kk: (i, kk))
    b_spec = pl.BlockSpec((tn, tk), lambda i, j, kk: (j, kk)) if dn == NT else pl.BlockSpec((tk, tn), lambda i, j, kk: (kk, j))
    in_specs = [a_spec, b_spec] + [pl.BlockSpec((tm, tn), lambda i, j, kk: (i, j)) for _ in tiles]
    in_specs += [pl.BlockSpec((1, tn), lambda i, j, kk: (0, j)) for _ in rows] + [HBM_SPEC] * n_ci
    res = pl.pallas_call(
        body, name=name, grid=grid, in_specs=in_specs,
        out_specs=[pl.BlockSpec((tm, tn), lambda i, j, kk: (i, j))] + [HBM_SPEC] * n_co,
        out_shape=[SDS((m, n), out_dtype)] + (list(carry.outs) if carry else []),
        scratch_shapes=([pltpu.VMEM((tm, tn), F32)] if nk > 1 else []) + (list(carry.sems) if carry else []),
        compiler_params=pltpu.CompilerParams(
            dimension_semantics=("arbitrary",) * 3 if carry else ("parallel", "parallel", "arbitrary"),
            vmem_limit_bytes=VMEM_LIMIT),
    )(a, b, *tiles, *rows, *(carry.ins if carry else []))
    if carry:
        carry.then(res[1:])
    return res[0]


def _mean_all(v):
    return jnp.mean(v, axis=-1, keepdims=True)


def _head_sums(v, bd):
    w = bd.shape[0]
    return jnp.concatenate([_dot01(v[:, j:j + w], bd) for j in range(0, v.shape[1], w)], axis=1)


def _mean_heads(bd):
    return lambda v: _head_sums(v, bd) * (1.0 / HD)


def _rms_fwd(x, g, mean):
    return x * lax.rsqrt(mean(x * x) + EPS) * g


def _rms_bwd(x, g, dy, mean):
    r = lax.rsqrt(mean(x * x) + EPS)
    dn = dy * g
    dx = r * dn - x * (r * r * r) * mean(dn * x)
    return dx, jnp.sum(dy * x * r, axis=0, keepdims=True)


def _swap_halves(x):
    w = x.shape[1]
    lane = lax.broadcasted_iota(jnp.int32, x.shape, 1)
    return jnp.where(lane % HD < HD // 2, pltpu.roll(x, w - HD // 2, 1), pltpu.roll(x, HD // 2, 1))


def _lanes(t, w):
    return jnp.tile(t, (1, w // t.shape[1]))


def _rope_fwd(x, cos, sin_signed):
    return x * _lanes(cos, x.shape[1]) + _swap_halves(x) * _lanes(sin_signed, x.shape[1])


def _rope_bwd(dy, cos, sin_signed):
    return dy * _lanes(cos, dy.shape[1]) + _swap_halves(dy * _lanes(sin_signed, dy.shape[1]))


def _bcast_heads(cols):
    return jnp.concatenate([jnp.broadcast_to(c, (c.shape[0], HD)) for c in cols], axis=1)


def _softplus(z):
    return jnp.maximum(z, 0.0) + jnp.log(1.0 + jnp.exp(-jnp.abs(z)))


def _block_diag(w):
    h = np.arange(w) // HD
    return jnp.asarray(h[:, None] == h[None, :], BF16)


def _sb_window(i, t):
    hi = (i + 1) * SB_QB - t * SB_WIN
    lo = hi - SB_WIN
    ws = pl.multiple_of(jnp.maximum(lo, 0), SB_QB)
    kpos = ws + lax.broadcasted_iota(jnp.int32, (SB_QB, SB_WIN), 1)
    qpos = i * SB_QB + lax.broadcasted_iota(jnp.int32, (SB_QB, SB_WIN), 0)
    return (kpos < qpos) & (kpos >= lo) & (kpos < hi), ws


def _sb_fwd(qkv):
    s = qkv.shape[0]
    assert s >= SB_WIN
    nq = s // SB_QB
    nh = SB_FWD_HEADS
    bw = HD * nh
    ngroups = SB_W // bw

    def body(q_ref, k_ref, v_ref, later_ref, o_ref, tot_ref, w0_ref, b0_ref, nb_ref):
        p, i = pl.program_id(0), pl.program_id(1)
        q = q_ref[...]
        later_of = later_ref[...]

        def window(t, tots, outs, keep):
            mask, ws = _sb_window(i, t)
            kw, vw = k_ref[pl.ds(ws, SB_WIN), :], v_ref[pl.ds(ws, SB_WIN), :]
            new_t, new_o, w_all, b_all = [], [], [], []
            for hh in range(nh):
                sl = slice(HD * hh, HD * hh + HD)
                z = _dot(q[:, sl], kw[:, sl], NT) * SCALE
                sp = _softplus(z)
                lf = jnp.where(mask, -sp, 0.0)
                later = tots[hh] + _dot01(lf, later_of, 2)
                w = jnp.where(mask, jnp.exp(z - sp + later), 0.0).astype(BF16)
                new_o.append(outs[hh] + _dot(w, vw[:, sl]))
                new_t.append(tots[hh] + jnp.sum(lf, axis=1, keepdims=True))
                if keep:
                    w_all.append(w)
                    b_all.append(jnp.where(mask, jnp.exp(z - sp), 0.0).astype(BF16))
            if keep:
                w0_ref[...] = jnp.concatenate(w_all, axis=1)
                b0_ref[...] = jnp.concatenate(b_all, axis=1)
            alive = functools.reduce(jnp.maximum, [jnp.max(v) for v in new_t])
            return t + 1, alive, tuple(new_t), tuple(new_o)

        zt, zo = jnp.zeros((SB_QB, 1), F32), jnp.zeros((SB_QB, HD), F32)
        first = window(jnp.int32(0), (zt,) * nh, (zo,) * nh, True)
        t, _, tots, outs = lax.while_loop(lambda c: ((i + 1) * SB_QB - c[0] * SB_WIN > 0) & (c[1] > EXHAUSTED),
                                          lambda c: window(c[0], c[2], c[3], False), first)
        o_ref[...] = jnp.concatenate(outs, axis=1).astype(o_ref.dtype)
        tot_ref[...] = _bcast_heads(tots)
        nb_ref[p, i] = t

    whole = lambda off: pl.BlockSpec((s, bw), lambda p, i: (0, off + p), pipeline_mode=pl.Buffered(1))
    tile = pl.BlockSpec((SB_QB, bw), lambda p, i: (i, p))
    tri = pl.BlockSpec((SB_WIN, SB_WIN), lambda p, i: (0, 0), pipeline_mode=pl.Buffered(1))
    near = pl.BlockSpec((SB_QB, nh * SB_WIN), lambda p, i: (i, p))
    n_heads = SB_W // HD
    idx = np.arange(SB_WIN)
    return pl.pallas_call(
        body, name="sb_fwd", grid=(ngroups, nq),
        in_specs=[tile, whole(ngroups), whole(2 * ngroups), tri],
        out_specs=[tile, tile, near, near, pl.BlockSpec(memory_space=pltpu.SMEM)],
        out_shape=[SDS((s, SB_W), BF16), SDS((s, SB_W), F32), SDS((s, n_heads * SB_WIN), BF16), SDS((s, n_heads * SB_WIN), BF16),
                   SDS((ngroups, nq), jnp.int32)],
        compiler_params=pltpu.CompilerParams(dimension_semantics=("arbitrary", "arbitrary"), vmem_limit_bytes=VMEM_LIMIT),
    )(qkv, qkv, qkv, jnp.asarray(idx[:, None] > idx[None, :], BF16))


def _sb_bwd(qkv, do, tot, nblk, w0, b0, buf, col):
    s = qkv.shape[0]
    nq = s // SB_QB
    npairs = SB_W // 128

    def body(nb_ref, q_ref, k_ref, v_ref, do_ref, tot_ref, upto_ref, before_ref, w0_ref, b0_ref, buf_ref,
             dq_ref, dk_ref, dv_ref):
        p, i = pl.program_id(0), pl.program_id(1)

        @pl.when(i == 0)
        def _():
            dk_ref[...] = jnp.zeros(dk_ref.shape, F32)
            dv_ref[...] = jnp.zeros(dv_ref.shape, F32)

        upto = upto_ref[...]
        before = before_ref[...]
        q, dout, tt = q_ref[...], do_ref[...], tot_ref[...]
        n = nb_ref[p * 2 // SB_FWD_HEADS, i]

        def step(it, c):
            pres, gpres, dqs = c
            mask, ws = _sb_window(i, n - 1 - it)
            kw, vw = k_ref[pl.ds(ws, SB_WIN), :], v_ref[pl.ds(ws, SB_WIN), :]
            new_p, new_g, new_dq, dks, dvs = [], [], [], [], []
            for hh in range(2):
                sl = slice(HD * hh, HD * hh + HD)
                z = _dot(q[:, sl], kw[:, sl], NT) * SCALE
                sp = _softplus(z)
                lf = jnp.where(mask, -sp, 0.0)
                later = tt[:, HD * hh:HD * hh + 1] - (pres[hh] + _dot01(lf, upto, 2))
                w = jnp.where(mask, jnp.exp(z - sp + later), 0.0)
                beta = jnp.exp(z - sp)
                g = _dot(dout[:, sl], vw[:, sl], NT) * w
                g_far = gpres[hh] + _dot(g.astype(BF16), before)
                dz = (jnp.where(mask, g * (1.0 - beta) - beta * g_far, 0.0) * SCALE).astype(BF16)
                new_dq.append(dqs[hh] + _dot(dz, kw[:, sl]))
                dks.append(_dot(dz, q[:, sl], TN))
                dvs.append(_dot(w.astype(BF16), dout[:, sl], TN))
                new_p.append(pres[hh] + jnp.sum(lf, axis=1, keepdims=True))
                new_g.append(gpres[hh] + jnp.sum(g, axis=1, keepdims=True))
            dk_ref[pl.ds(ws, SB_WIN), :] += jnp.concatenate(dks, axis=1)
            dv_ref[pl.ds(ws, SB_WIN), :] += jnp.concatenate(dvs, axis=1)
            return tuple(new_p), tuple(new_g), tuple(new_dq)

        zt, zo = jnp.zeros((SB_QB, 1), F32), jnp.zeros((SB_QB, HD), F32)
        _, gpres, dqs = lax.fori_loop(0, n - 1, step, ((zt, zt), (zt, zt), (zo, zo)))
        _, ws = _sb_window(i, 0)
        kw, vw = k_ref[pl.ds(ws, SB_WIN), :], v_ref[pl.ds(ws, SB_WIN), :]
        dqs, dks, dvs = list(dqs), [], []
        for hh in range(2):
            sl = slice(HD * hh, HD * hh + HD)
            w = w0_ref[:, SB_WIN * hh:SB_WIN * (hh + 1)]
            beta = b0_ref[:, SB_WIN * hh:SB_WIN * (hh + 1)].astype(F32)
            g = _dot(dout[:, sl], vw[:, sl], NT) * w.astype(F32)
            g_far = gpres[hh] + _dot(g.astype(BF16), before)
            dz = ((g * (1.0 - beta) - beta * g_far) * SCALE).astype(BF16)
            dqs[hh] = dqs[hh] + _dot(dz, kw[:, sl])
            dks.append(_dot(dz, q[:, sl], TN))
            dvs.append(_dot(w, dout[:, sl], TN))
        dk_ref[pl.ds(ws, SB_WIN), :] += jnp.concatenate(dks, axis=1)
        dv_ref[pl.ds(ws, SB_WIN), :] += jnp.concatenate(dvs, axis=1)
        dq_ref[...] = jnp.concatenate(dqs, axis=1).astype(dq_ref.dtype)

    whole_in = lambda off: pl.BlockSpec((s, 128), lambda p, i: (0, off + p), pipeline_mode=pl.Buffered(1))
    whole_out = pl.BlockSpec((s, 128), lambda p, i: (0, p), pipeline_mode=pl.Buffered(1))
    tile = pl.BlockSpec((SB_QB, 128), lambda p, i: (i, p))
    near = pl.BlockSpec((SB_QB, 2 * SB_WIN), lambda p, i: (i, p))
    dq_tile = pl.BlockSpec((SB_QB, 128), lambda p, i: (i, col // 128 + p))
    tri = pl.BlockSpec((SB_WIN, SB_WIN), lambda p, i: (0, 0), pipeline_mode=pl.Buffered(1))
    idx = np.arange(SB_WIN)
    return pl.pallas_call(
        body, name="sb_bwd", grid=(npairs, nq),
        in_specs=[pl.BlockSpec(memory_space=pltpu.SMEM), tile, whole_in(npairs), whole_in(2 * npairs), tile, tile, tri, tri,
                  near, near, HBM_SPEC],
        out_specs=[dq_tile, whole_out, whole_out],
        out_shape=[SDS(buf.shape, buf.dtype)] + [SDS((s, SB_W), F32)] * 2,
        input_output_aliases={10: 0},
        compiler_params=pltpu.CompilerParams(dimension_semantics=("arbitrary", "arbitrary"), vmem_limit_bytes=VMEM_LIMIT),
    )(nblk, qkv, qkv, qkv, do, tot, jnp.asarray(idx[:, None] <= idx[None, :], BF16), jnp.asarray(idx[:, None] < idx[None, :], BF16),
      w0, b0, buf)


def _dsa_mask(DSA_T, has_prev):
    r = lax.broadcasted_iota(jnp.int32, (DSA_T, QB + DSA_T), 0)
    j = lax.broadcasted_iota(jnp.int32, (DSA_T, QB + DSA_T), 1) - QB
    return (j <= r) & (j >= r - QB) & ((j >= 0) | has_prev)


def _dsa_fwd(q, k, v, dil):
    n = q.shape[0]
    DSA_T = DSA_T_FWD
    nt = n // DSA_T

    def body(q_ref, kc_ref, kp_ref, vc_ref, vp_ref, o_ref, lse_ref):
        mask = _dsa_mask(DSA_T, pl.program_id(1) > 0)
        outs, lses = [], []
        for hh in range(DSA_OUT_W // HD):
            sl = slice(HD * hh, HD * hh + HD)
            kcat = jnp.concatenate([kp_ref[:, sl], kc_ref[:, sl]], axis=0)
            vcat = jnp.concatenate([vp_ref[:, sl], vc_ref[:, sl]], axis=0)
            sc = jnp.where(mask, _dot(q_ref[:, sl], kcat, NT) * SCALE, NEG)
            m = jnp.max(sc, axis=1, keepdims=True)
            p = jnp.exp(sc - m)
            den = jnp.sum(p, axis=1, keepdims=True)
            outs.append(_dot(p.astype(BF16), vcat) / den)
            lses.append(m + jnp.log(den))
        o_ref[...] = jnp.concatenate(outs, axis=1)
        lse_ref[...] = _bcast_heads(lses)

    cur = pl.BlockSpec((DSA_T, DSA_OUT_W), lambda c, i: (i, c))
    prev = pl.BlockSpec((QB, DSA_OUT_W), lambda c, i: (jnp.maximum(i * (DSA_T // QB) - 1, 0), c))
    o, lse = pl.pallas_call(
        body, name=f"dsa_fwd_d{dil}", grid=(dil, nt), in_specs=[cur, cur, prev, cur, prev], out_specs=[cur, cur],
        out_shape=[SDS((n, dil * DSA_OUT_W), F32)] * 2,
        compiler_params=pltpu.CompilerParams(dimension_semantics=("parallel", "parallel")),
    )(q, k, k, v, v)
    return o, lse


def _dsa_bwd(q, k, v, do, cc, lse, dil):
    n = q.shape[0]
    DSA_T = DSA_T_BWD
    nt = n // DSA_T
    per = DSA_T // QB

    def body(qj_ref, qn_ref, kp_ref, kj_ref, vp_ref, vj_ref, doj_ref, don_ref, cj_ref, cn_ref, lj_ref, ln_ref,
             dq_ref, dk_ref, dv_ref):
        j = pl.program_id(1)
        mask = _dsa_mask(DSA_T, j > 0)
        r = lax.broadcasted_iota(jnp.int32, (QB, DSA_T), 0)
        kk = lax.broadcasted_iota(jnp.int32, (QB, DSA_T), 1)
        m_next = (kk >= r + QB) & (j + 1 < nt)
        dqs, dks, dvs = [], [], []
        for hh in range(DSA_OUT_W // HD):
            sl = slice(HD * hh, HD * hh + HD)
            one = slice(HD * hh, HD * hh + 1)
            qj, qn, kj, vj, doj, don = (t[:, sl] for t in (qj_ref, qn_ref, kj_ref, vj_ref, doj_ref, don_ref))
            kcat = jnp.concatenate([kp_ref[:, sl], kj], axis=0)
            vcat = jnp.concatenate([vp_ref[:, sl], vj], axis=0)
            p1 = jnp.where(mask, jnp.exp(_dot(qj, kcat, NT) * SCALE - lj_ref[:, one]), 0.0)
            ds1 = (p1 * (_dot(doj, vcat, NT) + cj_ref[:, one]) * SCALE).astype(BF16)
            p2 = jnp.where(m_next, jnp.exp(_dot(qn, kj, NT) * SCALE - ln_ref[:, one]), 0.0)
            ds2 = (p2 * (_dot(don, vj, NT) + cn_ref[:, one]) * SCALE).astype(BF16)
            dqs.append(_dot(ds1, kcat))
            dks.append(_dot(ds1[:, QB:], qj, TN) + _dot(ds2, qn, TN))
            dvs.append(_dot(p1[:, QB:].astype(BF16), doj, TN) + _dot(p2.astype(BF16), don, TN))
        dq_ref[...] = jnp.concatenate(dqs, axis=1)
        dk_ref[...] = jnp.concatenate(dks, axis=1)
        dv_ref[...] = jnp.concatenate(dvs, axis=1)

    cur = pl.BlockSpec((DSA_T, DSA_OUT_W), lambda c, j: (j, c))
    prev = pl.BlockSpec((QB, DSA_OUT_W), lambda c, j: (jnp.maximum(j * per - 1, 0), c))
    nxt = pl.BlockSpec((QB, DSA_OUT_W), lambda c, j: (jnp.minimum((j + 1) * per, n // QB - 1), c))
    dq, dk, dv = pl.pallas_call(
        body, name=f"dsa_bwd_d{dil}", grid=(dil, nt),
        in_specs=[cur, nxt, prev, cur, prev, cur, cur, nxt, cur, nxt, cur, nxt], out_specs=[cur, cur, cur],
        out_shape=[SDS((n, dil * DSA_OUT_W), F32)] * 3,
        compiler_params=pltpu.CompilerParams(dimension_semantics=("parallel", "parallel")),
    )(q, q, k, k, v, v, do, do, cc, cc, lse, lse)
    return dq, dk, dv


def _ffn_fwd(tag, x, gain, w13, w2, plan=None):
    n = _tokmap(f"{tag}_norm", lambda xv, g: _rms_fwd(xv, g, _mean_all), [x], [gain], [(D, BF16)])[0]
    ab = _matmul(f"{tag}_up", n, w13, NN, BF16, plan=plan)

    def gate(abv):
        a, b = abv[:, :D_FF].astype(F32), abv[:, D_FF:].astype(F32)
        return a * jax.nn.sigmoid(a) * b

    h = _tokmap(f"{tag}_gate", gate, [ab], [], [(D_FF, BF16)], tile=256)[0]
    y = _matmul(f"{tag}_down", h, w2(), NN, F32, epi=lambda acc, res: res + 0.5 * acc, tiles=[x], plan=plan)
    return y, (n, ab, h)


def _ffn_bwd(tag, x, gain, w13, w2, saved, dy, plan=None, on_dw=None):
    n, ab, h = saved
    dh = _matmul(f"{tag}_bwd_dh", dy, w2, NT, BF16, epi=lambda acc: 0.5 * acc)

    def gate_bwd(abv, dhv):
        a, b, dhf = abv[:, :D_FF].astype(F32), abv[:, D_FF:].astype(F32), dhv.astype(F32)
        sg = jax.nn.sigmoid(a)
        da = dhf * b * (sg * (1.0 + a * (1.0 - sg)))
        return jnp.concatenate([da, dhf * (a * sg)], axis=1)

    dab = _tokmap(f"{tag}_bwd_gate", gate_bwd, [ab, dh], [], [(2 * D_FF, BF16)], tile=256)[0]
    dw2 = _matmul(f"{tag}_bwd_dw2", h, dy, TN, F32, epi=lambda acc: 0.5 * acc)
    dw13 = _matmul(f"{tag}_bwd_dw13", n, dab, TN, F32, plan=plan)
    if on_dw is not None:
        on_dw(dw13, dw2)
    dn = _matmul(f"{tag}_bwd_dn", dab, w13, NT, F32, plan=plan)

    def norm_bwd(xv, dnv, dyv, g):
        dx, dg = _rms_bwd(xv, g, dnv, _mean_all)
        return dx + dyv, dg

    dx, dgain = _tokmap(f"{tag}_bwd_norm", norm_bwd, [x, dn, dy], [gain], [(D, F32)], [(1, D)])
    return dx, dgain, dw13, dw2


def _rope_tables(s):
    half = HD // 2
    inv_freq = jnp.power(10000.0, -jnp.arange(half, dtype=F32) / half)
    ang = jnp.arange(s).astype(F32)[:, None] * inv_freq[None, :]
    cos, sin = jnp.cos(ang), jnp.sin(ang)
    return jnp.tile(jnp.concatenate([cos, cos], axis=1), (1, 2)), jnp.tile(jnp.concatenate([-sin, sin], axis=1), (1, 2))


def _local_step(x, mem, tgt, w, sm, plan=None, on_grads=None):
    s = x.shape[0]
    assert s % (max(DSA_T_FWD, DSA_T_BWD) * max(DSA_DILS)) == 0
    on_grads = on_grads or (lambda group, grads: None)
    c_sb, c_dsa, c_qm, c_all = 3 * D, 3 * D + 3 * SB_W, 3 * D + 3 * SB_W + 3 * DSA_W, 3 * D + 4096
    cos, sin = _rope_tables(s)
    bd768 = bd256 = _block_diag(128)
    gq_dsa, gk_dsa = jnp.tile(sm["qn_dsa"], (1, DSA_W // HD)), jnp.tile(sm["kn_dsa"], (1, DSA_W // HD))
    gq_mem, gk_mem = jnp.tile(sm["qn_mem"], (1, MEM_W // HD)), jnp.tile(sm["kn_mem"], (1, MEM_W // HD))

    w13_1 = jnp.concatenate([w["ffn1_w1"], w["ffn1_w3"]], axis=1)
    x1, ffn1_saved = _ffn_fwd("ffn1", x, sm["ffn1_norm"], w13_1, lambda: w["ffn1_w2"], plan)
    w_all = jnp.concatenate([w["w_gate"], w["w_in"]], axis=1)
    wb_sb, wb_dsa, wb_mem = w["w_branch_sb"], w["w_branch_dsa"], w["w_branch_mem"]
    hmix = _tokmap("mix_norm", lambda xv, g: _rms_fwd(xv, g, _mean_all), [x1], [sm["mix_norm"]], [(D, BF16)])[0]
    qkv_sb = _matmul("proj_sb", hmix, w_all[:, c_sb:c_dsa], NN, BF16)
    qkv_dsa = _matmul("proj_dsa", hmix, w_all[:, c_dsa:c_qm], NN, BF16, plan=plan)
    q_mem = _matmul("proj_qmem", hmix, w_all[:, c_qm:], NN, BF16)
    gpre = _matmul("proj_gate", hmix, w_all[:, :c_sb], NN, BF16, epi=lambda acc, b: acc + b, rows=[sm["b_gate"]], plan=plan)

    o_sb, sb_tot, sb_w0, sb_b0, sb_nblk = _sb_fwd(qkv_sb)

    def dsa_prep(qkv, cs, sn, gq, gk, bd):
        mean = _mean_heads(bd)
        qn = _rope_fwd(_rms_fwd(qkv[:, :DSA_W].astype(F32), gq, mean), cs, sn)
        kn = _rope_fwd(_rms_fwd(qkv[:, DSA_W:2 * DSA_W].astype(F32), gk, mean), cs, sn)
        v = qkv[:, 2 * DSA_W:]
        outs = []
        for t in (qn, kn, v):
            outs += [t[:, DSA_OUT_W * g:DSA_OUT_W * (g + 1)] for g in range(3)]
        return outs

    dsa_in = _tokmap("dsa_prep", dsa_prep, [qkv_dsa, cos, sin], [gq_dsa, gk_dsa, bd768], [(DSA_OUT_W, BF16)] * 9, tile=256,
                     dil_outs={j: DSA_DILS[j % 3] for j in range(9)})
    dsa_q, dsa_k, dsa_v = dsa_in[0:3], dsa_in[3:6], dsa_in[6:9]
    dsa_o, dsa_lse = zip(*[_dsa_fwd(dsa_q[g], dsa_k[g], dsa_v[g], DSA_DILS[g]) for g in range(3)])

    def alphas(l0, l1, l2):
        m = jnp.maximum(jnp.maximum(l0, l1), l2)
        e = [jnp.exp(l - m) for l in (l0, l1, l2)]
        tot = e[0] + e[1] + e[2]
        return [t / tot for t in e]

    def dsa_mix(o0, o1, o2, l0, l1, l2):
        a = alphas(l0, l1, l2)
        return a[0] * o0 + a[1] * o1 + a[2] * o2

    o_dsa = _tokmap("dsa_mix", dsa_mix, [*dsa_o, *dsa_lse], [], [(DSA_OUT_W, BF16)], tile=256,
                    dil_ins={j: DSA_DILS[j % 3] for j in range(6)})[0]

    def mem_kv(memv, g, wkv, gk, bd):
        kv = _dot(_rms_fwd(memv, g, _mean_all).astype(BF16), wkv)
        return _rms_fwd(kv[:, :MEM_W], gk, _mean_heads(bd)), kv[:, MEM_W:]

    km, vm = _tokmap("mem_kv", mem_kv, [mem], [sm["mem_norm"], w["w_mem_kv"], gk_mem, bd256], [(MEM_W, BF16)] * 2)

    def mem_probs(qv, kmv, gq, bd):
        qn = _rms_fwd(qv.astype(F32), gq, _mean_heads(bd)).astype(BF16)
        ps = []
        for h in range(MEM_W // HD):
            sl = slice(HD * h, HD * h + HD)
            sc = _dot(qn[:, sl], kmv[:, sl], NT) * SCALE
            e = jnp.exp(sc - jnp.max(sc, axis=1, keepdims=True))
            ps.append(e / jnp.sum(e, axis=1, keepdims=True))
        return qn, ps

    def mem_attn(qv, kmv, vmv, gq, bd):
        _, ps = mem_probs(qv, kmv, gq, bd)
        return jnp.concatenate([_dot(p.astype(BF16), vmv[:, HD * h:HD * h + HD]) for h, p in enumerate(ps)], axis=1)

    o_mem = _tokmap("mem_attn", mem_attn, [q_mem], [km, vm, gq_mem, bd256], [(MEM_W, BF16)])[0]

    def merge(osb, odsa, omem, gp, w_sb, w_dsa, w_mem):
        gates = jax.nn.sigmoid(gp.astype(F32))
        ys = (_dot(osb, w_sb), _dot(odsa, w_dsa), _dot(omem, w_mem))
        return gates, ys, gates[:, :D] * ys[0] + gates[:, D:2 * D] * ys[1] + gates[:, 2 * D:] * ys[2]

    merged = _tokmap("merge", lambda *a: merge(*a)[2], [o_sb, o_dsa, o_mem, gpre], [wb_sb, wb_dsa, wb_mem], [(D, BF16)],
                     tile=256)[0]
    x2 = _matmul("out_proj", merged, w["w_out"], NN, F32, epi=lambda acc, res: res + acc, tiles=[x1])
    w13_2 = jnp.concatenate([w["ffn2_w1"], w["ffn2_w3"]], axis=1)
    y, ffn2_saved = _ffn_fwd("ffn2", x2, sm["ffn2_norm"], w13_2, lambda: w["ffn2_w2"])

    def loss_fn(yv, tv):
        e = yv - tv
        part = 0.5 * jnp.sum(jnp.mean(e * e, axis=1, keepdims=True), axis=0, keepdims=True)
        return e * (1.0 / D), jnp.broadcast_to(part, (1, 128))

    dy, loss = _tokmap("loss", loss_fn, [y, tgt], [], [(D, F32)], [(1, 128)])

    gw, gs = {}, {}
    def ffn_grads(tag):
        def on_dw(dw13, dw2):
            gw[f"{tag}_w1"], gw[f"{tag}_w3"], gw[f"{tag}_w2"] = dw13[:, :D_FF], dw13[:, D_FF:], dw2
            on_grads(tag, {n: gw[n] for n in (f"{tag}_w1", f"{tag}_w3", f"{tag}_w2")})
        return on_dw

    dx2, gs["ffn2_norm"], _, _ = _ffn_bwd("ffn2", x2, sm["ffn2_norm"], w13_2, w["ffn2_w2"], ffn2_saved, dy, plan,
                                          ffn_grads("ffn2"))
    dmerged = _matmul("out_proj_bwd_dx", dx2, w["w_out"], NT, BF16)
    gw["w_out"] = _matmul("out_proj_bwd_dw", merged, dx2, TN, F32)

    def merge_bwd(osb, odsa, omem, gp, dm, w_sb, w_dsa, w_mem):
        gates, ys, _ = merge(osb, odsa, omem, gp, w_sb, w_dsa, w_mem)
        dmf = dm.astype(F32)
        dgp, dos, dws = [], [], []
        for b, (ov, wv) in enumerate(((osb, w_sb), (odsa, w_dsa), (omem, w_mem))):
            gb = gates[:, D * b:D * (b + 1)]
            dgp.append(dmf * ys[b] * gb * (1.0 - gb))
            dyb = (dmf * gb).astype(BF16)
            dos.append(_dot(dyb, wv, NT))
            dws.append(_dot(ov, dyb, TN))
        dgp = jnp.concatenate(dgp, axis=1)
        return dos[0], dos[1], dos[2], dgp, dws[0], dws[1], dws[2], jnp.sum(dgp, axis=0, keepdims=True)

    do_sb, do_dsa, do_mem, dgpre, gw["w_branch_sb"], gw["w_branch_dsa"], gw["w_branch_mem"], gs["b_gate"] = _tokmap(
        "merge_bwd", merge_bwd, [o_sb, o_dsa, o_mem, gpre, dmerged], [wb_sb, wb_dsa, wb_mem],
        [(SB_W, BF16), (DSA_OUT_W, F32), (MEM_W, BF16), (3 * D, BF16)],
        [(SB_W, D), (DSA_OUT_W, D), (MEM_W, D), (1, 3 * D)], tile=256, place={3: (c_all, 0, None)})

    dall, dk_sb, dv_sb = _sb_bwd(qkv_sb, do_sb, sb_tot, sb_nblk, sb_w0, sb_b0, dgpre, c_sb)
    dall = lax.dynamic_update_slice(dall, dk_sb.astype(BF16), (0, c_sb + SB_W))
    dall = lax.dynamic_update_slice(dall, dv_sb.astype(BF16), (0, c_sb + 2 * SB_W))

    def dsa_mix_bwd(o0, o1, o2, l0, l1, l2, dov, bd):
        a = alphas(l0, l1, l2)
        omix = a[0] * o0 + a[1] * o1 + a[2] * o2
        dot_o = _head_sums(dov * omix, bd)
        return [dov * t for t in a] + [-t * dot_o for t in a]

    mixb = _tokmap("dsa_mix_bwd", dsa_mix_bwd, [*dsa_o, *dsa_lse, do_dsa], [bd256],
                   [(DSA_OUT_W, BF16)] * 3 + [(DSA_OUT_W, F32)] * 3, tile=256,
                   dil_ins={j: DSA_DILS[j % 3] for j in range(6)}, dil_outs={j: DSA_DILS[j % 3] for j in range(6)})
    dsa_d = [_dsa_bwd(dsa_q[g], dsa_k[g], dsa_v[g], mixb[g], mixb[3 + g], dsa_lse[g], DSA_DILS[g]) for g in range(3)]

    def dsa_prep_bwd(qkv, cs, sn, *rest):
        dqs, dks, dvs, (gq, gk, bd) = rest[0:3], rest[3:6], rest[6:9], rest[9:]
        mean = _mean_heads(bd)
        dq, dgq = _rms_bwd(qkv[:, :DSA_W].astype(F32), gq, _rope_bwd(jnp.concatenate(dqs, axis=1), cs, sn), mean)
        dk, dgk = _rms_bwd(qkv[:, DSA_W:2 * DSA_W].astype(F32), gk, _rope_bwd(jnp.concatenate(dks, axis=1), cs, sn), mean)
        return jnp.concatenate([dq, dk] + list(dvs), axis=1), dgq, dgk

    dall, dgq_dsa, dgk_dsa = _tokmap(
        "dsa_prep_bwd", dsa_prep_bwd,
        [qkv_dsa, cos, sin] + [dsa_d[g][t] for t in range(3) for g in range(3)], [gq_dsa, gk_dsa, bd768],
        [(3 * DSA_W, BF16)], [(1, DSA_W), (1, DSA_W)], tile=256, dil_ins={3 + j: DSA_DILS[j % 3] for j in range(9)},
        place={0: (c_all, c_dsa // (3 * DSA_W), dall)})
    gs["qn_dsa"] = dgq_dsa.reshape(DSA_W // HD, HD).sum(axis=0, keepdims=True)
    gs["kn_dsa"] = dgk_dsa.reshape(DSA_W // HD, HD).sum(axis=0, keepdims=True)

    def mem_attn_bwd(qv, dov, kmv, vmv, gq, bd):
        qn, ps = mem_probs(qv, kmv, gq, bd)
        dqn, dkm, dvm = [], [], []
        for h, p in enumerate(ps):
            sl = slice(HD * h, HD * h + HD)
            dp = _dot(dov[:, sl], vmv[:, sl], NT)
            ds = (p * (dp - jnp.sum(p * dp, axis=1, keepdims=True)) * SCALE).astype(BF16)
            dqn.append(_dot(ds, kmv[:, sl]))
            dkm.append(_dot(ds, qn[:, sl], TN))
            dvm.append(_dot(p.astype(BF16), dov[:, sl], TN))
        dq, dgq = _rms_bwd(qv.astype(F32), gq, jnp.concatenate(dqn, axis=1), _mean_heads(bd))
        return dq, jnp.concatenate(dkm, axis=1), jnp.concatenate(dvm, axis=1), dgq

    dall, dkm, dvm, dgq_mem = _tokmap("mem_attn_bwd", mem_attn_bwd, [q_mem, do_mem], [km, vm, gq_mem, bd256],
                                      [(MEM_W, BF16)], [(MEM_LEN, MEM_W), (MEM_LEN, MEM_W), (1, MEM_W)],
                                      place={0: (c_all, c_qm // MEM_W, dall)})
    gs["qn_mem"] = dgq_mem.reshape(MEM_W // HD, HD).sum(axis=0, keepdims=True)

    def mem_kv_bwd(memv, dkmv, dvmv, g, wkv, gk, bd):
        memn = _rms_fwd(memv, g, _mean_all).astype(BF16)
        kv = _dot(memn, wkv)
        dk, dgk = _rms_bwd(kv[:, :MEM_W], gk, dkmv, _mean_heads(bd))
        dkv = jnp.concatenate([dk, dvmv], axis=1).astype(BF16)
        _, dg = _rms_bwd(memv, g, _dot(dkv, wkv, NT), _mean_all)
        return _dot(memn, dkv, TN), dg, dgk

    gw["w_mem_kv"], gs["mem_norm"], dgk_mem = _tokmap(
        "mem_kv_bwd", mem_kv_bwd, [mem, dkm, dvm], [sm["mem_norm"], w["w_mem_kv"], gk_mem, bd256], [],
        [(D, 2 * MEM_W), (1, D), (1, MEM_W)])
    gs["kn_mem"] = dgk_mem.reshape(MEM_W // HD, HD).sum(axis=0, keepdims=True)

    dhmix = _matmul("proj_bwd_dx", dall, w_all, NT, F32)
    dw_all = _matmul("proj_bwd_dw", hmix, dall, TN, F32)
    gw["w_gate"], gw["w_in"] = dw_all[:, :c_sb], dw_all[:, c_sb:]
    on_grads("mid", {n: gw[n] for n in GROUPS["mid"]})

    def mix_norm_bwd(xv, dnv, dyv, g):
        dx, dg = _rms_bwd(xv, g, dnv, _mean_all)
        return dx + dyv, dg

    dx1, gs["mix_norm"] = _tokmap("mix_norm_bwd", mix_norm_bwd, [x1, dhmix, dx2], [sm["mix_norm"]], [(D, F32)], [(1, D)])
    gx, gs["ffn1_norm"], _, _ = _ffn_bwd("ffn1", x, sm["ffn1_norm"], w13_1, w["ffn1_w2"], ffn1_saved, dx1, plan,
                                         ffn_grads("ffn1"))
    return loss, gx, gw, gs


def _shard_shape(name):
    shape, axis = SHARDED_BY_NAME[name]
    return (shape[0] // N_CHIPS, shape[1]) if axis == 0 else (shape[0], shape[1] // N_CHIPS)


def _full_from_shards(name, shards):
    axis = SHARDED_BY_NAME[name][1]
    return shards.reshape(SHARDED_BY_NAME[name][0]) if axis == 0 else jnp.concatenate(list(shards), axis=1)


def _shards_from_full(name, full, dtype):
    axis, n = SHARDED_BY_NAME[name][1], _shard_shape(name)
    return jnp.stack([lax.slice_in_dim(full, c * n[axis], (c + 1) * n[axis], axis=axis).astype(dtype) for c in range(N_CHIPS)])


def _own_shard(name, full, chip):
    axis, n = SHARDED_BY_NAME[name][1], _shard_shape(name)
    return lax.dynamic_slice_in_dim(full, chip * n[axis], n[axis], axis=axis)


SMALL_USED = sum(n for _, n in SMALL)


def _pack_small(d, loss=None):
    parts = [d[n].reshape(-1) for n, _ in SMALL]
    parts.append(jnp.zeros((1,), F32) if loss is None else loss.reshape(1))
    parts.append(jnp.zeros((SMALL_ROWS * D - SMALL_USED - 1,), F32))
    return jnp.concatenate(parts).reshape(SMALL_ROWS, D)


def _unpack_small(v):
    flat, out, r = v.reshape(-1), {}, 0
    for n, k in SMALL:
        out[n] = flat[r:r + k]
        r += k
    return out, flat[r]


def _place():
    return lax.axis_index("x"), lax.axis_index("y"), lax.axis_index("c")


def _other_chips(x, y):
    return [(1 - x, y), (x, 1 - y), (1 - x, 1 - y)]


HBM_SPEC = pl.BlockSpec(memory_space=pl.ANY)


def _chip_sems(n):
    return (pltpu.SemaphoreType.DMA((3 * n,)), pltpu.SemaphoreType.DMA((3 * n,)), pltpu.SemaphoreType.DMA((n,)))


def _gather_copies(ins, outs, send_sems, recv_sems, local_sems):
    x, y, c = _place()
    me = 2 * x + y
    copies = []
    for a, (src, out) in enumerate(zip(ins, outs)):
        copies.append(pltpu.make_async_copy(src, out.at[me], local_sems.at[a]))
        copies += [pltpu.make_async_remote_copy(src_ref=src, dst_ref=out.at[me], send_sem=send_sems.at[3 * a + k],
                                                recv_sem=recv_sems.at[3 * a + k], device_id=(px, py, c), device_id_type=MESH)
                   for k, (px, py) in enumerate(_other_chips(x, y))]
    return copies


def _scatter_copies(ins, outs, send_sems, recv_sems, local_sems):
    x, y, c = _place()
    return [pltpu.make_async_remote_copy(src_ref=src.at[2 * px + py], dst_ref=out.at[k], send_sem=send_sems.at[3 * a + k],
                                         recv_sem=recv_sems.at[3 * a + k], device_id=(px, py, c), device_id_type=MESH)
            for a, (src, out) in enumerate(zip(ins, outs)) for k, (px, py) in enumerate(_other_chips(x, y))]


def _all_gather_chips(arrays):
    n = len(arrays)

    def body(*refs):
        copies = _gather_copies(refs[:n], refs[n:2 * n], *refs[2 * n:])
        for cp in copies:
            cp.start()
        for cp in copies:
            cp.wait()

    return pl.pallas_call(
        body, name="weights_all_gather", in_specs=[HBM_SPEC] * n, out_specs=[HBM_SPEC] * n,
        out_shape=[SDS((N_CHIPS,) + a.shape, a.dtype) for a in arrays], scratch_shapes=list(_chip_sems(n)),
    )(*arrays)


def _swap_with_sibling(name, arrays):
    n = len(arrays)

    def body(*refs):
        x, y, c = _place()
        send_sems, recv_sems = refs[2 * n:]
        copies = [pltpu.make_async_remote_copy(src_ref=refs[a], dst_ref=refs[n + a], send_sem=send_sems.at[a],
                                               recv_sem=recv_sems.at[a], device_id=(x, y, 1 - c), device_id_type=MESH)
                  for a in range(n)]
        for cp in copies:
            cp.start()
        for cp in copies:
            cp.wait()

    return pl.pallas_call(
        body, name=name, in_specs=[HBM_SPEC] * n, out_specs=[HBM_SPEC] * n, out_shape=[SDS(a.shape, a.dtype) for a in arrays],
        scratch_shapes=[pltpu.SemaphoreType.DMA((n,)), pltpu.SemaphoreType.DMA((n,))],
    )(*arrays)


def _all_reduce_small(v):
    n_dev = 8

    def body(v_ref, out_ref, land, send_sems, recv_sems):
        x, y, c = _place()
        me = 4 * x + 2 * y + c
        land[me] = v_ref[...]
        copies = []
        for k in range(1, n_dev):
            peer = (x ^ (k >> 2), y ^ ((k >> 1) & 1), c ^ (k & 1))
            copies.append(pltpu.make_async_remote_copy(src_ref=v_ref, dst_ref=land.at[me], send_sem=send_sems.at[k - 1],
                                                       recv_sem=recv_sems.at[k - 1], device_id=peer, device_id_type=MESH))
        for cp in copies:
            cp.start()
        for cp in copies:
            cp.wait()
        acc = land[0]
        for d in range(1, n_dev):
            acc = acc + land[d]
        out_ref[...] = acc

    return pl.pallas_call(
        body, name="small_all_reduce", in_specs=[pl.BlockSpec(memory_space=pltpu.VMEM)],
        out_specs=pl.BlockSpec(memory_space=pltpu.VMEM), out_shape=SDS(v.shape, v.dtype),
        scratch_shapes=[pltpu.VMEM((n_dev,) + v.shape, v.dtype), pltpu.SemaphoreType.DMA((n_dev - 1,)),
                        pltpu.SemaphoreType.DMA((n_dev - 1,))],
    )(v)


def _adamw(g, wv, m, v):
    m = ADAM_B1 * m + (1.0 - ADAM_B1) * g
    v = ADAM_B2 * v + (1.0 - ADAM_B2) * (g * g)
    m_hat = m / (1.0 - ADAM_B1 ** ADAM_STEP)
    v_hat = v / (1.0 - ADAM_B2 ** ADAM_STEP)
    delta = -ADAM_LR * (m_hat / (jnp.sqrt(v_hat) + ADAM_EPS) + ADAM_WD * wv)
    return delta, m, v


def kernel(x, mem, ffn1_norm, ffn1_w1, ffn1_w3, ffn1_w2, mix_norm, mem_norm, w_in, w_mem_kv, qn_dsa, kn_dsa, qn_mem, kn_mem, w_branch_sb, w_branch_dsa, w_branch_mem, w_gate, b_gate, w_out, ffn2_norm, ffn2_w1, ffn2_w3, ffn2_w2, loss_target, m_ffn1_norm, m_ffn1_w1, m_ffn1_w3, m_ffn1_w2, m_mix_norm, m_mem_norm, m_w_in, m_w_mem_kv, m_qn_dsa, m_kn_dsa, m_qn_mem, m_kn_mem, m_w_branch_sb, m_w_branch_dsa, m_w_branch_mem, m_w_gate, m_b_gate, m_w_out, m_ffn2_norm, m_ffn2_w1, m_ffn2_w3, m_ffn2_w2, v_ffn1_norm, v_ffn1_w1, v_ffn1_w3, v_ffn1_w2, v_mix_norm, v_mem_norm, v_w_in, v_w_mem_kv, v_qn_dsa, v_kn_dsa, v_qn_mem, v_kn_mem, v_w_branch_sb, v_w_branch_dsa, v_w_branch_mem, v_w_gate, v_b_gate, v_w_out, v_ffn2_norm, v_ffn2_w1, v_ffn2_w3, v_ffn2_w2):
    given = dict(locals())
    wts = {n: given[n][0] for n in WEIGHTS}
    moms = {n: given["m_" + n][0] for n in WEIGHTS}
    vars_ = {n: given["v_" + n][0] for n in WEIGHTS}

    plan = _Plan()
    x_i, y_i, _ = _place()
    my_chip = 2 * x_i + y_i

    full = {}

    def gathered(names):
        return lambda res: full.update({n: _full_from_shards(n, g) for n, g in zip(names, res)})

    for host, names in WEIGHT_PIECES:
        shards = [wts[n].astype(BF16) for n in names]
        if host is None:
            gathered(names)(_all_gather_chips(shards))
        else:
            plan.put(host, _Carry(shards, [SDS((N_CHIPS,) + a.shape, BF16) for a in shards], _chip_sems(len(names)),
                                  _gather_copies, gathered(names)))
    small = {n: wts[n].reshape(1, -1) for n, _ in SMALL}

    landed = {}

    def on_grads(group, grads):
        names = GROUPS[group]
        slices = [_shards_from_full(n, grads[n], BF16) for n in names]
        own = [_own_shard(n, grads[n], my_chip) for n in names]
        plan.put(GRAD_HOSTS[group], _Carry(slices, [SDS((3,) + a.shape[1:], BF16) for a in slices], _chip_sems(len(names)),
                                           _scatter_copies, lambda res: landed.update({group: (own, res)})))

    loss, gx, _, gs = _local_step(x[0], mem[0], loss_target[0], full, small, plan, on_grads)
    assert not plan.pending, list(plan.pending)

    def update(hv, ov, wv, mv, vv):
        g = hv + ov
        return (g,) + _adamw(g, wv, mv, vv)

    outs = [{}, {}, {}, {}]
    for group, names in GROUPS.items():
        own, got = landed[group]
        halves = [_tokmap(f"grads_sum_chips_{n}",
                          lambda a, b0, b1, b2: ((a + b0.astype(F32)) + b1.astype(F32)) + b2.astype(F32),
                          [o, g[0], g[1], g[2]], [], [(o.shape[1], F32)])[0] for n, o, g in zip(names, own, got)]
        others = _swap_with_sibling(f"grads_swap_cores_{group}", halves)
        for n, half, other in zip(names, halves, others):
            res = _tokmap(f"adamw_{n}", update, [half, other, wts[n], moms[n], vars_[n]], [], [(half.shape[1], F32)] * 4)
            for d, r in zip(outs, res):
                d[n] = r

    s_red = _all_reduce_small(_pack_small(gs, loss[0, 0]))
    res = _tokmap(
        "adamw_small", lambda g, wv, mv, vv: (g,) + _adamw(g, wv, mv, vv),
        [s_red, _pack_small(small), _pack_small({n: moms[n] for n, _ in SMALL}), _pack_small({n: vars_[n] for n, _ in SMALL})],
        [], [(D, F32)] * 4)
    for d, packed in zip(outs, res):
        d.update(_unpack_small(packed)[0])
    _, total_loss = _unpack_small(s_red)
    return (total_loss, gx[None], *[d[n][None] for d in outs for n in WEIGHTS])
```

```python
import functools

import numpy as np
import jax
import jax.numpy as jnp
from jax import lax
from jax.experimental import pallas as pl
from jax.experimental.pallas import tpu as pltpu

F32, BF16 = jnp.float32, jnp.bfloat16
SDS = jax.ShapeDtypeStruct
MESH = pl.DeviceIdType.MESH

D = 1024
HD = 64
QB = 128
DSA_T_FWD, DSA_T_BWD = 512, 256
D_FF = 2816
SB_W, DSA_W, DSA_OUT_W, MEM_W = 512, 768, 256, 256
DSA_DILS = (1, 4, 16)
MEM_LEN = 256
N_CHIPS = 4
EPS = 1e-6
SCALE = HD ** -0.5
EXHAUSTED = -104.0
SB_FWD_HEADS = 4
SB_QB = 256
SB_WIN = 512
NEG = -1e30
VMEM_LIMIT = 56 * 1024 * 1024

ADAM_LR, ADAM_B1, ADAM_B2, ADAM_EPS, ADAM_WD, ADAM_STEP = 0.001, 0.9, 0.999, 1e-08, 0.01, 10

NN = (((1,), (0,)), ((), ()))
NT = (((1,), (1,)), ((), ()))
TN = (((0,), (0,)), ((), ()))

SHARDED = (
    ("ffn1_w1", (D, D_FF), 1), ("ffn1_w3", (D, D_FF), 1), ("ffn1_w2", (D_FF, D), 0),
    ("w_in", (D, 4096), 1), ("w_mem_kv", (D, 512), 0),
    ("w_branch_sb", (SB_W, D), 1), ("w_branch_dsa", (DSA_OUT_W, D), 1), ("w_branch_mem", (MEM_W, D), 1),
    ("w_gate", (D, 3 * D), 1), ("w_out", (D, D), 0),
    ("ffn2_w1", (D, D_FF), 1), ("ffn2_w3", (D, D_FF), 1), ("ffn2_w2", (D_FF, D), 0),
)
SHARDED_BY_NAME = {n: (sh, ax) for n, sh, ax in SHARDED}
GROUPS = {
    "ffn2": ("ffn2_w1", "ffn2_w3", "ffn2_w2"),
    "mid": ("w_in", "w_mem_kv", "w_branch_sb", "w_branch_dsa", "w_branch_mem", "w_gate", "w_out"),
    "ffn1": ("ffn1_w1", "ffn1_w3", "ffn1_w2"),
}
WEIGHT_PIECES = (
    (None, ("ffn1_w1", "ffn1_w3")),
    ("ffn1_up", ("ffn1_w2", "w_in")),
    ("ffn1_down", ("w_gate", "w_mem_kv", "w_branch_sb", "w_branch_dsa", "w_branch_mem", "w_out")),
    ("proj_dsa", ("ffn2_w2",)),
    ("proj_gate", ("ffn2_w1", "ffn2_w3")),
)
GRAD_HOSTS = {"ffn2": "ffn2_bwd_dn", "mid": "ffn1_bwd_dw13", "ffn1": "ffn1_bwd_dn"}
SMALL = (("ffn1_norm", D), ("mix_norm", D), ("mem_norm", D), ("ffn2_norm", D), ("b_gate", 3 * D),
         ("qn_dsa", HD), ("kn_dsa", HD), ("qn_mem", HD), ("kn_mem", HD))
WEIGHTS = ("ffn1_norm", "ffn1_w1", "ffn1_w3", "ffn1_w2", "mix_norm", "mem_norm", "w_in", "w_mem_kv", "qn_dsa", "kn_dsa",
           "qn_mem", "kn_mem", "w_branch_sb", "w_branch_dsa", "w_branch_mem", "w_gate", "b_gate", "w_out", "ffn2_norm",
           "ffn2_w1", "ffn2_w3", "ffn2_w2")
SMALL_ROWS = 8


def _dot(a, b, dn=NN):
    return lax.dot_general(a, b, dn, preferred_element_type=F32)


def _dot01(x, m01, pieces=3):
    hi = x.astype(BF16)
    r1 = x - hi.astype(F32)
    mid = r1.astype(BF16)
    if pieces == 2:
        return _dot(hi, m01) + _dot(mid, m01)
    lo = (r1 - mid.astype(F32)).astype(BF16)
    return _dot(hi, m01) + _dot(mid, m01) + _dot(lo, m01)


def _pick(n, cands):
    for c in cands:
        if n % c == 0:
            return c
    raise ValueError(f"no tile for {n}")


def _from_dilated(v, d, scr):
    w = v.shape[1] // d
    v = v.astype(F32)
    for c in range(d):
        for p, buf in enumerate(scr[:w // 128]):
            buf[pl.ds(c, v.shape[0], stride=d), :] = v[:, c * w + 128 * p:c * w + 128 * (p + 1)]
    return jnp.concatenate([buf[...] for buf in scr[:w // 128]], axis=1)


def _to_dilated(v, d, scr):
    w = v.shape[1]
    for p, buf in enumerate(scr[:w // 128]):
        buf[...] = v[:, 128 * p:128 * (p + 1)].astype(F32)
    return jnp.concatenate([buf[pl.ds(c, v.shape[0] // d, stride=d), :] for c in range(d) for buf in scr[:w // 128]], axis=1)


def _tokmap(name, fn, tok_ins, consts, tok_outs, acc_outs=(), tile=512, dil_ins=None, dil_outs=None, place=None):
    dil_ins, dil_outs, place = dil_ins or {}, dil_outs or {}, place or {}
    bufs = [(j, buf) for j, (_, _, buf) in place.items() if buf is not None]
    n_buf = len(bufs)
    n = tok_ins[0].shape[0] * dil_ins.get(0, 1)
    tile = _pick(n, [t for t in (512, 256, 128, 64, 32, 16, 8) if t <= tile])
    n_tin, n_in, n_tok, n_acc = len(tok_ins), len(tok_ins) + len(consts), len(tok_outs), len(acc_outs)
    n_scr = max([tok_ins[j].shape[1] // d // 128 for j, d in dil_ins.items() if d > 1]
                + [tok_outs[j][0] // 128 for j, d in dil_outs.items() if d > 1] + [0])

    def body(*refs):
        scr = refs[len(refs) - n_scr:]
        vals = [r[...] for r in refs[:n_in]]
        for j, d in dil_ins.items():
            if d > 1:
                vals[j] = _from_dilated(vals[j], d, scr)
        outs = fn(*vals)
        outs = list(outs) if isinstance(outs, (tuple, list)) else [outs]
        assert len(outs) == n_tok + n_acc, (name, len(outs))
        for j, d in dil_outs.items():
            if d > 1:
                outs[j] = _to_dilated(outs[j], d, scr)
        orefs = refs[n_in + n_buf:]
        for r, v in zip(orefs[:n_tok], outs[:n_tok]):
            r[...] = v.astype(r.dtype)
        if n_acc:
            @pl.when(pl.program_id(0) == 0)
            def _():
                for r in orefs[n_tok:n_tok + n_acc]:
                    r[...] = jnp.zeros(r.shape, r.dtype)
            for r, v in zip(orefs[n_tok:n_tok + n_acc], outs[n_tok:]):
                r[...] += v.astype(r.dtype)

    def tok_spec(width, d):
        return pl.BlockSpec((tile // d, d * width), lambda i: (i, 0))

    in_specs = [tok_spec(a.shape[1] // dil_ins.get(j, 1), dil_ins.get(j, 1)) for j, a in enumerate(tok_ins)]
    in_specs += [pl.BlockSpec(c.shape, lambda i: (0, 0)) for c in consts]
    in_specs += [HBM_SPEC] * n_buf
    out_specs = [tok_spec(w, dil_outs.get(j, 1)) for j, (w, _) in enumerate(tok_outs)]
    out_shape = [SDS((n // dil_outs.get(j, 1), w * dil_outs.get(j, 1)), dt) for j, (w, dt) in enumerate(tok_outs)]
    for j, (total, col_block, _) in place.items():
        out_specs[j] = pl.BlockSpec((tile, tok_outs[j][0]), lambda i, cb=col_block: (i, cb))
        out_shape[j] = SDS((n, total), tok_outs[j][1])
    out_specs += [pl.BlockSpec(s, lambda i: (0, 0)) for s in acc_outs]
    out_shape += [SDS(s, F32) for s in acc_outs]
    res = pl.pallas_call(
        body, name=name, grid=(n // tile,), in_specs=in_specs, out_specs=out_specs, out_shape=out_shape,
        scratch_shapes=[pltpu.VMEM((tile, 128), F32)] * n_scr,
        input_output_aliases={n_in + b: j for b, (j, _) in enumerate(bufs)},
        compiler_params=pltpu.CompilerParams(dimension_semantics=("arbitrary",), vmem_limit_bytes=VMEM_LIMIT),
    )(*tok_ins, *consts, *[buf for _, buf in bufs])
    return res


MATMUL_VMEM_BUDGET = 40 * 1024 * 1024


def _matmul_tiles(m, n, k, a_bytes, b_bytes, o_bytes, extra_bytes):
    best = None
    for tk in [c for c in (3584, 2816, 2048, 1408, 1024, 512, 256, 128) if k % c == 0]:
        for tm in [c for c in (1408, 1024, 768, 512, 256, 128) if m % c == 0]:
            for tn in [c for c in (1408, 1024, 768, 512, 256, 128) if n % c == 0]:
                need = 2 * tk * (tm * a_bytes + tn * b_bytes) + tm * tn * (2 * o_bytes + 2 * extra_bytes + 8)
                if need > MATMUL_VMEM_BUDGET:
                    continue
                score = (min(tm, 512) * min(tn, 512), tk, tm * tn, tn)
                if best is None or score > best[0]:
                    best = (score, (tm, tn, tk))
    return best[1]


class _Carry:
    def __init__(self, ins, outs, sems, copies, then):
        self.ins, self.outs, self.sems, self.copies, self.then = ins, outs, sems, copies, then


class _Plan:
    def __init__(self):
        self.pending = {}

    def put(self, host, carry):
        assert host not in self.pending, host
        self.pending[host] = carry

    def take(self, host):
        return self.pending.pop(host, None)


def _matmul(name, a, b, dn, out_dtype, epi=None, tiles=(), rows=(), plan=None, a_pro=None):
    if dn == NN:
        (m, k), n = a.shape, b.shape[1]
    elif dn == NT:
        (m, k), n = a.shape, b.shape[0]
    else:
        (k, m), n = a.shape, b.shape[1]
    n_t, n_r = len(tiles), len(rows)
    pro, n_parts = a_pro if a_pro is not None else (None, 1)
    if pro is not None:
        assert dn == NN and n == _pick(n, (1024, 512))
        k //= n_parts
    if pro is not None:
        tm, tn, tk = _pick(m, (256, 128)), n, k
    else:
        tm, tn, tk = _matmul_tiles(m, n, k, a.dtype.itemsize, b.dtype.itemsize, jnp.dtype(out_dtype).itemsize,
                                   sum(t.dtype.itemsize for t in tiles))
    nk = k // tk
    grid = (m // tm, n // tn, nk)
    assert pro is None or grid[1] == 1
    carry = plan.take(name) if plan is not None else None
    n_ci, n_co = (len(carry.ins), len(carry.outs)) if carry else (0, 0)
    n_keep = 1 if pro is not None else 0

    def body(*refs):
        a_refs, b_ref, rest = refs[:n_parts], refs[n_parts], refs[n_parts + 1:]
        extras, rest = rest[:n_t + n_r], rest[n_t + n_r:]
        c_in, o_ref, rest = rest[:n_ci], rest[n_ci], rest[n_ci + 1:]
        keep, c_out, scratch = rest[:n_keep], rest[n_keep:n_keep + n_co], rest[n_keep + n_co:]
        ids = [pl.program_id(d) for d in range(3)]
        if carry:
            sems = scratch[1:] if nk > 1 else scratch

            @pl.when((ids[0] == 0) & (ids[1] == 0) & (ids[2] == 0))
            def _():
                for cp in carry.copies(c_in, c_out, *sems):
                    cp.start()

        if pro is not None:
            av = pro(*[r[...] for r in a_refs])
            keep[0][...] = av
        else:
            av = a_refs[0][...].astype(BF16)
        part = _dot(av, b_ref[...].astype(BF16), dn)

        def finish(r):
            if epi is not None:
                r = epi(r, *[e[...] for e in extras])
            o_ref[...] = r.astype(o_ref.dtype)

        if nk == 1:
            finish(part)
        else:
            acc = scratch[0]

            @pl.when(ids[2] == 0)
            def _():
                acc[...] = part

            @pl.when(ids[2] > 0)
            def _():
                acc[...] += part

            @pl.when(ids[2] == nk - 1)
            def _():
                finish(acc[...])

        if carry:
            @pl.when((ids[0] == grid[0] - 1) & (ids[1] == grid[1] - 1) & (ids[2] == nk - 1))
            def _():
                for cp in carry.copies(c_in, c_out, *sems):
                    cp.wait()

    if dn == TN:
        a_specs = [pl.BlockSpec((tk, tm), lambda i, j, kk: (kk, i))]
    else:
        a_specs = [pl.BlockSpec((tm, tk), lambda i, j, kk, p=p: (i, kk + p * nk)) for p in range(n_parts)]
    b_spec = pl.BlockSpec((tn, tk), lambda i, j, kk: (j, kk)) if dn == NT else pl.BlockSpec((tk, tn), lambda i, j, kk: (kk, j))
    in_specs = a_specs + [b_spec] + [pl.BlockSpec((tm, tn), lambda i, j, kk: (i, j)) for _ in tiles]
    in_specs += [pl.BlockSpec((1, tn), lambda i, j, kk: (0, j)) for _ in rows] + [HBM_SPEC] * n_ci
    res = pl.pallas_call(
        body, name=name, grid=grid, in_specs=in_specs,
        out_specs=[pl.BlockSpec((tm, tn), lambda i, j, kk: (i, j))]
        + [pl.BlockSpec((tm, tk), lambda i, j, kk: (i, kk))] * n_keep + [HBM_SPEC] * n_co,
        out_shape=[SDS((m, n), out_dtype)] + [SDS((m, k), BF16)] * n_keep + (list(carry.outs) if carry else []),
        scratch_shapes=([pltpu.VMEM((tm, tn), F32)] if nk > 1 else []) + (list(carry.sems) if carry else []),
        compiler_params=pltpu.CompilerParams(
            dimension_semantics=("arbitrary",) * 3 if carry else ("parallel", "parallel", "arbitrary"),
            vmem_limit_bytes=VMEM_LIMIT),
    )(*[a] * n_parts, b, *tiles, *rows, *(carry.ins if carry else []))
    if carry:
        carry.then(res[1 + n_keep:])
    return (res[0], res[1]) if n_keep else res[0]


def _mean_all(v):
    return jnp.mean(v, axis=-1, keepdims=True)


def _head_sums(v, bd):
    w = bd.shape[0]
    return jnp.concatenate([_dot01(v[:, j:j + w], bd, 2) for j in range(0, v.shape[1], w)], axis=1)


def _mean_heads(bd):
    return lambda v: _head_sums(v, bd) * (1.0 / HD)


def _rms_fwd(x, g, mean):
    return x * lax.rsqrt(mean(x * x) + EPS) * g


def _rms_bwd(x, g, dy, mean):
    r = lax.rsqrt(mean(x * x) + EPS)
    dn = dy * g
    dx = r * dn - x * (r * r * r) * mean(dn * x)
    return dx, jnp.sum(dy * x * r, axis=0, keepdims=True)


def _swap_halves(x):
    w = x.shape[1]
    lane = lax.broadcasted_iota(jnp.int32, x.shape, 1)
    return jnp.where(lane % HD < HD // 2, pltpu.roll(x, w - HD // 2, 1), pltpu.roll(x, HD // 2, 1))


def _lanes(t, w):
    return jnp.tile(t, (1, w // t.shape[1]))


def _rope_fwd(x, cos, sin_signed):
    return x * _lanes(cos, x.shape[1]) + _swap_halves(x) * _lanes(sin_signed, x.shape[1])


def _rope_bwd(dy, cos, sin_signed):
    return dy * _lanes(cos, dy.shape[1]) + _swap_halves(dy * _lanes(sin_signed, dy.shape[1]))


def _bcast_heads(cols):
    return jnp.concatenate([jnp.broadcast_to(c, (c.shape[0], HD)) for c in cols], axis=1)


def _softplus(z):
    return jnp.maximum(z, 0.0) + jnp.log(1.0 + jnp.exp(-jnp.abs(z)))


def _block_diag(w):
    h = np.arange(w) // HD
    return jnp.asarray(h[:, None] == h[None, :], BF16)


def _sb_window(i, t):
    hi = (i + 1) * SB_QB - t * SB_WIN
    lo = hi - SB_WIN
    ws = pl.multiple_of(jnp.maximum(lo, 0), SB_QB)
    kpos = ws + lax.broadcasted_iota(jnp.int32, (SB_QB, SB_WIN), 1)
    qpos = i * SB_QB + lax.broadcasted_iota(jnp.int32, (SB_QB, SB_WIN), 0)
    return (kpos < qpos) & (kpos >= lo) & (kpos < hi), ws


def _sb_fwd(qkv):
    s = qkv.shape[0]
    assert s >= SB_WIN
    nq = s // SB_QB
    nh = SB_FWD_HEADS
    bw = HD * nh
    ngroups = SB_W // bw

    def body(q_ref, k_ref, v_ref, later_ref, o_ref, tot_ref, w0_ref, b0_ref, nb_ref):
        p, i = pl.program_id(0), pl.program_id(1)
        q = q_ref[...]
        later_of = later_ref[...]

        def window(t, tots, outs, keep):
            mask, ws = _sb_window(i, t)
            kw, vw = k_ref[pl.ds(ws, SB_WIN), :], v_ref[pl.ds(ws, SB_WIN), :]
            new_t, new_o, w_all, b_all = [], [], [], []
            for hh in range(nh):
                sl = slice(HD * hh, HD * hh + HD)
                z = _dot(q[:, sl], kw[:, sl], NT) * SCALE
                sp = _softplus(z)
                lf = jnp.where(mask, -sp, 0.0)
                later = tots[hh] + _dot01(lf, later_of, 2)
                w = jnp.where(mask, jnp.exp(z - sp + later), 0.0).astype(BF16)
                new_o.append(outs[hh] + _dot(w, vw[:, sl]))
                new_t.append(tots[hh] + jnp.sum(lf, axis=1, keepdims=True))
                if keep:
                    w_all.append(w)
                    b_all.append(jnp.where(mask, jnp.exp(z - sp), 0.0).astype(BF16))
            if keep:
                w0_ref[...] = jnp.concatenate(w_all, axis=1)
                b0_ref[...] = jnp.concatenate(b_all, axis=1)
            alive = functools.reduce(jnp.maximum, [jnp.max(v) for v in new_t])
            return t + 1, alive, tuple(new_t), tuple(new_o)

        zt, zo = jnp.zeros((SB_QB, 1), F32), jnp.zeros((SB_QB, HD), F32)
        first = window(jnp.int32(0), (zt,) * nh, (zo,) * nh, True)
        t, _, tots, outs = lax.while_loop(lambda c: ((i + 1) * SB_QB - c[0] * SB_WIN > 0) & (c[1] > EXHAUSTED),
                                          lambda c: window(c[0], c[2], c[3], False), first)
        o_ref[...] = jnp.concatenate(outs, axis=1).astype(o_ref.dtype)
        tot_ref[...] = _bcast_heads(tots)
        nb_ref[p, i] = t

    whole = lambda off: pl.BlockSpec((s, bw), lambda p, i: (0, off + p), pipeline_mode=pl.Buffered(1))
    tile = pl.BlockSpec((SB_QB, bw), lambda p, i: (i, p))
    tri = pl.BlockSpec((SB_WIN, SB_WIN), lambda p, i: (0, 0), pipeline_mode=pl.Buffered(1))
    near = pl.BlockSpec((SB_QB, nh * SB_WIN), lambda p, i: (i, p))
    n_heads = SB_W // HD
    idx = np.arange(SB_WIN)
    return pl.pallas_call(
        body, name="sb_fwd", grid=(ngroups, nq),
        in_specs=[tile, whole(ngroups), whole(2 * ngroups), tri],
        out_specs=[tile, tile, near, near, pl.BlockSpec(memory_space=pltpu.SMEM)],
        out_shape=[SDS((s, SB_W), BF16), SDS((s, SB_W), F32), SDS((s, n_heads * SB_WIN), BF16), SDS((s, n_heads * SB_WIN), BF16),
                   SDS((ngroups, nq), jnp.int32)],
        compiler_params=pltpu.CompilerParams(dimension_semantics=("arbitrary", "arbitrary"), vmem_limit_bytes=VMEM_LIMIT),
    )(qkv, qkv, qkv, jnp.asarray(idx[:, None] > idx[None, :], BF16))


def _sb_bwd(qkv, do, tot, nblk, w0, b0, buf, col):
    s = qkv.shape[0]
    nq = s // SB_QB
    npairs = SB_W // 128

    def body(nb_ref, q_ref, k_ref, v_ref, do_ref, tot_ref, upto_ref, before_ref, w0_ref, b0_ref, buf_ref,
             dq_ref, dk_ref, dv_ref):
        p, i = pl.program_id(0), pl.program_id(1)

        @pl.when(i == 0)
        def _():
            dk_ref[...] = jnp.zeros(dk_ref.shape, F32)
            dv_ref[...] = jnp.zeros(dv_ref.shape, F32)

        upto = upto_ref[...]
        before = before_ref[...]
        q, dout, tt = q_ref[...], do_ref[...], tot_ref[...]
        n = nb_ref[p * 2 // SB_FWD_HEADS, i]

        def step(it, c):
            pres, gpres, dqs = c
            mask, ws = _sb_window(i, n - 1 - it)
            kw, vw = k_ref[pl.ds(ws, SB_WIN), :], v_ref[pl.ds(ws, SB_WIN), :]
            new_p, new_g, new_dq, dks, dvs = [], [], [], [], []
            for hh in range(2):
                sl = slice(HD * hh, HD * hh + HD)
                z = _dot(q[:, sl], kw[:, sl], NT) * SCALE
                sp = _softplus(z)
                lf = jnp.where(mask, -sp, 0.0)
                later = tt[:, HD * hh:HD * hh + 1] - (pres[hh] + _dot01(lf, upto, 2))
                w = jnp.where(mask, jnp.exp(z - sp + later), 0.0)
                beta = jnp.exp(z - sp)
                g = _dot(dout[:, sl], vw[:, sl], NT) * w
                g_far = gpres[hh] + _dot(g.astype(BF16), before)
                dz = (jnp.where(mask, g * (1.0 - beta) - beta * g_far, 0.0) * SCALE).astype(BF16)
                new_dq.append(dqs[hh] + _dot(dz, kw[:, sl]))
                dks.append(_dot(dz, q[:, sl], TN))
                dvs.append(_dot(w.astype(BF16), dout[:, sl], TN))
                new_p.append(pres[hh] + jnp.sum(lf, axis=1, keepdims=True))
                new_g.append(gpres[hh] + jnp.sum(g, axis=1, keepdims=True))
            dk_ref[pl.ds(ws, SB_WIN), :] += jnp.concatenate(dks, axis=1)
            dv_ref[pl.ds(ws, SB_WIN), :] += jnp.concatenate(dvs, axis=1)
            return tuple(new_p), tuple(new_g), tuple(new_dq)

        zt, zo = jnp.zeros((SB_QB, 1), F32), jnp.zeros((SB_QB, HD), F32)
        _, gpres, dqs = lax.fori_loop(0, n - 1, step, ((zt, zt), (zt, zt), (zo, zo)))
        _, ws = _sb_window(i, 0)
        kw, vw = k_ref[pl.ds(ws, SB_WIN), :], v_ref[pl.ds(ws, SB_WIN), :]
        dqs, dks, dvs = list(dqs), [], []
        for hh in range(2):
            sl = slice(HD * hh, HD * hh + HD)
            w = w0_ref[:, SB_WIN * hh:SB_WIN * (hh + 1)]
            beta = b0_ref[:, SB_WIN * hh:SB_WIN * (hh + 1)].astype(F32)
            g = _dot(dout[:, sl], vw[:, sl], NT) * w.astype(F32)
            g_far = gpres[hh] + _dot(g.astype(BF16), before)
            dz = ((g * (1.0 - beta) - beta * g_far) * SCALE).astype(BF16)
            dqs[hh] = dqs[hh] + _dot(dz, kw[:, sl])
            dks.append(_dot(dz, q[:, sl], TN))
            dvs.append(_dot(w, dout[:, sl], TN))
        dk_ref[pl.ds(ws, SB_WIN), :] += jnp.concatenate(dks, axis=1)
        dv_ref[pl.ds(ws, SB_WIN), :] += jnp.concatenate(dvs, axis=1)
        dq_ref[...] = jnp.concatenate(dqs, axis=1).astype(dq_ref.dtype)

    whole_in = lambda off: pl.BlockSpec((s, 128), lambda p, i: (0, off + p), pipeline_mode=pl.Buffered(1))
    whole_out = pl.BlockSpec((s, 128), lambda p, i: (0, p), pipeline_mode=pl.Buffered(1))
    tile = pl.BlockSpec((SB_QB, 128), lambda p, i: (i, p))
    near = pl.BlockSpec((SB_QB, 2 * SB_WIN), lambda p, i: (i, p))
    dq_tile = pl.BlockSpec((SB_QB, 128), lambda p, i: (i, col // 128 + p))
    tri = pl.BlockSpec((SB_WIN, SB_WIN), lambda p, i: (0, 0), pipeline_mode=pl.Buffered(1))
    idx = np.arange(SB_WIN)
    return pl.pallas_call(
        body, name="sb_bwd", grid=(npairs, nq),
        in_specs=[pl.BlockSpec(memory_space=pltpu.SMEM), tile, whole_in(npairs), whole_in(2 * npairs), tile, tile, tri, tri,
                  near, near, HBM_SPEC],
        out_specs=[dq_tile, whole_out, whole_out],
        out_shape=[SDS(buf.shape, buf.dtype)] + [SDS((s, SB_W), F32)] * 2,
        input_output_aliases={10: 0},
        compiler_params=pltpu.CompilerParams(dimension_semantics=("arbitrary", "arbitrary"), vmem_limit_bytes=VMEM_LIMIT),
    )(nblk, qkv, qkv, qkv, do, tot, jnp.asarray(idx[:, None] <= idx[None, :], BF16), jnp.asarray(idx[:, None] < idx[None, :], BF16),
      w0, b0, buf)


def _dsa_mask(DSA_T, has_prev):
    r = lax.broadcasted_iota(jnp.int32, (DSA_T, QB + DSA_T), 0)
    j = lax.broadcasted_iota(jnp.int32, (DSA_T, QB + DSA_T), 1) - QB
    return (j <= r) & (j >= r - QB) & ((j >= 0) | has_prev)


def _dsa_fwd(q, k, v, dil):
    n = q.shape[0]
    DSA_T = DSA_T_FWD
    nt = n // DSA_T

    def body(q_ref, kc_ref, kp_ref, vc_ref, vp_ref, o_ref, lse_ref):
        mask = _dsa_mask(DSA_T, pl.program_id(1) > 0)
        outs, lses = [], []
        for hh in range(DSA_OUT_W // HD):
            sl = slice(HD * hh, HD * hh + HD)
            kcat = jnp.concatenate([kp_ref[:, sl], kc_ref[:, sl]], axis=0)
            vcat = jnp.concatenate([vp_ref[:, sl], vc_ref[:, sl]], axis=0)
            sc = jnp.where(mask, _dot(q_ref[:, sl], kcat, NT) * SCALE, NEG)
            m = jnp.max(sc, axis=1, keepdims=True)
            p = jnp.exp(sc - m)
            den = jnp.sum(p, axis=1, keepdims=True)
            outs.append(_dot(p.astype(BF16), vcat) / den)
            lses.append(m + jnp.log(den))
        o_ref[...] = jnp.concatenate(outs, axis=1)
        lse_ref[...] = _bcast_heads(lses)

    cur = pl.BlockSpec((DSA_T, DSA_OUT_W), lambda c, i: (i, c))
    prev = pl.BlockSpec((QB, DSA_OUT_W), lambda c, i: (jnp.maximum(i * (DSA_T // QB) - 1, 0), c))
    o, lse = pl.pallas_call(
        body, name=f"dsa_fwd_d{dil}", grid=(dil, nt), in_specs=[cur, cur, prev, cur, prev], out_specs=[cur, cur],
        out_shape=[SDS((n, dil * DSA_OUT_W), F32)] * 2,
        compiler_params=pltpu.CompilerParams(dimension_semantics=("parallel", "parallel")),
    )(q, k, k, v, v)
    return o, lse


def _dsa_bwd(q, k, v, do, cc, lse, dil):
    n = q.shape[0]
    DSA_T = DSA_T_BWD
    nt = n // DSA_T
    per = DSA_T // QB

    def body(qj_ref, qn_ref, kp_ref, kj_ref, vp_ref, vj_ref, doj_ref, don_ref, cj_ref, cn_ref, lj_ref, ln_ref,
             dq_ref, dk_ref, dv_ref):
        j = pl.program_id(1)
        mask = _dsa_mask(DSA_T, j > 0)
        r = lax.broadcasted_iota(jnp.int32, (QB, DSA_T), 0)
        kk = lax.broadcasted_iota(jnp.int32, (QB, DSA_T), 1)
        m_next = (kk >= r + QB) & (j + 1 < nt)
        dqs, dks, dvs = [], [], []
        for hh in range(DSA_OUT_W // HD):
            sl = slice(HD * hh, HD * hh + HD)
            one = slice(HD * hh, HD * hh + 1)
            qj, qn, kj, vj, doj, don = (t[:, sl] for t in (qj_ref, qn_ref, kj_ref, vj_ref, doj_ref, don_ref))
            kcat = jnp.concatenate([kp_ref[:, sl], kj], axis=0)
            vcat = jnp.concatenate([vp_ref[:, sl], vj], axis=0)
            p1 = jnp.where(mask, jnp.exp(_dot(qj, kcat, NT) * SCALE - lj_ref[:, one]), 0.0)
            ds1 = (p1 * (_dot(doj, vcat, NT) + cj_ref[:, one]) * SCALE).astype(BF16)
            p2 = jnp.where(m_next, jnp.exp(_dot(qn, kj, NT) * SCALE - ln_ref[:, one]), 0.0)
            ds2 = (p2 * (_dot(don, vj, NT) + cn_ref[:, one]) * SCALE).astype(BF16)
            dqs.append(_dot(ds1, kcat))
            dks.append(_dot(ds1[:, QB:], qj, TN) + _dot(ds2, qn, TN))
            dvs.append(_dot(p1[:, QB:].astype(BF16), doj, TN) + _dot(p2.astype(BF16), don, TN))
        dq_ref[...] = jnp.concatenate(dqs, axis=1).astype(dq_ref.dtype)
        dk_ref[...] = jnp.concatenate(dks, axis=1).astype(dk_ref.dtype)
        dv_ref[...] = jnp.concatenate(dvs, axis=1).astype(dv_ref.dtype)

    cur = pl.BlockSpec((DSA_T, DSA_OUT_W), lambda c, j: (j, c))
    prev = pl.BlockSpec((QB, DSA_OUT_W), lambda c, j: (jnp.maximum(j * per - 1, 0), c))
    nxt = pl.BlockSpec((QB, DSA_OUT_W), lambda c, j: (jnp.minimum((j + 1) * per, n // QB - 1), c))
    dq, dk, dv = pl.pallas_call(
        body, name=f"dsa_bwd_d{dil}", grid=(dil, nt),
        in_specs=[cur, nxt, prev, cur, prev, cur, cur, nxt, cur, nxt, cur, nxt], out_specs=[cur, cur, cur],
        out_shape=[SDS((n, dil * DSA_OUT_W), BF16)] * 3,
        compiler_params=pltpu.CompilerParams(dimension_semantics=("parallel", "parallel")),
    )(q, q, k, k, v, v, do, do, cc, cc, lse, lse)
    return dq, dk, dv


def _ffn_fwd(tag, x, gain, w13, w2, plan=None):
    n = _tokmap(f"{tag}_norm", lambda xv, g: _rms_fwd(xv, g, _mean_all), [x], [gain], [(D, BF16)])[0]
    ab = _matmul(f"{tag}_up", n, w13, NN, BF16, plan=plan)

    def gate(av, bv):
        a, b = av.astype(F32), bv.astype(F32)
        return (a * jax.nn.sigmoid(a) * b).astype(BF16)

    y, h = _matmul(f"{tag}_down", ab, w2(), NN, F32, epi=lambda acc, res: res + 0.5 * acc, tiles=[x], plan=plan,
                   a_pro=(gate, 2))
    return y, (n, ab, h)


def _ffn_bwd(tag, x, gain, w13, w2, saved, dy, plan=None, on_dw=None):
    n, ab, h = saved
    dh = _matmul(f"{tag}_bwd_dh", dy, w2, NT, BF16, epi=lambda acc: 0.5 * acc)

    def gate_bwd(abv, dhv):
        a, b, dhf = abv[:, :D_FF].astype(F32), abv[:, D_FF:].astype(F32), dhv.astype(F32)
        sg = jax.nn.sigmoid(a)
        da = dhf * b * (sg * (1.0 + a * (1.0 - sg)))
        return jnp.concatenate([da, dhf * (a * sg)], axis=1)

    dab = _tokmap(f"{tag}_bwd_gate", gate_bwd, [ab, dh], [], [(2 * D_FF, BF16)], tile=256)[0]
    dw2 = _matmul(f"{tag}_bwd_dw2", h, dy, TN, F32, epi=lambda acc: 0.5 * acc)
    dw13 = _matmul(f"{tag}_bwd_dw13", n, dab, TN, F32, plan=plan)
    if on_dw is not None:
        on_dw(dw13, dw2)
    dn = _matmul(f"{tag}_bwd_dn", dab, w13, NT, F32, plan=plan)

    def norm_bwd(xv, dnv, dyv, g):
        dx, dg = _rms_bwd(xv, g, dnv, _mean_all)
        return dx + dyv, dg

    dx, dgain = _tokmap(f"{tag}_bwd_norm", norm_bwd, [x, dn, dy], [gain], [(D, F32)], [(1, D)])
    return dx, dgain, dw13, dw2


def _rope_tables(s):
    half = HD // 2
    inv_freq = jnp.power(10000.0, -jnp.arange(half, dtype=F32) / half)
    ang = jnp.arange(s).astype(F32)[:, None] * inv_freq[None, :]
    cos, sin = jnp.cos(ang), jnp.sin(ang)
    return jnp.tile(jnp.concatenate([cos, cos], axis=1), (1, 2)), jnp.tile(jnp.concatenate([-sin, sin], axis=1), (1, 2))


def _local_step(x, mem, tgt, w, sm, plan=None, on_grads=None):
    s = x.shape[0]
    assert s % (max(DSA_T_FWD, DSA_T_BWD) * max(DSA_DILS)) == 0
    on_grads = on_grads or (lambda group, grads: None)
    c_sb, c_dsa, c_qm, c_all = 3 * D, 3 * D + 3 * SB_W, 3 * D + 3 * SB_W + 3 * DSA_W, 3 * D + 4096
    cos, sin = _rope_tables(s)
    bd768 = bd256 = _block_diag(128)
    gq_dsa, gk_dsa = jnp.tile(sm["qn_dsa"], (1, DSA_W // HD)), jnp.tile(sm["kn_dsa"], (1, DSA_W // HD))
    gq_mem, gk_mem = jnp.tile(sm["qn_mem"], (1, MEM_W // HD)), jnp.tile(sm["kn_mem"], (1, MEM_W // HD))

    w13_1 = jnp.concatenate([w["ffn1_w1"], w["ffn1_w3"]], axis=1)
    x1, ffn1_saved = _ffn_fwd("ffn1", x, sm["ffn1_norm"], w13_1, lambda: w["ffn1_w2"], plan)
    w_all = jnp.concatenate([w["w_gate"], w["w_in"]], axis=1)
    wb_sb, wb_dsa, wb_mem = w["w_branch_sb"], w["w_branch_dsa"], w["w_branch_mem"]
    hmix = _tokmap("mix_norm", lambda xv, g: _rms_fwd(xv, g, _mean_all), [x1], [sm["mix_norm"]], [(D, BF16)])[0]
    qkv_sb = _matmul("proj_sb", hmix, w_all[:, c_sb:c_dsa], NN, BF16)
    qkv_dsa = _matmul("proj_dsa", hmix, w_all[:, c_dsa:c_qm], NN, BF16, plan=plan)
    q_mem = _matmul("proj_qmem", hmix, w_all[:, c_qm:], NN, BF16)
    gpre = _matmul("proj_gate", hmix, w_all[:, :c_sb], NN, BF16, epi=lambda acc, b: acc + b, rows=[sm["b_gate"]], plan=plan)

    o_sb, sb_tot, sb_w0, sb_b0, sb_nblk = _sb_fwd(qkv_sb)

    def dsa_prep(qkv, cs, sn, gq, gk, bd):
        mean = _mean_heads(bd)
        qn = _rope_fwd(_rms_fwd(qkv[:, :DSA_W].astype(F32), gq, mean), cs, sn)
        kn = _rope_fwd(_rms_fwd(qkv[:, DSA_W:2 * DSA_W].astype(F32), gk, mean), cs, sn)
        v = qkv[:, 2 * DSA_W:]
        outs = []
        for t in (qn, kn, v):
            outs += [t[:, DSA_OUT_W * g:DSA_OUT_W * (g + 1)] for g in range(3)]
        return outs

    dsa_in = _tokmap("dsa_prep", dsa_prep, [qkv_dsa, cos, sin], [gq_dsa, gk_dsa, bd768], [(DSA_OUT_W, BF16)] * 9, tile=256,
                     dil_outs={j: DSA_DILS[j % 3] for j in range(9)})
    dsa_q, dsa_k, dsa_v = dsa_in[0:3], dsa_in[3:6], dsa_in[6:9]
    dsa_o, dsa_lse = zip(*[_dsa_fwd(dsa_q[g], dsa_k[g], dsa_v[g], DSA_DILS[g]) for g in range(3)])

    def alphas(l0, l1, l2):
        m = jnp.maximum(jnp.maximum(l0, l1), l2)
        e = [jnp.exp(l - m) for l in (l0, l1, l2)]
        tot = e[0] + e[1] + e[2]
        return [t / tot for t in e]

    def dsa_mix(o0, o1, o2, l0, l1, l2):
        a = alphas(l0, l1, l2)
        return a[0] * o0 + a[1] * o1 + a[2] * o2

    o_dsa = _tokmap("dsa_mix", dsa_mix, [*dsa_o, *dsa_lse], [], [(DSA_OUT_W, BF16)], tile=256,
                    dil_ins={j: DSA_DILS[j % 3] for j in range(6)})[0]

    def mem_kv(memv, g, wkv, gk, bd):
        kv = _dot(_rms_fwd(memv, g, _mean_all).astype(BF16), wkv)
        return _rms_fwd(kv[:, :MEM_W], gk, _mean_heads(bd)), kv[:, MEM_W:]

    km, vm = _tokmap("mem_kv", mem_kv, [mem], [sm["mem_norm"], w["w_mem_kv"], gk_mem, bd256], [(MEM_W, BF16)] * 2)

    def mem_probs(qv, kmv, gq, bd):
        qn = _rms_fwd(qv.astype(F32), gq, _mean_heads(bd)).astype(BF16)
        ps = []
        for h in range(MEM_W // HD):
            sl = slice(HD * h, HD * h + HD)
            sc = _dot(qn[:, sl], kmv[:, sl], NT) * SCALE
            e = jnp.exp(sc - jnp.max(sc, axis=1, keepdims=True))
            ps.append(e / jnp.sum(e, axis=1, keepdims=True))
        return qn, ps

    def mem_attn(qv, kmv, vmv, gq, bd):
        _, ps = mem_probs(qv, kmv, gq, bd)
        return jnp.concatenate([_dot(p.astype(BF16), vmv[:, HD * h:HD * h + HD]) for h, p in enumerate(ps)], axis=1)

    o_mem = _tokmap("mem_attn", mem_attn, [q_mem], [km, vm, gq_mem, bd256], [(MEM_W, BF16)])[0]

    def merge(osb, odsa, omem, gp, w_sb, w_dsa, w_mem):
        gates = jax.nn.sigmoid(gp.astype(F32))
        ys = (_dot(osb, w_sb), _dot(odsa, w_dsa), _dot(omem, w_mem))
        return gates, ys, gates[:, :D] * ys[0] + gates[:, D:2 * D] * ys[1] + gates[:, 2 * D:] * ys[2]

    merged = _tokmap("merge", lambda *a: merge(*a)[2], [o_sb, o_dsa, o_mem, gpre], [wb_sb, wb_dsa, wb_mem], [(D, BF16)],
                     tile=256)[0]
    x2 = _matmul("out_proj", merged, w["w_out"], NN, F32, epi=lambda acc, res: res + acc, tiles=[x1])
    w13_2 = jnp.concatenate([w["ffn2_w1"], w["ffn2_w3"]], axis=1)
    y, ffn2_saved = _ffn_fwd("ffn2", x2, sm["ffn2_norm"], w13_2, lambda: w["ffn2_w2"])

    def loss_fn(yv, tv):
        e = yv - tv
        part = 0.5 * jnp.sum(jnp.mean(e * e, axis=1, keepdims=True), axis=0, keepdims=True)
        return e * (1.0 / D), jnp.broadcast_to(part, (1, 128))

    dy, loss = _tokmap("loss", loss_fn, [y, tgt], [], [(D, F32)], [(1, 128)])

    gw, gs = {}, {}
    def ffn_grads(tag):
        def on_dw(dw13, dw2):
            gw[f"{tag}_w1"], gw[f"{tag}_w3"], gw[f"{tag}_w2"] = dw13[:, :D_FF], dw13[:, D_FF:], dw2
            on_grads(tag, {n: gw[n] for n in (f"{tag}_w1", f"{tag}_w3", f"{tag}_w2")})
        return on_dw

    dx2, gs["ffn2_norm"], _, _ = _ffn_bwd("ffn2", x2, sm["ffn2_norm"], w13_2, w["ffn2_w2"], ffn2_saved, dy, plan,
                                          ffn_grads("ffn2"))
    dmerged = _matmul("out_proj_bwd_dx", dx2, w["w_out"], NT, BF16)
    gw["w_out"] = _matmul("out_proj_bwd_dw", merged, dx2, TN, F32)

    def merge_bwd(osb, odsa, omem, gp, dm, w_sb, w_dsa, w_mem):
        gates, ys, _ = merge(osb, odsa, omem, gp, w_sb, w_dsa, w_mem)
        dmf = dm.astype(F32)
        dgp, dos, dws = [], [], []
        for b, (ov, wv) in enumerate(((osb, w_sb), (odsa, w_dsa), (omem, w_mem))):
            gb = gates[:, D * b:D * (b + 1)]
            dgp.append(dmf * ys[b] * gb * (1.0 - gb))
            dyb = (dmf * gb).astype(BF16)
            dos.append(_dot(dyb, wv, NT))
            dws.append(_dot(ov, dyb, TN))
        dgp = jnp.concatenate(dgp, axis=1)
        return dos[0], dos[1], dos[2], dgp, dws[0], dws[1], dws[2], jnp.sum(dgp, axis=0, keepdims=True)

    do_sb, do_dsa, do_mem, dgpre, gw["w_branch_sb"], gw["w_branch_dsa"], gw["w_branch_mem"], gs["b_gate"] = _tokmap(
        "merge_bwd", merge_bwd, [o_sb, o_dsa, o_mem, gpre, dmerged], [wb_sb, wb_dsa, wb_mem],
        [(SB_W, BF16), (DSA_OUT_W, F32), (MEM_W, BF16), (3 * D, BF16)],
        [(SB_W, D), (DSA_OUT_W, D), (MEM_W, D), (1, 3 * D)], tile=256, place={3: (c_all, 0, None)})

    dall, dk_sb, dv_sb = _sb_bwd(qkv_sb, do_sb, sb_tot, sb_nblk, sb_w0, sb_b0, dgpre, c_sb)
    dall = lax.dynamic_update_slice(dall, dk_sb.astype(BF16), (0, c_sb + SB_W))
    dall = lax.dynamic_update_slice(dall, dv_sb.astype(BF16), (0, c_sb + 2 * SB_W))

    def dsa_mix_bwd(o0, o1, o2, l0, l1, l2, dov, bd):
        a = alphas(l0, l1, l2)
        omix = a[0] * o0 + a[1] * o1 + a[2] * o2
        dot_o = _head_sums(dov * omix, bd)
        return [dov * t for t in a] + [-t * dot_o for t in a]

    mixb = _tokmap("dsa_mix_bwd", dsa_mix_bwd, [*dsa_o, *dsa_lse, do_dsa], [bd256],
                   [(DSA_OUT_W, BF16)] * 3 + [(DSA_OUT_W, F32)] * 3, tile=256,
                   dil_ins={j: DSA_DILS[j % 3] for j in range(6)}, dil_outs={j: DSA_DILS[j % 3] for j in range(6)})
    dsa_d = [_dsa_bwd(dsa_q[g], dsa_k[g], dsa_v[g], mixb[g], mixb[3 + g], dsa_lse[g], DSA_DILS[g]) for g in range(3)]

    def dsa_prep_bwd(qkv, cs, sn, *rest):
        dqs, dks, dvs, (gq, gk, bd) = rest[0:3], rest[3:6], rest[6:9], rest[9:]
        mean = _mean_heads(bd)
        dq, dgq = _rms_bwd(qkv[:, :DSA_W].astype(F32), gq, _rope_bwd(jnp.concatenate(dqs, axis=1), cs, sn), mean)
        dk, dgk = _rms_bwd(qkv[:, DSA_W:2 * DSA_W].astype(F32), gk, _rope_bwd(jnp.concatenate(dks, axis=1), cs, sn), mean)
        return jnp.concatenate([dq, dk] + list(dvs), axis=1), dgq, dgk

    dall, dgq_dsa, dgk_dsa = _tokmap(
        "dsa_prep_bwd", dsa_prep_bwd,
        [qkv_dsa, cos, sin] + [dsa_d[g][t] for t in range(3) for g in range(3)], [gq_dsa, gk_dsa, bd768],
        [(3 * DSA_W, BF16)], [(1, DSA_W), (1, DSA_W)], tile=256, dil_ins={3 + j: DSA_DILS[j % 3] for j in range(9)},
        place={0: (c_all, c_dsa // (3 * DSA_W), dall)})
    gs["qn_dsa"] = dgq_dsa.reshape(DSA_W // HD, HD).sum(axis=0, keepdims=True)
    gs["kn_dsa"] = dgk_dsa.reshape(DSA_W // HD, HD).sum(axis=0, keepdims=True)

    def mem_attn_bwd(qv, dov, kmv, vmv, gq, bd):
        qn, ps = mem_probs(qv, kmv, gq, bd)
        dqn, dkm, dvm = [], [], []
        for h, p in enumerate(ps):
            sl = slice(HD * h, HD * h + HD)
            dp = _dot(dov[:, sl], vmv[:, sl], NT)
            ds = (p * (dp - jnp.sum(p * dp, axis=1, keepdims=True)) * SCALE).astype(BF16)
            dqn.append(_dot(ds, kmv[:, sl]))
            dkm.append(_dot(ds, qn[:, sl], TN))
            dvm.append(_dot(p.astype(BF16), dov[:, sl], TN))
        dq, dgq = _rms_bwd(qv.astype(F32), gq, jnp.concatenate(dqn, axis=1), _mean_heads(bd))
        return dq, jnp.concatenate(dkm, axis=1), jnp.concatenate(dvm, axis=1), dgq

    dall, dkm, dvm, dgq_mem = _tokmap("mem_attn_bwd", mem_attn_bwd, [q_mem, do_mem], [km, vm, gq_mem, bd256],
                                      [(MEM_W, BF16)], [(MEM_LEN, MEM_W), (MEM_LEN, MEM_W), (1, MEM_W)],
                                      place={0: (c_all, c_qm // MEM_W, dall)})
    gs["qn_mem"] = dgq_mem.reshape(MEM_W // HD, HD).sum(axis=0, keepdims=True)

    def mem_kv_bwd(memv, dkmv, dvmv, g, wkv, gk, bd):
        memn = _rms_fwd(memv, g, _mean_all).astype(BF16)
        kv = _dot(memn, wkv)
        dk, dgk = _rms_bwd(kv[:, :MEM_W], gk, dkmv, _mean_heads(bd))
        dkv = jnp.concatenate([dk, dvmv], axis=1).astype(BF16)
        _, dg = _rms_bwd(memv, g, _dot(dkv, wkv, NT), _mean_all)
        return _dot(memn, dkv, TN), dg, dgk

    gw["w_mem_kv"], gs["mem_norm"], dgk_mem = _tokmap(
        "mem_kv_bwd", mem_kv_bwd, [mem, dkm, dvm], [sm["mem_norm"], w["w_mem_kv"], gk_mem, bd256], [],
        [(D, 2 * MEM_W), (1, D), (1, MEM_W)])
    gs["kn_mem"] = dgk_mem.reshape(MEM_W // HD, HD).sum(axis=0, keepdims=True)

    dhmix = _matmul("proj_bwd_dx", dall, w_all, NT, F32)
    dw_all = _matmul("proj_bwd_dw", hmix, dall, TN, F32)
    gw["w_gate"], gw["w_in"] = dw_all[:, :c_sb], dw_all[:, c_sb:]
    on_grads("mid", {n: gw[n] for n in GROUPS["mid"]})

    def mix_norm_bwd(xv, dnv, dyv, g):
        dx, dg = _rms_bwd(xv, g, dnv, _mean_all)
        return dx + dyv, dg

    dx1, gs["mix_norm"] = _tokmap("mix_norm_bwd", mix_norm_bwd, [x1, dhmix, dx2], [sm["mix_norm"]], [(D, F32)], [(1, D)])
    gx, gs["ffn1_norm"], _, _ = _ffn_bwd("ffn1", x, sm["ffn1_norm"], w13_1, w["ffn1_w2"], ffn1_saved, dx1, plan,
                                         ffn_grads("ffn1"))
    return loss, gx, gw, gs


def _shard_shape(name):
    shape, axis = SHARDED_BY_NAME[name]
    return (shape[0] // N_CHIPS, shape[1]) if axis == 0 else (shape[0], shape[1] // N_CHIPS)


def _full_from_shards(name, shards):
    axis = SHARDED_BY_NAME[name][1]
    return shards.reshape(SHARDED_BY_NAME[name][0]) if axis == 0 else jnp.concatenate(list(shards), axis=1)


def _shards_from_full(name, full, dtype):
    axis, n = SHARDED_BY_NAME[name][1], _shard_shape(name)
    return jnp.stack([lax.slice_in_dim(full, c * n[axis], (c + 1) * n[axis], axis=axis).astype(dtype) for c in range(N_CHIPS)])


def _own_shard(name, full, chip):
    axis, n = SHARDED_BY_NAME[name][1], _shard_shape(name)
    return lax.dynamic_slice_in_dim(full, chip * n[axis], n[axis], axis=axis)


SMALL_USED = sum(n for _, n in SMALL)


def _pack_small(d, loss=None):
    parts = [d[n].reshape(-1) for n, _ in SMALL]
    parts.append(jnp.zeros((1,), F32) if loss is None else loss.reshape(1))
    parts.append(jnp.zeros((SMALL_ROWS * D - SMALL_USED - 1,), F32))
    return jnp.concatenate(parts).reshape(SMALL_ROWS, D)


def _unpack_small(v):
    flat, out, r = v.reshape(-1), {}, 0
    for n, k in SMALL:
        out[n] = flat[r:r + k]
        r += k
    return out, flat[r]


def _place():
    return lax.axis_index("x"), lax.axis_index("y"), lax.axis_index("c")


def _other_chips(x, y):
    return [(1 - x, y), (x, 1 - y), (1 - x, 1 - y)]


HBM_SPEC = pl.BlockSpec(memory_space=pl.ANY)


def _chip_sems(n):
    return (pltpu.SemaphoreType.DMA((3 * n,)), pltpu.SemaphoreType.DMA((3 * n,)), pltpu.SemaphoreType.DMA((n,)))


def _gather_copies(ins, outs, send_sems, recv_sems, local_sems):
    x, y, c = _place()
    me = 2 * x + y
    copies = []
    for a, (src, out) in enumerate(zip(ins, outs)):
        copies.append(pltpu.make_async_copy(src, out.at[me], local_sems.at[a]))
        copies += [pltpu.make_async_remote_copy(src_ref=src, dst_ref=out.at[me], send_sem=send_sems.at[3 * a + k],
                                                recv_sem=recv_sems.at[3 * a + k], device_id=(px, py, c), device_id_type=MESH)
                   for k, (px, py) in enumerate(_other_chips(x, y))]
    return copies


def _scatter_copies(ins, outs, send_sems, recv_sems, local_sems):
    x, y, c = _place()
    return [pltpu.make_async_remote_copy(src_ref=src.at[2 * px + py], dst_ref=out.at[k], send_sem=send_sems.at[3 * a + k],
                                         recv_sem=recv_sems.at[3 * a + k], device_id=(px, py, c), device_id_type=MESH)
            for a, (src, out) in enumerate(zip(ins, outs)) for k, (px, py) in enumerate(_other_chips(x, y))]


def _all_gather_chips(arrays):
    n = len(arrays)

    def body(*refs):
        copies = _gather_copies(refs[:n], refs[n:2 * n], *refs[2 * n:])
        for cp in copies:
            cp.start()
        for cp in copies:
            cp.wait()

    return pl.pallas_call(
        body, name="weights_all_gather", in_specs=[HBM_SPEC] * n, out_specs=[HBM_SPEC] * n,
        out_shape=[SDS((N_CHIPS,) + a.shape, a.dtype) for a in arrays], scratch_shapes=list(_chip_sems(n)),
    )(*arrays)


def _swap_with_sibling(name, arrays):
    n = len(arrays)

    def body(*refs):
        x, y, c = _place()
        send_sems, recv_sems = refs[2 * n:]
        copies = [pltpu.make_async_remote_copy(src_ref=refs[a], dst_ref=refs[n + a], send_sem=send_sems.at[a],
                                               recv_sem=recv_sems.at[a], device_id=(x, y, 1 - c), device_id_type=MESH)
                  for a in range(n)]
        for cp in copies:
            cp.start()
        for cp in copies:
            cp.wait()

    return pl.pallas_call(
        body, name=name, in_specs=[HBM_SPEC] * n, out_specs=[HBM_SPEC] * n, out_shape=[SDS(a.shape, a.dtype) for a in arrays],
        scratch_shapes=[pltpu.SemaphoreType.DMA((n,)), pltpu.SemaphoreType.DMA((n,))],
    )(*arrays)


def _all_reduce_small(v):
    n_dev = 8

    def body(v_ref, out_ref, land, send_sems, recv_sems):
        x, y, c = _place()
        me = 4 * x + 2 * y + c
        land[me] = v_ref[...]
        copies = []
        for k in range(1, n_dev):
            peer = (x ^ (k >> 2), y ^ ((k >> 1) & 1), c ^ (k & 1))
            copies.append(pltpu.make_async_remote_copy(src_ref=v_ref, dst_ref=land.at[me], send_sem=send_sems.at[k - 1],
                                                       recv_sem=recv_sems.at[k - 1], device_id=peer, device_id_type=MESH))
        for cp in copies:
            cp.start()
        for cp in copies:
            cp.wait()
        acc = land[0]
        for d in range(1, n_dev):
            acc = acc + land[d]
        out_ref[...] = acc

    return pl.pallas_call(
        body, name="small_all_reduce", in_specs=[pl.BlockSpec(memory_space=pltpu.VMEM)],
        out_specs=pl.BlockSpec(memory_space=pltpu.VMEM), out_shape=SDS(v.shape, v.dtype),
        scratch_shapes=[pltpu.VMEM((n_dev,) + v.shape, v.dtype), pltpu.SemaphoreType.DMA((n_dev - 1,)),
                        pltpu.SemaphoreType.DMA((n_dev - 1,))],
    )(v)


def _adamw(g, wv, m, v):
    m = ADAM_B1 * m + (1.0 - ADAM_B1) * g
    v = ADAM_B2 * v + (1.0 - ADAM_B2) * (g * g)
    m_hat = m / (1.0 - ADAM_B1 ** ADAM_STEP)
    v_hat = v / (1.0 - ADAM_B2 ** ADAM_STEP)
    delta = -ADAM_LR * (m_hat / (jnp.sqrt(v_hat) + ADAM_EPS) + ADAM_WD * wv)
    return delta, m, v


def kernel(x, mem, ffn1_norm, ffn1_w1, ffn1_w3, ffn1_w2, mix_norm, mem_norm, w_in, w_mem_kv, qn_dsa, kn_dsa, qn_mem, kn_mem, w_branch_sb, w_branch_dsa, w_branch_mem, w_gate, b_gate, w_out, ffn2_norm, ffn2_w1, ffn2_w3, ffn2_w2, loss_target, m_ffn1_norm, m_ffn1_w1, m_ffn1_w3, m_ffn1_w2, m_mix_norm, m_mem_norm, m_w_in, m_w_mem_kv, m_qn_dsa, m_kn_dsa, m_qn_mem, m_kn_mem, m_w_branch_sb, m_w_branch_dsa, m_w_branch_mem, m_w_gate, m_b_gate, m_w_out, m_ffn2_norm, m_ffn2_w1, m_ffn2_w3, m_ffn2_w2, v_ffn1_norm, v_ffn1_w1, v_ffn1_w3, v_ffn1_w2, v_mix_norm, v_mem_norm, v_w_in, v_w_mem_kv, v_qn_dsa, v_kn_dsa, v_qn_mem, v_kn_mem, v_w_branch_sb, v_w_branch_dsa, v_w_branch_mem, v_w_gate, v_b_gate, v_w_out, v_ffn2_norm, v_ffn2_w1, v_ffn2_w3, v_ffn2_w2):
    given = dict(locals())
    wts = {n: given[n][0] for n in WEIGHTS}
    moms = {n: given["m_" + n][0] for n in WEIGHTS}
    vars_ = {n: given["v_" + n][0] for n in WEIGHTS}

    plan = _Plan()
    x_i, y_i, _ = _place()
    my_chip = 2 * x_i + y_i

    full = {}

    def gathered(names):
        return lambda res: full.update({n: _full_from_shards(n, g) for n, g in zip(names, res)})

    for host, names in WEIGHT_PIECES:
        shards = [wts[n].astype(BF16) for n in names]
        if host is None:
            gathered(names)(_all_gather_chips(shards))
        else:
            plan.put(host, _Carry(shards, [SDS((N_CHIPS,) + a.shape, BF16) for a in shards], _chip_sems(len(names)),
                                  _gather_copies, gathered(names)))
    small = {n: wts[n].reshape(1, -1) for n, _ in SMALL}

    landed = {}

    def on_grads(group, grads):
        names = GROUPS[group]
        slices = [_shards_from_full(n, grads[n], BF16) for n in names]
        own = [_own_shard(n, grads[n], my_chip) for n in names]
        plan.put(GRAD_HOSTS[group], _Carry(slices, [SDS((3,) + a.shape[1:], BF16) for a in slices], _chip_sems(len(names)),
                                           _scatter_copies, lambda res: landed.update({group: (own, res)})))

    loss, gx, _, gs = _local_step(x[0], mem[0], loss_target[0], full, small, plan, on_grads)
    assert not plan.pending, list(plan.pending)

    def update(hv, ov, wv, mv, vv):
        g = hv + ov
        return (g,) + _adamw(g, wv, mv, vv)

    outs = [{}, {}, {}, {}]
    for group, names in GROUPS.items():
        own, got = landed[group]
        halves = [_tokmap(f"grads_sum_chips_{n}",
                          lambda a, b0, b1, b2: ((a + b0.astype(F32)) + b1.astype(F32)) + b2.astype(F32),
                          [o, g[0], g[1], g[2]], [], [(o.shape[1], F32)])[0] for n, o, g in zip(names, own, got)]
        others = _swap_with_sibling(f"grads_swap_cores_{group}", halves)
        for n, half, other in zip(names, halves, others):
            res = _tokmap(f"adamw_{n}", update, [half, other, wts[n], moms[n], vars_[n]], [], [(half.shape[1], F32)] * 4)
            for d, r in zip(outs, res):
                d[n] = r

    s_red = _all_reduce_small(_pack_small(gs, loss[0, 0]))
    res = _tokmap(
        "adamw_small", lambda g, wv, mv, vv: (g,) + _adamw(g, wv, mv, vv),
        [s_red, _pack_small(small), _pack_small({n: moms[n] for n, _ in SMALL}), _pack_small({n: vars_[n] for n, _ in SMALL})],
        [], [(D, F32)] * 4)
    for d, packed in zip(outs, res):
        d.update(_unpack_small(packed)[0])
    _, total_loss = _unpack_small(s_red)
    return (total_loss, gx[None], *[d[n][None] for d in outs for n in WEIGHTS])
```

```python
import functools

import numpy as np
import jax
import jax.numpy as jnp
from jax import lax
from jax.experimental import pallas as pl
from jax.experimental.pallas import tpu as pltpu

F32, BF16 = jnp.float32, jnp.bfloat16
SDS = jax.ShapeDtypeStruct
MESH = pl.DeviceIdType.MESH

D = 1024
HD = 64
QB = 128
DSA_T_FWD, DSA_T_BWD = 512, 256
D_FF = 2816
SB_W, DSA_W, DSA_OUT_W, MEM_W = 512, 768, 256, 256
DSA_DILS = (1, 4, 16)
MEM_LEN = 256
N_CHIPS = 4
EPS = 1e-6
SCALE = HD ** -0.5
EXHAUSTED = -104.0
SB_FWD_HEADS = 4
SB_QB = 256
SB_WIN = 512
NEG = -1e30
VMEM_LIMIT = 56 * 1024 * 1024

ADAM_LR, ADAM_B1, ADAM_B2, ADAM_EPS, ADAM_WD, ADAM_STEP = 0.001, 0.9, 0.999, 1e-08, 0.01, 10

NN = (((1,), (0,)), ((), ()))
NT = (((1,), (1,)), ((), ()))
TN = (((0,), (0,)), ((), ()))

SHARDED = (
    ("ffn1_w1", (D, D_FF), 1), ("ffn1_w3", (D, D_FF), 1), ("ffn1_w2", (D_FF, D), 0),
    ("w_in", (D, 4096), 1), ("w_mem_kv", (D, 512), 0),
    ("w_branch_sb", (SB_W, D), 1), ("w_branch_dsa", (DSA_OUT_W, D), 1), ("w_branch_mem", (MEM_W, D), 1),
    ("w_gate", (D, 3 * D), 1), ("w_out", (D, D), 0),
    ("ffn2_w1", (D, D_FF), 1), ("ffn2_w3", (D, D_FF), 1), ("ffn2_w2", (D_FF, D), 0),
)
SHARDED_BY_NAME = {n: (sh, ax) for n, sh, ax in SHARDED}
GROUPS = {
    "ffn2": ("ffn2_w1", "ffn2_w3", "ffn2_w2"),
    "mid": ("w_in", "w_mem_kv", "w_branch_sb", "w_branch_dsa", "w_branch_mem", "w_gate", "w_out"),
    "ffn1": ("ffn1_w1", "ffn1_w3", "ffn1_w2"),
}
WEIGHT_PIECES = (
    (None, ("ffn1_w1", "ffn1_w3")),
    ("ffn1_up", ("ffn1_w2", "w_in")),
    ("ffn1_down", ("w_gate", "w_mem_kv", "w_branch_sb", "w_branch_dsa", "w_branch_mem", "w_out")),
    ("proj_dsa", ("ffn2_w2",)),
    ("proj_gate", ("ffn2_w1", "ffn2_w3")),
)
GRAD_HOSTS = {"ffn2": "ffn2_bwd_dn", "mid": "ffn1_bwd_dw13", "ffn1": "ffn1_bwd_dn"}
SMALL = (("ffn1_norm", D), ("mix_norm", D), ("mem_norm", D), ("ffn2_norm", D), ("b_gate", 3 * D),
         ("qn_dsa", HD), ("kn_dsa", HD), ("qn_mem", HD), ("kn_mem", HD))
WEIGHTS = ("ffn1_norm", "ffn1_w1", "ffn1_w3", "ffn1_w2", "mix_norm", "mem_norm", "w_in", "w_mem_kv", "qn_dsa", "kn_dsa",
           "qn_mem", "kn_mem", "w_branch_sb", "w_branch_dsa", "w_branch_mem", "w_gate", "b_gate", "w_out", "ffn2_norm",
           "ffn2_w1", "ffn2_w3", "ffn2_w2")
SMALL_ROWS = 8


def _dot(a, b, dn=NN):
    return lax.dot_general(a, b, dn, preferred_element_type=F32)


def _dot01(x, m01, pieces=3):
    hi = x.astype(BF16)
    r1 = x - hi.astype(F32)
    mid = r1.astype(BF16)
    if pieces == 2:
        return _dot(hi, m01) + _dot(mid, m01)
    lo = (r1 - mid.astype(F32)).astype(BF16)
    return _dot(hi, m01) + _dot(mid, m01) + _dot(lo, m01)


def _pick(n, cands):
    for c in cands:
        if n % c == 0:
            return c
    raise ValueError(f"no tile for {n}")


def _from_dilated(v, d, scr):
    w = v.shape[1] // d
    v = v.astype(F32)
    for c in range(d):
        for p, buf in enumerate(scr[:w // 128]):
            buf[pl.ds(c, v.shape[0], stride=d), :] = v[:, c * w + 128 * p:c * w + 128 * (p + 1)]
    return jnp.concatenate([buf[...] for buf in scr[:w // 128]], axis=1)


def _to_dilated(v, d, scr):
    w = v.shape[1]
    for p, buf in enumerate(scr[:w // 128]):
        buf[...] = v[:, 128 * p:128 * (p + 1)].astype(F32)
    return jnp.concatenate([buf[pl.ds(c, v.shape[0] // d, stride=d), :] for c in range(d) for buf in scr[:w // 128]], axis=1)


def _tokmap(name, fn, tok_ins, consts, tok_outs, acc_outs=(), tile=512, dil_ins=None, dil_outs=None, place=None):
    dil_ins, dil_outs, place = dil_ins or {}, dil_outs or {}, place or {}
    bufs = [(j, buf) for j, (_, _, buf) in place.items() if buf is not None]
    n_buf = len(bufs)
    n = tok_ins[0].shape[0] * dil_ins.get(0, 1)
    tile = _pick(n, [t for t in (512, 256, 128, 64, 32, 16, 8) if t <= tile])
    n_tin, n_in, n_tok, n_acc = len(tok_ins), len(tok_ins) + len(consts), len(tok_outs), len(acc_outs)
    n_scr = max([tok_ins[j].shape[1] // d // 128 for j, d in dil_ins.items() if d > 1]
                + [tok_outs[j][0] // 128 for j, d in dil_outs.items() if d > 1] + [0])

    def body(*refs):
        scr = refs[len(refs) - n_scr:]
        vals = [r[...] for r in refs[:n_in]]
        for j, d in dil_ins.items():
            if d > 1:
                vals[j] = _from_dilated(vals[j], d, scr)
        outs = fn(*vals)
        outs = list(outs) if isinstance(outs, (tuple, list)) else [outs]
        assert len(outs) == n_tok + n_acc, (name, len(outs))
        for j, d in dil_outs.items():
            if d > 1:
                outs[j] = _to_dilated(outs[j], d, scr)
        orefs = refs[n_in + n_buf:]
        for r, v in zip(orefs[:n_tok], outs[:n_tok]):
            r[...] = v.astype(r.dtype)
        if n_acc:
            @pl.when(pl.program_id(0) == 0)
            def _():
                for r in orefs[n_tok:n_tok + n_acc]:
                    r[...] = jnp.zeros(r.shape, r.dtype)
            for r, v in zip(orefs[n_tok:n_tok + n_acc], outs[n_tok:]):
                r[...] += v.astype(r.dtype)

    def tok_spec(width, d):
        return pl.BlockSpec((tile // d, d * width), lambda i: (i, 0))

    in_specs = [tok_spec(a.shape[1] // dil_ins.get(j, 1), dil_ins.get(j, 1)) for j, a in enumerate(tok_ins)]
    in_specs += [pl.BlockSpec(c.shape, lambda i: (0, 0)) for c in consts]
    in_specs += [HBM_SPEC] * n_buf
    out_specs = [tok_spec(w, dil_outs.get(j, 1)) for j, (w, _) in enumerate(tok_outs)]
    out_shape = [SDS((n // dil_outs.get(j, 1), w * dil_outs.get(j, 1)), dt) for j, (w, dt) in enumerate(tok_outs)]
    for j, (total, col_block, _) in place.items():
        out_specs[j] = pl.BlockSpec((tile, tok_outs[j][0]), lambda i, cb=col_block: (i, cb))
        out_shape[j] = SDS((n, total), tok_outs[j][1])
    out_specs += [pl.BlockSpec(s, lambda i: (0, 0)) for s in acc_outs]
    out_shape += [SDS(s, F32) for s in acc_outs]
    res = pl.pallas_call(
        body, name=name, grid=(n // tile,), in_specs=in_specs, out_specs=out_specs, out_shape=out_shape,
        scratch_shapes=[pltpu.VMEM((tile, 128), F32)] * n_scr,
        input_output_aliases={n_in + b: j for b, (j, _) in enumerate(bufs)},
        compiler_params=pltpu.CompilerParams(dimension_semantics=("arbitrary",), vmem_limit_bytes=VMEM_LIMIT),
    )(*tok_ins, *consts, *[buf for _, buf in bufs])
    return res


MATMUL_VMEM_BUDGET = 40 * 1024 * 1024


def _matmul_tiles(m, n, k, a_bytes, b_bytes, o_bytes, extra_bytes, whole_n=False):
    best = None
    for tk in [c for c in (3584, 2816, 2048, 1408, 1024, 512, 256, 128) if k % c == 0]:
        for tm in [c for c in (1408, 1024, 768, 512, 256, 128) if m % c == 0]:
            for tn in [n] if whole_n else [c for c in (1408, 1024, 768, 512, 256, 128) if n % c == 0]:
                need = 2 * tk * (tm * a_bytes + tn * b_bytes) + tm * tn * (2 * o_bytes + 2 * extra_bytes + 8)
                if need > MATMUL_VMEM_BUDGET:
                    continue
                score = (min(tm, 512) * min(tn, 512), tk, tm * tn, tn)
                if best is None or score > best[0]:
                    best = (score, (tm, tn, tk))
    return best[1]


class _Carry:
    def __init__(self, ins, outs, sems, copies, then):
        self.ins, self.outs, self.sems, self.copies, self.then = ins, outs, sems, copies, then


class _Plan:
    def __init__(self):
        self.pending = {}

    def put(self, host, carry):
        assert host not in self.pending, host
        self.pending[host] = carry

    def take(self, host):
        return self.pending.pop(host, None)


def _matmul(name, a, b, dn, out_dtype, epi=None, tiles=(), rows=(), plan=None, a_pro=None, n_sum=0):
    if dn == NN:
        (m, k), n = a.shape, b.shape[1]
    elif dn == NT:
        (m, k), n = a.shape, b.shape[0]
    else:
        (k, m), n = a.shape, b.shape[1]
    n_t, n_r = len(tiles), len(rows)
    pro, n_parts = a_pro if a_pro is not None else (None, 1)
    if pro is not None:
        assert dn == NN and n == _pick(n, (1024, 512))
        k //= n_parts
    if pro is not None:
        tm, tn, tk = _pick(m, (256, 128)), n, k
    else:
        tm, tn, tk = _matmul_tiles(m, n, k, a.dtype.itemsize, b.dtype.itemsize, jnp.dtype(out_dtype).itemsize,
                                   sum(t.dtype.itemsize for t in tiles), whole_n=n_sum > 0)
    nk = k // tk
    grid = (m // tm, n // tn, nk)
    assert pro is None or grid[1] == 1
    assert n_sum == 0 or grid[1] == 1
    carry = plan.take(name) if plan is not None else None
    n_ci, n_co = (len(carry.ins), len(carry.outs)) if carry else (0, 0)
    n_keep = 1 if pro is not None else 0

    def body(*refs):
        a_refs, b_ref, rest = refs[:n_parts], refs[n_parts], refs[n_parts + 1:]
        extras, rest = rest[:n_t + n_r], rest[n_t + n_r:]
        c_in, o_ref, rest = rest[:n_ci], rest[n_ci], rest[n_ci + 1:]
        sums, rest = rest[:n_sum], rest[n_sum:]
        keep, c_out, scratch = rest[:n_keep], rest[n_keep:n_keep + n_co], rest[n_keep + n_co:]
        ids = [pl.program_id(d) for d in range(3)]
        if n_sum:
            @pl.when((ids[0] == 0) & (ids[2] == 0))
            def _():
                for r in sums:
                    r[...] = jnp.zeros(r.shape, F32)
        if carry:
            sems = scratch[1:] if nk > 1 else scratch

            @pl.when((ids[0] == 0) & (ids[1] == 0) & (ids[2] == 0))
            def _():
                for cp in carry.copies(c_in, c_out, *sems):
                    cp.start()

        if pro is not None:
            av = pro(*[r[...] for r in a_refs])
            keep[0][...] = av
        else:
            av = a_refs[0][...].astype(BF16)
        part = _dot(av, b_ref[...].astype(BF16), dn)

        def finish(r):
            if epi is not None:
                r = epi(r, *[e[...] for e in extras])
            if n_sum:
                for ref, v in zip(sums, r[1:]):
                    ref[...] += v
                r = r[0]
            o_ref[...] = r.astype(o_ref.dtype)

        if nk == 1:
            finish(part)
        else:
            acc = scratch[0]

            @pl.when(ids[2] == 0)
            def _():
                acc[...] = part

            @pl.when(ids[2] > 0)
            def _():
                acc[...] += part

            @pl.when(ids[2] == nk - 1)
            def _():
                finish(acc[...])

        if carry:
            @pl.when((ids[0] == grid[0] - 1) & (ids[1] == grid[1] - 1) & (ids[2] == nk - 1))
            def _():
                for cp in carry.copies(c_in, c_out, *sems):
                    cp.wait()

    if dn == TN:
        a_specs = [pl.BlockSpec((tk, tm), lambda i, j, kk: (kk, i))]
    else:
        a_specs = [pl.BlockSpec((tm, tk), lambda i, j, kk, p=p: (i, kk + p * nk)) for p in range(n_parts)]
    b_spec = pl.BlockSpec((tn, tk), lambda i, j, kk: (j, kk)) if dn == NT else pl.BlockSpec((tk, tn), lambda i, j, kk: (kk, j))
    in_specs = a_specs + [b_spec] + [pl.BlockSpec((tm, tn), lambda i, j, kk: (i, j)) for _ in tiles]
    in_specs += [pl.BlockSpec((1, tn), lambda i, j, kk: (0, j)) for _ in rows] + [HBM_SPEC] * n_ci
    res = pl.pallas_call(
        body, name=name, grid=grid, in_specs=in_specs,
        out_specs=[pl.BlockSpec((tm, tn), lambda i, j, kk: (i, j))] + [pl.BlockSpec((1, tn), lambda i, j, kk: (0, 0))] * n_sum
        + [pl.BlockSpec((tm, tk), lambda i, j, kk: (i, kk))] * n_keep + [HBM_SPEC] * n_co,
        out_shape=[SDS((m, n), out_dtype)] + [SDS((1, n), F32)] * n_sum + [SDS((m, k), BF16)] * n_keep
        + (list(carry.outs) if carry else []),
        scratch_shapes=([pltpu.VMEM((tm, tn), F32)] if nk > 1 else []) + (list(carry.sems) if carry else []),
        compiler_params=pltpu.CompilerParams(
            dimension_semantics=("arbitrary",) * 3 if (carry or n_sum) else ("parallel", "parallel", "arbitrary"),
            vmem_limit_bytes=VMEM_LIMIT),
    )(*[a] * n_parts, b, *tiles, *rows, *(carry.ins if carry else []))
    if carry:
        carry.then(res[1 + n_sum + n_keep:])
    return tuple(res[:1 + n_sum + n_keep]) if n_sum + n_keep else res[0]


def _mean_all(v):
    return jnp.mean(v, axis=-1, keepdims=True)


def _head_sums(v, bd):
    w = bd.shape[0]
    return jnp.concatenate([_dot01(v[:, j:j + w], bd, 2) for j in range(0, v.shape[1], w)], axis=1)


def _mean_heads(bd):
    return lambda v: _head_sums(v, bd) * (1.0 / HD)


def _rms_fwd(x, g, mean):
    return x * lax.rsqrt(mean(x * x) + EPS) * g


def _rms_bwd(x, g, dy, mean):
    r = lax.rsqrt(mean(x * x) + EPS)
    dn = dy * g
    dx = r * dn - x * (r * r * r) * mean(dn * x)
    return dx, jnp.sum(dy * x * r, axis=0, keepdims=True)


def _swap_halves(x):
    w = x.shape[1]
    lane = lax.broadcasted_iota(jnp.int32, x.shape, 1)
    return jnp.where(lane % HD < HD // 2, pltpu.roll(x, w - HD // 2, 1), pltpu.roll(x, HD // 2, 1))


def _lanes(t, w):
    return jnp.tile(t, (1, w // t.shape[1]))


def _rope_fwd(x, cos, sin_signed):
    return x * _lanes(cos, x.shape[1]) + _swap_halves(x) * _lanes(sin_signed, x.shape[1])


def _rope_bwd(dy, cos, sin_signed):
    return dy * _lanes(cos, dy.shape[1]) + _swap_halves(dy * _lanes(sin_signed, dy.shape[1]))


def _bcast_heads(cols):
    return jnp.concatenate([jnp.broadcast_to(c, (c.shape[0], HD)) for c in cols], axis=1)


def _softplus(z):
    return jnp.maximum(z, 0.0) + jnp.log(1.0 + jnp.exp(-jnp.abs(z)))


def _block_diag(w):
    h = np.arange(w) // HD
    return jnp.asarray(h[:, None] == h[None, :], BF16)


def _sb_window(i, t):
    hi = (i + 1) * SB_QB - t * SB_WIN
    lo = hi - SB_WIN
    ws = pl.multiple_of(jnp.maximum(lo, 0), SB_QB)
    kpos = ws + lax.broadcasted_iota(jnp.int32, (SB_QB, SB_WIN), 1)
    qpos = i * SB_QB + lax.broadcasted_iota(jnp.int32, (SB_QB, SB_WIN), 0)
    return (kpos < qpos) & (kpos >= lo) & (kpos < hi), ws


def _sb_fwd(qkv):
    s = qkv.shape[0]
    assert s >= SB_WIN
    nq = s // SB_QB
    nh = SB_FWD_HEADS
    bw = HD * nh
    ngroups = SB_W // bw

    def body(q_ref, k_ref, v_ref, later_ref, o_ref, tot_ref, w0_ref, b0_ref, nb_ref):
        p, i = pl.program_id(0), pl.program_id(1)
        q = q_ref[...]
        later_of = later_ref[...]

        def window(t, tots, outs, keep):
            mask, ws = _sb_window(i, t)
            kw, vw = k_ref[pl.ds(ws, SB_WIN), :], v_ref[pl.ds(ws, SB_WIN), :]
            new_t, new_o, w_all, b_all = [], [], [], []
            for hh in range(nh):
                sl = slice(HD * hh, HD * hh + HD)
                z = _dot(q[:, sl], kw[:, sl], NT) * SCALE
                sp = _softplus(z)
                lf = jnp.where(mask, -sp, 0.0)
                later = tots[hh] + _dot01(lf, later_of, 2)
                w = jnp.where(mask, jnp.exp(z - sp + later), 0.0).astype(BF16)
                new_o.append(outs[hh] + _dot(w, vw[:, sl]))
                new_t.append(tots[hh] + jnp.sum(lf, axis=1, keepdims=True))
                if keep:
                    w_all.append(w)
                    b_all.append(jnp.where(mask, jnp.exp(z - sp), 0.0).astype(BF16))
            if keep:
                w0_ref[...] = jnp.concatenate(w_all, axis=1)
                b0_ref[...] = jnp.concatenate(b_all, axis=1)
            alive = functools.reduce(jnp.maximum, [jnp.max(v) for v in new_t])
            return t + 1, alive, tuple(new_t), tuple(new_o)

        zt, zo = jnp.zeros((SB_QB, 1), F32), jnp.zeros((SB_QB, HD), F32)
        first = window(jnp.int32(0), (zt,) * nh, (zo,) * nh, True)
        t, _, tots, outs = lax.while_loop(lambda c: ((i + 1) * SB_QB - c[0] * SB_WIN > 0) & (c[1] > EXHAUSTED),
                                          lambda c: window(c[0], c[2], c[3], False), first)
        o_ref[...] = jnp.concatenate(outs, axis=1).astype(o_ref.dtype)
        tot_ref[...] = _bcast_heads(tots)
        nb_ref[p, i] = t

    whole = lambda off: pl.BlockSpec((s, bw), lambda p, i: (0, off + p), pipeline_mode=pl.Buffered(1))
    tile = pl.BlockSpec((SB_QB, bw), lambda p, i: (i, p))
    tri = pl.BlockSpec((SB_WIN, SB_WIN), lambda p, i: (0, 0), pipeline_mode=pl.Buffered(1))
    near = pl.BlockSpec((SB_QB, nh * SB_WIN), lambda p, i: (i, p))
    n_heads = SB_W // HD
    idx = np.arange(SB_WIN)
    return pl.pallas_call(
        body, name="sb_fwd", grid=(ngroups, nq),
        in_specs=[tile, whole(ngroups), whole(2 * ngroups), tri],
        out_specs=[tile, tile, near, near, pl.BlockSpec(memory_space=pltpu.SMEM)],
        out_shape=[SDS((s, SB_W), BF16), SDS((s, SB_W), F32), SDS((s, n_heads * SB_WIN), BF16), SDS((s, n_heads * SB_WIN), BF16),
                   SDS((ngroups, nq), jnp.int32)],
        compiler_params=pltpu.CompilerParams(dimension_semantics=("arbitrary", "arbitrary"), vmem_limit_bytes=VMEM_LIMIT),
    )(qkv, qkv, qkv, jnp.asarray(idx[:, None] > idx[None, :], BF16))


def _sb_bwd(qkv, do, tot, nblk, w0, b0, buf, col):
    s = qkv.shape[0]
    nq = s // SB_QB
    npairs = SB_W // 128

    def body(nb_ref, q_ref, k_ref, v_ref, do_ref, tot_ref, upto_ref, before_ref, w0_ref, b0_ref, buf_ref,
             dq_ref, dk_ref, dv_ref):
        p, i = pl.program_id(0), pl.program_id(1)

        @pl.when(i == 0)
        def _():
            dk_ref[...] = jnp.zeros(dk_ref.shape, F32)
            dv_ref[...] = jnp.zeros(dv_ref.shape, F32)

        upto = upto_ref[...]
        before = before_ref[...]
        q, dout, tt = q_ref[...], do_ref[...], tot_ref[...]
        n = nb_ref[p * 2 // SB_FWD_HEADS, i]

        def step(it, c):
            pres, gpres, dqs = c
            mask, ws = _sb_window(i, n - 1 - it)
            kw, vw = k_ref[pl.ds(ws, SB_WIN), :], v_ref[pl.ds(ws, SB_WIN), :]
            new_p, new_g, new_dq, dks, dvs = [], [], [], [], []
            for hh in range(2):
                sl = slice(HD * hh, HD * hh + HD)
                z = _dot(q[:, sl], kw[:, sl], NT) * SCALE
                sp = _softplus(z)
                lf = jnp.where(mask, -sp, 0.0)
                later = tt[:, HD * hh:HD * hh + 1] - (pres[hh] + _dot01(lf, upto, 2))
                w = jnp.where(mask, jnp.exp(z - sp + later), 0.0)
                beta = jnp.exp(z - sp)
                g = _dot(dout[:, sl], vw[:, sl], NT) * w
                g_far = gpres[hh] + _dot(g.astype(BF16), before)
                dz = (jnp.where(mask, g * (1.0 - beta) - beta * g_far, 0.0) * SCALE).astype(BF16)
                new_dq.append(dqs[hh] + _dot(dz, kw[:, sl]))
                dks.append(_dot(dz, q[:, sl], TN))
                dvs.append(_dot(w.astype(BF16), dout[:, sl], TN))
                new_p.append(pres[hh] + jnp.sum(lf, axis=1, keepdims=True))
                new_g.append(gpres[hh] + jnp.sum(g, axis=1, keepdims=True))
            dk_ref[pl.ds(ws, SB_WIN), :] += jnp.concatenate(dks, axis=1)
            dv_ref[pl.ds(ws, SB_WIN), :] += jnp.concatenate(dvs, axis=1)
            return tuple(new_p), tuple(new_g), tuple(new_dq)

        zt, zo = jnp.zeros((SB_QB, 1), F32), jnp.zeros((SB_QB, HD), F32)
        _, gpres, dqs = lax.fori_loop(0, n - 1, step, ((zt, zt), (zt, zt), (zo, zo)))
        _, ws = _sb_window(i, 0)
        kw, vw = k_ref[pl.ds(ws, SB_WIN), :], v_ref[pl.ds(ws, SB_WIN), :]
        dqs, dks, dvs = list(dqs), [], []
        for hh in range(2):
            sl = slice(HD * hh, HD * hh + HD)
            w = w0_ref[:, SB_WIN * hh:SB_WIN * (hh + 1)]
            beta = b0_ref[:, SB_WIN * hh:SB_WIN * (hh + 1)].astype(F32)
            g = _dot(dout[:, sl], vw[:, sl], NT) * w.astype(F32)
            g_far = gpres[hh] + _dot(g.astype(BF16), before)
            dz = ((g * (1.0 - beta) - beta * g_far) * SCALE).astype(BF16)
            dqs[hh] = dqs[hh] + _dot(dz, kw[:, sl])
            dks.append(_dot(dz, q[:, sl], TN))
            dvs.append(_dot(w, dout[:, sl], TN))
        dk_ref[pl.ds(ws, SB_WIN), :] += jnp.concatenate(dks, axis=1)
        dv_ref[pl.ds(ws, SB_WIN), :] += jnp.concatenate(dvs, axis=1)
        dq_ref[...] = jnp.concatenate(dqs, axis=1).astype(dq_ref.dtype)

    whole_in = lambda off: pl.BlockSpec((s, 128), lambda p, i: (0, off + p), pipeline_mode=pl.Buffered(1))
    whole_out = pl.BlockSpec((s, 128), lambda p, i: (0, p), pipeline_mode=pl.Buffered(1))
    tile = pl.BlockSpec((SB_QB, 128), lambda p, i: (i, p))
    near = pl.BlockSpec((SB_QB, 2 * SB_WIN), lambda p, i: (i, p))
    dq_tile = pl.BlockSpec((SB_QB, 128), lambda p, i: (i, col // 128 + p))
    tri = pl.BlockSpec((SB_WIN, SB_WIN), lambda p, i: (0, 0), pipeline_mode=pl.Buffered(1))
    idx = np.arange(SB_WIN)
    return pl.pallas_call(
        body, name="sb_bwd", grid=(npairs, nq),
        in_specs=[pl.BlockSpec(memory_space=pltpu.SMEM), tile, whole_in(npairs), whole_in(2 * npairs), tile, tile, tri, tri,
                  near, near, HBM_SPEC],
        out_specs=[dq_tile, whole_out, whole_out],
        out_shape=[SDS(buf.shape, buf.dtype)] + [SDS((s, SB_W), F32)] * 2,
        input_output_aliases={10: 0},
        compiler_params=pltpu.CompilerParams(dimension_semantics=("arbitrary", "arbitrary"), vmem_limit_bytes=VMEM_LIMIT),
    )(nblk, qkv, qkv, qkv, do, tot, jnp.asarray(idx[:, None] <= idx[None, :], BF16), jnp.asarray(idx[:, None] < idx[None, :], BF16),
      w0, b0, buf)


def _dsa_mask(DSA_T, has_prev):
    r = lax.broadcasted_iota(jnp.int32, (DSA_T, QB + DSA_T), 0)
    j = lax.broadcasted_iota(jnp.int32, (DSA_T, QB + DSA_T), 1) - QB
    return (j <= r) & (j >= r - QB) & ((j >= 0) | has_prev)


def _dsa_fwd(q, k, v, dil):
    n = q.shape[0]
    DSA_T = DSA_T_FWD
    nt = n // DSA_T

    def body(q_ref, kc_ref, kp_ref, vc_ref, vp_ref, o_ref, lse_ref):
        mask = _dsa_mask(DSA_T, pl.program_id(1) > 0)
        outs, lses = [], []
        for hh in range(DSA_OUT_W // HD):
            sl = slice(HD * hh, HD * hh + HD)
            kcat = jnp.concatenate([kp_ref[:, sl], kc_ref[:, sl]], axis=0)
            vcat = jnp.concatenate([vp_ref[:, sl], vc_ref[:, sl]], axis=0)
            sc = jnp.where(mask, _dot(q_ref[:, sl], kcat, NT) * SCALE, NEG)
            m = jnp.max(sc, axis=1, keepdims=True)
            p = jnp.exp(sc - m)
            den = jnp.sum(p, axis=1, keepdims=True)
            outs.append(_dot(p.astype(BF16), vcat) / den)
            lses.append(m + jnp.log(den))
        o_ref[...] = jnp.concatenate(outs, axis=1)
        lse_ref[...] = _bcast_heads(lses)

    cur = pl.BlockSpec((DSA_T, DSA_OUT_W), lambda c, i: (i, c))
    prev = pl.BlockSpec((QB, DSA_OUT_W), lambda c, i: (jnp.maximum(i * (DSA_T // QB) - 1, 0), c))
    o, lse = pl.pallas_call(
        body, name=f"dsa_fwd_d{dil}", grid=(dil, nt), in_specs=[cur, cur, prev, cur, prev], out_specs=[cur, cur],
        out_shape=[SDS((n, dil * DSA_OUT_W), F32)] * 2,
        compiler_params=pltpu.CompilerParams(dimension_semantics=("parallel", "parallel")),
    )(q, k, k, v, v)
    return o, lse


def _dsa_bwd(q, k, v, do, cc, lse, dil):
    n = q.shape[0]
    DSA_T = DSA_T_BWD
    nt = n // DSA_T
    per = DSA_T // QB

    def body(qj_ref, qn_ref, kp_ref, kj_ref, vp_ref, vj_ref, doj_ref, don_ref, cj_ref, cn_ref, lj_ref, ln_ref,
             dq_ref, dk_ref, dv_ref):
        j = pl.program_id(1)
        mask = _dsa_mask(DSA_T, j > 0)
        r = lax.broadcasted_iota(jnp.int32, (QB, DSA_T), 0)
        kk = lax.broadcasted_iota(jnp.int32, (QB, DSA_T), 1)
        m_next = (kk >= r + QB) & (j + 1 < nt)
        dqs, dks, dvs = [], [], []
        for hh in range(DSA_OUT_W // HD):
            sl = slice(HD * hh, HD * hh + HD)
            one = slice(HD * hh, HD * hh + 1)
            qj, qn, kj, vj, doj, don = (t[:, sl] for t in (qj_ref, qn_ref, kj_ref, vj_ref, doj_ref, don_ref))
            kcat = jnp.concatenate([kp_ref[:, sl], kj], axis=0)
            vcat = jnp.concatenate([vp_ref[:, sl], vj], axis=0)
            p1 = jnp.where(mask, jnp.exp(_dot(qj, kcat, NT) * SCALE - lj_ref[:, one]), 0.0)
            ds1 = (p1 * (_dot(doj, vcat, NT) + cj_ref[:, one]) * SCALE).astype(BF16)
            p2 = jnp.where(m_next, jnp.exp(_dot(qn, kj, NT) * SCALE - ln_ref[:, one]), 0.0)
            ds2 = (p2 * (_dot(don, vj, NT) + cn_ref[:, one]) * SCALE).astype(BF16)
            dqs.append(_dot(ds1, kcat))
            dks.append(_dot(ds1[:, QB:], qj, TN) + _dot(ds2, qn, TN))
            dvs.append(_dot(p1[:, QB:].astype(BF16), doj, TN) + _dot(p2.astype(BF16), don, TN))
        dq_ref[...] = jnp.concatenate(dqs, axis=1).astype(dq_ref.dtype)
        dk_ref[...] = jnp.concatenate(dks, axis=1).astype(dk_ref.dtype)
        dv_ref[...] = jnp.concatenate(dvs, axis=1).astype(dv_ref.dtype)

    cur = pl.BlockSpec((DSA_T, DSA_OUT_W), lambda c, j: (j, c))
    prev = pl.BlockSpec((QB, DSA_OUT_W), lambda c, j: (jnp.maximum(j * per - 1, 0), c))
    nxt = pl.BlockSpec((QB, DSA_OUT_W), lambda c, j: (jnp.minimum((j + 1) * per, n // QB - 1), c))
    dq, dk, dv = pl.pallas_call(
        body, name=f"dsa_bwd_d{dil}", grid=(dil, nt),
        in_specs=[cur, nxt, prev, cur, prev, cur, cur, nxt, cur, nxt, cur, nxt], out_specs=[cur, cur, cur],
        out_shape=[SDS((n, dil * DSA_OUT_W), BF16)] * 3,
        compiler_params=pltpu.CompilerParams(dimension_semantics=("parallel", "parallel")),
    )(q, q, k, k, v, v, do, do, cc, cc, lse, lse)
    return dq, dk, dv


def _norm_bwd_epi(acc, xv, dyv, g):
    dx, dg = _rms_bwd(xv, g, acc, _mean_all)
    return dx + dyv, dg


def _ffn_fwd(tag, x, gain, w13, w2, plan=None, target=None):
    n = _tokmap(f"{tag}_norm", lambda xv, g: _rms_fwd(xv, g, _mean_all), [x], [gain], [(D, BF16)])[0]
    ab = _matmul(f"{tag}_up", n, w13, NN, BF16, plan=plan)

    def gate(av, bv):
        a, b = av.astype(F32), bv.astype(F32)
        return (a * jax.nn.sigmoid(a) * b).astype(BF16)

    if target is None:
        y, h = _matmul(f"{tag}_down", ab, w2(), NN, F32, epi=lambda acc, res: res + 0.5 * acc, tiles=[x], plan=plan,
                       a_pro=(gate, 2))
        return y, (n, ab, h)

    def loss_epi(acc, res, tv):
        e = res + 0.5 * acc - tv
        return e * (1.0 / D), (0.5 / D) * jnp.sum(e * e, axis=0, keepdims=True)

    dy, loss_row, h = _matmul(f"{tag}_down", ab, w2(), NN, F32, epi=loss_epi, tiles=[x, target], plan=plan, a_pro=(gate, 2),
                              n_sum=1)
    return (dy, loss_row), (n, ab, h)


def _ffn_bwd(tag, x, gain, w13, w2, saved, dy, plan=None, on_dw=None):
    n, ab, h = saved
    dh = _matmul(f"{tag}_bwd_dh", dy, w2, NT, BF16, epi=lambda acc: 0.5 * acc)

    def gate_bwd(abv, dhv):
        a, b, dhf = abv[:, :D_FF].astype(F32), abv[:, D_FF:].astype(F32), dhv.astype(F32)
        sg = jax.nn.sigmoid(a)
        da = dhf * b * (sg * (1.0 + a * (1.0 - sg)))
        return jnp.concatenate([da, dhf * (a * sg)], axis=1)

    dab = _tokmap(f"{tag}_bwd_gate", gate_bwd, [ab, dh], [], [(2 * D_FF, BF16)], tile=256)[0]
    dw2 = _matmul(f"{tag}_bwd_dw2", h, dy, TN, F32, epi=lambda acc: 0.5 * acc)
    dw13 = _matmul(f"{tag}_bwd_dw13", n, dab, TN, F32, plan=plan)
    if on_dw is not None:
        on_dw(dw13, dw2)
    dx, dgain = _matmul(f"{tag}_bwd_dn", dab, w13, NT, F32, epi=_norm_bwd_epi, tiles=[x, dy], rows=[gain], plan=plan, n_sum=1)
    return dx, dgain, dw13, dw2


def _rope_tables(s):
    half = HD // 2
    inv_freq = jnp.power(10000.0, -jnp.arange(half, dtype=F32) / half)
    ang = jnp.arange(s).astype(F32)[:, None] * inv_freq[None, :]
    cos, sin = jnp.cos(ang), jnp.sin(ang)
    return jnp.tile(jnp.concatenate([cos, cos], axis=1), (1, 2)), jnp.tile(jnp.concatenate([-sin, sin], axis=1), (1, 2))


def _local_step(x, mem, tgt, w, sm, plan=None, on_grads=None):
    s = x.shape[0]
    assert s % (max(DSA_T_FWD, DSA_T_BWD) * max(DSA_DILS)) == 0
    on_grads = on_grads or (lambda group, grads: None)
    c_sb, c_dsa, c_qm, c_all = 3 * D, 3 * D + 3 * SB_W, 3 * D + 3 * SB_W + 3 * DSA_W, 3 * D + 4096
    cos, sin = _rope_tables(s)
    bd768 = bd256 = _block_diag(128)
    gq_dsa, gk_dsa = jnp.tile(sm["qn_dsa"], (1, DSA_W // HD)), jnp.tile(sm["kn_dsa"], (1, DSA_W // HD))
    gq_mem, gk_mem = jnp.tile(sm["qn_mem"], (1, MEM_W // HD)), jnp.tile(sm["kn_mem"], (1, MEM_W // HD))

    w13_1 = jnp.concatenate([w["ffn1_w1"], w["ffn1_w3"]], axis=1)
    x1, ffn1_saved = _ffn_fwd("ffn1", x, sm["ffn1_norm"], w13_1, lambda: w["ffn1_w2"], plan)
    w_all = jnp.concatenate([w["w_gate"], w["w_in"]], axis=1)
    wb_sb, wb_dsa, wb_mem = w["w_branch_sb"], w["w_branch_dsa"], w["w_branch_mem"]
    hmix = _tokmap("mix_norm", lambda xv, g: _rms_fwd(xv, g, _mean_all), [x1], [sm["mix_norm"]], [(D, BF16)])[0]
    qkv_sb = _matmul("proj_sb", hmix, w_all[:, c_sb:c_dsa], NN, BF16)
    qkv_dsa = _matmul("proj_dsa", hmix, w_all[:, c_dsa:c_qm], NN, BF16, plan=plan)
    q_mem = _matmul("proj_qmem", hmix, w_all[:, c_qm:], NN, BF16)
    gpre = _matmul("proj_gate", hmix, w_all[:, :c_sb], NN, BF16, epi=lambda acc, b: acc + b, rows=[sm["b_gate"]], plan=plan)

    o_sb, sb_tot, sb_w0, sb_b0, sb_nblk = _sb_fwd(qkv_sb)

    def dsa_prep(qkv, cs, sn, gq, gk, bd):
        mean = _mean_heads(bd)
        qn = _rope_fwd(_rms_fwd(qkv[:, :DSA_W].astype(F32), gq, mean), cs, sn)
        kn = _rope_fwd(_rms_fwd(qkv[:, DSA_W:2 * DSA_W].astype(F32), gk, mean), cs, sn)
        v = qkv[:, 2 * DSA_W:]
        outs = []
        for t in (qn, kn, v):
            outs += [t[:, DSA_OUT_W * g:DSA_OUT_W * (g + 1)] for g in range(3)]
        return outs

    dsa_in = _tokmap("dsa_prep", dsa_prep, [qkv_dsa, cos, sin], [gq_dsa, gk_dsa, bd768], [(DSA_OUT_W, BF16)] * 9, tile=256,
                     dil_outs={j: DSA_DILS[j % 3] for j in range(9)})
    dsa_q, dsa_k, dsa_v = dsa_in[0:3], dsa_in[3:6], dsa_in[6:9]
    dsa_o, dsa_lse = zip(*[_dsa_fwd(dsa_q[g], dsa_k[g], dsa_v[g], DSA_DILS[g]) for g in range(3)])

    def alphas(l0, l1, l2):
        m = jnp.maximum(jnp.maximum(l0, l1), l2)
        e = [jnp.exp(l - m) for l in (l0, l1, l2)]
        tot = e[0] + e[1] + e[2]
        return [t / tot for t in e]

    def dsa_mix(o0, o1, o2, l0, l1, l2):
        a = alphas(l0, l1, l2)
        return a[0] * o0 + a[1] * o1 + a[2] * o2

    o_dsa = _tokmap("dsa_mix", dsa_mix, [*dsa_o, *dsa_lse], [], [(DSA_OUT_W, BF16)], tile=256,
                    dil_ins={j: DSA_DILS[j % 3] for j in range(6)})[0]

    def mem_kv(memv, g, wkv, gk, bd):
        kv = _dot(_rms_fwd(memv, g, _mean_all).astype(BF16), wkv)
        return _rms_fwd(kv[:, :MEM_W], gk, _mean_heads(bd)), kv[:, MEM_W:]

    km, vm = _tokmap("mem_kv", mem_kv, [mem], [sm["mem_norm"], w["w_mem_kv"], gk_mem, bd256], [(MEM_W, BF16)] * 2)

    def mem_probs(qv, kmv, gq, bd):
        qn = _rms_fwd(qv.astype(F32), gq, _mean_heads(bd)).astype(BF16)
        ps = []
        for h in range(MEM_W // HD):
            sl = slice(HD * h, HD * h + HD)
            sc = _dot(qn[:, sl], kmv[:, sl], NT) * SCALE
            e = jnp.exp(sc - jnp.max(sc, axis=1, keepdims=True))
            ps.append(e / jnp.sum(e, axis=1, keepdims=True))
        return qn, ps

    def mem_attn(qv, kmv, vmv, gq, bd):
        _, ps = mem_probs(qv, kmv, gq, bd)
        return jnp.concatenate([_dot(p.astype(BF16), vmv[:, HD * h:HD * h + HD]) for h, p in enumerate(ps)], axis=1)

    o_mem = _tokmap("mem_attn", mem_attn, [q_mem], [km, vm, gq_mem, bd256], [(MEM_W, BF16)])[0]

    def merge(osb, odsa, omem, gp, w_sb, w_dsa, w_mem):
        gates = jax.nn.sigmoid(gp.astype(F32))
        ys = (_dot(osb, w_sb), _dot(odsa, w_dsa), _dot(omem, w_mem))
        return gates, ys, gates[:, :D] * ys[0] + gates[:, D:2 * D] * ys[1] + gates[:, 2 * D:] * ys[2]

    merged = _tokmap("merge", lambda *a: merge(*a)[2], [o_sb, o_dsa, o_mem, gpre], [wb_sb, wb_dsa, wb_mem], [(D, BF16)],
                     tile=256)[0]
    x2 = _matmul("out_proj", merged, w["w_out"], NN, F32, epi=lambda acc, res: res + acc, tiles=[x1])
    w13_2 = jnp.concatenate([w["ffn2_w1"], w["ffn2_w3"]], axis=1)
    (dy, loss_row), ffn2_saved = _ffn_fwd("ffn2", x2, sm["ffn2_norm"], w13_2, lambda: w["ffn2_w2"], target=tgt)
    loss = jnp.sum(loss_row).reshape(1, 1)

    gw, gs = {}, {}
    def ffn_grads(tag):
        def on_dw(dw13, dw2):
            gw[f"{tag}_w1"], gw[f"{tag}_w3"], gw[f"{tag}_w2"] = dw13[:, :D_FF], dw13[:, D_FF:], dw2
            on_grads(tag, {n: gw[n] for n in (f"{tag}_w1", f"{tag}_w3", f"{tag}_w2")})
        return on_dw

    dx2, gs["ffn2_norm"], _, _ = _ffn_bwd("ffn2", x2, sm["ffn2_norm"], w13_2, w["ffn2_w2"], ffn2_saved, dy, plan,
                                          ffn_grads("ffn2"))
    dmerged = _matmul("out_proj_bwd_dx", dx2, w["w_out"], NT, BF16)
    gw["w_out"] = _matmul("out_proj_bwd_dw", merged, dx2, TN, F32)

    def merge_bwd(osb, odsa, omem, gp, dm, w_sb, w_dsa, w_mem):
        gates, ys, _ = merge(osb, odsa, omem, gp, w_sb, w_dsa, w_mem)
        dmf = dm.astype(F32)
        dgp, dos, dws = [], [], []
        for b, (ov, wv) in enumerate(((osb, w_sb), (odsa, w_dsa), (omem, w_mem))):
            gb = gates[:, D * b:D * (b + 1)]
            dgp.append(dmf * ys[b] * gb * (1.0 - gb))
            dyb = (dmf * gb).astype(BF16)
            dos.append(_dot(dyb, wv, NT))
            dws.append(_dot(ov, dyb, TN))
        dgp = jnp.concatenate(dgp, axis=1)
        return dos[0], dos[1], dos[2], dgp, dws[0], dws[1], dws[2], jnp.sum(dgp, axis=0, keepdims=True)

    do_sb, do_dsa, do_mem, dgpre, gw["w_branch_sb"], gw["w_branch_dsa"], gw["w_branch_mem"], gs["b_gate"] = _tokmap(
        "merge_bwd", merge_bwd, [o_sb, o_dsa, o_mem, gpre, dmerged], [wb_sb, wb_dsa, wb_mem],
        [(SB_W, BF16), (DSA_OUT_W, F32), (MEM_W, BF16), (3 * D, BF16)],
        [(SB_W, D), (DSA_OUT_W, D), (MEM_W, D), (1, 3 * D)], tile=256, place={3: (c_all, 0, None)})

    dall, dk_sb, dv_sb = _sb_bwd(qkv_sb, do_sb, sb_tot, sb_nblk, sb_w0, sb_b0, dgpre, c_sb)
    dall = lax.dynamic_update_slice(dall, dk_sb.astype(BF16), (0, c_sb + SB_W))
    dall = lax.dynamic_update_slice(dall, dv_sb.astype(BF16), (0, c_sb + 2 * SB_W))

    def dsa_mix_bwd(o0, o1, o2, l0, l1, l2, dov, bd):
        a = alphas(l0, l1, l2)
        omix = a[0] * o0 + a[1] * o1 + a[2] * o2
        dot_o = _head_sums(dov * omix, bd)
        return [dov * t for t in a] + [-t * dot_o for t in a]

    mixb = _tokmap("dsa_mix_bwd", dsa_mix_bwd, [*dsa_o, *dsa_lse, do_dsa], [bd256],
                   [(DSA_OUT_W, BF16)] * 3 + [(DSA_OUT_W, F32)] * 3, tile=256,
                   dil_ins={j: DSA_DILS[j % 3] for j in range(6)}, dil_outs={j: DSA_DILS[j % 3] for j in range(6)})
    dsa_d = [_dsa_bwd(dsa_q[g], dsa_k[g], dsa_v[g], mixb[g], mixb[3 + g], dsa_lse[g], DSA_DILS[g]) for g in range(3)]

    def dsa_prep_bwd(qkv, cs, sn, *rest):
        dqs, dks, dvs, (gq, gk, bd) = rest[0:3], rest[3:6], rest[6:9], rest[9:]
        mean = _mean_heads(bd)
        dq, dgq = _rms_bwd(qkv[:, :DSA_W].astype(F32), gq, _rope_bwd(jnp.concatenate(dqs, axis=1), cs, sn), mean)
        dk, dgk = _rms_bwd(qkv[:, DSA_W:2 * DSA_W].astype(F32), gk, _rope_bwd(jnp.concatenate(dks, axis=1), cs, sn), mean)
        return jnp.concatenate([dq, dk] + list(dvs), axis=1), dgq, dgk

    dall, dgq_dsa, dgk_dsa = _tokmap(
        "dsa_prep_bwd", dsa_prep_bwd,
        [qkv_dsa, cos, sin] + [dsa_d[g][t] for t in range(3) for g in range(3)], [gq_dsa, gk_dsa, bd768],
        [(3 * DSA_W, BF16)], [(1, DSA_W), (1, DSA_W)], tile=256, dil_ins={3 + j: DSA_DILS[j % 3] for j in range(9)},
        place={0: (c_all, c_dsa // (3 * DSA_W), dall)})
    gs["qn_dsa"] = dgq_dsa.reshape(DSA_W // HD, HD).sum(axis=0, keepdims=True)
    gs["kn_dsa"] = dgk_dsa.reshape(DSA_W // HD, HD).sum(axis=0, keepdims=True)

    def mem_attn_bwd(qv, dov, kmv, vmv, gq, bd):
        qn, ps = mem_probs(qv, kmv, gq, bd)
        dqn, dkm, dvm = [], [], []
        for h, p in enumerate(ps):
            sl = slice(HD * h, HD * h + HD)
            dp = _dot(dov[:, sl], vmv[:, sl], NT)
            ds = (p * (dp - jnp.sum(p * dp, axis=1, keepdims=True)) * SCALE).astype(BF16)
            dqn.append(_dot(ds, kmv[:, sl]))
            dkm.append(_dot(ds, qn[:, sl], TN))
            dvm.append(_dot(p.astype(BF16), dov[:, sl], TN))
        dq, dgq = _rms_bwd(qv.astype(F32), gq, jnp.concatenate(dqn, axis=1), _mean_heads(bd))
        return dq, jnp.concatenate(dkm, axis=1), jnp.concatenate(dvm, axis=1), dgq

    dall, dkm, dvm, dgq_mem = _tokmap("mem_attn_bwd", mem_attn_bwd, [q_mem, do_mem], [km, vm, gq_mem, bd256],
                                      [(MEM_W, BF16)], [(MEM_LEN, MEM_W), (MEM_LEN, MEM_W), (1, MEM_W)],
                                      place={0: (c_all, c_qm // MEM_W, dall)})
    gs["qn_mem"] = dgq_mem.reshape(MEM_W // HD, HD).sum(axis=0, keepdims=True)

    def mem_kv_bwd(memv, dkmv, dvmv, g, wkv, gk, bd):
        memn = _rms_fwd(memv, g, _mean_all).astype(BF16)
        kv = _dot(memn, wkv)
        dk, dgk = _rms_bwd(kv[:, :MEM_W], gk, dkmv, _mean_heads(bd))
        dkv = jnp.concatenate([dk, dvmv], axis=1).astype(BF16)
        _, dg = _rms_bwd(memv, g, _dot(dkv, wkv, NT), _mean_all)
        return _dot(memn, dkv, TN), dg, dgk

    gw["w_mem_kv"], gs["mem_norm"], dgk_mem = _tokmap(
        "mem_kv_bwd", mem_kv_bwd, [mem, dkm, dvm], [sm["mem_norm"], w["w_mem_kv"], gk_mem, bd256], [],
        [(D, 2 * MEM_W), (1, D), (1, MEM_W)])
    gs["kn_mem"] = dgk_mem.reshape(MEM_W // HD, HD).sum(axis=0, keepdims=True)

    dx1, gs["mix_norm"] = _matmul("proj_bwd_dx", dall, w_all, NT, F32, epi=_norm_bwd_epi, tiles=[x1, dx2],
                                  rows=[sm["mix_norm"]], n_sum=1)
    dw_all = _matmul("proj_bwd_dw", hmix, dall, TN, F32)
    gw["w_gate"], gw["w_in"] = dw_all[:, :c_sb], dw_all[:, c_sb:]
    on_grads("mid", {n: gw[n] for n in GROUPS["mid"]})
    gx, gs["ffn1_norm"], _, _ = _ffn_bwd("ffn1", x, sm["ffn1_norm"], w13_1, w["ffn1_w2"], ffn1_saved, dx1, plan,
                                         ffn_grads("ffn1"))
    return loss, gx, gw, gs


def _shard_shape(name):
    shape, axis = SHARDED_BY_NAME[name]
    return (shape[0] // N_CHIPS, shape[1]) if axis == 0 else (shape[0], shape[1] // N_CHIPS)


def _full_from_shards(name, shards):
    axis = SHARDED_BY_NAME[name][1]
    return shards.reshape(SHARDED_BY_NAME[name][0]) if axis == 0 else jnp.concatenate(list(shards), axis=1)


def _shards_from_full(name, full, dtype):
    axis, n = SHARDED_BY_NAME[name][1], _shard_shape(name)
    return jnp.stack([lax.slice_in_dim(full, c * n[axis], (c + 1) * n[axis], axis=axis).astype(dtype) for c in range(N_CHIPS)])


def _own_shard(name, full, chip):
    axis, n = SHARDED_BY_NAME[name][1], _shard_shape(name)
    return lax.dynamic_slice_in_dim(full, chip * n[axis], n[axis], axis=axis)


SMALL_USED = sum(n for _, n in SMALL)


def _pack_small(d, loss=None):
    parts = [d[n].reshape(-1) for n, _ in SMALL]
    parts.append(jnp.zeros((1,), F32) if loss is None else loss.reshape(1))
    parts.append(jnp.zeros((SMALL_ROWS * D - SMALL_USED - 1,), F32))
    return jnp.concatenate(parts).reshape(SMALL_ROWS, D)


def _unpack_small(v):
    flat, out, r = v.reshape(-1), {}, 0
    for n, k in SMALL:
        out[n] = flat[r:r + k]
        r += k
    return out, flat[r]


def _place():
    return lax.axis_index("x"), lax.axis_index("y"), lax.axis_index("c")


def _other_chips(x, y):
    return [(1 - x, y), (x, 1 - y), (1 - x, 1 - y)]


HBM_SPEC = pl.BlockSpec(memory_space=pl.ANY)


def _chip_sems(n):
    return (pltpu.SemaphoreType.DMA((3 * n,)), pltpu.SemaphoreType.DMA((3 * n,)), pltpu.SemaphoreType.DMA((n,)))


def _gather_copies(ins, outs, send_sems, recv_sems, local_sems):
    x, y, c = _place()
    me = 2 * x + y
    copies = []
    for a, (src, out) in enumerate(zip(ins, outs)):
        copies.append(pltpu.make_async_copy(src, out.at[me], local_sems.at[a]))
        copies += [pltpu.make_async_remote_copy(src_ref=src, dst_ref=out.at[me], send_sem=send_sems.at[3 * a + k],
                                                recv_sem=recv_sems.at[3 * a + k], device_id=(px, py, c), device_id_type=MESH)
                   for k, (px, py) in enumerate(_other_chips(x, y))]
    return copies


def _scatter_copies(ins, outs, send_sems, recv_sems, local_sems):
    x, y, c = _place()
    return [pltpu.make_async_remote_copy(src_ref=src.at[2 * px + py], dst_ref=out.at[k], send_sem=send_sems.at[3 * a + k],
                                         recv_sem=recv_sems.at[3 * a + k], device_id=(px, py, c), device_id_type=MESH)
            for a, (src, out) in enumerate(zip(ins, outs)) for k, (px, py) in enumerate(_other_chips(x, y))]


def _all_gather_chips(arrays):
    n = len(arrays)

    def body(*refs):
        copies = _gather_copies(refs[:n], refs[n:2 * n], *refs[2 * n:])
        for cp in copies:
            cp.start()
        for cp in copies:
            cp.wait()

    return pl.pallas_call(
        body, name="weights_all_gather", in_specs=[HBM_SPEC] * n, out_specs=[HBM_SPEC] * n,
        out_shape=[SDS((N_CHIPS,) + a.shape, a.dtype) for a in arrays], scratch_shapes=list(_chip_sems(n)),
    )(*arrays)


def _swap_with_sibling(name, arrays):
    n = len(arrays)

    def body(*refs):
        x, y, c = _place()
        send_sems, recv_sems = refs[2 * n:]
        copies = [pltpu.make_async_remote_copy(src_ref=refs[a], dst_ref=refs[n + a], send_sem=send_sems.at[a],
                                               recv_sem=recv_sems.at[a], device_id=(x, y, 1 - c), device_id_type=MESH)
                  for a in range(n)]
        for cp in copies:
            cp.start()
        for cp in copies:
            cp.wait()

    return pl.pallas_call(
        body, name=name, in_specs=[HBM_SPEC] * n, out_specs=[HBM_SPEC] * n, out_shape=[SDS(a.shape, a.dtype) for a in arrays],
        scratch_shapes=[pltpu.SemaphoreType.DMA((n,)), pltpu.SemaphoreType.DMA((n,))],
    )(*arrays)


def _all_reduce_small(v):
    n_dev = 8

    def body(v_ref, out_ref, land, send_sems, recv_sems):
        x, y, c = _place()
        me = 4 * x + 2 * y + c
        land[me] = v_ref[...]
        copies = []
        for k in range(1, n_dev):
            peer = (x ^ (k >> 2), y ^ ((k >> 1) & 1), c ^ (k & 1))
            copies.append(pltpu.make_async_remote_copy(src_ref=v_ref, dst_ref=land.at[me], send_sem=send_sems.at[k - 1],
                                                       recv_sem=recv_sems.at[k - 1], device_id=peer, device_id_type=MESH))
        for cp in copies:
            cp.start()
        for cp in copies:
            cp.wait()
        acc = land[0]
        for d in range(1, n_dev):
            acc = acc + land[d]
        out_ref[...] = acc

    return pl.pallas_call(
        body, name="small_all_reduce", in_specs=[pl.BlockSpec(memory_space=pltpu.VMEM)],
        out_specs=pl.BlockSpec(memory_space=pltpu.VMEM), out_shape=SDS(v.shape, v.dtype),
        scratch_shapes=[pltpu.VMEM((n_dev,) + v.shape, v.dtype), pltpu.SemaphoreType.DMA((n_dev - 1,)),
                        pltpu.SemaphoreType.DMA((n_dev - 1,))],
    )(v)


def _adamw(g, wv, m, v):
    m = ADAM_B1 * m + (1.0 - ADAM_B1) * g
    v = ADAM_B2 * v + (1.0 - ADAM_B2) * (g * g)
    m_hat = m / (1.0 - ADAM_B1 ** ADAM_STEP)
    v_hat = v / (1.0 - ADAM_B2 ** ADAM_STEP)
    delta = -ADAM_LR * (m_hat / (jnp.sqrt(v_hat) + ADAM_EPS) + ADAM_WD * wv)
    return delta, m, v


def kernel(x, mem, ffn1_norm, ffn1_w1, ffn1_w3, ffn1_w2, mix_norm, mem_norm, w_in, w_mem_kv, qn_dsa, kn_dsa, qn_mem, kn_mem, w_branch_sb, w_branch_dsa, w_branch_mem, w_gate, b_gate, w_out, ffn2_norm, ffn2_w1, ffn2_w3, ffn2_w2, loss_target, m_ffn1_norm, m_ffn1_w1, m_ffn1_w3, m_ffn1_w2, m_mix_norm, m_mem_norm, m_w_in, m_w_mem_kv, m_qn_dsa, m_kn_dsa, m_qn_mem, m_kn_mem, m_w_branch_sb, m_w_branch_dsa, m_w_branch_mem, m_w_gate, m_b_gate, m_w_out, m_ffn2_norm, m_ffn2_w1, m_ffn2_w3, m_ffn2_w2, v_ffn1_norm, v_ffn1_w1, v_ffn1_w3, v_ffn1_w2, v_mix_norm, v_mem_norm, v_w_in, v_w_mem_kv, v_qn_dsa, v_kn_dsa, v_qn_mem, v_kn_mem, v_w_branch_sb, v_w_branch_dsa, v_w_branch_mem, v_w_gate, v_b_gate, v_w_out, v_ffn2_norm, v_ffn2_w1, v_ffn2_w3, v_ffn2_w2):
    given = dict(locals())
    wts = {n: given[n][0] for n in WEIGHTS}
    moms = {n: given["m_" + n][0] for n in WEIGHTS}
    vars_ = {n: given["v_" + n][0] for n in WEIGHTS}

    plan = _Plan()
    x_i, y_i, _ = _place()
    my_chip = 2 * x_i + y_i

    full = {}

    def gathered(names):
        return lambda res: full.update({n: _full_from_shards(n, g) for n, g in zip(names, res)})

    for host, names in WEIGHT_PIECES:
        shards = [wts[n].astype(BF16) for n in names]
        if host is None:
            gathered(names)(_all_gather_chips(shards))
        else:
            plan.put(host, _Carry(shards, [SDS((N_CHIPS,) + a.shape, BF16) for a in shards], _chip_sems(len(names)),
                                  _gather_copies, gathered(names)))
    small = {n: wts[n].reshape(1, -1) for n, _ in SMALL}

    landed = {}

    def on_grads(group, grads):
        names = GROUPS[group]
        slices = [_shards_from_full(n, grads[n], BF16) for n in names]
        own = [_own_shard(n, grads[n], my_chip) for n in names]
        plan.put(GRAD_HOSTS[group], _Carry(slices, [SDS((3,) + a.shape[1:], BF16) for a in slices], _chip_sems(len(names)),
                                           _scatter_copies, lambda res: landed.update({group: (own, res)})))

    loss, gx, _, gs = _local_step(x[0], mem[0], loss_target[0], full, small, plan, on_grads)
    assert not plan.pending, list(plan.pending)

    def update(hv, ov, wv, mv, vv):
        g = hv + ov
        return (g,) + _adamw(g, wv, mv, vv)

    outs = [{}, {}, {}, {}]
    for group, names in GROUPS.items():
        own, got = landed[group]
        halves = [_tokmap(f"grads_sum_chips_{n}",
                          lambda a, b0, b1, b2: ((a + b0.astype(F32)) + b1.astype(F32)) + b2.astype(F32),
                          [o, g[0], g[1], g[2]], [], [(o.shape[1], F32)])[0] for n, o, g in zip(names, own, got)]
        others = _swap_with_sibling(f"grads_swap_cores_{group}", halves)
        for n, half, other in zip(names, halves, others):
            res = _tokmap(f"adamw_{n}", update, [half, other, wts[n], moms[n], vars_[n]], [], [(half.shape[1], F32)] * 4)
            for d, r in zip(outs, res):
                d[n] = r

    s_red = _all_reduce_small(_pack_small(gs, loss[0, 0]))
    res = _tokmap(
        "adamw_small", lambda g, wv, mv, vv: (g,) + _adamw(g, wv, mv, vv),
        [s_red, _pack_small(small), _pack_small({n: moms[n] for n, _ in SMALL}), _pack_small({n: vars_[n] for n, _ in SMALL})],
        [], [(D, F32)] * 4)
    for d, packed in zip(outs, res):
        d.update(_unpack_small(packed)[0])
    _, total_loss = _unpack_small(s_red)
    return (total_loss, gx[None], *[d[n][None] for d in outs for n in WEIGHTS])
```

```python
import functools

import numpy as np
import jax
import jax.numpy as jnp
from jax import lax
from jax.experimental import pallas as pl
from jax.experimental.pallas import tpu as pltpu

F32, BF16 = jnp.float32, jnp.bfloat16
SDS = jax.ShapeDtypeStruct
MESH = pl.DeviceIdType.MESH

D = 1024
HD = 64
QB = 128
DSA_T_FWD, DSA_T_BWD = 512, 256
D_FF = 2816
SB_W, DSA_W, DSA_OUT_W, MEM_W = 512, 768, 256, 256
DSA_DILS = (1, 4, 16)
MEM_LEN = 256
N_CHIPS = 4
EPS = 1e-6
SCALE = HD ** -0.5
EXHAUSTED = -104.0
SB_FWD_HEADS = 4
SB_QB = 256
SB_WIN = 512
NEG = -1e30
VMEM_LIMIT = 56 * 1024 * 1024

ADAM_LR, ADAM_B1, ADAM_B2, ADAM_EPS, ADAM_WD, ADAM_STEP = 0.001, 0.9, 0.999, 1e-08, 0.01, 10

NN = (((1,), (0,)), ((), ()))
NT = (((1,), (1,)), ((), ()))
TN = (((0,), (0,)), ((), ()))

SHARDED = (
    ("ffn1_w1", (D, D_FF), 1), ("ffn1_w3", (D, D_FF), 1), ("ffn1_w2", (D_FF, D), 0),
    ("w_in", (D, 4096), 1), ("w_mem_kv", (D, 512), 0),
    ("w_branch_sb", (SB_W, D), 1), ("w_branch_dsa", (DSA_OUT_W, D), 1), ("w_branch_mem", (MEM_W, D), 1),
    ("w_gate", (D, 3 * D), 1), ("w_out", (D, D), 0),
    ("ffn2_w1", (D, D_FF), 1), ("ffn2_w3", (D, D_FF), 1), ("ffn2_w2", (D_FF, D), 0),
)
SHARDED_BY_NAME = {n: (sh, ax) for n, sh, ax in SHARDED}
GROUPS = {
    "ffn2": ("ffn2_w1", "ffn2_w3", "ffn2_w2"),
    "mid": ("w_in", "w_mem_kv", "w_branch_sb", "w_branch_dsa", "w_branch_mem", "w_gate", "w_out"),
    "ffn1": ("ffn1_w1", "ffn1_w3", "ffn1_w2"),
}
WEIGHT_PIECES = (
    (None, ("ffn1_w1", "ffn1_w3")),
    ("ffn1_up", ("ffn1_w2", "w_in")),
    ("ffn1_down", ("w_gate", "w_mem_kv", "w_branch_sb", "w_branch_dsa", "w_branch_mem", "w_out")),
    ("proj_dsa", ("ffn2_w2",)),
    ("proj_gate", ("ffn2_w1", "ffn2_w3")),
)
GRAD_HOSTS = {"ffn2": "ffn2_bwd_dn", "mid": "ffn1_bwd_dw13", "ffn1": "ffn1_bwd_dn"}
SMALL = (("ffn1_norm", D), ("mix_norm", D), ("mem_norm", D), ("ffn2_norm", D), ("b_gate", 3 * D),
         ("qn_dsa", HD), ("kn_dsa", HD), ("qn_mem", HD), ("kn_mem", HD))
WEIGHTS = ("ffn1_norm", "ffn1_w1", "ffn1_w3", "ffn1_w2", "mix_norm", "mem_norm", "w_in", "w_mem_kv", "qn_dsa", "kn_dsa",
           "qn_mem", "kn_mem", "w_branch_sb", "w_branch_dsa", "w_branch_mem", "w_gate", "b_gate", "w_out", "ffn2_norm",
           "ffn2_w1", "ffn2_w3", "ffn2_w2")
SMALL_ROWS = 8


def _dot(a, b, dn=NN):
    return lax.dot_general(a, b, dn, preferred_element_type=F32)


def _dot01(x, m01, pieces=3):
    hi = x.astype(BF16)
    r1 = x - hi.astype(F32)
    mid = r1.astype(BF16)
    if pieces == 2:
        return _dot(hi, m01) + _dot(mid, m01)
    lo = (r1 - mid.astype(F32)).astype(BF16)
    return _dot(hi, m01) + _dot(mid, m01) + _dot(lo, m01)


def _pick(n, cands):
    for c in cands:
        if n % c == 0:
            return c
    raise ValueError(f"no tile for {n}")


def _from_dilated(v, d, scr):
    w = v.shape[1] // d
    v = v.astype(F32)
    for c in range(d):
        for p, buf in enumerate(scr[:w // 128]):
            buf[pl.ds(c, v.shape[0], stride=d), :] = v[:, c * w + 128 * p:c * w + 128 * (p + 1)]
    return jnp.concatenate([buf[...] for buf in scr[:w // 128]], axis=1)


def _to_dilated(v, d, scr):
    w = v.shape[1]
    for p, buf in enumerate(scr[:w // 128]):
        buf[...] = v[:, 128 * p:128 * (p + 1)].astype(F32)
    return jnp.concatenate([buf[pl.ds(c, v.shape[0] // d, stride=d), :] for c in range(d) for buf in scr[:w // 128]], axis=1)


def _tokmap(name, fn, tok_ins, consts, tok_outs, acc_outs=(), tile=512, dil_ins=None, dil_outs=None, place=None):
    dil_ins, dil_outs, place = dil_ins or {}, dil_outs or {}, place or {}
    bufs = [(j, buf) for j, (_, _, buf) in place.items() if buf is not None]
    n_buf = len(bufs)
    n = tok_ins[0].shape[0] * dil_ins.get(0, 1)
    tile = _pick(n, [t for t in (512, 256, 128, 64, 32, 16, 8) if t <= tile])
    n_tin, n_in, n_tok, n_acc = len(tok_ins), len(tok_ins) + len(consts), len(tok_outs), len(acc_outs)
    n_scr = max([tok_ins[j].shape[1] // d // 128 for j, d in dil_ins.items() if d > 1]
                + [tok_outs[j][0] // 128 for j, d in dil_outs.items() if d > 1] + [0])

    def body(*refs):
        scr = refs[len(refs) - n_scr:]
        vals = [r[...] for r in refs[:n_in]]
        for j, d in dil_ins.items():
            if d > 1:
                vals[j] = _from_dilated(vals[j], d, scr)
        outs = fn(*vals)
        outs = list(outs) if isinstance(outs, (tuple, list)) else [outs]
        assert len(outs) == n_tok + n_acc, (name, len(outs))
        for j, d in dil_outs.items():
            if d > 1:
                outs[j] = _to_dilated(outs[j], d, scr)
        orefs = refs[n_in + n_buf:]
        for r, v in zip(orefs[:n_tok], outs[:n_tok]):
            r[...] = v.astype(r.dtype)
        if n_acc:
            @pl.when(pl.program_id(0) == 0)
            def _():
                for r in orefs[n_tok:n_tok + n_acc]:
                    r[...] = jnp.zeros(r.shape, r.dtype)
            for r, v in zip(orefs[n_tok:n_tok + n_acc], outs[n_tok:]):
                r[...] += v.astype(r.dtype)

    def tok_spec(width, d):
        return pl.BlockSpec((tile // d, d * width), lambda i: (i, 0))

    in_specs = [tok_spec(a.shape[1] // dil_ins.get(j, 1), dil_ins.get(j, 1)) for j, a in enumerate(tok_ins)]
    in_specs += [pl.BlockSpec(c.shape, lambda i: (0, 0)) for c in consts]
    in_specs += [HBM_SPEC] * n_buf
    out_specs = [tok_spec(w, dil_outs.get(j, 1)) for j, (w, _) in enumerate(tok_outs)]
    out_shape = [SDS((n // dil_outs.get(j, 1), w * dil_outs.get(j, 1)), dt) for j, (w, dt) in enumerate(tok_outs)]
    for j, (total, col_block, _) in place.items():
        out_specs[j] = pl.BlockSpec((tile, tok_outs[j][0]), lambda i, cb=col_block: (i, cb))
        out_shape[j] = SDS((n, total), tok_outs[j][1])
    out_specs += [pl.BlockSpec(s, lambda i: (0, 0)) for s in acc_outs]
    out_shape += [SDS(s, F32) for s in acc_outs]
    res = pl.pallas_call(
        body, name=name, grid=(n // tile,), in_specs=in_specs, out_specs=out_specs, out_shape=out_shape,
        scratch_shapes=[pltpu.VMEM((tile, 128), F32)] * n_scr,
        input_output_aliases={n_in + b: j for b, (j, _) in enumerate(bufs)},
        compiler_params=pltpu.CompilerParams(dimension_semantics=("arbitrary",), vmem_limit_bytes=VMEM_LIMIT),
    )(*tok_ins, *consts, *[buf for _, buf in bufs])
    return res


MATMUL_VMEM_BUDGET = 40 * 1024 * 1024


def _matmul_tiles(m, n, k, a_bytes, b_bytes, o_bytes, extra_bytes, whole_n=False):
    best = None
    for tk in [c for c in (3584, 2816, 2048, 1408, 1024, 512, 256, 128) if k % c == 0]:
        for tm in [c for c in (1408, 1024, 768, 512, 256, 128) if m % c == 0]:
            for tn in [n] if whole_n else [c for c in (1408, 1024, 768, 512, 256, 128) if n % c == 0]:
                need = 2 * tk * (tm * a_bytes + tn * b_bytes) + tm * tn * (2 * o_bytes + 2 * extra_bytes + 8)
                if need > MATMUL_VMEM_BUDGET:
                    continue
                score = (min(tm, 512) * min(tn, 512), tk, tm * tn, tn)
                if best is None or score > best[0]:
                    best = (score, (tm, tn, tk))
    return best[1]


class _Carry:
    def __init__(self, ins, outs, sems, copies, then):
        self.ins, self.outs, self.sems, self.copies, self.then = ins, outs, sems, copies, then


class _Plan:
    def __init__(self):
        self.pending = {}

    def put(self, host, carry):
        assert host not in self.pending, host
        self.pending[host] = carry

    def take(self, host):
        return self.pending.pop(host, None)


def _matmul(name, a, b, dn, out_dtype, epi=None, tiles=(), rows=(), plan=None, a_pro=None, n_sum=0):
    if dn == NN:
        (m, k), n = a.shape, b.shape[1]
    elif dn == NT:
        (m, k), n = a.shape, b.shape[0]
    else:
        (k, m), n = a.shape, b.shape[1]
    n_t, n_r = len(tiles), len(rows)
    pro, n_parts = a_pro if a_pro is not None else (None, 1)
    if pro is not None:
        assert dn == NN and n == _pick(n, (1024, 512))
        k //= n_parts
    if pro is not None:
        tm, tn, tk = _pick(m, (256, 128)), n, k
    else:
        tm, tn, tk = _matmul_tiles(m, n, k, a.dtype.itemsize, b.dtype.itemsize, jnp.dtype(out_dtype).itemsize,
                                   sum(t.dtype.itemsize for t in tiles), whole_n=n_sum > 0)
    nk = k // tk
    grid = (m // tm, n // tn, nk)
    assert pro is None or grid[1] == 1
    assert n_sum == 0 or grid[1] == 1
    carry = plan.take(name) if plan is not None else None
    n_ci, n_co = (len(carry.ins), len(carry.outs)) if carry else (0, 0)
    n_keep = 1 if pro is not None else 0

    def body(*refs):
        a_refs, b_ref, rest = refs[:n_parts], refs[n_parts], refs[n_parts + 1:]
        extras, rest = rest[:n_t + n_r], rest[n_t + n_r:]
        c_in, o_ref, rest = rest[:n_ci], rest[n_ci], rest[n_ci + 1:]
        sums, rest = rest[:n_sum], rest[n_sum:]
        keep, c_out, scratch = rest[:n_keep], rest[n_keep:n_keep + n_co], rest[n_keep + n_co:]
        ids = [pl.program_id(d) for d in range(3)]
        if n_sum:
            @pl.when((ids[0] == 0) & (ids[2] == 0))
            def _():
                for r in sums:
                    r[...] = jnp.zeros(r.shape, F32)
        if carry:
            sems = scratch[1:] if nk > 1 else scratch

            @pl.when((ids[0] == 0) & (ids[1] == 0) & (ids[2] == 0))
            def _():
                for cp in carry.copies(c_in, c_out, *sems):
                    cp.start()

        if pro is not None:
            av = pro(*[r[...] for r in a_refs])
            keep[0][...] = av
        else:
            av = a_refs[0][...].astype(BF16)
        part = _dot(av, b_ref[...].astype(BF16), dn)

        def finish(r):
            if epi is not None:
                r = epi(r, *[e[...] for e in extras])
            if n_sum:
                for ref, v in zip(sums, r[1:]):
                    ref[...] += v
                r = r[0]
            o_ref[...] = r.astype(o_ref.dtype)

        if nk == 1:
            finish(part)
        else:
            acc = scratch[0]

            @pl.when(ids[2] == 0)
            def _():
                acc[...] = part

            @pl.when(ids[2] > 0)
            def _():
                acc[...] += part

            @pl.when(ids[2] == nk - 1)
            def _():
                finish(acc[...])

        if carry:
            @pl.when((ids[0] == grid[0] - 1) & (ids[1] == grid[1] - 1) & (ids[2] == nk - 1))
            def _():
                for cp in carry.copies(c_in, c_out, *sems):
                    cp.wait()

    if dn == TN:
        a_specs = [pl.BlockSpec((tk, tm), lambda i, j, kk: (kk, i))]
    else:
        a_specs = [pl.BlockSpec((tm, tk), lambda i, j, kk, p=p: (i, kk + p * nk)) for p in range(n_parts)]
    b_spec = pl.BlockSpec((tn, tk), lambda i, j, kk: (j, kk)) if dn == NT else pl.BlockSpec((tk, tn), lambda i, j, kk: (kk, j))
    in_specs = a_specs + [b_spec] + [pl.BlockSpec((tm, tn), lambda i, j, kk: (i, j)) for _ in tiles]
    in_specs += [pl.BlockSpec((1, tn), lambda i, j, kk: (0, j)) for _ in rows] + [HBM_SPEC] * n_ci
    res = pl.pallas_call(
        body, name=name, grid=grid, in_specs=in_specs,
        out_specs=[pl.BlockSpec((tm, tn), lambda i, j, kk: (i, j))] + [pl.BlockSpec((1, tn), lambda i, j, kk: (0, 0))] * n_sum
        + [pl.BlockSpec((tm, tk), lambda i, j, kk: (i, kk))] * n_keep + [HBM_SPEC] * n_co,
        out_shape=[SDS((m, n), out_dtype)] + [SDS((1, n), F32)] * n_sum + [SDS((m, k), BF16)] * n_keep
        + (list(carry.outs) if carry else []),
        scratch_shapes=([pltpu.VMEM((tm, tn), F32)] if nk > 1 else []) + (list(carry.sems) if carry else []),
        compiler_params=pltpu.CompilerParams(
            dimension_semantics=("arbitrary",) * 3 if (carry or n_sum) else ("parallel", "parallel", "arbitrary"),
            vmem_limit_bytes=VMEM_LIMIT),
    )(*[a] * n_parts, b, *tiles, *rows, *(carry.ins if carry else []))
    if carry:
        carry.then(res[1 + n_sum + n_keep:])
    return tuple(res[:1 + n_sum + n_keep]) if n_sum + n_keep else res[0]


def _mean_all(v):
    return jnp.mean(v, axis=-1, keepdims=True)


def _head_sums(v, bd):
    w = bd.shape[0]
    return jnp.concatenate([_dot01(v[:, j:j + w], bd, 2) for j in range(0, v.shape[1], w)], axis=1)


def _mean_heads(bd):
    return lambda v: _head_sums(v, bd) * (1.0 / HD)


def _rms_fwd(x, g, mean):
    return x * lax.rsqrt(mean(x * x) + EPS) * g


def _rms_bwd(x, g, dy, mean):
    r = lax.rsqrt(mean(x * x) + EPS)
    dn = dy * g
    dx = r * dn - x * (r * r * r) * mean(dn * x)
    return dx, jnp.sum(dy * x * r, axis=0, keepdims=True)


def _swap_halves(x):
    w = x.shape[1]
    lane = lax.broadcasted_iota(jnp.int32, x.shape, 1)
    return jnp.where(lane % HD < HD // 2, pltpu.roll(x, w - HD // 2, 1), pltpu.roll(x, HD // 2, 1))


def _lanes(t, w):
    return jnp.tile(t, (1, w // t.shape[1]))


def _rope_fwd(x, cos, sin_signed):
    return x * _lanes(cos, x.shape[1]) + _swap_halves(x) * _lanes(sin_signed, x.shape[1])


def _rope_bwd(dy, cos, sin_signed):
    return dy * _lanes(cos, dy.shape[1]) + _swap_halves(dy * _lanes(sin_signed, dy.shape[1]))


def _bcast_heads(cols):
    return jnp.concatenate([jnp.broadcast_to(c, (c.shape[0], HD)) for c in cols], axis=1)


def _softplus(z):
    return jnp.maximum(z, 0.0) + jnp.log(1.0 + jnp.exp(-jnp.abs(z)))


def _block_diag(w):
    h = np.arange(w) // HD
    return jnp.asarray(h[:, None] == h[None, :], BF16)


def _sb_window(i, t):
    hi = (i + 1) * SB_QB - t * SB_WIN
    lo = hi - SB_WIN
    ws = pl.multiple_of(jnp.maximum(lo, 0), SB_QB)
    kpos = ws + lax.broadcasted_iota(jnp.int32, (SB_QB, SB_WIN), 1)
    qpos = i * SB_QB + lax.broadcasted_iota(jnp.int32, (SB_QB, SB_WIN), 0)
    return (kpos < qpos) & (kpos >= lo) & (kpos < hi), ws


def _sb_fwd(qkv):
    s = qkv.shape[0]
    assert s >= SB_WIN
    nq = s // SB_QB
    nh = SB_FWD_HEADS
    bw = HD * nh
    ngroups = SB_W // bw

    def body(q_ref, k_ref, v_ref, later_ref, o_ref, tot_ref, w0_ref, b0_ref, nb_ref):
        p, i = pl.program_id(0), pl.program_id(1)
        q = q_ref[...]
        later_of = later_ref[...]

        def window(t, tots, outs, keep):
            mask, ws = _sb_window(i, t)
            kw, vw = k_ref[pl.ds(ws, SB_WIN), :], v_ref[pl.ds(ws, SB_WIN), :]
            new_t, new_o, w_all, b_all = [], [], [], []
            for hh in range(nh):
                sl = slice(HD * hh, HD * hh + HD)
                z = _dot(q[:, sl], kw[:, sl], NT) * SCALE
                sp = _softplus(z)
                lf = jnp.where(mask, -sp, 0.0)
                later = tots[hh] + _dot01(lf, later_of, 2)
                w = jnp.where(mask, jnp.exp(z - sp + later), 0.0).astype(BF16)
                new_o.append(outs[hh] + _dot(w, vw[:, sl]))
                new_t.append(tots[hh] + jnp.sum(lf, axis=1, keepdims=True))
                if keep:
                    w_all.append(w)
                    b_all.append(jnp.where(mask, jnp.exp(z - sp), 0.0).astype(BF16))
            if keep:
                w0_ref[...] = jnp.concatenate(w_all, axis=1)
                b0_ref[...] = jnp.concatenate(b_all, axis=1)
            alive = functools.reduce(jnp.maximum, [jnp.max(v) for v in new_t])
            return t + 1, alive, tuple(new_t), tuple(new_o)

        zt, zo = jnp.zeros((SB_QB, 1), F32), jnp.zeros((SB_QB, HD), F32)
        first = window(jnp.int32(0), (zt,) * nh, (zo,) * nh, True)
        t, _, tots, outs = lax.while_loop(lambda c: ((i + 1) * SB_QB - c[0] * SB_WIN > 0) & (c[1] > EXHAUSTED),
                                          lambda c: window(c[0], c[2], c[3], False), first)
        o_ref[...] = jnp.concatenate(outs, axis=1).astype(o_ref.dtype)
        tot_ref[...] = _bcast_heads(tots)
        nb_ref[p, i] = t

    whole = lambda off: pl.BlockSpec((s, bw), lambda p, i: (0, off + p), pipeline_mode=pl.Buffered(1))
    tile = pl.BlockSpec((SB_QB, bw), lambda p, i: (i, p))
    tri = pl.BlockSpec((SB_WIN, SB_WIN), lambda p, i: (0, 0), pipeline_mode=pl.Buffered(1))
    near = pl.BlockSpec((SB_QB, nh * SB_WIN), lambda p, i: (i, p))
    n_heads = SB_W // HD
    idx = np.arange(SB_WIN)
    return pl.pallas_call(
        body, name="sb_fwd", grid=(ngroups, nq),
        in_specs=[tile, whole(ngroups), whole(2 * ngroups), tri],
        out_specs=[tile, tile, near, near, pl.BlockSpec(memory_space=pltpu.SMEM)],
        out_shape=[SDS((s, SB_W), BF16), SDS((s, SB_W), F32), SDS((s, n_heads * SB_WIN), BF16), SDS((s, n_heads * SB_WIN), BF16),
                   SDS((ngroups, nq), jnp.int32)],
        compiler_params=pltpu.CompilerParams(dimension_semantics=("arbitrary", "arbitrary"), vmem_limit_bytes=VMEM_LIMIT),
    )(qkv, qkv, qkv, jnp.asarray(idx[:, None] > idx[None, :], BF16))


def _sb_bwd(qkv, do, tot, nblk, w0, b0, buf, col):
    s = qkv.shape[0]
    nq = s // SB_QB
    npairs = SB_W // 128

    def body(nb_ref, q_ref, k_ref, v_ref, do_ref, tot_ref, upto_ref, before_ref, w0_ref, b0_ref, buf_ref,
             dq_ref, dk_ref, dv_ref):
        p, i = pl.program_id(0), pl.program_id(1)

        @pl.when(i == 0)
        def _():
            dk_ref[...] = jnp.zeros(dk_ref.shape, F32)
            dv_ref[...] = jnp.zeros(dv_ref.shape, F32)

        upto = upto_ref[...]
        before = before_ref[...]
        q, dout, tt = q_ref[...], do_ref[...], tot_ref[...]
        n = nb_ref[p * 2 // SB_FWD_HEADS, i]

        def step(it, c):
            pres, gpres, dqs = c
            mask, ws = _sb_window(i, n - 1 - it)
            kw, vw = k_ref[pl.ds(ws, SB_WIN), :], v_ref[pl.ds(ws, SB_WIN), :]
            new_p, new_g, new_dq, dks, dvs = [], [], [], [], []
            for hh in range(2):
                sl = slice(HD * hh, HD * hh + HD)
                z = _dot(q[:, sl], kw[:, sl], NT) * SCALE
                sp = _softplus(z)
                lf = jnp.where(mask, -sp, 0.0)
                later = tt[:, HD * hh:HD * hh + 1] - (pres[hh] + _dot01(lf, upto, 2))
                w = jnp.where(mask, jnp.exp(z - sp + later), 0.0)
                beta = jnp.exp(z - sp)
                g = _dot(dout[:, sl], vw[:, sl], NT) * w
                g_far = gpres[hh] + _dot(g.astype(BF16), before)
                dz = (jnp.where(mask, g * (1.0 - beta) - beta * g_far, 0.0) * SCALE).astype(BF16)
                new_dq.append(dqs[hh] + _dot(dz, kw[:, sl]))
                dks.append(_dot(dz, q[:, sl], TN))
                dvs.append(_dot(w.astype(BF16), dout[:, sl], TN))
                new_p.append(pres[hh] + jnp.sum(lf, axis=1, keepdims=True))
                new_g.append(gpres[hh] + jnp.sum(g, axis=1, keepdims=True))
            dk_ref[pl.ds(ws, SB_WIN), :] += jnp.concatenate(dks, axis=1)
            dv_ref[pl.ds(ws, SB_WIN), :] += jnp.concatenate(dvs, axis=1)
            return tuple(new_p), tuple(new_g), tuple(new_dq)

        zt, zo = jnp.zeros((SB_QB, 1), F32), jnp.zeros((SB_QB, HD), F32)
        _, gpres, dqs = lax.fori_loop(0, n - 1, step, ((zt, zt), (zt, zt), (zo, zo)))
        _, ws = _sb_window(i, 0)
        kw, vw = k_ref[pl.ds(ws, SB_WIN), :], v_ref[pl.ds(ws, SB_WIN), :]
        dqs, dks, dvs = list(dqs), [], []
        for hh in range(2):
            sl = slice(HD * hh, HD * hh + HD)
            w = w0_ref[:, SB_WIN * hh:SB_WIN * (hh + 1)]
            beta = b0_ref[:, SB_WIN * hh:SB_WIN * (hh + 1)].astype(F32)
            g = _dot(dout[:, sl], vw[:, sl], NT) * w.astype(F32)
            g_far = gpres[hh] + _dot(g.astype(BF16), before)
            dz = ((g * (1.0 - beta) - beta * g_far) * SCALE).astype(BF16)
            dqs[hh] = dqs[hh] + _dot(dz, kw[:, sl])
            dks.append(_dot(dz, q[:, sl], TN))
            dvs.append(_dot(w, dout[:, sl], TN))
        dk_ref[pl.ds(ws, SB_WIN), :] += jnp.concatenate(dks, axis=1)
        dv_ref[pl.ds(ws, SB_WIN), :] += jnp.concatenate(dvs, axis=1)
        dq_ref[...] = jnp.concatenate(dqs, axis=1).astype(dq_ref.dtype)

    whole_in = lambda off: pl.BlockSpec((s, 128), lambda p, i: (0, off + p), pipeline_mode=pl.Buffered(1))
    whole_out = pl.BlockSpec((s, 128), lambda p, i: (0, p), pipeline_mode=pl.Buffered(1))
    tile = pl.BlockSpec((SB_QB, 128), lambda p, i: (i, p))
    near = pl.BlockSpec((SB_QB, 2 * SB_WIN), lambda p, i: (i, p))
    dq_tile = pl.BlockSpec((SB_QB, 128), lambda p, i: (i, col // 128 + p))
    tri = pl.BlockSpec((SB_WIN, SB_WIN), lambda p, i: (0, 0), pipeline_mode=pl.Buffered(1))
    idx = np.arange(SB_WIN)
    return pl.pallas_call(
        body, name="sb_bwd", grid=(npairs, nq),
        in_specs=[pl.BlockSpec(memory_space=pltpu.SMEM), tile, whole_in(npairs), whole_in(2 * npairs), tile, tile, tri, tri,
                  near, near, HBM_SPEC],
        out_specs=[dq_tile, whole_out, whole_out],
        out_shape=[SDS(buf.shape, buf.dtype)] + [SDS((s, SB_W), F32)] * 2,
        input_output_aliases={10: 0},
        compiler_params=pltpu.CompilerParams(dimension_semantics=("arbitrary", "arbitrary"), vmem_limit_bytes=VMEM_LIMIT),
    )(nblk, qkv, qkv, qkv, do, tot, jnp.asarray(idx[:, None] <= idx[None, :], BF16), jnp.asarray(idx[:, None] < idx[None, :], BF16),
      w0, b0, buf)


def _dsa_mask(DSA_T, has_prev):
    r = lax.broadcasted_iota(jnp.int32, (DSA_T, QB + DSA_T), 0)
    j = lax.broadcasted_iota(jnp.int32, (DSA_T, QB + DSA_T), 1) - QB
    return (j <= r) & (j >= r - QB) & ((j >= 0) | has_prev)


def _dsa_fwd(q, k, v, dil):
    n = q.shape[0]
    DSA_T = DSA_T_FWD
    nt = n // DSA_T

    def body(q_ref, kc_ref, kp_ref, vc_ref, vp_ref, o_ref, lse_ref):
        mask = _dsa_mask(DSA_T, pl.program_id(1) > 0)
        outs, lses = [], []
        for hh in range(DSA_OUT_W // HD):
            sl = slice(HD * hh, HD * hh + HD)
            kcat = jnp.concatenate([kp_ref[:, sl], kc_ref[:, sl]], axis=0)
            vcat = jnp.concatenate([vp_ref[:, sl], vc_ref[:, sl]], axis=0)
            sc = jnp.where(mask, _dot(q_ref[:, sl], kcat, NT) * SCALE, NEG)
            m = jnp.max(sc, axis=1, keepdims=True)
            p = jnp.exp(sc - m)
            den = jnp.sum(p, axis=1, keepdims=True)
            outs.append(_dot(p.astype(BF16), vcat) / den)
            lses.append(m + jnp.log(den))
        o_ref[...] = jnp.concatenate(outs, axis=1)
        lse_ref[...] = _bcast_heads(lses)

    cur = pl.BlockSpec((DSA_T, DSA_OUT_W), lambda c, i: (i, c))
    prev = pl.BlockSpec((QB, DSA_OUT_W), lambda c, i: (jnp.maximum(i * (DSA_T // QB) - 1, 0), c))
    o, lse = pl.pallas_call(
        body, name=f"dsa_fwd_d{dil}", grid=(dil, nt), in_specs=[cur, cur, prev, cur, prev], out_specs=[cur, cur],
        out_shape=[SDS((n, dil * DSA_OUT_W), F32)] * 2,
        compiler_params=pltpu.CompilerParams(dimension_semantics=("parallel", "parallel")),
    )(q, k, k, v, v)
    return o, lse


def _dsa_bwd(q, k, v, do, cc, lse, dil):
    n = q.shape[0]
    DSA_T = DSA_T_BWD
    nt = n // DSA_T
    per = DSA_T // QB

    def body(qj_ref, qn_ref, kp_ref, kj_ref, vp_ref, vj_ref, doj_ref, don_ref, cj_ref, cn_ref, lj_ref, ln_ref,
             dq_ref, dk_ref, dv_ref):
        j = pl.program_id(1)
        mask = _dsa_mask(DSA_T, j > 0)
        r = lax.broadcasted_iota(jnp.int32, (QB, DSA_T), 0)
        kk = lax.broadcasted_iota(jnp.int32, (QB, DSA_T), 1)
        m_next = (kk >= r + QB) & (j + 1 < nt)
        dqs, dks, dvs = [], [], []
        for hh in range(DSA_OUT_W // HD):
            sl = slice(HD * hh, HD * hh + HD)
            one = slice(HD * hh, HD * hh + 1)
            qj, qn, kj, vj, doj, don = (t[:, sl] for t in (qj_ref, qn_ref, kj_ref, vj_ref, doj_ref, don_ref))
            kcat = jnp.concatenate([kp_ref[:, sl], kj], axis=0)
            vcat = jnp.concatenate([vp_ref[:, sl], vj], axis=0)
            p1 = jnp.where(mask, jnp.exp(_dot(qj, kcat, NT) * SCALE - lj_ref[:, one]), 0.0)
            ds1 = (p1 * (_dot(doj, vcat, NT) + cj_ref[:, one]) * SCALE).astype(BF16)
            p2 = jnp.where(m_next, jnp.exp(_dot(qn, kj, NT) * SCALE - ln_ref[:, one]), 0.0)
            ds2 = (p2 * (_dot(don, vj, NT) + cn_ref[:, one]) * SCALE).astype(BF16)
            dqs.append(_dot(ds1, kcat))
            dks.append(_dot(ds1[:, QB:], qj, TN) + _dot(ds2, qn, TN))
            dvs.append(_dot(p1[:, QB:].astype(BF16), doj, TN) + _dot(p2.astype(BF16), don, TN))
        dq_ref[...] = jnp.concatenate(dqs, axis=1).astype(dq_ref.dtype)
        dk_ref[...] = jnp.concatenate(dks, axis=1).astype(dk_ref.dtype)
        dv_ref[...] = jnp.concatenate(dvs, axis=1).astype(dv_ref.dtype)

    cur = pl.BlockSpec((DSA_T, DSA_OUT_W), lambda c, j: (j, c))
    prev = pl.BlockSpec((QB, DSA_OUT_W), lambda c, j: (jnp.maximum(j * per - 1, 0), c))
    nxt = pl.BlockSpec((QB, DSA_OUT_W), lambda c, j: (jnp.minimum((j + 1) * per, n // QB - 1), c))
    dq, dk, dv = pl.pallas_call(
        body, name=f"dsa_bwd_d{dil}", grid=(dil, nt),
        in_specs=[cur, nxt, prev, cur, prev, cur, cur, nxt, cur, nxt, cur, nxt], out_specs=[cur, cur, cur],
        out_shape=[SDS((n, dil * DSA_OUT_W), BF16)] * 3,
        compiler_params=pltpu.CompilerParams(dimension_semantics=("parallel", "parallel")),
    )(q, q, k, k, v, v, do, do, cc, cc, lse, lse)
    return dq, dk, dv


def _norm_bwd_epi(acc, xv, dyv, g):
    dx, dg = _rms_bwd(xv, g, acc, _mean_all)
    return dx + dyv, dg


def _ffn_fwd(tag, x, gain, w13, w2, plan=None, target=None):
    n = _tokmap(f"{tag}_norm", lambda xv, g: _rms_fwd(xv, g, _mean_all), [x], [gain], [(D, BF16)])[0]
    ab = _matmul(f"{tag}_up", n, w13, NN, BF16, plan=plan)

    def gate(av, bv):
        a, b = av.astype(F32), bv.astype(F32)
        return (a * jax.nn.sigmoid(a) * b).astype(BF16)

    if target is None:
        y, h = _matmul(f"{tag}_down", ab, w2(), NN, F32, epi=lambda acc, res: res + 0.5 * acc, tiles=[x], plan=plan,
                       a_pro=(gate, 2))
        return y, (n, ab, h)

    def loss_epi(acc, res, tv):
        e = res + 0.5 * acc - tv
        return e * (1.0 / D), (0.5 / D) * jnp.sum(e * e, axis=0, keepdims=True)

    dy, loss_row, h = _matmul(f"{tag}_down", ab, w2(), NN, F32, epi=loss_epi, tiles=[x, target], plan=plan, a_pro=(gate, 2),
                              n_sum=1)
    return (dy, loss_row), (n, ab, h)


def _ffn_bwd(tag, x, gain, w13, w2, saved, dy, plan=None, on_dw=None, final=False):
    n, ab, h = saved
    dh = _matmul(f"{tag}_bwd_dh", dy, w2, NT, BF16, epi=lambda acc: 0.5 * acc)

    def gate_bwd(abv, dhv):
        a, b, dhf = abv[:, :D_FF].astype(F32), abv[:, D_FF:].astype(F32), dhv.astype(F32)
        sg = jax.nn.sigmoid(a)
        da = dhf * b * (sg * (1.0 + a * (1.0 - sg)))
        return jnp.concatenate([da, dhf * (a * sg)], axis=1)

    dab = _tokmap(f"{tag}_bwd_gate", gate_bwd, [ab, dh], [], [(2 * D_FF, BF16)], tile=256)[0]
    dw2 = _matmul(f"{tag}_bwd_dw2", h, dy, TN, F32, epi=lambda acc: 0.5 * acc)
    dw13 = _matmul(f"{tag}_bwd_dw13", n, dab, TN, F32, plan=plan)
    if on_dw is not None:
        on_dw(dw13, dw2)
    if not final:
        dx, dgain = _matmul(f"{tag}_bwd_dn", dab, w13, NT, F32, epi=_norm_bwd_epi, tiles=[x, dy], rows=[gain], plan=plan,
                            n_sum=1)
    else:
        dn = _matmul(f"{tag}_bwd_dn", dab, w13, NT, F32, plan=plan)
        dx, dgain = _tokmap(f"{tag}_bwd_norm", lambda xv, dnv, dyv, g: _norm_bwd_epi(dnv, xv, dyv, g), [x, dn, dy], [gain],
                            [(D, F32)], [(1, D)])
    return dx, dgain, dw13, dw2


def _rope_tables(s):
    half = HD // 2
    inv_freq = jnp.power(10000.0, -jnp.arange(half, dtype=F32) / half)
    ang = jnp.arange(s).astype(F32)[:, None] * inv_freq[None, :]
    cos, sin = jnp.cos(ang), jnp.sin(ang)
    return jnp.tile(jnp.concatenate([cos, cos], axis=1), (1, 2)), jnp.tile(jnp.concatenate([-sin, sin], axis=1), (1, 2))


def _local_step(x, mem, tgt, w, sm, plan=None, on_grads=None):
    s = x.shape[0]
    assert s % (max(DSA_T_FWD, DSA_T_BWD) * max(DSA_DILS)) == 0
    on_grads = on_grads or (lambda group, grads: None)
    c_sb, c_dsa, c_qm, c_all = 3 * D, 3 * D + 3 * SB_W, 3 * D + 3 * SB_W + 3 * DSA_W, 3 * D + 4096
    cos, sin = _rope_tables(s)
    bd768 = bd256 = _block_diag(128)
    gq_dsa, gk_dsa = jnp.tile(sm["qn_dsa"], (1, DSA_W // HD)), jnp.tile(sm["kn_dsa"], (1, DSA_W // HD))
    gq_mem, gk_mem = jnp.tile(sm["qn_mem"], (1, MEM_W // HD)), jnp.tile(sm["kn_mem"], (1, MEM_W // HD))

    w13_1 = jnp.concatenate([w["ffn1_w1"], w["ffn1_w3"]], axis=1)
    x1, ffn1_saved = _ffn_fwd("ffn1", x, sm["ffn1_norm"], w13_1, lambda: w["ffn1_w2"], plan)
    w_all = jnp.concatenate([w["w_gate"], w["w_in"]], axis=1)
    wb_sb, wb_dsa, wb_mem = w["w_branch_sb"], w["w_branch_dsa"], w["w_branch_mem"]
    hmix = _tokmap("mix_norm", lambda xv, g: _rms_fwd(xv, g, _mean_all), [x1], [sm["mix_norm"]], [(D, BF16)])[0]
    qkv_sb = _matmul("proj_sb", hmix, w_all[:, c_sb:c_dsa], NN, BF16)
    qkv_dsa = _matmul("proj_dsa", hmix, w_all[:, c_dsa:c_qm], NN, BF16, plan=plan)
    q_mem = _matmul("proj_qmem", hmix, w_all[:, c_qm:], NN, BF16)
    gpre = _matmul("proj_gate", hmix, w_all[:, :c_sb], NN, BF16, epi=lambda acc, b: acc + b, rows=[sm["b_gate"]], plan=plan)

    o_sb, sb_tot, sb_w0, sb_b0, sb_nblk = _sb_fwd(qkv_sb)

    def dsa_prep(qkv, cs, sn, gq, gk, bd):
        mean = _mean_heads(bd)
        qn = _rope_fwd(_rms_fwd(qkv[:, :DSA_W].astype(F32), gq, mean), cs, sn)
        kn = _rope_fwd(_rms_fwd(qkv[:, DSA_W:2 * DSA_W].astype(F32), gk, mean), cs, sn)
        v = qkv[:, 2 * DSA_W:]
        outs = []
        for t in (qn, kn, v):
            outs += [t[:, DSA_OUT_W * g:DSA_OUT_W * (g + 1)] for g in range(3)]
        return outs

    dsa_in = _tokmap("dsa_prep", dsa_prep, [qkv_dsa, cos, sin], [gq_dsa, gk_dsa, bd768], [(DSA_OUT_W, BF16)] * 9, tile=256,
                     dil_outs={j: DSA_DILS[j % 3] for j in range(9)})
    dsa_q, dsa_k, dsa_v = dsa_in[0:3], dsa_in[3:6], dsa_in[6:9]
    dsa_o, dsa_lse = zip(*[_dsa_fwd(dsa_q[g], dsa_k[g], dsa_v[g], DSA_DILS[g]) for g in range(3)])

    def alphas(l0, l1, l2):
        m = jnp.maximum(jnp.maximum(l0, l1), l2)
        e = [jnp.exp(l - m) for l in (l0, l1, l2)]
        tot = e[0] + e[1] + e[2]
        return [t / tot for t in e]

    def dsa_mix(o0, o1, o2, l0, l1, l2):
        a = alphas(l0, l1, l2)
        return a[0] * o0 + a[1] * o1 + a[2] * o2

    o_dsa = _tokmap("dsa_mix", dsa_mix, [*dsa_o, *dsa_lse], [], [(DSA_OUT_W, BF16)], tile=256,
                    dil_ins={j: DSA_DILS[j % 3] for j in range(6)})[0]

    def mem_kv(memv, g, wkv, gk, bd):
        kv = _dot(_rms_fwd(memv, g, _mean_all).astype(BF16), wkv)
        return _rms_fwd(kv[:, :MEM_W], gk, _mean_heads(bd)), kv[:, MEM_W:]

    km, vm = _tokmap("mem_kv", mem_kv, [mem], [sm["mem_norm"], w["w_mem_kv"], gk_mem, bd256], [(MEM_W, BF16)] * 2)

    def mem_probs(qv, kmv, gq, bd):
        qn = _rms_fwd(qv.astype(F32), gq, _mean_heads(bd)).astype(BF16)
        ps = []
        for h in range(MEM_W // HD):
            sl = slice(HD * h, HD * h + HD)
            sc = _dot(qn[:, sl], kmv[:, sl], NT) * SCALE
            e = jnp.exp(sc - jnp.max(sc, axis=1, keepdims=True))
            ps.append(e / jnp.sum(e, axis=1, keepdims=True))
        return qn, ps

    def mem_attn(qv, kmv, vmv, gq, bd):
        _, ps = mem_probs(qv, kmv, gq, bd)
        return jnp.concatenate([_dot(p.astype(BF16), vmv[:, HD * h:HD * h + HD]) for h, p in enumerate(ps)], axis=1)

    o_mem = _tokmap("mem_attn", mem_attn, [q_mem], [km, vm, gq_mem, bd256], [(MEM_W, BF16)])[0]

    def merge(osb, odsa, omem, gp, w_sb, w_dsa, w_mem):
        gates = jax.nn.sigmoid(gp.astype(F32))
        ys = (_dot(osb, w_sb), _dot(odsa, w_dsa), _dot(omem, w_mem))
        return gates, ys, gates[:, :D] * ys[0] + gates[:, D:2 * D] * ys[1] + gates[:, 2 * D:] * ys[2]

    merged = _tokmap("merge", lambda *a: merge(*a)[2], [o_sb, o_dsa, o_mem, gpre], [wb_sb, wb_dsa, wb_mem], [(D, BF16)],
                     tile=256)[0]
    x2 = _matmul("out_proj", merged, w["w_out"], NN, F32, epi=lambda acc, res: res + acc, tiles=[x1])
    w13_2 = jnp.concatenate([w["ffn2_w1"], w["ffn2_w3"]], axis=1)
    (dy, loss_row), ffn2_saved = _ffn_fwd("ffn2", x2, sm["ffn2_norm"], w13_2, lambda: w["ffn2_w2"], target=tgt)
    loss = jnp.sum(loss_row).reshape(1, 1)

    gw, gs = {}, {}
    def ffn_grads(tag):
        def on_dw(dw13, dw2):
            gw[f"{tag}_w1"], gw[f"{tag}_w3"], gw[f"{tag}_w2"] = dw13[:, :D_FF], dw13[:, D_FF:], dw2
            on_grads(tag, {n: gw[n] for n in (f"{tag}_w1", f"{tag}_w3", f"{tag}_w2")})
        return on_dw

    dx2, gs["ffn2_norm"], _, _ = _ffn_bwd("ffn2", x2, sm["ffn2_norm"], w13_2, w["ffn2_w2"], ffn2_saved, dy, plan,
                                          ffn_grads("ffn2"))
    dmerged = _matmul("out_proj_bwd_dx", dx2, w["w_out"], NT, BF16)
    gw["w_out"] = _matmul("out_proj_bwd_dw", merged, dx2, TN, F32)

    def merge_bwd(osb, odsa, omem, gp, dm, w_sb, w_dsa, w_mem):
        gates, ys, _ = merge(osb, odsa, omem, gp, w_sb, w_dsa, w_mem)
        dmf = dm.astype(F32)
        dgp, dos, dws = [], [], []
        for b, (ov, wv) in enumerate(((osb, w_sb), (odsa, w_dsa), (omem, w_mem))):
            gb = gates[:, D * b:D * (b + 1)]
            dgp.append(dmf * ys[b] * gb * (1.0 - gb))
            dyb = (dmf * gb).astype(BF16)
            dos.append(_dot(dyb, wv, NT))
            dws.append(_dot(ov, dyb, TN))
        dgp = jnp.concatenate(dgp, axis=1)
        return dos[0], dos[1], dos[2], dgp, dws[0], dws[1], dws[2], jnp.sum(dgp, axis=0, keepdims=True)

    do_sb, do_dsa, do_mem, dgpre, gw["w_branch_sb"], gw["w_branch_dsa"], gw["w_branch_mem"], gs["b_gate"] = _tokmap(
        "merge_bwd", merge_bwd, [o_sb, o_dsa, o_mem, gpre, dmerged], [wb_sb, wb_dsa, wb_mem],
        [(SB_W, BF16), (DSA_OUT_W, F32), (MEM_W, BF16), (3 * D, BF16)],
        [(SB_W, D), (DSA_OUT_W, D), (MEM_W, D), (1, 3 * D)], tile=256, place={3: (c_all, 0, None)})

    dall, dk_sb, dv_sb = _sb_bwd(qkv_sb, do_sb, sb_tot, sb_nblk, sb_w0, sb_b0, dgpre, c_sb)
    dall = lax.dynamic_update_slice(dall, dk_sb.astype(BF16), (0, c_sb + SB_W))
    dall = lax.dynamic_update_slice(dall, dv_sb.astype(BF16), (0, c_sb + 2 * SB_W))

    def dsa_mix_bwd(o0, o1, o2, l0, l1, l2, dov, bd):
        a = alphas(l0, l1, l2)
        omix = a[0] * o0 + a[1] * o1 + a[2] * o2
        dot_o = _head_sums(dov * omix, bd)
        return [dov * t for t in a] + [-t * dot_o for t in a]

    mixb = _tokmap("dsa_mix_bwd", dsa_mix_bwd, [*dsa_o, *dsa_lse, do_dsa], [bd256],
                   [(DSA_OUT_W, BF16)] * 3 + [(DSA_OUT_W, F32)] * 3, tile=256,
                   dil_ins={j: DSA_DILS[j % 3] for j in range(6)}, dil_outs={j: DSA_DILS[j % 3] for j in range(6)})
    dsa_d = [_dsa_bwd(dsa_q[g], dsa_k[g], dsa_v[g], mixb[g], mixb[3 + g], dsa_lse[g], DSA_DILS[g]) for g in range(3)]

    def dsa_prep_bwd(qkv, cs, sn, *rest):
        dqs, dks, dvs, (gq, gk, bd) = rest[0:3], rest[3:6], rest[6:9], rest[9:]
        mean = _mean_heads(bd)
        dq, dgq = _rms_bwd(qkv[:, :DSA_W].astype(F32), gq, _rope_bwd(jnp.concatenate(dqs, axis=1), cs, sn), mean)
        dk, dgk = _rms_bwd(qkv[:, DSA_W:2 * DSA_W].astype(F32), gk, _rope_bwd(jnp.concatenate(dks, axis=1), cs, sn), mean)
        return jnp.concatenate([dq, dk] + list(dvs), axis=1), dgq, dgk

    dall, dgq_dsa, dgk_dsa = _tokmap(
        "dsa_prep_bwd", dsa_prep_bwd,
        [qkv_dsa, cos, sin] + [dsa_d[g][t] for t in range(3) for g in range(3)], [gq_dsa, gk_dsa, bd768],
        [(3 * DSA_W, BF16)], [(1, DSA_W), (1, DSA_W)], tile=256, dil_ins={3 + j: DSA_DILS[j % 3] for j in range(9)},
        place={0: (c_all, c_dsa // (3 * DSA_W), dall)})
    gs["qn_dsa"] = dgq_dsa.reshape(DSA_W // HD, HD).sum(axis=0, keepdims=True)
    gs["kn_dsa"] = dgk_dsa.reshape(DSA_W // HD, HD).sum(axis=0, keepdims=True)

    def mem_attn_bwd(qv, dov, kmv, vmv, gq, bd):
        qn, ps = mem_probs(qv, kmv, gq, bd)
        dqn, dkm, dvm = [], [], []
        for h, p in enumerate(ps):
            sl = slice(HD * h, HD * h + HD)
            dp = _dot(dov[:, sl], vmv[:, sl], NT)
            ds = (p * (dp - jnp.sum(p * dp, axis=1, keepdims=True)) * SCALE).astype(BF16)
            dqn.append(_dot(ds, kmv[:, sl]))
            dkm.append(_dot(ds, qn[:, sl], TN))
            dvm.append(_dot(p.astype(BF16), dov[:, sl], TN))
        dq, dgq = _rms_bwd(qv.astype(F32), gq, jnp.concatenate(dqn, axis=1), _mean_heads(bd))
        return dq, jnp.concatenate(dkm, axis=1), jnp.concatenate(dvm, axis=1), dgq

    dall, dkm, dvm, dgq_mem = _tokmap("mem_attn_bwd", mem_attn_bwd, [q_mem, do_mem], [km, vm, gq_mem, bd256],
                                      [(MEM_W, BF16)], [(MEM_LEN, MEM_W), (MEM_LEN, MEM_W), (1, MEM_W)],
                                      place={0: (c_all, c_qm // MEM_W, dall)})
    gs["qn_mem"] = dgq_mem.reshape(MEM_W // HD, HD).sum(axis=0, keepdims=True)

    def mem_kv_bwd(memv, dkmv, dvmv, g, wkv, gk, bd):
        memn = _rms_fwd(memv, g, _mean_all).astype(BF16)
        kv = _dot(memn, wkv)
        dk, dgk = _rms_bwd(kv[:, :MEM_W], gk, dkmv, _mean_heads(bd))
        dkv = jnp.concatenate([dk, dvmv], axis=1).astype(BF16)
        _, dg = _rms_bwd(memv, g, _dot(dkv, wkv, NT), _mean_all)
        return _dot(memn, dkv, TN), dg, dgk

    gw["w_mem_kv"], gs["mem_norm"], dgk_mem = _tokmap(
        "mem_kv_bwd", mem_kv_bwd, [mem, dkm, dvm], [sm["mem_norm"], w["w_mem_kv"], gk_mem, bd256], [],
        [(D, 2 * MEM_W), (1, D), (1, MEM_W)])
    gs["kn_mem"] = dgk_mem.reshape(MEM_W // HD, HD).sum(axis=0, keepdims=True)

    dx1, gs["mix_norm"] = _matmul("proj_bwd_dx", dall, w_all, NT, F32, epi=_norm_bwd_epi, tiles=[x1, dx2],
                                  rows=[sm["mix_norm"]], n_sum=1)
    dw_all = _matmul("proj_bwd_dw", hmix, dall, TN, F32)
    gw["w_gate"], gw["w_in"] = dw_all[:, :c_sb], dw_all[:, c_sb:]
    on_grads("mid", {n: gw[n] for n in GROUPS["mid"]})
    gx, gs["ffn1_norm"], _, _ = _ffn_bwd("ffn1", x, sm["ffn1_norm"], w13_1, w["ffn1_w2"], ffn1_saved, dx1, plan,
                                         ffn_grads("ffn1"), final=True)
    return loss, gx, gw, gs


def _shard_shape(name):
    shape, axis = SHARDED_BY_NAME[name]
    return (shape[0] // N_CHIPS, shape[1]) if axis == 0 else (shape[0], shape[1] // N_CHIPS)


def _full_from_shards(name, shards):
    axis = SHARDED_BY_NAME[name][1]
    return shards.reshape(SHARDED_BY_NAME[name][0]) if axis == 0 else jnp.concatenate(list(shards), axis=1)


def _shards_from_full(name, full, dtype):
    axis, n = SHARDED_BY_NAME[name][1], _shard_shape(name)
    return jnp.stack([lax.slice_in_dim(full, c * n[axis], (c + 1) * n[axis], axis=axis).astype(dtype) for c in range(N_CHIPS)])


def _own_shard(name, full, chip):
    axis, n = SHARDED_BY_NAME[name][1], _shard_shape(name)
    return lax.dynamic_slice_in_dim(full, chip * n[axis], n[axis], axis=axis)


SMALL_USED = sum(n for _, n in SMALL)


def _pack_small(d, loss=None):
    parts = [d[n].reshape(-1) for n, _ in SMALL]
    parts.append(jnp.zeros((1,), F32) if loss is None else loss.reshape(1))
    parts.append(jnp.zeros((SMALL_ROWS * D - SMALL_USED - 1,), F32))
    return jnp.concatenate(parts).reshape(SMALL_ROWS, D)


def _unpack_small(v):
    flat, out, r = v.reshape(-1), {}, 0
    for n, k in SMALL:
        out[n] = flat[r:r + k]
        r += k
    return out, flat[r]


def _place():
    return lax.axis_index("x"), lax.axis_index("y"), lax.axis_index("c")


def _other_chips(x, y):
    return [(1 - x, y), (x, 1 - y), (1 - x, 1 - y)]


HBM_SPEC = pl.BlockSpec(memory_space=pl.ANY)


def _chip_sems(n):
    return (pltpu.SemaphoreType.DMA((3 * n,)), pltpu.SemaphoreType.DMA((3 * n,)), pltpu.SemaphoreType.DMA((n,)))


def _gather_copies(ins, outs, send_sems, recv_sems, local_sems):
    x, y, c = _place()
    me = 2 * x + y
    copies = []
    for a, (src, out) in enumerate(zip(ins, outs)):
        copies.append(pltpu.make_async_copy(src, out.at[me], local_sems.at[a]))
        copies += [pltpu.make_async_remote_copy(src_ref=src, dst_ref=out.at[me], send_sem=send_sems.at[3 * a + k],
                                                recv_sem=recv_sems.at[3 * a + k], device_id=(px, py, c), device_id_type=MESH)
                   for k, (px, py) in enumerate(_other_chips(x, y))]
    return copies


def _scatter_copies(ins, outs, send_sems, recv_sems, local_sems):
    x, y, c = _place()
    return [pltpu.make_async_remote_copy(src_ref=src.at[2 * px + py], dst_ref=out.at[k], send_sem=send_sems.at[3 * a + k],
                                         recv_sem=recv_sems.at[3 * a + k], device_id=(px, py, c), device_id_type=MESH)
            for a, (src, out) in enumerate(zip(ins, outs)) for k, (px, py) in enumerate(_other_chips(x, y))]


def _all_gather_chips(arrays):
    n = len(arrays)

    def body(*refs):
        ins, outs = refs[:n], refs[n:2 * n]
        send1, recv1, send2, recv2, local_sems = refs[2 * n:]
        x, y, c = _place()
        me = 2 * x + y
        chips = _other_chips(x, y)
        local = [pltpu.make_async_copy(src, out.at[me], local_sems.at[a]) for a, (src, out) in enumerate(zip(ins, outs))]
        for cp in local:
            cp.start()

        def half(ref, chip, which):
            rows = ref.shape[-2] // 2
            return ref.at[chip, pl.ds(which * rows, rows)] if chip is not None else ref.at[pl.ds(which * rows, rows)]

        first = [pltpu.make_async_remote_copy(src_ref=half(src, None, c), dst_ref=half(out, me, c), send_sem=send1.at[3 * a + k],
                                              recv_sem=recv1.at[3 * a + k], device_id=(px, py, c), device_id_type=MESH)
                 for a, (src, out) in enumerate(zip(ins, outs)) for k, (px, py) in enumerate(chips)]
        for cp in first:
            cp.start()
        passed = []
        for a, out in enumerate(outs):
            for k, (px, py) in enumerate(chips):
                pltpu.make_async_remote_copy(src_ref=half(out, 2 * px + py, c), dst_ref=half(out, 2 * px + py, c),
                                             send_sem=send1.at[3 * a + k], recv_sem=recv1.at[3 * a + k],
                                             device_id=(px, py, c), device_id_type=MESH).wait_recv()
                cp = pltpu.make_async_remote_copy(src_ref=half(out, 2 * px + py, c), dst_ref=half(out, 2 * px + py, c),
                                                  send_sem=send2.at[3 * a + k], recv_sem=recv2.at[3 * a + k],
                                                  device_id=(x, y, 1 - c), device_id_type=MESH)
                cp.start()
                passed.append(cp)
        for a, out in enumerate(outs):
            for k, (px, py) in enumerate(chips):
                pltpu.make_async_remote_copy(src_ref=half(out, 2 * px + py, 1 - c), dst_ref=half(out, 2 * px + py, 1 - c),
                                             send_sem=send2.at[3 * a + k], recv_sem=recv2.at[3 * a + k],
                                             device_id=(x, y, 1 - c), device_id_type=MESH).wait_recv()
        for cp in first + passed:
            cp.wait_send()
        for cp in local:
            cp.wait()

    sems = pltpu.SemaphoreType.DMA((3 * n,))
    return pl.pallas_call(
        body, name="weights_all_gather", in_specs=[HBM_SPEC] * n, out_specs=[HBM_SPEC] * n,
        out_shape=[SDS((N_CHIPS,) + a.shape, a.dtype) for a in arrays],
        scratch_shapes=[sems, sems, sems, sems, pltpu.SemaphoreType.DMA((n,))],
    )(*arrays)


def _swap_with_sibling(name, arrays):
    n = len(arrays)

    def body(*refs):
        x, y, c = _place()
        send_sems, recv_sems = refs[2 * n:]
        copies = [pltpu.make_async_remote_copy(src_ref=refs[a], dst_ref=refs[n + a], send_sem=send_sems.at[a],
                                               recv_sem=recv_sems.at[a], device_id=(x, y, 1 - c), device_id_type=MESH)
                  for a in range(n)]
        for cp in copies:
            cp.start()
        for cp in copies:
            cp.wait()

    return pl.pallas_call(
        body, name=name, in_specs=[HBM_SPEC] * n, out_specs=[HBM_SPEC] * n, out_shape=[SDS(a.shape, a.dtype) for a in arrays],
        scratch_shapes=[pltpu.SemaphoreType.DMA((n,)), pltpu.SemaphoreType.DMA((n,))],
    )(*arrays)


def _all_reduce_small(v):
    n_dev = 8

    def body(v_ref, out_ref, land, send_sems, recv_sems):
        x, y, c = _place()
        me = 4 * x + 2 * y + c
        land[me] = v_ref[...]
        copies = []
        for k in range(1, n_dev):
            peer = (x ^ (k >> 2), y ^ ((k >> 1) & 1), c ^ (k & 1))
            copies.append(pltpu.make_async_remote_copy(src_ref=v_ref, dst_ref=land.at[me], send_sem=send_sems.at[k - 1],
                                                       recv_sem=recv_sems.at[k - 1], device_id=peer, device_id_type=MESH))
        for cp in copies:
            cp.start()
        for cp in copies:
            cp.wait()
        acc = land[0]
        for d in range(1, n_dev):
            acc = acc + land[d]
        out_ref[...] = acc

    return pl.pallas_call(
        body, name="small_all_reduce", in_specs=[pl.BlockSpec(memory_space=pltpu.VMEM)],
        out_specs=pl.BlockSpec(memory_space=pltpu.VMEM), out_shape=SDS(v.shape, v.dtype),
        scratch_shapes=[pltpu.VMEM((n_dev,) + v.shape, v.dtype), pltpu.SemaphoreType.DMA((n_dev - 1,)),
                        pltpu.SemaphoreType.DMA((n_dev - 1,))],
    )(v)


def _adamw(g, wv, m, v):
    m = ADAM_B1 * m + (1.0 - ADAM_B1) * g
    v = ADAM_B2 * v + (1.0 - ADAM_B2) * (g * g)
    m_hat = m / (1.0 - ADAM_B1 ** ADAM_STEP)
    v_hat = v / (1.0 - ADAM_B2 ** ADAM_STEP)
    delta = -ADAM_LR * (m_hat / (jnp.sqrt(v_hat) + ADAM_EPS) + ADAM_WD * wv)
    return delta, m, v


def kernel(x, mem, ffn1_norm, ffn1_w1, ffn1_w3, ffn1_w2, mix_norm, mem_norm, w_in, w_mem_kv, qn_dsa, kn_dsa, qn_mem, kn_mem, w_branch_sb, w_branch_dsa, w_branch_mem, w_gate, b_gate, w_out, ffn2_norm, ffn2_w1, ffn2_w3, ffn2_w2, loss_target, m_ffn1_norm, m_ffn1_w1, m_ffn1_w3, m_ffn1_w2, m_mix_norm, m_mem_norm, m_w_in, m_w_mem_kv, m_qn_dsa, m_kn_dsa, m_qn_mem, m_kn_mem, m_w_branch_sb, m_w_branch_dsa, m_w_branch_mem, m_w_gate, m_b_gate, m_w_out, m_ffn2_norm, m_ffn2_w1, m_ffn2_w3, m_ffn2_w2, v_ffn1_norm, v_ffn1_w1, v_ffn1_w3, v_ffn1_w2, v_mix_norm, v_mem_norm, v_w_in, v_w_mem_kv, v_qn_dsa, v_kn_dsa, v_qn_mem, v_kn_mem, v_w_branch_sb, v_w_branch_dsa, v_w_branch_mem, v_w_gate, v_b_gate, v_w_out, v_ffn2_norm, v_ffn2_w1, v_ffn2_w3, v_ffn2_w2):
    given = dict(locals())
    wts = {n: given[n][0] for n in WEIGHTS}
    moms = {n: given["m_" + n][0] for n in WEIGHTS}
    vars_ = {n: given["v_" + n][0] for n in WEIGHTS}

    plan = _Plan()
    x_i, y_i, _ = _place()
    my_chip = 2 * x_i + y_i

    full = {}

    def gathered(names):
        return lambda res: full.update({n: _full_from_shards(n, g) for n, g in zip(names, res)})

    for host, names in WEIGHT_PIECES:
        shards = [wts[n].astype(BF16) for n in names]
        if host is None:
            gathered(names)(_all_gather_chips(shards))
        else:
            plan.put(host, _Carry(shards, [SDS((N_CHIPS,) + a.shape, BF16) for a in shards], _chip_sems(len(names)),
                                  _gather_copies, gathered(names)))
    small = {n: wts[n].reshape(1, -1) for n, _ in SMALL}

    landed = {}

    def on_grads(group, grads):
        names = GROUPS[group]
        slices = [_shards_from_full(n, grads[n], BF16) for n in names]
        own = [_own_shard(n, grads[n], my_chip) for n in names]
        plan.put(GRAD_HOSTS[group], _Carry(slices, [SDS((3,) + a.shape[1:], BF16) for a in slices], _chip_sems(len(names)),
                                           _scatter_copies, lambda res: landed.update({group: (own, res)})))

    loss, gx, _, gs = _local_step(x[0], mem[0], loss_target[0], full, small, plan, on_grads)
    assert not plan.pending, list(plan.pending)

    def update(hv, ov, wv, mv, vv):
        g = hv + ov
        return (g,) + _adamw(g, wv, mv, vv)

    outs = [{}, {}, {}, {}]
    for group, names in GROUPS.items():
        own, got = landed[group]
        halves = [_tokmap(f"grads_sum_chips_{n}",
                          lambda a, b0, b1, b2: ((a + b0.astype(F32)) + b1.astype(F32)) + b2.astype(F32),
                          [o, g[0], g[1], g[2]], [], [(o.shape[1], F32)])[0] for n, o, g in zip(names, own, got)]
        others = _swap_with_sibling(f"grads_swap_cores_{group}", halves)
        for n, half, other in zip(names, halves, others):
            res = _tokmap(f"adamw_{n}", update, [half, other, wts[n], moms[n], vars_[n]], [], [(half.shape[1], F32)] * 4)
            for d, r in zip(outs, res):
                d[n] = r

    s_red = _all_reduce_small(_pack_small(gs, loss[0, 0]))
    res = _tokmap(
        "adamw_small", lambda g, wv, mv, vv: (g,) + _adamw(g, wv, mv, vv),
        [s_red, _pack_small(small), _pack_small({n: moms[n] for n, _ in SMALL}), _pack_small({n: vars_[n] for n, _ in SMALL})],
        [], [(D, F32)] * 4)
    for d, packed in zip(outs, res):
        d.update(_unpack_small(packed)[0])
    _, total_loss = _unpack_small(s_red)
    return (total_loss, gx[None], *[d[n][None] for d in outs for n in WEIGHTS])
```

```python
import functools

import numpy as np
import jax
import jax.numpy as jnp
from jax import lax
from jax.experimental import pallas as pl
from jax.experimental.pallas import tpu as pltpu

F32, BF16 = jnp.float32, jnp.bfloat16
SDS = jax.ShapeDtypeStruct
MESH = pl.DeviceIdType.MESH

D = 1024
HD = 64
QB = 128
DSA_T_FWD, DSA_T_BWD = 512, 256
D_FF = 2816
SB_W, DSA_W, DSA_OUT_W, MEM_W = 512, 768, 256, 256
DSA_DILS = (1, 4, 16)
MEM_LEN = 256
N_CHIPS = 4
EPS = 1e-6
SCALE = HD ** -0.5
EXHAUSTED = -104.0
SB_FWD_HEADS = 4
SB_QB = 256
SB_WIN = 512
NEG = -1e30
VMEM_LIMIT = 56 * 1024 * 1024

ADAM_LR, ADAM_B1, ADAM_B2, ADAM_EPS, ADAM_WD, ADAM_STEP = 0.001, 0.9, 0.999, 1e-08, 0.01, 10

NN = (((1,), (0,)), ((), ()))
NT = (((1,), (1,)), ((), ()))
TN = (((0,), (0,)), ((), ()))

SHARDED = (
    ("ffn1_w1", (D, D_FF), 1), ("ffn1_w3", (D, D_FF), 1), ("ffn1_w2", (D_FF, D), 0),
    ("w_in", (D, 4096), 1), ("w_mem_kv", (D, 512), 0),
    ("w_branch_sb", (SB_W, D), 1), ("w_branch_dsa", (DSA_OUT_W, D), 1), ("w_branch_mem", (MEM_W, D), 1),
    ("w_gate", (D, 3 * D), 1), ("w_out", (D, D), 0),
    ("ffn2_w1", (D, D_FF), 1), ("ffn2_w3", (D, D_FF), 1), ("ffn2_w2", (D_FF, D), 0),
)
SHARDED_BY_NAME = {n: (sh, ax) for n, sh, ax in SHARDED}
GROUPS = {
    "ffn2": ("ffn2_w1", "ffn2_w3", "ffn2_w2"),
    "mid": ("w_in", "w_mem_kv", "w_branch_sb", "w_branch_dsa", "w_branch_mem", "w_gate", "w_out"),
    "ffn1": ("ffn1_w1", "ffn1_w3", "ffn1_w2"),
}
WEIGHT_PIECES = (
    (None, ("ffn1_w1", "ffn1_w3")),
    ("ffn1_up", ("ffn1_w2", "w_in")),
    ("ffn1_down", ("w_gate", "w_mem_kv", "w_branch_sb", "w_branch_dsa", "w_branch_mem", "w_out")),
    ("proj_dsa", ("ffn2_w2",)),
    ("proj_gate", ("ffn2_w1", "ffn2_w3")),
)
GRAD_HOSTS = {"ffn2": "ffn2_bwd_dn", "mid": "ffn1_bwd_dw13", "ffn1": "ffn1_bwd_dn"}
SMALL = (("ffn1_norm", D), ("mix_norm", D), ("mem_norm", D), ("ffn2_norm", D), ("b_gate", 3 * D),
         ("qn_dsa", HD), ("kn_dsa", HD), ("qn_mem", HD), ("kn_mem", HD))
WEIGHTS = ("ffn1_norm", "ffn1_w1", "ffn1_w3", "ffn1_w2", "mix_norm", "mem_norm", "w_in", "w_mem_kv", "qn_dsa", "kn_dsa",
           "qn_mem", "kn_mem", "w_branch_sb", "w_branch_dsa", "w_branch_mem", "w_gate", "b_gate", "w_out", "ffn2_norm",
           "ffn2_w1", "ffn2_w3", "ffn2_w2")
SMALL_ROWS = 8


def _dot(a, b, dn=NN):
    return lax.dot_general(a, b, dn, preferred_element_type=F32)


def _dot01(x, m01, pieces=3):
    hi = x.astype(BF16)
    r1 = x - hi.astype(F32)
    mid = r1.astype(BF16)
    if pieces == 2:
        return _dot(hi, m01) + _dot(mid, m01)
    lo = (r1 - mid.astype(F32)).astype(BF16)
    return _dot(hi, m01) + _dot(mid, m01) + _dot(lo, m01)


def _pick(n, cands):
    for c in cands:
        if n % c == 0:
            return c
    raise ValueError(f"no tile for {n}")


def _from_dilated(v, d, scr):
    w = v.shape[1] // d
    v = v.astype(F32)
    for c in range(d):
        for p, buf in enumerate(scr[:w // 128]):
            buf[pl.ds(c, v.shape[0], stride=d), :] = v[:, c * w + 128 * p:c * w + 128 * (p + 1)]
    return jnp.concatenate([buf[...] for buf in scr[:w // 128]], axis=1)


def _to_dilated(v, d, scr):
    w = v.shape[1]
    for p, buf in enumerate(scr[:w // 128]):
        buf[...] = v[:, 128 * p:128 * (p + 1)].astype(F32)
    return jnp.concatenate([buf[pl.ds(c, v.shape[0] // d, stride=d), :] for c in range(d) for buf in scr[:w // 128]], axis=1)


def _tokmap(name, fn, tok_ins, consts, tok_outs, acc_outs=(), tile=512, dil_ins=None, dil_outs=None, place=None):
    dil_ins, dil_outs, place = dil_ins or {}, dil_outs or {}, place or {}
    bufs = [(j, buf) for j, (_, _, buf) in place.items() if buf is not None]
    n_buf = len(bufs)
    n = tok_ins[0].shape[0] * dil_ins.get(0, 1)
    tile = _pick(n, [t for t in (512, 352, 256, 128, 64, 32, 16, 8) if t <= tile])
    n_tin, n_in, n_tok, n_acc = len(tok_ins), len(tok_ins) + len(consts), len(tok_outs), len(acc_outs)
    n_scr = max([tok_ins[j].shape[1] // d // 128 for j, d in dil_ins.items() if d > 1]
                + [tok_outs[j][0] // 128 for j, d in dil_outs.items() if d > 1] + [0])

    def body(*refs):
        scr = refs[len(refs) - n_scr:]
        vals = [r[...] for r in refs[:n_in]]
        for j, d in dil_ins.items():
            if d > 1:
                vals[j] = _from_dilated(vals[j], d, scr)
        outs = fn(*vals)
        outs = list(outs) if isinstance(outs, (tuple, list)) else [outs]
        assert len(outs) == n_tok + n_acc, (name, len(outs))
        for j, d in dil_outs.items():
            if d > 1:
                outs[j] = _to_dilated(outs[j], d, scr)
        orefs = refs[n_in + n_buf:]
        for r, v in zip(orefs[:n_tok], outs[:n_tok]):
            r[...] = v.astype(r.dtype)
        if n_acc:
            @pl.when(pl.program_id(0) == 0)
            def _():
                for r in orefs[n_tok:n_tok + n_acc]:
                    r[...] = jnp.zeros(r.shape, r.dtype)
            for r, v in zip(orefs[n_tok:n_tok + n_acc], outs[n_tok:]):
                r[...] += v.astype(r.dtype)

    def tok_spec(width, d):
        return pl.BlockSpec((tile // d, d * width), lambda i: (i, 0))

    in_specs = [tok_spec(a.shape[1] // dil_ins.get(j, 1), dil_ins.get(j, 1)) for j, a in enumerate(tok_ins)]
    in_specs += [pl.BlockSpec(c.shape, lambda i: (0, 0)) for c in consts]
    in_specs += [HBM_SPEC] * n_buf
    out_specs = [tok_spec(w, dil_outs.get(j, 1)) for j, (w, _) in enumerate(tok_outs)]
    out_shape = [SDS((n // dil_outs.get(j, 1), w * dil_outs.get(j, 1)), dt) for j, (w, dt) in enumerate(tok_outs)]
    for j, (total, col_block, _) in place.items():
        out_specs[j] = pl.BlockSpec((tile, tok_outs[j][0]), lambda i, cb=col_block: (i, cb))
        out_shape[j] = SDS((n, total), tok_outs[j][1])
    out_specs += [pl.BlockSpec(s, lambda i: (0, 0)) for s in acc_outs]
    out_shape += [SDS(s, F32) for s in acc_outs]
    res = pl.pallas_call(
        body, name=name, grid=(n // tile,), in_specs=in_specs, out_specs=out_specs, out_shape=out_shape,
        scratch_shapes=[pltpu.VMEM((tile, 128), F32)] * n_scr,
        input_output_aliases={n_in + b: j for b, (j, _) in enumerate(bufs)},
        compiler_params=pltpu.CompilerParams(dimension_semantics=("arbitrary",), vmem_limit_bytes=VMEM_LIMIT),
    )(*tok_ins, *consts, *[buf for _, buf in bufs])
    return res


MATMUL_VMEM_BUDGET = 40 * 1024 * 1024


def _matmul_tiles(m, n, k, a_bytes, b_bytes, o_bytes, extra_bytes, whole_n=False):
    best = None
    for tk in [c for c in (3584, 2816, 2048, 1408, 1024, 512, 256, 128) if k % c == 0]:
        for tm in [c for c in (1408, 1024, 768, 512, 256, 128) if m % c == 0]:
            for tn in [n] if whole_n else [c for c in (1408, 1024, 768, 512, 256, 128) if n % c == 0]:
                need = 2 * tk * (tm * a_bytes + tn * b_bytes) + tm * tn * (2 * o_bytes + 2 * extra_bytes + 8)
                if need > MATMUL_VMEM_BUDGET:
                    continue
                score = (min(tm, 512) * min(tn, 512), tk, tm * tn, tn)
                if best is None or score > best[0]:
                    best = (score, (tm, tn, tk))
    return best[1]


class _Carry:
    def __init__(self, ins, outs, sems, copies, then):
        self.ins, self.outs, self.sems, self.copies, self.then = ins, outs, sems, copies, then


class _Plan:
    def __init__(self):
        self.pending = {}

    def put(self, host, carry):
        assert host not in self.pending, host
        self.pending[host] = carry

    def take(self, host):
        return self.pending.pop(host, None)


def _matmul(name, a, b, dn, out_dtype, epi=None, tiles=(), rows=(), plan=None, a_pro=None, n_sum=0):
    if dn == NN:
        (m, k), n = a.shape, b.shape[1]
    elif dn == NT:
        (m, k), n = a.shape, b.shape[0]
    else:
        (k, m), n = a.shape, b.shape[1]
    n_t, n_r = len(tiles), len(rows)
    pro, n_parts = a_pro if a_pro is not None else (None, 1)
    if pro is not None:
        assert dn == NN and n == _pick(n, (1024, 512))
        k //= n_parts
    if pro is not None:
        tm, tn, tk = _pick(m, (256, 128)), n, k
    else:
        tm, tn, tk = _matmul_tiles(m, n, k, a.dtype.itemsize, b.dtype.itemsize, jnp.dtype(out_dtype).itemsize,
                                   sum(t.dtype.itemsize for t in tiles), whole_n=n_sum > 0)
    nk = k // tk
    grid = (m // tm, n // tn, nk)
    assert pro is None or grid[1] == 1
    assert n_sum == 0 or grid[1] == 1
    carry = plan.take(name) if plan is not None else None
    n_ci, n_co = (len(carry.ins), len(carry.outs)) if carry else (0, 0)
    n_keep = 1 if pro is not None else 0

    def body(*refs):
        a_refs, b_ref, rest = refs[:n_parts], refs[n_parts], refs[n_parts + 1:]
        extras, rest = rest[:n_t + n_r], rest[n_t + n_r:]
        c_in, o_ref, rest = rest[:n_ci], rest[n_ci], rest[n_ci + 1:]
        sums, rest = rest[:n_sum], rest[n_sum:]
        keep, c_out, scratch = rest[:n_keep], rest[n_keep:n_keep + n_co], rest[n_keep + n_co:]
        ids = [pl.program_id(d) for d in range(3)]
        if n_sum:
            @pl.when((ids[0] == 0) & (ids[2] == 0))
            def _():
                for r in sums:
                    r[...] = jnp.zeros(r.shape, F32)
        if carry:
            sems = scratch[1:] if nk > 1 else scratch

            @pl.when((ids[0] == 0) & (ids[1] == 0) & (ids[2] == 0))
            def _():
                for cp in carry.copies(c_in, c_out, *sems):
                    cp.start()

        if pro is not None:
            av = pro(*[r[...] for r in a_refs])
            keep[0][...] = av
        else:
            av = a_refs[0][...].astype(BF16)
        part = _dot(av, b_ref[...].astype(BF16), dn)

        def finish(r):
            if epi is not None:
                r = epi(r, *[e[...] for e in extras])
            if n_sum:
                for ref, v in zip(sums, r[1:]):
                    ref[...] += v
                r = r[0]
            o_ref[...] = r.astype(o_ref.dtype)

        if nk == 1:
            finish(part)
        else:
            acc = scratch[0]

            @pl.when(ids[2] == 0)
            def _():
                acc[...] = part

            @pl.when(ids[2] > 0)
            def _():
                acc[...] += part

            @pl.when(ids[2] == nk - 1)
            def _():
                finish(acc[...])

        if carry:
            @pl.when((ids[0] == grid[0] - 1) & (ids[1] == grid[1] - 1) & (ids[2] == nk - 1))
            def _():
                for cp in carry.copies(c_in, c_out, *sems):
                    cp.wait()

    if dn == TN:
        a_specs = [pl.BlockSpec((tk, tm), lambda i, j, kk: (kk, i))]
    else:
        a_specs = [pl.BlockSpec((tm, tk), lambda i, j, kk, p=p: (i, kk + p * nk)) for p in range(n_parts)]
    b_spec = pl.BlockSpec((tn, tk), lambda i, j, kk: (j, kk)) if dn == NT else pl.BlockSpec((tk, tn), lambda i, j, kk: (kk, j))
    in_specs = a_specs + [b_spec] + [pl.BlockSpec((tm, tn), lambda i, j, kk: (i, j)) for _ in tiles]
    in_specs += [pl.BlockSpec((1, tn), lambda i, j, kk: (0, j)) for _ in rows] + [HBM_SPEC] * n_ci
    res = pl.pallas_call(
        body, name=name, grid=grid, in_specs=in_specs,
        out_specs=[pl.BlockSpec((tm, tn), lambda i, j, kk: (i, j))] + [pl.BlockSpec((1, tn), lambda i, j, kk: (0, 0))] * n_sum
        + [pl.BlockSpec((tm, tk), lambda i, j, kk: (i, kk))] * n_keep + [HBM_SPEC] * n_co,
        out_shape=[SDS((m, n), out_dtype)] + [SDS((1, n), F32)] * n_sum + [SDS((m, k), BF16)] * n_keep
        + (list(carry.outs) if carry else []),
        scratch_shapes=([pltpu.VMEM((tm, tn), F32)] if nk > 1 else []) + (list(carry.sems) if carry else []),
        compiler_params=pltpu.CompilerParams(
            dimension_semantics=("arbitrary",) * 3 if (carry or n_sum) else ("parallel", "parallel", "arbitrary"),
            vmem_limit_bytes=VMEM_LIMIT),
    )(*[a] * n_parts, b, *tiles, *rows, *(carry.ins if carry else []))
    if carry:
        carry.then(res[1 + n_sum + n_keep:])
    return tuple(res[:1 + n_sum + n_keep]) if n_sum + n_keep else res[0]


def _mean_all(v):
    return jnp.mean(v, axis=-1, keepdims=True)


def _head_sums(v, bd):
    w = bd.shape[0]
    return jnp.concatenate([_dot01(v[:, j:j + w], bd, 2) for j in range(0, v.shape[1], w)], axis=1)


def _mean_heads(bd):
    return lambda v: _head_sums(v, bd) * (1.0 / HD)


def _rms_fwd(x, g, mean):
    return x * lax.rsqrt(mean(x * x) + EPS) * g


def _rms_bwd(x, g, dy, mean):
    r = lax.rsqrt(mean(x * x) + EPS)
    dn = dy * g
    dx = r * dn - x * (r * r * r) * mean(dn * x)
    return dx, jnp.sum(dy * x * r, axis=0, keepdims=True)


def _swap_halves(x):
    w = x.shape[1]
    lane = lax.broadcasted_iota(jnp.int32, x.shape, 1)
    return jnp.where(lane % HD < HD // 2, pltpu.roll(x, w - HD // 2, 1), pltpu.roll(x, HD // 2, 1))


def _lanes(t, w):
    return jnp.tile(t, (1, w // t.shape[1]))


def _rope_fwd(x, cos, sin_signed):
    return x * _lanes(cos, x.shape[1]) + _swap_halves(x) * _lanes(sin_signed, x.shape[1])


def _rope_bwd(dy, cos, sin_signed):
    return dy * _lanes(cos, dy.shape[1]) + _swap_halves(dy * _lanes(sin_signed, dy.shape[1]))


def _bcast_heads(cols):
    return jnp.concatenate([jnp.broadcast_to(c, (c.shape[0], HD)) for c in cols], axis=1)


def _softplus(z):
    return jnp.maximum(z, 0.0) + jnp.log(1.0 + jnp.exp(-jnp.abs(z)))


def _block_diag(w):
    h = np.arange(w) // HD
    return jnp.asarray(h[:, None] == h[None, :], BF16)


def _sb_window(i, t):
    hi = (i + 1) * SB_QB - t * SB_WIN
    lo = hi - SB_WIN
    ws = pl.multiple_of(jnp.maximum(lo, 0), SB_QB)
    kpos = ws + lax.broadcasted_iota(jnp.int32, (SB_QB, SB_WIN), 1)
    qpos = i * SB_QB + lax.broadcasted_iota(jnp.int32, (SB_QB, SB_WIN), 0)
    return (kpos < qpos) & (kpos >= lo) & (kpos < hi), ws


def _sb_fwd(qkv):
    s = qkv.shape[0]
    assert s >= SB_WIN
    nq = s // SB_QB
    nh = SB_FWD_HEADS
    bw = HD * nh
    ngroups = SB_W // bw

    def body(q_ref, k_ref, v_ref, later_ref, o_ref, tot_ref, w0_ref, b0_ref, nb_ref):
        p, i = pl.program_id(0), pl.program_id(1)
        q = q_ref[...]
        later_of = later_ref[...]

        def window(t, tots, outs, keep):
            mask, ws = _sb_window(i, t)
            kw, vw = k_ref[pl.ds(ws, SB_WIN), :], v_ref[pl.ds(ws, SB_WIN), :]
            new_t, new_o, w_all, b_all = [], [], [], []
            for hh in range(nh):
                sl = slice(HD * hh, HD * hh + HD)
                z = _dot(q[:, sl], kw[:, sl], NT) * SCALE
                sp = _softplus(z)
                lf = jnp.where(mask, -sp, 0.0)
                later = tots[hh] + _dot01(lf, later_of, 2)
                w = jnp.where(mask, jnp.exp(z - sp + later), 0.0).astype(BF16)
                new_o.append(outs[hh] + _dot(w, vw[:, sl]))
                new_t.append(tots[hh] + jnp.sum(lf, axis=1, keepdims=True))
                if keep:
                    w_all.append(w)
                    b_all.append(jnp.where(mask, jnp.exp(z - sp), 0.0).astype(BF16))
            if keep:
                w0_ref[...] = jnp.concatenate(w_all, axis=1)
                b0_ref[...] = jnp.concatenate(b_all, axis=1)
            alive = functools.reduce(jnp.maximum, [jnp.max(v) for v in new_t])
            return t + 1, alive, tuple(new_t), tuple(new_o)

        zt, zo = jnp.zeros((SB_QB, 1), F32), jnp.zeros((SB_QB, HD), F32)
        first = window(jnp.int32(0), (zt,) * nh, (zo,) * nh, True)
        t, _, tots, outs = lax.while_loop(lambda c: ((i + 1) * SB_QB - c[0] * SB_WIN > 0) & (c[1] > EXHAUSTED),
                                          lambda c: window(c[0], c[2], c[3], False), first)
        o_ref[...] = jnp.concatenate(outs, axis=1).astype(o_ref.dtype)
        tot_ref[...] = _bcast_heads(tots)
        nb_ref[p, i] = t

    whole = lambda off: pl.BlockSpec((s, bw), lambda p, i: (0, off + p), pipeline_mode=pl.Buffered(1))
    tile = pl.BlockSpec((SB_QB, bw), lambda p, i: (i, p))
    tri = pl.BlockSpec((SB_WIN, SB_WIN), lambda p, i: (0, 0), pipeline_mode=pl.Buffered(1))
    near = pl.BlockSpec((SB_QB, nh * SB_WIN), lambda p, i: (i, p))
    n_heads = SB_W // HD
    idx = np.arange(SB_WIN)
    return pl.pallas_call(
        body, name="sb_fwd", grid=(ngroups, nq),
        in_specs=[tile, whole(ngroups), whole(2 * ngroups), tri],
        out_specs=[tile, tile, near, near, pl.BlockSpec(memory_space=pltpu.SMEM)],
        out_shape=[SDS((s, SB_W), BF16), SDS((s, SB_W), F32), SDS((s, n_heads * SB_WIN), BF16), SDS((s, n_heads * SB_WIN), BF16),
                   SDS((ngroups, nq), jnp.int32)],
        compiler_params=pltpu.CompilerParams(dimension_semantics=("arbitrary", "arbitrary"), vmem_limit_bytes=VMEM_LIMIT),
    )(qkv, qkv, qkv, jnp.asarray(idx[:, None] > idx[None, :], BF16))


def _sb_bwd(qkv, do, tot, nblk, w0, b0, buf, col):
    s = qkv.shape[0]
    nq = s // SB_QB
    npairs = SB_W // 128

    def body(nb_ref, q_ref, k_ref, v_ref, do_ref, tot_ref, upto_ref, before_ref, w0_ref, b0_ref, buf_ref,
             dq_ref, dk_ref, dv_ref):
        p, i = pl.program_id(0), pl.program_id(1)

        @pl.when(i == 0)
        def _():
            dk_ref[...] = jnp.zeros(dk_ref.shape, F32)
            dv_ref[...] = jnp.zeros(dv_ref.shape, F32)

        upto = upto_ref[...]
        before = before_ref[...]
        q, dout, tt = q_ref[...], do_ref[...], tot_ref[...]
        n = nb_ref[p * 2 // SB_FWD_HEADS, i]

        def step(it, c):
            pres, gpres, dqs = c
            mask, ws = _sb_window(i, n - 1 - it)
            kw, vw = k_ref[pl.ds(ws, SB_WIN), :], v_ref[pl.ds(ws, SB_WIN), :]
            new_p, new_g, new_dq, dks, dvs = [], [], [], [], []
            for hh in range(2):
                sl = slice(HD * hh, HD * hh + HD)
                z = _dot(q[:, sl], kw[:, sl], NT) * SCALE
                sp = _softplus(z)
                lf = jnp.where(mask, -sp, 0.0)
                later = tt[:, HD * hh:HD * hh + 1] - (pres[hh] + _dot01(lf, upto, 2))
                w = jnp.where(mask, jnp.exp(z - sp + later), 0.0)
                beta = jnp.exp(z - sp)
                g = _dot(dout[:, sl], vw[:, sl], NT) * w
                g_far = gpres[hh] + _dot(g.astype(BF16), before)
                dz = (jnp.where(mask, g * (1.0 - beta) - beta * g_far, 0.0) * SCALE).astype(BF16)
                new_dq.append(dqs[hh] + _dot(dz, kw[:, sl]))
                dks.append(_dot(dz, q[:, sl], TN))
                dvs.append(_dot(w.astype(BF16), dout[:, sl], TN))
                new_p.append(pres[hh] + jnp.sum(lf, axis=1, keepdims=True))
                new_g.append(gpres[hh] + jnp.sum(g, axis=1, keepdims=True))
            dk_ref[pl.ds(ws, SB_WIN), :] += jnp.concatenate(dks, axis=1)
            dv_ref[pl.ds(ws, SB_WIN), :] += jnp.concatenate(dvs, axis=1)
            return tuple(new_p), tuple(new_g), tuple(new_dq)

        zt, zo = jnp.zeros((SB_QB, 1), F32), jnp.zeros((SB_QB, HD), F32)
        _, gpres, dqs = lax.fori_loop(0, n - 1, step, ((zt, zt), (zt, zt), (zo, zo)))
        _, ws = _sb_window(i, 0)
        kw, vw = k_ref[pl.ds(ws, SB_WIN), :], v_ref[pl.ds(ws, SB_WIN), :]
        dqs, dks, dvs = list(dqs), [], []
        for hh in range(2):
            sl = slice(HD * hh, HD * hh + HD)
            w = w0_ref[:, SB_WIN * hh:SB_WIN * (hh + 1)]
            beta = b0_ref[:, SB_WIN * hh:SB_WIN * (hh + 1)].astype(F32)
            g = _dot(dout[:, sl], vw[:, sl], NT) * w.astype(F32)
            g_far = gpres[hh] + _dot(g.astype(BF16), before)
            dz = ((g * (1.0 - beta) - beta * g_far) * SCALE).astype(BF16)
            dqs[hh] = dqs[hh] + _dot(dz, kw[:, sl])
            dks.append(_dot(dz, q[:, sl], TN))
            dvs.append(_dot(w, dout[:, sl], TN))
        dk_ref[pl.ds(ws, SB_WIN), :] += jnp.concatenate(dks, axis=1)
        dv_ref[pl.ds(ws, SB_WIN), :] += jnp.concatenate(dvs, axis=1)
        dq_ref[...] = jnp.concatenate(dqs, axis=1).astype(dq_ref.dtype)

    whole_in = lambda off: pl.BlockSpec((s, 128), lambda p, i: (0, off + p), pipeline_mode=pl.Buffered(1))
    whole_out = pl.BlockSpec((s, 128), lambda p, i: (0, p), pipeline_mode=pl.Buffered(1))
    tile = pl.BlockSpec((SB_QB, 128), lambda p, i: (i, p))
    near = pl.BlockSpec((SB_QB, 2 * SB_WIN), lambda p, i: (i, p))
    dq_tile = pl.BlockSpec((SB_QB, 128), lambda p, i: (i, col // 128 + p))
    tri = pl.BlockSpec((SB_WIN, SB_WIN), lambda p, i: (0, 0), pipeline_mode=pl.Buffered(1))
    idx = np.arange(SB_WIN)
    return pl.pallas_call(
        body, name="sb_bwd", grid=(npairs, nq),
        in_specs=[pl.BlockSpec(memory_space=pltpu.SMEM), tile, whole_in(npairs), whole_in(2 * npairs), tile, tile, tri, tri,
                  near, near, HBM_SPEC],
        out_specs=[dq_tile, whole_out, whole_out],
        out_shape=[SDS(buf.shape, buf.dtype)] + [SDS((s, SB_W), F32)] * 2,
        input_output_aliases={10: 0},
        compiler_params=pltpu.CompilerParams(dimension_semantics=("arbitrary", "arbitrary"), vmem_limit_bytes=VMEM_LIMIT),
    )(nblk, qkv, qkv, qkv, do, tot, jnp.asarray(idx[:, None] <= idx[None, :], BF16), jnp.asarray(idx[:, None] < idx[None, :], BF16),
      w0, b0, buf)


def _dsa_mask(DSA_T, has_prev):
    r = lax.broadcasted_iota(jnp.int32, (DSA_T, QB + DSA_T), 0)
    j = lax.broadcasted_iota(jnp.int32, (DSA_T, QB + DSA_T), 1) - QB
    return (j <= r) & (j >= r - QB) & ((j >= 0) | has_prev)


def _dsa_fwd(q, k, v, dil):
    n = q.shape[0]
    DSA_T = DSA_T_FWD
    nt = n // DSA_T

    def body(q_ref, kc_ref, kp_ref, vc_ref, vp_ref, o_ref, lse_ref):
        mask = _dsa_mask(DSA_T, pl.program_id(1) > 0)
        outs, lses = [], []
        for hh in range(DSA_OUT_W // HD):
            sl = slice(HD * hh, HD * hh + HD)
            kcat = jnp.concatenate([kp_ref[:, sl], kc_ref[:, sl]], axis=0)
            vcat = jnp.concatenate([vp_ref[:, sl], vc_ref[:, sl]], axis=0)
            sc = jnp.where(mask, _dot(q_ref[:, sl], kcat, NT) * SCALE, NEG)
            m = jnp.max(sc, axis=1, keepdims=True)
            p = jnp.exp(sc - m)
            den = jnp.sum(p, axis=1, keepdims=True)
            outs.append(_dot(p.astype(BF16), vcat) / den)
            lses.append(m + jnp.log(den))
        o_ref[...] = jnp.concatenate(outs, axis=1)
        lse_ref[...] = _bcast_heads(lses)

    cur = pl.BlockSpec((DSA_T, DSA_OUT_W), lambda c, i: (i, c))
    prev = pl.BlockSpec((QB, DSA_OUT_W), lambda c, i: (jnp.maximum(i * (DSA_T // QB) - 1, 0), c))
    o, lse = pl.pallas_call(
        body, name=f"dsa_fwd_d{dil}", grid=(dil, nt), in_specs=[cur, cur, prev, cur, prev], out_specs=[cur, cur],
        out_shape=[SDS((n, dil * DSA_OUT_W), F32)] * 2,
        compiler_params=pltpu.CompilerParams(dimension_semantics=("parallel", "parallel")),
    )(q, k, k, v, v)
    return o, lse


def _dsa_bwd(q, k, v, do, cc, lse, dil):
    n = q.shape[0]
    DSA_T = DSA_T_BWD
    nt = n // DSA_T
    per = DSA_T // QB

    def body(qj_ref, qn_ref, kp_ref, kj_ref, vp_ref, vj_ref, doj_ref, don_ref, cj_ref, cn_ref, lj_ref, ln_ref,
             dq_ref, dk_ref, dv_ref):
        j = pl.program_id(1)
        mask = _dsa_mask(DSA_T, j > 0)
        r = lax.broadcasted_iota(jnp.int32, (QB, DSA_T), 0)
        kk = lax.broadcasted_iota(jnp.int32, (QB, DSA_T), 1)
        m_next = (kk >= r + QB) & (j + 1 < nt)
        dqs, dks, dvs = [], [], []
        for hh in range(DSA_OUT_W // HD):
            sl = slice(HD * hh, HD * hh + HD)
            one = slice(HD * hh, HD * hh + 1)
            qj, qn, kj, vj, doj, don = (t[:, sl] for t in (qj_ref, qn_ref, kj_ref, vj_ref, doj_ref, don_ref))
            kcat = jnp.concatenate([kp_ref[:, sl], kj], axis=0)
            vcat = jnp.concatenate([vp_ref[:, sl], vj], axis=0)
            p1 = jnp.where(mask, jnp.exp(_dot(qj, kcat, NT) * SCALE - lj_ref[:, one]), 0.0)
            ds1 = (p1 * (_dot(doj, vcat, NT) + cj_ref[:, one]) * SCALE).astype(BF16)
            p2 = jnp.where(m_next, jnp.exp(_dot(qn, kj, NT) * SCALE - ln_ref[:, one]), 0.0)
            ds2 = (p2 * (_dot(don, vj, NT) + cn_ref[:, one]) * SCALE).astype(BF16)
            dqs.append(_dot(ds1, kcat))
            dks.append(_dot(ds1[:, QB:], qj, TN) + _dot(ds2, qn, TN))
            dvs.append(_dot(p1[:, QB:].astype(BF16), doj, TN) + _dot(p2.astype(BF16), don, TN))
        dq_ref[...] = jnp.concatenate(dqs, axis=1).astype(dq_ref.dtype)
        dk_ref[...] = jnp.concatenate(dks, axis=1).astype(dk_ref.dtype)
        dv_ref[...] = jnp.concatenate(dvs, axis=1).astype(dv_ref.dtype)

    cur = pl.BlockSpec((DSA_T, DSA_OUT_W), lambda c, j: (j, c))
    prev = pl.BlockSpec((QB, DSA_OUT_W), lambda c, j: (jnp.maximum(j * per - 1, 0), c))
    nxt = pl.BlockSpec((QB, DSA_OUT_W), lambda c, j: (jnp.minimum((j + 1) * per, n // QB - 1), c))
    dq, dk, dv = pl.pallas_call(
        body, name=f"dsa_bwd_d{dil}", grid=(dil, nt),
        in_specs=[cur, nxt, prev, cur, prev, cur, cur, nxt, cur, nxt, cur, nxt], out_specs=[cur, cur, cur],
        out_shape=[SDS((n, dil * DSA_OUT_W), BF16)] * 3,
        compiler_params=pltpu.CompilerParams(dimension_semantics=("parallel", "parallel")),
    )(q, q, k, k, v, v, do, do, cc, cc, lse, lse)
    return dq, dk, dv


def _norm_bwd_epi(acc, xv, dyv, g):
    dx, dg = _rms_bwd(xv, g, acc, _mean_all)
    return dx + dyv, dg


def _ffn_fwd(tag, x, gain, w13, w2, plan=None, target=None):
    n = _tokmap(f"{tag}_norm", lambda xv, g: _rms_fwd(xv, g, _mean_all), [x], [gain], [(D, BF16)])[0]
    ab = _matmul(f"{tag}_up", n, w13, NN, BF16, plan=plan)

    def gate(av, bv):
        a, b = av.astype(F32), bv.astype(F32)
        return (a * jax.nn.sigmoid(a) * b).astype(BF16)

    if target is None:
        y, h = _matmul(f"{tag}_down", ab, w2(), NN, F32, epi=lambda acc, res: res + 0.5 * acc, tiles=[x], plan=plan,
                       a_pro=(gate, 2))
        return y, (n, ab, h)

    def loss_epi(acc, res, tv):
        e = res + 0.5 * acc - tv
        return e * (1.0 / D), (0.5 / D) * jnp.sum(e * e, axis=0, keepdims=True)

    dy, loss_row, h = _matmul(f"{tag}_down", ab, w2(), NN, F32, epi=loss_epi, tiles=[x, target], plan=plan, a_pro=(gate, 2),
                              n_sum=1)
    return (dy, loss_row), (n, ab, h)


def _ffn_bwd(tag, x, gain, w13, w2, saved, dy, plan=None, on_dw=None, final=False):
    n, ab, h = saved
    dh = _matmul(f"{tag}_bwd_dh", dy, w2, NT, BF16, epi=lambda acc: 0.5 * acc)

    def gate_bwd(abv, dhv):
        a, b, dhf = abv[:, :D_FF].astype(F32), abv[:, D_FF:].astype(F32), dhv.astype(F32)
        sg = jax.nn.sigmoid(a)
        da = dhf * b * (sg * (1.0 + a * (1.0 - sg)))
        return jnp.concatenate([da, dhf * (a * sg)], axis=1)

    dab = _tokmap(f"{tag}_bwd_gate", gate_bwd, [ab, dh], [], [(2 * D_FF, BF16)], tile=256)[0]
    dw2 = _matmul(f"{tag}_bwd_dw2", h, dy, TN, F32, epi=lambda acc: 0.5 * acc)
    dw13 = _matmul(f"{tag}_bwd_dw13", n, dab, TN, F32, plan=plan)
    if on_dw is not None:
        on_dw(dw13, dw2)
    if not final:
        dx, dgain = _matmul(f"{tag}_bwd_dn", dab, w13, NT, F32, epi=_norm_bwd_epi, tiles=[x, dy], rows=[gain], plan=plan,
                            n_sum=1)
    else:
        dn = _matmul(f"{tag}_bwd_dn", dab, w13, NT, F32, plan=plan)
        dx, dgain = _tokmap(f"{tag}_bwd_norm", lambda xv, dnv, dyv, g: _norm_bwd_epi(dnv, xv, dyv, g), [x, dn, dy], [gain],
                            [(D, F32)], [(1, D)])
    return dx, dgain, dw13, dw2


def _rope_tables(s):
    half = HD // 2
    inv_freq = jnp.power(10000.0, -jnp.arange(half, dtype=F32) / half)
    ang = jnp.arange(s).astype(F32)[:, None] * inv_freq[None, :]
    cos, sin = jnp.cos(ang), jnp.sin(ang)
    return jnp.tile(jnp.concatenate([cos, cos], axis=1), (1, 2)), jnp.tile(jnp.concatenate([-sin, sin], axis=1), (1, 2))


def _local_step(x, mem, tgt, w, sm, plan=None, on_grads=None):
    s = x.shape[0]
    assert s % (max(DSA_T_FWD, DSA_T_BWD) * max(DSA_DILS)) == 0
    on_grads = on_grads or (lambda group, grads: None)
    c_sb, c_dsa, c_qm, c_all = 3 * D, 3 * D + 3 * SB_W, 3 * D + 3 * SB_W + 3 * DSA_W, 3 * D + 4096
    cos, sin = _rope_tables(s)
    bd768 = bd256 = _block_diag(128)
    gq_dsa, gk_dsa = jnp.tile(sm["qn_dsa"], (1, DSA_W // HD)), jnp.tile(sm["kn_dsa"], (1, DSA_W // HD))
    gq_mem, gk_mem = jnp.tile(sm["qn_mem"], (1, MEM_W // HD)), jnp.tile(sm["kn_mem"], (1, MEM_W // HD))

    w13_1 = jnp.concatenate([w["ffn1_w1"], w["ffn1_w3"]], axis=1)
    x1, ffn1_saved = _ffn_fwd("ffn1", x, sm["ffn1_norm"], w13_1, lambda: w["ffn1_w2"], plan)
    w_all = jnp.concatenate([w["w_gate"], w["w_in"]], axis=1)
    wb_sb, wb_dsa, wb_mem = w["w_branch_sb"], w["w_branch_dsa"], w["w_branch_mem"]
    hmix = _tokmap("mix_norm", lambda xv, g: _rms_fwd(xv, g, _mean_all), [x1], [sm["mix_norm"]], [(D, BF16)])[0]
    qkv_sb = _matmul("proj_sb", hmix, w_all[:, c_sb:c_dsa], NN, BF16)
    qkv_dsa = _matmul("proj_dsa", hmix, w_all[:, c_dsa:c_qm], NN, BF16, plan=plan)
    q_mem = _matmul("proj_qmem", hmix, w_all[:, c_qm:], NN, BF16)
    gpre = _matmul("proj_gate", hmix, w_all[:, :c_sb], NN, BF16, epi=lambda acc, b: acc + b, rows=[sm["b_gate"]], plan=plan)

    o_sb, sb_tot, sb_w0, sb_b0, sb_nblk = _sb_fwd(qkv_sb)

    def dsa_prep(qkv, cs, sn, gq, gk, bd):
        mean = _mean_heads(bd)
        qn = _rope_fwd(_rms_fwd(qkv[:, :DSA_W].astype(F32), gq, mean), cs, sn)
        kn = _rope_fwd(_rms_fwd(qkv[:, DSA_W:2 * DSA_W].astype(F32), gk, mean), cs, sn)
        v = qkv[:, 2 * DSA_W:]
        outs = []
        for t in (qn, kn, v):
            outs += [t[:, DSA_OUT_W * g:DSA_OUT_W * (g + 1)] for g in range(3)]
        return outs

    dsa_in = _tokmap("dsa_prep", dsa_prep, [qkv_dsa, cos, sin], [gq_dsa, gk_dsa, bd768], [(DSA_OUT_W, BF16)] * 9, tile=512,
                     dil_outs={j: DSA_DILS[j % 3] for j in range(9)})
    dsa_q, dsa_k, dsa_v = dsa_in[0:3], dsa_in[3:6], dsa_in[6:9]
    dsa_o, dsa_lse = zip(*[_dsa_fwd(dsa_q[g], dsa_k[g], dsa_v[g], DSA_DILS[g]) for g in range(3)])

    def alphas(l0, l1, l2):
        m = jnp.maximum(jnp.maximum(l0, l1), l2)
        e = [jnp.exp(l - m) for l in (l0, l1, l2)]
        tot = e[0] + e[1] + e[2]
        return [t / tot for t in e]

    def dsa_mix(o0, o1, o2, l0, l1, l2):
        a = alphas(l0, l1, l2)
        return a[0] * o0 + a[1] * o1 + a[2] * o2

    o_dsa = _tokmap("dsa_mix", dsa_mix, [*dsa_o, *dsa_lse], [], [(DSA_OUT_W, BF16)], tile=512,
                    dil_ins={j: DSA_DILS[j % 3] for j in range(6)})[0]

    def mem_kv(memv, g, wkv, gk, bd):
        kv = _dot(_rms_fwd(memv, g, _mean_all).astype(BF16), wkv)
        return _rms_fwd(kv[:, :MEM_W], gk, _mean_heads(bd)), kv[:, MEM_W:]

    km, vm = _tokmap("mem_kv", mem_kv, [mem], [sm["mem_norm"], w["w_mem_kv"], gk_mem, bd256], [(MEM_W, BF16)] * 2)

    def mem_probs(qv, kmv, gq, bd):
        qn = _rms_fwd(qv.astype(F32), gq, _mean_heads(bd)).astype(BF16)
        ps = []
        for h in range(MEM_W // HD):
            sl = slice(HD * h, HD * h + HD)
            sc = _dot(qn[:, sl], kmv[:, sl], NT) * SCALE
            e = jnp.exp(sc - jnp.max(sc, axis=1, keepdims=True))
            ps.append(e / jnp.sum(e, axis=1, keepdims=True))
        return qn, ps

    def mem_attn(qv, kmv, vmv, gq, bd):
        _, ps = mem_probs(qv, kmv, gq, bd)
        return jnp.concatenate([_dot(p.astype(BF16), vmv[:, HD * h:HD * h + HD]) for h, p in enumerate(ps)], axis=1)

    o_mem = _tokmap("mem_attn", mem_attn, [q_mem], [km, vm, gq_mem, bd256], [(MEM_W, BF16)])[0]

    def merge(osb, odsa, omem, gp, w_sb, w_dsa, w_mem):
        gates = jax.nn.sigmoid(gp.astype(F32))
        ys = (_dot(osb, w_sb), _dot(odsa, w_dsa), _dot(omem, w_mem))
        return gates, ys, gates[:, :D] * ys[0] + gates[:, D:2 * D] * ys[1] + gates[:, 2 * D:] * ys[2]

    merged = _tokmap("merge", lambda *a: merge(*a)[2], [o_sb, o_dsa, o_mem, gpre], [wb_sb, wb_dsa, wb_mem], [(D, BF16)],
                     tile=512)[0]
    x2 = _matmul("out_proj", merged, w["w_out"], NN, F32, epi=lambda acc, res: res + acc, tiles=[x1])
    w13_2 = jnp.concatenate([w["ffn2_w1"], w["ffn2_w3"]], axis=1)
    (dy, loss_row), ffn2_saved = _ffn_fwd("ffn2", x2, sm["ffn2_norm"], w13_2, lambda: w["ffn2_w2"], target=tgt)
    loss = jnp.sum(loss_row).reshape(1, 1)

    gw, gs = {}, {}
    def ffn_grads(tag):
        def on_dw(dw13, dw2):
            gw[f"{tag}_w1"], gw[f"{tag}_w3"], gw[f"{tag}_w2"] = dw13[:, :D_FF], dw13[:, D_FF:], dw2
            on_grads(tag, {n: gw[n] for n in (f"{tag}_w1", f"{tag}_w3", f"{tag}_w2")})
        return on_dw

    dx2, gs["ffn2_norm"], _, _ = _ffn_bwd("ffn2", x2, sm["ffn2_norm"], w13_2, w["ffn2_w2"], ffn2_saved, dy, plan,
                                          ffn_grads("ffn2"))
    dmerged = _matmul("out_proj_bwd_dx", dx2, w["w_out"], NT, BF16)
    gw["w_out"] = _matmul("out_proj_bwd_dw", merged, dx2, TN, F32)

    def merge_bwd(osb, odsa, omem, gp, dm, w_sb, w_dsa, w_mem):
        gates, ys, _ = merge(osb, odsa, omem, gp, w_sb, w_dsa, w_mem)
        dmf = dm.astype(F32)
        dgp, dos, dws = [], [], []
        for b, (ov, wv) in enumerate(((osb, w_sb), (odsa, w_dsa), (omem, w_mem))):
            gb = gates[:, D * b:D * (b + 1)]
            dgp.append(dmf * ys[b] * gb * (1.0 - gb))
            dyb = (dmf * gb).astype(BF16)
            dos.append(_dot(dyb, wv, NT))
            dws.append(_dot(ov, dyb, TN))
        dgp = jnp.concatenate(dgp, axis=1)
        return dos[0], dos[1], dos[2], dgp, dws[0], dws[1], dws[2], jnp.sum(dgp, axis=0, keepdims=True)

    do_sb, do_dsa, do_mem, dgpre, gw["w_branch_sb"], gw["w_branch_dsa"], gw["w_branch_mem"], gs["b_gate"] = _tokmap(
        "merge_bwd", merge_bwd, [o_sb, o_dsa, o_mem, gpre, dmerged], [wb_sb, wb_dsa, wb_mem],
        [(SB_W, BF16), (DSA_OUT_W, F32), (MEM_W, BF16), (3 * D, BF16)],
        [(SB_W, D), (DSA_OUT_W, D), (MEM_W, D), (1, 3 * D)], tile=512, place={3: (c_all, 0, None)})

    dall, dk_sb, dv_sb = _sb_bwd(qkv_sb, do_sb, sb_tot, sb_nblk, sb_w0, sb_b0, dgpre, c_sb)
    dall = lax.dynamic_update_slice(dall, dk_sb.astype(BF16), (0, c_sb + SB_W))
    dall = lax.dynamic_update_slice(dall, dv_sb.astype(BF16), (0, c_sb + 2 * SB_W))

    def dsa_mix_bwd(o0, o1, o2, l0, l1, l2, dov, bd):
        a = alphas(l0, l1, l2)
        omix = a[0] * o0 + a[1] * o1 + a[2] * o2
        dot_o = _head_sums(dov * omix, bd)
        return [dov * t for t in a] + [-t * dot_o for t in a]

    mixb = _tokmap("dsa_mix_bwd", dsa_mix_bwd, [*dsa_o, *dsa_lse, do_dsa], [bd256],
                   [(DSA_OUT_W, BF16)] * 3 + [(DSA_OUT_W, F32)] * 3, tile=512,
                   dil_ins={j: DSA_DILS[j % 3] for j in range(6)}, dil_outs={j: DSA_DILS[j % 3] for j in range(6)})
    dsa_d = [_dsa_bwd(dsa_q[g], dsa_k[g], dsa_v[g], mixb[g], mixb[3 + g], dsa_lse[g], DSA_DILS[g]) for g in range(3)]

    def dsa_prep_bwd(qkv, cs, sn, *rest):
        dqs, dks, dvs, (gq, gk, bd) = rest[0:3], rest[3:6], rest[6:9], rest[9:]
        mean = _mean_heads(bd)
        dq, dgq = _rms_bwd(qkv[:, :DSA_W].astype(F32), gq, _rope_bwd(jnp.concatenate(dqs, axis=1), cs, sn), mean)
        dk, dgk = _rms_bwd(qkv[:, DSA_W:2 * DSA_W].astype(F32), gk, _rope_bwd(jnp.concatenate(dks, axis=1), cs, sn), mean)
        return jnp.concatenate([dq, dk] + list(dvs), axis=1), dgq, dgk

    dall, dgq_dsa, dgk_dsa = _tokmap(
        "dsa_prep_bwd", dsa_prep_bwd,
        [qkv_dsa, cos, sin] + [dsa_d[g][t] for t in range(3) for g in range(3)], [gq_dsa, gk_dsa, bd768],
        [(3 * DSA_W, BF16)], [(1, DSA_W), (1, DSA_W)], tile=512, dil_ins={3 + j: DSA_DILS[j % 3] for j in range(9)},
        place={0: (c_all, c_dsa // (3 * DSA_W), dall)})
    gs["qn_dsa"] = dgq_dsa.reshape(DSA_W // HD, HD).sum(axis=0, keepdims=True)
    gs["kn_dsa"] = dgk_dsa.reshape(DSA_W // HD, HD).sum(axis=0, keepdims=True)

    def mem_attn_bwd(qv, dov, kmv, vmv, gq, bd):
        qn, ps = mem_probs(qv, kmv, gq, bd)
        dqn, dkm, dvm = [], [], []
        for h, p in enumerate(ps):
            sl = slice(HD * h, HD * h + HD)
            dp = _dot(dov[:, sl], vmv[:, sl], NT)
            ds = (p * (dp - jnp.sum(p * dp, axis=1, keepdims=True)) * SCALE).astype(BF16)
            dqn.append(_dot(ds, kmv[:, sl]))
            dkm.append(_dot(ds, qn[:, sl], TN))
            dvm.append(_dot(p.astype(BF16), dov[:, sl], TN))
        dq, dgq = _rms_bwd(qv.astype(F32), gq, jnp.concatenate(dqn, axis=1), _mean_heads(bd))
        return dq, jnp.concatenate(dkm, axis=1), jnp.concatenate(dvm, axis=1), dgq

    dall, dkm, dvm, dgq_mem = _tokmap("mem_attn_bwd", mem_attn_bwd, [q_mem, do_mem], [km, vm, gq_mem, bd256],
                                      [(MEM_W, BF16)], [(MEM_LEN, MEM_W), (MEM_LEN, MEM_W), (1, MEM_W)],
                                      place={0: (c_all, c_qm // MEM_W, dall)})
    gs["qn_mem"] = dgq_mem.reshape(MEM_W // HD, HD).sum(axis=0, keepdims=True)

    def mem_kv_bwd(memv, dkmv, dvmv, g, wkv, gk, bd):
        memn = _rms_fwd(memv, g, _mean_all).astype(BF16)
        kv = _dot(memn, wkv)
        dk, dgk = _rms_bwd(kv[:, :MEM_W], gk, dkmv, _mean_heads(bd))
        dkv = jnp.concatenate([dk, dvmv], axis=1).astype(BF16)
        _, dg = _rms_bwd(memv, g, _dot(dkv, wkv, NT), _mean_all)
        return _dot(memn, dkv, TN), dg, dgk

    gw["w_mem_kv"], gs["mem_norm"], dgk_mem = _tokmap(
        "mem_kv_bwd", mem_kv_bwd, [mem, dkm, dvm], [sm["mem_norm"], w["w_mem_kv"], gk_mem, bd256], [],
        [(D, 2 * MEM_W), (1, D), (1, MEM_W)])
    gs["kn_mem"] = dgk_mem.reshape(MEM_W // HD, HD).sum(axis=0, keepdims=True)

    dx1, gs["mix_norm"] = _matmul("proj_bwd_dx", dall, w_all, NT, F32, epi=_norm_bwd_epi, tiles=[x1, dx2],
                                  rows=[sm["mix_norm"]], n_sum=1)
    dw_all = _matmul("proj_bwd_dw", hmix, dall, TN, F32)
    gw["w_gate"], gw["w_in"] = dw_all[:, :c_sb], dw_all[:, c_sb:]
    on_grads("mid", {n: gw[n] for n in GROUPS["mid"]})
    gx, gs["ffn1_norm"], _, _ = _ffn_bwd("ffn1", x, sm["ffn1_norm"], w13_1, w["ffn1_w2"], ffn1_saved, dx1, plan,
                                         ffn_grads("ffn1"), final=True)
    return loss, gx, gw, gs


def _shard_shape(name):
    shape, axis = SHARDED_BY_NAME[name]
    return (shape[0] // N_CHIPS, shape[1]) if axis == 0 else (shape[0], shape[1] // N_CHIPS)


def _full_from_shards(name, shards):
    axis = SHARDED_BY_NAME[name][1]
    return shards.reshape(SHARDED_BY_NAME[name][0]) if axis == 0 else jnp.concatenate(list(shards), axis=1)


def _shards_from_full(name, full, dtype):
    axis, n = SHARDED_BY_NAME[name][1], _shard_shape(name)
    return jnp.stack([lax.slice_in_dim(full, c * n[axis], (c + 1) * n[axis], axis=axis).astype(dtype) for c in range(N_CHIPS)])


def _own_shard(name, full, chip):
    axis, n = SHARDED_BY_NAME[name][1], _shard_shape(name)
    return lax.dynamic_slice_in_dim(full, chip * n[axis], n[axis], axis=axis)


SMALL_USED = sum(n for _, n in SMALL)


def _pack_small(d, loss=None):
    parts = [d[n].reshape(-1) for n, _ in SMALL]
    parts.append(jnp.zeros((1,), F32) if loss is None else loss.reshape(1))
    parts.append(jnp.zeros((SMALL_ROWS * D - SMALL_USED - 1,), F32))
    return jnp.concatenate(parts).reshape(SMALL_ROWS, D)


def _unpack_small(v):
    flat, out, r = v.reshape(-1), {}, 0
    for n, k in SMALL:
        out[n] = flat[r:r + k]
        r += k
    return out, flat[r]


def _place():
    return lax.axis_index("x"), lax.axis_index("y"), lax.axis_index("c")


def _other_chips(x, y):
    return [(1 - x, y), (x, 1 - y), (1 - x, 1 - y)]


HBM_SPEC = pl.BlockSpec(memory_space=pl.ANY)


def _chip_sems(n):
    return (pltpu.SemaphoreType.DMA((3 * n,)), pltpu.SemaphoreType.DMA((3 * n,)), pltpu.SemaphoreType.DMA((n,)))


def _gather_copies(ins, outs, send_sems, recv_sems, local_sems):
    x, y, c = _place()
    me = 2 * x + y
    copies = []
    for a, (src, out) in enumerate(zip(ins, outs)):
        copies.append(pltpu.make_async_copy(src, out.at[me], local_sems.at[a]))
        copies += [pltpu.make_async_remote_copy(src_ref=src, dst_ref=out.at[me], send_sem=send_sems.at[3 * a + k],
                                                recv_sem=recv_sems.at[3 * a + k], device_id=(px, py, c), device_id_type=MESH)
                   for k, (px, py) in enumerate(_other_chips(x, y))]
    return copies


def _scatter_copies(ins, outs, send_sems, recv_sems, local_sems):
    x, y, c = _place()
    return [pltpu.make_async_remote_copy(src_ref=src.at[2 * px + py], dst_ref=out.at[k], send_sem=send_sems.at[3 * a + k],
                                         recv_sem=recv_sems.at[3 * a + k], device_id=(px, py, c), device_id_type=MESH)
            for a, (src, out) in enumerate(zip(ins, outs)) for k, (px, py) in enumerate(_other_chips(x, y))]


def _all_gather_chips(arrays):
    n = len(arrays)

    def body(*refs):
        ins, outs = refs[:n], refs[n:2 * n]
        send1, recv1, send2, recv2, local_sems = refs[2 * n:]
        x, y, c = _place()
        me = 2 * x + y
        chips = _other_chips(x, y)
        local = [pltpu.make_async_copy(src, out.at[me], local_sems.at[a]) for a, (src, out) in enumerate(zip(ins, outs))]
        for cp in local:
            cp.start()

        def half(ref, chip, which):
            rows = ref.shape[-2] // 2
            return ref.at[chip, pl.ds(which * rows, rows)] if chip is not None else ref.at[pl.ds(which * rows, rows)]

        first = [pltpu.make_async_remote_copy(src_ref=half(src, None, c), dst_ref=half(out, me, c), send_sem=send1.at[3 * a + k],
                                              recv_sem=recv1.at[3 * a + k], device_id=(px, py, c), device_id_type=MESH)
                 for a, (src, out) in enumerate(zip(ins, outs)) for k, (px, py) in enumerate(chips)]
        for cp in first:
            cp.start()
        passed = []
        for a, out in enumerate(outs):
            for k, (px, py) in enumerate(chips):
                pltpu.make_async_remote_copy(src_ref=half(out, 2 * px + py, c), dst_ref=half(out, 2 * px + py, c),
                                             send_sem=send1.at[3 * a + k], recv_sem=recv1.at[3 * a + k],
                                             device_id=(px, py, c), device_id_type=MESH).wait_recv()
                cp = pltpu.make_async_remote_copy(src_ref=half(out, 2 * px + py, c), dst_ref=half(out, 2 * px + py, c),
                                                  send_sem=send2.at[3 * a + k], recv_sem=recv2.at[3 * a + k],
                                                  device_id=(x, y, 1 - c), device_id_type=MESH)
                cp.start()
                passed.append(cp)
        for a, out in enumerate(outs):
            for k, (px, py) in enumerate(chips):
                pltpu.make_async_remote_copy(src_ref=half(out, 2 * px + py, 1 - c), dst_ref=half(out, 2 * px + py, 1 - c),
                                             send_sem=send2.at[3 * a + k], recv_sem=recv2.at[3 * a + k],
                                             device_id=(x, y, 1 - c), device_id_type=MESH).wait_recv()
        for cp in first + passed:
            cp.wait_send()
        for cp in local:
            cp.wait()

    sems = pltpu.SemaphoreType.DMA((3 * n,))
    return pl.pallas_call(
        body, name="weights_all_gather", in_specs=[HBM_SPEC] * n, out_specs=[HBM_SPEC] * n,
        out_shape=[SDS((N_CHIPS,) + a.shape, a.dtype) for a in arrays],
        scratch_shapes=[sems, sems, sems, sems, pltpu.SemaphoreType.DMA((n,))],
    )(*arrays)


def _swap_with_sibling(name, arrays):
    n = len(arrays)

    def body(*refs):
        x, y, c = _place()
        send_sems, recv_sems = refs[2 * n:]
        copies = [pltpu.make_async_remote_copy(src_ref=refs[a], dst_ref=refs[n + a], send_sem=send_sems.at[a],
                                               recv_sem=recv_sems.at[a], device_id=(x, y, 1 - c), device_id_type=MESH)
                  for a in range(n)]
        for cp in copies:
            cp.start()
        for cp in copies:
            cp.wait()

    return pl.pallas_call(
        body, name=name, in_specs=[HBM_SPEC] * n, out_specs=[HBM_SPEC] * n, out_shape=[SDS(a.shape, a.dtype) for a in arrays],
        scratch_shapes=[pltpu.SemaphoreType.DMA((n,)), pltpu.SemaphoreType.DMA((n,))],
    )(*arrays)


def _all_reduce_small(v):
    n_dev = 8

    def body(v_ref, out_ref, land, send_sems, recv_sems):
        x, y, c = _place()
        me = 4 * x + 2 * y + c
        land[me] = v_ref[...]
        copies = []
        for k in range(1, n_dev):
            peer = (x ^ (k >> 2), y ^ ((k >> 1) & 1), c ^ (k & 1))
            copies.append(pltpu.make_async_remote_copy(src_ref=v_ref, dst_ref=land.at[me], send_sem=send_sems.at[k - 1],
                                                       recv_sem=recv_sems.at[k - 1], device_id=peer, device_id_type=MESH))
        for cp in copies:
            cp.start()
        for cp in copies:
            cp.wait()
        acc = land[0]
        for d in range(1, n_dev):
            acc = acc + land[d]
        out_ref[...] = acc

    return pl.pallas_call(
        body, name="small_all_reduce", in_specs=[pl.BlockSpec(memory_space=pltpu.VMEM)],
        out_specs=pl.BlockSpec(memory_space=pltpu.VMEM), out_shape=SDS(v.shape, v.dtype),
        scratch_shapes=[pltpu.VMEM((n_dev,) + v.shape, v.dtype), pltpu.SemaphoreType.DMA((n_dev - 1,)),
                        pltpu.SemaphoreType.DMA((n_dev - 1,))],
    )(v)


def _adamw(g, wv, m, v):
    m = ADAM_B1 * m + (1.0 - ADAM_B1) * g
    v = ADAM_B2 * v + (1.0 - ADAM_B2) * (g * g)
    m_hat = m / (1.0 - ADAM_B1 ** ADAM_STEP)
    v_hat = v / (1.0 - ADAM_B2 ** ADAM_STEP)
    delta = -ADAM_LR * (m_hat / (jnp.sqrt(v_hat) + ADAM_EPS) + ADAM_WD * wv)
    return delta, m, v


def kernel(x, mem, ffn1_norm, ffn1_w1, ffn1_w3, ffn1_w2, mix_norm, mem_norm, w_in, w_mem_kv, qn_dsa, kn_dsa, qn_mem, kn_mem, w_branch_sb, w_branch_dsa, w_branch_mem, w_gate, b_gate, w_out, ffn2_norm, ffn2_w1, ffn2_w3, ffn2_w2, loss_target, m_ffn1_norm, m_ffn1_w1, m_ffn1_w3, m_ffn1_w2, m_mix_norm, m_mem_norm, m_w_in, m_w_mem_kv, m_qn_dsa, m_kn_dsa, m_qn_mem, m_kn_mem, m_w_branch_sb, m_w_branch_dsa, m_w_branch_mem, m_w_gate, m_b_gate, m_w_out, m_ffn2_norm, m_ffn2_w1, m_ffn2_w3, m_ffn2_w2, v_ffn1_norm, v_ffn1_w1, v_ffn1_w3, v_ffn1_w2, v_mix_norm, v_mem_norm, v_w_in, v_w_mem_kv, v_qn_dsa, v_kn_dsa, v_qn_mem, v_kn_mem, v_w_branch_sb, v_w_branch_dsa, v_w_branch_mem, v_w_gate, v_b_gate, v_w_out, v_ffn2_norm, v_ffn2_w1, v_ffn2_w3, v_ffn2_w2):
    given = dict(locals())
    wts = {n: given[n][0] for n in WEIGHTS}
    moms = {n: given["m_" + n][0] for n in WEIGHTS}
    vars_ = {n: given["v_" + n][0] for n in WEIGHTS}

    plan = _Plan()
    x_i, y_i, _ = _place()
    my_chip = 2 * x_i + y_i

    full = {}

    def gathered(names):
        return lambda res: full.update({n: _full_from_shards(n, g) for n, g in zip(names, res)})

    for host, names in WEIGHT_PIECES:
        shards = [wts[n].astype(BF16) for n in names]
        if host is None:
            gathered(names)(_all_gather_chips(shards))
        else:
            plan.put(host, _Carry(shards, [SDS((N_CHIPS,) + a.shape, BF16) for a in shards], _chip_sems(len(names)),
                                  _gather_copies, gathered(names)))
    small = {n: wts[n].reshape(1, -1) for n, _ in SMALL}

    landed = {}

    def on_grads(group, grads):
        names = GROUPS[group]
        slices = [_shards_from_full(n, grads[n], BF16) for n in names]
        own = [_own_shard(n, grads[n], my_chip) for n in names]
        plan.put(GRAD_HOSTS[group], _Carry(slices, [SDS((3,) + a.shape[1:], BF16) for a in slices], _chip_sems(len(names)),
                                           _scatter_copies, lambda res: landed.update({group: (own, res)})))

    loss, gx, _, gs = _local_step(x[0], mem[0], loss_target[0], full, small, plan, on_grads)
    assert not plan.pending, list(plan.pending)

    def update(hv, ov, wv, mv, vv):
        g = hv + ov
        return (g,) + _adamw(g, wv, mv, vv)

    outs = [{}, {}, {}, {}]
    for group, names in GROUPS.items():
        own, got = landed[group]
        halves = [_tokmap(f"grads_sum_chips_{n}",
                          lambda a, b0, b1, b2: ((a + b0.astype(F32)) + b1.astype(F32)) + b2.astype(F32),
                          [o, g[0], g[1], g[2]], [], [(o.shape[1], F32)])[0] for n, o, g in zip(names, own, got)]
        others = _swap_with_sibling(f"grads_swap_cores_{group}", halves)
        for n, half, other in zip(names, halves, others):
            res = _tokmap(f"adamw_{n}", update, [half, other, wts[n], moms[n], vars_[n]], [], [(half.shape[1], F32)] * 4)
            for d, r in zip(outs, res):
                d[n] = r

    s_red = _all_reduce_small(_pack_small(gs, loss[0, 0]))
    res = _tokmap(
        "adamw_small", lambda g, wv, mv, vv: (g,) + _adamw(g, wv, mv, vv),
        [s_red, _pack_small(small), _pack_small({n: moms[n] for n, _ in SMALL}), _pack_small({n: vars_[n] for n, _ in SMALL})],
        [], [(D, F32)] * 4)
    for d, packed in zip(outs, res):
        d.update(_unpack_small(packed)[0])
    _, total_loss = _unpack_small(s_red)
    return (total_loss, gx[None], *[d[n][None] for d in outs for n in WEIGHTS])
```

```python
import functools

import numpy as np
import jax
import jax.numpy as jnp
from jax import lax
from jax.experimental import pallas as pl
from jax.experimental.pallas import tpu as pltpu

F32, BF16 = jnp.float32, jnp.bfloat16
SDS = jax.ShapeDtypeStruct
MESH = pl.DeviceIdType.MESH

D = 1024
HD = 64
QB = 128
DSA_T_FWD, DSA_T_BWD = 512, 256
D_FF = 2816
SB_W, DSA_W, DSA_OUT_W, MEM_W = 512, 768, 256, 256
DSA_DILS = (1, 4, 16)
MEM_LEN = 256
N_CHIPS = 4
EPS = 1e-6
SCALE = HD ** -0.5
EXHAUSTED = -104.0
SB_FWD_HEADS = 4
SB_QB = 256
SB_WIN = 512
NEG = -1e30
VMEM_LIMIT = 56 * 1024 * 1024

ADAM_LR, ADAM_B1, ADAM_B2, ADAM_EPS, ADAM_WD, ADAM_STEP = 0.001, 0.9, 0.999, 1e-08, 0.01, 10

NN = (((1,), (0,)), ((), ()))
NT = (((1,), (1,)), ((), ()))
TN = (((0,), (0,)), ((), ()))

SHARDED = (
    ("ffn1_w1", (D, D_FF), 1), ("ffn1_w3", (D, D_FF), 1), ("ffn1_w2", (D_FF, D), 0),
    ("w_in", (D, 4096), 1), ("w_mem_kv", (D, 512), 0),
    ("w_branch_sb", (SB_W, D), 1), ("w_branch_dsa", (DSA_OUT_W, D), 1), ("w_branch_mem", (MEM_W, D), 1),
    ("w_gate", (D, 3 * D), 1), ("w_out", (D, D), 0),
    ("ffn2_w1", (D, D_FF), 1), ("ffn2_w3", (D, D_FF), 1), ("ffn2_w2", (D_FF, D), 0),
)
SHARDED_BY_NAME = {n: (sh, ax) for n, sh, ax in SHARDED}
GROUPS = {
    "ffn2": ("ffn2_w1", "ffn2_w3", "ffn2_w2"),
    "mid": ("w_in", "w_mem_kv", "w_branch_sb", "w_branch_dsa", "w_branch_mem", "w_gate", "w_out"),
    "ffn1": ("ffn1_w1", "ffn1_w3", "ffn1_w2"),
}
WEIGHT_PIECES = (
    (None, ("ffn1_w1", "ffn1_w3")),
    ("ffn1_up", ("ffn1_w2", "w_in")),
    ("ffn1_down", ("w_gate", "w_mem_kv", "w_branch_sb", "w_branch_dsa", "w_branch_mem", "w_out")),
    ("proj_dsa", ("ffn2_w2",)),
    ("proj_gate", ("ffn2_w1", "ffn2_w3")),
)
GRAD_HOSTS = {"ffn2": "ffn2_bwd_dn", "mid": "ffn1_bwd_dw13", "ffn1": "ffn1_bwd_dn"}
SMALL = (("ffn1_norm", D), ("mix_norm", D), ("mem_norm", D), ("ffn2_norm", D), ("b_gate", 3 * D),
         ("qn_dsa", HD), ("kn_dsa", HD), ("qn_mem", HD), ("kn_mem", HD))
WEIGHTS = ("ffn1_norm", "ffn1_w1", "ffn1_w3", "ffn1_w2", "mix_norm", "mem_norm", "w_in", "w_mem_kv", "qn_dsa", "kn_dsa",
           "qn_mem", "kn_mem", "w_branch_sb", "w_branch_dsa", "w_branch_mem", "w_gate", "b_gate", "w_out", "ffn2_norm",
           "ffn2_w1", "ffn2_w3", "ffn2_w2")
SMALL_ROWS = 8


def _dot(a, b, dn=NN):
    return lax.dot_general(a, b, dn, preferred_element_type=F32)


def _dot01(x, m01, pieces=3):
    hi = x.astype(BF16)
    r1 = x - hi.astype(F32)
    mid = r1.astype(BF16)
    if pieces == 2:
        return _dot(hi, m01) + _dot(mid, m01)
    lo = (r1 - mid.astype(F32)).astype(BF16)
    return _dot(hi, m01) + _dot(mid, m01) + _dot(lo, m01)


def _pick(n, cands):
    for c in cands:
        if n % c == 0:
            return c
    raise ValueError(f"no tile for {n}")


def _from_dilated(v, d, scr):
    w = v.shape[1] // d
    v = v.astype(F32)
    for c in range(d):
        for p, buf in enumerate(scr[:w // 128]):
            buf[pl.ds(c, v.shape[0], stride=d), :] = v[:, c * w + 128 * p:c * w + 128 * (p + 1)]
    return jnp.concatenate([buf[...] for buf in scr[:w // 128]], axis=1)


def _to_dilated(v, d, scr):
    w = v.shape[1]
    for p, buf in enumerate(scr[:w // 128]):
        buf[...] = v[:, 128 * p:128 * (p + 1)].astype(F32)
    return jnp.concatenate([buf[pl.ds(c, v.shape[0] // d, stride=d), :] for c in range(d) for buf in scr[:w // 128]], axis=1)


def _tokmap(name, fn, tok_ins, consts, tok_outs, acc_outs=(), tile=512, dil_ins=None, dil_outs=None, place=None):
    dil_ins, dil_outs, place = dil_ins or {}, dil_outs or {}, place or {}
    bufs = [(j, buf) for j, (_, _, buf) in place.items() if buf is not None]
    n_buf = len(bufs)
    n = tok_ins[0].shape[0] * dil_ins.get(0, 1)
    tile = _pick(n, [t for t in (512, 352, 256, 128, 64, 32, 16, 8) if t <= tile])
    n_tin, n_in, n_tok, n_acc = len(tok_ins), len(tok_ins) + len(consts), len(tok_outs), len(acc_outs)
    n_scr = max([tok_ins[j].shape[1] // d // 128 for j, d in dil_ins.items() if d > 1]
                + [tok_outs[j][0] // 128 for j, d in dil_outs.items() if d > 1] + [0])

    def body(*refs):
        scr = refs[len(refs) - n_scr:]
        vals = [r[...] for r in refs[:n_in]]
        for j, d in dil_ins.items():
            if d > 1:
                vals[j] = _from_dilated(vals[j], d, scr)
        outs = fn(*vals)
        outs = list(outs) if isinstance(outs, (tuple, list)) else [outs]
        assert len(outs) == n_tok + n_acc, (name, len(outs))
        for j, d in dil_outs.items():
            if d > 1:
                outs[j] = _to_dilated(outs[j], d, scr)
        orefs = refs[n_in + n_buf:]
        for r, v in zip(orefs[:n_tok], outs[:n_tok]):
            r[...] = v.astype(r.dtype)
        if n_acc:
            @pl.when(pl.program_id(0) == 0)
            def _():
                for r in orefs[n_tok:n_tok + n_acc]:
                    r[...] = jnp.zeros(r.shape, r.dtype)
            for r, v in zip(orefs[n_tok:n_tok + n_acc], outs[n_tok:]):
                r[...] += v.astype(r.dtype)

    def tok_spec(width, d):
        return pl.BlockSpec((tile // d, d * width), lambda i: (i, 0))

    in_specs = [tok_spec(a.shape[1] // dil_ins.get(j, 1), dil_ins.get(j, 1)) for j, a in enumerate(tok_ins)]
    in_specs += [pl.BlockSpec(c.shape, lambda i: (0, 0)) for c in consts]
    in_specs += [HBM_SPEC] * n_buf
    out_specs = [tok_spec(w, dil_outs.get(j, 1)) for j, (w, _) in enumerate(tok_outs)]
    out_shape = [SDS((n // dil_outs.get(j, 1), w * dil_outs.get(j, 1)), dt) for j, (w, dt) in enumerate(tok_outs)]
    for j, (total, col_block, _) in place.items():
        out_specs[j] = pl.BlockSpec((tile, tok_outs[j][0]), lambda i, cb=col_block: (i, cb))
        out_shape[j] = SDS((n, total), tok_outs[j][1])
    out_specs += [pl.BlockSpec(s, lambda i: (0, 0)) for s in acc_outs]
    out_shape += [SDS(s, F32) for s in acc_outs]
    res = pl.pallas_call(
        body, name=name, grid=(n // tile,), in_specs=in_specs, out_specs=out_specs, out_shape=out_shape,
        scratch_shapes=[pltpu.VMEM((tile, 128), F32)] * n_scr,
        input_output_aliases={n_in + b: j for b, (j, _) in enumerate(bufs)},
        compiler_params=pltpu.CompilerParams(dimension_semantics=("arbitrary",), vmem_limit_bytes=VMEM_LIMIT),
    )(*tok_ins, *consts, *[buf for _, buf in bufs])
    return res


MATMUL_VMEM_BUDGET = 40 * 1024 * 1024


def _matmul_tiles(m, n, k, a_bytes, b_bytes, o_bytes, extra_bytes, whole_n=False):
    best = None
    for tk in [c for c in (3584, 2816, 2048, 1408, 1024, 512, 256, 128) if k % c == 0]:
        for tm in [c for c in (1408, 1024, 768, 512, 256, 128) if m % c == 0]:
            for tn in [n] if whole_n else [c for c in (1408, 1024, 768, 512, 256, 128) if n % c == 0]:
                need = 2 * tk * (tm * a_bytes + tn * b_bytes) + tm * tn * (2 * o_bytes + 2 * extra_bytes + 8)
                if need > MATMUL_VMEM_BUDGET:
                    continue
                score = (min(tm, 512) * min(tn, 512), tk, tm * tn, tn)
                if best is None or score > best[0]:
                    best = (score, (tm, tn, tk))
    return best[1]


class _Carry:
    def __init__(self, ins, outs, sems, copies, then):
        self.ins, self.outs, self.sems, self.copies, self.then = ins, outs, sems, copies, then


class _Plan:
    def __init__(self):
        self.pending = {}

    def put(self, host, carry):
        assert host not in self.pending, host
        self.pending[host] = carry

    def take(self, host):
        return self.pending.pop(host, None)


def _matmul(name, a, b, dn, out_dtype, epi=None, tiles=(), rows=(), plan=None, a_pro=None, n_sum=0):
    if dn == NN:
        (m, k), n = a.shape, b.shape[1]
    elif dn == NT:
        (m, k), n = a.shape, b.shape[0]
    else:
        (k, m), n = a.shape, b.shape[1]
    n_t, n_r = len(tiles), len(rows)
    pro, n_parts = a_pro if a_pro is not None else (None, 1)
    if pro is not None:
        assert dn == NN and n == _pick(n, (1024, 512))
        k //= n_parts
    if pro is not None:
        tm, tn, tk = _pick(m, (256, 128)), n, k
    else:
        tm, tn, tk = _matmul_tiles(m, n, k, a.dtype.itemsize, b.dtype.itemsize, jnp.dtype(out_dtype).itemsize,
                                   sum(t.dtype.itemsize for t in tiles), whole_n=n_sum > 0)
    nk = k // tk
    grid = (m // tm, n // tn, nk)
    assert pro is None or grid[1] == 1
    assert n_sum == 0 or grid[1] == 1
    carry = plan.take(name) if plan is not None else None
    n_ci, n_co = (len(carry.ins), len(carry.outs)) if carry else (0, 0)
    n_keep = 1 if pro is not None else 0

    def body(*refs):
        a_refs, b_ref, rest = refs[:n_parts], refs[n_parts], refs[n_parts + 1:]
        extras, rest = rest[:n_t + n_r], rest[n_t + n_r:]
        c_in, o_ref, rest = rest[:n_ci], rest[n_ci], rest[n_ci + 1:]
        sums, rest = rest[:n_sum], rest[n_sum:]
        keep, c_out, scratch = rest[:n_keep], rest[n_keep:n_keep + n_co], rest[n_keep + n_co:]
        ids = [pl.program_id(d) for d in range(3)]
        if n_sum:
            @pl.when((ids[0] == 0) & (ids[2] == 0))
            def _():
                for r in sums:
                    r[...] = jnp.zeros(r.shape, F32)
        if carry:
            sems = scratch[1:] if nk > 1 else scratch

            @pl.when((ids[0] == 0) & (ids[1] == 0) & (ids[2] == 0))
            def _():
                for cp in carry.copies(c_in, c_out, *sems):
                    cp.start()

        if pro is not None:
            av = pro(*[r[...] for r in a_refs])
            keep[0][...] = av
        else:
            av = a_refs[0][...].astype(BF16)
        part = _dot(av, b_ref[...].astype(BF16), dn)

        def finish(r):
            if epi is not None:
                r = epi(r, *[e[...] for e in extras])
            if n_sum:
                for ref, v in zip(sums, r[1:]):
                    ref[...] += v
                r = r[0]
            o_ref[...] = r.astype(o_ref.dtype)

        if nk == 1:
            finish(part)
        else:
            acc = scratch[0]

            @pl.when(ids[2] == 0)
            def _():
                acc[...] = part

            @pl.when(ids[2] > 0)
            def _():
                acc[...] += part

            @pl.when(ids[2] == nk - 1)
            def _():
                finish(acc[...])

        if carry:
            @pl.when((ids[0] == grid[0] - 1) & (ids[1] == grid[1] - 1) & (ids[2] == nk - 1))
            def _():
                for cp in carry.copies(c_in, c_out, *sems):
                    cp.wait()

    if dn == TN:
        a_specs = [pl.BlockSpec((tk, tm), lambda i, j, kk: (kk, i))]
    else:
        a_specs = [pl.BlockSpec((tm, tk), lambda i, j, kk, p=p: (i, kk + p * nk)) for p in range(n_parts)]
    b_spec = pl.BlockSpec((tn, tk), lambda i, j, kk: (j, kk)) if dn == NT else pl.BlockSpec((tk, tn), lambda i, j, kk: (kk, j))
    in_specs = a_specs + [b_spec] + [pl.BlockSpec((tm, tn), lambda i, j, kk: (i, j)) for _ in tiles]
    in_specs += [pl.BlockSpec((1, tn), lambda i, j, kk: (0, j)) for _ in rows] + [HBM_SPEC] * n_ci
    res = pl.pallas_call(
        body, name=name, grid=grid, in_specs=in_specs,
        out_specs=[pl.BlockSpec((tm, tn), lambda i, j, kk: (i, j))] + [pl.BlockSpec((1, tn), lambda i, j, kk: (0, 0))] * n_sum
        + [pl.BlockSpec((tm, tk), lambda i, j, kk: (i, kk))] * n_keep + [HBM_SPEC] * n_co,
        out_shape=[SDS((m, n), out_dtype)] + [SDS((1, n), F32)] * n_sum + [SDS((m, k), BF16)] * n_keep
        + (list(carry.outs) if carry else []),
        scratch_shapes=([pltpu.VMEM((tm, tn), F32)] if nk > 1 else []) + (list(carry.sems) if carry else []),
        compiler_params=pltpu.CompilerParams(
            dimension_semantics=("arbitrary",) * 3 if (carry or n_sum) else ("parallel", "parallel", "arbitrary"),
            vmem_limit_bytes=VMEM_LIMIT),
    )(*[a] * n_parts, b, *tiles, *rows, *(carry.ins if carry else []))
    if carry:
        carry.then(res[1 + n_sum + n_keep:])
    return tuple(res[:1 + n_sum + n_keep]) if n_sum + n_keep else res[0]


def _mean_all(v):
    return jnp.mean(v, axis=-1, keepdims=True)


def _head_sums(v, bd):
    w = bd.shape[0]
    return jnp.concatenate([_dot01(v[:, j:j + w], bd, 2) for j in range(0, v.shape[1], w)], axis=1)


def _mean_heads(bd):
    return lambda v: _head_sums(v, bd) * (1.0 / HD)


def _rms_fwd(x, g, mean):
    return x * lax.rsqrt(mean(x * x) + EPS) * g


def _rms_bwd(x, g, dy, mean):
    r = lax.rsqrt(mean(x * x) + EPS)
    dn = dy * g
    dx = r * dn - x * (r * r * r) * mean(dn * x)
    return dx, jnp.sum(dy * x * r, axis=0, keepdims=True)


def _swap_halves(x):
    w = x.shape[1]
    lane = lax.broadcasted_iota(jnp.int32, x.shape, 1)
    return jnp.where(lane % HD < HD // 2, pltpu.roll(x, w - HD // 2, 1), pltpu.roll(x, HD // 2, 1))


def _lanes(t, w):
    return jnp.tile(t, (1, w // t.shape[1]))


def _rope_fwd(x, cos, sin_signed):
    return x * _lanes(cos, x.shape[1]) + _swap_halves(x) * _lanes(sin_signed, x.shape[1])


def _rope_bwd(dy, cos, sin_signed):
    return dy * _lanes(cos, dy.shape[1]) + _swap_halves(dy * _lanes(sin_signed, dy.shape[1]))


def _bcast_heads(cols):
    return jnp.concatenate([jnp.broadcast_to(c, (c.shape[0], HD)) for c in cols], axis=1)


def _softplus(z):
    return jnp.maximum(z, 0.0) + jnp.log(1.0 + jnp.exp(-jnp.abs(z)))


def _block_diag(w):
    h = np.arange(w) // HD
    return jnp.asarray(h[:, None] == h[None, :], BF16)


def _sb_window(i, t):
    hi = (i + 1) * SB_QB - t * SB_WIN
    lo = hi - SB_WIN
    ws = pl.multiple_of(jnp.maximum(lo, 0), SB_QB)
    kpos = ws + lax.broadcasted_iota(jnp.int32, (SB_QB, SB_WIN), 1)
    qpos = i * SB_QB + lax.broadcasted_iota(jnp.int32, (SB_QB, SB_WIN), 0)
    return (kpos < qpos) & (kpos >= lo) & (kpos < hi), ws


def _sb_fwd(qkv):
    s = qkv.shape[0]
    assert s >= SB_WIN
    nq = s // SB_QB
    nh = SB_FWD_HEADS
    bw = HD * nh
    ngroups = SB_W // bw

    def body(q_ref, k_ref, v_ref, later_ref, o_ref, tot_ref, w0_ref, b0_ref, nb_ref):
        p, i = pl.program_id(0), pl.program_id(1)
        q = q_ref[...]
        later_of = later_ref[...]

        def window(t, tots, outs, keep):
            mask, ws = _sb_window(i, t)
            kw, vw = k_ref[pl.ds(ws, SB_WIN), :], v_ref[pl.ds(ws, SB_WIN), :]
            new_t, new_o, w_all, b_all = [], [], [], []
            for hh in range(nh):
                sl = slice(HD * hh, HD * hh + HD)
                z = _dot(q[:, sl], kw[:, sl], NT) * SCALE
                sp = _softplus(z)
                lf = jnp.where(mask, -sp, 0.0)
                lf_far, lf_near = lf[:, :SB_WIN // 2], lf[:, SB_WIN // 2:]
                later = tots[hh] + jnp.concatenate(
                    [_dot01(lf_far, later_of, 2) + jnp.sum(lf_near, axis=1, keepdims=True), _dot01(lf_near, later_of, 2)], axis=1)
                w = jnp.where(mask, jnp.exp(z - sp + later), 0.0).astype(BF16)
                new_o.append(outs[hh] + _dot(w, vw[:, sl]))
                new_t.append(tots[hh] + jnp.sum(lf, axis=1, keepdims=True))
                if keep:
                    w_all.append(w)
                    b_all.append(jnp.where(mask, jnp.exp(z - sp), 0.0).astype(BF16))
            if keep:
                w0_ref[...] = jnp.concatenate(w_all, axis=1)
                b0_ref[...] = jnp.concatenate(b_all, axis=1)
            alive = functools.reduce(jnp.maximum, [jnp.max(v) for v in new_t])
            return t + 1, alive, tuple(new_t), tuple(new_o)

        zt, zo = jnp.zeros((SB_QB, 1), F32), jnp.zeros((SB_QB, HD), F32)
        first = window(jnp.int32(0), (zt,) * nh, (zo,) * nh, True)
        t, _, tots, outs = lax.while_loop(lambda c: ((i + 1) * SB_QB - c[0] * SB_WIN > 0) & (c[1] > EXHAUSTED),
                                          lambda c: window(c[0], c[2], c[3], False), first)
        o_ref[...] = jnp.concatenate(outs, axis=1).astype(o_ref.dtype)
        tot_ref[...] = _bcast_heads(tots)
        nb_ref[p, i] = t

    whole = lambda off: pl.BlockSpec((s, bw), lambda p, i: (0, off + p), pipeline_mode=pl.Buffered(1))
    tile = pl.BlockSpec((SB_QB, bw), lambda p, i: (i, p))
    tri = pl.BlockSpec((SB_WIN // 2, SB_WIN // 2), lambda p, i: (0, 0), pipeline_mode=pl.Buffered(1))
    near = pl.BlockSpec((SB_QB, nh * SB_WIN), lambda p, i: (i, p))
    n_heads = SB_W // HD
    idx = np.arange(SB_WIN // 2)
    return pl.pallas_call(
        body, name="sb_fwd", grid=(ngroups, nq),
        in_specs=[tile, whole(ngroups), whole(2 * ngroups), tri],
        out_specs=[tile, tile, near, near, pl.BlockSpec(memory_space=pltpu.SMEM)],
        out_shape=[SDS((s, SB_W), BF16), SDS((s, SB_W), F32), SDS((s, n_heads * SB_WIN), BF16), SDS((s, n_heads * SB_WIN), BF16),
                   SDS((ngroups, nq), jnp.int32)],
        compiler_params=pltpu.CompilerParams(dimension_semantics=("arbitrary", "arbitrary"), vmem_limit_bytes=VMEM_LIMIT),
    )(qkv, qkv, qkv, jnp.asarray(idx[:, None] > idx[None, :], BF16))


def _sb_bwd(qkv, do, tot, nblk, w0, b0, buf, col):
    s = qkv.shape[0]
    nq = s // SB_QB
    npairs = SB_W // 128

    def body(nb_ref, q_ref, k_ref, v_ref, do_ref, tot_ref, upto_ref, before_ref, w0_ref, b0_ref, buf_ref,
             dq_ref, dk_ref, dv_ref):
        p, i = pl.program_id(0), pl.program_id(1)

        @pl.when(i == 0)
        def _():
            dk_ref[...] = jnp.zeros(dk_ref.shape, F32)
            dv_ref[...] = jnp.zeros(dv_ref.shape, F32)

        upto = upto_ref[...]
        before = before_ref[...]
        q, dout, tt = q_ref[...], do_ref[...], tot_ref[...]
        n = nb_ref[p * 2 // SB_FWD_HEADS, i]

        def step(it, c):
            pres, gpres, dqs = c
            mask, ws = _sb_window(i, n - 1 - it)
            kw, vw = k_ref[pl.ds(ws, SB_WIN), :], v_ref[pl.ds(ws, SB_WIN), :]
            new_p, new_g, new_dq, dks, dvs = [], [], [], [], []
            for hh in range(2):
                sl = slice(HD * hh, HD * hh + HD)
                z = _dot(q[:, sl], kw[:, sl], NT) * SCALE
                sp = _softplus(z)
                lf = jnp.where(mask, -sp, 0.0)
                later = tt[:, HD * hh:HD * hh + 1] - (pres[hh] + _dot01(lf, upto, 2))
                w = jnp.where(mask, jnp.exp(z - sp + later), 0.0)
                beta = jnp.exp(z - sp)
                g = _dot(dout[:, sl], vw[:, sl], NT) * w
                g_far = gpres[hh] + _dot(g.astype(BF16), before)
                dz = (jnp.where(mask, g * (1.0 - beta) - beta * g_far, 0.0) * SCALE).astype(BF16)
                new_dq.append(dqs[hh] + _dot(dz, kw[:, sl]))
                dks.append(_dot(dz, q[:, sl], TN))
                dvs.append(_dot(w.astype(BF16), dout[:, sl], TN))
                new_p.append(pres[hh] + jnp.sum(lf, axis=1, keepdims=True))
                new_g.append(gpres[hh] + jnp.sum(g, axis=1, keepdims=True))
            dk_ref[pl.ds(ws, SB_WIN), :] += jnp.concatenate(dks, axis=1)
            dv_ref[pl.ds(ws, SB_WIN), :] += jnp.concatenate(dvs, axis=1)
            return tuple(new_p), tuple(new_g), tuple(new_dq)

        zt, zo = jnp.zeros((SB_QB, 1), F32), jnp.zeros((SB_QB, HD), F32)
        _, gpres, dqs = lax.fori_loop(0, n - 1, step, ((zt, zt), (zt, zt), (zo, zo)))
        _, ws = _sb_window(i, 0)
        kw, vw = k_ref[pl.ds(ws, SB_WIN), :], v_ref[pl.ds(ws, SB_WIN), :]
        dqs, dks, dvs = list(dqs), [], []
        for hh in range(2):
            sl = slice(HD * hh, HD * hh + HD)
            w = w0_ref[:, SB_WIN * hh:SB_WIN * (hh + 1)]
            beta = b0_ref[:, SB_WIN * hh:SB_WIN * (hh + 1)].astype(F32)
            g = _dot(dout[:, sl], vw[:, sl], NT) * w.astype(F32)
            g16, half = g.astype(BF16), SB_WIN // 2
            g_far = gpres[hh] + jnp.concatenate(
                [_dot(g16[:, :half], before[:half, :half]),
                 _dot(g16[:, half:], before[:half, :half]) + jnp.sum(g[:, :half], axis=1, keepdims=True)], axis=1)
            dz = ((g * (1.0 - beta) - beta * g_far) * SCALE).astype(BF16)
            dqs[hh] = dqs[hh] + _dot(dz, kw[:, sl])
            dks.append(_dot(dz, q[:, sl], TN))
            dvs.append(_dot(w, dout[:, sl], TN))
        dk_ref[pl.ds(ws, SB_WIN), :] += jnp.concatenate(dks, axis=1)
        dv_ref[pl.ds(ws, SB_WIN), :] += jnp.concatenate(dvs, axis=1)
        dq_ref[...] = jnp.concatenate(dqs, axis=1).astype(dq_ref.dtype)

    whole_in = lambda off: pl.BlockSpec((s, 128), lambda p, i: (0, off + p), pipeline_mode=pl.Buffered(1))
    whole_out = pl.BlockSpec((s, 128), lambda p, i: (0, p), pipeline_mode=pl.Buffered(1))
    tile = pl.BlockSpec((SB_QB, 128), lambda p, i: (i, p))
    near = pl.BlockSpec((SB_QB, 2 * SB_WIN), lambda p, i: (i, p))
    dq_tile = pl.BlockSpec((SB_QB, 128), lambda p, i: (i, col // 128 + p))
    tri = pl.BlockSpec((SB_WIN, SB_WIN), lambda p, i: (0, 0), pipeline_mode=pl.Buffered(1))
    idx = np.arange(SB_WIN)
    return pl.pallas_call(
        body, name="sb_bwd", grid=(npairs, nq),
        in_specs=[pl.BlockSpec(memory_space=pltpu.SMEM), tile, whole_in(npairs), whole_in(2 * npairs), tile, tile, tri, tri,
                  near, near, HBM_SPEC],
        out_specs=[dq_tile, whole_out, whole_out],
        out_shape=[SDS(buf.shape, buf.dtype)] + [SDS((s, SB_W), F32)] * 2,
        input_output_aliases={10: 0},
        compiler_params=pltpu.CompilerParams(dimension_semantics=("arbitrary", "arbitrary"), vmem_limit_bytes=VMEM_LIMIT),
    )(nblk, qkv, qkv, qkv, do, tot, jnp.asarray(idx[:, None] <= idx[None, :], BF16), jnp.asarray(idx[:, None] < idx[None, :], BF16),
      w0, b0, buf)


def _dsa_mask(DSA_T, has_prev):
    r = lax.broadcasted_iota(jnp.int32, (DSA_T, QB + DSA_T), 0)
    j = lax.broadcasted_iota(jnp.int32, (DSA_T, QB + DSA_T), 1) - QB
    return (j <= r) & (j >= r - QB) & ((j >= 0) | has_prev)


def _dsa_fwd(q, k, v, dil):
    n = q.shape[0]
    DSA_T = DSA_T_FWD
    nt = n // DSA_T

    def body(q_ref, kc_ref, kp_ref, vc_ref, vp_ref, o_ref, lse_ref):
        mask = _dsa_mask(DSA_T, pl.program_id(1) > 0)
        outs, lses = [], []
        for hh in range(DSA_OUT_W // HD):
            sl = slice(HD * hh, HD * hh + HD)
            kcat = jnp.concatenate([kp_ref[:, sl], kc_ref[:, sl]], axis=0)
            vcat = jnp.concatenate([vp_ref[:, sl], vc_ref[:, sl]], axis=0)
            sc = jnp.where(mask, _dot(q_ref[:, sl], kcat, NT) * SCALE, NEG)
            m = jnp.max(sc, axis=1, keepdims=True)
            p = jnp.exp(sc - m)
            den = jnp.sum(p, axis=1, keepdims=True)
            outs.append(_dot(p.astype(BF16), vcat) / den)
            lses.append(m + jnp.log(den))
        o_ref[...] = jnp.concatenate(outs, axis=1)
        lse_ref[...] = _bcast_heads(lses)

    cur = pl.BlockSpec((DSA_T, DSA_OUT_W), lambda c, i: (i, c))
    prev = pl.BlockSpec((QB, DSA_OUT_W), lambda c, i: (jnp.maximum(i * (DSA_T // QB) - 1, 0), c))
    o, lse = pl.pallas_call(
        body, name=f"dsa_fwd_d{dil}", grid=(dil, nt), in_specs=[cur, cur, prev, cur, prev], out_specs=[cur, cur],
        out_shape=[SDS((n, dil * DSA_OUT_W), F32)] * 2,
        compiler_params=pltpu.CompilerParams(dimension_semantics=("parallel", "parallel")),
    )(q, k, k, v, v)
    return o, lse


def _dsa_bwd(q, k, v, do, cc, lse, dil):
    n = q.shape[0]
    DSA_T = DSA_T_BWD
    nt = n // DSA_T
    per = DSA_T // QB

    def body(qj_ref, qn_ref, kp_ref, kj_ref, vp_ref, vj_ref, doj_ref, don_ref, cj_ref, cn_ref, lj_ref, ln_ref,
             dq_ref, dk_ref, dv_ref):
        j = pl.program_id(1)
        mask = _dsa_mask(DSA_T, j > 0)
        r = lax.broadcasted_iota(jnp.int32, (QB, DSA_T), 0)
        kk = lax.broadcasted_iota(jnp.int32, (QB, DSA_T), 1)
        m_next = (kk >= r + QB) & (j + 1 < nt)
        dqs, dks, dvs = [], [], []
        for hh in range(DSA_OUT_W // HD):
            sl = slice(HD * hh, HD * hh + HD)
            one = slice(HD * hh, HD * hh + 1)
            qj, qn, kj, vj, doj, don = (t[:, sl] for t in (qj_ref, qn_ref, kj_ref, vj_ref, doj_ref, don_ref))
            kcat = jnp.concatenate([kp_ref[:, sl], kj], axis=0)
            vcat = jnp.concatenate([vp_ref[:, sl], vj], axis=0)
            p1 = jnp.where(mask, jnp.exp(_dot(qj, kcat, NT) * SCALE - lj_ref[:, one]), 0.0)
            ds1 = (p1 * (_dot(doj, vcat, NT) + cj_ref[:, one]) * SCALE).astype(BF16)
            p2 = jnp.where(m_next, jnp.exp(_dot(qn, kj, NT) * SCALE - ln_ref[:, one]), 0.0)
            ds2 = (p2 * (_dot(don, vj, NT) + cn_ref[:, one]) * SCALE).astype(BF16)
            dqs.append(_dot(ds1, kcat))
            dks.append(_dot(ds1[:, QB:], qj, TN) + _dot(ds2, qn, TN))
            dvs.append(_dot(p1[:, QB:].astype(BF16), doj, TN) + _dot(p2.astype(BF16), don, TN))
        dq_ref[...] = jnp.concatenate(dqs, axis=1).astype(dq_ref.dtype)
        dk_ref[...] = jnp.concatenate(dks, axis=1).astype(dk_ref.dtype)
        dv_ref[...] = jnp.concatenate(dvs, axis=1).astype(dv_ref.dtype)

    cur = pl.BlockSpec((DSA_T, DSA_OUT_W), lambda c, j: (j, c))
    prev = pl.BlockSpec((QB, DSA_OUT_W), lambda c, j: (jnp.maximum(j * per - 1, 0), c))
    nxt = pl.BlockSpec((QB, DSA_OUT_W), lambda c, j: (jnp.minimum((j + 1) * per, n // QB - 1), c))
    dq, dk, dv = pl.pallas_call(
        body, name=f"dsa_bwd_d{dil}", grid=(dil, nt),
        in_specs=[cur, nxt, prev, cur, prev, cur, cur, nxt, cur, nxt, cur, nxt], out_specs=[cur, cur, cur],
        out_shape=[SDS((n, dil * DSA_OUT_W), BF16)] * 3,
        compiler_params=pltpu.CompilerParams(dimension_semantics=("parallel", "parallel")),
    )(q, q, k, k, v, v, do, do, cc, cc, lse, lse)
    return dq, dk, dv


def _norm_bwd_epi(acc, xv, dyv, g):
    dx, dg = _rms_bwd(xv, g, acc, _mean_all)
    return dx + dyv, dg


def _ffn_fwd(tag, x, gain, w13, w2, plan=None, target=None):
    n = _tokmap(f"{tag}_norm", lambda xv, g: _rms_fwd(xv, g, _mean_all), [x], [gain], [(D, BF16)])[0]
    ab = _matmul(f"{tag}_up", n, w13, NN, BF16, plan=plan)

    def gate(av, bv):
        a, b = av.astype(F32), bv.astype(F32)
        return (a * jax.nn.sigmoid(a) * b).astype(BF16)

    if target is None:
        y, h = _matmul(f"{tag}_down", ab, w2(), NN, F32, epi=lambda acc, res: res + 0.5 * acc, tiles=[x], plan=plan,
                       a_pro=(gate, 2))
        return y, (n, ab, h)

    def loss_epi(acc, res, tv):
        e = res + 0.5 * acc - tv
        return e * (1.0 / D), (0.5 / D) * jnp.sum(e * e, axis=0, keepdims=True)

    dy, loss_row, h = _matmul(f"{tag}_down", ab, w2(), NN, F32, epi=loss_epi, tiles=[x, target], plan=plan, a_pro=(gate, 2),
                              n_sum=1)
    return (dy, loss_row), (n, ab, h)


def _ffn_bwd(tag, x, gain, w13, w2, saved, dy, plan=None, on_dw=None, final=False):
    n, ab, h = saved
    dh = _matmul(f"{tag}_bwd_dh", dy, w2, NT, BF16, epi=lambda acc: 0.5 * acc)

    def gate_bwd(abv, dhv):
        a, b, dhf = abv[:, :D_FF].astype(F32), abv[:, D_FF:].astype(F32), dhv.astype(F32)
        sg = jax.nn.sigmoid(a)
        da = dhf * b * (sg * (1.0 + a * (1.0 - sg)))
        return jnp.concatenate([da, dhf * (a * sg)], axis=1)

    dab = _tokmap(f"{tag}_bwd_gate", gate_bwd, [ab, dh], [], [(2 * D_FF, BF16)], tile=256)[0]
    dw2 = _matmul(f"{tag}_bwd_dw2", h, dy, TN, F32, epi=lambda acc: 0.5 * acc)
    dw13 = _matmul(f"{tag}_bwd_dw13", n, dab, TN, F32, plan=plan)
    if on_dw is not None:
        on_dw(dw13, dw2)
    if not final:
        dx, dgain = _matmul(f"{tag}_bwd_dn", dab, w13, NT, F32, epi=_norm_bwd_epi, tiles=[x, dy], rows=[gain], plan=plan,
                            n_sum=1)
    else:
        dn = _matmul(f"{tag}_bwd_dn", dab, w13, NT, F32, plan=plan)
        dx, dgain = _tokmap(f"{tag}_bwd_norm", lambda xv, dnv, dyv, g: _norm_bwd_epi(dnv, xv, dyv, g), [x, dn, dy], [gain],
                            [(D, F32)], [(1, D)])
    return dx, dgain, dw13, dw2


def _rope_tables(s):
    half = HD // 2
    inv_freq = jnp.power(10000.0, -jnp.arange(half, dtype=F32) / half)
    ang = jnp.arange(s).astype(F32)[:, None] * inv_freq[None, :]
    cos, sin = jnp.cos(ang), jnp.sin(ang)
    return jnp.tile(jnp.concatenate([cos, cos], axis=1), (1, 2)), jnp.tile(jnp.concatenate([-sin, sin], axis=1), (1, 2))


def _local_step(x, mem, tgt, w, sm, plan=None, on_grads=None):
    s = x.shape[0]
    assert s % (max(DSA_T_FWD, DSA_T_BWD) * max(DSA_DILS)) == 0
    on_grads = on_grads or (lambda group, grads: None)
    c_sb, c_dsa, c_qm, c_all = 3 * D, 3 * D + 3 * SB_W, 3 * D + 3 * SB_W + 3 * DSA_W, 3 * D + 4096
    cos, sin = _rope_tables(s)
    bd768 = bd256 = _block_diag(128)
    gq_dsa, gk_dsa = jnp.tile(sm["qn_dsa"], (1, DSA_W // HD)), jnp.tile(sm["kn_dsa"], (1, DSA_W // HD))
    gq_mem, gk_mem = jnp.tile(sm["qn_mem"], (1, MEM_W // HD)), jnp.tile(sm["kn_mem"], (1, MEM_W // HD))

    w13_1 = jnp.concatenate([w["ffn1_w1"], w["ffn1_w3"]], axis=1)
    x1, ffn1_saved = _ffn_fwd("ffn1", x, sm["ffn1_norm"], w13_1, lambda: w["ffn1_w2"], plan)
    w_all = jnp.concatenate([w["w_gate"], w["w_in"]], axis=1)
    wb_sb, wb_dsa, wb_mem = w["w_branch_sb"], w["w_branch_dsa"], w["w_branch_mem"]
    hmix = _tokmap("mix_norm", lambda xv, g: _rms_fwd(xv, g, _mean_all), [x1], [sm["mix_norm"]], [(D, BF16)])[0]
    qkv_sb = _matmul("proj_sb", hmix, w_all[:, c_sb:c_dsa], NN, BF16)
    qkv_dsa = _matmul("proj_dsa", hmix, w_all[:, c_dsa:c_qm], NN, BF16, plan=plan)
    q_mem = _matmul("proj_qmem", hmix, w_all[:, c_qm:], NN, BF16)
    gpre = _matmul("proj_gate", hmix, w_all[:, :c_sb], NN, BF16, epi=lambda acc, b: acc + b, rows=[sm["b_gate"]], plan=plan)

    o_sb, sb_tot, sb_w0, sb_b0, sb_nblk = _sb_fwd(qkv_sb)

    def dsa_prep(qkv, cs, sn, gq, gk, bd):
        mean = _mean_heads(bd)
        qn = _rope_fwd(_rms_fwd(qkv[:, :DSA_W].astype(F32), gq, mean), cs, sn)
        kn = _rope_fwd(_rms_fwd(qkv[:, DSA_W:2 * DSA_W].astype(F32), gk, mean), cs, sn)
        v = qkv[:, 2 * DSA_W:]
        outs = []
        for t in (qn, kn, v):
            outs += [t[:, DSA_OUT_W * g:DSA_OUT_W * (g + 1)] for g in range(3)]
        return outs

    dsa_in = _tokmap("dsa_prep", dsa_prep, [qkv_dsa, cos, sin], [gq_dsa, gk_dsa, bd768], [(DSA_OUT_W, BF16)] * 9, tile=512,
                     dil_outs={j: DSA_DILS[j % 3] for j in range(9)})
    dsa_q, dsa_k, dsa_v = dsa_in[0:3], dsa_in[3:6], dsa_in[6:9]
    dsa_o, dsa_lse = zip(*[_dsa_fwd(dsa_q[g], dsa_k[g], dsa_v[g], DSA_DILS[g]) for g in range(3)])

    def alphas(l0, l1, l2):
        m = jnp.maximum(jnp.maximum(l0, l1), l2)
        e = [jnp.exp(l - m) for l in (l0, l1, l2)]
        tot = e[0] + e[1] + e[2]
        return [t / tot for t in e]

    def dsa_mix(o0, o1, o2, l0, l1, l2):
        a = alphas(l0, l1, l2)
        return a[0] * o0 + a[1] * o1 + a[2] * o2

    o_dsa = _tokmap("dsa_mix", dsa_mix, [*dsa_o, *dsa_lse], [], [(DSA_OUT_W, BF16)], tile=512,
                    dil_ins={j: DSA_DILS[j % 3] for j in range(6)})[0]

    def mem_kv(memv, g, wkv, gk, bd):
        kv = _dot(_rms_fwd(memv, g, _mean_all).astype(BF16), wkv)
        return _rms_fwd(kv[:, :MEM_W], gk, _mean_heads(bd)), kv[:, MEM_W:]

    km, vm = _tokmap("mem_kv", mem_kv, [mem], [sm["mem_norm"], w["w_mem_kv"], gk_mem, bd256], [(MEM_W, BF16)] * 2)

    def mem_probs(qv, kmv, gq, bd):
        qn = _rms_fwd(qv.astype(F32), gq, _mean_heads(bd)).astype(BF16)
        ps = []
        for h in range(MEM_W // HD):
            sl = slice(HD * h, HD * h + HD)
            sc = _dot(qn[:, sl], kmv[:, sl], NT) * SCALE
            e = jnp.exp(sc - jnp.max(sc, axis=1, keepdims=True))
            ps.append(e / jnp.sum(e, axis=1, keepdims=True))
        return qn, ps

    def mem_attn(qv, kmv, vmv, gq, bd):
        _, ps = mem_probs(qv, kmv, gq, bd)
        return jnp.concatenate([_dot(p.astype(BF16), vmv[:, HD * h:HD * h + HD]) for h, p in enumerate(ps)], axis=1)

    o_mem = _tokmap("mem_attn", mem_attn, [q_mem], [km, vm, gq_mem, bd256], [(MEM_W, BF16)])[0]

    def merge(osb, odsa, omem, gp, w_sb, w_dsa, w_mem):
        gates = jax.nn.sigmoid(gp.astype(F32))
        ys = (_dot(osb, w_sb), _dot(odsa, w_dsa), _dot(omem, w_mem))
        return gates, ys, gates[:, :D] * ys[0] + gates[:, D:2 * D] * ys[1] + gates[:, 2 * D:] * ys[2]

    merged = _tokmap("merge", lambda *a: merge(*a)[2], [o_sb, o_dsa, o_mem, gpre], [wb_sb, wb_dsa, wb_mem], [(D, BF16)],
                     tile=512)[0]
    x2 = _matmul("out_proj", merged, w["w_out"], NN, F32, epi=lambda acc, res: res + acc, tiles=[x1])
    w13_2 = jnp.concatenate([w["ffn2_w1"], w["ffn2_w3"]], axis=1)
    (dy, loss_row), ffn2_saved = _ffn_fwd("ffn2", x2, sm["ffn2_norm"], w13_2, lambda: w["ffn2_w2"], target=tgt)
    loss = jnp.sum(loss_row).reshape(1, 1)

    gw, gs = {}, {}
    def ffn_grads(tag):
        def on_dw(dw13, dw2):
            gw[f"{tag}_w1"], gw[f"{tag}_w3"], gw[f"{tag}_w2"] = dw13[:, :D_FF], dw13[:, D_FF:], dw2
            on_grads(tag, {n: gw[n] for n in (f"{tag}_w1", f"{tag}_w3", f"{tag}_w2")})
        return on_dw

    dx2, gs["ffn2_norm"], _, _ = _ffn_bwd("ffn2", x2, sm["ffn2_norm"], w13_2, w["ffn2_w2"], ffn2_saved, dy, plan,
                                          ffn_grads("ffn2"))
    dmerged = _matmul("out_proj_bwd_dx", dx2, w["w_out"], NT, BF16)
    gw["w_out"] = _matmul("out_proj_bwd_dw", merged, dx2, TN, F32)

    def merge_bwd(osb, odsa, omem, gp, dm, w_sb, w_dsa, w_mem):
        gates, ys, _ = merge(osb, odsa, omem, gp, w_sb, w_dsa, w_mem)
        dmf = dm.astype(F32)
        dgp, dos, dws = [], [], []
        for b, (ov, wv) in enumerate(((osb, w_sb), (odsa, w_dsa), (omem, w_mem))):
            gb = gates[:, D * b:D * (b + 1)]
            dgp.append(dmf * ys[b] * gb * (1.0 - gb))
            dyb = (dmf * gb).astype(BF16)
            dos.append(_dot(dyb, wv, NT))
            dws.append(_dot(ov, dyb, TN))
        dgp = jnp.concatenate(dgp, axis=1)
        return dos[0], dos[1], dos[2], dgp, dws[0], dws[1], dws[2], jnp.sum(dgp, axis=0, keepdims=True)

    do_sb, do_dsa, do_mem, dgpre, gw["w_branch_sb"], gw["w_branch_dsa"], gw["w_branch_mem"], gs["b_gate"] = _tokmap(
        "merge_bwd", merge_bwd, [o_sb, o_dsa, o_mem, gpre, dmerged], [wb_sb, wb_dsa, wb_mem],
        [(SB_W, BF16), (DSA_OUT_W, F32), (MEM_W, BF16), (3 * D, BF16)],
        [(SB_W, D), (DSA_OUT_W, D), (MEM_W, D), (1, 3 * D)], tile=512, place={3: (c_all, 0, None)})

    dall, dk_sb, dv_sb = _sb_bwd(qkv_sb, do_sb, sb_tot, sb_nblk, sb_w0, sb_b0, dgpre, c_sb)
    dall = lax.dynamic_update_slice(dall, dk_sb.astype(BF16), (0, c_sb + SB_W))
    dall = lax.dynamic_update_slice(dall, dv_sb.astype(BF16), (0, c_sb + 2 * SB_W))

    def dsa_mix_bwd(o0, o1, o2, l0, l1, l2, dov, bd):
        a = alphas(l0, l1, l2)
        omix = a[0] * o0 + a[1] * o1 + a[2] * o2
        dot_o = _head_sums(dov * omix, bd)
        return [dov * t for t in a] + [-t * dot_o for t in a]

    mixb = _tokmap("dsa_mix_bwd", dsa_mix_bwd, [*dsa_o, *dsa_lse, do_dsa], [bd256],
                   [(DSA_OUT_W, BF16)] * 3 + [(DSA_OUT_W, F32)] * 3, tile=512,
                   dil_ins={j: DSA_DILS[j % 3] for j in range(6)}, dil_outs={j: DSA_DILS[j % 3] for j in range(6)})
    dsa_d = [_dsa_bwd(dsa_q[g], dsa_k[g], dsa_v[g], mixb[g], mixb[3 + g], dsa_lse[g], DSA_DILS[g]) for g in range(3)]

    def dsa_prep_bwd(qkv, cs, sn, *rest):
        dqs, dks, dvs, (gq, gk, bd) = rest[0:3], rest[3:6], rest[6:9], rest[9:]
        mean = _mean_heads(bd)
        dq, dgq = _rms_bwd(qkv[:, :DSA_W].astype(F32), gq, _rope_bwd(jnp.concatenate(dqs, axis=1), cs, sn), mean)
        dk, dgk = _rms_bwd(qkv[:, DSA_W:2 * DSA_W].astype(F32), gk, _rope_bwd(jnp.concatenate(dks, axis=1), cs, sn), mean)
        return jnp.concatenate([dq, dk] + list(dvs), axis=1), dgq, dgk

    dall, dgq_dsa, dgk_dsa = _tokmap(
        "dsa_prep_bwd", dsa_prep_bwd,
        [qkv_dsa, cos, sin] + [dsa_d[g][t] for t in range(3) for g in range(3)], [gq_dsa, gk_dsa, bd768],
        [(3 * DSA_W, BF16)], [(1, DSA_W), (1, DSA_W)], tile=512, dil_ins={3 + j: DSA_DILS[j % 3] for j in range(9)},
        place={0: (c_all, c_dsa // (3 * DSA_W), dall)})
    gs["qn_dsa"] = dgq_dsa.reshape(DSA_W // HD, HD).sum(axis=0, keepdims=True)
    gs["kn_dsa"] = dgk_dsa.reshape(DSA_W // HD, HD).sum(axis=0, keepdims=True)

    def mem_attn_bwd(qv, dov, kmv, vmv, gq, bd):
        qn, ps = mem_probs(qv, kmv, gq, bd)
        dqn, dkm, dvm = [], [], []
        for h, p in enumerate(ps):
            sl = slice(HD * h, HD * h + HD)
            dp = _dot(dov[:, sl], vmv[:, sl], NT)
            ds = (p * (dp - jnp.sum(p * dp, axis=1, keepdims=True)) * SCALE).astype(BF16)
            dqn.append(_dot(ds, kmv[:, sl]))
            dkm.append(_dot(ds, qn[:, sl], TN))
            dvm.append(_dot(p.astype(BF16), dov[:, sl], TN))
        dq, dgq = _rms_bwd(qv.astype(F32), gq, jnp.concatenate(dqn, axis=1), _mean_heads(bd))
        return dq, jnp.concatenate(dkm, axis=1), jnp.concatenate(dvm, axis=1), dgq

    dall, dkm, dvm, dgq_mem = _tokmap("mem_attn_bwd", mem_attn_bwd, [q_mem, do_mem], [km, vm, gq_mem, bd256],
                                      [(MEM_W, BF16)], [(MEM_LEN, MEM_W), (MEM_LEN, MEM_W), (1, MEM_W)],
                                      place={0: (c_all, c_qm // MEM_W, dall)})
    gs["qn_mem"] = dgq_mem.reshape(MEM_W // HD, HD).sum(axis=0, keepdims=True)

    def mem_kv_bwd(memv, dkmv, dvmv, g, wkv, gk, bd):
        memn = _rms_fwd(memv, g, _mean_all).astype(BF16)
        kv = _dot(memn, wkv)
        dk, dgk = _rms_bwd(kv[:, :MEM_W], gk, dkmv, _mean_heads(bd))
        dkv = jnp.concatenate([dk, dvmv], axis=1).astype(BF16)
        _, dg = _rms_bwd(memv, g, _dot(dkv, wkv, NT), _mean_all)
        return _dot(memn, dkv, TN), dg, dgk

    gw["w_mem_kv"], gs["mem_norm"], dgk_mem = _tokmap(
        "mem_kv_bwd", mem_kv_bwd, [mem, dkm, dvm], [sm["mem_norm"], w["w_mem_kv"], gk_mem, bd256], [],
        [(D, 2 * MEM_W), (1, D), (1, MEM_W)])
    gs["kn_mem"] = dgk_mem.reshape(MEM_W // HD, HD).sum(axis=0, keepdims=True)

    dx1, gs["mix_norm"] = _matmul("proj_bwd_dx", dall, w_all, NT, F32, epi=_norm_bwd_epi, tiles=[x1, dx2],
                                  rows=[sm["mix_norm"]], n_sum=1)
    dw_all = _matmul("proj_bwd_dw", hmix, dall, TN, F32)
    gw["w_gate"], gw["w_in"] = dw_all[:, :c_sb], dw_all[:, c_sb:]
    on_grads("mid", {n: gw[n] for n in GROUPS["mid"]})
    gx, gs["ffn1_norm"], _, _ = _ffn_bwd("ffn1", x, sm["ffn1_norm"], w13_1, w["ffn1_w2"], ffn1_saved, dx1, plan,
                                         ffn_grads("ffn1"), final=True)
    return loss, gx, gw, gs


def _shard_shape(name):
    shape, axis = SHARDED_BY_NAME[name]
    return (shape[0] // N_CHIPS, shape[1]) if axis == 0 else (shape[0], shape[1] // N_CHIPS)


def _full_from_shards(name, shards):
    axis = SHARDED_BY_NAME[name][1]
    return shards.reshape(SHARDED_BY_NAME[name][0]) if axis == 0 else jnp.concatenate(list(shards), axis=1)


def _shards_from_full(name, full, dtype):
    axis, n = SHARDED_BY_NAME[name][1], _shard_shape(name)
    return jnp.stack([lax.slice_in_dim(full, c * n[axis], (c + 1) * n[axis], axis=axis).astype(dtype) for c in range(N_CHIPS)])


def _own_shard(name, full, chip):
    axis, n = SHARDED_BY_NAME[name][1], _shard_shape(name)
    return lax.dynamic_slice_in_dim(full, chip * n[axis], n[axis], axis=axis)


SMALL_USED = sum(n for _, n in SMALL)


def _pack_small(d, loss=None):
    parts = [d[n].reshape(-1) for n, _ in SMALL]
    parts.append(jnp.zeros((1,), F32) if loss is None else loss.reshape(1))
    parts.append(jnp.zeros((SMALL_ROWS * D - SMALL_USED - 1,), F32))
    return jnp.concatenate(parts).reshape(SMALL_ROWS, D)


def _unpack_small(v):
    flat, out, r = v.reshape(-1), {}, 0
    for n, k in SMALL:
        out[n] = flat[r:r + k]
        r += k
    return out, flat[r]


def _place():
    return lax.axis_index("x"), lax.axis_index("y"), lax.axis_index("c")


def _other_chips(x, y):
    return [(1 - x, y), (x, 1 - y), (1 - x, 1 - y)]


HBM_SPEC = pl.BlockSpec(memory_space=pl.ANY)


def _chip_sems(n):
    return (pltpu.SemaphoreType.DMA((3 * n,)), pltpu.SemaphoreType.DMA((3 * n,)), pltpu.SemaphoreType.DMA((n,)))


def _gather_copies(ins, outs, send_sems, recv_sems, local_sems):
    x, y, c = _place()
    me = 2 * x + y
    copies = []
    for a, (src, out) in enumerate(zip(ins, outs)):
        copies.append(pltpu.make_async_copy(src, out.at[me], local_sems.at[a]))
        copies += [pltpu.make_async_remote_copy(src_ref=src, dst_ref=out.at[me], send_sem=send_sems.at[3 * a + k],
                                                recv_sem=recv_sems.at[3 * a + k], device_id=(px, py, c), device_id_type=MESH)
                   for k, (px, py) in enumerate(_other_chips(x, y))]
    return copies


def _scatter_copies(ins, outs, send_sems, recv_sems, local_sems):
    x, y, c = _place()
    return [pltpu.make_async_remote_copy(src_ref=src.at[2 * px + py], dst_ref=out.at[k], send_sem=send_sems.at[3 * a + k],
                                         recv_sem=recv_sems.at[3 * a + k], device_id=(px, py, c), device_id_type=MESH)
            for a, (src, out) in enumerate(zip(ins, outs)) for k, (px, py) in enumerate(_other_chips(x, y))]


def _all_gather_chips(arrays):
    n = len(arrays)

    def body(*refs):
        ins, outs = refs[:n], refs[n:2 * n]
        send1, recv1, send2, recv2, local_sems = refs[2 * n:]
        x, y, c = _place()
        me = 2 * x + y
        chips = _other_chips(x, y)
        local = [pltpu.make_async_copy(src, out.at[me], local_sems.at[a]) for a, (src, out) in enumerate(zip(ins, outs))]
        for cp in local:
            cp.start()

        def half(ref, chip, which):
            rows = ref.shape[-2] // 2
            return ref.at[chip, pl.ds(which * rows, rows)] if chip is not None else ref.at[pl.ds(which * rows, rows)]

        first = [pltpu.make_async_remote_copy(src_ref=half(src, None, c), dst_ref=half(out, me, c), send_sem=send1.at[3 * a + k],
                                              recv_sem=recv1.at[3 * a + k], device_id=(px, py, c), device_id_type=MESH)
                 for a, (src, out) in enumerate(zip(ins, outs)) for k, (px, py) in enumerate(chips)]
        for cp in first:
            cp.start()
        passed = []
        for a, out in enumerate(outs):
            for k, (px, py) in enumerate(chips):
                pltpu.make_async_remote_copy(src_ref=half(out, 2 * px + py, c), dst_ref=half(out, 2 * px + py, c),
                                             send_sem=send1.at[3 * a + k], recv_sem=recv1.at[3 * a + k],
                                             device_id=(px, py, c), device_id_type=MESH).wait_recv()
                cp = pltpu.make_async_remote_copy(src_ref=half(out, 2 * px + py, c), dst_ref=half(out, 2 * px + py, c),
                                                  send_sem=send2.at[3 * a + k], recv_sem=recv2.at[3 * a + k],
                                                  device_id=(x, y, 1 - c), device_id_type=MESH)
                cp.start()
                passed.append(cp)
        for a, out in enumerate(outs):
            for k, (px, py) in enumerate(chips):
                pltpu.make_async_remote_copy(src_ref=half(out, 2 * px + py, 1 - c), dst_ref=half(out, 2 * px + py, 1 - c),
                                             send_sem=send2.at[3 * a + k], recv_sem=recv2.at[3 * a + k],
                                             device_id=(x, y, 1 - c), device_id_type=MESH).wait_recv()
        for cp in first + passed:
            cp.wait_send()
        for cp in local:
            cp.wait()

    sems = pltpu.SemaphoreType.DMA((3 * n,))
    return pl.pallas_call(
        body, name="weights_all_gather", in_specs=[HBM_SPEC] * n, out_specs=[HBM_SPEC] * n,
        out_shape=[SDS((N_CHIPS,) + a.shape, a.dtype) for a in arrays],
        scratch_shapes=[sems, sems, sems, sems, pltpu.SemaphoreType.DMA((n,))],
    )(*arrays)


def _swap_with_sibling(name, arrays):
    n = len(arrays)

    def body(*refs):
        x, y, c = _place()
        send_sems, recv_sems = refs[2 * n:]
        copies = [pltpu.make_async_remote_copy(src_ref=refs[a], dst_ref=refs[n + a], send_sem=send_sems.at[a],
                                               recv_sem=recv_sems.at[a], device_id=(x, y, 1 - c), device_id_type=MESH)
                  for a in range(n)]
        for cp in copies:
            cp.start()
        for cp in copies:
            cp.wait()

    return pl.pallas_call(
        body, name=name, in_specs=[HBM_SPEC] * n, out_specs=[HBM_SPEC] * n, out_shape=[SDS(a.shape, a.dtype) for a in arrays],
        scratch_shapes=[pltpu.SemaphoreType.DMA((n,)), pltpu.SemaphoreType.DMA((n,))],
    )(*arrays)


def _all_reduce_small(v):
    n_dev = 8

    def body(v_ref, out_ref, land, send_sems, recv_sems):
        x, y, c = _place()
        me = 4 * x + 2 * y + c
        land[me] = v_ref[...]
        copies = []
        for k in range(1, n_dev):
            peer = (x ^ (k >> 2), y ^ ((k >> 1) & 1), c ^ (k & 1))
            copies.append(pltpu.make_async_remote_copy(src_ref=v_ref, dst_ref=land.at[me], send_sem=send_sems.at[k - 1],
                                                       recv_sem=recv_sems.at[k - 1], device_id=peer, device_id_type=MESH))
        for cp in copies:
            cp.start()
        for cp in copies:
            cp.wait()
        acc = land[0]
        for d in range(1, n_dev):
            acc = acc + land[d]
        out_ref[...] = acc

    return pl.pallas_call(
        body, name="small_all_reduce", in_specs=[pl.BlockSpec(memory_space=pltpu.VMEM)],
        out_specs=pl.BlockSpec(memory_space=pltpu.VMEM), out_shape=SDS(v.shape, v.dtype),
        scratch_shapes=[pltpu.VMEM((n_dev,) + v.shape, v.dtype), pltpu.SemaphoreType.DMA((n_dev - 1,)),
                        pltpu.SemaphoreType.DMA((n_dev - 1,))],
    )(v)


def _adamw(g, wv, m, v):
    m = ADAM_B1 * m + (1.0 - ADAM_B1) * g
    v = ADAM_B2 * v + (1.0 - ADAM_B2) * (g * g)
    m_hat = m / (1.0 - ADAM_B1 ** ADAM_STEP)
    v_hat = v / (1.0 - ADAM_B2 ** ADAM_STEP)
    delta = -ADAM_LR * (m_hat / (jnp.sqrt(v_hat) + ADAM_EPS) + ADAM_WD * wv)
    return delta, m, v


def kernel(x, mem, ffn1_norm, ffn1_w1, ffn1_w3, ffn1_w2, mix_norm, mem_norm, w_in, w_mem_kv, qn_dsa, kn_dsa, qn_mem, kn_mem, w_branch_sb, w_branch_dsa, w_branch_mem, w_gate, b_gate, w_out, ffn2_norm, ffn2_w1, ffn2_w3, ffn2_w2, loss_target, m_ffn1_norm, m_ffn1_w1, m_ffn1_w3, m_ffn1_w2, m_mix_norm, m_mem_norm, m_w_in, m_w_mem_kv, m_qn_dsa, m_kn_dsa, m_qn_mem, m_kn_mem, m_w_branch_sb, m_w_branch_dsa, m_w_branch_mem, m_w_gate, m_b_gate, m_w_out, m_ffn2_norm, m_ffn2_w1, m_ffn2_w3, m_ffn2_w2, v_ffn1_norm, v_ffn1_w1, v_ffn1_w3, v_ffn1_w2, v_mix_norm, v_mem_norm, v_w_in, v_w_mem_kv, v_qn_dsa, v_kn_dsa, v_qn_mem, v_kn_mem, v_w_branch_sb, v_w_branch_dsa, v_w_branch_mem, v_w_gate, v_b_gate, v_w_out, v_ffn2_norm, v_ffn2_w1, v_ffn2_w3, v_ffn2_w2):
    given = dict(locals())
    wts = {n: given[n][0] for n in WEIGHTS}
    moms = {n: given["m_" + n][0] for n in WEIGHTS}
    vars_ = {n: given["v_" + n][0] for n in WEIGHTS}

    plan = _Plan()
    x_i, y_i, _ = _place()
    my_chip = 2 * x_i + y_i

    full = {}

    def gathered(names):
        return lambda res: full.update({n: _full_from_shards(n, g) for n, g in zip(names, res)})

    for host, names in WEIGHT_PIECES:
        shards = [wts[n].astype(BF16) for n in names]
        if host is None:
            gathered(names)(_all_gather_chips(shards))
        else:
            plan.put(host, _Carry(shards, [SDS((N_CHIPS,) + a.shape, BF16) for a in shards], _chip_sems(len(names)),
                                  _gather_copies, gathered(names)))
    small = {n: wts[n].reshape(1, -1) for n, _ in SMALL}

    landed = {}

    def on_grads(group, grads):
        names = GROUPS[group]
        slices = [_shards_from_full(n, grads[n], BF16) for n in names]
        own = [_own_shard(n, grads[n], my_chip) for n in names]
        plan.put(GRAD_HOSTS[group], _Carry(slices, [SDS((3,) + a.shape[1:], BF16) for a in slices], _chip_sems(len(names)),
                                           _scatter_copies, lambda res: landed.update({group: (own, res)})))

    loss, gx, _, gs = _local_step(x[0], mem[0], loss_target[0], full, small, plan, on_grads)
    assert not plan.pending, list(plan.pending)

    def update(hv, ov, wv, mv, vv):
        g = hv + ov
        return (g,) + _adamw(g, wv, mv, vv)

    outs = [{}, {}, {}, {}]
    for group, names in GROUPS.items():
        own, got = landed[group]
        halves = [_tokmap(f"grads_sum_chips_{n}",
                          lambda a, b0, b1, b2: ((a + b0.astype(F32)) + b1.astype(F32)) + b2.astype(F32),
                          [o, g[0], g[1], g[2]], [], [(o.shape[1], F32)])[0] for n, o, g in zip(names, own, got)]
        others = _swap_with_sibling(f"grads_swap_cores_{group}", halves)
        for n, half, other in zip(names, halves, others):
            res = _tokmap(f"adamw_{n}", update, [half, other, wts[n], moms[n], vars_[n]], [], [(half.shape[1], F32)] * 4)
            for d, r in zip(outs, res):
                d[n] = r

    s_red = _all_reduce_small(_pack_small(gs, loss[0, 0]))
    res = _tokmap(
        "adamw_small", lambda g, wv, mv, vv: (g,) + _adamw(g, wv, mv, vv),
        [s_red, _pack_small(small), _pack_small({n: moms[n] for n, _ in SMALL}), _pack_small({n: vars_[n] for n, _ in SMALL})],
        [], [(D, F32)] * 4)
    for d, packed in zip(outs, res):
        d.update(_unpack_small(packed)[0])
    _, total_loss = _unpack_small(s_red)
    return (total_loss, gx[None], *[d[n][None] for d in outs for n in WEIGHTS])
```

```python
import functools

import numpy as np
import jax
import jax.numpy as jnp
from jax import lax
from jax.experimental import pallas as pl
from jax.experimental.pallas import tpu as pltpu

F32, BF16 = jnp.float32, jnp.bfloat16
SDS = jax.ShapeDtypeStruct
MESH = pl.DeviceIdType.MESH

D = 1024
HD = 64
QB = 128
DSA_T_FWD, DSA_T_BWD = 512, 256
D_FF = 2816
SB_W, DSA_W, DSA_OUT_W, MEM_W = 512, 768, 256, 256
DSA_DILS = (1, 4, 16)
MEM_LEN = 256
N_CHIPS = 4
EPS = 1e-6
SCALE = HD ** -0.5
EXHAUSTED = -104.0
SB_FWD_HEADS = 4
SB_QB = 256
SB_WIN = 512
NEG = -1e30
VMEM_LIMIT = 56 * 1024 * 1024

ADAM_LR, ADAM_B1, ADAM_B2, ADAM_EPS, ADAM_WD, ADAM_STEP = 0.001, 0.9, 0.999, 1e-08, 0.01, 10

NN = (((1,), (0,)), ((), ()))
NT = (((1,), (1,)), ((), ()))
TN = (((0,), (0,)), ((), ()))

SHARDED = (
    ("ffn1_w1", (D, D_FF), 1), ("ffn1_w3", (D, D_FF), 1), ("ffn1_w2", (D_FF, D), 0),
    ("w_in", (D, 4096), 1), ("w_mem_kv", (D, 512), 0),
    ("w_branch_sb", (SB_W, D), 1), ("w_branch_dsa", (DSA_OUT_W, D), 1), ("w_branch_mem", (MEM_W, D), 1),
    ("w_gate", (D, 3 * D), 1), ("w_out", (D, D), 0),
    ("ffn2_w1", (D, D_FF), 1), ("ffn2_w3", (D, D_FF), 1), ("ffn2_w2", (D_FF, D), 0),
)
SHARDED_BY_NAME = {n: (sh, ax) for n, sh, ax in SHARDED}
GROUPS = {
    "ffn2": ("ffn2_w1", "ffn2_w3", "ffn2_w2"),
    "mid": ("w_in", "w_mem_kv", "w_branch_sb", "w_branch_dsa", "w_branch_mem", "w_gate", "w_out"),
    "ffn1": ("ffn1_w1", "ffn1_w3", "ffn1_w2"),
}
WEIGHT_PIECES = (
    (None, ("ffn1_w1", "ffn1_w3")),
    ("ffn1_up", ("ffn1_w2", "w_in")),
    ("ffn1_down", ("w_gate", "w_mem_kv", "w_branch_sb", "w_branch_dsa", "w_branch_mem", "w_out")),
    ("proj_dsa", ("ffn2_w2",)),
    ("proj_gate", ("ffn2_w1", "ffn2_w3")),
)
GRAD_HOSTS = {"ffn2": "ffn2_bwd_dn", "mid": "ffn1_bwd_dw13", "ffn1": "ffn1_bwd_dn"}
SMALL = (("ffn1_norm", D), ("mix_norm", D), ("mem_norm", D), ("ffn2_norm", D), ("b_gate", 3 * D),
         ("qn_dsa", HD), ("kn_dsa", HD), ("qn_mem", HD), ("kn_mem", HD))
WEIGHTS = ("ffn1_norm", "ffn1_w1", "ffn1_w3", "ffn1_w2", "mix_norm", "mem_norm", "w_in", "w_mem_kv", "qn_dsa", "kn_dsa",
           "qn_mem", "kn_mem", "w_branch_sb", "w_branch_dsa", "w_branch_mem", "w_gate", "b_gate", "w_out", "ffn2_norm",
           "ffn2_w1", "ffn2_w3", "ffn2_w2")
SMALL_ROWS = 8


def _dot(a, b, dn=NN):
    return lax.dot_general(a, b, dn, preferred_element_type=F32)


def _dot01(x, m01, pieces=3):
    hi = x.astype(BF16)
    r1 = x - hi.astype(F32)
    mid = r1.astype(BF16)
    if pieces == 2:
        return _dot(hi, m01) + _dot(mid, m01)
    lo = (r1 - mid.astype(F32)).astype(BF16)
    return _dot(hi, m01) + _dot(mid, m01) + _dot(lo, m01)


def _pick(n, cands):
    for c in cands:
        if n % c == 0:
            return c
    raise ValueError(f"no tile for {n}")


def _from_dilated(v, d, scr):
    w = v.shape[1] // d
    v = v.astype(F32)
    for c in range(d):
        for p, buf in enumerate(scr[:w // 128]):
            buf[pl.ds(c, v.shape[0], stride=d), :] = v[:, c * w + 128 * p:c * w + 128 * (p + 1)]
    return jnp.concatenate([buf[...] for buf in scr[:w // 128]], axis=1)


def _to_dilated(v, d, scr):
    w = v.shape[1]
    for p, buf in enumerate(scr[:w // 128]):
        buf[...] = v[:, 128 * p:128 * (p + 1)].astype(F32)
    return jnp.concatenate([buf[pl.ds(c, v.shape[0] // d, stride=d), :] for c in range(d) for buf in scr[:w // 128]], axis=1)


def _tokmap(name, fn, tok_ins, consts, tok_outs, acc_outs=(), tile=512, dil_ins=None, dil_outs=None, place=None):
    dil_ins, dil_outs, place = dil_ins or {}, dil_outs or {}, place or {}
    bufs = [(j, buf) for j, (_, _, buf) in place.items() if buf is not None]
    n_buf = len(bufs)
    n = tok_ins[0].shape[0] * dil_ins.get(0, 1)
    tile = _pick(n, [t for t in (512, 352, 256, 128, 64, 32, 16, 8) if t <= tile])
    n_tin, n_in, n_tok, n_acc = len(tok_ins), len(tok_ins) + len(consts), len(tok_outs), len(acc_outs)
    n_scr = max([tok_ins[j].shape[1] // d // 128 for j, d in dil_ins.items() if d > 1]
                + [tok_outs[j][0] // 128 for j, d in dil_outs.items() if d > 1] + [0])

    def body(*refs):
        scr = refs[len(refs) - n_scr:]
        vals = [r[...] for r in refs[:n_in]]
        for j, d in dil_ins.items():
            if d > 1:
                vals[j] = _from_dilated(vals[j], d, scr)
        outs = fn(*vals)
        outs = list(outs) if isinstance(outs, (tuple, list)) else [outs]
        assert len(outs) == n_tok + n_acc, (name, len(outs))
        for j, d in dil_outs.items():
            if d > 1:
                outs[j] = _to_dilated(outs[j], d, scr)
        orefs = refs[n_in + n_buf:]
        for r, v in zip(orefs[:n_tok], outs[:n_tok]):
            r[...] = v.astype(r.dtype)
        if n_acc:
            @pl.when(pl.program_id(0) == 0)
            def _():
                for r in orefs[n_tok:n_tok + n_acc]:
                    r[...] = jnp.zeros(r.shape, r.dtype)
            for r, v in zip(orefs[n_tok:n_tok + n_acc], outs[n_tok:]):
                r[...] += v.astype(r.dtype)

    def tok_spec(width, d):
        return pl.BlockSpec((tile // d, d * width), lambda i: (i, 0))

    in_specs = [tok_spec(a.shape[1] // dil_ins.get(j, 1), dil_ins.get(j, 1)) for j, a in enumerate(tok_ins)]
    in_specs += [pl.BlockSpec(c.shape, lambda i: (0, 0)) for c in consts]
    in_specs += [HBM_SPEC] * n_buf
    out_specs = [tok_spec(w, dil_outs.get(j, 1)) for j, (w, _) in enumerate(tok_outs)]
    out_shape = [SDS((n // dil_outs.get(j, 1), w * dil_outs.get(j, 1)), dt) for j, (w, dt) in enumerate(tok_outs)]
    for j, (total, col_block, _) in place.items():
        out_specs[j] = pl.BlockSpec((tile, tok_outs[j][0]), lambda i, cb=col_block: (i, cb))
        out_shape[j] = SDS((n, total), tok_outs[j][1])
    out_specs += [pl.BlockSpec(s, lambda i: (0, 0)) for s in acc_outs]
    out_shape += [SDS(s, F32) for s in acc_outs]
    res = pl.pallas_call(
        body, name=name, grid=(n // tile,), in_specs=in_specs, out_specs=out_specs, out_shape=out_shape,
        scratch_shapes=[pltpu.VMEM((tile, 128), F32)] * n_scr,
        input_output_aliases={n_in + b: j for b, (j, _) in enumerate(bufs)},
        compiler_params=pltpu.CompilerParams(dimension_semantics=("arbitrary",), vmem_limit_bytes=VMEM_LIMIT),
    )(*tok_ins, *consts, *[buf for _, buf in bufs])
    return res


MATMUL_VMEM_BUDGET = 40 * 1024 * 1024


def _matmul_tiles(m, n, k, a_bytes, b_bytes, o_bytes, extra_bytes, whole_n=False):
    best = None
    for tk in [c for c in (3584, 2816, 2048, 1408, 1024, 512, 256, 128) if k % c == 0]:
        for tm in [c for c in (1408, 1024, 768, 512, 256, 128) if m % c == 0]:
            for tn in [n] if whole_n else [c for c in (1408, 1024, 768, 512, 256, 128) if n % c == 0]:
                need = 2 * tk * (tm * a_bytes + tn * b_bytes) + tm * tn * (2 * o_bytes + 2 * extra_bytes + 8)
                if need > MATMUL_VMEM_BUDGET:
                    continue
                score = (min(tm, 512) * min(tn, 512), tk, tm * tn, tn)
                if best is None or score > best[0]:
                    best = (score, (tm, tn, tk))
    return best[1]


class _Carry:
    def __init__(self, ins, outs, sems, copies, then):
        self.ins, self.outs, self.sems, self.copies, self.then = ins, outs, sems, copies, then


class _Plan:
    def __init__(self):
        self.pending = {}

    def put(self, host, carry):
        assert host not in self.pending, host
        self.pending[host] = carry

    def take(self, host):
        return self.pending.pop(host, None)


def _matmul(name, a, b, dn, out_dtype, epi=None, tiles=(), rows=(), plan=None, a_pro=None, n_sum=0):
    if dn == NN:
        (m, k), n = a.shape, b.shape[1]
    elif dn == NT:
        (m, k), n = a.shape, b.shape[0]
    else:
        (k, m), n = a.shape, b.shape[1]
    n_t, n_r = len(tiles), len(rows)
    pro, n_parts = a_pro if a_pro is not None else (None, 1)
    if pro is not None:
        assert dn == NN and n == _pick(n, (1024, 512))
        k //= n_parts
    if pro is not None:
        tm, tn, tk = _pick(m, (256, 128)), n, k
    else:
        tm, tn, tk = _matmul_tiles(m, n, k, a.dtype.itemsize, b.dtype.itemsize, jnp.dtype(out_dtype).itemsize,
                                   sum(t.dtype.itemsize for t in tiles), whole_n=n_sum > 0)
    nk = k // tk
    grid = (m // tm, n // tn, nk)
    assert pro is None or grid[1] == 1
    assert n_sum == 0 or grid[1] == 1
    carry = plan.take(name) if plan is not None else None
    n_ci, n_co = (len(carry.ins), len(carry.outs)) if carry else (0, 0)
    n_keep = 1 if pro is not None else 0

    def body(*refs):
        a_refs, b_ref, rest = refs[:n_parts], refs[n_parts], refs[n_parts + 1:]
        extras, rest = rest[:n_t + n_r], rest[n_t + n_r:]
        c_in, o_ref, rest = rest[:n_ci], rest[n_ci], rest[n_ci + 1:]
        sums, rest = rest[:n_sum], rest[n_sum:]
        keep, c_out, scratch = rest[:n_keep], rest[n_keep:n_keep + n_co], rest[n_keep + n_co:]
        ids = [pl.program_id(d) for d in range(3)]
        if n_sum:
            @pl.when((ids[0] == 0) & (ids[2] == 0))
            def _():
                for r in sums:
                    r[...] = jnp.zeros(r.shape, F32)
        if carry:
            sems = scratch[1:] if nk > 1 else scratch

            @pl.when((ids[0] == 0) & (ids[1] == 0) & (ids[2] == 0))
            def _():
                for cp in carry.copies(c_in, c_out, *sems):
                    cp.start()

        if pro is not None:
            av = pro(*[r[...] for r in a_refs])
            keep[0][...] = av
        else:
            av = a_refs[0][...].astype(BF16)
        part = _dot(av, b_ref[...].astype(BF16), dn)

        def finish(r):
            if epi is not None:
                r = epi(r, *[e[...] for e in extras])
            if n_sum:
                for ref, v in zip(sums, r[1:]):
                    ref[...] += v
                r = r[0]
            o_ref[...] = r.astype(o_ref.dtype)

        if nk == 1:
            finish(part)
        else:
            acc = scratch[0]

            @pl.when(ids[2] == 0)
            def _():
                acc[...] = part

            @pl.when(ids[2] > 0)
            def _():
                acc[...] += part

            @pl.when(ids[2] == nk - 1)
            def _():
                finish(acc[...])

        if carry:
            @pl.when((ids[0] == grid[0] - 1) & (ids[1] == grid[1] - 1) & (ids[2] == nk - 1))
            def _():
                for cp in carry.copies(c_in, c_out, *sems):
                    cp.wait()

    if dn == TN:
        a_specs = [pl.BlockSpec((tk, tm), lambda i, j, kk: (kk, i))]
    else:
        a_specs = [pl.BlockSpec((tm, tk), lambda i, j, kk, p=p: (i, kk + p * nk)) for p in range(n_parts)]
    b_spec = pl.BlockSpec((tn, tk), lambda i, j, kk: (j, kk)) if dn == NT else pl.BlockSpec((tk, tn), lambda i, j, kk: (kk, j))
    in_specs = a_specs + [b_spec] + [pl.BlockSpec((tm, tn), lambda i, j, kk: (i, j)) for _ in tiles]
    in_specs += [pl.BlockSpec((1, tn), lambda i, j, kk: (0, j)) for _ in rows] + [HBM_SPEC] * n_ci
    res = pl.pallas_call(
        body, name=name, grid=grid, in_specs=in_specs,
        out_specs=[pl.BlockSpec((tm, tn), lambda i, j, kk: (i, j))] + [pl.BlockSpec((1, tn), lambda i, j, kk: (0, 0))] * n_sum
        + [pl.BlockSpec((tm, tk), lambda i, j, kk: (i, kk))] * n_keep + [HBM_SPEC] * n_co,
        out_shape=[SDS((m, n), out_dtype)] + [SDS((1, n), F32)] * n_sum + [SDS((m, k), BF16)] * n_keep
        + (list(carry.outs) if carry else []),
        scratch_shapes=([pltpu.VMEM((tm, tn), F32)] if nk > 1 else []) + (list(carry.sems) if carry else []),
        compiler_params=pltpu.CompilerParams(
            dimension_semantics=("arbitrary",) * 3 if (carry or n_sum) else ("parallel", "parallel", "arbitrary"),
            vmem_limit_bytes=VMEM_LIMIT),
    )(*[a] * n_parts, b, *tiles, *rows, *(carry.ins if carry else []))
    if carry:
        carry.then(res[1 + n_sum + n_keep:])
    return tuple(res[:1 + n_sum + n_keep]) if n_sum + n_keep else res[0]


def _mean_all(v):
    return jnp.mean(v, axis=-1, keepdims=True)


def _head_sums(v, bd):
    w = bd.shape[0]
    return jnp.concatenate([_dot01(v[:, j:j + w], bd, 2) for j in range(0, v.shape[1], w)], axis=1)


def _mean_heads(bd):
    return lambda v: _head_sums(v, bd) * (1.0 / HD)


def _rms_fwd(x, g, mean):
    return x * lax.rsqrt(mean(x * x) + EPS) * g


def _rms_bwd(x, g, dy, mean):
    r = lax.rsqrt(mean(x * x) + EPS)
    dn = dy * g
    dx = r * dn - x * (r * r * r) * mean(dn * x)
    return dx, jnp.sum(dy * x * r, axis=0, keepdims=True)


def _swap_halves(x):
    w = x.shape[1]
    lane = lax.broadcasted_iota(jnp.int32, x.shape, 1)
    return jnp.where(lane % HD < HD // 2, pltpu.roll(x, w - HD // 2, 1), pltpu.roll(x, HD // 2, 1))


def _lanes(t, w):
    return jnp.tile(t, (1, w // t.shape[1]))


def _rope_fwd(x, cos, sin_signed):
    return x * _lanes(cos, x.shape[1]) + _swap_halves(x) * _lanes(sin_signed, x.shape[1])


def _rope_bwd(dy, cos, sin_signed):
    return dy * _lanes(cos, dy.shape[1]) + _swap_halves(dy * _lanes(sin_signed, dy.shape[1]))


def _bcast_heads(cols):
    return jnp.concatenate([jnp.broadcast_to(c, (c.shape[0], HD)) for c in cols], axis=1)


def _softplus(z):
    return jnp.maximum(z, 0.0) + jnp.log(1.0 + jnp.exp(-jnp.abs(z)))


def _block_diag(w):
    h = np.arange(w) // HD
    return jnp.asarray(h[:, None] == h[None, :], BF16)


def _sb_window(i, t):
    hi = (i + 1) * SB_QB - t * SB_WIN
    lo = hi - SB_WIN
    ws = pl.multiple_of(jnp.maximum(lo, 0), SB_QB)
    kpos = ws + lax.broadcasted_iota(jnp.int32, (SB_QB, SB_WIN), 1)
    qpos = i * SB_QB + lax.broadcasted_iota(jnp.int32, (SB_QB, SB_WIN), 0)
    return (kpos < qpos) & (kpos >= lo) & (kpos < hi), ws


def _sb_fwd(qkv):
    s = qkv.shape[0]
    assert s >= SB_WIN
    nq = s // SB_QB
    nh = SB_FWD_HEADS
    bw = HD * nh
    ngroups = SB_W // bw

    def body(q_ref, k_ref, v_ref, later_ref, o_ref, tot_ref, w0_ref, b0_ref, nb_ref):
        p, i = pl.program_id(0), pl.program_id(1)
        q = q_ref[...] * SCALE
        later_of = later_ref[...]

        def window(t, tots, outs, keep):
            mask, ws = _sb_window(i, t)
            kw, vw = k_ref[pl.ds(ws, SB_WIN), :], v_ref[pl.ds(ws, SB_WIN), :]
            new_t, new_o, w_all, b_all = [], [], [], []
            for hh in range(nh):
                sl = slice(HD * hh, HD * hh + HD)
                z = _dot(q[:, sl], kw[:, sl], NT)
                sp = _softplus(z)
                lf = jnp.where(mask, -sp, 0.0)
                lf_far, lf_near = lf[:, :SB_WIN // 2], lf[:, SB_WIN // 2:]
                later = tots[hh] + jnp.concatenate(
                    [_dot01(lf_far, later_of, 2) + jnp.sum(lf_near, axis=1, keepdims=True), _dot01(lf_near, later_of, 2)], axis=1)
                w = jnp.where(mask, jnp.exp(z - sp + later), 0.0).astype(BF16)
                new_o.append(outs[hh] + _dot(w, vw[:, sl]))
                new_t.append(tots[hh] + jnp.sum(lf, axis=1, keepdims=True))
                if keep:
                    w_all.append(w)
                    b_all.append(jnp.where(mask, jnp.exp(z - sp), 0.0).astype(BF16))
            if keep:
                w0_ref[...] = jnp.concatenate(w_all, axis=1)
                b0_ref[...] = jnp.concatenate(b_all, axis=1)
            alive = functools.reduce(jnp.maximum, [jnp.max(v) for v in new_t])
            return t + 1, alive, tuple(new_t), tuple(new_o)

        zt, zo = jnp.zeros((SB_QB, 1), F32), jnp.zeros((SB_QB, HD), F32)
        first = window(jnp.int32(0), (zt,) * nh, (zo,) * nh, True)
        t, _, tots, outs = lax.while_loop(lambda c: ((i + 1) * SB_QB - c[0] * SB_WIN > 0) & (c[1] > EXHAUSTED),
                                          lambda c: window(c[0], c[2], c[3], False), first)
        o_ref[...] = jnp.concatenate(outs, axis=1).astype(o_ref.dtype)
        tot_ref[...] = _bcast_heads(tots)
        nb_ref[p, i] = t

    whole = lambda off: pl.BlockSpec((s, bw), lambda p, i: (0, off + p), pipeline_mode=pl.Buffered(1))
    tile = pl.BlockSpec((SB_QB, bw), lambda p, i: (i, p))
    tri = pl.BlockSpec((SB_WIN // 2, SB_WIN // 2), lambda p, i: (0, 0), pipeline_mode=pl.Buffered(1))
    near = pl.BlockSpec((SB_QB, nh * SB_WIN), lambda p, i: (i, p))
    n_heads = SB_W // HD
    idx = np.arange(SB_WIN // 2)
    return pl.pallas_call(
        body, name="sb_fwd", grid=(ngroups, nq),
        in_specs=[tile, whole(ngroups), whole(2 * ngroups), tri],
        out_specs=[tile, tile, near, near, pl.BlockSpec(memory_space=pltpu.SMEM)],
        out_shape=[SDS((s, SB_W), BF16), SDS((s, SB_W), F32), SDS((s, n_heads * SB_WIN), BF16), SDS((s, n_heads * SB_WIN), BF16),
                   SDS((ngroups, nq), jnp.int32)],
        compiler_params=pltpu.CompilerParams(dimension_semantics=("arbitrary", "arbitrary"), vmem_limit_bytes=VMEM_LIMIT),
    )(qkv, qkv, qkv, jnp.asarray(idx[:, None] > idx[None, :], BF16))


def _sb_bwd(qkv, do, tot, nblk, w0, b0, buf, col):
    s = qkv.shape[0]
    nq = s // SB_QB
    npairs = SB_W // 128

    def body(nb_ref, q_ref, k_ref, v_ref, do_ref, tot_ref, upto_ref, before_ref, w0_ref, b0_ref, buf_ref,
             dq_ref, dk_ref, dv_ref):
        p, i = pl.program_id(0), pl.program_id(1)

        @pl.when(i == 0)
        def _():
            dk_ref[...] = jnp.zeros(dk_ref.shape, F32)
            dv_ref[...] = jnp.zeros(dv_ref.shape, F32)

        upto = upto_ref[...]
        before = before_ref[...]
        q, dout, tt = q_ref[...] * SCALE, do_ref[...], tot_ref[...]
        n = nb_ref[p * 2 // SB_FWD_HEADS, i]

        def step(it, c):
            pres, gpres, dqs = c
            mask, ws = _sb_window(i, n - 1 - it)
            kw, vw = k_ref[pl.ds(ws, SB_WIN), :], v_ref[pl.ds(ws, SB_WIN), :]
            new_p, new_g, new_dq, dks, dvs = [], [], [], [], []
            for hh in range(2):
                sl = slice(HD * hh, HD * hh + HD)
                z = _dot(q[:, sl], kw[:, sl], NT)
                sp = _softplus(z)
                lf = jnp.where(mask, -sp, 0.0)
                later = tt[:, HD * hh:HD * hh + 1] - (pres[hh] + _dot01(lf, upto, 2))
                w = jnp.where(mask, jnp.exp(z - sp + later), 0.0)
                beta = jnp.exp(z - sp)
                g = _dot(dout[:, sl], vw[:, sl], NT) * w
                g_far = gpres[hh] + _dot(g.astype(BF16), before)
                dz = jnp.where(mask, g * (1.0 - beta) - beta * g_far, 0.0).astype(BF16)
                new_dq.append(dqs[hh] + _dot(dz, kw[:, sl]))
                dks.append(_dot(dz, q[:, sl], TN))
                dvs.append(_dot(w.astype(BF16), dout[:, sl], TN))
                new_p.append(pres[hh] + jnp.sum(lf, axis=1, keepdims=True))
                new_g.append(gpres[hh] + jnp.sum(g, axis=1, keepdims=True))
            dk_ref[pl.ds(ws, SB_WIN), :] += jnp.concatenate(dks, axis=1)
            dv_ref[pl.ds(ws, SB_WIN), :] += jnp.concatenate(dvs, axis=1)
            return tuple(new_p), tuple(new_g), tuple(new_dq)

        zt, zo = jnp.zeros((SB_QB, 1), F32), jnp.zeros((SB_QB, HD), F32)
        _, gpres, dqs = lax.fori_loop(0, n - 1, step, ((zt, zt), (zt, zt), (zo, zo)))
        _, ws = _sb_window(i, 0)
        kw, vw = k_ref[pl.ds(ws, SB_WIN), :], v_ref[pl.ds(ws, SB_WIN), :]
        dqs, dks, dvs = list(dqs), [], []
        for hh in range(2):
            sl = slice(HD * hh, HD * hh + HD)
            w = w0_ref[:, SB_WIN * hh:SB_WIN * (hh + 1)]
            beta = b0_ref[:, SB_WIN * hh:SB_WIN * (hh + 1)].astype(F32)
            g = _dot(dout[:, sl], vw[:, sl], NT) * w.astype(F32)
            g16, half = g.astype(BF16), SB_WIN // 2
            g_far = gpres[hh] + jnp.concatenate(
                [_dot(g16[:, :half], before[:half, :half]),
                 _dot(g16[:, half:], before[:half, :half]) + jnp.sum(g[:, :half], axis=1, keepdims=True)], axis=1)
            dz = (g * (1.0 - beta) - beta * g_far).astype(BF16)
            dqs[hh] = dqs[hh] + _dot(dz, kw[:, sl])
            dks.append(_dot(dz, q[:, sl], TN))
            dvs.append(_dot(w, dout[:, sl], TN))
        dk_ref[pl.ds(ws, SB_WIN), :] += jnp.concatenate(dks, axis=1)
        dv_ref[pl.ds(ws, SB_WIN), :] += jnp.concatenate(dvs, axis=1)
        dq_ref[...] = (jnp.concatenate(dqs, axis=1) * SCALE).astype(dq_ref.dtype)

    whole_in = lambda off: pl.BlockSpec((s, 128), lambda p, i: (0, off + p), pipeline_mode=pl.Buffered(1))
    whole_out = pl.BlockSpec((s, 128), lambda p, i: (0, p), pipeline_mode=pl.Buffered(1))
    tile = pl.BlockSpec((SB_QB, 128), lambda p, i: (i, p))
    near = pl.BlockSpec((SB_QB, 2 * SB_WIN), lambda p, i: (i, p))
    dq_tile = pl.BlockSpec((SB_QB, 128), lambda p, i: (i, col // 128 + p))
    tri = pl.BlockSpec((SB_WIN, SB_WIN), lambda p, i: (0, 0), pipeline_mode=pl.Buffered(1))
    idx = np.arange(SB_WIN)
    return pl.pallas_call(
        body, name="sb_bwd", grid=(npairs, nq),
        in_specs=[pl.BlockSpec(memory_space=pltpu.SMEM), tile, whole_in(npairs), whole_in(2 * npairs), tile, tile, tri, tri,
                  near, near, HBM_SPEC],
        out_specs=[dq_tile, whole_out, whole_out],
        out_shape=[SDS(buf.shape, buf.dtype)] + [SDS((s, SB_W), F32)] * 2,
        input_output_aliases={10: 0},
        compiler_params=pltpu.CompilerParams(dimension_semantics=("arbitrary", "arbitrary"), vmem_limit_bytes=VMEM_LIMIT),
    )(nblk, qkv, qkv, qkv, do, tot, jnp.asarray(idx[:, None] <= idx[None, :], BF16), jnp.asarray(idx[:, None] < idx[None, :], BF16),
      w0, b0, buf)


def _dsa_mask(DSA_T, has_prev):
    r = lax.broadcasted_iota(jnp.int32, (DSA_T, QB + DSA_T), 0)
    j = lax.broadcasted_iota(jnp.int32, (DSA_T, QB + DSA_T), 1) - QB
    return (j <= r) & (j >= r - QB) & ((j >= 0) | has_prev)


def _dsa_fwd(q, k, v, dil):
    n = q.shape[0]
    DSA_T = DSA_T_FWD
    nt = n // DSA_T

    def body(q_ref, kc_ref, kp_ref, vc_ref, vp_ref, o_ref, lse_ref):
        mask = _dsa_mask(DSA_T, pl.program_id(1) > 0)
        outs, lses = [], []
        for hh in range(DSA_OUT_W // HD):
            sl = slice(HD * hh, HD * hh + HD)
            kcat = jnp.concatenate([kp_ref[:, sl], kc_ref[:, sl]], axis=0)
            vcat = jnp.concatenate([vp_ref[:, sl], vc_ref[:, sl]], axis=0)
            sc = jnp.where(mask, _dot(q_ref[:, sl] * SCALE, kcat, NT), NEG)
            m = jnp.max(sc, axis=1, keepdims=True)
            p = jnp.exp(sc - m)
            den = jnp.sum(p, axis=1, keepdims=True)
            outs.append(_dot(p.astype(BF16), vcat) / den)
            lses.append(m + jnp.log(den))
        o_ref[...] = jnp.concatenate(outs, axis=1)
        lse_ref[...] = _bcast_heads(lses)

    cur = pl.BlockSpec((DSA_T, DSA_OUT_W), lambda c, i: (i, c))
    prev = pl.BlockSpec((QB, DSA_OUT_W), lambda c, i: (jnp.maximum(i * (DSA_T // QB) - 1, 0), c))
    o, lse = pl.pallas_call(
        body, name=f"dsa_fwd_d{dil}", grid=(dil, nt), in_specs=[cur, cur, prev, cur, prev], out_specs=[cur, cur],
        out_shape=[SDS((n, dil * DSA_OUT_W), F32)] * 2,
        compiler_params=pltpu.CompilerParams(dimension_semantics=("parallel", "parallel")),
    )(q, k, k, v, v)
    return o, lse


def _dsa_bwd(q, k, v, do, cc, lse, dil):
    n = q.shape[0]
    DSA_T = DSA_T_BWD
    nt = n // DSA_T
    per = DSA_T // QB

    def body(qj_ref, qn_ref, kp_ref, kj_ref, vp_ref, vj_ref, doj_ref, don_ref, cj_ref, cn_ref, lj_ref, ln_ref,
             dq_ref, dk_ref, dv_ref):
        j = pl.program_id(1)
        mask = _dsa_mask(DSA_T, j > 0)
        r = lax.broadcasted_iota(jnp.int32, (QB, DSA_T), 0)
        kk = lax.broadcasted_iota(jnp.int32, (QB, DSA_T), 1)
        m_next = (kk >= r + QB) & (j + 1 < nt)
        dqs, dks, dvs = [], [], []
        for hh in range(DSA_OUT_W // HD):
            sl = slice(HD * hh, HD * hh + HD)
            one = slice(HD * hh, HD * hh + 1)
            kj, vj, doj, don = (t[:, sl] for t in (kj_ref, vj_ref, doj_ref, don_ref))
            qj, qn = qj_ref[:, sl] * SCALE, qn_ref[:, sl] * SCALE
            kcat = jnp.concatenate([kp_ref[:, sl], kj], axis=0)
            vcat = jnp.concatenate([vp_ref[:, sl], vj], axis=0)
            p1 = jnp.where(mask, jnp.exp(_dot(qj, kcat, NT) - lj_ref[:, one]), 0.0)
            ds1 = (p1 * (_dot(doj, vcat, NT) + cj_ref[:, one])).astype(BF16)
            p2 = jnp.where(m_next, jnp.exp(_dot(qn, kj, NT) - ln_ref[:, one]), 0.0)
            ds2 = (p2 * (_dot(don, vj, NT) + cn_ref[:, one])).astype(BF16)
            dqs.append(_dot(ds1, kcat) * SCALE)
            dks.append(_dot(ds1[:, QB:], qj, TN) + _dot(ds2, qn, TN))
            dvs.append(_dot(p1[:, QB:].astype(BF16), doj, TN) + _dot(p2.astype(BF16), don, TN))
        dq_ref[...] = jnp.concatenate(dqs, axis=1).astype(dq_ref.dtype)
        dk_ref[...] = jnp.concatenate(dks, axis=1).astype(dk_ref.dtype)
        dv_ref[...] = jnp.concatenate(dvs, axis=1).astype(dv_ref.dtype)

    cur = pl.BlockSpec((DSA_T, DSA_OUT_W), lambda c, j: (j, c))
    prev = pl.BlockSpec((QB, DSA_OUT_W), lambda c, j: (jnp.maximum(j * per - 1, 0), c))
    nxt = pl.BlockSpec((QB, DSA_OUT_W), lambda c, j: (jnp.minimum((j + 1) * per, n // QB - 1), c))
    dq, dk, dv = pl.pallas_call(
        body, name=f"dsa_bwd_d{dil}", grid=(dil, nt),
        in_specs=[cur, nxt, prev, cur, prev, cur, cur, nxt, cur, nxt, cur, nxt], out_specs=[cur, cur, cur],
        out_shape=[SDS((n, dil * DSA_OUT_W), BF16)] * 3,
        compiler_params=pltpu.CompilerParams(dimension_semantics=("parallel", "parallel")),
    )(q, q, k, k, v, v, do, do, cc, cc, lse, lse)
    return dq, dk, dv


def _norm_bwd_epi(acc, xv, dyv, g):
    dx, dg = _rms_bwd(xv, g, acc, _mean_all)
    return dx + dyv, dg


def _ffn_fwd(tag, x, gain, w13, w2, plan=None, target=None):
    n = _tokmap(f"{tag}_norm", lambda xv, g: _rms_fwd(xv, g, _mean_all), [x], [gain], [(D, BF16)])[0]
    ab = _matmul(f"{tag}_up", n, w13, NN, BF16, plan=plan)

    def gate(av, bv):
        a, b = av.astype(F32), bv.astype(F32)
        return (a * jax.nn.sigmoid(a) * b).astype(BF16)

    if target is None:
        y, h = _matmul(f"{tag}_down", ab, w2(), NN, F32, epi=lambda acc, res: res + 0.5 * acc, tiles=[x], plan=plan,
                       a_pro=(gate, 2))
        return y, (n, ab, h)

    def loss_epi(acc, res, tv):
        e = res + 0.5 * acc - tv
        return e * (1.0 / D), (0.5 / D) * jnp.sum(e * e, axis=0, keepdims=True)

    dy, loss_row, h = _matmul(f"{tag}_down", ab, w2(), NN, F32, epi=loss_epi, tiles=[x, target], plan=plan, a_pro=(gate, 2),
                              n_sum=1)
    return (dy, loss_row), (n, ab, h)


def _ffn_bwd(tag, x, gain, w13, w2, saved, dy, plan=None, on_dw=None, final=False):
    n, ab, h = saved
    dh = _matmul(f"{tag}_bwd_dh", dy, w2, NT, BF16, epi=lambda acc: 0.5 * acc)

    def gate_bwd(abv, dhv):
        a, b, dhf = abv[:, :D_FF].astype(F32), abv[:, D_FF:].astype(F32), dhv.astype(F32)
        sg = jax.nn.sigmoid(a)
        da = dhf * b * (sg * (1.0 + a * (1.0 - sg)))
        return jnp.concatenate([da, dhf * (a * sg)], axis=1)

    dab = _tokmap(f"{tag}_bwd_gate", gate_bwd, [ab, dh], [], [(2 * D_FF, BF16)], tile=256)[0]
    dw2 = _matmul(f"{tag}_bwd_dw2", h, dy, TN, F32, epi=lambda acc: 0.5 * acc)
    dw13 = _matmul(f"{tag}_bwd_dw13", n, dab, TN, F32, plan=plan)
    if on_dw is not None:
        on_dw(dw13, dw2)
    if not final:
        dx, dgain = _matmul(f"{tag}_bwd_dn", dab, w13, NT, F32, epi=_norm_bwd_epi, tiles=[x, dy], rows=[gain], plan=plan,
                            n_sum=1)
    else:
        dn = _matmul(f"{tag}_bwd_dn", dab, w13, NT, F32, plan=plan)
        dx, dgain = _tokmap(f"{tag}_bwd_norm", lambda xv, dnv, dyv, g: _norm_bwd_epi(dnv, xv, dyv, g), [x, dn, dy], [gain],
                            [(D, F32)], [(1, D)])
    return dx, dgain, dw13, dw2


def _rope_tables(s):
    half = HD // 2
    inv_freq = jnp.power(10000.0, -jnp.arange(half, dtype=F32) / half)
    ang = jnp.arange(s).astype(F32)[:, None] * inv_freq[None, :]
    cos, sin = jnp.cos(ang), jnp.sin(ang)
    return jnp.tile(jnp.concatenate([cos, cos], axis=1), (1, 2)), jnp.tile(jnp.concatenate([-sin, sin], axis=1), (1, 2))


def _local_step(x, mem, tgt, w, sm, plan=None, on_grads=None):
    s = x.shape[0]
    assert s % (max(DSA_T_FWD, DSA_T_BWD) * max(DSA_DILS)) == 0
    on_grads = on_grads or (lambda group, grads: None)
    c_sb, c_dsa, c_qm, c_all = 3 * D, 3 * D + 3 * SB_W, 3 * D + 3 * SB_W + 3 * DSA_W, 3 * D + 4096
    cos, sin = _rope_tables(s)
    bd768 = bd256 = _block_diag(128)
    gq_dsa, gk_dsa = jnp.tile(sm["qn_dsa"], (1, DSA_W // HD)), jnp.tile(sm["kn_dsa"], (1, DSA_W // HD))
    gq_mem, gk_mem = jnp.tile(sm["qn_mem"], (1, MEM_W // HD)), jnp.tile(sm["kn_mem"], (1, MEM_W // HD))

    w13_1 = jnp.concatenate([w["ffn1_w1"], w["ffn1_w3"]], axis=1)
    x1, ffn1_saved = _ffn_fwd("ffn1", x, sm["ffn1_norm"], w13_1, lambda: w["ffn1_w2"], plan)
    w_all = jnp.concatenate([w["w_gate"], w["w_in"]], axis=1)
    wb_sb, wb_dsa, wb_mem = w["w_branch_sb"], w["w_branch_dsa"], w["w_branch_mem"]
    hmix = _tokmap("mix_norm", lambda xv, g: _rms_fwd(xv, g, _mean_all), [x1], [sm["mix_norm"]], [(D, BF16)])[0]
    qkv_sb = _matmul("proj_sb", hmix, w_all[:, c_sb:c_dsa], NN, BF16)
    qkv_dsa = _matmul("proj_dsa", hmix, w_all[:, c_dsa:c_qm], NN, BF16, plan=plan)
    q_mem = _matmul("proj_qmem", hmix, w_all[:, c_qm:], NN, BF16)
    gpre = _matmul("proj_gate", hmix, w_all[:, :c_sb], NN, BF16, epi=lambda acc, b: acc + b, rows=[sm["b_gate"]], plan=plan)

    o_sb, sb_tot, sb_w0, sb_b0, sb_nblk = _sb_fwd(qkv_sb)

    def dsa_prep(qkv, cs, sn, gq, gk, bd):
        mean = _mean_heads(bd)
        qn = _rope_fwd(_rms_fwd(qkv[:, :DSA_W].astype(F32), gq, mean), cs, sn)
        kn = _rope_fwd(_rms_fwd(qkv[:, DSA_W:2 * DSA_W].astype(F32), gk, mean), cs, sn)
        v = qkv[:, 2 * DSA_W:]
        outs = []
        for t in (qn, kn, v):
            outs += [t[:, DSA_OUT_W * g:DSA_OUT_W * (g + 1)] for g in range(3)]
        return outs

    dsa_in = _tokmap("dsa_prep", dsa_prep, [qkv_dsa, cos, sin], [gq_dsa, gk_dsa, bd768], [(DSA_OUT_W, BF16)] * 9, tile=512,
                     dil_outs={j: DSA_DILS[j % 3] for j in range(9)})
    dsa_q, dsa_k, dsa_v = dsa_in[0:3], dsa_in[3:6], dsa_in[6:9]
    dsa_o, dsa_lse = zip(*[_dsa_fwd(dsa_q[g], dsa_k[g], dsa_v[g], DSA_DILS[g]) for g in range(3)])

    def alphas(l0, l1, l2):
        m = jnp.maximum(jnp.maximum(l0, l1), l2)
        e = [jnp.exp(l - m) for l in (l0, l1, l2)]
        tot = e[0] + e[1] + e[2]
        return [t / tot for t in e]

    def dsa_mix(o0, o1, o2, l0, l1, l2):
        a = alphas(l0, l1, l2)
        return a[0] * o0 + a[1] * o1 + a[2] * o2

    o_dsa = _tokmap("dsa_mix", dsa_mix, [*dsa_o, *dsa_lse], [], [(DSA_OUT_W, BF16)], tile=512,
                    dil_ins={j: DSA_DILS[j % 3] for j in range(6)})[0]

    def mem_kv(memv, g, wkv, gk, bd):
        kv = _dot(_rms_fwd(memv, g, _mean_all).astype(BF16), wkv)
        return _rms_fwd(kv[:, :MEM_W], gk, _mean_heads(bd)), kv[:, MEM_W:]

    km, vm = _tokmap("mem_kv", mem_kv, [mem], [sm["mem_norm"], w["w_mem_kv"], gk_mem, bd256], [(MEM_W, BF16)] * 2)

    def mem_probs(qv, kmv, gq, bd):
        qn = _rms_fwd(qv.astype(F32), gq, _mean_heads(bd)).astype(BF16)
        ps = []
        for h in range(MEM_W // HD):
            sl = slice(HD * h, HD * h + HD)
            sc = _dot(qn[:, sl], kmv[:, sl], NT) * SCALE
            e = jnp.exp(sc - jnp.max(sc, axis=1, keepdims=True))
            ps.append(e / jnp.sum(e, axis=1, keepdims=True))
        return qn, ps

    def mem_attn(qv, kmv, vmv, gq, bd):
        _, ps = mem_probs(qv, kmv, gq, bd)
        return jnp.concatenate([_dot(p.astype(BF16), vmv[:, HD * h:HD * h + HD]) for h, p in enumerate(ps)], axis=1)

    o_mem = _tokmap("mem_attn", mem_attn, [q_mem], [km, vm, gq_mem, bd256], [(MEM_W, BF16)])[0]

    def merge(osb, odsa, omem, gp, w_sb, w_dsa, w_mem):
        gates = jax.nn.sigmoid(gp.astype(F32))
        ys = (_dot(osb, w_sb), _dot(odsa, w_dsa), _dot(omem, w_mem))
        return gates, ys, gates[:, :D] * ys[0] + gates[:, D:2 * D] * ys[1] + gates[:, 2 * D:] * ys[2]

    merged = _tokmap("merge", lambda *a: merge(*a)[2], [o_sb, o_dsa, o_mem, gpre], [wb_sb, wb_dsa, wb_mem], [(D, BF16)],
                     tile=512)[0]
    x2 = _matmul("out_proj", merged, w["w_out"], NN, F32, epi=lambda acc, res: res + acc, tiles=[x1])
    w13_2 = jnp.concatenate([w["ffn2_w1"], w["ffn2_w3"]], axis=1)
    (dy, loss_row), ffn2_saved = _ffn_fwd("ffn2", x2, sm["ffn2_norm"], w13_2, lambda: w["ffn2_w2"], target=tgt)
    loss = jnp.sum(loss_row).reshape(1, 1)

    gw, gs = {}, {}
    def ffn_grads(tag):
        def on_dw(dw13, dw2):
            gw[f"{tag}_w1"], gw[f"{tag}_w3"], gw[f"{tag}_w2"] = dw13[:, :D_FF], dw13[:, D_FF:], dw2
            on_grads(tag, {n: gw[n] for n in (f"{tag}_w1", f"{tag}_w3", f"{tag}_w2")})
        return on_dw

    dx2, gs["ffn2_norm"], _, _ = _ffn_bwd("ffn2", x2, sm["ffn2_norm"], w13_2, w["ffn2_w2"], ffn2_saved, dy, plan,
                                          ffn_grads("ffn2"))
    dmerged = _matmul("out_proj_bwd_dx", dx2, w["w_out"], NT, BF16)
    gw["w_out"] = _matmul("out_proj_bwd_dw", merged, dx2, TN, F32)

    def merge_bwd(osb, odsa, omem, gp, dm, w_sb, w_dsa, w_mem):
        gates, ys, _ = merge(osb, odsa, omem, gp, w_sb, w_dsa, w_mem)
        dmf = dm.astype(F32)
        dgp, dos, dws = [], [], []
        for b, (ov, wv) in enumerate(((osb, w_sb), (odsa, w_dsa), (omem, w_mem))):
            gb = gates[:, D * b:D * (b + 1)]
            dgp.append(dmf * ys[b] * gb * (1.0 - gb))
            dyb = (dmf * gb).astype(BF16)
            dos.append(_dot(dyb, wv, NT))
            dws.append(_dot(ov, dyb, TN))
        dgp = jnp.concatenate(dgp, axis=1)
        return dos[0], dos[1], dos[2], dgp, dws[0], dws[1], dws[2], jnp.sum(dgp, axis=0, keepdims=True)

    do_sb, do_dsa, do_mem, dgpre, gw["w_branch_sb"], gw["w_branch_dsa"], gw["w_branch_mem"], gs["b_gate"] = _tokmap(
        "merge_bwd", merge_bwd, [o_sb, o_dsa, o_mem, gpre, dmerged], [wb_sb, wb_dsa, wb_mem],
        [(SB_W, BF16), (DSA_OUT_W, F32), (MEM_W, BF16), (3 * D, BF16)],
        [(SB_W, D), (DSA_OUT_W, D), (MEM_W, D), (1, 3 * D)], tile=512, place={3: (c_all, 0, None)})

    dall, dk_sb, dv_sb = _sb_bwd(qkv_sb, do_sb, sb_tot, sb_nblk, sb_w0, sb_b0, dgpre, c_sb)
    dall = lax.dynamic_update_slice(dall, dk_sb.astype(BF16), (0, c_sb + SB_W))
    dall = lax.dynamic_update_slice(dall, dv_sb.astype(BF16), (0, c_sb + 2 * SB_W))

    def dsa_mix_bwd(o0, o1, o2, l0, l1, l2, dov, bd):
        a = alphas(l0, l1, l2)
        omix = a[0] * o0 + a[1] * o1 + a[2] * o2
        dot_o = _head_sums(dov * omix, bd)
        return [dov * t for t in a] + [-t * dot_o for t in a]

    mixb = _tokmap("dsa_mix_bwd", dsa_mix_bwd, [*dsa_o, *dsa_lse, do_dsa], [bd256],
                   [(DSA_OUT_W, BF16)] * 3 + [(DSA_OUT_W, F32)] * 3, tile=512,
                   dil_ins={j: DSA_DILS[j % 3] for j in range(6)}, dil_outs={j: DSA_DILS[j % 3] for j in range(6)})
    dsa_d = [_dsa_bwd(dsa_q[g], dsa_k[g], dsa_v[g], mixb[g], mixb[3 + g], dsa_lse[g], DSA_DILS[g]) for g in range(3)]

    def dsa_prep_bwd(qkv, cs, sn, *rest):
        dqs, dks, dvs, (gq, gk, bd) = rest[0:3], rest[3:6], rest[6:9], rest[9:]
        mean = _mean_heads(bd)
        dq, dgq = _rms_bwd(qkv[:, :DSA_W].astype(F32), gq, _rope_bwd(jnp.concatenate(dqs, axis=1), cs, sn), mean)
        dk, dgk = _rms_bwd(qkv[:, DSA_W:2 * DSA_W].astype(F32), gk, _rope_bwd(jnp.concatenate(dks, axis=1), cs, sn), mean)
        return jnp.concatenate([dq, dk] + list(dvs), axis=1), dgq, dgk

    dall, dgq_dsa, dgk_dsa = _tokmap(
        "dsa_prep_bwd", dsa_prep_bwd,
        [qkv_dsa, cos, sin] + [dsa_d[g][t] for t in range(3) for g in range(3)], [gq_dsa, gk_dsa, bd768],
        [(3 * DSA_W, BF16)], [(1, DSA_W), (1, DSA_W)], tile=512, dil_ins={3 + j: DSA_DILS[j % 3] for j in range(9)},
        place={0: (c_all, c_dsa // (3 * DSA_W), dall)})
    gs["qn_dsa"] = dgq_dsa.reshape(DSA_W // HD, HD).sum(axis=0, keepdims=True)
    gs["kn_dsa"] = dgk_dsa.reshape(DSA_W // HD, HD).sum(axis=0, keepdims=True)

    def mem_attn_bwd(qv, dov, kmv, vmv, gq, bd):
        qn, ps = mem_probs(qv, kmv, gq, bd)
        dqn, dkm, dvm = [], [], []
        for h, p in enumerate(ps):
            sl = slice(HD * h, HD * h + HD)
            dp = _dot(dov[:, sl], vmv[:, sl], NT)
            ds = (p * (dp - jnp.sum(p * dp, axis=1, keepdims=True)) * SCALE).astype(BF16)
            dqn.append(_dot(ds, kmv[:, sl]))
            dkm.append(_dot(ds, qn[:, sl], TN))
            dvm.append(_dot(p.astype(BF16), dov[:, sl], TN))
        dq, dgq = _rms_bwd(qv.astype(F32), gq, jnp.concatenate(dqn, axis=1), _mean_heads(bd))
        return dq, jnp.concatenate(dkm, axis=1), jnp.concatenate(dvm, axis=1), dgq

    dall, dkm, dvm, dgq_mem = _tokmap("mem_attn_bwd", mem_attn_bwd, [q_mem, do_mem], [km, vm, gq_mem, bd256],
                                      [(MEM_W, BF16)], [(MEM_LEN, MEM_W), (MEM_LEN, MEM_W), (1, MEM_W)],
                                      place={0: (c_all, c_qm // MEM_W, dall)})
    gs["qn_mem"] = dgq_mem.reshape(MEM_W // HD, HD).sum(axis=0, keepdims=True)

    def mem_kv_bwd(memv, dkmv, dvmv, g, wkv, gk, bd):
        memn = _rms_fwd(memv, g, _mean_all).astype(BF16)
        kv = _dot(memn, wkv)
        dk, dgk = _rms_bwd(kv[:, :MEM_W], gk, dkmv, _mean_heads(bd))
        dkv = jnp.concatenate([dk, dvmv], axis=1).astype(BF16)
        _, dg = _rms_bwd(memv, g, _dot(dkv, wkv, NT), _mean_all)
        return _dot(memn, dkv, TN), dg, dgk

    gw["w_mem_kv"], gs["mem_norm"], dgk_mem = _tokmap(
        "mem_kv_bwd", mem_kv_bwd, [mem, dkm, dvm], [sm["mem_norm"], w["w_mem_kv"], gk_mem, bd256], [],
        [(D, 2 * MEM_W), (1, D), (1, MEM_W)])
    gs["kn_mem"] = dgk_mem.reshape(MEM_W // HD, HD).sum(axis=0, keepdims=True)

    dx1, gs["mix_norm"] = _matmul("proj_bwd_dx", dall, w_all, NT, F32, epi=_norm_bwd_epi, tiles=[x1, dx2],
                                  rows=[sm["mix_norm"]], n_sum=1)
    dw_all = _matmul("proj_bwd_dw", hmix, dall, TN, F32)
    gw["w_gate"], gw["w_in"] = dw_all[:, :c_sb], dw_all[:, c_sb:]
    on_grads("mid", {n: gw[n] for n in GROUPS["mid"]})
    gx, gs["ffn1_norm"], _, _ = _ffn_bwd("ffn1", x, sm["ffn1_norm"], w13_1, w["ffn1_w2"], ffn1_saved, dx1, plan,
                                         ffn_grads("ffn1"), final=True)
    return loss, gx, gw, gs


def _shard_shape(name):
    shape, axis = SHARDED_BY_NAME[name]
    return (shape[0] // N_CHIPS, shape[1]) if axis == 0 else (shape[0], shape[1] // N_CHIPS)


def _full_from_shards(name, shards):
    axis = SHARDED_BY_NAME[name][1]
    return shards.reshape(SHARDED_BY_NAME[name][0]) if axis == 0 else jnp.concatenate(list(shards), axis=1)


def _shards_from_full(name, full, dtype):
    axis, n = SHARDED_BY_NAME[name][1], _shard_shape(name)
    return jnp.stack([lax.slice_in_dim(full, c * n[axis], (c + 1) * n[axis], axis=axis).astype(dtype) for c in range(N_CHIPS)])


def _own_shard(name, full, chip):
    axis, n = SHARDED_BY_NAME[name][1], _shard_shape(name)
    return lax.dynamic_slice_in_dim(full, chip * n[axis], n[axis], axis=axis)


SMALL_USED = sum(n for _, n in SMALL)


def _pack_small(d, loss=None):
    parts = [d[n].reshape(-1) for n, _ in SMALL]
    parts.append(jnp.zeros((1,), F32) if loss is None else loss.reshape(1))
    parts.append(jnp.zeros((SMALL_ROWS * D - SMALL_USED - 1,), F32))
    return jnp.concatenate(parts).reshape(SMALL_ROWS, D)


def _unpack_small(v):
    flat, out, r = v.reshape(-1), {}, 0
    for n, k in SMALL:
        out[n] = flat[r:r + k]
        r += k
    return out, flat[r]


def _place():
    return lax.axis_index("x"), lax.axis_index("y"), lax.axis_index("c")


def _other_chips(x, y):
    return [(1 - x, y), (x, 1 - y), (1 - x, 1 - y)]


HBM_SPEC = pl.BlockSpec(memory_space=pl.ANY)


def _chip_sems(n):
    return (pltpu.SemaphoreType.DMA((3 * n,)), pltpu.SemaphoreType.DMA((3 * n,)), pltpu.SemaphoreType.DMA((n,)))


def _gather_copies(ins, outs, send_sems, recv_sems, local_sems):
    x, y, c = _place()
    me = 2 * x + y
    copies = []
    for a, (src, out) in enumerate(zip(ins, outs)):
        copies.append(pltpu.make_async_copy(src, out.at[me], local_sems.at[a]))
        copies += [pltpu.make_async_remote_copy(src_ref=src, dst_ref=out.at[me], send_sem=send_sems.at[3 * a + k],
                                                recv_sem=recv_sems.at[3 * a + k], device_id=(px, py, c), device_id_type=MESH)
                   for k, (px, py) in enumerate(_other_chips(x, y))]
    return copies


def _scatter_copies(ins, outs, send_sems, recv_sems, local_sems):
    x, y, c = _place()
    return [pltpu.make_async_remote_copy(src_ref=src.at[2 * px + py], dst_ref=out.at[k], send_sem=send_sems.at[3 * a + k],
                                         recv_sem=recv_sems.at[3 * a + k], device_id=(px, py, c), device_id_type=MESH)
            for a, (src, out) in enumerate(zip(ins, outs)) for k, (px, py) in enumerate(_other_chips(x, y))]


def _all_gather_chips(arrays):
    n = len(arrays)

    def body(*refs):
        ins, outs = refs[:n], refs[n:2 * n]
        send1, recv1, send2, recv2, local_sems = refs[2 * n:]
        x, y, c = _place()
        me = 2 * x + y
        chips = _other_chips(x, y)
        local = [pltpu.make_async_copy(src, out.at[me], local_sems.at[a]) for a, (src, out) in enumerate(zip(ins, outs))]
        for cp in local:
            cp.start()

        def half(ref, chip, which):
            rows = ref.shape[-2] // 2
            return ref.at[chip, pl.ds(which * rows, rows)] if chip is not None else ref.at[pl.ds(which * rows, rows)]

        first = [pltpu.make_async_remote_copy(src_ref=half(src, None, c), dst_ref=half(out, me, c), send_sem=send1.at[3 * a + k],
                                              recv_sem=recv1.at[3 * a + k], device_id=(px, py, c), device_id_type=MESH)
                 for a, (src, out) in enumerate(zip(ins, outs)) for k, (px, py) in enumerate(chips)]
        for cp in first:
            cp.start()
        passed = []
        for a, out in enumerate(outs):
            for k, (px, py) in enumerate(chips):
                pltpu.make_async_remote_copy(src_ref=half(out, 2 * px + py, c), dst_ref=half(out, 2 * px + py, c),
                                             send_sem=send1.at[3 * a + k], recv_sem=recv1.at[3 * a + k],
                                             device_id=(px, py, c), device_id_type=MESH).wait_recv()
                cp = pltpu.make_async_remote_copy(src_ref=half(out, 2 * px + py, c), dst_ref=half(out, 2 * px + py, c),
                                                  send_sem=send2.at[3 * a + k], recv_sem=recv2.at[3 * a + k],
                                                  device_id=(x, y, 1 - c), device_id_type=MESH)
                cp.start()
                passed.append(cp)
        for a, out in enumerate(outs):
            for k, (px, py) in enumerate(chips):
                pltpu.make_async_remote_copy(src_ref=half(out, 2 * px + py, 1 - c), dst_ref=half(out, 2 * px + py, 1 - c),
                                             send_sem=send2.at[3 * a + k], recv_sem=recv2.at[3 * a + k],
                                             device_id=(x, y, 1 - c), device_id_type=MESH).wait_recv()
        for cp in first + passed:
            cp.wait_send()
        for cp in local:
            cp.wait()

    sems = pltpu.SemaphoreType.DMA((3 * n,))
    return pl.pallas_call(
        body, name="weights_all_gather", in_specs=[HBM_SPEC] * n, out_specs=[HBM_SPEC] * n,
        out_shape=[SDS((N_CHIPS,) + a.shape, a.dtype) for a in arrays],
        scratch_shapes=[sems, sems, sems, sems, pltpu.SemaphoreType.DMA((n,))],
    )(*arrays)


def _swap_with_sibling(name, arrays):
    n = len(arrays)

    def body(*refs):
        x, y, c = _place()
        send_sems, recv_sems = refs[2 * n:]
        copies = [pltpu.make_async_remote_copy(src_ref=refs[a], dst_ref=refs[n + a], send_sem=send_sems.at[a],
                                               recv_sem=recv_sems.at[a], device_id=(x, y, 1 - c), device_id_type=MESH)
                  for a in range(n)]
        for cp in copies:
            cp.start()
        for cp in copies:
            cp.wait()

    return pl.pallas_call(
        body, name=name, in_specs=[HBM_SPEC] * n, out_specs=[HBM_SPEC] * n, out_shape=[SDS(a.shape, a.dtype) for a in arrays],
        scratch_shapes=[pltpu.SemaphoreType.DMA((n,)), pltpu.SemaphoreType.DMA((n,))],
    )(*arrays)


def _all_reduce_small(v):
    n_dev = 8

    def body(v_ref, out_ref, land, send_sems, recv_sems):
        x, y, c = _place()
        me = 4 * x + 2 * y + c
        land[me] = v_ref[...]
        copies = []
        for k in range(1, n_dev):
            peer = (x ^ (k >> 2), y ^ ((k >> 1) & 1), c ^ (k & 1))
            copies.append(pltpu.make_async_remote_copy(src_ref=v_ref, dst_ref=land.at[me], send_sem=send_sems.at[k - 1],
                                                       recv_sem=recv_sems.at[k - 1], device_id=peer, device_id_type=MESH))
        for cp in copies:
            cp.start()
        for cp in copies:
            cp.wait()
        acc = land[0]
        for d in range(1, n_dev):
            acc = acc + land[d]
        out_ref[...] = acc

    return pl.pallas_call(
        body, name="small_all_reduce", in_specs=[pl.BlockSpec(memory_space=pltpu.VMEM)],
        out_specs=pl.BlockSpec(memory_space=pltpu.VMEM), out_shape=SDS(v.shape, v.dtype),
        scratch_shapes=[pltpu.VMEM((n_dev,) + v.shape, v.dtype), pltpu.SemaphoreType.DMA((n_dev - 1,)),
                        pltpu.SemaphoreType.DMA((n_dev - 1,))],
    )(v)


def _adamw(g, wv, m, v):
    m = ADAM_B1 * m + (1.0 - ADAM_B1) * g
    v = ADAM_B2 * v + (1.0 - ADAM_B2) * (g * g)
    m_hat = m / (1.0 - ADAM_B1 ** ADAM_STEP)
    v_hat = v / (1.0 - ADAM_B2 ** ADAM_STEP)
    delta = -ADAM_LR * (m_hat / (jnp.sqrt(v_hat) + ADAM_EPS) + ADAM_WD * wv)
    return delta, m, v


def kernel(x, mem, ffn1_norm, ffn1_w1, ffn1_w3, ffn1_w2, mix_norm, mem_norm, w_in, w_mem_kv, qn_dsa, kn_dsa, qn_mem, kn_mem, w_branch_sb, w_branch_dsa, w_branch_mem, w_gate, b_gate, w_out, ffn2_norm, ffn2_w1, ffn2_w3, ffn2_w2, loss_target, m_ffn1_norm, m_ffn1_w1, m_ffn1_w3, m_ffn1_w2, m_mix_norm, m_mem_norm, m_w_in, m_w_mem_kv, m_qn_dsa, m_kn_dsa, m_qn_mem, m_kn_mem, m_w_branch_sb, m_w_branch_dsa, m_w_branch_mem, m_w_gate, m_b_gate, m_w_out, m_ffn2_norm, m_ffn2_w1, m_ffn2_w3, m_ffn2_w2, v_ffn1_norm, v_ffn1_w1, v_ffn1_w3, v_ffn1_w2, v_mix_norm, v_mem_norm, v_w_in, v_w_mem_kv, v_qn_dsa, v_kn_dsa, v_qn_mem, v_kn_mem, v_w_branch_sb, v_w_branch_dsa, v_w_branch_mem, v_w_gate, v_b_gate, v_w_out, v_ffn2_norm, v_ffn2_w1, v_ffn2_w3, v_ffn2_w2):
    given = dict(locals())
    wts = {n: given[n][0] for n in WEIGHTS}
    moms = {n: given["m_" + n][0] for n in WEIGHTS}
    vars_ = {n: given["v_" + n][0] for n in WEIGHTS}

    plan = _Plan()
    x_i, y_i, _ = _place()
    my_chip = 2 * x_i + y_i

    full = {}

    def gathered(names):
        return lambda res: full.update({n: _full_from_shards(n, g) for n, g in zip(names, res)})

    for host, names in WEIGHT_PIECES:
        shards = [wts[n].astype(BF16) for n in names]
        if host is None:
            gathered(names)(_all_gather_chips(shards))
        else:
            plan.put(host, _Carry(shards, [SDS((N_CHIPS,) + a.shape, BF16) for a in shards], _chip_sems(len(names)),
                                  _gather_copies, gathered(names)))
    small = {n: wts[n].reshape(1, -1) for n, _ in SMALL}

    landed = {}

    def on_grads(group, grads):
        names = GROUPS[group]
        slices = [_shards_from_full(n, grads[n], BF16) for n in names]
        own = [_own_shard(n, grads[n], my_chip) for n in names]
        plan.put(GRAD_HOSTS[group], _Carry(slices, [SDS((3,) + a.shape[1:], BF16) for a in slices], _chip_sems(len(names)),
                                           _scatter_copies, lambda res: landed.update({group: (own, res)})))

    loss, gx, _, gs = _local_step(x[0], mem[0], loss_target[0], full, small, plan, on_grads)
    assert not plan.pending, list(plan.pending)

    def update(hv, ov, wv, mv, vv):
        g = hv + ov
        return (g,) + _adamw(g, wv, mv, vv)

    outs = [{}, {}, {}, {}]
    for group, names in GROUPS.items():
        own, got = landed[group]
        halves = [_tokmap(f"grads_sum_chips_{n}",
                          lambda a, b0, b1, b2: ((a + b0.astype(F32)) + b1.astype(F32)) + b2.astype(F32),
                          [o, g[0], g[1], g[2]], [], [(o.shape[1], F32)])[0] for n, o, g in zip(names, own, got)]
        others = _swap_with_sibling(f"grads_swap_cores_{group}", halves)
        for n, half, other in zip(names, halves, others):
            res = _tokmap(f"adamw_{n}", update, [half, other, wts[n], moms[n], vars_[n]], [], [(half.shape[1], F32)] * 4)
            for d, r in zip(outs, res):
                d[n] = r

    s_red = _all_reduce_small(_pack_small(gs, loss[0, 0]))
    res = _tokmap(
        "adamw_small", lambda g, wv, mv, vv: (g,) + _adamw(g, wv, mv, vv),
        [s_red, _pack_small(small), _pack_small({n: moms[n] for n, _ in SMALL}), _pack_small({n: vars_[n] for n, _ in SMALL})],
        [], [(D, F32)] * 4)
    for d, packed in zip(outs, res):
        d.update(_unpack_small(packed)[0])
    _, total_loss = _unpack_small(s_red)
    return (total_loss, gx[None], *[d[n][None] for d in outs for n in WEIGHTS])
```

```python
import functools

import numpy as np
import jax
import jax.numpy as jnp
from jax import lax
from jax.experimental import pallas as pl
from jax.experimental.pallas import tpu as pltpu

F32, BF16 = jnp.float32, jnp.bfloat16
SDS = jax.ShapeDtypeStruct
MESH = pl.DeviceIdType.MESH

D = 1024
HD = 64
QB = 128
DSA_T_FWD, DSA_T_BWD = 512, 256
D_FF = 2816
SB_W, DSA_W, DSA_OUT_W, MEM_W = 512, 768, 256, 256
DSA_DILS = (1, 4, 16)
MEM_LEN = 256
N_CHIPS = 4
EPS = 1e-6
SCALE = HD ** -0.5
EXHAUSTED = -104.0
SB_FWD_HEADS = 4
SB_QB = 256
SB_WIN = 512
NEG = -1e30
VMEM_LIMIT = 56 * 1024 * 1024

ADAM_LR, ADAM_B1, ADAM_B2, ADAM_EPS, ADAM_WD, ADAM_STEP = 0.001, 0.9, 0.999, 1e-08, 0.01, 10

NN = (((1,), (0,)), ((), ()))
NT = (((1,), (1,)), ((), ()))
TN = (((0,), (0,)), ((), ()))

SHARDED = (
    ("ffn1_w1", (D, D_FF), 1), ("ffn1_w3", (D, D_FF), 1), ("ffn1_w2", (D_FF, D), 0),
    ("w_in", (D, 4096), 1), ("w_mem_kv", (D, 512), 0),
    ("w_branch_sb", (SB_W, D), 1), ("w_branch_dsa", (DSA_OUT_W, D), 1), ("w_branch_mem", (MEM_W, D), 1),
    ("w_gate", (D, 3 * D), 1), ("w_out", (D, D), 0),
    ("ffn2_w1", (D, D_FF), 1), ("ffn2_w3", (D, D_FF), 1), ("ffn2_w2", (D_FF, D), 0),
)
SHARDED_BY_NAME = {n: (sh, ax) for n, sh, ax in SHARDED}
GROUPS = {
    "ffn2": ("ffn2_w1", "ffn2_w3", "ffn2_w2"),
    "mid": ("w_in", "w_mem_kv", "w_branch_sb", "w_branch_dsa", "w_branch_mem", "w_gate", "w_out"),
    "ffn1": ("ffn1_w1", "ffn1_w3", "ffn1_w2"),
}
WEIGHT_PIECES = (
    (None, ("ffn1_w1",)),
    ("ffn1_up_a", ("ffn1_w3",)),
    ("ffn1_up_b", ("ffn1_w2", "w_in")),
    ("ffn1_down", ("w_gate", "w_mem_kv", "w_branch_sb", "w_branch_dsa", "w_branch_mem", "w_out")),
    ("proj_dsa", ("ffn2_w2",)),
    ("proj_gate", ("ffn2_w1", "ffn2_w3")),
)
GRAD_HOSTS = {"ffn2": "ffn2_bwd_dn", "mid": "ffn1_bwd_dw13", "ffn1": "ffn1_bwd_dn"}
SMALL = (("ffn1_norm", D), ("mix_norm", D), ("mem_norm", D), ("ffn2_norm", D), ("b_gate", 3 * D),
         ("qn_dsa", HD), ("kn_dsa", HD), ("qn_mem", HD), ("kn_mem", HD))
WEIGHTS = ("ffn1_norm", "ffn1_w1", "ffn1_w3", "ffn1_w2", "mix_norm", "mem_norm", "w_in", "w_mem_kv", "qn_dsa", "kn_dsa",
           "qn_mem", "kn_mem", "w_branch_sb", "w_branch_dsa", "w_branch_mem", "w_gate", "b_gate", "w_out", "ffn2_norm",
           "ffn2_w1", "ffn2_w3", "ffn2_w2")
SMALL_ROWS = 8


def _dot(a, b, dn=NN):
    return lax.dot_general(a, b, dn, preferred_element_type=F32)


def _dot01(x, m01, pieces=3):
    hi = x.astype(BF16)
    r1 = x - hi.astype(F32)
    mid = r1.astype(BF16)
    if pieces == 2:
        return _dot(hi, m01) + _dot(mid, m01)
    lo = (r1 - mid.astype(F32)).astype(BF16)
    return _dot(hi, m01) + _dot(mid, m01) + _dot(lo, m01)


def _pick(n, cands):
    for c in cands:
        if n % c == 0:
            return c
    raise ValueError(f"no tile for {n}")


def _from_dilated(v, d, scr):
    w = v.shape[1] // d
    v = v.astype(F32)
    for c in range(d):
        for p, buf in enumerate(scr[:w // 128]):
            buf[pl.ds(c, v.shape[0], stride=d), :] = v[:, c * w + 128 * p:c * w + 128 * (p + 1)]
    return jnp.concatenate([buf[...] for buf in scr[:w // 128]], axis=1)


def _to_dilated(v, d, scr):
    w = v.shape[1]
    for p, buf in enumerate(scr[:w // 128]):
        buf[...] = v[:, 128 * p:128 * (p + 1)].astype(F32)
    return jnp.concatenate([buf[pl.ds(c, v.shape[0] // d, stride=d), :] for c in range(d) for buf in scr[:w // 128]], axis=1)


def _tokmap(name, fn, tok_ins, consts, tok_outs, acc_outs=(), tile=512, dil_ins=None, dil_outs=None, place=None):
    dil_ins, dil_outs, place = dil_ins or {}, dil_outs or {}, place or {}
    bufs = [(j, buf) for j, (_, _, buf) in place.items() if buf is not None]
    n_buf = len(bufs)
    n = tok_ins[0].shape[0] * dil_ins.get(0, 1)
    tile = _pick(n, [t for t in (512, 352, 256, 128, 64, 32, 16, 8) if t <= tile])
    n_tin, n_in, n_tok, n_acc = len(tok_ins), len(tok_ins) + len(consts), len(tok_outs), len(acc_outs)
    n_scr = max([tok_ins[j].shape[1] // d // 128 for j, d in dil_ins.items() if d > 1]
                + [tok_outs[j][0] // 128 for j, d in dil_outs.items() if d > 1] + [0])

    def body(*refs):
        scr = refs[len(refs) - n_scr:]
        vals = [r[...] for r in refs[:n_in]]
        for j, d in dil_ins.items():
            if d > 1:
                vals[j] = _from_dilated(vals[j], d, scr)
        outs = fn(*vals)
        outs = list(outs) if isinstance(outs, (tuple, list)) else [outs]
        assert len(outs) == n_tok + n_acc, (name, len(outs))
        for j, d in dil_outs.items():
            if d > 1:
                outs[j] = _to_dilated(outs[j], d, scr)
        orefs = refs[n_in + n_buf:]
        for r, v in zip(orefs[:n_tok], outs[:n_tok]):
            r[...] = v.astype(r.dtype)
        if n_acc:
            @pl.when(pl.program_id(0) == 0)
            def _():
                for r in orefs[n_tok:n_tok + n_acc]:
                    r[...] = jnp.zeros(r.shape, r.dtype)
            for r, v in zip(orefs[n_tok:n_tok + n_acc], outs[n_tok:]):
                r[...] += v.astype(r.dtype)

    def tok_spec(width, d):
        return pl.BlockSpec((tile // d, d * width), lambda i: (i, 0))

    in_specs = [tok_spec(a.shape[1] // dil_ins.get(j, 1), dil_ins.get(j, 1)) for j, a in enumerate(tok_ins)]
    in_specs += [pl.BlockSpec(c.shape, lambda i: (0, 0)) for c in consts]
    in_specs += [HBM_SPEC] * n_buf
    out_specs = [tok_spec(w, dil_outs.get(j, 1)) for j, (w, _) in enumerate(tok_outs)]
    out_shape = [SDS((n // dil_outs.get(j, 1), w * dil_outs.get(j, 1)), dt) for j, (w, dt) in enumerate(tok_outs)]
    for j, (total, col_block, _) in place.items():
        out_specs[j] = pl.BlockSpec((tile, tok_outs[j][0]), lambda i, cb=col_block: (i, cb))
        out_shape[j] = SDS((n, total), tok_outs[j][1])
    out_specs += [pl.BlockSpec(s, lambda i: (0, 0)) for s in acc_outs]
    out_shape += [SDS(s, F32) for s in acc_outs]
    res = pl.pallas_call(
        body, name=name, grid=(n // tile,), in_specs=in_specs, out_specs=out_specs, out_shape=out_shape,
        scratch_shapes=[pltpu.VMEM((tile, 128), F32)] * n_scr,
        input_output_aliases={n_in + b: j for b, (j, _) in enumerate(bufs)},
        compiler_params=pltpu.CompilerParams(dimension_semantics=("arbitrary",), vmem_limit_bytes=VMEM_LIMIT),
    )(*tok_ins, *consts, *[buf for _, buf in bufs])
    return res


MATMUL_VMEM_BUDGET = 40 * 1024 * 1024


def _matmul_tiles(m, n, k, a_bytes, b_bytes, o_bytes, extra_bytes, whole_n=False):
    best = None
    for tk in [c for c in (3584, 2816, 2048, 1408, 1024, 512, 256, 128) if k % c == 0]:
        for tm in [c for c in (1408, 1024, 768, 512, 256, 128) if m % c == 0]:
            for tn in [n] if whole_n else [c for c in (1408, 1024, 768, 512, 256, 128) if n % c == 0]:
                need = 2 * tk * (tm * a_bytes + tn * b_bytes) + tm * tn * (2 * o_bytes + 2 * extra_bytes + 8)
                if need > MATMUL_VMEM_BUDGET:
                    continue
                score = (min(tm, 512) * min(tn, 512), tk, tm * tn, tn)
                if best is None or score > best[0]:
                    best = (score, (tm, tn, tk))
    return best[1]


class _Carry:
    def __init__(self, ins, outs, sems, copies, then):
        self.ins, self.outs, self.sems, self.copies, self.then = ins, outs, sems, copies, then


class _Plan:
    def __init__(self):
        self.pending = {}

    def put(self, host, carry):
        assert host not in self.pending, host
        self.pending[host] = carry

    def take(self, host):
        return self.pending.pop(host, None)


def _matmul(name, a, b, dn, out_dtype, epi=None, tiles=(), rows=(), plan=None, a_pro=None, n_sum=0):
    pro = a_pro
    a_list = list(a) if isinstance(a, (list, tuple)) else [a]
    a, n_parts = a_list[0], len(a_list)
    if dn == NN:
        (m, k), n = a.shape, b.shape[1]
    elif dn == NT:
        (m, k), n = a.shape, b.shape[0]
    else:
        (k, m), n = a.shape, b.shape[1]
    n_t, n_r = len(tiles), len(rows)
    if pro is not None:
        assert dn == NN and n == _pick(n, (1024, 512))
    if pro is not None:
        tm, tn, tk = _pick(m, (256, 128)), n, k
    else:
        tm, tn, tk = _matmul_tiles(m, n, k, a.dtype.itemsize, b.dtype.itemsize, jnp.dtype(out_dtype).itemsize,
                                   sum(t.dtype.itemsize for t in tiles), whole_n=n_sum > 0)
    nk = k // tk
    grid = (m // tm, n // tn, nk)
    assert pro is None or grid[1] == 1
    assert n_sum == 0 or grid[1] == 1
    carry = plan.take(name) if plan is not None else None
    n_ci, n_co = (len(carry.ins), len(carry.outs)) if carry else (0, 0)
    n_keep = 1 if pro is not None else 0

    def body(*refs):
        a_refs, b_ref, rest = refs[:n_parts], refs[n_parts], refs[n_parts + 1:]
        extras, rest = rest[:n_t + n_r], rest[n_t + n_r:]
        c_in, o_ref, rest = rest[:n_ci], rest[n_ci], rest[n_ci + 1:]
        sums, rest = rest[:n_sum], rest[n_sum:]
        keep, c_out, scratch = rest[:n_keep], rest[n_keep:n_keep + n_co], rest[n_keep + n_co:]
        ids = [pl.program_id(d) for d in range(3)]
        if n_sum:
            @pl.when((ids[0] == 0) & (ids[2] == 0))
            def _():
                for r in sums:
                    r[...] = jnp.zeros(r.shape, F32)
        if carry:
            sems = scratch[1:] if nk > 1 else scratch

            @pl.when((ids[0] == 0) & (ids[1] == 0) & (ids[2] == 0))
            def _():
                for cp in carry.copies(c_in, c_out, *sems):
                    cp.start()

        if pro is not None:
            av = pro(*[r[...] for r in a_refs])
            keep[0][...] = av
        else:
            av = a_refs[0][...].astype(BF16)
        part = _dot(av, b_ref[...].astype(BF16), dn)

        def finish(r):
            if epi is not None:
                r = epi(r, *[e[...] for e in extras])
            if n_sum:
                for ref, v in zip(sums, r[1:]):
                    ref[...] += v
                r = r[0]
            o_ref[...] = r.astype(o_ref.dtype)

        if nk == 1:
            finish(part)
        else:
            acc = scratch[0]

            @pl.when(ids[2] == 0)
            def _():
                acc[...] = part

            @pl.when(ids[2] > 0)
            def _():
                acc[...] += part

            @pl.when(ids[2] == nk - 1)
            def _():
                finish(acc[...])

        if carry:
            @pl.when((ids[0] == grid[0] - 1) & (ids[1] == grid[1] - 1) & (ids[2] == nk - 1))
            def _():
                for cp in carry.copies(c_in, c_out, *sems):
                    cp.wait()

    if dn == TN:
        a_specs = [pl.BlockSpec((tk, tm), lambda i, j, kk: (kk, i))]
    else:
        a_specs = [pl.BlockSpec((tm, tk), lambda i, j, kk: (i, kk))] * n_parts
    b_spec = pl.BlockSpec((tn, tk), lambda i, j, kk: (j, kk)) if dn == NT else pl.BlockSpec((tk, tn), lambda i, j, kk: (kk, j))
    in_specs = a_specs + [b_spec] + [pl.BlockSpec((tm, tn), lambda i, j, kk: (i, j)) for _ in tiles]
    in_specs += [pl.BlockSpec((1, tn), lambda i, j, kk: (0, j)) for _ in rows] + [HBM_SPEC] * n_ci
    res = pl.pallas_call(
        body, name=name, grid=grid, in_specs=in_specs,
        out_specs=[pl.BlockSpec((tm, tn), lambda i, j, kk: (i, j))] + [pl.BlockSpec((1, tn), lambda i, j, kk: (0, 0))] * n_sum
        + [pl.BlockSpec((tm, tk), lambda i, j, kk: (i, kk))] * n_keep + [HBM_SPEC] * n_co,
        out_shape=[SDS((m, n), out_dtype)] + [SDS((1, n), F32)] * n_sum + [SDS((m, k), BF16)] * n_keep
        + (list(carry.outs) if carry else []),
        scratch_shapes=([pltpu.VMEM((tm, tn), F32)] if nk > 1 else []) + (list(carry.sems) if carry else []),
        compiler_params=pltpu.CompilerParams(
            dimension_semantics=("arbitrary",) * 3 if (carry or n_sum) else ("parallel", "parallel", "arbitrary"),
            vmem_limit_bytes=VMEM_LIMIT),
    )(*a_list, b, *tiles, *rows, *(carry.ins if carry else []))
    if carry:
        carry.then(res[1 + n_sum + n_keep:])
    return tuple(res[:1 + n_sum + n_keep]) if n_sum + n_keep else res[0]


def _mean_all(v):
    return jnp.mean(v, axis=-1, keepdims=True)


def _head_sums(v, bd):
    w = bd.shape[0]
    return jnp.concatenate([_dot01(v[:, j:j + w], bd, 2) for j in range(0, v.shape[1], w)], axis=1)


def _mean_heads(bd):
    return lambda v: _head_sums(v, bd) * (1.0 / HD)


def _rms_fwd(x, g, mean):
    return x * lax.rsqrt(mean(x * x) + EPS) * g


def _rms_bwd(x, g, dy, mean):
    r = lax.rsqrt(mean(x * x) + EPS)
    dn = dy * g
    dx = r * dn - x * (r * r * r) * mean(dn * x)
    return dx, jnp.sum(dy * x * r, axis=0, keepdims=True)


def _swap_halves(x):
    w = x.shape[1]
    lane = lax.broadcasted_iota(jnp.int32, x.shape, 1)
    return jnp.where(lane % HD < HD // 2, pltpu.roll(x, w - HD // 2, 1), pltpu.roll(x, HD // 2, 1))


def _lanes(t, w):
    return jnp.tile(t, (1, w // t.shape[1]))


def _rope_fwd(x, cos, sin_signed):
    return x * _lanes(cos, x.shape[1]) + _swap_halves(x) * _lanes(sin_signed, x.shape[1])


def _rope_bwd(dy, cos, sin_signed):
    return dy * _lanes(cos, dy.shape[1]) + _swap_halves(dy * _lanes(sin_signed, dy.shape[1]))


def _bcast_heads(cols):
    return jnp.concatenate([jnp.broadcast_to(c, (c.shape[0], HD)) for c in cols], axis=1)


def _softplus(z):
    return jnp.maximum(z, 0.0) + jnp.log(1.0 + jnp.exp(-jnp.abs(z)))


def _block_diag(w):
    h = np.arange(w) // HD
    return jnp.asarray(h[:, None] == h[None, :], BF16)


def _sb_window(i, t):
    hi = (i + 1) * SB_QB - t * SB_WIN
    lo = hi - SB_WIN
    ws = pl.multiple_of(jnp.maximum(lo, 0), SB_QB)
    kpos = ws + lax.broadcasted_iota(jnp.int32, (SB_QB, SB_WIN), 1)
    qpos = i * SB_QB + lax.broadcasted_iota(jnp.int32, (SB_QB, SB_WIN), 0)
    return (kpos < qpos) & (kpos >= lo) & (kpos < hi), ws


def _sb_fwd(qkv):
    s = qkv.shape[0]
    assert s >= SB_WIN
    nq = s // SB_QB
    nh = SB_FWD_HEADS
    bw = HD * nh
    ngroups = SB_W // bw

    def body(q_ref, k_ref, v_ref, later_ref, o_ref, tot_ref, w0_ref, b0_ref, nb_ref):
        p, i = pl.program_id(0), pl.program_id(1)
        q = q_ref[...] * SCALE
        later_of = later_ref[...]

        def window(t, tots, outs, keep):
            mask, ws = _sb_window(i, t)
            kw, vw = k_ref[pl.ds(ws, SB_WIN), :], v_ref[pl.ds(ws, SB_WIN), :]
            new_t, new_o, w_all, b_all = [], [], [], []
            for hh in range(nh):
                sl = slice(HD * hh, HD * hh + HD)
                z = _dot(q[:, sl], kw[:, sl], NT)
                sp = _softplus(z)
                lf = jnp.where(mask, -sp, 0.0)
                lf_far, lf_near = lf[:, :SB_WIN // 2], lf[:, SB_WIN // 2:]
                later = tots[hh] + jnp.concatenate(
                    [_dot01(lf_far, later_of, 2) + jnp.sum(lf_near, axis=1, keepdims=True), _dot01(lf_near, later_of, 2)], axis=1)
                w = jnp.where(mask, jnp.exp(z - sp + later), 0.0).astype(BF16)
                new_o.append(outs[hh] + _dot(w, vw[:, sl]))
                new_t.append(tots[hh] + jnp.sum(lf, axis=1, keepdims=True))
                if keep:
                    w_all.append(w)
                    b_all.append(jnp.where(mask, jnp.exp(z - sp), 0.0).astype(BF16))
            if keep:
                w0_ref[...] = jnp.concatenate(w_all, axis=1)
                b0_ref[...] = jnp.concatenate(b_all, axis=1)
            alive = functools.reduce(jnp.maximum, [jnp.max(v) for v in new_t])
            return t + 1, alive, tuple(new_t), tuple(new_o)

        zt, zo = jnp.zeros((SB_QB, 1), F32), jnp.zeros((SB_QB, HD), F32)
        first = window(jnp.int32(0), (zt,) * nh, (zo,) * nh, True)
        t, _, tots, outs = lax.while_loop(lambda c: ((i + 1) * SB_QB - c[0] * SB_WIN > 0) & (c[1] > EXHAUSTED),
                                          lambda c: window(c[0], c[2], c[3], False), first)
        o_ref[...] = jnp.concatenate(outs, axis=1).astype(o_ref.dtype)
        tot_ref[...] = _bcast_heads(tots)
        nb_ref[p, i] = t

    whole = lambda off: pl.BlockSpec((s, bw), lambda p, i: (0, off + p), pipeline_mode=pl.Buffered(1))
    tile = pl.BlockSpec((SB_QB, bw), lambda p, i: (i, p))
    tri = pl.BlockSpec((SB_WIN // 2, SB_WIN // 2), lambda p, i: (0, 0), pipeline_mode=pl.Buffered(1))
    near = pl.BlockSpec((SB_QB, nh * SB_WIN), lambda p, i: (i, p))
    n_heads = SB_W // HD
    idx = np.arange(SB_WIN // 2)
    return pl.pallas_call(
        body, name="sb_fwd", grid=(ngroups, nq),
        in_specs=[tile, whole(ngroups), whole(2 * ngroups), tri],
        out_specs=[tile, tile, near, near, pl.BlockSpec(memory_space=pltpu.SMEM)],
        out_shape=[SDS((s, SB_W), BF16), SDS((s, SB_W), F32), SDS((s, n_heads * SB_WIN), BF16), SDS((s, n_heads * SB_WIN), BF16),
                   SDS((ngroups, nq), jnp.int32)],
        compiler_params=pltpu.CompilerParams(dimension_semantics=("arbitrary", "arbitrary"), vmem_limit_bytes=VMEM_LIMIT),
    )(qkv, qkv, qkv, jnp.asarray(idx[:, None] > idx[None, :], BF16))


def _sb_bwd(qkv, do, tot, nblk, w0, b0, buf, col):
    s = qkv.shape[0]
    nq = s // SB_QB
    npairs = SB_W // 128

    def body(nb_ref, q_ref, k_ref, v_ref, do_ref, tot_ref, upto_ref, before_ref, w0_ref, b0_ref, buf_ref,
             dq_ref, dk_ref, dv_ref):
        p, i = pl.program_id(0), pl.program_id(1)

        @pl.when(i == 0)
        def _():
            dk_ref[...] = jnp.zeros(dk_ref.shape, F32)
            dv_ref[...] = jnp.zeros(dv_ref.shape, F32)

        upto = upto_ref[...]
        before = before_ref[...]
        q, dout, tt = q_ref[...] * SCALE, do_ref[...], tot_ref[...]
        n = nb_ref[p * 2 // SB_FWD_HEADS, i]

        def step(it, c):
            pres, gpres, dqs = c
            mask, ws = _sb_window(i, n - 1 - it)
            kw, vw = k_ref[pl.ds(ws, SB_WIN), :], v_ref[pl.ds(ws, SB_WIN), :]
            new_p, new_g, new_dq, dks, dvs = [], [], [], [], []
            for hh in range(2):
                sl = slice(HD * hh, HD * hh + HD)
                z = _dot(q[:, sl], kw[:, sl], NT)
                sp = _softplus(z)
                lf = jnp.where(mask, -sp, 0.0)
                later = tt[:, HD * hh:HD * hh + 1] - (pres[hh] + _dot01(lf, upto, 2))
                w = jnp.where(mask, jnp.exp(z - sp + later), 0.0)
                beta = jnp.exp(z - sp)
                g = _dot(dout[:, sl], vw[:, sl], NT) * w
                g_far = gpres[hh] + _dot(g.astype(BF16), before)
                dz = jnp.where(mask, g * (1.0 - beta) - beta * g_far, 0.0).astype(BF16)
                new_dq.append(dqs[hh] + _dot(dz, kw[:, sl]))
                dks.append(_dot(dz, q[:, sl], TN))
                dvs.append(_dot(w.astype(BF16), dout[:, sl], TN))
                new_p.append(pres[hh] + jnp.sum(lf, axis=1, keepdims=True))
                new_g.append(gpres[hh] + jnp.sum(g, axis=1, keepdims=True))
            dk_ref[pl.ds(ws, SB_WIN), :] += jnp.concatenate(dks, axis=1)
            dv_ref[pl.ds(ws, SB_WIN), :] += jnp.concatenate(dvs, axis=1)
            return tuple(new_p), tuple(new_g), tuple(new_dq)

        zt, zo = jnp.zeros((SB_QB, 1), F32), jnp.zeros((SB_QB, HD), F32)
        _, gpres, dqs = lax.fori_loop(0, n - 1, step, ((zt, zt), (zt, zt), (zo, zo)))
        _, ws = _sb_window(i, 0)
        kw, vw = k_ref[pl.ds(ws, SB_WIN), :], v_ref[pl.ds(ws, SB_WIN), :]
        dqs, dks, dvs = list(dqs), [], []
        for hh in range(2):
            sl = slice(HD * hh, HD * hh + HD)
            w = w0_ref[:, SB_WIN * hh:SB_WIN * (hh + 1)]
            beta = b0_ref[:, SB_WIN * hh:SB_WIN * (hh + 1)].astype(F32)
            g = _dot(dout[:, sl], vw[:, sl], NT) * w.astype(F32)
            g16, half = g.astype(BF16), SB_WIN // 2
            g_far = gpres[hh] + jnp.concatenate(
                [_dot(g16[:, :half], before[:half, :half]),
                 _dot(g16[:, half:], before[:half, :half]) + jnp.sum(g[:, :half], axis=1, keepdims=True)], axis=1)
            dz = (g * (1.0 - beta) - beta * g_far).astype(BF16)
            dqs[hh] = dqs[hh] + _dot(dz, kw[:, sl])
            dks.append(_dot(dz, q[:, sl], TN))
            dvs.append(_dot(w, dout[:, sl], TN))
        dk_ref[pl.ds(ws, SB_WIN), :] += jnp.concatenate(dks, axis=1)
        dv_ref[pl.ds(ws, SB_WIN), :] += jnp.concatenate(dvs, axis=1)
        dq_ref[...] = (jnp.concatenate(dqs, axis=1) * SCALE).astype(dq_ref.dtype)

    whole_in = lambda off: pl.BlockSpec((s, 128), lambda p, i: (0, off + p), pipeline_mode=pl.Buffered(1))
    whole_out = pl.BlockSpec((s, 128), lambda p, i: (0, p), pipeline_mode=pl.Buffered(1))
    tile = pl.BlockSpec((SB_QB, 128), lambda p, i: (i, p))
    near = pl.BlockSpec((SB_QB, 2 * SB_WIN), lambda p, i: (i, p))
    dq_tile = pl.BlockSpec((SB_QB, 128), lambda p, i: (i, col // 128 + p))
    tri = pl.BlockSpec((SB_WIN, SB_WIN), lambda p, i: (0, 0), pipeline_mode=pl.Buffered(1))
    idx = np.arange(SB_WIN)
    return pl.pallas_call(
        body, name="sb_bwd", grid=(npairs, nq),
        in_specs=[pl.BlockSpec(memory_space=pltpu.SMEM), tile, whole_in(npairs), whole_in(2 * npairs), tile, tile, tri, tri,
                  near, near, HBM_SPEC],
        out_specs=[dq_tile, whole_out, whole_out],
        out_shape=[SDS(buf.shape, buf.dtype)] + [SDS((s, SB_W), F32)] * 2,
        input_output_aliases={10: 0},
        compiler_params=pltpu.CompilerParams(dimension_semantics=("arbitrary", "arbitrary"), vmem_limit_bytes=VMEM_LIMIT),
    )(nblk, qkv, qkv, qkv, do, tot, jnp.asarray(idx[:, None] <= idx[None, :], BF16), jnp.asarray(idx[:, None] < idx[None, :], BF16),
      w0, b0, buf)


def _dsa_mask(DSA_T, has_prev):
    r = lax.broadcasted_iota(jnp.int32, (DSA_T, QB + DSA_T), 0)
    j = lax.broadcasted_iota(jnp.int32, (DSA_T, QB + DSA_T), 1) - QB
    return (j <= r) & (j >= r - QB) & ((j >= 0) | has_prev)


def _dsa_fwd(q, k, v, dil):
    n = q.shape[0]
    DSA_T = DSA_T_FWD
    nt = n // DSA_T

    def body(q_ref, kc_ref, kp_ref, vc_ref, vp_ref, o_ref, lse_ref):
        mask = _dsa_mask(DSA_T, pl.program_id(1) > 0)
        outs, lses = [], []
        for hh in range(DSA_OUT_W // HD):
            sl = slice(HD * hh, HD * hh + HD)
            kcat = jnp.concatenate([kp_ref[:, sl], kc_ref[:, sl]], axis=0)
            vcat = jnp.concatenate([vp_ref[:, sl], vc_ref[:, sl]], axis=0)
            sc = jnp.where(mask, _dot(q_ref[:, sl] * SCALE, kcat, NT), NEG)
            m = jnp.max(sc, axis=1, keepdims=True)
            p = jnp.exp(sc - m)
            den = jnp.sum(p, axis=1, keepdims=True)
            outs.append(_dot(p.astype(BF16), vcat) / den)
            lses.append(m + jnp.log(den))
        o_ref[...] = jnp.concatenate(outs, axis=1)
        lse_ref[...] = _bcast_heads(lses)

    cur = pl.BlockSpec((DSA_T, DSA_OUT_W), lambda c, i: (i, c))
    prev = pl.BlockSpec((QB, DSA_OUT_W), lambda c, i: (jnp.maximum(i * (DSA_T // QB) - 1, 0), c))
    o, lse = pl.pallas_call(
        body, name=f"dsa_fwd_d{dil}", grid=(dil, nt), in_specs=[cur, cur, prev, cur, prev], out_specs=[cur, cur],
        out_shape=[SDS((n, dil * DSA_OUT_W), F32)] * 2,
        compiler_params=pltpu.CompilerParams(dimension_semantics=("parallel", "parallel")),
    )(q, k, k, v, v)
    return o, lse


def _dsa_bwd(q, k, v, do, cc, lse, dil):
    n = q.shape[0]
    DSA_T = DSA_T_BWD
    nt = n // DSA_T
    per = DSA_T // QB

    def body(qj_ref, qn_ref, kp_ref, kj_ref, vp_ref, vj_ref, doj_ref, don_ref, cj_ref, cn_ref, lj_ref, ln_ref,
             dq_ref, dk_ref, dv_ref):
        j = pl.program_id(1)
        mask = _dsa_mask(DSA_T, j > 0)
        r = lax.broadcasted_iota(jnp.int32, (QB, DSA_T), 0)
        kk = lax.broadcasted_iota(jnp.int32, (QB, DSA_T), 1)
        m_next = (kk >= r + QB) & (j + 1 < nt)
        dqs, dks, dvs = [], [], []
        for hh in range(DSA_OUT_W // HD):
            sl = slice(HD * hh, HD * hh + HD)
            one = slice(HD * hh, HD * hh + 1)
            kj, vj, doj, don = (t[:, sl] for t in (kj_ref, vj_ref, doj_ref, don_ref))
            qj, qn = qj_ref[:, sl] * SCALE, qn_ref[:, sl] * SCALE
            kcat = jnp.concatenate([kp_ref[:, sl], kj], axis=0)
            vcat = jnp.concatenate([vp_ref[:, sl], vj], axis=0)
            p1 = jnp.where(mask, jnp.exp(_dot(qj, kcat, NT) - lj_ref[:, one]), 0.0)
            ds1 = (p1 * (_dot(doj, vcat, NT) + cj_ref[:, one])).astype(BF16)
            p2 = jnp.where(m_next, jnp.exp(_dot(qn, kj, NT) - ln_ref[:, one]), 0.0)
            ds2 = (p2 * (_dot(don, vj, NT) + cn_ref[:, one])).astype(BF16)
            dqs.append(_dot(ds1, kcat) * SCALE)
            dks.append(_dot(ds1[:, QB:], qj, TN) + _dot(ds2, qn, TN))
            dvs.append(_dot(p1[:, QB:].astype(BF16), doj, TN) + _dot(p2.astype(BF16), don, TN))
        dq_ref[...] = jnp.concatenate(dqs, axis=1).astype(dq_ref.dtype)
        dk_ref[...] = jnp.concatenate(dks, axis=1).astype(dk_ref.dtype)
        dv_ref[...] = jnp.concatenate(dvs, axis=1).astype(dv_ref.dtype)

    cur = pl.BlockSpec((DSA_T, DSA_OUT_W), lambda c, j: (j, c))
    prev = pl.BlockSpec((QB, DSA_OUT_W), lambda c, j: (jnp.maximum(j * per - 1, 0), c))
    nxt = pl.BlockSpec((QB, DSA_OUT_W), lambda c, j: (jnp.minimum((j + 1) * per, n // QB - 1), c))
    dq, dk, dv = pl.pallas_call(
        body, name=f"dsa_bwd_d{dil}", grid=(dil, nt),
        in_specs=[cur, nxt, prev, cur, prev, cur, cur, nxt, cur, nxt, cur, nxt], out_specs=[cur, cur, cur],
        out_shape=[SDS((n, dil * DSA_OUT_W), BF16)] * 3,
        compiler_params=pltpu.CompilerParams(dimension_semantics=("parallel", "parallel")),
    )(q, q, k, k, v, v, do, do, cc, cc, lse, lse)
    return dq, dk, dv


def _norm_bwd_epi(acc, xv, dyv, g):
    dx, dg = _rms_bwd(xv, g, acc, _mean_all)
    return dx + dyv, dg


def _ffn_fwd(tag, x, gain, w1, w3, w2, plan=None, target=None):
    n = _tokmap(f"{tag}_norm", lambda xv, g: _rms_fwd(xv, g, _mean_all), [x], [gain], [(D, BF16)])[0]
    a_arr = _matmul(f"{tag}_up_a", n, w1(), NN, BF16, plan=plan)
    b_arr = _matmul(f"{tag}_up_b", n, w3(), NN, BF16, plan=plan)

    def gate(av, bv):
        a, b = av.astype(F32), bv.astype(F32)
        return (a * jax.nn.sigmoid(a) * b).astype(BF16)

    if target is None:
        y, h = _matmul(f"{tag}_down", [a_arr, b_arr], w2(), NN, F32, epi=lambda acc, res: res + 0.5 * acc, tiles=[x], plan=plan,
                       a_pro=gate)
        return y, (n, a_arr, b_arr, h)

    def loss_epi(acc, res, tv):
        e = res + 0.5 * acc - tv
        return e * (1.0 / D), (0.5 / D) * jnp.sum(e * e, axis=0, keepdims=True)

    dy, loss_row, h = _matmul(f"{tag}_down", [a_arr, b_arr], w2(), NN, F32, epi=loss_epi, tiles=[x, target], plan=plan,
                              a_pro=gate, n_sum=1)
    return (dy, loss_row), (n, a_arr, b_arr, h)


def _ffn_bwd(tag, x, gain, w13, w2, saved, dy, plan=None, on_dw=None, final=False):
    n, a_arr, b_arr, h = saved
    dh = _matmul(f"{tag}_bwd_dh", dy, w2, NT, BF16, epi=lambda acc: 0.5 * acc)

    def gate_bwd(av, bv, dhv):
        a, b, dhf = av.astype(F32), bv.astype(F32), dhv.astype(F32)
        sg = jax.nn.sigmoid(a)
        da = dhf * b * (sg * (1.0 + a * (1.0 - sg)))
        return jnp.concatenate([da, dhf * (a * sg)], axis=1)

    dab = _tokmap(f"{tag}_bwd_gate", gate_bwd, [a_arr, b_arr, dh], [], [(2 * D_FF, BF16)], tile=256)[0]
    dw2 = _matmul(f"{tag}_bwd_dw2", h, dy, TN, F32, epi=lambda acc: 0.5 * acc)
    dw13 = _matmul(f"{tag}_bwd_dw13", n, dab, TN, F32, plan=plan)
    if on_dw is not None:
        on_dw(dw13, dw2)
    if not final:
        dx, dgain = _matmul(f"{tag}_bwd_dn", dab, w13, NT, F32, epi=_norm_bwd_epi, tiles=[x, dy], rows=[gain], plan=plan,
                            n_sum=1)
    else:
        dn = _matmul(f"{tag}_bwd_dn", dab, w13, NT, F32, plan=plan)
        dx, dgain = _tokmap(f"{tag}_bwd_norm", lambda xv, dnv, dyv, g: _norm_bwd_epi(dnv, xv, dyv, g), [x, dn, dy], [gain],
                            [(D, F32)], [(1, D)])
    return dx, dgain, dw13, dw2


def _rope_tables(s):
    half = HD // 2
    inv_freq = jnp.power(10000.0, -jnp.arange(half, dtype=F32) / half)
    ang = jnp.arange(s).astype(F32)[:, None] * inv_freq[None, :]
    cos, sin = jnp.cos(ang), jnp.sin(ang)
    return jnp.tile(jnp.concatenate([cos, cos], axis=1), (1, 2)), jnp.tile(jnp.concatenate([-sin, sin], axis=1), (1, 2))


def _local_step(x, mem, tgt, w, sm, plan=None, on_grads=None):
    s = x.shape[0]
    assert s % (max(DSA_T_FWD, DSA_T_BWD) * max(DSA_DILS)) == 0
    on_grads = on_grads or (lambda group, grads: None)
    c_sb, c_dsa, c_qm, c_all = 3 * D, 3 * D + 3 * SB_W, 3 * D + 3 * SB_W + 3 * DSA_W, 3 * D + 4096
    cos, sin = _rope_tables(s)
    bd768 = bd256 = _block_diag(128)
    gq_dsa, gk_dsa = jnp.tile(sm["qn_dsa"], (1, DSA_W // HD)), jnp.tile(sm["kn_dsa"], (1, DSA_W // HD))
    gq_mem, gk_mem = jnp.tile(sm["qn_mem"], (1, MEM_W // HD)), jnp.tile(sm["kn_mem"], (1, MEM_W // HD))

    x1, ffn1_saved = _ffn_fwd("ffn1", x, sm["ffn1_norm"], lambda: w["ffn1_w1"], lambda: w["ffn1_w3"], lambda: w["ffn1_w2"], plan)
    w13_1 = jnp.concatenate([w["ffn1_w1"], w["ffn1_w3"]], axis=1)
    w_all = jnp.concatenate([w["w_gate"], w["w_in"]], axis=1)
    wb_sb, wb_dsa, wb_mem = w["w_branch_sb"], w["w_branch_dsa"], w["w_branch_mem"]
    hmix = _tokmap("mix_norm", lambda xv, g: _rms_fwd(xv, g, _mean_all), [x1], [sm["mix_norm"]], [(D, BF16)])[0]
    qkv_sb = _matmul("proj_sb", hmix, w_all[:, c_sb:c_dsa], NN, BF16)
    qkv_dsa = _matmul("proj_dsa", hmix, w_all[:, c_dsa:c_qm], NN, BF16, plan=plan)
    q_mem = _matmul("proj_qmem", hmix, w_all[:, c_qm:], NN, BF16)
    gpre = _matmul("proj_gate", hmix, w_all[:, :c_sb], NN, BF16, epi=lambda acc, b: acc + b, rows=[sm["b_gate"]], plan=plan)

    o_sb, sb_tot, sb_w0, sb_b0, sb_nblk = _sb_fwd(qkv_sb)

    def dsa_prep(qkv, cs, sn, gq, gk, bd):
        mean = _mean_heads(bd)
        qn = _rope_fwd(_rms_fwd(qkv[:, :DSA_W].astype(F32), gq, mean), cs, sn)
        kn = _rope_fwd(_rms_fwd(qkv[:, DSA_W:2 * DSA_W].astype(F32), gk, mean), cs, sn)
        v = qkv[:, 2 * DSA_W:]
        outs = []
        for t in (qn, kn, v):
            outs += [t[:, DSA_OUT_W * g:DSA_OUT_W * (g + 1)] for g in range(3)]
        return outs

    dsa_in = _tokmap("dsa_prep", dsa_prep, [qkv_dsa, cos, sin], [gq_dsa, gk_dsa, bd768], [(DSA_OUT_W, BF16)] * 9, tile=512,
                     dil_outs={j: DSA_DILS[j % 3] for j in range(9)})
    dsa_q, dsa_k, dsa_v = dsa_in[0:3], dsa_in[3:6], dsa_in[6:9]
    dsa_o, dsa_lse = zip(*[_dsa_fwd(dsa_q[g], dsa_k[g], dsa_v[g], DSA_DILS[g]) for g in range(3)])

    def alphas(l0, l1, l2):
        m = jnp.maximum(jnp.maximum(l0, l1), l2)
        e = [jnp.exp(l - m) for l in (l0, l1, l2)]
        tot = e[0] + e[1] + e[2]
        return [t / tot for t in e]

    def dsa_mix(o0, o1, o2, l0, l1, l2):
        a = alphas(l0, l1, l2)
        return a[0] * o0 + a[1] * o1 + a[2] * o2

    o_dsa = _tokmap("dsa_mix", dsa_mix, [*dsa_o, *dsa_lse], [], [(DSA_OUT_W, BF16)], tile=512,
                    dil_ins={j: DSA_DILS[j % 3] for j in range(6)})[0]

    def mem_kv(memv, g, wkv, gk, bd):
        kv = _dot(_rms_fwd(memv, g, _mean_all).astype(BF16), wkv)
        return _rms_fwd(kv[:, :MEM_W], gk, _mean_heads(bd)), kv[:, MEM_W:]

    km, vm = _tokmap("mem_kv", mem_kv, [mem], [sm["mem_norm"], w["w_mem_kv"], gk_mem, bd256], [(MEM_W, BF16)] * 2)

    def mem_probs(qv, kmv, gq, bd):
        qn = _rms_fwd(qv.astype(F32), gq, _mean_heads(bd)).astype(BF16)
        ps = []
        for h in range(MEM_W // HD):
            sl = slice(HD * h, HD * h + HD)
            sc = _dot(qn[:, sl], kmv[:, sl], NT) * SCALE
            e = jnp.exp(sc - jnp.max(sc, axis=1, keepdims=True))
            ps.append(e / jnp.sum(e, axis=1, keepdims=True))
        return qn, ps

    def mem_attn(qv, kmv, vmv, gq, bd):
        _, ps = mem_probs(qv, kmv, gq, bd)
        return jnp.concatenate([_dot(p.astype(BF16), vmv[:, HD * h:HD * h + HD]) for h, p in enumerate(ps)], axis=1)

    o_mem = _tokmap("mem_attn", mem_attn, [q_mem], [km, vm, gq_mem, bd256], [(MEM_W, BF16)])[0]

    def merge(osb, odsa, omem, gp, w_sb, w_dsa, w_mem):
        gates = jax.nn.sigmoid(gp.astype(F32))
        ys = (_dot(osb, w_sb), _dot(odsa, w_dsa), _dot(omem, w_mem))
        return gates, ys, gates[:, :D] * ys[0] + gates[:, D:2 * D] * ys[1] + gates[:, 2 * D:] * ys[2]

    merged = _tokmap("merge", lambda *a: merge(*a)[2], [o_sb, o_dsa, o_mem, gpre], [wb_sb, wb_dsa, wb_mem], [(D, BF16)],
                     tile=512)[0]
    x2 = _matmul("out_proj", merged, w["w_out"], NN, F32, epi=lambda acc, res: res + acc, tiles=[x1])
    w13_2 = jnp.concatenate([w["ffn2_w1"], w["ffn2_w3"]], axis=1)
    (dy, loss_row), ffn2_saved = _ffn_fwd("ffn2", x2, sm["ffn2_norm"], lambda: w["ffn2_w1"], lambda: w["ffn2_w3"],
                                          lambda: w["ffn2_w2"], target=tgt)
    loss = jnp.sum(loss_row).reshape(1, 1)

    gw, gs = {}, {}
    def ffn_grads(tag):
        def on_dw(dw13, dw2):
            gw[f"{tag}_w1"], gw[f"{tag}_w3"], gw[f"{tag}_w2"] = dw13[:, :D_FF], dw13[:, D_FF:], dw2
            on_grads(tag, {n: gw[n] for n in (f"{tag}_w1", f"{tag}_w3", f"{tag}_w2")})
        return on_dw

    dx2, gs["ffn2_norm"], _, _ = _ffn_bwd("ffn2", x2, sm["ffn2_norm"], w13_2, w["ffn2_w2"], ffn2_saved, dy, plan,
                                          ffn_grads("ffn2"))
    dmerged = _matmul("out_proj_bwd_dx", dx2, w["w_out"], NT, BF16)
    gw["w_out"] = _matmul("out_proj_bwd_dw", merged, dx2, TN, F32)

    def merge_bwd(osb, odsa, omem, gp, dm, w_sb, w_dsa, w_mem):
        gates, ys, _ = merge(osb, odsa, omem, gp, w_sb, w_dsa, w_mem)
        dmf = dm.astype(F32)
        dgp, dos, dws = [], [], []
        for b, (ov, wv) in enumerate(((osb, w_sb), (odsa, w_dsa), (omem, w_mem))):
            gb = gates[:, D * b:D * (b + 1)]
            dgp.append(dmf * ys[b] * gb * (1.0 - gb))
            dyb = (dmf * gb).astype(BF16)
            dos.append(_dot(dyb, wv, NT))
            dws.append(_dot(ov, dyb, TN))
        dgp = jnp.concatenate(dgp, axis=1)
        return dos[0], dos[1], dos[2], dgp, dws[0], dws[1], dws[2], jnp.sum(dgp, axis=0, keepdims=True)

    do_sb, do_dsa, do_mem, dgpre, gw["w_branch_sb"], gw["w_branch_dsa"], gw["w_branch_mem"], gs["b_gate"] = _tokmap(
        "merge_bwd", merge_bwd, [o_sb, o_dsa, o_mem, gpre, dmerged], [wb_sb, wb_dsa, wb_mem],
        [(SB_W, BF16), (DSA_OUT_W, F32), (MEM_W, BF16), (3 * D, BF16)],
        [(SB_W, D), (DSA_OUT_W, D), (MEM_W, D), (1, 3 * D)], tile=512, place={3: (c_all, 0, None)})

    dall, dk_sb, dv_sb = _sb_bwd(qkv_sb, do_sb, sb_tot, sb_nblk, sb_w0, sb_b0, dgpre, c_sb)
    dall = lax.dynamic_update_slice(dall, dk_sb.astype(BF16), (0, c_sb + SB_W))
    dall = lax.dynamic_update_slice(dall, dv_sb.astype(BF16), (0, c_sb + 2 * SB_W))

    def dsa_mix_bwd(o0, o1, o2, l0, l1, l2, dov, bd):
        a = alphas(l0, l1, l2)
        omix = a[0] * o0 + a[1] * o1 + a[2] * o2
        dot_o = _head_sums(dov * omix, bd)
        return [dov * t for t in a] + [-t * dot_o for t in a]

    mixb = _tokmap("dsa_mix_bwd", dsa_mix_bwd, [*dsa_o, *dsa_lse, do_dsa], [bd256],
                   [(DSA_OUT_W, BF16)] * 3 + [(DSA_OUT_W, F32)] * 3, tile=512,
                   dil_ins={j: DSA_DILS[j % 3] for j in range(6)}, dil_outs={j: DSA_DILS[j % 3] for j in range(6)})
    dsa_d = [_dsa_bwd(dsa_q[g], dsa_k[g], dsa_v[g], mixb[g], mixb[3 + g], dsa_lse[g], DSA_DILS[g]) for g in range(3)]

    def dsa_prep_bwd(qkv, cs, sn, *rest):
        dqs, dks, dvs, (gq, gk, bd) = rest[0:3], rest[3:6], rest[6:9], rest[9:]
        mean = _mean_heads(bd)
        dq, dgq = _rms_bwd(qkv[:, :DSA_W].astype(F32), gq, _rope_bwd(jnp.concatenate(dqs, axis=1), cs, sn), mean)
        dk, dgk = _rms_bwd(qkv[:, DSA_W:2 * DSA_W].astype(F32), gk, _rope_bwd(jnp.concatenate(dks, axis=1), cs, sn), mean)
        return jnp.concatenate([dq, dk] + list(dvs), axis=1), dgq, dgk

    dall, dgq_dsa, dgk_dsa = _tokmap(
        "dsa_prep_bwd", dsa_prep_bwd,
        [qkv_dsa, cos, sin] + [dsa_d[g][t] for t in range(3) for g in range(3)], [gq_dsa, gk_dsa, bd768],
        [(3 * DSA_W, BF16)], [(1, DSA_W), (1, DSA_W)], tile=512, dil_ins={3 + j: DSA_DILS[j % 3] for j in range(9)},
        place={0: (c_all, c_dsa // (3 * DSA_W), dall)})
    gs["qn_dsa"] = dgq_dsa.reshape(DSA_W // HD, HD).sum(axis=0, keepdims=True)
    gs["kn_dsa"] = dgk_dsa.reshape(DSA_W // HD, HD).sum(axis=0, keepdims=True)

    def mem_attn_bwd(qv, dov, kmv, vmv, gq, bd):
        qn, ps = mem_probs(qv, kmv, gq, bd)
        dqn, dkm, dvm = [], [], []
        for h, p in enumerate(ps):
            sl = slice(HD * h, HD * h + HD)
            dp = _dot(dov[:, sl], vmv[:, sl], NT)
            ds = (p * (dp - jnp.sum(p * dp, axis=1, keepdims=True)) * SCALE).astype(BF16)
            dqn.append(_dot(ds, kmv[:, sl]))
            dkm.append(_dot(ds, qn[:, sl], TN))
            dvm.append(_dot(p.astype(BF16), dov[:, sl], TN))
        dq, dgq = _rms_bwd(qv.astype(F32), gq, jnp.concatenate(dqn, axis=1), _mean_heads(bd))
        return dq, jnp.concatenate(dkm, axis=1), jnp.concatenate(dvm, axis=1), dgq

    dall, dkm, dvm, dgq_mem = _tokmap("mem_attn_bwd", mem_attn_bwd, [q_mem, do_mem], [km, vm, gq_mem, bd256],
                                      [(MEM_W, BF16)], [(MEM_LEN, MEM_W), (MEM_LEN, MEM_W), (1, MEM_W)],
                                      place={0: (c_all, c_qm // MEM_W, dall)})
    gs["qn_mem"] = dgq_mem.reshape(MEM_W // HD, HD).sum(axis=0, keepdims=True)

    def mem_kv_bwd(memv, dkmv, dvmv, g, wkv, gk, bd):
        memn = _rms_fwd(memv, g, _mean_all).astype(BF16)
        kv = _dot(memn, wkv)
        dk, dgk = _rms_bwd(kv[:, :MEM_W], gk, dkmv, _mean_heads(bd))
        dkv = jnp.concatenate([dk, dvmv], axis=1).astype(BF16)
        _, dg = _rms_bwd(memv, g, _dot(dkv, wkv, NT), _mean_all)
        return _dot(memn, dkv, TN), dg, dgk

    gw["w_mem_kv"], gs["mem_norm"], dgk_mem = _tokmap(
        "mem_kv_bwd", mem_kv_bwd, [mem, dkm, dvm], [sm["mem_norm"], w["w_mem_kv"], gk_mem, bd256], [],
        [(D, 2 * MEM_W), (1, D), (1, MEM_W)])
    gs["kn_mem"] = dgk_mem.reshape(MEM_W // HD, HD).sum(axis=0, keepdims=True)

    dx1, gs["mix_norm"] = _matmul("proj_bwd_dx", dall, w_all, NT, F32, epi=_norm_bwd_epi, tiles=[x1, dx2],
                                  rows=[sm["mix_norm"]], n_sum=1)
    dw_all = _matmul("proj_bwd_dw", hmix, dall, TN, F32)
    gw["w_gate"], gw["w_in"] = dw_all[:, :c_sb], dw_all[:, c_sb:]
    on_grads("mid", {n: gw[n] for n in GROUPS["mid"]})
    gx, gs["ffn1_norm"], _, _ = _ffn_bwd("ffn1", x, sm["ffn1_norm"], w13_1, w["ffn1_w2"], ffn1_saved, dx1, plan,
                                         ffn_grads("ffn1"), final=True)
    return loss, gx, gw, gs


def _shard_shape(name):
    shape, axis = SHARDED_BY_NAME[name]
    return (shape[0] // N_CHIPS, shape[1]) if axis == 0 else (shape[0], shape[1] // N_CHIPS)


def _full_from_shards(name, shards):
    axis = SHARDED_BY_NAME[name][1]
    return shards.reshape(SHARDED_BY_NAME[name][0]) if axis == 0 else jnp.concatenate(list(shards), axis=1)


def _shards_from_full(name, full, dtype):
    axis, n = SHARDED_BY_NAME[name][1], _shard_shape(name)
    return jnp.stack([lax.slice_in_dim(full, c * n[axis], (c + 1) * n[axis], axis=axis).astype(dtype) for c in range(N_CHIPS)])


def _own_shard(name, full, chip):
    axis, n = SHARDED_BY_NAME[name][1], _shard_shape(name)
    return lax.dynamic_slice_in_dim(full, chip * n[axis], n[axis], axis=axis)


SMALL_USED = sum(n for _, n in SMALL)


def _pack_small(d, loss=None):
    parts = [d[n].reshape(-1) for n, _ in SMALL]
    parts.append(jnp.zeros((1,), F32) if loss is None else loss.reshape(1))
    parts.append(jnp.zeros((SMALL_ROWS * D - SMALL_USED - 1,), F32))
    return jnp.concatenate(parts).reshape(SMALL_ROWS, D)


def _unpack_small(v):
    flat, out, r = v.reshape(-1), {}, 0
    for n, k in SMALL:
        out[n] = flat[r:r + k]
        r += k
    return out, flat[r]


def _place():
    return lax.axis_index("x"), lax.axis_index("y"), lax.axis_index("c")


def _other_chips(x, y):
    return [(1 - x, y), (x, 1 - y), (1 - x, 1 - y)]


HBM_SPEC = pl.BlockSpec(memory_space=pl.ANY)


def _chip_sems(n):
    return (pltpu.SemaphoreType.DMA((3 * n,)), pltpu.SemaphoreType.DMA((3 * n,)), pltpu.SemaphoreType.DMA((n,)))


def _gather_copies(ins, outs, send_sems, recv_sems, local_sems):
    x, y, c = _place()
    me = 2 * x + y
    copies = []
    for a, (src, out) in enumerate(zip(ins, outs)):
        copies.append(pltpu.make_async_copy(src, out.at[me], local_sems.at[a]))
        copies += [pltpu.make_async_remote_copy(src_ref=src, dst_ref=out.at[me], send_sem=send_sems.at[3 * a + k],
                                                recv_sem=recv_sems.at[3 * a + k], device_id=(px, py, c), device_id_type=MESH)
                   for k, (px, py) in enumerate(_other_chips(x, y))]
    return copies


def _scatter_copies(ins, outs, send_sems, recv_sems, local_sems):
    x, y, c = _place()
    return [pltpu.make_async_remote_copy(src_ref=src.at[2 * px + py], dst_ref=out.at[k], send_sem=send_sems.at[3 * a + k],
                                         recv_sem=recv_sems.at[3 * a + k], device_id=(px, py, c), device_id_type=MESH)
            for a, (src, out) in enumerate(zip(ins, outs)) for k, (px, py) in enumerate(_other_chips(x, y))]


def _all_gather_chips(arrays):
    n = len(arrays)

    def body(*refs):
        ins, outs = refs[:n], refs[n:2 * n]
        send1, recv1, send2, recv2, local_sems = refs[2 * n:]
        x, y, c = _place()
        me = 2 * x + y
        chips = _other_chips(x, y)
        local = [pltpu.make_async_copy(src, out.at[me], local_sems.at[a]) for a, (src, out) in enumerate(zip(ins, outs))]
        for cp in local:
            cp.start()

        def half(ref, chip, which):
            rows = ref.shape[-2] // 2
            return ref.at[chip, pl.ds(which * rows, rows)] if chip is not None else ref.at[pl.ds(which * rows, rows)]

        first = [pltpu.make_async_remote_copy(src_ref=half(src, None, c), dst_ref=half(out, me, c), send_sem=send1.at[3 * a + k],
                                              recv_sem=recv1.at[3 * a + k], device_id=(px, py, c), device_id_type=MESH)
                 for a, (src, out) in enumerate(zip(ins, outs)) for k, (px, py) in enumerate(chips)]
        for cp in first:
            cp.start()
        passed = []
        for a, out in enumerate(outs):
            for k, (px, py) in enumerate(chips):
                pltpu.make_async_remote_copy(src_ref=half(out, 2 * px + py, c), dst_ref=half(out, 2 * px + py, c),
                                             send_sem=send1.at[3 * a + k], recv_sem=recv1.at[3 * a + k],
                                             device_id=(px, py, c), device_id_type=MESH).wait_recv()
                cp = pltpu.make_async_remote_copy(src_ref=half(out, 2 * px + py, c), dst_ref=half(out, 2 * px + py, c),
                                                  send_sem=send2.at[3 * a + k], recv_sem=recv2.at[3 * a + k],
                                                  device_id=(x, y, 1 - c), device_id_type=MESH)
                cp.start()
                passed.append(cp)
        for a, out in enumerate(outs):
            for k, (px, py) in enumerate(chips):
                pltpu.make_async_remote_copy(src_ref=half(out, 2 * px + py, 1 - c), dst_ref=half(out, 2 * px + py, 1 - c),
                                             send_sem=send2.at[3 * a + k], recv_sem=recv2.at[3 * a + k],
                                             device_id=(x, y, 1 - c), device_id_type=MESH).wait_recv()
        for cp in first + passed:
            cp.wait_send()
        for cp in local:
            cp.wait()

    sems = pltpu.SemaphoreType.DMA((3 * n,))
    return pl.pallas_call(
        body, name="weights_all_gather", in_specs=[HBM_SPEC] * n, out_specs=[HBM_SPEC] * n,
        out_shape=[SDS((N_CHIPS,) + a.shape, a.dtype) for a in arrays],
        scratch_shapes=[sems, sems, sems, sems, pltpu.SemaphoreType.DMA((n,))],
    )(*arrays)


def _swap_with_sibling(name, arrays):
    n = len(arrays)

    def body(*refs):
        x, y, c = _place()
        send_sems, recv_sems = refs[2 * n:]
        copies = [pltpu.make_async_remote_copy(src_ref=refs[a], dst_ref=refs[n + a], send_sem=send_sems.at[a],
                                               recv_sem=recv_sems.at[a], device_id=(x, y, 1 - c), device_id_type=MESH)
                  for a in range(n)]
        for cp in copies:
            cp.start()
        for cp in copies:
            cp.wait()

    return pl.pallas_call(
        body, name=name, in_specs=[HBM_SPEC] * n, out_specs=[HBM_SPEC] * n, out_shape=[SDS(a.shape, a.dtype) for a in arrays],
        scratch_shapes=[pltpu.SemaphoreType.DMA((n,)), pltpu.SemaphoreType.DMA((n,))],
    )(*arrays)


def _all_reduce_small(v):
    n_dev = 8

    def body(v_ref, out_ref, land, send_sems, recv_sems):
        x, y, c = _place()
        me = 4 * x + 2 * y + c
        land[me] = v_ref[...]
        copies = []
        for k in range(1, n_dev):
            peer = (x ^ (k >> 2), y ^ ((k >> 1) & 1), c ^ (k & 1))
            copies.append(pltpu.make_async_remote_copy(src_ref=v_ref, dst_ref=land.at[me], send_sem=send_sems.at[k - 1],
                                                       recv_sem=recv_sems.at[k - 1], device_id=peer, device_id_type=MESH))
        for cp in copies:
            cp.start()
        for cp in copies:
            cp.wait()
        acc = land[0]
        for d in range(1, n_dev):
            acc = acc + land[d]
        out_ref[...] = acc

    return pl.pallas_call(
        body, name="small_all_reduce", in_specs=[pl.BlockSpec(memory_space=pltpu.VMEM)],
        out_specs=pl.BlockSpec(memory_space=pltpu.VMEM), out_shape=SDS(v.shape, v.dtype),
        scratch_shapes=[pltpu.VMEM((n_dev,) + v.shape, v.dtype), pltpu.SemaphoreType.DMA((n_dev - 1,)),
                        pltpu.SemaphoreType.DMA((n_dev - 1,))],
    )(v)


def _adamw(g, wv, m, v):
    m = ADAM_B1 * m + (1.0 - ADAM_B1) * g
    v = ADAM_B2 * v + (1.0 - ADAM_B2) * (g * g)
    m_hat = m / (1.0 - ADAM_B1 ** ADAM_STEP)
    v_hat = v / (1.0 - ADAM_B2 ** ADAM_STEP)
    delta = -ADAM_LR * (m_hat / (jnp.sqrt(v_hat) + ADAM_EPS) + ADAM_WD * wv)
    return delta, m, v


def kernel(x, mem, ffn1_norm, ffn1_w1, ffn1_w3, ffn1_w2, mix_norm, mem_norm, w_in, w_mem_kv, qn_dsa, kn_dsa, qn_mem, kn_mem, w_branch_sb, w_branch_dsa, w_branch_mem, w_gate, b_gate, w_out, ffn2_norm, ffn2_w1, ffn2_w3, ffn2_w2, loss_target, m_ffn1_norm, m_ffn1_w1, m_ffn1_w3, m_ffn1_w2, m_mix_norm, m_mem_norm, m_w_in, m_w_mem_kv, m_qn_dsa, m_kn_dsa, m_qn_mem, m_kn_mem, m_w_branch_sb, m_w_branch_dsa, m_w_branch_mem, m_w_gate, m_b_gate, m_w_out, m_ffn2_norm, m_ffn2_w1, m_ffn2_w3, m_ffn2_w2, v_ffn1_norm, v_ffn1_w1, v_ffn1_w3, v_ffn1_w2, v_mix_norm, v_mem_norm, v_w_in, v_w_mem_kv, v_qn_dsa, v_kn_dsa, v_qn_mem, v_kn_mem, v_w_branch_sb, v_w_branch_dsa, v_w_branch_mem, v_w_gate, v_b_gate, v_w_out, v_ffn2_norm, v_ffn2_w1, v_ffn2_w3, v_ffn2_w2):
    given = dict(locals())
    wts = {n: given[n][0] for n in WEIGHTS}
    moms = {n: given["m_" + n][0] for n in WEIGHTS}
    vars_ = {n: given["v_" + n][0] for n in WEIGHTS}

    plan = _Plan()
    x_i, y_i, _ = _place()
    my_chip = 2 * x_i + y_i

    full = {}

    def gathered(names):
        return lambda res: full.update({n: _full_from_shards(n, g) for n, g in zip(names, res)})

    for host, names in WEIGHT_PIECES:
        shards = [wts[n].astype(BF16) for n in names]
        if host is None:
            gathered(names)(_all_gather_chips(shards))
        else:
            plan.put(host, _Carry(shards, [SDS((N_CHIPS,) + a.shape, BF16) for a in shards], _chip_sems(len(names)),
                                  _gather_copies, gathered(names)))
    small = {n: wts[n].reshape(1, -1) for n, _ in SMALL}

    landed = {}

    def on_grads(group, grads):
        names = GROUPS[group]
        slices = [_shards_from_full(n, grads[n], BF16) for n in names]
        own = [_own_shard(n, grads[n], my_chip) for n in names]
        plan.put(GRAD_HOSTS[group], _Carry(slices, [SDS((3,) + a.shape[1:], BF16) for a in slices], _chip_sems(len(names)),
                                           _scatter_copies, lambda res: landed.update({group: (own, res)})))

    loss, gx, _, gs = _local_step(x[0], mem[0], loss_target[0], full, small, plan, on_grads)
    assert not plan.pending, list(plan.pending)

    def update(hv, ov, wv, mv, vv):
        g = hv + ov
        return (g,) + _adamw(g, wv, mv, vv)

    outs = [{}, {}, {}, {}]
    for group, names in GROUPS.items():
        own, got = landed[group]
        halves = [_tokmap(f"grads_sum_chips_{n}",
                          lambda a, b0, b1, b2: ((a + b0.astype(F32)) + b1.astype(F32)) + b2.astype(F32),
                          [o, g[0], g[1], g[2]], [], [(o.shape[1], F32)])[0] for n, o, g in zip(names, own, got)]
        others = _swap_with_sibling(f"grads_swap_cores_{group}", halves)
        for n, half, other in zip(names, halves, others):
            res = _tokmap(f"adamw_{n}", update, [half, other, wts[n], moms[n], vars_[n]], [], [(half.shape[1], F32)] * 4)
            for d, r in zip(outs, res):
                d[n] = r

    s_red = _all_reduce_small(_pack_small(gs, loss[0, 0]))
    res = _tokmap(
        "adamw_small", lambda g, wv, mv, vv: (g,) + _adamw(g, wv, mv, vv),
        [s_red, _pack_small(small), _pack_small({n: moms[n] for n, _ in SMALL}), _pack_small({n: vars_[n] for n, _ in SMALL})],
        [], [(D, F32)] * 4)
    for d, packed in zip(outs, res):
        d.update(_unpack_small(packed)[0])
    _, total_loss = _unpack_small(s_red)
    return (total_loss, gx[None], *[d[n][None] for d in outs for n in WEIGHTS])
```

```python
import functools

import numpy as np
import jax
import jax.numpy as jnp
from jax import lax
from jax.experimental import pallas as pl
from jax.experimental.pallas import tpu as pltpu

F32, BF16 = jnp.float32, jnp.bfloat16
SDS = jax.ShapeDtypeStruct
MESH = pl.DeviceIdType.MESH

D = 1024
HD = 64
QB = 128
DSA_T_FWD, DSA_T_BWD = 512, 256
D_FF = 2816
SB_W, DSA_W, DSA_OUT_W, MEM_W = 512, 768, 256, 256
DSA_DILS = (1, 4, 16)
MEM_LEN = 256
N_CHIPS = 4
EPS = 1e-6
SCALE = HD ** -0.5
EXHAUSTED = -104.0
SB_FWD_HEADS = 4
SB_QB = 256
SB_WIN = 512
NEG = -1e30
VMEM_LIMIT = 56 * 1024 * 1024

ADAM_LR, ADAM_B1, ADAM_B2, ADAM_EPS, ADAM_WD, ADAM_STEP = 0.001, 0.9, 0.999, 1e-08, 0.01, 10

NN = (((1,), (0,)), ((), ()))
NT = (((1,), (1,)), ((), ()))
TN = (((0,), (0,)), ((), ()))

SHARDED = (
    ("ffn1_w1", (D, D_FF), 1), ("ffn1_w3", (D, D_FF), 1), ("ffn1_w2", (D_FF, D), 0),
    ("w_in", (D, 4096), 1), ("w_mem_kv", (D, 512), 0),
    ("w_branch_sb", (SB_W, D), 1), ("w_branch_dsa", (DSA_OUT_W, D), 1), ("w_branch_mem", (MEM_W, D), 1),
    ("w_gate", (D, 3 * D), 1), ("w_out", (D, D), 0),
    ("ffn2_w1", (D, D_FF), 1), ("ffn2_w3", (D, D_FF), 1), ("ffn2_w2", (D_FF, D), 0),
)
SHARDED_BY_NAME = {n: (sh, ax) for n, sh, ax in SHARDED}
GROUPS = {
    "ffn2": ("ffn2_w1", "ffn2_w3", "ffn2_w2"),
    "mid": ("w_in", "w_mem_kv", "w_branch_sb", "w_branch_dsa", "w_branch_mem", "w_gate", "w_out"),
    "ffn1": ("ffn1_w1", "ffn1_w3", "ffn1_w2"),
}
WEIGHT_PIECES = (
    (None, ("ffn1_w1",)),
    ("ffn1_up_a", ("ffn1_w3", "ffn1_w2")),
    ("ffn1_up_b", ("w_in", "w_mem_kv")),
    ("ffn1_down", ("w_gate", "w_branch_sb", "w_branch_dsa", "w_branch_mem", "w_out")),
    ("proj_dsa", ("ffn2_w2",)),
    ("proj_gate", ("ffn2_w1", "ffn2_w3")),
)
GRAD_HOSTS = {"ffn2": "ffn2_bwd_dn", "mid": "ffn1_bwd_dw13", "ffn1": "ffn1_bwd_dn"}
SMALL = (("ffn1_norm", D), ("mix_norm", D), ("mem_norm", D), ("ffn2_norm", D), ("b_gate", 3 * D),
         ("qn_dsa", HD), ("kn_dsa", HD), ("qn_mem", HD), ("kn_mem", HD))
WEIGHTS = ("ffn1_norm", "ffn1_w1", "ffn1_w3", "ffn1_w2", "mix_norm", "mem_norm", "w_in", "w_mem_kv", "qn_dsa", "kn_dsa",
           "qn_mem", "kn_mem", "w_branch_sb", "w_branch_dsa", "w_branch_mem", "w_gate", "b_gate", "w_out", "ffn2_norm",
           "ffn2_w1", "ffn2_w3", "ffn2_w2")
SMALL_ROWS = 8


def _dot(a, b, dn=NN):
    return lax.dot_general(a, b, dn, preferred_element_type=F32)


def _dot01(x, m01, pieces=3):
    hi = x.astype(BF16)
    r1 = x - hi.astype(F32)
    mid = r1.astype(BF16)
    if pieces == 2:
        return _dot(hi, m01) + _dot(mid, m01)
    lo = (r1 - mid.astype(F32)).astype(BF16)
    return _dot(hi, m01) + _dot(mid, m01) + _dot(lo, m01)


def _pick(n, cands):
    for c in cands:
        if n % c == 0:
            return c
    raise ValueError(f"no tile for {n}")


def _from_dilated(v, d, scr):
    w = v.shape[1] // d
    v = v.astype(F32)
    for c in range(d):
        for p, buf in enumerate(scr[:w // 128]):
            buf[pl.ds(c, v.shape[0], stride=d), :] = v[:, c * w + 128 * p:c * w + 128 * (p + 1)]
    return jnp.concatenate([buf[...] for buf in scr[:w // 128]], axis=1)


def _to_dilated(v, d, scr):
    w = v.shape[1]
    for p, buf in enumerate(scr[:w // 128]):
        buf[...] = v[:, 128 * p:128 * (p + 1)].astype(F32)
    return jnp.concatenate([buf[pl.ds(c, v.shape[0] // d, stride=d), :] for c in range(d) for buf in scr[:w // 128]], axis=1)


def _tokmap(name, fn, tok_ins, consts, tok_outs, acc_outs=(), tile=512, dil_ins=None, dil_outs=None, place=None):
    dil_ins, dil_outs, place = dil_ins or {}, dil_outs or {}, place or {}
    bufs = [(j, buf) for j, (_, _, buf) in place.items() if buf is not None]
    n_buf = len(bufs)
    n = tok_ins[0].shape[0] * dil_ins.get(0, 1)
    tile = _pick(n, [t for t in (512, 352, 256, 128, 64, 32, 16, 8) if t <= tile])
    n_tin, n_in, n_tok, n_acc = len(tok_ins), len(tok_ins) + len(consts), len(tok_outs), len(acc_outs)
    n_scr = max([tok_ins[j].shape[1] // d // 128 for j, d in dil_ins.items() if d > 1]
                + [tok_outs[j][0] // 128 for j, d in dil_outs.items() if d > 1] + [0])

    def body(*refs):
        scr = refs[len(refs) - n_scr:]
        vals = [r[...] for r in refs[:n_in]]
        for j, d in dil_ins.items():
            if d > 1:
                vals[j] = _from_dilated(vals[j], d, scr)
        outs = fn(*vals)
        outs = list(outs) if isinstance(outs, (tuple, list)) else [outs]
        assert len(outs) == n_tok + n_acc, (name, len(outs))
        for j, d in dil_outs.items():
            if d > 1:
                outs[j] = _to_dilated(outs[j], d, scr)
        orefs = refs[n_in + n_buf:]
        for r, v in zip(orefs[:n_tok], outs[:n_tok]):
            r[...] = v.astype(r.dtype)
        if n_acc:
            @pl.when(pl.program_id(0) == 0)
            def _():
                for r in orefs[n_tok:n_tok + n_acc]:
                    r[...] = jnp.zeros(r.shape, r.dtype)
            for r, v in zip(orefs[n_tok:n_tok + n_acc], outs[n_tok:]):
                r[...] += v.astype(r.dtype)

    def tok_spec(width, d):
        return pl.BlockSpec((tile // d, d * width), lambda i: (i, 0))

    in_specs = [tok_spec(a.shape[1] // dil_ins.get(j, 1), dil_ins.get(j, 1)) for j, a in enumerate(tok_ins)]
    in_specs += [pl.BlockSpec(c.shape, lambda i: (0, 0)) for c in consts]
    in_specs += [HBM_SPEC] * n_buf
    out_specs = [tok_spec(w, dil_outs.get(j, 1)) for j, (w, _) in enumerate(tok_outs)]
    out_shape = [SDS((n // dil_outs.get(j, 1), w * dil_outs.get(j, 1)), dt) for j, (w, dt) in enumerate(tok_outs)]
    for j, (total, col_block, _) in place.items():
        out_specs[j] = pl.BlockSpec((tile, tok_outs[j][0]), lambda i, cb=col_block: (i, cb))
        out_shape[j] = SDS((n, total), tok_outs[j][1])
    out_specs += [pl.BlockSpec(s, lambda i: (0, 0)) for s in acc_outs]
    out_shape += [SDS(s, F32) for s in acc_outs]
    res = pl.pallas_call(
        body, name=name, grid=(n // tile,), in_specs=in_specs, out_specs=out_specs, out_shape=out_shape,
        scratch_shapes=[pltpu.VMEM((tile, 128), F32)] * n_scr,
        input_output_aliases={n_in + b: j for b, (j, _) in enumerate(bufs)},
        compiler_params=pltpu.CompilerParams(dimension_semantics=("arbitrary",), vmem_limit_bytes=VMEM_LIMIT),
    )(*tok_ins, *consts, *[buf for _, buf in bufs])
    return res


MATMUL_VMEM_BUDGET = 40 * 1024 * 1024


def _matmul_tiles(m, n, k, a_bytes, b_bytes, o_bytes, extra_bytes, whole_n=False):
    best = None
    for tk in [c for c in (3584, 2816, 2048, 1408, 1024, 512, 256, 128) if k % c == 0]:
        for tm in [c for c in (1408, 1024, 768, 512, 256, 128) if m % c == 0]:
            for tn in [n] if whole_n else [c for c in (1408, 1024, 768, 512, 256, 128) if n % c == 0]:
                need = 2 * tk * (tm * a_bytes + tn * b_bytes) + tm * tn * (2 * o_bytes + 2 * extra_bytes + 8)
                if need > MATMUL_VMEM_BUDGET:
                    continue
                score = (min(tm, 512) * min(tn, 512), tk, tm * tn, tn)
                if best is None or score > best[0]:
                    best = (score, (tm, tn, tk))
    return best[1]


class _Carry:
    def __init__(self, ins, outs, sems, copies, then):
        self.ins, self.outs, self.sems, self.copies, self.then = ins, outs, sems, copies, then


class _Plan:
    def __init__(self):
        self.pending = {}

    def put(self, host, carry):
        assert host not in self.pending, host
        self.pending[host] = carry

    def take(self, host):
        return self.pending.pop(host, None)


def _matmul(name, a, b, dn, out_dtype, epi=None, tiles=(), rows=(), plan=None, a_pro=None, n_sum=0):
    pro = a_pro
    a_list = list(a) if isinstance(a, (list, tuple)) else [a]
    a, n_parts = a_list[0], len(a_list)
    if dn == NN:
        (m, k), n = a.shape, b.shape[1]
    elif dn == NT:
        (m, k), n = a.shape, b.shape[0]
    else:
        (k, m), n = a.shape, b.shape[1]
    n_t, n_r = len(tiles), len(rows)
    if pro is not None:
        assert dn == NN and n == _pick(n, (1024, 512))
    if pro is not None:
        tm, tn, tk = _pick(m, (256, 128)), n, k
    else:
        tm, tn, tk = _matmul_tiles(m, n, k, a.dtype.itemsize, b.dtype.itemsize, jnp.dtype(out_dtype).itemsize,
                                   sum(t.dtype.itemsize for t in tiles), whole_n=n_sum > 0)
    nk = k // tk
    grid = (m // tm, n // tn, nk)
    assert pro is None or grid[1] == 1
    assert n_sum == 0 or grid[1] == 1
    carry = plan.take(name) if plan is not None else None
    n_ci, n_co = (len(carry.ins), len(carry.outs)) if carry else (0, 0)
    n_keep = 1 if pro is not None else 0

    def body(*refs):
        a_refs, b_ref, rest = refs[:n_parts], refs[n_parts], refs[n_parts + 1:]
        extras, rest = rest[:n_t + n_r], rest[n_t + n_r:]
        c_in, o_ref, rest = rest[:n_ci], rest[n_ci], rest[n_ci + 1:]
        sums, rest = rest[:n_sum], rest[n_sum:]
        keep, c_out, scratch = rest[:n_keep], rest[n_keep:n_keep + n_co], rest[n_keep + n_co:]
        ids = [pl.program_id(d) for d in range(3)]
        if n_sum:
            @pl.when((ids[0] == 0) & (ids[2] == 0))
            def _():
                for r in sums:
                    r[...] = jnp.zeros(r.shape, F32)
        if carry:
            sems = scratch[1:] if nk > 1 else scratch

            @pl.when((ids[0] == 0) & (ids[1] == 0) & (ids[2] == 0))
            def _():
                for cp in carry.copies(c_in, c_out, *sems):
                    cp.start()

        if pro is not None:
            av = pro(*[r[...] for r in a_refs])
            keep[0][...] = av
        else:
            av = a_refs[0][...].astype(BF16)
        part = _dot(av, b_ref[...].astype(BF16), dn)

        def finish(r):
            if epi is not None:
                r = epi(r, *[e[...] for e in extras])
            if n_sum:
                for ref, v in zip(sums, r[1:]):
                    ref[...] += v
                r = r[0]
            o_ref[...] = r.astype(o_ref.dtype)

        if nk == 1:
            finish(part)
        else:
            acc = scratch[0]

            @pl.when(ids[2] == 0)
            def _():
                acc[...] = part

            @pl.when(ids[2] > 0)
            def _():
                acc[...] += part

            @pl.when(ids[2] == nk - 1)
            def _():
                finish(acc[...])

        if carry:
            @pl.when((ids[0] == grid[0] - 1) & (ids[1] == grid[1] - 1) & (ids[2] == nk - 1))
            def _():
                for cp in carry.copies(c_in, c_out, *sems):
                    cp.wait()

    if dn == TN:
        a_specs = [pl.BlockSpec((tk, tm), lambda i, j, kk: (kk, i))]
    else:
        a_specs = [pl.BlockSpec((tm, tk), lambda i, j, kk: (i, kk))] * n_parts
    b_spec = pl.BlockSpec((tn, tk), lambda i, j, kk: (j, kk)) if dn == NT else pl.BlockSpec((tk, tn), lambda i, j, kk: (kk, j))
    in_specs = a_specs + [b_spec] + [pl.BlockSpec((tm, tn), lambda i, j, kk: (i, j)) for _ in tiles]
    in_specs += [pl.BlockSpec((1, tn), lambda i, j, kk: (0, j)) for _ in rows] + [HBM_SPEC] * n_ci
    res = pl.pallas_call(
        body, name=name, grid=grid, in_specs=in_specs,
        out_specs=[pl.BlockSpec((tm, tn), lambda i, j, kk: (i, j))] + [pl.BlockSpec((1, tn), lambda i, j, kk: (0, 0))] * n_sum
        + [pl.BlockSpec((tm, tk), lambda i, j, kk: (i, kk))] * n_keep + [HBM_SPEC] * n_co,
        out_shape=[SDS((m, n), out_dtype)] + [SDS((1, n), F32)] * n_sum + [SDS((m, k), BF16)] * n_keep
        + (list(carry.outs) if carry else []),
        scratch_shapes=([pltpu.VMEM((tm, tn), F32)] if nk > 1 else []) + (list(carry.sems) if carry else []),
        compiler_params=pltpu.CompilerParams(
            dimension_semantics=("arbitrary",) * 3 if (carry or n_sum) else ("parallel", "parallel", "arbitrary"),
            vmem_limit_bytes=VMEM_LIMIT),
    )(*a_list, b, *tiles, *rows, *(carry.ins if carry else []))
    if carry:
        carry.then(res[1 + n_sum + n_keep:])
    return tuple(res[:1 + n_sum + n_keep]) if n_sum + n_keep else res[0]


def _mean_all(v):
    return jnp.mean(v, axis=-1, keepdims=True)


def _head_sums(v, bd):
    w = bd.shape[0]
    return jnp.concatenate([_dot01(v[:, j:j + w], bd, 2) for j in range(0, v.shape[1], w)], axis=1)


def _mean_heads(bd):
    return lambda v: _head_sums(v, bd) * (1.0 / HD)


def _rms_fwd(x, g, mean):
    return x * lax.rsqrt(mean(x * x) + EPS) * g


def _rms_bwd(x, g, dy, mean):
    r = lax.rsqrt(mean(x * x) + EPS)
    dn = dy * g
    dx = r * dn - x * (r * r * r) * mean(dn * x)
    return dx, jnp.sum(dy * x * r, axis=0, keepdims=True)


def _swap_halves(x):
    w = x.shape[1]
    lane = lax.broadcasted_iota(jnp.int32, x.shape, 1)
    return jnp.where(lane % HD < HD // 2, pltpu.roll(x, w - HD // 2, 1), pltpu.roll(x, HD // 2, 1))


def _lanes(t, w):
    return jnp.tile(t, (1, w // t.shape[1]))


def _rope_fwd(x, cos, sin_signed):
    return x * _lanes(cos, x.shape[1]) + _swap_halves(x) * _lanes(sin_signed, x.shape[1])


def _rope_bwd(dy, cos, sin_signed):
    return dy * _lanes(cos, dy.shape[1]) + _swap_halves(dy * _lanes(sin_signed, dy.shape[1]))


def _bcast_heads(cols):
    return jnp.concatenate([jnp.broadcast_to(c, (c.shape[0], HD)) for c in cols], axis=1)


def _softplus(z):
    return jnp.maximum(z, 0.0) + jnp.log(1.0 + jnp.exp(-jnp.abs(z)))


def _block_diag(w):
    h = np.arange(w) // HD
    return jnp.asarray(h[:, None] == h[None, :], BF16)


def _sb_window(i, t):
    hi = (i + 1) * SB_QB - t * SB_WIN
    lo = hi - SB_WIN
    ws = pl.multiple_of(jnp.maximum(lo, 0), SB_QB)
    kpos = ws + lax.broadcasted_iota(jnp.int32, (SB_QB, SB_WIN), 1)
    qpos = i * SB_QB + lax.broadcasted_iota(jnp.int32, (SB_QB, SB_WIN), 0)
    return (kpos < qpos) & (kpos >= lo) & (kpos < hi), ws


def _sb_fwd(qkv):
    s = qkv.shape[0]
    assert s >= SB_WIN
    nq = s // SB_QB
    nh = SB_FWD_HEADS
    bw = HD * nh
    ngroups = SB_W // bw

    def body(q_ref, k_ref, v_ref, later_ref, o_ref, tot_ref, w0_ref, b0_ref, nb_ref):
        p, i = pl.program_id(0), pl.program_id(1)
        q = q_ref[...] * SCALE
        later_of = later_ref[...]

        def window(t, tots, outs, keep):
            mask, ws = _sb_window(i, t)
            kw, vw = k_ref[pl.ds(ws, SB_WIN), :], v_ref[pl.ds(ws, SB_WIN), :]
            new_t, new_o, w_all, b_all = [], [], [], []
            for hh in range(nh):
                sl = slice(HD * hh, HD * hh + HD)
                z = _dot(q[:, sl], kw[:, sl], NT)
                sp = _softplus(z)
                lf = jnp.where(mask, -sp, 0.0)
                lf_far, lf_near = lf[:, :SB_WIN // 2], lf[:, SB_WIN // 2:]
                later = tots[hh] + jnp.concatenate(
                    [_dot01(lf_far, later_of, 2) + jnp.sum(lf_near, axis=1, keepdims=True), _dot01(lf_near, later_of, 2)], axis=1)
                w = jnp.where(mask, jnp.exp(z - sp + later), 0.0).astype(BF16)
                new_o.append(outs[hh] + _dot(w, vw[:, sl]))
                new_t.append(tots[hh] + jnp.sum(lf, axis=1, keepdims=True))
                if keep:
                    w_all.append(w)
                    b_all.append(jnp.where(mask, jnp.exp(z - sp), 0.0).astype(BF16))
            if keep:
                w0_ref[...] = jnp.concatenate(w_all, axis=1)
                b0_ref[...] = jnp.concatenate(b_all, axis=1)
            alive = functools.reduce(jnp.maximum, [jnp.max(v) for v in new_t])
            return t + 1, alive, tuple(new_t), tuple(new_o)

        zt, zo = jnp.zeros((SB_QB, 1), F32), jnp.zeros((SB_QB, HD), F32)
        first = window(jnp.int32(0), (zt,) * nh, (zo,) * nh, True)
        t, _, tots, outs = lax.while_loop(lambda c: ((i + 1) * SB_QB - c[0] * SB_WIN > 0) & (c[1] > EXHAUSTED),
                                          lambda c: window(c[0], c[2], c[3], False), first)
        o_ref[...] = jnp.concatenate(outs, axis=1).astype(o_ref.dtype)
        tot_ref[...] = _bcast_heads(tots)
        nb_ref[p, i] = t

    whole = lambda off: pl.BlockSpec((s, bw), lambda p, i: (0, off + p), pipeline_mode=pl.Buffered(1))
    tile = pl.BlockSpec((SB_QB, bw), lambda p, i: (i, p))
    tri = pl.BlockSpec((SB_WIN // 2, SB_WIN // 2), lambda p, i: (0, 0), pipeline_mode=pl.Buffered(1))
    near = pl.BlockSpec((SB_QB, nh * SB_WIN), lambda p, i: (i, p))
    n_heads = SB_W // HD
    idx = np.arange(SB_WIN // 2)
    return pl.pallas_call(
        body, name="sb_fwd", grid=(ngroups, nq),
        in_specs=[tile, whole(ngroups), whole(2 * ngroups), tri],
        out_specs=[tile, tile, near, near, pl.BlockSpec(memory_space=pltpu.SMEM)],
        out_shape=[SDS((s, SB_W), BF16), SDS((s, SB_W), F32), SDS((s, n_heads * SB_WIN), BF16), SDS((s, n_heads * SB_WIN), BF16),
                   SDS((ngroups, nq), jnp.int32)],
        compiler_params=pltpu.CompilerParams(dimension_semantics=("arbitrary", "arbitrary"), vmem_limit_bytes=VMEM_LIMIT),
    )(qkv, qkv, qkv, jnp.asarray(idx[:, None] > idx[None, :], BF16))


def _sb_bwd(qkv, do, tot, nblk, w0, b0, buf, col):
    s = qkv.shape[0]
    nq = s // SB_QB
    npairs = SB_W // 128

    def body(nb_ref, q_ref, k_ref, v_ref, do_ref, tot_ref, upto_ref, before_ref, w0_ref, b0_ref, buf_ref,
             dq_ref, dk_ref, dv_ref):
        p, i = pl.program_id(0), pl.program_id(1)

        @pl.when(i == 0)
        def _():
            dk_ref[...] = jnp.zeros(dk_ref.shape, F32)
            dv_ref[...] = jnp.zeros(dv_ref.shape, F32)

        upto = upto_ref[...]
        before = before_ref[...]
        q, dout, tt = q_ref[...] * SCALE, do_ref[...], tot_ref[...]
        n = nb_ref[p * 2 // SB_FWD_HEADS, i]

        def step(it, c):
            pres, gpres, dqs = c
            mask, ws = _sb_window(i, n - 1 - it)
            kw, vw = k_ref[pl.ds(ws, SB_WIN), :], v_ref[pl.ds(ws, SB_WIN), :]
            new_p, new_g, new_dq, dks, dvs = [], [], [], [], []
            for hh in range(2):
                sl = slice(HD * hh, HD * hh + HD)
                z = _dot(q[:, sl], kw[:, sl], NT)
                sp = _softplus(z)
                lf = jnp.where(mask, -sp, 0.0)
                later = tt[:, HD * hh:HD * hh + 1] - (pres[hh] + _dot01(lf, upto, 2))
                w = jnp.where(mask, jnp.exp(z - sp + later), 0.0)
                beta = jnp.exp(z - sp)
                g = _dot(dout[:, sl], vw[:, sl], NT) * w
                g_far = gpres[hh] + _dot(g.astype(BF16), before)
                dz = jnp.where(mask, g * (1.0 - beta) - beta * g_far, 0.0).astype(BF16)
                new_dq.append(dqs[hh] + _dot(dz, kw[:, sl]))
                dks.append(_dot(dz, q[:, sl], TN))
                dvs.append(_dot(w.astype(BF16), dout[:, sl], TN))
                new_p.append(pres[hh] + jnp.sum(lf, axis=1, keepdims=True))
                new_g.append(gpres[hh] + jnp.sum(g, axis=1, keepdims=True))
            dk_ref[pl.ds(ws, SB_WIN), :] += jnp.concatenate(dks, axis=1)
            dv_ref[pl.ds(ws, SB_WIN), :] += jnp.concatenate(dvs, axis=1)
            return tuple(new_p), tuple(new_g), tuple(new_dq)

        zt, zo = jnp.zeros((SB_QB, 1), F32), jnp.zeros((SB_QB, HD), F32)
        _, gpres, dqs = lax.fori_loop(0, n - 1, step, ((zt, zt), (zt, zt), (zo, zo)))
        _, ws = _sb_window(i, 0)
        kw, vw = k_ref[pl.ds(ws, SB_WIN), :], v_ref[pl.ds(ws, SB_WIN), :]
        dqs, dks, dvs = list(dqs), [], []
        for hh in range(2):
            sl = slice(HD * hh, HD * hh + HD)
            w = w0_ref[:, SB_WIN * hh:SB_WIN * (hh + 1)]
            beta = b0_ref[:, SB_WIN * hh:SB_WIN * (hh + 1)].astype(F32)
            g = _dot(dout[:, sl], vw[:, sl], NT) * w.astype(F32)
            g16, half = g.astype(BF16), SB_WIN // 2
            g_far = gpres[hh] + jnp.concatenate(
                [_dot(g16[:, :half], before[:half, :half]),
                 _dot(g16[:, half:], before[:half, :half]) + jnp.sum(g[:, :half], axis=1, keepdims=True)], axis=1)
            dz = (g * (1.0 - beta) - beta * g_far).astype(BF16)
            dqs[hh] = dqs[hh] + _dot(dz, kw[:, sl])
            dks.append(_dot(dz, q[:, sl], TN))
            dvs.append(_dot(w, dout[:, sl], TN))
        dk_ref[pl.ds(ws, SB_WIN), :] += jnp.concatenate(dks, axis=1)
        dv_ref[pl.ds(ws, SB_WIN), :] += jnp.concatenate(dvs, axis=1)
        dq_ref[...] = (jnp.concatenate(dqs, axis=1) * SCALE).astype(dq_ref.dtype)

    whole_in = lambda off: pl.BlockSpec((s, 128), lambda p, i: (0, off + p), pipeline_mode=pl.Buffered(1))
    whole_out = pl.BlockSpec((s, 128), lambda p, i: (0, p), pipeline_mode=pl.Buffered(1))
    tile = pl.BlockSpec((SB_QB, 128), lambda p, i: (i, p))
    near = pl.BlockSpec((SB_QB, 2 * SB_WIN), lambda p, i: (i, p))
    dq_tile = pl.BlockSpec((SB_QB, 128), lambda p, i: (i, col // 128 + p))
    tri = pl.BlockSpec((SB_WIN, SB_WIN), lambda p, i: (0, 0), pipeline_mode=pl.Buffered(1))
    idx = np.arange(SB_WIN)
    return pl.pallas_call(
        body, name="sb_bwd", grid=(npairs, nq),
        in_specs=[pl.BlockSpec(memory_space=pltpu.SMEM), tile, whole_in(npairs), whole_in(2 * npairs), tile, tile, tri, tri,
                  near, near, HBM_SPEC],
        out_specs=[dq_tile, whole_out, whole_out],
        out_shape=[SDS(buf.shape, buf.dtype)] + [SDS((s, SB_W), F32)] * 2,
        input_output_aliases={10: 0},
        compiler_params=pltpu.CompilerParams(dimension_semantics=("arbitrary", "arbitrary"), vmem_limit_bytes=VMEM_LIMIT),
    )(nblk, qkv, qkv, qkv, do, tot, jnp.asarray(idx[:, None] <= idx[None, :], BF16), jnp.asarray(idx[:, None] < idx[None, :], BF16),
      w0, b0, buf)


def _dsa_mask(DSA_T, has_prev):
    r = lax.broadcasted_iota(jnp.int32, (DSA_T, QB + DSA_T), 0)
    j = lax.broadcasted_iota(jnp.int32, (DSA_T, QB + DSA_T), 1) - QB
    return (j <= r) & (j >= r - QB) & ((j >= 0) | has_prev)


def _dsa_fwd(q, k, v, dil):
    n = q.shape[0]
    DSA_T = DSA_T_FWD
    nt = n // DSA_T

    def body(q_ref, kc_ref, kp_ref, vc_ref, vp_ref, o_ref, lse_ref):
        mask = _dsa_mask(DSA_T, pl.program_id(1) > 0)
        outs, lses = [], []
        for hh in range(DSA_OUT_W // HD):
            sl = slice(HD * hh, HD * hh + HD)
            kcat = jnp.concatenate([kp_ref[:, sl], kc_ref[:, sl]], axis=0)
            vcat = jnp.concatenate([vp_ref[:, sl], vc_ref[:, sl]], axis=0)
            sc = jnp.where(mask, _dot(q_ref[:, sl] * SCALE, kcat, NT), NEG)
            m = jnp.max(sc, axis=1, keepdims=True)
            p = jnp.exp(sc - m)
            den = jnp.sum(p, axis=1, keepdims=True)
            outs.append(_dot(p.astype(BF16), vcat) / den)
            lses.append(m + jnp.log(den))
        o_ref[...] = jnp.concatenate(outs, axis=1)
        lse_ref[...] = _bcast_heads(lses)

    cur = pl.BlockSpec((DSA_T, DSA_OUT_W), lambda c, i: (i, c))
    prev = pl.BlockSpec((QB, DSA_OUT_W), lambda c, i: (jnp.maximum(i * (DSA_T // QB) - 1, 0), c))
    o, lse = pl.pallas_call(
        body, name=f"dsa_fwd_d{dil}", grid=(dil, nt), in_specs=[cur, cur, prev, cur, prev], out_specs=[cur, cur],
        out_shape=[SDS((n, dil * DSA_OUT_W), F32)] * 2,
        compiler_params=pltpu.CompilerParams(dimension_semantics=("parallel", "parallel")),
    )(q, k, k, v, v)
    return o, lse


def _dsa_bwd(q, k, v, do, cc, lse, dil):
    n = q.shape[0]
    DSA_T = DSA_T_BWD
    nt = n // DSA_T
    per = DSA_T // QB

    def body(qj_ref, qn_ref, kp_ref, kj_ref, vp_ref, vj_ref, doj_ref, don_ref, cj_ref, cn_ref, lj_ref, ln_ref,
             dq_ref, dk_ref, dv_ref):
        j = pl.program_id(1)
        mask = _dsa_mask(DSA_T, j > 0)
        r = lax.broadcasted_iota(jnp.int32, (QB, DSA_T), 0)
        kk = lax.broadcasted_iota(jnp.int32, (QB, DSA_T), 1)
        m_next = (kk >= r + QB) & (j + 1 < nt)
        dqs, dks, dvs = [], [], []
        for hh in range(DSA_OUT_W // HD):
            sl = slice(HD * hh, HD * hh + HD)
            one = slice(HD * hh, HD * hh + 1)
            kj, vj, doj, don = (t[:, sl] for t in (kj_ref, vj_ref, doj_ref, don_ref))
            qj, qn = qj_ref[:, sl] * SCALE, qn_ref[:, sl] * SCALE
            kcat = jnp.concatenate([kp_ref[:, sl], kj], axis=0)
            vcat = jnp.concatenate([vp_ref[:, sl], vj], axis=0)
            p1 = jnp.where(mask, jnp.exp(_dot(qj, kcat, NT) - lj_ref[:, one]), 0.0)
            ds1 = (p1 * (_dot(doj, vcat, NT) + cj_ref[:, one])).astype(BF16)
            p2 = jnp.where(m_next, jnp.exp(_dot(qn, kj, NT) - ln_ref[:, one]), 0.0)
            ds2 = (p2 * (_dot(don, vj, NT) + cn_ref[:, one])).astype(BF16)
            dqs.append(_dot(ds1, kcat) * SCALE)
            dks.append(_dot(ds1[:, QB:], qj, TN) + _dot(ds2, qn, TN))
            dvs.append(_dot(p1[:, QB:].astype(BF16), doj, TN) + _dot(p2.astype(BF16), don, TN))
        dq_ref[...] = jnp.concatenate(dqs, axis=1).astype(dq_ref.dtype)
        dk_ref[...] = jnp.concatenate(dks, axis=1).astype(dk_ref.dtype)
        dv_ref[...] = jnp.concatenate(dvs, axis=1).astype(dv_ref.dtype)

    cur = pl.BlockSpec((DSA_T, DSA_OUT_W), lambda c, j: (j, c))
    prev = pl.BlockSpec((QB, DSA_OUT_W), lambda c, j: (jnp.maximum(j * per - 1, 0), c))
    nxt = pl.BlockSpec((QB, DSA_OUT_W), lambda c, j: (jnp.minimum((j + 1) * per, n // QB - 1), c))
    dq, dk, dv = pl.pallas_call(
        body, name=f"dsa_bwd_d{dil}", grid=(dil, nt),
        in_specs=[cur, nxt, prev, cur, prev, cur, cur, nxt, cur, nxt, cur, nxt], out_specs=[cur, cur, cur],
        out_shape=[SDS((n, dil * DSA_OUT_W), BF16)] * 3,
        compiler_params=pltpu.CompilerParams(dimension_semantics=("parallel", "parallel")),
    )(q, q, k, k, v, v, do, do, cc, cc, lse, lse)
    return dq, dk, dv


def _norm_bwd_epi(acc, xv, dyv, g):
    dx, dg = _rms_bwd(xv, g, acc, _mean_all)
    return dx + dyv, dg


def _ffn_fwd(tag, x, gain, w1, w3, w2, plan=None, target=None):
    n = _tokmap(f"{tag}_norm", lambda xv, g: _rms_fwd(xv, g, _mean_all), [x], [gain], [(D, BF16)])[0]
    a_arr = _matmul(f"{tag}_up_a", n, w1(), NN, BF16, plan=plan)
    b_arr = _matmul(f"{tag}_up_b", n, w3(), NN, BF16, plan=plan)

    def gate(av, bv):
        a, b = av.astype(F32), bv.astype(F32)
        return (a * jax.nn.sigmoid(a) * b).astype(BF16)

    if target is None:
        y, h = _matmul(f"{tag}_down", [a_arr, b_arr], w2(), NN, F32, epi=lambda acc, res: res + 0.5 * acc, tiles=[x], plan=plan,
                       a_pro=gate)
        return y, (n, a_arr, b_arr, h)

    def loss_epi(acc, res, tv):
        e = res + 0.5 * acc - tv
        return e * (1.0 / D), (0.5 / D) * jnp.sum(e * e, axis=0, keepdims=True)

    dy, loss_row, h = _matmul(f"{tag}_down", [a_arr, b_arr], w2(), NN, F32, epi=loss_epi, tiles=[x, target], plan=plan,
                              a_pro=gate, n_sum=1)
    return (dy, loss_row), (n, a_arr, b_arr, h)


def _ffn_bwd(tag, x, gain, w13, w2, saved, dy, plan=None, on_dw=None, final=False):
    n, a_arr, b_arr, h = saved
    dh = _matmul(f"{tag}_bwd_dh", dy, w2, NT, BF16, epi=lambda acc: 0.5 * acc)

    def gate_bwd(av, bv, dhv):
        a, b, dhf = av.astype(F32), bv.astype(F32), dhv.astype(F32)
        sg = jax.nn.sigmoid(a)
        da = dhf * b * (sg * (1.0 + a * (1.0 - sg)))
        return jnp.concatenate([da, dhf * (a * sg)], axis=1)

    dab = _tokmap(f"{tag}_bwd_gate", gate_bwd, [a_arr, b_arr, dh], [], [(2 * D_FF, BF16)], tile=256)[0]
    dw2 = _matmul(f"{tag}_bwd_dw2", h, dy, TN, F32, epi=lambda acc: 0.5 * acc)
    dw13 = _matmul(f"{tag}_bwd_dw13", n, dab, TN, F32, plan=plan)
    if on_dw is not None:
        on_dw(dw13, dw2)
    if not final:
        dx, dgain = _matmul(f"{tag}_bwd_dn", dab, w13, NT, F32, epi=_norm_bwd_epi, tiles=[x, dy], rows=[gain], plan=plan,
                            n_sum=1)
    else:
        dn = _matmul(f"{tag}_bwd_dn", dab, w13, NT, F32, plan=plan)
        dx, dgain = _tokmap(f"{tag}_bwd_norm", lambda xv, dnv, dyv, g: _norm_bwd_epi(dnv, xv, dyv, g), [x, dn, dy], [gain],
                            [(D, F32)], [(1, D)])
    return dx, dgain, dw13, dw2


def _rope_tables(s):
    half = HD // 2
    inv_freq = jnp.power(10000.0, -jnp.arange(half, dtype=F32) / half)
    ang = jnp.arange(s).astype(F32)[:, None] * inv_freq[None, :]
    cos, sin = jnp.cos(ang), jnp.sin(ang)
    return jnp.tile(jnp.concatenate([cos, cos], axis=1), (1, 2)), jnp.tile(jnp.concatenate([-sin, sin], axis=1), (1, 2))


def _local_step(x, mem, tgt, w, sm, plan=None, on_grads=None):
    s = x.shape[0]
    assert s % (max(DSA_T_FWD, DSA_T_BWD) * max(DSA_DILS)) == 0
    on_grads = on_grads or (lambda group, grads: None)
    c_sb, c_dsa, c_qm, c_all = 3 * D, 3 * D + 3 * SB_W, 3 * D + 3 * SB_W + 3 * DSA_W, 3 * D + 4096
    cos, sin = _rope_tables(s)
    bd768 = bd256 = _block_diag(128)
    gq_dsa, gk_dsa = jnp.tile(sm["qn_dsa"], (1, DSA_W // HD)), jnp.tile(sm["kn_dsa"], (1, DSA_W // HD))
    gq_mem, gk_mem = jnp.tile(sm["qn_mem"], (1, MEM_W // HD)), jnp.tile(sm["kn_mem"], (1, MEM_W // HD))

    x1, ffn1_saved = _ffn_fwd("ffn1", x, sm["ffn1_norm"], lambda: w["ffn1_w1"], lambda: w["ffn1_w3"], lambda: w["ffn1_w2"], plan)
    w13_1 = jnp.concatenate([w["ffn1_w1"], w["ffn1_w3"]], axis=1)
    w_all = jnp.concatenate([w["w_gate"], w["w_in"]], axis=1)
    wb_sb, wb_dsa, wb_mem = w["w_branch_sb"], w["w_branch_dsa"], w["w_branch_mem"]
    hmix = _tokmap("mix_norm", lambda xv, g: _rms_fwd(xv, g, _mean_all), [x1], [sm["mix_norm"]], [(D, BF16)])[0]
    qkv_sb = _matmul("proj_sb", hmix, w_all[:, c_sb:c_dsa], NN, BF16)
    qkv_dsa = _matmul("proj_dsa", hmix, w_all[:, c_dsa:c_qm], NN, BF16, plan=plan)
    q_mem = _matmul("proj_qmem", hmix, w_all[:, c_qm:], NN, BF16)
    gpre = _matmul("proj_gate", hmix, w_all[:, :c_sb], NN, BF16, epi=lambda acc, b: acc + b, rows=[sm["b_gate"]], plan=plan)

    o_sb, sb_tot, sb_w0, sb_b0, sb_nblk = _sb_fwd(qkv_sb)

    def dsa_prep(qkv, cs, sn, gq, gk, bd):
        mean = _mean_heads(bd)
        qn = _rope_fwd(_rms_fwd(qkv[:, :DSA_W].astype(F32), gq, mean), cs, sn)
        kn = _rope_fwd(_rms_fwd(qkv[:, DSA_W:2 * DSA_W].astype(F32), gk, mean), cs, sn)
        v = qkv[:, 2 * DSA_W:]
        outs = []
        for t in (qn, kn, v):
            outs += [t[:, DSA_OUT_W * g:DSA_OUT_W * (g + 1)] for g in range(3)]
        return outs

    dsa_in = _tokmap("dsa_prep", dsa_prep, [qkv_dsa, cos, sin], [gq_dsa, gk_dsa, bd768], [(DSA_OUT_W, BF16)] * 9, tile=512,
                     dil_outs={j: DSA_DILS[j % 3] for j in range(9)})
    dsa_q, dsa_k, dsa_v = dsa_in[0:3], dsa_in[3:6], dsa_in[6:9]
    dsa_o, dsa_lse = zip(*[_dsa_fwd(dsa_q[g], dsa_k[g], dsa_v[g], DSA_DILS[g]) for g in range(3)])

    def alphas(l0, l1, l2):
        m = jnp.maximum(jnp.maximum(l0, l1), l2)
        e = [jnp.exp(l - m) for l in (l0, l1, l2)]
        tot = e[0] + e[1] + e[2]
        return [t / tot for t in e]

    def dsa_mix(o0, o1, o2, l0, l1, l2):
        a = alphas(l0, l1, l2)
        return a[0] * o0 + a[1] * o1 + a[2] * o2

    o_dsa = _tokmap("dsa_mix", dsa_mix, [*dsa_o, *dsa_lse], [], [(DSA_OUT_W, BF16)], tile=512,
                    dil_ins={j: DSA_DILS[j % 3] for j in range(6)})[0]

    def mem_kv(memv, g, wkv, gk, bd):
        kv = _dot(_rms_fwd(memv, g, _mean_all).astype(BF16), wkv)
        return _rms_fwd(kv[:, :MEM_W], gk, _mean_heads(bd)), kv[:, MEM_W:]

    km, vm = _tokmap("mem_kv", mem_kv, [mem], [sm["mem_norm"], w["w_mem_kv"], gk_mem, bd256], [(MEM_W, BF16)] * 2)

    def mem_probs(qv, kmv, gq, bd):
        qn = _rms_fwd(qv.astype(F32), gq, _mean_heads(bd)).astype(BF16)
        ps = []
        for h in range(MEM_W // HD):
            sl = slice(HD * h, HD * h + HD)
            sc = _dot(qn[:, sl], kmv[:, sl], NT) * SCALE
            e = jnp.exp(sc - jnp.max(sc, axis=1, keepdims=True))
            ps.append(e / jnp.sum(e, axis=1, keepdims=True))
        return qn, ps

    def mem_attn(qv, kmv, vmv, gq, bd):
        _, ps = mem_probs(qv, kmv, gq, bd)
        return jnp.concatenate([_dot(p.astype(BF16), vmv[:, HD * h:HD * h + HD]) for h, p in enumerate(ps)], axis=1)

    o_mem = _tokmap("mem_attn", mem_attn, [q_mem], [km, vm, gq_mem, bd256], [(MEM_W, BF16)])[0]

    def merge(osb, odsa, omem, gp, w_sb, w_dsa, w_mem):
        gates = jax.nn.sigmoid(gp.astype(F32))
        ys = (_dot(osb, w_sb), _dot(odsa, w_dsa), _dot(omem, w_mem))
        return gates, ys, gates[:, :D] * ys[0] + gates[:, D:2 * D] * ys[1] + gates[:, 2 * D:] * ys[2]

    merged = _tokmap("merge", lambda *a: merge(*a)[2], [o_sb, o_dsa, o_mem, gpre], [wb_sb, wb_dsa, wb_mem], [(D, BF16)],
                     tile=512)[0]
    x2 = _matmul("out_proj", merged, w["w_out"], NN, F32, epi=lambda acc, res: res + acc, tiles=[x1])
    w13_2 = jnp.concatenate([w["ffn2_w1"], w["ffn2_w3"]], axis=1)
    (dy, loss_row), ffn2_saved = _ffn_fwd("ffn2", x2, sm["ffn2_norm"], lambda: w["ffn2_w1"], lambda: w["ffn2_w3"],
                                          lambda: w["ffn2_w2"], target=tgt)
    loss = jnp.sum(loss_row).reshape(1, 1)

    gw, gs = {}, {}
    def ffn_grads(tag):
        def on_dw(dw13, dw2):
            gw[f"{tag}_w1"], gw[f"{tag}_w3"], gw[f"{tag}_w2"] = dw13[:, :D_FF], dw13[:, D_FF:], dw2
            on_grads(tag, {n: gw[n] for n in (f"{tag}_w1", f"{tag}_w3", f"{tag}_w2")})
        return on_dw

    dx2, gs["ffn2_norm"], _, _ = _ffn_bwd("ffn2", x2, sm["ffn2_norm"], w13_2, w["ffn2_w2"], ffn2_saved, dy, plan,
                                          ffn_grads("ffn2"))
    dmerged = _matmul("out_proj_bwd_dx", dx2, w["w_out"], NT, BF16)
    gw["w_out"] = _matmul("out_proj_bwd_dw", merged, dx2, TN, F32)

    def merge_bwd(osb, odsa, omem, gp, dm, w_sb, w_dsa, w_mem):
        gates, ys, _ = merge(osb, odsa, omem, gp, w_sb, w_dsa, w_mem)
        dmf = dm.astype(F32)
        dgp, dos, dws = [], [], []
        for b, (ov, wv) in enumerate(((osb, w_sb), (odsa, w_dsa), (omem, w_mem))):
            gb = gates[:, D * b:D * (b + 1)]
            dgp.append(dmf * ys[b] * gb * (1.0 - gb))
            dyb = (dmf * gb).astype(BF16)
            dos.append(_dot(dyb, wv, NT))
            dws.append(_dot(ov, dyb, TN))
        dgp = jnp.concatenate(dgp, axis=1)
        return dos[0], dos[1], dos[2], dgp, dws[0], dws[1], dws[2], jnp.sum(dgp, axis=0, keepdims=True)

    do_sb, do_dsa, do_mem, dgpre, gw["w_branch_sb"], gw["w_branch_dsa"], gw["w_branch_mem"], gs["b_gate"] = _tokmap(
        "merge_bwd", merge_bwd, [o_sb, o_dsa, o_mem, gpre, dmerged], [wb_sb, wb_dsa, wb_mem],
        [(SB_W, BF16), (DSA_OUT_W, F32), (MEM_W, BF16), (3 * D, BF16)],
        [(SB_W, D), (DSA_OUT_W, D), (MEM_W, D), (1, 3 * D)], tile=512, place={3: (c_all, 0, None)})

    dall, dk_sb, dv_sb = _sb_bwd(qkv_sb, do_sb, sb_tot, sb_nblk, sb_w0, sb_b0, dgpre, c_sb)
    dall = lax.dynamic_update_slice(dall, dk_sb.astype(BF16), (0, c_sb + SB_W))
    dall = lax.dynamic_update_slice(dall, dv_sb.astype(BF16), (0, c_sb + 2 * SB_W))

    def dsa_mix_bwd(o0, o1, o2, l0, l1, l2, dov, bd):
        a = alphas(l0, l1, l2)
        omix = a[0] * o0 + a[1] * o1 + a[2] * o2
        dot_o = _head_sums(dov * omix, bd)
        return [dov * t for t in a] + [-t * dot_o for t in a]

    mixb = _tokmap("dsa_mix_bwd", dsa_mix_bwd, [*dsa_o, *dsa_lse, do_dsa], [bd256],
                   [(DSA_OUT_W, BF16)] * 3 + [(DSA_OUT_W, F32)] * 3, tile=512,
                   dil_ins={j: DSA_DILS[j % 3] for j in range(6)}, dil_outs={j: DSA_DILS[j % 3] for j in range(6)})
    dsa_d = [_dsa_bwd(dsa_q[g], dsa_k[g], dsa_v[g], mixb[g], mixb[3 + g], dsa_lse[g], DSA_DILS[g]) for g in range(3)]

    def dsa_prep_bwd(qkv, cs, sn, *rest):
        dqs, dks, dvs, (gq, gk, bd) = rest[0:3], rest[3:6], rest[6:9], rest[9:]
        mean = _mean_heads(bd)
        dq, dgq = _rms_bwd(qkv[:, :DSA_W].astype(F32), gq, _rope_bwd(jnp.concatenate(dqs, axis=1), cs, sn), mean)
        dk, dgk = _rms_bwd(qkv[:, DSA_W:2 * DSA_W].astype(F32), gk, _rope_bwd(jnp.concatenate(dks, axis=1), cs, sn), mean)
        return jnp.concatenate([dq, dk] + list(dvs), axis=1), dgq, dgk

    dall, dgq_dsa, dgk_dsa = _tokmap(
        "dsa_prep_bwd", dsa_prep_bwd,
        [qkv_dsa, cos, sin] + [dsa_d[g][t] for t in range(3) for g in range(3)], [gq_dsa, gk_dsa, bd768],
        [(3 * DSA_W, BF16)], [(1, DSA_W), (1, DSA_W)], tile=512, dil_ins={3 + j: DSA_DILS[j % 3] for j in range(9)},
        place={0: (c_all, c_dsa // (3 * DSA_W), dall)})
    gs["qn_dsa"] = dgq_dsa.reshape(DSA_W // HD, HD).sum(axis=0, keepdims=True)
    gs["kn_dsa"] = dgk_dsa.reshape(DSA_W // HD, HD).sum(axis=0, keepdims=True)

    def mem_attn_bwd(qv, dov, kmv, vmv, gq, bd):
        qn, ps = mem_probs(qv, kmv, gq, bd)
        dqn, dkm, dvm = [], [], []
        for h, p in enumerate(ps):
            sl = slice(HD * h, HD * h + HD)
            dp = _dot(dov[:, sl], vmv[:, sl], NT)
            ds = (p * (dp - jnp.sum(p * dp, axis=1, keepdims=True)) * SCALE).astype(BF16)
            dqn.append(_dot(ds, kmv[:, sl]))
            dkm.append(_dot(ds, qn[:, sl], TN))
            dvm.append(_dot(p.astype(BF16), dov[:, sl], TN))
        dq, dgq = _rms_bwd(qv.astype(F32), gq, jnp.concatenate(dqn, axis=1), _mean_heads(bd))
        return dq, jnp.concatenate(dkm, axis=1), jnp.concatenate(dvm, axis=1), dgq

    dall, dkm, dvm, dgq_mem = _tokmap("mem_attn_bwd", mem_attn_bwd, [q_mem, do_mem], [km, vm, gq_mem, bd256],
                                      [(MEM_W, BF16)], [(MEM_LEN, MEM_W), (MEM_LEN, MEM_W), (1, MEM_W)],
                                      place={0: (c_all, c_qm // MEM_W, dall)})
    gs["qn_mem"] = dgq_mem.reshape(MEM_W // HD, HD).sum(axis=0, keepdims=True)

    def mem_kv_bwd(memv, dkmv, dvmv, g, wkv, gk, bd):
        memn = _rms_fwd(memv, g, _mean_all).astype(BF16)
        kv = _dot(memn, wkv)
        dk, dgk = _rms_bwd(kv[:, :MEM_W], gk, dkmv, _mean_heads(bd))
        dkv = jnp.concatenate([dk, dvmv], axis=1).astype(BF16)
        _, dg = _rms_bwd(memv, g, _dot(dkv, wkv, NT), _mean_all)
        return _dot(memn, dkv, TN), dg, dgk

    gw["w_mem_kv"], gs["mem_norm"], dgk_mem = _tokmap(
        "mem_kv_bwd", mem_kv_bwd, [mem, dkm, dvm], [sm["mem_norm"], w["w_mem_kv"], gk_mem, bd256], [],
        [(D, 2 * MEM_W), (1, D), (1, MEM_W)])
    gs["kn_mem"] = dgk_mem.reshape(MEM_W // HD, HD).sum(axis=0, keepdims=True)

    dx1, gs["mix_norm"] = _matmul("proj_bwd_dx", dall, w_all, NT, F32, epi=_norm_bwd_epi, tiles=[x1, dx2],
                                  rows=[sm["mix_norm"]], n_sum=1)
    dw_all = _matmul("proj_bwd_dw", hmix, dall, TN, F32)
    gw["w_gate"], gw["w_in"] = dw_all[:, :c_sb], dw_all[:, c_sb:]
    on_grads("mid", {n: gw[n] for n in GROUPS["mid"]})
    gx, gs["ffn1_norm"], _, _ = _ffn_bwd("ffn1", x, sm["ffn1_norm"], w13_1, w["ffn1_w2"], ffn1_saved, dx1, plan,
                                         ffn_grads("ffn1"), final=True)
    return loss, gx, gw, gs


def _shard_shape(name):
    shape, axis = SHARDED_BY_NAME[name]
    return (shape[0] // N_CHIPS, shape[1]) if axis == 0 else (shape[0], shape[1] // N_CHIPS)


def _full_from_shards(name, shards):
    axis = SHARDED_BY_NAME[name][1]
    return shards.reshape(SHARDED_BY_NAME[name][0]) if axis == 0 else jnp.concatenate(list(shards), axis=1)


def _shards_from_full(name, full, dtype):
    axis, n = SHARDED_BY_NAME[name][1], _shard_shape(name)
    return jnp.stack([lax.slice_in_dim(full, c * n[axis], (c + 1) * n[axis], axis=axis).astype(dtype) for c in range(N_CHIPS)])


def _own_shard(name, full, chip):
    axis, n = SHARDED_BY_NAME[name][1], _shard_shape(name)
    return lax.dynamic_slice_in_dim(full, chip * n[axis], n[axis], axis=axis)


SMALL_USED = sum(n for _, n in SMALL)


def _pack_small(d, loss=None):
    parts = [d[n].reshape(-1) for n, _ in SMALL]
    parts.append(jnp.zeros((1,), F32) if loss is None else loss.reshape(1))
    parts.append(jnp.zeros((SMALL_ROWS * D - SMALL_USED - 1,), F32))
    return jnp.concatenate(parts).reshape(SMALL_ROWS, D)


def _unpack_small(v):
    flat, out, r = v.reshape(-1), {}, 0
    for n, k in SMALL:
        out[n] = flat[r:r + k]
        r += k
    return out, flat[r]


def _place():
    return lax.axis_index("x"), lax.axis_index("y"), lax.axis_index("c")


def _other_chips(x, y):
    return [(1 - x, y), (x, 1 - y), (1 - x, 1 - y)]


HBM_SPEC = pl.BlockSpec(memory_space=pl.ANY)


def _chip_sems(n):
    return (pltpu.SemaphoreType.DMA((3 * n,)), pltpu.SemaphoreType.DMA((3 * n,)), pltpu.SemaphoreType.DMA((n,)))


def _gather_copies(ins, outs, send_sems, recv_sems, local_sems):
    x, y, c = _place()
    me = 2 * x + y
    copies = []
    for a, (src, out) in enumerate(zip(ins, outs)):
        copies.append(pltpu.make_async_copy(src, out.at[me], local_sems.at[a]))
        copies += [pltpu.make_async_remote_copy(src_ref=src, dst_ref=out.at[me], send_sem=send_sems.at[3 * a + k],
                                                recv_sem=recv_sems.at[3 * a + k], device_id=(px, py, c), device_id_type=MESH)
                   for k, (px, py) in enumerate(_other_chips(x, y))]
    return copies


def _scatter_copies(ins, outs, send_sems, recv_sems, local_sems):
    x, y, c = _place()
    return [pltpu.make_async_remote_copy(src_ref=src.at[2 * px + py], dst_ref=out.at[k], send_sem=send_sems.at[3 * a + k],
                                         recv_sem=recv_sems.at[3 * a + k], device_id=(px, py, c), device_id_type=MESH)
            for a, (src, out) in enumerate(zip(ins, outs)) for k, (px, py) in enumerate(_other_chips(x, y))]


def _all_gather_chips(arrays):
    n = len(arrays)

    def body(*refs):
        ins, outs = refs[:n], refs[n:2 * n]
        send1, recv1, send2, recv2, local_sems = refs[2 * n:]
        x, y, c = _place()
        me = 2 * x + y
        chips = _other_chips(x, y)
        local = [pltpu.make_async_copy(src, out.at[me], local_sems.at[a]) for a, (src, out) in enumerate(zip(ins, outs))]
        for cp in local:
            cp.start()

        def half(ref, chip, which):
            rows = ref.shape[-2] // 2
            return ref.at[chip, pl.ds(which * rows, rows)] if chip is not None else ref.at[pl.ds(which * rows, rows)]

        first = [pltpu.make_async_remote_copy(src_ref=half(src, None, c), dst_ref=half(out, me, c), send_sem=send1.at[3 * a + k],
                                              recv_sem=recv1.at[3 * a + k], device_id=(px, py, c), device_id_type=MESH)
                 for a, (src, out) in enumerate(zip(ins, outs)) for k, (px, py) in enumerate(chips)]
        for cp in first:
            cp.start()
        passed = []
        for a, out in enumerate(outs):
            for k, (px, py) in enumerate(chips):
                pltpu.make_async_remote_copy(src_ref=half(out, 2 * px + py, c), dst_ref=half(out, 2 * px + py, c),
                                             send_sem=send1.at[3 * a + k], recv_sem=recv1.at[3 * a + k],
                                             device_id=(px, py, c), device_id_type=MESH).wait_recv()
                cp = pltpu.make_async_remote_copy(src_ref=half(out, 2 * px + py, c), dst_ref=half(out, 2 * px + py, c),
                                                  send_sem=send2.at[3 * a + k], recv_sem=recv2.at[3 * a + k],
                                                  device_id=(x, y, 1 - c), device_id_type=MESH)
                cp.start()
                passed.append(cp)
        for a, out in enumerate(outs):
            for k, (px, py) in enumerate(chips):
                pltpu.make_async_remote_copy(src_ref=half(out, 2 * px + py, 1 - c), dst_ref=half(out, 2 * px + py, 1 - c),
                                             send_sem=send2.at[3 * a + k], recv_sem=recv2.at[3 * a + k],
                                             device_id=(x, y, 1 - c), device_id_type=MESH).wait_recv()
        for cp in first + passed:
            cp.wait_send()
        for cp in local:
            cp.wait()

    sems = pltpu.SemaphoreType.DMA((3 * n,))
    return pl.pallas_call(
        body, name="weights_all_gather", in_specs=[HBM_SPEC] * n, out_specs=[HBM_SPEC] * n,
        out_shape=[SDS((N_CHIPS,) + a.shape, a.dtype) for a in arrays],
        scratch_shapes=[sems, sems, sems, sems, pltpu.SemaphoreType.DMA((n,))],
    )(*arrays)


def _swap_with_sibling(name, arrays):
    n = len(arrays)

    def body(*refs):
        x, y, c = _place()
        send_sems, recv_sems = refs[2 * n:]
        copies = [pltpu.make_async_remote_copy(src_ref=refs[a], dst_ref=refs[n + a], send_sem=send_sems.at[a],
                                               recv_sem=recv_sems.at[a], device_id=(x, y, 1 - c), device_id_type=MESH)
                  for a in range(n)]
        for cp in copies:
            cp.start()
        for cp in copies:
            cp.wait()

    return pl.pallas_call(
        body, name=name, in_specs=[HBM_SPEC] * n, out_specs=[HBM_SPEC] * n, out_shape=[SDS(a.shape, a.dtype) for a in arrays],
        scratch_shapes=[pltpu.SemaphoreType.DMA((n,)), pltpu.SemaphoreType.DMA((n,))],
    )(*arrays)


def _all_reduce_small(v):
    n_dev = 8

    def body(v_ref, out_ref, land, send_sems, recv_sems):
        x, y, c = _place()
        me = 4 * x + 2 * y + c
        land[me] = v_ref[...]
        copies = []
        for k in range(1, n_dev):
            peer = (x ^ (k >> 2), y ^ ((k >> 1) & 1), c ^ (k & 1))
            copies.append(pltpu.make_async_remote_copy(src_ref=v_ref, dst_ref=land.at[me], send_sem=send_sems.at[k - 1],
                                                       recv_sem=recv_sems.at[k - 1], device_id=peer, device_id_type=MESH))
        for cp in copies:
            cp.start()
        for cp in copies:
            cp.wait()
        acc = land[0]
        for d in range(1, n_dev):
            acc = acc + land[d]
        out_ref[...] = acc

    return pl.pallas_call(
        body, name="small_all_reduce", in_specs=[pl.BlockSpec(memory_space=pltpu.VMEM)],
        out_specs=pl.BlockSpec(memory_space=pltpu.VMEM), out_shape=SDS(v.shape, v.dtype),
        scratch_shapes=[pltpu.VMEM((n_dev,) + v.shape, v.dtype), pltpu.SemaphoreType.DMA((n_dev - 1,)),
                        pltpu.SemaphoreType.DMA((n_dev - 1,))],
    )(v)


def _adamw(g, wv, m, v):
    m = ADAM_B1 * m + (1.0 - ADAM_B1) * g
    v = ADAM_B2 * v + (1.0 - ADAM_B2) * (g * g)
    m_hat = m / (1.0 - ADAM_B1 ** ADAM_STEP)
    v_hat = v / (1.0 - ADAM_B2 ** ADAM_STEP)
    delta = -ADAM_LR * (m_hat / (jnp.sqrt(v_hat) + ADAM_EPS) + ADAM_WD * wv)
    return delta, m, v


def kernel(x, mem, ffn1_norm, ffn1_w1, ffn1_w3, ffn1_w2, mix_norm, mem_norm, w_in, w_mem_kv, qn_dsa, kn_dsa, qn_mem, kn_mem, w_branch_sb, w_branch_dsa, w_branch_mem, w_gate, b_gate, w_out, ffn2_norm, ffn2_w1, ffn2_w3, ffn2_w2, loss_target, m_ffn1_norm, m_ffn1_w1, m_ffn1_w3, m_ffn1_w2, m_mix_norm, m_mem_norm, m_w_in, m_w_mem_kv, m_qn_dsa, m_kn_dsa, m_qn_mem, m_kn_mem, m_w_branch_sb, m_w_branch_dsa, m_w_branch_mem, m_w_gate, m_b_gate, m_w_out, m_ffn2_norm, m_ffn2_w1, m_ffn2_w3, m_ffn2_w2, v_ffn1_norm, v_ffn1_w1, v_ffn1_w3, v_ffn1_w2, v_mix_norm, v_mem_norm, v_w_in, v_w_mem_kv, v_qn_dsa, v_kn_dsa, v_qn_mem, v_kn_mem, v_w_branch_sb, v_w_branch_dsa, v_w_branch_mem, v_w_gate, v_b_gate, v_w_out, v_ffn2_norm, v_ffn2_w1, v_ffn2_w3, v_ffn2_w2):
    given = dict(locals())
    wts = {n: given[n][0] for n in WEIGHTS}
    moms = {n: given["m_" + n][0] for n in WEIGHTS}
    vars_ = {n: given["v_" + n][0] for n in WEIGHTS}

    plan = _Plan()
    x_i, y_i, _ = _place()
    my_chip = 2 * x_i + y_i

    full = {}

    def gathered(names):
        return lambda res: full.update({n: _full_from_shards(n, g) for n, g in zip(names, res)})

    for host, names in WEIGHT_PIECES:
        shards = [wts[n].astype(BF16) for n in names]
        if host is None:
            gathered(names)(_all_gather_chips(shards))
        else:
            plan.put(host, _Carry(shards, [SDS((N_CHIPS,) + a.shape, BF16) for a in shards], _chip_sems(len(names)),
                                  _gather_copies, gathered(names)))
    small = {n: wts[n].reshape(1, -1) for n, _ in SMALL}

    landed = {}

    def on_grads(group, grads):
        names = GROUPS[group]
        slices = [_shards_from_full(n, grads[n], BF16) for n in names]
        own = [_own_shard(n, grads[n], my_chip) for n in names]
        plan.put(GRAD_HOSTS[group], _Carry(slices, [SDS((3,) + a.shape[1:], BF16) for a in slices], _chip_sems(len(names)),
                                           _scatter_copies, lambda res: landed.update({group: (own, res)})))

    loss, gx, _, gs = _local_step(x[0], mem[0], loss_target[0], full, small, plan, on_grads)
    assert not plan.pending, list(plan.pending)

    def update(hv, ov, wv, mv, vv):
        g = hv + ov
        return (g,) + _adamw(g, wv, mv, vv)

    outs = [{}, {}, {}, {}]
    for group, names in GROUPS.items():
        own, got = landed[group]
        halves = [_tokmap(f"grads_sum_chips_{n}",
                          lambda a, b0, b1, b2: ((a + b0.astype(F32)) + b1.astype(F32)) + b2.astype(F32),
                          [o, g[0], g[1], g[2]], [], [(o.shape[1], F32)])[0] for n, o, g in zip(names, own, got)]
        others = _swap_with_sibling(f"grads_swap_cores_{group}", halves)
        for n, half, other in zip(names, halves, others):
            res = _tokmap(f"adamw_{n}", update, [half, other, wts[n], moms[n], vars_[n]], [], [(half.shape[1], F32)] * 4)
            for d, r in zip(outs, res):
                d[n] = r

    s_red = _all_reduce_small(_pack_small(gs, loss[0, 0]))
    res = _tokmap(
        "adamw_small", lambda g, wv, mv, vv: (g,) + _adamw(g, wv, mv, vv),
        [s_red, _pack_small(small), _pack_small({n: moms[n] for n, _ in SMALL}), _pack_small({n: vars_[n] for n, _ in SMALL})],
        [], [(D, F32)] * 4)
    for d, packed in zip(outs, res):
        d.update(_unpack_small(packed)[0])
    _, total_loss = _unpack_small(s_red)
    return (total_loss, gx[None], *[d[n][None] for d in outs for n in WEIGHTS])
```

```python
import functools

import numpy as np
import jax
import jax.numpy as jnp
from jax import lax
from jax.experimental import pallas as pl
from jax.experimental.pallas import tpu as pltpu

F32, BF16 = jnp.float32, jnp.bfloat16
SDS = jax.ShapeDtypeStruct
MESH = pl.DeviceIdType.MESH

D = 1024
HD = 64
QB = 128
DSA_T_FWD, DSA_T_BWD = 512, 256
D_FF = 2816
SB_W, DSA_W, DSA_OUT_W, MEM_W = 512, 768, 256, 256
DSA_DILS = (1, 4, 16)
MEM_LEN = 256
N_CHIPS = 4
EPS = 1e-6
SCALE = HD ** -0.5
EXHAUSTED = -104.0
SB_FWD_HEADS = 4
SB_QB = 256
SB_CUM = 128
SB_WIN = 512
NEG = -1e30
VMEM_LIMIT = 56 * 1024 * 1024

ADAM_LR, ADAM_B1, ADAM_B2, ADAM_EPS, ADAM_WD, ADAM_STEP = 0.001, 0.9, 0.999, 1e-08, 0.01, 10

NN = (((1,), (0,)), ((), ()))
NT = (((1,), (1,)), ((), ()))
TN = (((0,), (0,)), ((), ()))

SHARDED = (
    ("ffn1_w1", (D, D_FF), 1), ("ffn1_w3", (D, D_FF), 1), ("ffn1_w2", (D_FF, D), 0),
    ("w_in", (D, 4096), 1), ("w_mem_kv", (D, 512), 0),
    ("w_branch_sb", (SB_W, D), 1), ("w_branch_dsa", (DSA_OUT_W, D), 1), ("w_branch_mem", (MEM_W, D), 1),
    ("w_gate", (D, 3 * D), 1), ("w_out", (D, D), 0),
    ("ffn2_w1", (D, D_FF), 1), ("ffn2_w3", (D, D_FF), 1), ("ffn2_w2", (D_FF, D), 0),
)
SHARDED_BY_NAME = {n: (sh, ax) for n, sh, ax in SHARDED}
GROUPS = {
    "ffn2": ("ffn2_w1", "ffn2_w3", "ffn2_w2"),
    "mid": ("w_in", "w_mem_kv", "w_branch_sb", "w_branch_dsa", "w_branch_mem", "w_gate", "w_out"),
    "ffn1": ("ffn1_w1", "ffn1_w3", "ffn1_w2"),
}
WEIGHT_PIECES = (
    (None, ("ffn1_w1", "ffn1_w3")),
    ("ffn1_up", ("ffn1_w2", "w_in")),
    ("ffn1_down", ("w_gate", "w_mem_kv", "w_branch_sb", "w_branch_dsa", "w_branch_mem", "w_out")),
    ("proj_dsa", ("ffn2_w2",)),
    ("proj_gate", ("ffn2_w1", "ffn2_w3")),
)
GRAD_HOSTS = {"ffn2": "ffn2_bwd_dn", "mid": "ffn1_bwd_dw13", "ffn1": "ffn1_bwd_dn"}
SMALL = (("ffn1_norm", D), ("mix_norm", D), ("mem_norm", D), ("ffn2_norm", D), ("b_gate", 3 * D),
         ("qn_dsa", HD), ("kn_dsa", HD), ("qn_mem", HD), ("kn_mem", HD))
WEIGHTS = ("ffn1_norm", "ffn1_w1", "ffn1_w3", "ffn1_w2", "mix_norm", "mem_norm", "w_in", "w_mem_kv", "qn_dsa", "kn_dsa",
           "qn_mem", "kn_mem", "w_branch_sb", "w_branch_dsa", "w_branch_mem", "w_gate", "b_gate", "w_out", "ffn2_norm",
           "ffn2_w1", "ffn2_w3", "ffn2_w2")
SMALL_ROWS = 8


def _dot(a, b, dn=NN):
    return lax.dot_general(a, b, dn, preferred_element_type=F32)


def _dot01(x, m01, pieces=3):
    hi = x.astype(BF16)
    r1 = x - hi.astype(F32)
    mid = r1.astype(BF16)
    if pieces == 2:
        return _dot(hi, m01) + _dot(mid, m01)
    lo = (r1 - mid.astype(F32)).astype(BF16)
    return _dot(hi, m01) + _dot(mid, m01) + _dot(lo, m01)


def _pick(n, cands):
    for c in cands:
        if n % c == 0:
            return c
    raise ValueError(f"no tile for {n}")


def _from_dilated(v, d, scr):
    w = v.shape[1] // d
    v = v.astype(F32)
    for c in range(d):
        for p, buf in enumerate(scr[:w // 128]):
            buf[pl.ds(c, v.shape[0], stride=d), :] = v[:, c * w + 128 * p:c * w + 128 * (p + 1)]
    return jnp.concatenate([buf[...] for buf in scr[:w // 128]], axis=1)


def _to_dilated(v, d, scr):
    w = v.shape[1]
    for p, buf in enumerate(scr[:w // 128]):
        buf[...] = v[:, 128 * p:128 * (p + 1)].astype(F32)
    return jnp.concatenate([buf[pl.ds(c, v.shape[0] // d, stride=d), :] for c in range(d) for buf in scr[:w // 128]], axis=1)


def _tokmap(name, fn, tok_ins, consts, tok_outs, acc_outs=(), tile=512, dil_ins=None, dil_outs=None, place=None):
    dil_ins, dil_outs, place = dil_ins or {}, dil_outs or {}, place or {}
    bufs = [(j, buf) for j, (_, _, buf) in place.items() if buf is not None]
    n_buf = len(bufs)
    n = tok_ins[0].shape[0] * dil_ins.get(0, 1)
    tile = _pick(n, [t for t in (512, 352, 256, 128, 64, 32, 16, 8) if t <= tile])
    n_tin, n_in, n_tok, n_acc = len(tok_ins), len(tok_ins) + len(consts), len(tok_outs), len(acc_outs)
    n_scr = max([tok_ins[j].shape[1] // d // 128 for j, d in dil_ins.items() if d > 1]
                + [tok_outs[j][0] // 128 for j, d in dil_outs.items() if d > 1] + [0])

    def body(*refs):
        scr = refs[len(refs) - n_scr:]
        vals = [r[...] for r in refs[:n_in]]
        for j, d in dil_ins.items():
            if d > 1:
                vals[j] = _from_dilated(vals[j], d, scr)
        outs = fn(*vals)
        outs = list(outs) if isinstance(outs, (tuple, list)) else [outs]
        assert len(outs) == n_tok + n_acc, (name, len(outs))
        for j, d in dil_outs.items():
            if d > 1:
                outs[j] = _to_dilated(outs[j], d, scr)
        orefs = refs[n_in + n_buf:]
        for r, v in zip(orefs[:n_tok], outs[:n_tok]):
            r[...] = v.astype(r.dtype)
        if n_acc:
            @pl.when(pl.program_id(0) == 0)
            def _():
                for r in orefs[n_tok:n_tok + n_acc]:
                    r[...] = jnp.zeros(r.shape, r.dtype)
            for r, v in zip(orefs[n_tok:n_tok + n_acc], outs[n_tok:]):
                r[...] += v.astype(r.dtype)

    def tok_spec(width, d):
        return pl.BlockSpec((tile // d, d * width), lambda i: (i, 0))

    in_specs = [tok_spec(a.shape[1] // dil_ins.get(j, 1), dil_ins.get(j, 1)) for j, a in enumerate(tok_ins)]
    in_specs += [pl.BlockSpec(c.shape, lambda i: (0, 0)) for c in consts]
    in_specs += [HBM_SPEC] * n_buf
    out_specs = [tok_spec(w, dil_outs.get(j, 1)) for j, (w, _) in enumerate(tok_outs)]
    out_shape = [SDS((n // dil_outs.get(j, 1), w * dil_outs.get(j, 1)), dt) for j, (w, dt) in enumerate(tok_outs)]
    for j, (total, col_block, _) in place.items():
        out_specs[j] = pl.BlockSpec((tile, tok_outs[j][0]), lambda i, cb=col_block: (i, cb))
        out_shape[j] = SDS((n, total), tok_outs[j][1])
    out_specs += [pl.BlockSpec(s, lambda i: (0, 0)) for s in acc_outs]
    out_shape += [SDS(s, F32) for s in acc_outs]
    res = pl.pallas_call(
        body, name=name, grid=(n // tile,), in_specs=in_specs, out_specs=out_specs, out_shape=out_shape,
        scratch_shapes=[pltpu.VMEM((tile, 128), F32)] * n_scr,
        input_output_aliases={n_in + b: j for b, (j, _) in enumerate(bufs)},
        compiler_params=pltpu.CompilerParams(dimension_semantics=("arbitrary",), vmem_limit_bytes=VMEM_LIMIT),
    )(*tok_ins, *consts, *[buf for _, buf in bufs])
    return res


MATMUL_VMEM_BUDGET = 40 * 1024 * 1024


def _matmul_tiles(m, n, k, a_bytes, b_bytes, o_bytes, extra_bytes, whole_n=False):
    best = None
    for tk in [c for c in (3584, 2816, 2048, 1408, 1024, 512, 256, 128) if k % c == 0]:
        for tm in [c for c in (1408, 1024, 768, 512, 256, 128) if m % c == 0]:
            for tn in [n] if whole_n else [c for c in (1408, 1024, 768, 512, 256, 128) if n % c == 0]:
                need = 2 * tk * (tm * a_bytes + tn * b_bytes) + tm * tn * (2 * o_bytes + 2 * extra_bytes + 8)
                if need > MATMUL_VMEM_BUDGET:
                    continue
                score = (min(tm, 512) * min(tn, 512), tk, tm * tn, tn)
                if best is None or score > best[0]:
                    best = (score, (tm, tn, tk))
    return best[1]


class _Carry:
    def __init__(self, ins, outs, sems, copies, then):
        self.ins, self.outs, self.sems, self.copies, self.then = ins, outs, sems, copies, then


class _Plan:
    def __init__(self):
        self.pending = {}

    def put(self, host, carry):
        assert host not in self.pending, host
        self.pending[host] = carry

    def take(self, host):
        return self.pending.pop(host, None)


def _matmul(name, a, b, dn, out_dtype, epi=None, tiles=(), rows=(), plan=None, a_pro=None, n_sum=0):
    if dn == NN:
        (m, k), n = a.shape, b.shape[1]
    elif dn == NT:
        (m, k), n = a.shape, b.shape[0]
    else:
        (k, m), n = a.shape, b.shape[1]
    n_t, n_r = len(tiles), len(rows)
    pro, n_parts = a_pro if a_pro is not None else (None, 1)
    if pro is not None:
        assert dn == NN and n == _pick(n, (1024, 512))
        k //= n_parts
    if pro is not None:
        tm, tn, tk = _pick(m, (256, 128)), n, k
    else:
        tm, tn, tk = _matmul_tiles(m, n, k, a.dtype.itemsize, b.dtype.itemsize, jnp.dtype(out_dtype).itemsize,
                                   sum(t.dtype.itemsize for t in tiles), whole_n=n_sum > 0)
    nk = k // tk
    grid = (m // tm, n // tn, nk)
    assert pro is None or grid[1] == 1
    assert n_sum == 0 or grid[1] == 1
    carry = plan.take(name) if plan is not None else None
    n_ci, n_co = (len(carry.ins), len(carry.outs)) if carry else (0, 0)
    n_keep = 1 if pro is not None else 0

    def body(*refs):
        a_refs, b_ref, rest = refs[:n_parts], refs[n_parts], refs[n_parts + 1:]
        extras, rest = rest[:n_t + n_r], rest[n_t + n_r:]
        c_in, o_ref, rest = rest[:n_ci], rest[n_ci], rest[n_ci + 1:]
        sums, rest = rest[:n_sum], rest[n_sum:]
        keep, c_out, scratch = rest[:n_keep], rest[n_keep:n_keep + n_co], rest[n_keep + n_co:]
        ids = [pl.program_id(d) for d in range(3)]
        if n_sum:
            @pl.when((ids[0] == 0) & (ids[2] == 0))
            def _():
                for r in sums:
                    r[...] = jnp.zeros(r.shape, F32)
        if carry:
            sems = scratch[1:] if nk > 1 else scratch

            @pl.when((ids[0] == 0) & (ids[1] == 0) & (ids[2] == 0))
            def _():
                for cp in carry.copies(c_in, c_out, *sems):
                    cp.start()

        if pro is not None:
            av = pro(*[r[...] for r in a_refs])
            keep[0][...] = av
        else:
            av = a_refs[0][...].astype(BF16)
        part = _dot(av, b_ref[...].astype(BF16), dn)

        def finish(r):
            if epi is not None:
                r = epi(r, *[e[...] for e in extras])
            if n_sum:
                for ref, v in zip(sums, r[1:]):
                    ref[...] += v
                r = r[0]
            o_ref[...] = r.astype(o_ref.dtype)

        if nk == 1:
            finish(part)
        else:
            acc = scratch[0]

            @pl.when(ids[2] == 0)
            def _():
                acc[...] = part

            @pl.when(ids[2] > 0)
            def _():
                acc[...] += part

            @pl.when(ids[2] == nk - 1)
            def _():
                finish(acc[...])

        if carry:
            @pl.when((ids[0] == grid[0] - 1) & (ids[1] == grid[1] - 1) & (ids[2] == nk - 1))
            def _():
                for cp in carry.copies(c_in, c_out, *sems):
                    cp.wait()

    if dn == TN:
        a_specs = [pl.BlockSpec((tk, tm), lambda i, j, kk: (kk, i))]
    else:
        a_specs = [pl.BlockSpec((tm, tk), lambda i, j, kk, p=p: (i, kk + p * nk)) for p in range(n_parts)]
    b_spec = pl.BlockSpec((tn, tk), lambda i, j, kk: (j, kk)) if dn == NT else pl.BlockSpec((tk, tn), lambda i, j, kk: (kk, j))
    in_specs = a_specs + [b_spec] + [pl.BlockSpec((tm, tn), lambda i, j, kk: (i, j)) for _ in tiles]
    in_specs += [pl.BlockSpec((1, tn), lambda i, j, kk: (0, j)) for _ in rows] + [HBM_SPEC] * n_ci
    res = pl.pallas_call(
        body, name=name, grid=grid, in_specs=in_specs,
        out_specs=[pl.BlockSpec((tm, tn), lambda i, j, kk: (i, j))] + [pl.BlockSpec((1, tn), lambda i, j, kk: (0, 0))] * n_sum
        + [pl.BlockSpec((tm, tk), lambda i, j, kk: (i, kk))] * n_keep + [HBM_SPEC] * n_co,
        out_shape=[SDS((m, n), out_dtype)] + [SDS((1, n), F32)] * n_sum + [SDS((m, k), BF16)] * n_keep
        + (list(carry.outs) if carry else []),
        scratch_shapes=([pltpu.VMEM((tm, tn), F32)] if nk > 1 else []) + (list(carry.sems) if carry else []),
        compiler_params=pltpu.CompilerParams(
            dimension_semantics=("arbitrary",) * 3 if (carry or n_sum) else ("parallel", "parallel", "arbitrary"),
            vmem_limit_bytes=VMEM_LIMIT),
    )(*[a] * n_parts, b, *tiles, *rows, *(carry.ins if carry else []))
    if carry:
        carry.then(res[1 + n_sum + n_keep:])
    return tuple(res[:1 + n_sum + n_keep]) if n_sum + n_keep else res[0]


def _mean_all(v):
    return jnp.mean(v, axis=-1, keepdims=True)


def _head_sums(v, bd):
    w = bd.shape[0]
    return jnp.concatenate([_dot01(v[:, j:j + w], bd, 2) for j in range(0, v.shape[1], w)], axis=1)


def _mean_heads(bd):
    return lambda v: _head_sums(v, bd) * (1.0 / HD)


def _rms_fwd(x, g, mean):
    return x * lax.rsqrt(mean(x * x) + EPS) * g


def _rms_bwd(x, g, dy, mean):
    r = lax.rsqrt(mean(x * x) + EPS)
    dn = dy * g
    dx = r * dn - x * (r * r * r) * mean(dn * x)
    return dx, jnp.sum(dy * x * r, axis=0, keepdims=True)


def _swap_halves(x):
    w = x.shape[1]
    lane = lax.broadcasted_iota(jnp.int32, x.shape, 1)
    return jnp.where(lane % HD < HD // 2, pltpu.roll(x, w - HD // 2, 1), pltpu.roll(x, HD // 2, 1))


def _lanes(t, w):
    return jnp.tile(t, (1, w // t.shape[1]))


def _rope_fwd(x, cos, sin_signed):
    return x * _lanes(cos, x.shape[1]) + _swap_halves(x) * _lanes(sin_signed, x.shape[1])


def _rope_bwd(dy, cos, sin_signed):
    return dy * _lanes(cos, dy.shape[1]) + _swap_halves(dy * _lanes(sin_signed, dy.shape[1]))


def _bcast_heads(cols):
    return jnp.concatenate([jnp.broadcast_to(c, (c.shape[0], HD)) for c in cols], axis=1)


def _softplus(z):
    return jnp.maximum(z, 0.0) + jnp.log(1.0 + jnp.exp(-jnp.abs(z)))


def _block_diag(w):
    h = np.arange(w) // HD
    return jnp.asarray(h[:, None] == h[None, :], BF16)


def _sb_window(i, t):
    hi = (i + 1) * SB_QB - t * SB_WIN
    lo = hi - SB_WIN
    ws = pl.multiple_of(jnp.maximum(lo, 0), SB_QB)
    kpos = ws + lax.broadcasted_iota(jnp.int32, (SB_QB, SB_WIN), 1)
    qpos = i * SB_QB + lax.broadcasted_iota(jnp.int32, (SB_QB, SB_WIN), 0)
    return (kpos < qpos) & (kpos >= lo) & (kpos < hi), ws


def _sb_fwd(qkv):
    s = qkv.shape[0]
    assert s >= SB_WIN
    nq = s // SB_QB
    nh = SB_FWD_HEADS
    bw = HD * nh
    ngroups = SB_W // bw

    def body(q_ref, k_ref, v_ref, later_ref, o_ref, tot_ref, w0_ref, b0_ref, nb_ref):
        p, i = pl.program_id(0), pl.program_id(1)
        q = q_ref[...] * SCALE
        later_of = later_ref[...]

        def window(t, tots, outs, keep):
            mask, ws = _sb_window(i, t)
            kw, vw = k_ref[pl.ds(ws, SB_WIN), :], v_ref[pl.ds(ws, SB_WIN), :]
            new_t, new_o, w_all, b_all = [], [], [], []
            for hh in range(nh):
                sl = slice(HD * hh, HD * hh + HD)
                z = _dot(q[:, sl], kw[:, sl], NT)
                sp = _softplus(z)
                lf = jnp.where(mask, -sp, 0.0)
                cols = [lf[:, c0:c0 + SB_CUM] for c0 in range(0, SB_WIN, SB_CUM)]
                nearer, parts = jnp.zeros((SB_QB, 1), F32), []
                for blk in reversed(cols):
                    parts.append(_dot01(blk, later_of, 2) + nearer)
                    nearer = nearer + jnp.sum(blk, axis=1, keepdims=True)
                later = tots[hh] + jnp.concatenate(parts[::-1], axis=1)
                w = jnp.where(mask, jnp.exp(z - sp + later), 0.0).astype(BF16)
                new_o.append(outs[hh] + _dot(w, vw[:, sl]))
                new_t.append(tots[hh] + jnp.sum(lf, axis=1, keepdims=True))
                if keep:
                    w_all.append(w)
                    b_all.append(jnp.where(mask, jnp.exp(z - sp), 0.0).astype(BF16))
            if keep:
                w0_ref[...] = jnp.concatenate(w_all, axis=1)
                b0_ref[...] = jnp.concatenate(b_all, axis=1)
            alive = functools.reduce(jnp.maximum, [jnp.max(v) for v in new_t])
            return t + 1, alive, tuple(new_t), tuple(new_o)

        zt, zo = jnp.zeros((SB_QB, 1), F32), jnp.zeros((SB_QB, HD), F32)
        first = window(jnp.int32(0), (zt,) * nh, (zo,) * nh, True)
        t, _, tots, outs = lax.while_loop(lambda c: ((i + 1) * SB_QB - c[0] * SB_WIN > 0) & (c[1] > EXHAUSTED),
                                          lambda c: window(c[0], c[2], c[3], False), first)
        o_ref[...] = jnp.concatenate(outs, axis=1).astype(o_ref.dtype)
        tot_ref[...] = _bcast_heads(tots)
        nb_ref[p, i] = t

    whole = lambda off: pl.BlockSpec((s, bw), lambda p, i: (0, off + p), pipeline_mode=pl.Buffered(1))
    tile = pl.BlockSpec((SB_QB, bw), lambda p, i: (i, p))
    tri = pl.BlockSpec((SB_CUM, SB_CUM), lambda p, i: (0, 0), pipeline_mode=pl.Buffered(1))
    near = pl.BlockSpec((SB_QB, nh * SB_WIN), lambda p, i: (i, p))
    n_heads = SB_W // HD
    idx = np.arange(SB_CUM)
    return pl.pallas_call(
        body, name="sb_fwd", grid=(ngroups, nq),
        in_specs=[tile, whole(ngroups), whole(2 * ngroups), tri],
        out_specs=[tile, tile, near, near, pl.BlockSpec(memory_space=pltpu.SMEM)],
        out_shape=[SDS((s, SB_W), BF16), SDS((s, SB_W), F32), SDS((s, n_heads * SB_WIN), BF16), SDS((s, n_heads * SB_WIN), BF16),
                   SDS((ngroups, nq), jnp.int32)],
        compiler_params=pltpu.CompilerParams(dimension_semantics=("arbitrary", "arbitrary"), vmem_limit_bytes=VMEM_LIMIT),
    )(qkv, qkv, qkv, jnp.asarray(idx[:, None] > idx[None, :], BF16))


def _sb_bwd(qkv, do, tot, nblk, w0, b0, buf, col):
    s = qkv.shape[0]
    nq = s // SB_QB
    npairs = SB_W // 128

    def body(nb_ref, q_ref, k_ref, v_ref, do_ref, tot_ref, upto_ref, before_ref, w0_ref, b0_ref, buf_ref,
             dq_ref, dk_ref, dv_ref):
        p, i = pl.program_id(0), pl.program_id(1)

        @pl.when(i == 0)
        def _():
            dk_ref[...] = jnp.zeros(dk_ref.shape, F32)
            dv_ref[...] = jnp.zeros(dv_ref.shape, F32)

        upto = upto_ref[...]
        before = before_ref[...]
        q, dout, tt = q_ref[...] * SCALE, do_ref[...], tot_ref[...]
        n = nb_ref[p * 2 // SB_FWD_HEADS, i]

        def step(it, c):
            pres, gpres, dqs = c
            mask, ws = _sb_window(i, n - 1 - it)
            kw, vw = k_ref[pl.ds(ws, SB_WIN), :], v_ref[pl.ds(ws, SB_WIN), :]
            new_p, new_g, new_dq, dks, dvs = [], [], [], [], []
            for hh in range(2):
                sl = slice(HD * hh, HD * hh + HD)
                z = _dot(q[:, sl], kw[:, sl], NT)
                sp = _softplus(z)
                lf = jnp.where(mask, -sp, 0.0)
                later = tt[:, HD * hh:HD * hh + 1] - (pres[hh] + _dot01(lf, upto, 2))
                w = jnp.where(mask, jnp.exp(z - sp + later), 0.0)
                beta = jnp.exp(z - sp)
                g = _dot(dout[:, sl], vw[:, sl], NT) * w
                g_far = gpres[hh] + _dot(g.astype(BF16), before)
                dz = jnp.where(mask, g * (1.0 - beta) - beta * g_far, 0.0).astype(BF16)
                new_dq.append(dqs[hh] + _dot(dz, kw[:, sl]))
                dks.append(_dot(dz, q[:, sl], TN))
                dvs.append(_dot(w.astype(BF16), dout[:, sl], TN))
                new_p.append(pres[hh] + jnp.sum(lf, axis=1, keepdims=True))
                new_g.append(gpres[hh] + jnp.sum(g, axis=1, keepdims=True))
            dk_ref[pl.ds(ws, SB_WIN), :] += jnp.concatenate(dks, axis=1)
            dv_ref[pl.ds(ws, SB_WIN), :] += jnp.concatenate(dvs, axis=1)
            return tuple(new_p), tuple(new_g), tuple(new_dq)

        zt, zo = jnp.zeros((SB_QB, 1), F32), jnp.zeros((SB_QB, HD), F32)
        _, gpres, dqs = lax.fori_loop(0, n - 1, step, ((zt, zt), (zt, zt), (zo, zo)))
        _, ws = _sb_window(i, 0)
        kw, vw = k_ref[pl.ds(ws, SB_WIN), :], v_ref[pl.ds(ws, SB_WIN), :]
        dqs, dks, dvs = list(dqs), [], []
        for hh in range(2):
            sl = slice(HD * hh, HD * hh + HD)
            w = w0_ref[:, SB_WIN * hh:SB_WIN * (hh + 1)]
            beta = b0_ref[:, SB_WIN * hh:SB_WIN * (hh + 1)].astype(F32)
            g = _dot(dout[:, sl], vw[:, sl], NT) * w.astype(F32)
            g16, half = g.astype(BF16), SB_WIN // 2
            g_far = gpres[hh] + jnp.concatenate(
                [_dot(g16[:, :half], before[:half, :half]),
                 _dot(g16[:, half:], before[:half, :half]) + jnp.sum(g[:, :half], axis=1, keepdims=True)], axis=1)
            dz = (g * (1.0 - beta) - beta * g_far).astype(BF16)
            dqs[hh] = dqs[hh] + _dot(dz, kw[:, sl])
            dks.append(_dot(dz, q[:, sl], TN))
            dvs.append(_dot(w, dout[:, sl], TN))
        dk_ref[pl.ds(ws, SB_WIN), :] += jnp.concatenate(dks, axis=1)
        dv_ref[pl.ds(ws, SB_WIN), :] += jnp.concatenate(dvs, axis=1)
        dq_ref[...] = (jnp.concatenate(dqs, axis=1) * SCALE).astype(dq_ref.dtype)

    whole_in = lambda off: pl.BlockSpec((s, 128), lambda p, i: (0, off + p), pipeline_mode=pl.Buffered(1))
    whole_out = pl.BlockSpec((s, 128), lambda p, i: (0, p), pipeline_mode=pl.Buffered(1))
    tile = pl.BlockSpec((SB_QB, 128), lambda p, i: (i, p))
    near = pl.BlockSpec((SB_QB, 2 * SB_WIN), lambda p, i: (i, p))
    dq_tile = pl.BlockSpec((SB_QB, 128), lambda p, i: (i, col // 128 + p))
    tri = pl.BlockSpec((SB_WIN, SB_WIN), lambda p, i: (0, 0), pipeline_mode=pl.Buffered(1))
    idx = np.arange(SB_WIN)
    return pl.pallas_call(
        body, name="sb_bwd", grid=(npairs, nq),
        in_specs=[pl.BlockSpec(memory_space=pltpu.SMEM), tile, whole_in(npairs), whole_in(2 * npairs), tile, tile, tri, tri,
                  near, near, HBM_SPEC],
        out_specs=[dq_tile, whole_out, whole_out],
        out_shape=[SDS(buf.shape, buf.dtype)] + [SDS((s, SB_W), F32)] * 2,
        input_output_aliases={10: 0},
        compiler_params=pltpu.CompilerParams(dimension_semantics=("arbitrary", "arbitrary"), vmem_limit_bytes=VMEM_LIMIT),
    )(nblk, qkv, qkv, qkv, do, tot, jnp.asarray(idx[:, None] <= idx[None, :], BF16), jnp.asarray(idx[:, None] < idx[None, :], BF16),
      w0, b0, buf)


def _dsa_mask(DSA_T, has_prev):
    r = lax.broadcasted_iota(jnp.int32, (DSA_T, QB + DSA_T), 0)
    j = lax.broadcasted_iota(jnp.int32, (DSA_T, QB + DSA_T), 1) - QB
    return (j <= r) & (j >= r - QB) & ((j >= 0) | has_prev)


def _dsa_fwd(q, k, v, dil):
    n = q.shape[0]
    DSA_T = DSA_T_FWD
    nt = n // DSA_T

    def body(q_ref, kc_ref, kp_ref, vc_ref, vp_ref, o_ref, lse_ref):
        mask = _dsa_mask(DSA_T, pl.program_id(1) > 0)
        outs, lses = [], []
        for hh in range(DSA_OUT_W // HD):
            sl = slice(HD * hh, HD * hh + HD)
            kcat = jnp.concatenate([kp_ref[:, sl], kc_ref[:, sl]], axis=0)
            vcat = jnp.concatenate([vp_ref[:, sl], vc_ref[:, sl]], axis=0)
            sc = jnp.where(mask, _dot(q_ref[:, sl] * SCALE, kcat, NT), NEG)
            m = jnp.max(sc, axis=1, keepdims=True)
            p = jnp.exp(sc - m)
            den = jnp.sum(p, axis=1, keepdims=True)
            outs.append(_dot(p.astype(BF16), vcat) / den)
            lses.append(m + jnp.log(den))
        o_ref[...] = jnp.concatenate(outs, axis=1)
        lse_ref[...] = _bcast_heads(lses)

    cur = pl.BlockSpec((DSA_T, DSA_OUT_W), lambda c, i: (i, c))
    prev = pl.BlockSpec((QB, DSA_OUT_W), lambda c, i: (jnp.maximum(i * (DSA_T // QB) - 1, 0), c))
    o, lse = pl.pallas_call(
        body, name=f"dsa_fwd_d{dil}", grid=(dil, nt), in_specs=[cur, cur, prev, cur, prev], out_specs=[cur, cur],
        out_shape=[SDS((n, dil * DSA_OUT_W), F32)] * 2,
        compiler_params=pltpu.CompilerParams(dimension_semantics=("parallel", "parallel")),
    )(q, k, k, v, v)
    return o, lse


def _dsa_bwd(q, k, v, do, cc, lse, dil):
    n = q.shape[0]
    DSA_T = DSA_T_BWD
    nt = n // DSA_T
    per = DSA_T // QB

    def body(qj_ref, qn_ref, kp_ref, kj_ref, vp_ref, vj_ref, doj_ref, don_ref, cj_ref, cn_ref, lj_ref, ln_ref,
             dq_ref, dk_ref, dv_ref):
        j = pl.program_id(1)
        mask = _dsa_mask(DSA_T, j > 0)
        r = lax.broadcasted_iota(jnp.int32, (QB, DSA_T), 0)
        kk = lax.broadcasted_iota(jnp.int32, (QB, DSA_T), 1)
        m_next = (kk >= r + QB) & (j + 1 < nt)
        dqs, dks, dvs = [], [], []
        for hh in range(DSA_OUT_W // HD):
            sl = slice(HD * hh, HD * hh + HD)
            one = slice(HD * hh, HD * hh + 1)
            kj, vj, doj, don = (t[:, sl] for t in (kj_ref, vj_ref, doj_ref, don_ref))
            qj, qn = qj_ref[:, sl] * SCALE, qn_ref[:, sl] * SCALE
            kcat = jnp.concatenate([kp_ref[:, sl], kj], axis=0)
            vcat = jnp.concatenate([vp_ref[:, sl], vj], axis=0)
            p1 = jnp.where(mask, jnp.exp(_dot(qj, kcat, NT) - lj_ref[:, one]), 0.0)
            ds1 = (p1 * (_dot(doj, vcat, NT) + cj_ref[:, one])).astype(BF16)
            p2 = jnp.where(m_next, jnp.exp(_dot(qn, kj, NT) - ln_ref[:, one]), 0.0)
            ds2 = (p2 * (_dot(don, vj, NT) + cn_ref[:, one])).astype(BF16)
            dqs.append(_dot(ds1, kcat) * SCALE)
            dks.append(_dot(ds1[:, QB:], qj, TN) + _dot(ds2, qn, TN))
            dvs.append(_dot(p1[:, QB:].astype(BF16), doj, TN) + _dot(p2.astype(BF16), don, TN))
        dq_ref[...] = jnp.concatenate(dqs, axis=1).astype(dq_ref.dtype)
        dk_ref[...] = jnp.concatenate(dks, axis=1).astype(dk_ref.dtype)
        dv_ref[...] = jnp.concatenate(dvs, axis=1).astype(dv_ref.dtype)

    cur = pl.BlockSpec((DSA_T, DSA_OUT_W), lambda c, j: (j, c))
    prev = pl.BlockSpec((QB, DSA_OUT_W), lambda c, j: (jnp.maximum(j * per - 1, 0), c))
    nxt = pl.BlockSpec((QB, DSA_OUT_W), lambda c, j: (jnp.minimum((j + 1) * per, n // QB - 1), c))
    dq, dk, dv = pl.pallas_call(
        body, name=f"dsa_bwd_d{dil}", grid=(dil, nt),
        in_specs=[cur, nxt, prev, cur, prev, cur, cur, nxt, cur, nxt, cur, nxt], out_specs=[cur, cur, cur],
        out_shape=[SDS((n, dil * DSA_OUT_W), BF16)] * 3,
        compiler_params=pltpu.CompilerParams(dimension_semantics=("parallel", "parallel")),
    )(q, q, k, k, v, v, do, do, cc, cc, lse, lse)
    return dq, dk, dv


def _norm_bwd_epi(acc, xv, dyv, g):
    dx, dg = _rms_bwd(xv, g, acc, _mean_all)
    return dx + dyv, dg


def _ffn_fwd(tag, x, gain, w13, w2, plan=None, target=None):
    n = _tokmap(f"{tag}_norm", lambda xv, g: _rms_fwd(xv, g, _mean_all), [x], [gain], [(D, BF16)])[0]
    ab = _matmul(f"{tag}_up", n, w13, NN, BF16, plan=plan)

    def gate(av, bv):
        a, b = av.astype(F32), bv.astype(F32)
        return (a * jax.nn.sigmoid(a) * b).astype(BF16)

    if target is None:
        y, h = _matmul(f"{tag}_down", ab, w2(), NN, F32, epi=lambda acc, res: res + 0.5 * acc, tiles=[x], plan=plan,
                       a_pro=(gate, 2))
        return y, (n, ab, h)

    def loss_epi(acc, res, tv):
        e = res + 0.5 * acc - tv
        return e * (1.0 / D), (0.5 / D) * jnp.sum(e * e, axis=0, keepdims=True)

    dy, loss_row, h = _matmul(f"{tag}_down", ab, w2(), NN, F32, epi=loss_epi, tiles=[x, target], plan=plan, a_pro=(gate, 2),
                              n_sum=1)
    return (dy, loss_row), (n, ab, h)


def _ffn_bwd(tag, x, gain, w13, w2, saved, dy, plan=None, on_dw=None, final=False):
    n, ab, h = saved
    dh = _matmul(f"{tag}_bwd_dh", dy, w2, NT, BF16, epi=lambda acc: 0.5 * acc)

    def gate_bwd(abv, dhv):
        a, b, dhf = abv[:, :D_FF].astype(F32), abv[:, D_FF:].astype(F32), dhv.astype(F32)
        sg = jax.nn.sigmoid(a)
        da = dhf * b * (sg * (1.0 + a * (1.0 - sg)))
        return jnp.concatenate([da, dhf * (a * sg)], axis=1)

    dab = _tokmap(f"{tag}_bwd_gate", gate_bwd, [ab, dh], [], [(2 * D_FF, BF16)], tile=256)[0]
    dw2 = _matmul(f"{tag}_bwd_dw2", h, dy, TN, F32, epi=lambda acc: 0.5 * acc)
    dw13 = _matmul(f"{tag}_bwd_dw13", n, dab, TN, F32, plan=plan)
    if on_dw is not None:
        on_dw(dw13, dw2)
    if not final:
        dx, dgain = _matmul(f"{tag}_bwd_dn", dab, w13, NT, F32, epi=_norm_bwd_epi, tiles=[x, dy], rows=[gain], plan=plan,
                            n_sum=1)
    else:
        dn = _matmul(f"{tag}_bwd_dn", dab, w13, NT, F32, plan=plan)
        dx, dgain = _tokmap(f"{tag}_bwd_norm", lambda xv, dnv, dyv, g: _norm_bwd_epi(dnv, xv, dyv, g), [x, dn, dy], [gain],
                            [(D, F32)], [(1, D)])
    return dx, dgain, dw13, dw2


def _rope_tables(s):
    half = HD // 2
    inv_freq = jnp.power(10000.0, -jnp.arange(half, dtype=F32) / half)
    ang = jnp.arange(s).astype(F32)[:, None] * inv_freq[None, :]
    cos, sin = jnp.cos(ang), jnp.sin(ang)
    return jnp.tile(jnp.concatenate([cos, cos], axis=1), (1, 2)), jnp.tile(jnp.concatenate([-sin, sin], axis=1), (1, 2))


def _local_step(x, mem, tgt, w, sm, plan=None, on_grads=None):
    s = x.shape[0]
    assert s % (max(DSA_T_FWD, DSA_T_BWD) * max(DSA_DILS)) == 0
    on_grads = on_grads or (lambda group, grads: None)
    c_sb, c_dsa, c_qm, c_all = 3 * D, 3 * D + 3 * SB_W, 3 * D + 3 * SB_W + 3 * DSA_W, 3 * D + 4096
    cos, sin = _rope_tables(s)
    bd768 = bd256 = _block_diag(128)
    gq_dsa, gk_dsa = jnp.tile(sm["qn_dsa"], (1, DSA_W // HD)), jnp.tile(sm["kn_dsa"], (1, DSA_W // HD))
    gq_mem, gk_mem = jnp.tile(sm["qn_mem"], (1, MEM_W // HD)), jnp.tile(sm["kn_mem"], (1, MEM_W // HD))

    w13_1 = jnp.concatenate([w["ffn1_w1"], w["ffn1_w3"]], axis=1)
    x1, ffn1_saved = _ffn_fwd("ffn1", x, sm["ffn1_norm"], w13_1, lambda: w["ffn1_w2"], plan)
    w_all = jnp.concatenate([w["w_gate"], w["w_in"]], axis=1)
    wb_sb, wb_dsa, wb_mem = w["w_branch_sb"], w["w_branch_dsa"], w["w_branch_mem"]
    hmix = _tokmap("mix_norm", lambda xv, g: _rms_fwd(xv, g, _mean_all), [x1], [sm["mix_norm"]], [(D, BF16)])[0]
    qkv_sb = _matmul("proj_sb", hmix, w_all[:, c_sb:c_dsa], NN, BF16)
    qkv_dsa = _matmul("proj_dsa", hmix, w_all[:, c_dsa:c_qm], NN, BF16, plan=plan)
    q_mem = _matmul("proj_qmem", hmix, w_all[:, c_qm:], NN, BF16)
    gpre = _matmul("proj_gate", hmix, w_all[:, :c_sb], NN, BF16, epi=lambda acc, b: acc + b, rows=[sm["b_gate"]], plan=plan)

    o_sb, sb_tot, sb_w0, sb_b0, sb_nblk = _sb_fwd(qkv_sb)

    def dsa_prep(qkv, cs, sn, gq, gk, bd):
        mean = _mean_heads(bd)
        qn = _rope_fwd(_rms_fwd(qkv[:, :DSA_W].astype(F32), gq, mean), cs, sn)
        kn = _rope_fwd(_rms_fwd(qkv[:, DSA_W:2 * DSA_W].astype(F32), gk, mean), cs, sn)
        v = qkv[:, 2 * DSA_W:]
        outs = []
        for t in (qn, kn, v):
            outs += [t[:, DSA_OUT_W * g:DSA_OUT_W * (g + 1)] for g in range(3)]
        return outs

    dsa_in = _tokmap("dsa_prep", dsa_prep, [qkv_dsa, cos, sin], [gq_dsa, gk_dsa, bd768], [(DSA_OUT_W, BF16)] * 9, tile=512,
                     dil_outs={j: DSA_DILS[j % 3] for j in range(9)})
    dsa_q, dsa_k, dsa_v = dsa_in[0:3], dsa_in[3:6], dsa_in[6:9]
    dsa_o, dsa_lse = zip(*[_dsa_fwd(dsa_q[g], dsa_k[g], dsa_v[g], DSA_DILS[g]) for g in range(3)])

    def alphas(l0, l1, l2):
        m = jnp.maximum(jnp.maximum(l0, l1), l2)
        e = [jnp.exp(l - m) for l in (l0, l1, l2)]
        tot = e[0] + e[1] + e[2]
        return [t / tot for t in e]

    def dsa_mix(o0, o1, o2, l0, l1, l2):
        a = alphas(l0, l1, l2)
        return a[0] * o0 + a[1] * o1 + a[2] * o2

    o_dsa = _tokmap("dsa_mix", dsa_mix, [*dsa_o, *dsa_lse], [], [(DSA_OUT_W, BF16)], tile=512,
                    dil_ins={j: DSA_DILS[j % 3] for j in range(6)})[0]

    def mem_kv(memv, g, wkv, gk, bd):
        kv = _dot(_rms_fwd(memv, g, _mean_all).astype(BF16), wkv)
        return _rms_fwd(kv[:, :MEM_W], gk, _mean_heads(bd)), kv[:, MEM_W:]

    km, vm = _tokmap("mem_kv", mem_kv, [mem], [sm["mem_norm"], w["w_mem_kv"], gk_mem, bd256], [(MEM_W, BF16)] * 2)

    def mem_probs(qv, kmv, gq, bd):
        qn = _rms_fwd(qv.astype(F32), gq, _mean_heads(bd)).astype(BF16)
        ps = []
        for h in range(MEM_W // HD):
            sl = slice(HD * h, HD * h + HD)
            sc = _dot(qn[:, sl], kmv[:, sl], NT) * SCALE
            e = jnp.exp(sc - jnp.max(sc, axis=1, keepdims=True))
            ps.append(e / jnp.sum(e, axis=1, keepdims=True))
        return qn, ps

    def mem_attn(qv, kmv, vmv, gq, bd):
        _, ps = mem_probs(qv, kmv, gq, bd)
        return jnp.concatenate([_dot(p.astype(BF16), vmv[:, HD * h:HD * h + HD]) for h, p in enumerate(ps)], axis=1)

    o_mem = _tokmap("mem_attn", mem_attn, [q_mem], [km, vm, gq_mem, bd256], [(MEM_W, BF16)])[0]

    def merge(osb, odsa, omem, gp, w_sb, w_dsa, w_mem):
        gates = jax.nn.sigmoid(gp.astype(F32))
        ys = (_dot(osb, w_sb), _dot(odsa, w_dsa), _dot(omem, w_mem))
        return gates, ys, gates[:, :D] * ys[0] + gates[:, D:2 * D] * ys[1] + gates[:, 2 * D:] * ys[2]

    merged = _tokmap("merge", lambda *a: merge(*a)[2], [o_sb, o_dsa, o_mem, gpre], [wb_sb, wb_dsa, wb_mem], [(D, BF16)],
                     tile=512)[0]
    x2 = _matmul("out_proj", merged, w["w_out"], NN, F32, epi=lambda acc, res: res + acc, tiles=[x1])
    w13_2 = jnp.concatenate([w["ffn2_w1"], w["ffn2_w3"]], axis=1)
    (dy, loss_row), ffn2_saved = _ffn_fwd("ffn2", x2, sm["ffn2_norm"], w13_2, lambda: w["ffn2_w2"], target=tgt)
    loss = jnp.sum(loss_row).reshape(1, 1)

    gw, gs = {}, {}
    def ffn_grads(tag):
        def on_dw(dw13, dw2):
            gw[f"{tag}_w1"], gw[f"{tag}_w3"], gw[f"{tag}_w2"] = dw13[:, :D_FF], dw13[:, D_FF:], dw2
            on_grads(tag, {n: gw[n] for n in (f"{tag}_w1", f"{tag}_w3", f"{tag}_w2")})
        return on_dw

    dx2, gs["ffn2_norm"], _, _ = _ffn_bwd("ffn2", x2, sm["ffn2_norm"], w13_2, w["ffn2_w2"], ffn2_saved, dy, plan,
                                          ffn_grads("ffn2"))
    dmerged = _matmul("out_proj_bwd_dx", dx2, w["w_out"], NT, BF16)
    gw["w_out"] = _matmul("out_proj_bwd_dw", merged, dx2, TN, F32)

    def merge_bwd(osb, odsa, omem, gp, dm, w_sb, w_dsa, w_mem):
        gates, ys, _ = merge(osb, odsa, omem, gp, w_sb, w_dsa, w_mem)
        dmf = dm.astype(F32)
        dgp, dos, dws = [], [], []
        for b, (ov, wv) in enumerate(((osb, w_sb), (odsa, w_dsa), (omem, w_mem))):
            gb = gates[:, D * b:D * (b + 1)]
            dgp.append(dmf * ys[b] * gb * (1.0 - gb))
            dyb = (dmf * gb).astype(BF16)
            dos.append(_dot(dyb, wv, NT))
            dws.append(_dot(ov, dyb, TN))
        dgp = jnp.concatenate(dgp, axis=1)
        return dos[0], dos[1], dos[2], dgp, dws[0], dws[1], dws[2], jnp.sum(dgp, axis=0, keepdims=True)

    do_sb, do_dsa, do_mem, dgpre, gw["w_branch_sb"], gw["w_branch_dsa"], gw["w_branch_mem"], gs["b_gate"] = _tokmap(
        "merge_bwd", merge_bwd, [o_sb, o_dsa, o_mem, gpre, dmerged], [wb_sb, wb_dsa, wb_mem],
        [(SB_W, BF16), (DSA_OUT_W, F32), (MEM_W, BF16), (3 * D, BF16)],
        [(SB_W, D), (DSA_OUT_W, D), (MEM_W, D), (1, 3 * D)], tile=512, place={3: (c_all, 0, None)})

    dall, dk_sb, dv_sb = _sb_bwd(qkv_sb, do_sb, sb_tot, sb_nblk, sb_w0, sb_b0, dgpre, c_sb)
    dall = lax.dynamic_update_slice(dall, dk_sb.astype(BF16), (0, c_sb + SB_W))
    dall = lax.dynamic_update_slice(dall, dv_sb.astype(BF16), (0, c_sb + 2 * SB_W))

    def dsa_mix_bwd(o0, o1, o2, l0, l1, l2, dov, bd):
        a = alphas(l0, l1, l2)
        omix = a[0] * o0 + a[1] * o1 + a[2] * o2
        dot_o = _head_sums(dov * omix, bd)
        return [dov * t for t in a] + [-t * dot_o for t in a]

    mixb = _tokmap("dsa_mix_bwd", dsa_mix_bwd, [*dsa_o, *dsa_lse, do_dsa], [bd256],
                   [(DSA_OUT_W, BF16)] * 3 + [(DSA_OUT_W, F32)] * 3, tile=512,
                   dil_ins={j: DSA_DILS[j % 3] for j in range(6)}, dil_outs={j: DSA_DILS[j % 3] for j in range(6)})
    dsa_d = [_dsa_bwd(dsa_q[g], dsa_k[g], dsa_v[g], mixb[g], mixb[3 + g], dsa_lse[g], DSA_DILS[g]) for g in range(3)]

    def dsa_prep_bwd(qkv, cs, sn, *rest):
        dqs, dks, dvs, (gq, gk, bd) = rest[0:3], rest[3:6], rest[6:9], rest[9:]
        mean = _mean_heads(bd)
        dq, dgq = _rms_bwd(qkv[:, :DSA_W].astype(F32), gq, _rope_bwd(jnp.concatenate(dqs, axis=1), cs, sn), mean)
        dk, dgk = _rms_bwd(qkv[:, DSA_W:2 * DSA_W].astype(F32), gk, _rope_bwd(jnp.concatenate(dks, axis=1), cs, sn), mean)
        return jnp.concatenate([dq, dk] + list(dvs), axis=1), dgq, dgk

    dall, dgq_dsa, dgk_dsa = _tokmap(
        "dsa_prep_bwd", dsa_prep_bwd,
        [qkv_dsa, cos, sin] + [dsa_d[g][t] for t in range(3) for g in range(3)], [gq_dsa, gk_dsa, bd768],
        [(3 * DSA_W, BF16)], [(1, DSA_W), (1, DSA_W)], tile=512, dil_ins={3 + j: DSA_DILS[j % 3] for j in range(9)},
        place={0: (c_all, c_dsa // (3 * DSA_W), dall)})
    gs["qn_dsa"] = dgq_dsa.reshape(DSA_W // HD, HD).sum(axis=0, keepdims=True)
    gs["kn_dsa"] = dgk_dsa.reshape(DSA_W // HD, HD).sum(axis=0, keepdims=True)

    def mem_attn_bwd(qv, dov, kmv, vmv, gq, bd):
        qn, ps = mem_probs(qv, kmv, gq, bd)
        dqn, dkm, dvm = [], [], []
        for h, p in enumerate(ps):
            sl = slice(HD * h, HD * h + HD)
            dp = _dot(dov[:, sl], vmv[:, sl], NT)
            ds = (p * (dp - jnp.sum(p * dp, axis=1, keepdims=True)) * SCALE).astype(BF16)
            dqn.append(_dot(ds, kmv[:, sl]))
            dkm.append(_dot(ds, qn[:, sl], TN))
            dvm.append(_dot(p.astype(BF16), dov[:, sl], TN))
        dq, dgq = _rms_bwd(qv.astype(F32), gq, jnp.concatenate(dqn, axis=1), _mean_heads(bd))
        return dq, jnp.concatenate(dkm, axis=1), jnp.concatenate(dvm, axis=1), dgq

    dall, dkm, dvm, dgq_mem = _tokmap("mem_attn_bwd", mem_attn_bwd, [q_mem, do_mem], [km, vm, gq_mem, bd256],
                                      [(MEM_W, BF16)], [(MEM_LEN, MEM_W), (MEM_LEN, MEM_W), (1, MEM_W)],
                                      place={0: (c_all, c_qm // MEM_W, dall)})
    gs["qn_mem"] = dgq_mem.reshape(MEM_W // HD, HD).sum(axis=0, keepdims=True)

    def mem_kv_bwd(memv, dkmv, dvmv, g, wkv, gk, bd):
        memn = _rms_fwd(memv, g, _mean_all).astype(BF16)
        kv = _dot(memn, wkv)
        dk, dgk = _rms_bwd(kv[:, :MEM_W], gk, dkmv, _mean_heads(bd))
        dkv = jnp.concatenate([dk, dvmv], axis=1).astype(BF16)
        _, dg = _rms_bwd(memv, g, _dot(dkv, wkv, NT), _mean_all)
        return _dot(memn, dkv, TN), dg, dgk

    gw["w_mem_kv"], gs["mem_norm"], dgk_mem = _tokmap(
        "mem_kv_bwd", mem_kv_bwd, [mem, dkm, dvm], [sm["mem_norm"], w["w_mem_kv"], gk_mem, bd256], [],
        [(D, 2 * MEM_W), (1, D), (1, MEM_W)])
    gs["kn_mem"] = dgk_mem.reshape(MEM_W // HD, HD).sum(axis=0, keepdims=True)

    dx1, gs["mix_norm"] = _matmul("proj_bwd_dx", dall, w_all, NT, F32, epi=_norm_bwd_epi, tiles=[x1, dx2],
                                  rows=[sm["mix_norm"]], n_sum=1)
    dw_all = _matmul("proj_bwd_dw", hmix, dall, TN, F32)
    gw["w_gate"], gw["w_in"] = dw_all[:, :c_sb], dw_all[:, c_sb:]
    on_grads("mid", {n: gw[n] for n in GROUPS["mid"]})
    gx, gs["ffn1_norm"], _, _ = _ffn_bwd("ffn1", x, sm["ffn1_norm"], w13_1, w["ffn1_w2"], ffn1_saved, dx1, plan,
                                         ffn_grads("ffn1"), final=True)
    return loss, gx, gw, gs


def _shard_shape(name):
    shape, axis = SHARDED_BY_NAME[name]
    return (shape[0] // N_CHIPS, shape[1]) if axis == 0 else (shape[0], shape[1] // N_CHIPS)


def _full_from_shards(name, shards):
    axis = SHARDED_BY_NAME[name][1]
    return shards.reshape(SHARDED_BY_NAME[name][0]) if axis == 0 else jnp.concatenate(list(shards), axis=1)


def _shards_from_full(name, full, dtype):
    axis, n = SHARDED_BY_NAME[name][1], _shard_shape(name)
    return jnp.stack([lax.slice_in_dim(full, c * n[axis], (c + 1) * n[axis], axis=axis).astype(dtype) for c in range(N_CHIPS)])


def _own_shard(name, full, chip):
    axis, n = SHARDED_BY_NAME[name][1], _shard_shape(name)
    return lax.dynamic_slice_in_dim(full, chip * n[axis], n[axis], axis=axis)


SMALL_USED = sum(n for _, n in SMALL)


def _pack_small(d, loss=None):
    parts = [d[n].reshape(-1) for n, _ in SMALL]
    parts.append(jnp.zeros((1,), F32) if loss is None else loss.reshape(1))
    parts.append(jnp.zeros((SMALL_ROWS * D - SMALL_USED - 1,), F32))
    return jnp.concatenate(parts).reshape(SMALL_ROWS, D)


def _unpack_small(v):
    flat, out, r = v.reshape(-1), {}, 0
    for n, k in SMALL:
        out[n] = flat[r:r + k]
        r += k
    return out, flat[r]


def _place():
    return lax.axis_index("x"), lax.axis_index("y"), lax.axis_index("c")


def _other_chips(x, y):
    return [(1 - x, y), (x, 1 - y), (1 - x, 1 - y)]


HBM_SPEC = pl.BlockSpec(memory_space=pl.ANY)


def _chip_sems(n):
    return (pltpu.SemaphoreType.DMA((3 * n,)), pltpu.SemaphoreType.DMA((3 * n,)), pltpu.SemaphoreType.DMA((n,)))


def _gather_copies(ins, outs, send_sems, recv_sems, local_sems):
    x, y, c = _place()
    me = 2 * x + y
    copies = []
    for a, (src, out) in enumerate(zip(ins, outs)):
        copies.append(pltpu.make_async_copy(src, out.at[me], local_sems.at[a]))
        copies += [pltpu.make_async_remote_copy(src_ref=src, dst_ref=out.at[me], send_sem=send_sems.at[3 * a + k],
                                                recv_sem=recv_sems.at[3 * a + k], device_id=(px, py, c), device_id_type=MESH)
                   for k, (px, py) in enumerate(_other_chips(x, y))]
    return copies


def _scatter_copies(ins, outs, send_sems, recv_sems, local_sems):
    x, y, c = _place()
    return [pltpu.make_async_remote_copy(src_ref=src.at[2 * px + py], dst_ref=out.at[k], send_sem=send_sems.at[3 * a + k],
                                         recv_sem=recv_sems.at[3 * a + k], device_id=(px, py, c), device_id_type=MESH)
            for a, (src, out) in enumerate(zip(ins, outs)) for k, (px, py) in enumerate(_other_chips(x, y))]


def _all_gather_chips(arrays):
    n = len(arrays)

    def body(*refs):
        ins, outs = refs[:n], refs[n:2 * n]
        send1, recv1, send2, recv2, local_sems = refs[2 * n:]
        x, y, c = _place()
        me = 2 * x + y
        chips = _other_chips(x, y)
        local = [pltpu.make_async_copy(src, out.at[me], local_sems.at[a]) for a, (src, out) in enumerate(zip(ins, outs))]
        for cp in local:
            cp.start()

        def half(ref, chip, which):
            rows = ref.shape[-2] // 2
            return ref.at[chip, pl.ds(which * rows, rows)] if chip is not None else ref.at[pl.ds(which * rows, rows)]

        first = [pltpu.make_async_remote_copy(src_ref=half(src, None, c), dst_ref=half(out, me, c), send_sem=send1.at[3 * a + k],
                                              recv_sem=recv1.at[3 * a + k], device_id=(px, py, c), device_id_type=MESH)
                 for a, (src, out) in enumerate(zip(ins, outs)) for k, (px, py) in enumerate(chips)]
        for cp in first:
            cp.start()
        passed = []
        for a, out in enumerate(outs):
            for k, (px, py) in enumerate(chips):
                pltpu.make_async_remote_copy(src_ref=half(out, 2 * px + py, c), dst_ref=half(out, 2 * px + py, c),
                                             send_sem=send1.at[3 * a + k], recv_sem=recv1.at[3 * a + k],
                                             device_id=(px, py, c), device_id_type=MESH).wait_recv()
                cp = pltpu.make_async_remote_copy(src_ref=half(out, 2 * px + py, c), dst_ref=half(out, 2 * px + py, c),
                                                  send_sem=send2.at[3 * a + k], recv_sem=recv2.at[3 * a + k],
                                                  device_id=(x, y, 1 - c), device_id_type=MESH)
                cp.start()
                passed.append(cp)
        for a, out in enumerate(outs):
            for k, (px, py) in enumerate(chips):
                pltpu.make_async_remote_copy(src_ref=half(out, 2 * px + py, 1 - c), dst_ref=half(out, 2 * px + py, 1 - c),
                                             send_sem=send2.at[3 * a + k], recv_sem=recv2.at[3 * a + k],
                                             device_id=(x, y, 1 - c), device_id_type=MESH).wait_recv()
        for cp in first + passed:
            cp.wait_send()
        for cp in local:
            cp.wait()

    sems = pltpu.SemaphoreType.DMA((3 * n,))
    return pl.pallas_call(
        body, name="weights_all_gather", in_specs=[HBM_SPEC] * n, out_specs=[HBM_SPEC] * n,
        out_shape=[SDS((N_CHIPS,) + a.shape, a.dtype) for a in arrays],
        scratch_shapes=[sems, sems, sems, sems, pltpu.SemaphoreType.DMA((n,))],
    )(*arrays)


def _swap_with_sibling(name, arrays):
    n = len(arrays)

    def body(*refs):
        x, y, c = _place()
        send_sems, recv_sems = refs[2 * n:]
        copies = [pltpu.make_async_remote_copy(src_ref=refs[a], dst_ref=refs[n + a], send_sem=send_sems.at[a],
                                               recv_sem=recv_sems.at[a], device_id=(x, y, 1 - c), device_id_type=MESH)
                  for a in range(n)]
        for cp in copies:
            cp.start()
        for cp in copies:
            cp.wait()

    return pl.pallas_call(
        body, name=name, in_specs=[HBM_SPEC] * n, out_specs=[HBM_SPEC] * n, out_shape=[SDS(a.shape, a.dtype) for a in arrays],
        scratch_shapes=[pltpu.SemaphoreType.DMA((n,)), pltpu.SemaphoreType.DMA((n,))],
    )(*arrays)


def _all_reduce_small(v):
    n_dev = 8

    def body(v_ref, out_ref, land, send_sems, recv_sems):
        x, y, c = _place()
        me = 4 * x + 2 * y + c
        land[me] = v_ref[...]
        copies = []
        for k in range(1, n_dev):
            peer = (x ^ (k >> 2), y ^ ((k >> 1) & 1), c ^ (k & 1))
            copies.append(pltpu.make_async_remote_copy(src_ref=v_ref, dst_ref=land.at[me], send_sem=send_sems.at[k - 1],
                                                       recv_sem=recv_sems.at[k - 1], device_id=peer, device_id_type=MESH))
        for cp in copies:
            cp.start()
        for cp in copies:
            cp.wait()
        acc = land[0]
        for d in range(1, n_dev):
            acc = acc + land[d]
        out_ref[...] = acc

    return pl.pallas_call(
        body, name="small_all_reduce", in_specs=[pl.BlockSpec(memory_space=pltpu.VMEM)],
        out_specs=pl.BlockSpec(memory_space=pltpu.VMEM), out_shape=SDS(v.shape, v.dtype),
        scratch_shapes=[pltpu.VMEM((n_dev,) + v.shape, v.dtype), pltpu.SemaphoreType.DMA((n_dev - 1,)),
                        pltpu.SemaphoreType.DMA((n_dev - 1,))],
    )(v)


def _adamw(g, wv, m, v):
    m = ADAM_B1 * m + (1.0 - ADAM_B1) * g
    v = ADAM_B2 * v + (1.0 - ADAM_B2) * (g * g)
    m_hat = m / (1.0 - ADAM_B1 ** ADAM_STEP)
    v_hat = v / (1.0 - ADAM_B2 ** ADAM_STEP)
    delta = -ADAM_LR * (m_hat / (jnp.sqrt(v_hat) + ADAM_EPS) + ADAM_WD * wv)
    return delta, m, v


def kernel(x, mem, ffn1_norm, ffn1_w1, ffn1_w3, ffn1_w2, mix_norm, mem_norm, w_in, w_mem_kv, qn_dsa, kn_dsa, qn_mem, kn_mem, w_branch_sb, w_branch_dsa, w_branch_mem, w_gate, b_gate, w_out, ffn2_norm, ffn2_w1, ffn2_w3, ffn2_w2, loss_target, m_ffn1_norm, m_ffn1_w1, m_ffn1_w3, m_ffn1_w2, m_mix_norm, m_mem_norm, m_w_in, m_w_mem_kv, m_qn_dsa, m_kn_dsa, m_qn_mem, m_kn_mem, m_w_branch_sb, m_w_branch_dsa, m_w_branch_mem, m_w_gate, m_b_gate, m_w_out, m_ffn2_norm, m_ffn2_w1, m_ffn2_w3, m_ffn2_w2, v_ffn1_norm, v_ffn1_w1, v_ffn1_w3, v_ffn1_w2, v_mix_norm, v_mem_norm, v_w_in, v_w_mem_kv, v_qn_dsa, v_kn_dsa, v_qn_mem, v_kn_mem, v_w_branch_sb, v_w_branch_dsa, v_w_branch_mem, v_w_gate, v_b_gate, v_w_out, v_ffn2_norm, v_ffn2_w1, v_ffn2_w3, v_ffn2_w2):
    given = dict(locals())
    wts = {n: given[n][0] for n in WEIGHTS}
    moms = {n: given["m_" + n][0] for n in WEIGHTS}
    vars_ = {n: given["v_" + n][0] for n in WEIGHTS}

    plan = _Plan()
    x_i, y_i, _ = _place()
    my_chip = 2 * x_i + y_i

    full = {}

    def gathered(names):
        return lambda res: full.update({n: _full_from_shards(n, g) for n, g in zip(names, res)})

    for host, names in WEIGHT_PIECES:
        shards = [wts[n].astype(BF16) for n in names]
        if host is None:
            gathered(names)(_all_gather_chips(shards))
        else:
            plan.put(host, _Carry(shards, [SDS((N_CHIPS,) + a.shape, BF16) for a in shards], _chip_sems(len(names)),
                                  _gather_copies, gathered(names)))
    small = {n: wts[n].reshape(1, -1) for n, _ in SMALL}

    landed = {}

    def on_grads(group, grads):
        names = GROUPS[group]
        slices = [_shards_from_full(n, grads[n], BF16) for n in names]
        own = [_own_shard(n, grads[n], my_chip) for n in names]
        plan.put(GRAD_HOSTS[group], _Carry(slices, [SDS((3,) + a.shape[1:], BF16) for a in slices], _chip_sems(len(names)),
                                           _scatter_copies, lambda res: landed.update({group: (own, res)})))

    loss, gx, _, gs = _local_step(x[0], mem[0], loss_target[0], full, small, plan, on_grads)
    assert not plan.pending, list(plan.pending)

    def update(hv, ov, wv, mv, vv):
        g = hv + ov
        return (g,) + _adamw(g, wv, mv, vv)

    outs = [{}, {}, {}, {}]
    for group, names in GROUPS.items():
        own, got = landed[group]
        halves = [_tokmap(f"grads_sum_chips_{n}",
                          lambda a, b0, b1, b2: ((a + b0.astype(F32)) + b1.astype(F32)) + b2.astype(F32),
                          [o, g[0], g[1], g[2]], [], [(o.shape[1], F32)])[0] for n, o, g in zip(names, own, got)]
        others = _swap_with_sibling(f"grads_swap_cores_{group}", halves)
        for n, half, other in zip(names, halves, others):
            res = _tokmap(f"adamw_{n}", update, [half, other, wts[n], moms[n], vars_[n]], [], [(half.shape[1], F32)] * 4)
            for d, r in zip(outs, res):
                d[n] = r

    s_red = _all_reduce_small(_pack_small(gs, loss[0, 0]))
    res = _tokmap(
        "adamw_small", lambda g, wv, mv, vv: (g,) + _adamw(g, wv, mv, vv),
        [s_red, _pack_small(small), _pack_small({n: moms[n] for n, _ in SMALL}), _pack_small({n: vars_[n] for n, _ in SMALL})],
        [], [(D, F32)] * 4)
    for d, packed in zip(outs, res):
        d.update(_unpack_small(packed)[0])
    _, total_loss = _unpack_small(s_red)
    return (total_loss, gx[None], *[d[n][None] for d in outs for n in WEIGHTS])
```
